```python
import math
import jax, jax.numpy as jnp
from jax import lax
import numpy as np

D_MODEL = 1024
BATCH = 16
SEQ = 256
DEPTH = 2
DEC_BATCH = 2
DEC_SEQ = 2048
PAST_LEN = 512

GRID_W = 64
N_EVEN = (DEPTH + 1) // 2
N_ODD = DEPTH // 2
D_FF = 2816
N_MOD = 9
A_HEADS = 4
A_DK = 128
A_DV = 128
A_W = A_HEADS * A_DK
B_HEADS = 4
B_DK = 64
B_DV = 128
B_QK = B_HEADS * B_DK
B_V = B_HEADS * B_DV
GATE_RANK = 16
GLA_TAU = 16.0
CHUNK = 64
AB_SPLIT = (A_W, A_W, A_W, A_W, A_W, B_QK, B_QK, B_V, B_V, GATE_RANK, GATE_RANK)
AB_IN = sum(AB_SPLIT)
AB_OUT = A_W + B_V
C_HEADS = 16
C_KV_HEADS = 4
C_GROUPS = C_HEADS // C_KV_HEADS
C_HEAD_DIM = 64
C_Q = C_HEADS * C_HEAD_DIM
C_KV = C_KV_HEADS * C_HEAD_DIM
C_SPLIT = (C_Q, C_KV, C_KV)
C_IN = C_Q + 2 * C_KV
WINDOW = 128
BLOCK = 128
ROPE_AXIS_DIM = C_HEAD_DIM // 2
ROPE_FREQS = ROPE_AXIS_DIM // 2
ROPE_BASE = 10000.0
ALPHA = (2.0 * DEPTH) ** 0.25
BETA = (8.0 * DEPTH) ** -0.25
LN_EPS = 1e-5
RMS_EPS = 1e-6
NEG_BIG = -1e30

kernel_name = "hybrid_hgrn2_gla_swa_prefix_dit_step"


def split_cols(p, sizes):
    offs = []
    acc = 0
    for s in sizes[:-1]:
        acc += s
        offs.append(acc)
    return jnp.split(p, offs, axis=-1)


def layer_norm(x, g, b):
    xf = x.astype(jnp.float32)
    mu = jnp.mean(xf, axis=-1, keepdims=True)
    var = jnp.mean(jnp.square(xf - mu), axis=-1, keepdims=True)
    return ((xf - mu) * lax.rsqrt(var + LN_EPS) * g.astype(jnp.float32) + b.astype(jnp.float32)).astype(x.dtype)


def rms_norm_heads(o, w):
    of = o.astype(jnp.float32)
    return of * lax.rsqrt(jnp.mean(of * of, axis=-1, keepdims=True) + RMS_EPS) * w.astype(jnp.float32)


def modulate(x, shift, scale):
    return x * (1 + scale[..., None, :]) + shift[..., None, :]


def swiglu(h, w1, w3, w2):
    return (jax.nn.silu(h @ w1) * (h @ w3)) @ w2


def chunk_gla(q, k, v, log_a, s0):
    bsz, t, h, _ = q.shape
    dv = v.shape[-1]
    n = t // CHUNK

    def blocks(z):
        return jnp.moveaxis(z.astype(jnp.float32).reshape(bsz, n, CHUNK, h, z.shape[-1]), 1, 0)

    qc, kc, vc, ac = blocks(q), blocks(k), blocks(v), blocks(log_a)
    causal = jnp.tril(jnp.ones((CHUNK, CHUNK), dtype=bool))

    def step(s, inp):
        qb, kb, vb, ab = inp
        bcum = jnp.cumsum(ab, axis=1)
        blast = bcum[:, -1:]
        q_dec = qb * jnp.exp(bcum)
        k_inv = kb * jnp.exp(-bcum)
        att = jnp.where(causal, jnp.einsum('blhd,bmhd->bhlm', q_dec, k_inv), 0.0)
        o = jnp.einsum('bhlm,bmhe->blhe', att, vb) + jnp.einsum('blhd,bhde->blhe', q_dec, s)
        k_upd = kb * jnp.exp(blast - bcum)
        s_new = jnp.exp(blast[:, 0])[..., None] * s + jnp.einsum('blhd,blhe->bhde', k_upd, vb)
        return s_new, o

    s_fin, o = lax.scan(step, s0.astype(jnp.float32), (qc, kc, vc, ac))
    return jnp.moveaxis(o, 0, 1).reshape(bsz, t, h, dv), s_fin


def bidir_scan(q, k_f, k_b, v, la_f, la_b, s0_f, s0_b):
    o_f, s_f = chunk_gla(q, k_f, v, la_f, s0_f)
    flip = lambda z: jnp.flip(z, axis=1)
    o_b, s_b = chunk_gla(flip(q), flip(k_b), flip(v), flip(la_b), s0_b)
    return o_f + flip(o_b), s_f, s_b


def mixer_ab(h, s0_h, s0_g, lb, w_in, gate_up, gate_b, norm_a, norm_b, w_out):
    bsz, t, _ = h.shape
    p = h @ w_in
    aq, ai, aff, afb, ag, bq, bk, bv, bg, bzf, bzb = split_cols(p, AB_SPLIT)
    heads = lambda z, nh: z.reshape(bsz, t, nh, -1)
    f_f = lb[0] + (1.0 - lb[0]) * jax.nn.sigmoid(aff.astype(jnp.float32))
    f_b = lb[1] + (1.0 - lb[1]) * jax.nn.sigmoid(afb.astype(jnp.float32))
    o_a, sa_f, sa_b = bidir_scan(
        heads(aq, A_HEADS), heads(1.0 - f_f, A_HEADS), heads(1.0 - f_b, A_HEADS),
        heads(jax.nn.silu(ai), A_HEADS), heads(jnp.log(f_f), A_HEADS), heads(jnp.log(f_b), A_HEADS),
        s0_h[:, 0], s0_h[:, 1])
    o_a = rms_norm_heads(o_a, norm_a) * jax.nn.silu(heads(ag, A_HEADS).astype(jnp.float32))
    o_a = o_a.reshape(bsz, t, A_W).astype(h.dtype)
    la_f = jax.nn.log_sigmoid((bzf @ gate_up[0] + gate_b[0]).astype(jnp.float32)) / GLA_TAU
    la_b = jax.nn.log_sigmoid((bzb @ gate_up[1] + gate_b[1]).astype(jnp.float32)) / GLA_TAU
    k_g = heads(bk, B_HEADS)
    o_b, sb_f, sb_b = bidir_scan(
        heads(bq, B_HEADS) * (B_DK ** -0.5), k_g, k_g, heads(bv, B_HEADS),
        heads(la_f, B_HEADS), heads(la_b, B_HEADS), s0_g[:, 0], s0_g[:, 1])
    o_b = rms_norm_heads(o_b, norm_b) * jax.nn.silu(heads(bg, B_HEADS).astype(jnp.float32))
    o_b = o_b.reshape(bsz, t, B_V).astype(h.dtype)
    y = jnp.concatenate([o_a, o_b], axis=-1) @ w_out
    s_h = jnp.stack([sa_f, sa_b], axis=1).astype(h.dtype)
    s_g = jnp.stack([sb_f, sb_b], axis=1).astype(h.dtype)
    return y, s_h, s_g


def rope_2d(x):
    t = x.shape[1]
    rows = t // GRID_W
    row = jnp.repeat(jnp.arange(rows), GRID_W)
    col = jnp.tile(jnp.arange(GRID_W), rows)
    inv = ROPE_BASE ** (-jnp.arange(ROPE_FREQS, dtype=jnp.float32) / ROPE_FREQS)
    bshape = (1, t) + (1,) * (x.ndim - 3) + (ROPE_FREQS,)

    def rot(z, pos):
        ang = (pos.astype(jnp.float32)[:, None] * inv[None, :]).reshape(bshape)
        cos, sin = jnp.cos(ang), jnp.sin(ang)
        z = z.astype(jnp.float32)
        z1, z2 = z[..., :ROPE_FREQS], z[..., ROPE_FREQS:]
        return jnp.concatenate([z1 * cos - z2 * sin, z1 * sin + z2 * cos], axis=-1)

    out = jnp.concatenate([rot(x[..., :ROPE_AXIS_DIM], row), rot(x[..., ROPE_AXIS_DIM:], col)], axis=-1)
    return out.astype(x.dtype)


def attn_ctx(h, w_qkv, sink):
    bsz, t, _ = h.shape
    q, k, v = split_cols(h @ w_qkv, C_SPLIT)
    q = q.reshape(bsz, t, C_KV_HEADS, C_GROUPS, C_HEAD_DIM)
    k = k.reshape(bsz, t, C_KV_HEADS, C_HEAD_DIM)
    v = v.reshape(bsz, t, C_KV_HEADS, C_HEAD_DIM)
    s = jnp.einsum('bqkgd,bckd->bkgqc', q, k).astype(jnp.float32) * (C_HEAD_DIM ** -0.5)
    sk = jnp.broadcast_to(sink.astype(jnp.float32).reshape(1, C_KV_HEADS, C_GROUPS, 1, 1), s.shape[:-1] + (1,))
    p = jax.nn.softmax(jnp.concatenate([s, sk], axis=-1), axis=-1)[..., :-1].astype(v.dtype)
    o = jnp.einsum('bkgqc,bckd->bqkgd', p, v).reshape(bsz, t, C_Q)
    return o, k, v


def attn_latent(h, k_ctx, v_ctx, w_qkv, sink):
    bsz, t, _ = h.shape
    q, k, v = split_cols(h @ w_qkv, C_SPLIT)
    q = rope_2d(q.reshape(bsz, t, C_KV_HEADS, C_GROUPS, C_HEAD_DIM))
    k = rope_2d(k.reshape(bsz, t, C_KV_HEADS, C_HEAD_DIM))
    v = v.reshape(bsz, t, C_KV_HEADS, C_HEAD_DIM)
    nb = t // BLOCK
    qb = q.reshape(bsz, nb, BLOCK, C_KV_HEADS, C_GROUPS, C_HEAD_DIM)
    pad = ((0, 0), (BLOCK, BLOCK), (0, 0), (0, 0))

    def windows(z):
        zp = jnp.pad(z, pad).reshape(bsz, nb + 2, BLOCK, C_KV_HEADS, C_HEAD_DIM)
        return jnp.concatenate([zp[:, :-2], zp[:, 1:-1], zp[:, 2:]], axis=2)

    kwin, vwin = windows(k), windows(v)
    qpos = jnp.arange(nb)[:, None, None] * BLOCK + jnp.arange(BLOCK)[None, :, None]
    kpos = jnp.arange(nb)[:, None, None] * BLOCK - BLOCK + jnp.arange(3 * BLOCK)[None, None, :]
    mask = (jnp.abs(qpos - kpos) <= WINDOW) & (kpos >= 0) & (kpos < t)
    scale = C_HEAD_DIM ** -0.5
    s_loc = jnp.einsum('bnlkgd,bnmkd->bkgnlm', qb, kwin).astype(jnp.float32) * scale
    s_loc = jnp.where(mask, s_loc, NEG_BIG)
    s_ctx = jnp.einsum('bnlkgd,bckd->bkgnlc', qb, k_ctx).astype(jnp.float32) * scale
    sk = jnp.broadcast_to(sink.astype(jnp.float32).reshape(1, C_KV_HEADS, C_GROUPS, 1, 1, 1), s_loc.shape[:-1] + (1,))
    p = jax.nn.softmax(jnp.concatenate([s_loc, s_ctx, sk], axis=-1), axis=-1).astype(v.dtype)
    p_loc = p[..., :3 * BLOCK]
    p_ctx = p[..., 3 * BLOCK:-1]
    o = jnp.einsum('bkgnlm,bnmkd->bnlkgd', p_loc, vwin) + jnp.einsum('bkgnlc,bckd->bnlkgd', p_ctx, v_ctx)
    return o.reshape(bsz, t, C_Q)


def ffn_sublayer(x, shift, scale, gate, w1, w3, w2, g, b):
    y = swiglu(modulate(x, shift, scale), w1, w3, w2)
    return layer_norm(ALPHA * x + 0.5 * gate[..., None, :] * y, g, b)


def setup_inputs(seed: int = 0) -> dict:
    key = jax.random.key(seed)
    ks = jax.random.split(key, 32)
    nrm = lambda k, shape, s: jax.random.normal(k, shape, jnp.float32) * s
    return {
        "x_prompt": nrm(ks[0], (BATCH, SEQ, D_MODEL), 1.0),
        "x_sample": nrm(ks[1], (DEC_BATCH, DEC_SEQ, D_MODEL), 1.0),
        "state_hgrn": nrm(ks[2], (DEC_BATCH, N_EVEN, 2, A_HEADS, A_DK, A_DV), 0.5),
        "state_gla": nrm(ks[3], (DEC_BATCH, N_EVEN, 2, B_HEADS, B_DK, B_DV), 0.5),
        "cache_k": nrm(ks[4], (DEC_BATCH, N_ODD, PAST_LEN, C_KV_HEADS, C_HEAD_DIM), 1.0),
        "cache_v": nrm(ks[5], (DEC_BATCH, N_ODD, PAST_LEN, C_KV_HEADS, C_HEAD_DIM), 1.0),
        "c": nrm(ks[6], (DEC_BATCH, D_MODEL), 1.0),
        "c_ctx": nrm(ks[7], (D_MODEL,), 1.0),
        "w_mod": nrm(ks[8], (DEPTH, D_MODEL, N_MOD * D_MODEL), D_MODEL ** -0.5),
        "b_mod": nrm(ks[9], (DEPTH, N_MOD * D_MODEL), 0.02),
        "ln_g": 1.0 + nrm(ks[10], (DEPTH, 3, D_MODEL), 0.02),
        "ln_b": nrm(ks[11], (DEPTH, 3, D_MODEL), 0.02),
        "ffn_w1": nrm(ks[12], (DEPTH, 2, D_MODEL, D_FF), D_MODEL ** -0.5),
        "ffn_w3": nrm(ks[13], (DEPTH, 2, D_MODEL, D_FF), D_MODEL ** -0.5),
        "ffn_w2": nrm(ks[14], (DEPTH, 2, D_FF, D_MODEL), BETA * D_FF ** -0.5),
        "w_in_ab": nrm(ks[15], (N_EVEN, D_MODEL, AB_IN), D_MODEL ** -0.5),
        "hgrn_lb": nrm(ks[16], (2, N_EVEN + 1, A_W), 0.1),
        "gla_gate_up": nrm(ks[17], (N_EVEN, 2, GATE_RANK, B_QK), GATE_RANK ** -0.5),
        "gla_gate_b": nrm(ks[18], (N_EVEN, 2, B_QK), 0.1),
        "norm_a": 1.0 + nrm(ks[19], (N_EVEN, A_DV), 0.02),
        "norm_b": 1.0 + nrm(ks[20], (N_EVEN, B_DV), 0.02),
        "w_out_ab": nrm(ks[21], (N_EVEN, AB_OUT, D_MODEL), BETA * AB_OUT ** -0.5),
        "w_qkv_c": nrm(ks[22], (N_ODD, D_MODEL, C_IN), D_MODEL ** -0.5),
        "sink_c": nrm(ks[23], (N_ODD, C_HEADS), 0.5),
        "w_out_c": nrm(ks[24], (N_ODD, C_Q, D_MODEL), BETA * C_Q ** -0.5),
    }


def reference(x_prompt, x_sample, state_hgrn, state_gla, cache_k, cache_v, c, c_ctx,
              w_mod, b_mod, ln_g, ln_b, ffn_w1, ffn_w3, ffn_w2,
              w_in_ab, hgrn_lb, gla_gate_up, gla_gate_b, norm_a, norm_b, w_out_ab,
              w_qkv_c, sink_c, w_out_c):
    lb_all = jnp.cumsum(jax.nn.softmax(hgrn_lb.astype(jnp.float32), axis=1), axis=1)
    xp, xs = x_prompt, x_sample
    bp = x_prompt.shape[0]
    new_hgrn, new_gla, new_k, new_v = [], [], [], []
    for layer in range(DEPTH):
        mp = jnp.split(jax.nn.silu(c_ctx) @ w_mod[layer] + b_mod[layer], N_MOD, axis=-1)
        ms = jnp.split(jax.nn.silu(c) @ w_mod[layer] + b_mod[layer], N_MOD, axis=-1)
        xp = ffn_sublayer(xp, mp[0], mp[1], mp[2], ffn_w1[layer, 0], ffn_w3[layer, 0], ffn_w2[layer, 0], ln_g[layer, 0], ln_b[layer, 0])
        xs = ffn_sublayer(xs, ms[0], ms[1], ms[2], ffn_w1[layer, 0], ffn_w3[layer, 0], ffn_w2[layer, 0], ln_g[layer, 0], ln_b[layer, 0])
        hp = modulate(xp, mp[3], mp[4])
        hs = modulate(xs, ms[3], ms[4])
        if layer % 2 == 0:
            e = layer // 2
            lb = lb_all[:, e]
            z_h = jnp.zeros((bp, 2, A_HEADS, A_DK, A_DV), xp.dtype)
            z_g = jnp.zeros((bp, 2, B_HEADS, B_DK, B_DV), xp.dtype)
            yp, s_h, s_g = mixer_ab(hp, z_h, z_g, lb, w_in_ab[e], gla_gate_up[e], gla_gate_b[e], norm_a[e], norm_b[e], w_out_ab[e])
            ys, _, _ = mixer_ab(hs, state_hgrn[:, e], state_gla[:, e], lb, w_in_ab[e], gla_gate_up[e], gla_gate_b[e], norm_a[e], norm_b[e], w_out_ab[e])
            new_hgrn.append(s_h)
            new_gla.append(s_g)
        else:
            o = layer // 2
            op, k_p, v_p = attn_ctx(hp, w_qkv_c[o], sink_c[o])
            os_ = attn_latent(hs, cache_k[:, o], cache_v[:, o], w_qkv_c[o], sink_c[o])
            yp = op @ w_out_c[o]
            ys = os_ @ w_out_c[o]
            new_k.append(k_p)
            new_v.append(v_p)
        xp = layer_norm(ALPHA * xp + mp[5][..., None, :] * yp, ln_g[layer, 1], ln_b[layer, 1])
        xs = layer_norm(ALPHA * xs + ms[5][..., None, :] * ys, ln_g[layer, 1], ln_b[layer, 1])
        xp = ffn_sublayer(xp, mp[6], mp[7], mp[8], ffn_w1[layer, 1], ffn_w3[layer, 1], ffn_w2[layer, 1], ln_g[layer, 2], ln_b[layer, 2])
        xs = ffn_sublayer(xs, ms[6], ms[7], ms[8], ffn_w1[layer, 1], ffn_w3[layer, 1], ffn_w2[layer, 1], ln_g[layer, 2], ln_b[layer, 2])
    return (xp, xs, jnp.stack(new_hgrn, axis=1), jnp.stack(new_gla, axis=1), jnp.stack(new_k, axis=1), jnp.stack(new_v, axis=1))
```

```python
import functools
import math

import jax
import jax.numpy as jnp
import numpy as np
from jax import lax
from jax.experimental import pallas as pl
from jax.experimental.pallas import tpu as pltpu

D = 1024
BATCH, SEQ = 16, 256
DEC_BATCH, DEC_SEQ = 2, 2048
PAST_LEN = 512
GRID_W = 64
D_FF = 2816
N_MOD = 9
A_HEADS, A_DK, A_DV = 4, 128, 128
A_W = A_HEADS * A_DK
B_HEADS, B_DK, B_DV = 4, 64, 128
B_QK = B_HEADS * B_DK
B_V = B_HEADS * B_DV
GATE_RANK = 16
GLA_TAU = 16.0
CHUNK = 64
C_HEADS, C_KV_HEADS, C_HEAD_DIM = 16, 4, 64
C_GROUPS = C_HEADS // C_KV_HEADS
C_Q = C_HEADS * C_HEAD_DIM
C_KV = C_KV_HEADS * C_HEAD_DIM
WINDOW = 128
ROPE_FREQS = C_HEAD_DIM // 4
ROPE_BASE = 10000.0
DEPTH = 2
ALPHA = (2.0 * DEPTH) ** 0.25
LN_EPS = 1e-5
RMS_EPS = 1e-6

TP = BATCH * SEQ
TS = DEC_BATCH * DEC_SEQ
T = TP + TS
N_SEG = 1 + DEC_BATCH

LANES = 128
HALF = LANES // 2
VMEM_LIMIT = 56 * 1024 * 1024

F32 = jnp.float32
BF16 = jnp.bfloat16


def _dot(a, b):
    return jnp.dot(a, b, preferred_element_type=F32)


def _dot_nt(a, b):
    return lax.dot_general(a, b, (((1,), (1,)), ((), ())), preferred_element_type=F32)


def _dot_tn(a, b):
    return lax.dot_general(a, b, (((0,), (0,)), ((), ())), preferred_element_type=F32)


def _silu(x):
    return x * jax.nn.sigmoid(x)


def _layer_norm(z, g, b):
    mu = jnp.mean(z, axis=-1, keepdims=True)
    zc = z - mu
    var = jnp.mean(zc * zc, axis=-1, keepdims=True)
    return zc * lax.rsqrt(var + LN_EPS) * g + b


def _seg_of_tile(i, tm):
    n_p = TP // tm
    n_s = DEC_SEQ // tm
    return jnp.where(i < n_p, 0, 1 + lax.div(jnp.maximum(i - n_p, 0), n_s))


def _params(n_axes):
    return pltpu.CompilerParams(dimension_semantics=("arbitrary",) * n_axes, vmem_limit_bytes=VMEM_LIMIT)


def _resident(shape):
    nd = len(shape)
    return pl.BlockSpec(shape, lambda *_: (0,) * nd, pipeline_mode=pl.Buffered(1))


def _mod_body(c_ref, w_ref, b_ref, o_ref):
    c = c_ref[...]
    s = _silu(c).astype(BF16)
    o_ref[0] = _dot(s, w_ref[0].astype(BF16)) + b_ref[0]


def _mod_vectors(cs, w_mod, b_mod):
    tn = 1536
    n = N_MOD * D
    return pl.pallas_call(
        _mod_body,
        grid=(DEPTH, n // tn),
        in_specs=[
            pl.BlockSpec((8, D), lambda l, j: (0, 0)),
            pl.BlockSpec((1, D, tn), lambda l, j: (l, 0, j)),
            pl.BlockSpec((1, 1, tn), lambda l, j: (l, 0, j)),
        ],
        out_specs=pl.BlockSpec((1, 8, tn), lambda l, j: (l, 0, j)),
        out_shape=jax.ShapeDtypeStruct((DEPTH, 8, n), F32),
        compiler_params=_params(2),
        name="mod_vectors",
    )(cs, w_mod, b_mod.reshape(DEPTH, 1, n))


def _ffn_body(x_ref, m_ref, w1_ref, w3_ref, w2_ref, g_ref, b_ref, o_ref):
    x = x_ref[...]
    m = m_ref[0]
    shift, scale, gate = m[:, :D], m[:, D:2 * D], m[:, 2 * D:]
    h = (x * (1.0 + scale) + shift).astype(BF16)
    a = _dot(h, w1_ref[...])
    b = _dot(h, w3_ref[...])
    g = (_silu(a) * b).astype(BF16)
    y = _dot(g, w2_ref[...])
    z = ALPHA * x + (0.5 * gate) * y
    o_ref[...] = _layer_norm(z, g_ref[...], b_ref[...])


def _ffn_sublayer(x, mod3, w1, w3, w2, ln_g, ln_b, tm=256):
    return pl.pallas_call(
        _ffn_body,
        grid=(T // tm,),
        in_specs=[
            pl.BlockSpec((tm, D), lambda i: (i, 0)),
            pl.BlockSpec((1, 1, 3 * D), lambda i: (_seg_of_tile(i, tm), 0, 0)),
            _resident((D, D_FF)),
            _resident((D, D_FF)),
            _resident((D_FF, D)),
            _resident((1, D)),
            _resident((1, D)),
        ],
        out_specs=pl.BlockSpec((tm, D), lambda i: (i, 0)),
        out_shape=jax.ShapeDtypeStruct((T, D), F32),
        compiler_params=_params(1),
        name="ffn_sublayer",
    )(x, mod3, w1, w3, w2, ln_g.reshape(1, D), ln_b.reshape(1, D))


PF_AQ, PF_FF, PF_FB, PF_AG = 0, 512, 1024, 1536
PF_BQ, PF_BK, PF_BG, PF_LAF, PF_LAB = 2048, 2304, 2560, 3072, 3328
PF_W = 3584
PB_AV, PB_BV = 0, 512
PB_W = 1024
WM_AQ, WM_FF, WM_FB, WM_AG, WM_BQ, WM_BK, WM_BG, WM_AI, WM_BV = 0, 512, 1024, 1536, 2048, 2304, 2560, 3072, 3584
WM_W = 4096


def _log_sigmoid(x):
    return jnp.minimum(x, 0.0) - jnp.log(1.0 + jnp.exp(-jnp.abs(x)))


def _inproj_body(x_ref, m_ref, w_ref, wz_ref, gu_ref, gb_ref, lb_ref, pf_ref, pb_ref, *, layer_e):
    x = x_ref[...]
    m = m_ref[0]
    shift, scale = m[:, :D], m[:, D:]
    h = (x * (1.0 + scale) + shift).astype(BF16)

    def proj(off, width):
        return _dot(h, w_ref[:, off:off + width])

    def lower_bound(d):
        l = lb_ref[d]
        e = jnp.exp(l - jnp.max(l, axis=0, keepdims=True))
        sm = e / jnp.sum(e, axis=0, keepdims=True)
        return jnp.sum(sm[:layer_e + 1], axis=0, keepdims=True)

    pf_ref[:, PF_AQ:PF_AQ + A_W] = proj(WM_AQ, A_W)
    for d, (wm, pf) in enumerate(((WM_FF, PF_FF), (WM_FB, PF_FB))):
        lb = lower_bound(d)
        pf_ref[:, pf:pf + A_W] = lb + (1.0 - lb) * jax.nn.sigmoid(proj(wm, A_W))
    pf_ref[:, PF_AG:PF_AG + A_W] = _silu(proj(WM_AG, A_W))
    pf_ref[:, PF_BQ:PF_BQ + B_QK] = proj(WM_BQ, B_QK) * (B_DK ** -0.5)
    pf_ref[:, PF_BK:PF_BK + B_QK] = proj(WM_BK, B_QK)
    pf_ref[:, PF_BG:PF_BG + B_V] = _silu(proj(WM_BG, B_V))
    pb_ref[:, PB_AV:PB_AV + A_W] = _silu(proj(WM_AI, A_W)).astype(BF16)
    pb_ref[:, PB_BV:PB_BV + B_V] = proj(WM_BV, B_V).astype(BF16)
    z = _dot(h, wz_ref[...]).astype(BF16)
    pre = _dot(z, gu_ref[...]) + gb_ref[...]
    pf_ref[:, PF_LAF:PF_LAF + 2 * B_QK] = _log_sigmoid(pre) * (1.0 / GLA_TAU)


def _inproj(x, mod2, wmain, wz, gup, gb, hgrn_lb, layer_e, tm=256):
    n_l = hgrn_lb.shape[1]
    return pl.pallas_call(
        functools.partial(_inproj_body, layer_e=layer_e),
        grid=(T // tm,),
        in_specs=[
            pl.BlockSpec((tm, D), lambda i: (i, 0)),
            pl.BlockSpec((1, 1, 2 * D), lambda i: (_seg_of_tile(i, tm), 0, 0)),
            _resident((D, WM_W)),
            _resident((D, LANES)),
            _resident((LANES, 2 * B_QK)),
            _resident((1, 2 * B_QK)),
            _resident((2, n_l, A_W)),
        ],
        out_specs=[
            pl.BlockSpec((tm, PF_W), lambda i: (i, 0)),
            pl.BlockSpec((tm, PB_W), lambda i: (i, 0)),
        ],
        out_shape=[jax.ShapeDtypeStruct((T, PF_W), F32), jax.ShapeDtypeStruct((T, PB_W), BF16)],
        compiler_params=_params(1),
        name="ab_inproj",
    )(x, mod2, wmain, wz, gup, gb, hgrn_lb)


def _scan_body(*refs, seq_len, pair, has_s0, emit_state):
    n = seq_len // CHUNK
    nh = 2 if pair else 1
    it = iter(refs)
    q_ref = next(it)
    if pair:
        k_ref, laf_ref, lab_ref = next(it), next(it), next(it)
    else:
        ff_ref, fb_ref = next(it), next(it)
    g_ref, v_ref, nw_ref = next(it), next(it), next(it)
    s0_ref = next(it) if has_s0 else None
    o_ref = next(it)
    st_ref = next(it) if emit_state else None
    qf_s, qb_s, oi_s, kvf_s, kvb_s, decf_s, decb_s, sbf_s, sbb_s = it

    row = lax.broadcasted_iota(jnp.int32, (CHUNK, CHUNK), 0)
    col = lax.broadcasted_iota(jnp.int32, (CHUNK, CHUNK), 1)
    tril = row >= col
    triu = row <= col
    ltri = jnp.where(tril, 1.0, 0.0).astype(BF16)
    lane = lax.broadcasted_iota(jnp.int32, (1, LANES), 1)
    lane2 = lax.broadcasted_iota(jnp.int32, (1, 2 * LANES), 1)
    if pair:
        masks = [lane < HALF, lane >= HALF]
        masks2 = [(lane2 % LANES) < HALF, (lane2 % LANES) >= HALF]
    else:
        masks, masks2 = [None], [None]

    def pick(mask, x):
        return x if mask is None else jnp.where(mask, x, jnp.zeros_like(x))

    def rows_of(c):
        return pl.ds(pl.multiple_of(c * CHUNK, CHUNK), CHUNK)

    def loop(body, reverse=False):
        if n <= 8:
            for c in (range(n - 1, -1, -1) if reverse else range(n)):
                body(c)
        else:
            def fbody(i, carry):
                body(n - 1 - i if reverse else i)
                return carry
            lax.fori_loop(0, n, fbody, 0)

    def phase1(c):
        rows = rows_of(c)
        q = q_ref[rows, :]
        if pair:
            k_f = k_b = k_ref[rows, :]
            la_f, la_b = laf_ref[rows, :], lab_ref[rows, :]
        else:
            f_f, f_b = ff_ref[rows, :], fb_ref[rows, :]
            k_f, k_b = 1.0 - f_f, 1.0 - f_b
            la_f, la_b = jnp.log(f_f), jnp.log(f_b)
        la = jnp.concatenate([la_f, la_b], axis=1)
        b0 = la.astype(BF16)
        r1 = la - b0.astype(F32)
        b1 = r1.astype(BF16)
        b2 = (r1 - b1.astype(F32)).astype(BF16)
        p = _dot(ltri, b0) + _dot(ltri, b1) + _dot(ltri, b2)
        cf, pbk = p[:, :LANES], p[:, LANES:]
        tot_f, tot_b = cf[CHUNK - 1:CHUNK, :], pbk[CHUNK - 1:CHUNK, :]
        rb = tot_b - pbk + la_b
        qf = (q * jnp.exp(cf)).astype(BF16)
        qb = (q * jnp.exp(rb)).astype(BF16)
        kfi = (k_f * jnp.exp(-cf)).astype(BF16)
        kbi = (k_b * jnp.exp(-rb)).astype(BF16)
        kfu = (k_f * jnp.exp(tot_f - cf)).astype(BF16)
        kbu = (k_b * jnp.exp(tot_b - rb)).astype(BF16)
        qf_s[rows, :] = qf
        qb_s[rows, :] = qb
        kvf, kvb = None, None
        for hh in range(nh):
            v = v_ref[rows, hh * LANES:(hh + 1) * LANES]
            att = (jnp.where(tril, _dot_nt(pick(masks[hh], qf), kfi), 0.0)
                   + jnp.where(triu, _dot_nt(pick(masks[hh], qb), kbi), 0.0))
            oi_s[rows, hh * LANES:(hh + 1) * LANES] = _dot(att.astype(BF16), v)
            kf_h, kb_h = _dot_tn(v, kfu), _dot_tn(v, kbu)
            kvf = kf_h if kvf is None else jnp.where(masks[0], kvf, kf_h)
            kvb = kb_h if kvb is None else jnp.where(masks[0], kvb, kb_h)
        kvf_s[c] = kvf
        kvb_s[c] = kvb
        decf_s[c] = jnp.exp(tot_f)
        decb_s[c] = jnp.exp(tot_b)

    loop(phase1)

    def recurrence(d, kv_s, dec_s, sb_s, reverse):
        st0 = s0_ref[0, d, 0].T if has_s0 else jnp.zeros((LANES, LANES), F32)
        if n <= 8:
            st = st0
            for c in (range(n - 1, -1, -1) if reverse else range(n)):
                sb_s[c] = st.astype(BF16)
                st = st * dec_s[c] + kv_s[c]
        else:
            def fbody(i, st):
                c = n - 1 - i if reverse else i
                sb_s[c] = st.astype(BF16)
                return st * dec_s[c] + kv_s[c]
            st = lax.fori_loop(0, n, fbody, st0)
        if emit_state:
            st_ref[0, d, 0] = st.T

    recurrence(0, kvf_s, decf_s, sbf_s, False)
    recurrence(1, kvb_s, decb_s, sbb_s, True)

    nw = nw_ref[...]

    def phase2(c):
        rows = rows_of(c)
        qcat = jnp.concatenate([qf_s[rows, :], qb_s[rows, :]], axis=1)
        scat = jnp.concatenate([sbf_s[c], sbb_s[c]], axis=1)
        for hh in range(nh):
            cols = slice(hh * LANES, (hh + 1) * LANES)
            o = oi_s[rows, cols] + _dot_nt(pick(masks2[hh], qcat), scat)
            o = o * lax.rsqrt(jnp.mean(o * o, axis=-1, keepdims=True) + RMS_EPS) * nw
            o_ref[rows, cols] = (o * g_ref[rows, cols]).astype(BF16)

    loop(phase2)


def _scan(pf, pb, norm_w, s0, *, prompt, pair):
    seq_len = SEQ if prompt else DEC_SEQ
    nseq = BATCH if prompt else DEC_BATCH
    row_off = 0 if prompt else TP // DEC_SEQ
    units = B_HEADS // 2 if pair else A_HEADS
    nh = 2 if pair else 1
    n = seq_len // CHUNK
    has_s0 = s0 is not None
    emit_state = prompt

    def colspec(off, width=LANES):
        base = off // width
        return pl.BlockSpec((seq_len, width), lambda s, u: (s + row_off, base + u))

    if pair:
        in_specs = [colspec(PF_BQ), colspec(PF_BK), colspec(PF_LAF), colspec(PF_LAB),
                    colspec(PF_BG, 2 * LANES), colspec(PB_BV, 2 * LANES)]
        args = [pf, pf, pf, pf, pf, pb]
    else:
        in_specs = [colspec(PF_AQ), colspec(PF_FF), colspec(PF_FB), colspec(PF_AG), colspec(PB_AV)]
        args = [pf, pf, pf, pf, pb]
    in_specs.append(pl.BlockSpec((1, LANES), lambda s, u: (0, 0)))
    args.append(norm_w.reshape(1, LANES))
    state_spec = pl.BlockSpec((1, 2, 1, LANES, LANES), lambda s, u: (s, 0, u, 0, 0))
    if has_s0:
        in_specs.append(state_spec)
        args.append(s0)
    out_specs = [pl.BlockSpec((seq_len, nh * LANES), lambda s, u: (s, u))]
    out_shape = [jax.ShapeDtypeStruct((nseq * seq_len, units * nh * LANES), BF16)]
    if emit_state:
        out_specs.append(state_spec)
        out_shape.append(jax.ShapeDtypeStruct((nseq, 2, units, LANES, LANES), F32))
    scratch = [
        pltpu.VMEM((seq_len, LANES), BF16), pltpu.VMEM((seq_len, LANES), BF16),
        pltpu.VMEM((seq_len, nh * LANES), F32),
        pltpu.VMEM((n, LANES, LANES), F32), pltpu.VMEM((n, LANES, LANES), F32),
        pltpu.VMEM((n, 1, LANES), F32), pltpu.VMEM((n, 1, LANES), F32),
        pltpu.VMEM((n, LANES, LANES), BF16), pltpu.VMEM((n, LANES, LANES), BF16),
    ]
    return pl.pallas_call(
        functools.partial(_scan_body, seq_len=seq_len, pair=pair, has_s0=has_s0, emit_state=emit_state),
        grid=(nseq, units),
        in_specs=in_specs,
        out_specs=out_specs,
        out_shape=out_shape,
        scratch_shapes=scratch,
        compiler_params=_params(2),
        name=f"scan_{'p' if prompt else 's'}_{'gla' if pair else 'hgrn'}",
    )(*args)


def _outproj_body(*refs, n_lhs):
    lhs = refs[:n_lhs]
    ws = refs[n_lhs:2 * n_lhs]
    x_ref, m_ref, g_ref, b_ref, o_ref = refs[2 * n_lhs:]
    y = _dot(lhs[0][...], ws[0][...])
    for a_ref, w_ref in zip(lhs[1:], ws[1:]):
        y = y + _dot(a_ref[...], w_ref[...])
    z = ALPHA * x_ref[...] + m_ref[0] * y
    o_ref[...] = _layer_norm(z, g_ref[...], b_ref[...])


def _outproj(lhs, ws, x, gate, ln_g, ln_b, tm=256):
    n_lhs = len(lhs)
    in_specs = [pl.BlockSpec((tm, a.shape[1]), lambda i: (i, 0)) for a in lhs]
    in_specs += [_resident(w.shape) for w in ws]
    in_specs += [
        pl.BlockSpec((tm, D), lambda i: (i, 0)),
        pl.BlockSpec((1, 1, D), lambda i: (_seg_of_tile(i, tm), 0, 0)),
        _resident((1, D)),
        _resident((1, D)),
    ]
    return pl.pallas_call(
        functools.partial(_outproj_body, n_lhs=n_lhs),
        grid=(T // tm,),
        in_specs=in_specs,
        out_specs=pl.BlockSpec((tm, D), lambda i: (i, 0)),
        out_shape=jax.ShapeDtypeStruct((T, D), F32),
        compiler_params=_params(1),
        name="mixer_outproj",
    )(*lhs, *ws, x, gate, ln_g.reshape(1, D), ln_b.reshape(1, D))


def _rope_partner(x):
    lane = lax.broadcasted_iota(jnp.int32, x.shape, 1)
    first_half = (lane % (2 * ROPE_FREQS)) < ROPE_FREQS
    return jnp.where(first_half, pltpu.roll(x, LANES - ROPE_FREQS, axis=1), pltpu.roll(x, ROPE_FREQS, axis=1))


def _qkv_body(x_ref, m_ref, w_ref, cos_ref, sin_ref, q_ref, k_ref, v_ref):
    x = x_ref[...]
    m = m_ref[0]
    shift, scale = m[:, :D], m[:, D:]
    h = (x * (1.0 + scale) + shift).astype(BF16)
    cos, sin = cos_ref[...], sin_ref[...]

    def rope(z):
        return z * cos + _rope_partner(z) * sin

    for j in range(C_Q // LANES):
        z = _dot(h, w_ref[:, j * LANES:(j + 1) * LANES])
        q_ref[:, j * LANES:(j + 1) * LANES] = (rope(z) * (C_HEAD_DIM ** -0.5)).astype(BF16)
    for j in range(C_KV // LANES):
        z = _dot(h, w_ref[:, C_Q + j * LANES:C_Q + (j + 1) * LANES])
        k_ref[:, j * LANES:(j + 1) * LANES] = rope(z)
    v_ref[...] = _dot(h, w_ref[:, C_Q + C_KV:])


def _qkv(x, mod2, w, cos, sin, tm=256):
    return pl.pallas_call(
        _qkv_body,
        grid=(T // tm,),
        in_specs=[
            pl.BlockSpec((tm, D), lambda i: (i, 0)),
            pl.BlockSpec((1, 1, 2 * D), lambda i: (_seg_of_tile(i, tm), 0, 0)),
            _resident((D, C_Q + 2 * C_KV)),
            pl.BlockSpec((tm, LANES), lambda i: (i, 0)),
            pl.BlockSpec((tm, LANES), lambda i: (i, 0)),
        ],
        out_specs=[
            pl.BlockSpec((tm, C_Q), lambda i: (i, 0)),
            pl.BlockSpec((tm, C_KV), lambda i: (i, 0)),
            pl.BlockSpec((tm, C_KV), lambda i: (i, 0)),
        ],
        out_shape=[jax.ShapeDtypeStruct((T, C_Q), BF16), jax.ShapeDtypeStruct((T, C_KV), F32),
                   jax.ShapeDtypeStruct((T, C_KV), F32)],
        compiler_params=_params(1),
        name="c_qkv",
    )(x, mod2, w, cos, sin)


def _rope_tables():
    t = np.arange(DEC_SEQ)
    pos = np.stack([t // GRID_W, t % GRID_W], axis=1).astype(np.float32)
    inv = (ROPE_BASE ** (-np.arange(ROPE_FREQS, dtype=np.float32) / ROPE_FREQS)).astype(np.float32)
    d = np.arange(C_HEAD_DIM)
    axis = d // (2 * ROPE_FREQS)
    ang = pos[:, axis] * inv[d % ROPE_FREQS][None, :]
    sign = np.where((d % (2 * ROPE_FREQS)) < ROPE_FREQS, -1.0, 1.0)[None, :]
    cos_h, sin_h = np.cos(ang), np.sin(ang) * sign
    reps = LANES // C_HEAD_DIM
    cos_s = np.tile(np.tile(cos_h, (1, reps)), (DEC_BATCH, 1))
    sin_s = np.tile(np.tile(sin_h, (1, reps)), (DEC_BATCH, 1))
    cos = np.concatenate([np.ones((TP, LANES)), cos_s], axis=0).astype(np.float32)
    sin = np.concatenate([np.zeros((TP, LANES)), sin_s], axis=0).astype(np.float32)
    return jnp.asarray(cos), jnp.asarray(sin)


def _dup_head(blk, half):
    lane = lax.broadcasted_iota(jnp.int32, blk.shape, 1)
    keep = (lane >= HALF).astype(jnp.int32) == half
    return jnp.where(keep, blk, pltpu.roll(blk, HALF, axis=1))


def _ctx_attn_body(q_ref, k_ref, v_ref, sink_ref, o_ref):
    half = pl.program_id(1) % 2
    k2 = _dup_head(k_ref[...], half).astype(BF16)
    v2 = _dup_head(v_ref[...], half).astype(BF16)
    lane = lax.broadcasted_iota(jnp.int32, (1, LANES), 1)
    lo = lane < HALF
    for j in range(C_GROUPS * C_HEAD_DIM // LANES):
        q = q_ref[:, j * LANES:(j + 1) * LANES]
        outs = []
        for hh in range(2):
            qm = jnp.where(lo if hh == 0 else ~lo, q, jnp.zeros_like(q))
            sink = sink_ref[:, j * LANES + hh * HALF:j * LANES + hh * HALF + 1]
            s = _dot_nt(qm, k2)
            mx = jnp.maximum(jnp.max(s, axis=-1, keepdims=True), sink)
            e = jnp.exp(s - mx)
            den = jnp.sum(e, axis=-1, keepdims=True) + jnp.exp(sink - mx)
            outs.append(_dot(e.astype(BF16), v2) / den)
        o_ref[:, j * LANES:(j + 1) * LANES] = jnp.where(lo, outs[0], outs[1]).astype(BF16)


def _ctx_attn(q, k, v, sink_cols):
    gw = C_GROUPS * C_HEAD_DIM
    return pl.pallas_call(
        _ctx_attn_body,
        grid=(BATCH, C_KV_HEADS),
        in_specs=[
            pl.BlockSpec((SEQ, gw), lambda b, h: (b, h)),
            pl.BlockSpec((SEQ, LANES), lambda b, h: (b, h // 2)),
            pl.BlockSpec((SEQ, LANES), lambda b, h: (b, h // 2)),
            pl.BlockSpec((1, gw), lambda b, h: (0, h)),
        ],
        out_specs=pl.BlockSpec((SEQ, gw), lambda b, h: (b, h)),
        out_shape=jax.ShapeDtypeStruct((TP, C_Q), BF16),
        compiler_params=_params(2),
        name="c_attn_ctx",
    )(q, k, v, sink_cols)


LAT_TQ = 128
LAT_WIN = LAT_TQ + 2 * WINDOW


def _lat_attn_body(q_ref, k_ref, v_ref, ck_ref, cv_ref, sink_ref, o_ref, k2_s, v2_s, ck2_s, cv2_s):
    half = pl.program_id(1) % 2
    j = pl.program_id(2)

    @pl.when(j == 0)
    def _():
        k2_s[...] = _dup_head(k_ref[...], half).astype(BF16)
        v2_s[...] = _dup_head(v_ref[...], half).astype(BF16)
        ck2_s[...] = _dup_head(ck_ref[...], half).astype(BF16)
        cv2_s[...] = _dup_head(cv_ref[...], half).astype(BF16)

    start = pl.multiple_of(jnp.clip(j * LAT_TQ - WINDOW, 0, DEC_SEQ - LAT_WIN), LANES)
    kl = k2_s[pl.ds(start, LAT_WIN), :]
    vl = v2_s[pl.ds(start, LAT_WIN), :]
    kc, vc = ck2_s[...], cv2_s[...]
    qpos = j * LAT_TQ + lax.broadcasted_iota(jnp.int32, (LAT_TQ, LAT_WIN), 0)
    kpos = start + lax.broadcasted_iota(jnp.int32, (LAT_TQ, LAT_WIN), 1)
    band = jnp.abs(qpos - kpos) <= WINDOW
    lane = lax.broadcasted_iota(jnp.int32, (1, LANES), 1)
    lo = lane < HALF
    for jb in range(C_GROUPS * C_HEAD_DIM // LANES):
        q = q_ref[:, jb * LANES:(jb + 1) * LANES]
        outs = []
        for hh in range(2):
            qm = jnp.where(lo if hh == 0 else ~lo, q, jnp.zeros_like(q))
            sink = sink_ref[:, jb * LANES + hh * HALF:jb * LANES + hh * HALF + 1]
            sl = jnp.where(band, _dot_nt(qm, kl), -jnp.inf)
            sc = _dot_nt(qm, kc)
            mx = jnp.maximum(jnp.maximum(jnp.max(sl, axis=-1, keepdims=True),
                                         jnp.max(sc, axis=-1, keepdims=True)), sink)
            el = jnp.exp(sl - mx)
            ec = jnp.exp(sc - mx)
            den = (jnp.sum(el, axis=-1, keepdims=True) + jnp.sum(ec, axis=-1, keepdims=True)
                   + jnp.exp(sink - mx))
            outs.append((_dot(el.astype(BF16), vl) + _dot(ec.astype(BF16), vc)) / den)
        o_ref[:, jb * LANES:(jb + 1) * LANES] = jnp.where(lo, outs[0], outs[1]).astype(BF16)


def _lat_attn(q, k, v, ck, cv, sink_cols):
    gw = C_GROUPS * C_HEAD_DIM
    nq = DEC_SEQ // LAT_TQ
    q_off = TP // LAT_TQ
    kv_off = TP // DEC_SEQ
    return pl.pallas_call(
        _lat_attn_body,
        grid=(DEC_BATCH, C_KV_HEADS, nq),
        in_specs=[
            pl.BlockSpec((LAT_TQ, gw), lambda b, h, j: (q_off + b * nq + j, h)),
            pl.BlockSpec((DEC_SEQ, LANES), lambda b, h, j: (kv_off + b, h // 2)),
            pl.BlockSpec((DEC_SEQ, LANES), lambda b, h, j: (kv_off + b, h // 2)),
            pl.BlockSpec((PAST_LEN, LANES), lambda b, h, j: (b, h // 2)),
            pl.BlockSpec((PAST_LEN, LANES), lambda b, h, j: (b, h // 2)),
            pl.BlockSpec((1, gw), lambda b, h, j: (0, h)),
        ],
        out_specs=pl.BlockSpec((LAT_TQ, gw), lambda b, h, j: (b * nq + j, h)),
        out_shape=jax.ShapeDtypeStruct((TS, C_Q), BF16),
        scratch_shapes=[
            pltpu.VMEM((DEC_SEQ, LANES), BF16), pltpu.VMEM((DEC_SEQ, LANES), BF16),
            pltpu.VMEM((PAST_LEN, LANES), BF16), pltpu.VMEM((PAST_LEN, LANES), BF16),
        ],
        compiler_params=_params(3),
        name="c_attn_latent",
    )(q, k, v, ck, cv, sink_cols)


def _seg_rows(mod_l, lo, hi):
    return mod_l[:N_SEG, lo * D:hi * D].reshape(N_SEG, 1, (hi - lo) * D)


def kernel(x_prompt, x_sample, state_hgrn, state_gla, cache_k, cache_v, c, c_ctx, w_mod, b_mod, ln_g, ln_b,
           ffn_w1, ffn_w3, ffn_w2, w_in_ab, hgrn_lb, gla_gate_up, gla_gate_b, norm_a, norm_b, w_out_ab,
           w_qkv_c, sink_c, w_out_c):
    x = jnp.concatenate([x_prompt.reshape(TP, D), x_sample.reshape(TS, D)], axis=0)
    cs = jnp.zeros((8, D), F32).at[0].set(c_ctx).at[1:1 + DEC_BATCH].set(c)
    mod = _mod_vectors(cs, w_mod, b_mod)
    w1, w3, w2 = ffn_w1.astype(BF16), ffn_w3.astype(BF16), ffn_w2.astype(BF16)

    def ffn(x, layer, sub, mod_lo):
        return _ffn_sublayer(x, _seg_rows(mod[layer], mod_lo, mod_lo + 3), w1[layer, sub], w3[layer, sub],
                             w2[layer, sub], ln_g[layer, 2 * sub], ln_b[layer, 2 * sub])

    x = ffn(x, 0, 0, 0)
    w_in = w_in_ab[0]
    o_aq, o_ai, o_ff, o_fb, o_ag = 0, A_W, 2 * A_W, 3 * A_W, 4 * A_W
    o_bq = 5 * A_W
    o_bk, o_bv = o_bq + B_QK, o_bq + 2 * B_QK
    o_bg = o_bv + B_V
    o_z = o_bg + B_V
    order = [(o_aq, A_W), (o_ff, A_W), (o_fb, A_W), (o_ag, A_W), (o_bq, B_QK), (o_bk, B_QK), (o_bg, B_V),
             (o_ai, A_W), (o_bv, B_V)]
    wmain = jnp.concatenate([w_in[:, o:o + w] for o, w in order], axis=1).astype(BF16)
    wz = jnp.pad(w_in[:, o_z:o_z + 2 * GATE_RANK], ((0, 0), (0, LANES - 2 * GATE_RANK))).astype(BF16)
    gup = jnp.zeros((LANES, 2 * B_QK), F32)
    gup = gup.at[:GATE_RANK, :B_QK].set(gla_gate_up[0, 0]).at[GATE_RANK:2 * GATE_RANK, B_QK:].set(gla_gate_up[0, 1])
    gb = gla_gate_b[0].reshape(1, 2 * B_QK)
    pf, pb = _inproj(x, _seg_rows(mod[0], 3, 5), wmain, wz, gup.astype(BF16), gb, hgrn_lb, 0)

    s0_a = state_hgrn[:, 0]
    s0_b = state_gla[:, 0].reshape(DEC_BATCH, 2, B_HEADS // 2, LANES, B_DV)
    oa_p, st_a = _scan(pf, pb, norm_a[0], None, prompt=True, pair=False)
    ob_p, st_b = _scan(pf, pb, norm_b[0], None, prompt=True, pair=True)
    (oa_s,) = _scan(pf, pb, norm_a[0], s0_a, prompt=False, pair=False)
    (ob_s,) = _scan(pf, pb, norm_b[0], s0_b, prompt=False, pair=True)
    oa = jnp.concatenate([oa_p, oa_s], axis=0)
    ob = jnp.concatenate([ob_p, ob_s], axis=0)
    w_out = w_out_ab[0].astype(BF16)
    x = _outproj([oa, ob], [w_out[:A_W], w_out[A_W:]], x, _seg_rows(mod[0], 5, 6), ln_g[0, 1], ln_b[0, 1])
    x = ffn(x, 0, 1, 6)
    new_hgrn = st_a.reshape(BATCH, 1, 2, A_HEADS, A_DK, A_DV)
    new_gla = st_b.reshape(BATCH, 1, 2, B_HEADS, B_DK, B_DV)

    x = ffn(x, 1, 0, 0)
    cos, sin = _rope_tables()
    q, k, v = _qkv(x, _seg_rows(mod[1], 3, 5), w_qkv_c[0].astype(BF16), cos, sin)
    sink_cols = jnp.repeat(sink_c[0], C_HEAD_DIM).reshape(1, C_Q)
    o_p = _ctx_attn(q, k, v, sink_cols)
    ck = cache_k[:, 0].reshape(DEC_BATCH * PAST_LEN, C_KV)
    cv = cache_v[:, 0].reshape(DEC_BATCH * PAST_LEN, C_KV)
    o_s = _lat_attn(q, k, v, ck, cv, sink_cols)
    o = jnp.concatenate([o_p, o_s], axis=0)
    x = _outproj([o], [w_out_c[0].astype(BF16)], x, _seg_rows(mod[1], 5, 6), ln_g[1, 1], ln_b[1, 1])
    x = ffn(x, 1, 1, 6)
    new_k = k[:TP].reshape(BATCH, 1, SEQ, C_KV_HEADS, C_HEAD_DIM)
    new_v = v[:TP].reshape(BATCH, 1, SEQ, C_KV_HEADS, C_HEAD_DIM)

    return (x[:TP].reshape(BATCH, SEQ, D), x[TP:].reshape(DEC_BATCH, DEC_SEQ, D), new_hgrn, new_gla, new_k, new_v)
```

```python
import functools
import math

import jax
import jax.numpy as jnp
import numpy as np
from jax import lax
from jax.experimental import pallas as pl
from jax.experimental.pallas import tpu as pltpu

D = 1024
BATCH, SEQ = 16, 256
DEC_BATCH, DEC_SEQ = 2, 2048
PAST_LEN = 512
GRID_W = 64
D_FF = 2816
N_MOD = 9
A_HEADS, A_DK, A_DV = 4, 128, 128
A_W = A_HEADS * A_DK
B_HEADS, B_DK, B_DV = 4, 64, 128
B_QK = B_HEADS * B_DK
B_V = B_HEADS * B_DV
GATE_RANK = 16
GLA_TAU = 16.0
CHUNK = 128
C_HEADS, C_KV_HEADS, C_HEAD_DIM = 16, 4, 64
C_GROUPS = C_HEADS // C_KV_HEADS
C_Q = C_HEADS * C_HEAD_DIM
C_KV = C_KV_HEADS * C_HEAD_DIM
WINDOW = 128
ROPE_FREQS = C_HEAD_DIM // 4
ROPE_BASE = 10000.0
DEPTH = 2
ALPHA = (2.0 * DEPTH) ** 0.25
LN_EPS = 1e-5
RMS_EPS = 1e-6

TP = BATCH * SEQ
TS = DEC_BATCH * DEC_SEQ
T = TP + TS
N_SEG = 1 + DEC_BATCH

LANES = 128
HALF = LANES // 2
VMEM_LIMIT = 56 * 1024 * 1024

F32 = jnp.float32
BF16 = jnp.bfloat16


def _dot(a, b):
    return jnp.dot(a, b, preferred_element_type=F32)


def _dot_nt(a, b):
    return lax.dot_general(a, b, (((1,), (1,)), ((), ())), preferred_element_type=F32)


def _dot_tn(a, b):
    return lax.dot_general(a, b, (((0,), (0,)), ((), ())), preferred_element_type=F32)


def _silu(x):
    return x * jax.nn.sigmoid(x)


def _layer_norm(z, g, b):
    mu = jnp.mean(z, axis=-1, keepdims=True)
    zc = z - mu
    var = jnp.mean(zc * zc, axis=-1, keepdims=True)
    return zc * lax.rsqrt(var + LN_EPS) * g + b


def _seg_of_tile(i, tm):
    n_p = TP // tm
    n_s = DEC_SEQ // tm
    return jnp.where(i < n_p, 0, 1 + lax.div(jnp.maximum(i - n_p, 0), n_s))


def _params(n_axes):
    return pltpu.CompilerParams(dimension_semantics=("arbitrary",) * n_axes, vmem_limit_bytes=VMEM_LIMIT)


def _resident(shape):
    nd = len(shape)
    return pl.BlockSpec(shape, lambda *_: (0,) * nd, pipeline_mode=pl.Buffered(1))


def _mod_body(c_ref, w_ref, b_ref, o_ref):
    c = c_ref[...]
    s = _silu(c).astype(BF16)
    o_ref[0] = _dot(s, w_ref[0].astype(BF16)) + b_ref[0]


def _mod_vectors(cs, w_mod, b_mod):
    tn = 1536
    n = N_MOD * D
    return pl.pallas_call(
        _mod_body,
        grid=(DEPTH, n // tn),
        in_specs=[
            pl.BlockSpec((8, D), lambda l, j: (0, 0)),
            pl.BlockSpec((1, D, tn), lambda l, j: (l, 0, j)),
            pl.BlockSpec((1, 1, tn), lambda l, j: (l, 0, j)),
        ],
        out_specs=pl.BlockSpec((1, 8, tn), lambda l, j: (l, 0, j)),
        out_shape=jax.ShapeDtypeStruct((DEPTH, 8, n), F32),
        compiler_params=_params(2),
        name="mod_vectors",
    )(cs, w_mod, b_mod.reshape(DEPTH, 1, n))


def _ffn_body(x_ref, m_ref, w1_ref, w3_ref, w2_ref, g_ref, b_ref, o_ref):
    x = x_ref[...]
    m = m_ref[0]
    shift, scale, gate = m[:, :D], m[:, D:2 * D], m[:, 2 * D:]
    h = (x * (1.0 + scale) + shift).astype(BF16)
    a = _dot(h, w1_ref[...])
    b = _dot(h, w3_ref[...])
    g = (_silu(a) * b).astype(BF16)
    y = _dot(g, w2_ref[...])
    z = ALPHA * x + (0.5 * gate) * y
    o_ref[...] = _layer_norm(z, g_ref[...], b_ref[...])


def _ffn_sublayer(x, mod3, w1, w3, w2, ln_g, ln_b, tm=256):
    return pl.pallas_call(
        _ffn_body,
        grid=(T // tm,),
        in_specs=[
            pl.BlockSpec((tm, D), lambda i: (i, 0)),
            pl.BlockSpec((1, 1, 3 * D), lambda i: (_seg_of_tile(i, tm), 0, 0)),
            _resident((D, D_FF)),
            _resident((D, D_FF)),
            _resident((D_FF, D)),
            _resident((1, D)),
            _resident((1, D)),
        ],
        out_specs=pl.BlockSpec((tm, D), lambda i: (i, 0)),
        out_shape=jax.ShapeDtypeStruct((T, D), F32),
        compiler_params=_params(1),
        name="ffn_sublayer",
    )(x, mod3, w1, w3, w2, ln_g.reshape(1, D), ln_b.reshape(1, D))


PF_AQ, PF_FF, PF_FB, PF_AG = 0, 512, 1024, 1536
PF_BQ, PF_BK, PF_BG, PF_LAF, PF_LAB = 2048, 2304, 2560, 3072, 3328
PF_W = 3584
PB_AV, PB_BV = 0, 512
PB_W = 1024
WM_AQ, WM_FF, WM_FB, WM_AG, WM_BQ, WM_BK, WM_BG, WM_AI, WM_BV = 0, 512, 1024, 1536, 2048, 2304, 2560, 3072, 3584
WM_W = 4096


def _log_sigmoid(x):
    return jnp.minimum(x, 0.0) - jnp.log(1.0 + jnp.exp(-jnp.abs(x)))


def _inproj_body(x_ref, m_ref, w_ref, wz_ref, gu_ref, gb_ref, lb_ref, pf_ref, pb_ref, *, layer_e):
    x = x_ref[...]
    m = m_ref[0]
    shift, scale = m[:, :D], m[:, D:]
    h = (x * (1.0 + scale) + shift).astype(BF16)

    def proj(off, width):
        return _dot(h, w_ref[:, off:off + width])

    def lower_bound(d):
        l = lb_ref[d]
        e = jnp.exp(l - jnp.max(l, axis=0, keepdims=True))
        sm = e / jnp.sum(e, axis=0, keepdims=True)
        return jnp.sum(sm[:layer_e + 1], axis=0, keepdims=True)

    pf_ref[:, PF_AQ:PF_AQ + A_W] = proj(WM_AQ, A_W)
    for d, (wm, pf) in enumerate(((WM_FF, PF_FF), (WM_FB, PF_FB))):
        lb = lower_bound(d)
        pf_ref[:, pf:pf + A_W] = lb + (1.0 - lb) * jax.nn.sigmoid(proj(wm, A_W))
    pf_ref[:, PF_AG:PF_AG + A_W] = _silu(proj(WM_AG, A_W))
    pf_ref[:, PF_BQ:PF_BQ + B_QK] = proj(WM_BQ, B_QK) * (B_DK ** -0.5)
    pf_ref[:, PF_BK:PF_BK + B_QK] = proj(WM_BK, B_QK)
    pf_ref[:, PF_BG:PF_BG + B_V] = _silu(proj(WM_BG, B_V))
    pb_ref[:, PB_AV:PB_AV + A_W] = _silu(proj(WM_AI, A_W)).astype(BF16)
    pb_ref[:, PB_BV:PB_BV + B_V] = proj(WM_BV, B_V).astype(BF16)
    z = _dot(h, wz_ref[...]).astype(BF16)
    pre = _dot(z, gu_ref[...]) + gb_ref[...]
    pf_ref[:, PF_LAF:PF_LAF + 2 * B_QK] = _log_sigmoid(pre) * (1.0 / GLA_TAU)


def _inproj(x, mod2, wmain, wz, gup, gb, hgrn_lb, layer_e, tm=256):
    n_l = hgrn_lb.shape[1]
    return pl.pallas_call(
        functools.partial(_inproj_body, layer_e=layer_e),
        grid=(T // tm,),
        in_specs=[
            pl.BlockSpec((tm, D), lambda i: (i, 0)),
            pl.BlockSpec((1, 1, 2 * D), lambda i: (_seg_of_tile(i, tm), 0, 0)),
            _resident((D, WM_W)),
            _resident((D, LANES)),
            _resident((LANES, 2 * B_QK)),
            _resident((1, 2 * B_QK)),
            _resident((2, n_l, A_W)),
        ],
        out_specs=[
            pl.BlockSpec((tm, PF_W), lambda i: (i, 0)),
            pl.BlockSpec((tm, PB_W), lambda i: (i, 0)),
        ],
        out_shape=[jax.ShapeDtypeStruct((T, PF_W), F32), jax.ShapeDtypeStruct((T, PB_W), BF16)],
        compiler_params=_params(1),
        name="ab_inproj",
    )(x, mod2, wmain, wz, gup, gb, hgrn_lb)


SCAN_UNROLL = 2


def _prefix_rows(x):
    row = lax.broadcasted_iota(jnp.int32, x.shape, 0)
    s = 1
    while s < x.shape[0]:
        x = x + jnp.where(row >= s, pltpu.roll(x, s, axis=0), 0.0)
        s *= 2
    return x


def _scan_body(*refs, seq_len, pair, has_s0, emit_state):
    n = seq_len // CHUNK
    nh = 2 if pair else 1
    it = iter(refs)
    q_ref = next(it)
    if pair:
        k_ref, laf_ref, lab_ref = next(it), next(it), next(it)
    else:
        ff_ref, fb_ref = next(it), next(it)
    g_ref, v_ref, nw_ref = next(it), next(it), next(it)
    s0_ref = next(it) if has_s0 else None
    o_ref = next(it)
    st_ref = next(it) if emit_state else None
    qd_s, oi_s, kv_s, dec_s, sb_s = it

    row = lax.broadcasted_iota(jnp.int32, (CHUNK, CHUNK), 0)
    col = lax.broadcasted_iota(jnp.int32, (CHUNK, CHUNK), 1)
    tril = row >= col
    triu = row <= col
    lane = lax.broadcasted_iota(jnp.int32, (1, LANES), 1)
    lane2 = lax.broadcasted_iota(jnp.int32, (1, 2 * LANES), 1)
    if pair:
        masks = [lane < HALF, lane >= HALF]
        masks2 = [(lane2 % LANES) < HALF, (lane2 % LANES) >= HALF]
    else:
        masks, masks2 = [None], [None]

    def pick(mask, x):
        return x if mask is None else jnp.where(mask, x, jnp.zeros_like(x))

    def rows_of(c):
        return pl.ds(pl.multiple_of(c * CHUNK, CHUNK), CHUNK)

    def loop(body):
        if n <= SCAN_UNROLL:
            for c in range(n):
                body(c)
        else:
            def fbody(i, carry):
                for u in range(SCAN_UNROLL):
                    body(i * SCAN_UNROLL + u)
                return carry
            lax.fori_loop(0, n // SCAN_UNROLL, fbody, 0)

    def phase1(c):
        rows = rows_of(c)
        q = q_ref[rows, :]
        if pair:
            k_f = k_b = k_ref[rows, :]
            la_f, la_b = laf_ref[rows, :], lab_ref[rows, :]
        else:
            f_f, f_b = ff_ref[rows, :], fb_ref[rows, :]
            k_f, k_b = 1.0 - f_f, 1.0 - f_b
            la_f, la_b = jnp.log(f_f), jnp.log(f_b)
        cs = _prefix_rows(jnp.concatenate([la_f, la_b], axis=1))
        cf, cbi = cs[:, :LANES], cs[:, LANES:]
        tot_f, tot_b = cf[CHUNK - 1:CHUNK, :], cbi[CHUNK - 1:CHUNK, :]
        rb = tot_b - cbi + la_b
        ref_f, ref_b = cf[CHUNK // 2 - 1:CHUNK // 2, :], rb[CHUNK // 2:CHUNK // 2 + 1, :]
        qtf = q * jnp.exp(cf - ref_f)
        qtb = q * jnp.exp(rb - ref_b)
        ktf = k_f * jnp.exp(ref_f - cf)
        ktb = k_b * jnp.exp(ref_b - rb)
        qd = jnp.concatenate([qtf * jnp.exp(ref_f), qtb * jnp.exp(ref_b)], axis=1).astype(BF16)
        ku = jnp.concatenate([ktf * jnp.exp(tot_f - ref_f), ktb * jnp.exp(tot_b - ref_b)], axis=1).astype(BF16)
        qd_s[rows, :] = qd
        qt = jnp.concatenate([qtf, qtb], axis=0).astype(BF16)
        kt = jnp.concatenate([ktf, ktb], axis=0).astype(BF16)
        kv = None
        for hh in range(nh):
            v = v_ref[rows, hh * LANES:(hh + 1) * LANES]
            sc = _dot_nt(pick(masks[hh], qt), kt)
            att = jnp.where(tril, sc[:CHUNK, :CHUNK], 0.0) + jnp.where(triu, sc[CHUNK:, CHUNK:], 0.0)
            oi_s[rows, hh * LANES:(hh + 1) * LANES] = _dot(att.astype(BF16), v)
            kv_h = _dot_tn(v, ku)
            kv = kv_h if kv is None else jnp.where(masks2[0], kv, kv_h)
        kv_s[c] = kv
        dec_s[c] = jnp.exp(jnp.concatenate([tot_f, tot_b], axis=1))

    loop(phase1)

    def recurrence(d, reverse):
        cols = slice(d * LANES, (d + 1) * LANES)
        st0 = s0_ref[0, d, 0].T if has_s0 else jnp.zeros((LANES, LANES), F32)

        def step(c, st):
            sb_s[c, :, cols] = st.astype(BF16)
            return st * dec_s[c, :, cols] + kv_s[c, :, cols]

        if n <= 8:
            st = st0
            for c in (range(n - 1, -1, -1) if reverse else range(n)):
                st = step(c, st)
        else:
            st = lax.fori_loop(0, n, lambda i, st: step(n - 1 - i if reverse else i, st), st0)
        if emit_state:
            st_ref[0, d, 0] = st.T

    recurrence(0, False)
    recurrence(1, True)

    nw = nw_ref[...]

    def phase2(c):
        rows = rows_of(c)
        qcat = qd_s[rows, :]
        scat = sb_s[c]
        for hh in range(nh):
            cols = slice(hh * LANES, (hh + 1) * LANES)
            o = oi_s[rows, cols] + _dot_nt(pick(masks2[hh], qcat), scat)
            o = o * lax.rsqrt(jnp.mean(o * o, axis=-1, keepdims=True) + RMS_EPS) * nw
            o_ref[rows, cols] = (o * g_ref[rows, cols]).astype(BF16)

    loop(phase2)


def _scan(pf, pb, norm_w, s0, *, prompt, pair):
    seq_len = SEQ if prompt else DEC_SEQ
    nseq = BATCH if prompt else DEC_BATCH
    row_off = 0 if prompt else TP // DEC_SEQ
    units = B_HEADS // 2 if pair else A_HEADS
    nh = 2 if pair else 1
    n = seq_len // CHUNK
    has_s0 = s0 is not None
    emit_state = prompt

    def colspec(off, width=LANES):
        base = off // width
        return pl.BlockSpec((seq_len, width), lambda s, u: (s + row_off, base + u))

    if pair:
        in_specs = [colspec(PF_BQ), colspec(PF_BK), colspec(PF_LAF), colspec(PF_LAB),
                    colspec(PF_BG, 2 * LANES), colspec(PB_BV, 2 * LANES)]
        args = [pf, pf, pf, pf, pf, pb]
    else:
        in_specs = [colspec(PF_AQ), colspec(PF_FF), colspec(PF_FB), colspec(PF_AG), colspec(PB_AV)]
        args = [pf, pf, pf, pf, pb]
    in_specs.append(pl.BlockSpec((1, LANES), lambda s, u: (0, 0)))
    args.append(norm_w.reshape(1, LANES))
    state_spec = pl.BlockSpec((1, 2, 1, LANES, LANES), lambda s, u: (s, 0, u, 0, 0))
    if has_s0:
        in_specs.append(state_spec)
        args.append(s0)
    out_specs = [pl.BlockSpec((seq_len, nh * LANES), lambda s, u: (s, u))]
    out_shape = [jax.ShapeDtypeStruct((nseq * seq_len, units * nh * LANES), BF16)]
    if emit_state:
        out_specs.append(state_spec)
        out_shape.append(jax.ShapeDtypeStruct((nseq, 2, units, LANES, LANES), F32))
    scratch = [
        pltpu.VMEM((seq_len, 2 * LANES), BF16),
        pltpu.VMEM((seq_len, nh * LANES), F32),
        pltpu.VMEM((n, LANES, 2 * LANES), F32),
        pltpu.VMEM((n, 1, 2 * LANES), F32),
        pltpu.VMEM((n, LANES, 2 * LANES), BF16),
    ]
    return pl.pallas_call(
        functools.partial(_scan_body, seq_len=seq_len, pair=pair, has_s0=has_s0, emit_state=emit_state),
        grid=(nseq, units),
        in_specs=in_specs,
        out_specs=out_specs,
        out_shape=out_shape,
        scratch_shapes=scratch,
        compiler_params=_params(2),
        name=f"scan_{'p' if prompt else 's'}_{'gla' if pair else 'hgrn'}",
    )(*args)


def _outproj_body(*refs, n_lhs):
    lhs = refs[:n_lhs]
    ws = refs[n_lhs:2 * n_lhs]
    x_ref, m_ref, g_ref, b_ref, o_ref = refs[2 * n_lhs:]
    y = _dot(lhs[0][...], ws[0][...])
    for a_ref, w_ref in zip(lhs[1:], ws[1:]):
        y = y + _dot(a_ref[...], w_ref[...])
    z = ALPHA * x_ref[...] + m_ref[0] * y
    o_ref[...] = _layer_norm(z, g_ref[...], b_ref[...])


def _outproj(lhs, ws, x, gate, ln_g, ln_b, tm=256):
    n_lhs = len(lhs)
    in_specs = [pl.BlockSpec((tm, a.shape[1]), lambda i: (i, 0)) for a in lhs]
    in_specs += [_resident(w.shape) for w in ws]
    in_specs += [
        pl.BlockSpec((tm, D), lambda i: (i, 0)),
        pl.BlockSpec((1, 1, D), lambda i: (_seg_of_tile(i, tm), 0, 0)),
        _resident((1, D)),
        _resident((1, D)),
    ]
    return pl.pallas_call(
        functools.partial(_outproj_body, n_lhs=n_lhs),
        grid=(T // tm,),
        in_specs=in_specs,
        out_specs=pl.BlockSpec((tm, D), lambda i: (i, 0)),
        out_shape=jax.ShapeDtypeStruct((T, D), F32),
        compiler_params=_params(1),
        name="mixer_outproj",
    )(*lhs, *ws, x, gate, ln_g.reshape(1, D), ln_b.reshape(1, D))


def _rope_partner(x):
    lane = lax.broadcasted_iota(jnp.int32, x.shape, 1)
    first_half = (lane % (2 * ROPE_FREQS)) < ROPE_FREQS
    return jnp.where(first_half, pltpu.roll(x, LANES - ROPE_FREQS, axis=1), pltpu.roll(x, ROPE_FREQS, axis=1))


def _qkv_body(x_ref, m_ref, w_ref, cos_ref, sin_ref, q_ref, k_ref, v_ref):
    x = x_ref[...]
    m = m_ref[0]
    shift, scale = m[:, :D], m[:, D:]
    h = (x * (1.0 + scale) + shift).astype(BF16)
    cos, sin = cos_ref[...], sin_ref[...]

    def rope(z):
        return z * cos + _rope_partner(z) * sin

    for j in range(C_Q // LANES):
        z = _dot(h, w_ref[:, j * LANES:(j + 1) * LANES])
        q_ref[:, j * LANES:(j + 1) * LANES] = (rope(z) * (C_HEAD_DIM ** -0.5)).astype(BF16)
    for j in range(C_KV // LANES):
        z = _dot(h, w_ref[:, C_Q + j * LANES:C_Q + (j + 1) * LANES])
        k_ref[:, j * LANES:(j + 1) * LANES] = rope(z)
    v_ref[...] = _dot(h, w_ref[:, C_Q + C_KV:])


def _qkv(x, mod2, w, cos, sin, tm=256):
    return pl.pallas_call(
        _qkv_body,
        grid=(T // tm,),
        in_specs=[
            pl.BlockSpec((tm, D), lambda i: (i, 0)),
            pl.BlockSpec((1, 1, 2 * D), lambda i: (_seg_of_tile(i, tm), 0, 0)),
            _resident((D, C_Q + 2 * C_KV)),
            pl.BlockSpec((tm, LANES), lambda i: (i, 0)),
            pl.BlockSpec((tm, LANES), lambda i: (i, 0)),
        ],
        out_specs=[
            pl.BlockSpec((tm, C_Q), lambda i: (i, 0)),
            pl.BlockSpec((tm, C_KV), lambda i: (i, 0)),
            pl.BlockSpec((tm, C_KV), lambda i: (i, 0)),
        ],
        out_shape=[jax.ShapeDtypeStruct((T, C_Q), BF16), jax.ShapeDtypeStruct((T, C_KV), F32),
                   jax.ShapeDtypeStruct((T, C_KV), F32)],
        compiler_params=_params(1),
        name="c_qkv",
    )(x, mod2, w, cos, sin)


def _rope_tables():
    t = np.arange(DEC_SEQ)
    pos = np.stack([t // GRID_W, t % GRID_W], axis=1).astype(np.float32)
    inv = (ROPE_BASE ** (-np.arange(ROPE_FREQS, dtype=np.float32) / ROPE_FREQS)).astype(np.float32)
    d = np.arange(C_HEAD_DIM)
    axis = d // (2 * ROPE_FREQS)
    ang = pos[:, axis] * inv[d % ROPE_FREQS][None, :]
    sign = np.where((d % (2 * ROPE_FREQS)) < ROPE_FREQS, -1.0, 1.0)[None, :]
    cos_h, sin_h = np.cos(ang), np.sin(ang) * sign
    reps = LANES // C_HEAD_DIM
    cos_s = np.tile(np.tile(cos_h, (1, reps)), (DEC_BATCH, 1))
    sin_s = np.tile(np.tile(sin_h, (1, reps)), (DEC_BATCH, 1))
    cos = np.concatenate([np.ones((TP, LANES)), cos_s], axis=0).astype(np.float32)
    sin = np.concatenate([np.zeros((TP, LANES)), sin_s], axis=0).astype(np.float32)
    return jnp.asarray(cos), jnp.asarray(sin)


def _dup_head(blk, half):
    lane = lax.broadcasted_iota(jnp.int32, blk.shape, 1)
    keep = (lane >= HALF).astype(jnp.int32) == half
    return jnp.where(keep, blk, pltpu.roll(blk, HALF, axis=1))


def _stack_group(q_ref, sink_ref):
    tq = q_ref.shape[0]
    lane = lax.broadcasted_iota(jnp.int32, (1, LANES), 1)
    qs, sinks = [], []
    for g in range(C_GROUPS):
        blk = q_ref[:, (g // 2) * LANES:(g // 2 + 1) * LANES]
        keep = (lane < HALF) if g % 2 == 0 else (lane >= HALF)
        qs.append(jnp.where(keep, blk, jnp.zeros_like(blk)))
        sinks.append(jnp.broadcast_to(sink_ref[:, g * C_HEAD_DIM:g * C_HEAD_DIM + 1], (tq, 1)))
    return jnp.concatenate(qs, axis=0), jnp.concatenate(sinks, axis=0)


def _unstack_group(o4, o_ref):
    tq = o_ref.shape[0]
    lo = lax.broadcasted_iota(jnp.int32, (1, LANES), 1) < HALF
    for jb in range(C_GROUPS // 2):
        a = o4[(2 * jb) * tq:(2 * jb + 1) * tq]
        b = o4[(2 * jb + 1) * tq:(2 * jb + 2) * tq]
        o_ref[:, jb * LANES:(jb + 1) * LANES] = jnp.where(lo, a, b).astype(BF16)


def _values_and_ones(blk, half):
    v2 = _dup_head(blk, half).astype(BF16)
    return jnp.concatenate([v2, jnp.ones_like(v2)], axis=1)


def _ctx_attn_body(q_ref, k_ref, v_ref, sink_ref, o_ref):
    half = pl.program_id(1) % 2
    k2 = _dup_head(k_ref[...], half).astype(BF16)
    v2 = _values_and_ones(v_ref[...], half)
    q4, sink = _stack_group(q_ref, sink_ref)
    s = _dot_nt(q4, k2)
    mx = jnp.maximum(jnp.max(s, axis=-1, keepdims=True), sink)
    od = _dot(jnp.exp(s - mx).astype(BF16), v2)
    den = od[:, LANES:] + jnp.exp(sink - mx)
    _unstack_group(od[:, :LANES] / den, o_ref)


def _ctx_attn(q, k, v, sink_cols):
    gw = C_GROUPS * C_HEAD_DIM
    return pl.pallas_call(
        _ctx_attn_body,
        grid=(BATCH, C_KV_HEADS),
        in_specs=[
            pl.BlockSpec((SEQ, gw), lambda b, h: (b, h)),
            pl.BlockSpec((SEQ, LANES), lambda b, h: (b, h // 2)),
            pl.BlockSpec((SEQ, LANES), lambda b, h: (b, h // 2)),
            pl.BlockSpec((1, gw), lambda b, h: (0, h)),
        ],
        out_specs=pl.BlockSpec((SEQ, gw), lambda b, h: (b, h)),
        out_shape=jax.ShapeDtypeStruct((TP, C_Q), BF16),
        compiler_params=_params(2),
        name="c_attn_ctx",
    )(q, k, v, sink_cols)


LAT_TQ = 128
LAT_WIN = LAT_TQ + 2 * WINDOW


def _lat_attn_body(q_ref, k_ref, v_ref, ck_ref, cv_ref, sink_ref, o_ref, k2_s, v2_s, ck2_s, cv2_s):
    half = pl.program_id(1) % 2
    j = pl.program_id(2)

    @pl.when(j == 0)
    def _():
        k2_s[...] = _dup_head(k_ref[...], half).astype(BF16)
        v2_s[...] = _values_and_ones(v_ref[...], half)
        ck2_s[...] = _dup_head(ck_ref[...], half).astype(BF16)
        cv2_s[...] = _values_and_ones(cv_ref[...], half)

    start = pl.multiple_of(jnp.clip(j * LAT_TQ - WINDOW, 0, DEC_SEQ - LAT_WIN), LANES)
    kl = k2_s[pl.ds(start, LAT_WIN), :]
    vl = v2_s[pl.ds(start, LAT_WIN), :]
    q4, sink = _stack_group(q_ref, sink_ref)
    rows = C_GROUPS * LAT_TQ
    qpos = j * LAT_TQ + lax.broadcasted_iota(jnp.int32, (rows, LAT_WIN), 0) % LAT_TQ
    kpos = start + lax.broadcasted_iota(jnp.int32, (rows, LAT_WIN), 1)
    sl = jnp.where(jnp.abs(qpos - kpos) <= WINDOW, _dot_nt(q4, kl), -jnp.inf)
    sc = _dot_nt(q4, ck2_s[...])
    mx = jnp.maximum(jnp.maximum(jnp.max(sl, axis=-1, keepdims=True), jnp.max(sc, axis=-1, keepdims=True)), sink)
    el = jnp.exp(sl - mx).astype(BF16)
    ec = jnp.exp(sc - mx).astype(BF16)
    od = _dot(el, vl) + _dot(ec, cv2_s[...])
    den = od[:, LANES:] + jnp.exp(sink - mx)
    _unstack_group(od[:, :LANES] / den, o_ref)


def _lat_attn(q, k, v, ck, cv, sink_cols):
    gw = C_GROUPS * C_HEAD_DIM
    nq = DEC_SEQ // LAT_TQ
    q_off = TP // LAT_TQ
    kv_off = TP // DEC_SEQ
    return pl.pallas_call(
        _lat_attn_body,
        grid=(DEC_BATCH, C_KV_HEADS, nq),
        in_specs=[
            pl.BlockSpec((LAT_TQ, gw), lambda b, h, j: (q_off + b * nq + j, h)),
            pl.BlockSpec((DEC_SEQ, LANES), lambda b, h, j: (kv_off + b, h // 2)),
            pl.BlockSpec((DEC_SEQ, LANES), lambda b, h, j: (kv_off + b, h // 2)),
            pl.BlockSpec((PAST_LEN, LANES), lambda b, h, j: (b, h // 2)),
            pl.BlockSpec((PAST_LEN, LANES), lambda b, h, j: (b, h // 2)),
            pl.BlockSpec((1, gw), lambda b, h, j: (0, h)),
        ],
        out_specs=pl.BlockSpec((LAT_TQ, gw), lambda b, h, j: (b * nq + j, h)),
        out_shape=jax.ShapeDtypeStruct((TS, C_Q), BF16),
        scratch_shapes=[
            pltpu.VMEM((DEC_SEQ, LANES), BF16), pltpu.VMEM((DEC_SEQ, 2 * LANES), BF16),
            pltpu.VMEM((PAST_LEN, LANES), BF16), pltpu.VMEM((PAST_LEN, 2 * LANES), BF16),
        ],
        compiler_params=_params(3),
        name="c_attn_latent",
    )(q, k, v, ck, cv, sink_cols)


def _seg_rows(mod_l, lo, hi):
    return mod_l[:N_SEG, lo * D:hi * D].reshape(N_SEG, 1, (hi - lo) * D)


def kernel(x_prompt, x_sample, state_hgrn, state_gla, cache_k, cache_v, c, c_ctx, w_mod, b_mod, ln_g, ln_b,
           ffn_w1, ffn_w3, ffn_w2, w_in_ab, hgrn_lb, gla_gate_up, gla_gate_b, norm_a, norm_b, w_out_ab,
           w_qkv_c, sink_c, w_out_c):
    x = jnp.concatenate([x_prompt.reshape(TP, D), x_sample.reshape(TS, D)], axis=0)
    cs = jnp.zeros((8, D), F32).at[0].set(c_ctx).at[1:1 + DEC_BATCH].set(c)
    mod = _mod_vectors(cs, w_mod, b_mod)
    w1, w3, w2 = ffn_w1.astype(BF16), ffn_w3.astype(BF16), ffn_w2.astype(BF16)

    def ffn(x, layer, sub, mod_lo):
        return _ffn_sublayer(x, _seg_rows(mod[layer], mod_lo, mod_lo + 3), w1[layer, sub], w3[layer, sub],
                             w2[layer, sub], ln_g[layer, 2 * sub], ln_b[layer, 2 * sub])

    x = ffn(x, 0, 0, 0)
    w_in = w_in_ab[0]
    o_aq, o_ai, o_ff, o_fb, o_ag = 0, A_W, 2 * A_W, 3 * A_W, 4 * A_W
    o_bq = 5 * A_W
    o_bk, o_bv = o_bq + B_QK, o_bq + 2 * B_QK
    o_bg = o_bv + B_V
    o_z = o_bg + B_V
    order = [(o_aq, A_W), (o_ff, A_W), (o_fb, A_W), (o_ag, A_W), (o_bq, B_QK), (o_bk, B_QK), (o_bg, B_V),
             (o_ai, A_W), (o_bv, B_V)]
    wmain = jnp.concatenate([w_in[:, o:o + w] for o, w in order], axis=1).astype(BF16)
    wz = jnp.pad(w_in[:, o_z:o_z + 2 * GATE_RANK], ((0, 0), (0, LANES - 2 * GATE_RANK))).astype(BF16)
    gup = jnp.zeros((LANES, 2 * B_QK), F32)
    gup = gup.at[:GATE_RANK, :B_QK].set(gla_gate_up[0, 0]).at[GATE_RANK:2 * GATE_RANK, B_QK:].set(gla_gate_up[0, 1])
    gb = gla_gate_b[0].reshape(1, 2 * B_QK)
    pf, pb = _inproj(x, _seg_rows(mod[0], 3, 5), wmain, wz, gup.astype(BF16), gb, hgrn_lb, 0)

    s0_a = state_hgrn[:, 0]
    s0_b = state_gla[:, 0].reshape(DEC_BATCH, 2, B_HEADS // 2, LANES, B_DV)
    oa_p, st_a = _scan(pf, pb, norm_a[0], None, prompt=True, pair=False)
    ob_p, st_b = _scan(pf, pb, norm_b[0], None, prompt=True, pair=True)
    (oa_s,) = _scan(pf, pb, norm_a[0], s0_a, prompt=False, pair=False)
    (ob_s,) = _scan(pf, pb, norm_b[0], s0_b, prompt=False, pair=True)
    oa = jnp.concatenate([oa_p, oa_s], axis=0)
    ob = jnp.concatenate([ob_p, ob_s], axis=0)
    w_out = w_out_ab[0].astype(BF16)
    x = _outproj([oa, ob], [w_out[:A_W], w_out[A_W:]], x, _seg_rows(mod[0], 5, 6), ln_g[0, 1], ln_b[0, 1])
    x = ffn(x, 0, 1, 6)
    new_hgrn = st_a.reshape(BATCH, 1, 2, A_HEADS, A_DK, A_DV)
    new_gla = st_b.reshape(BATCH, 1, 2, B_HEADS, B_DK, B_DV)

    x = ffn(x, 1, 0, 0)
    cos, sin = _rope_tables()
    q, k, v = _qkv(x, _seg_rows(mod[1], 3, 5), w_qkv_c[0].astype(BF16), cos, sin)
    sink_cols = jnp.repeat(sink_c[0], C_HEAD_DIM).reshape(1, C_Q)
    o_p = _ctx_attn(q, k, v, sink_cols)
    ck = cache_k[:, 0].reshape(DEC_BATCH * PAST_LEN, C_KV)
    cv = cache_v[:, 0].reshape(DEC_BATCH * PAST_LEN, C_KV)
    o_s = _lat_attn(q, k, v, ck, cv, sink_cols)
    o = jnp.concatenate([o_p, o_s], axis=0)
    x = _outproj([o], [w_out_c[0].astype(BF16)], x, _seg_rows(mod[1], 5, 6), ln_g[1, 1], ln_b[1, 1])
    x = ffn(x, 1, 1, 6)
    new_k = k[:TP].reshape(BATCH, 1, SEQ, C_KV_HEADS, C_HEAD_DIM)
    new_v = v[:TP].reshape(BATCH, 1, SEQ, C_KV_HEADS, C_HEAD_DIM)

    return (x[:TP].reshape(BATCH, SEQ, D), x[TP:].reshape(DEC_BATCH, DEC_SEQ, D), new_hgrn, new_gla, new_k, new_v)
```

```python
import functools
import math

import jax
import jax.numpy as jnp
import numpy as np
from jax import lax
from jax.experimental import pallas as pl
from jax.experimental.pallas import tpu as pltpu

D = 1024
BATCH, SEQ = 16, 256
DEC_BATCH, DEC_SEQ = 2, 2048
PAST_LEN = 512
GRID_W = 64
D_FF = 2816
N_MOD = 9
A_HEADS, A_DK, A_DV = 4, 128, 128
A_W = A_HEADS * A_DK
B_HEADS, B_DK, B_DV = 4, 64, 128
B_QK = B_HEADS * B_DK
B_V = B_HEADS * B_DV
GATE_RANK = 16
GLA_TAU = 16.0
CHUNK = 128
C_HEADS, C_KV_HEADS, C_HEAD_DIM = 16, 4, 64
C_GROUPS = C_HEADS // C_KV_HEADS
C_Q = C_HEADS * C_HEAD_DIM
C_KV = C_KV_HEADS * C_HEAD_DIM
WINDOW = 128
ROPE_FREQS = C_HEAD_DIM // 4
ROPE_BASE = 10000.0
DEPTH = 2
ALPHA = (2.0 * DEPTH) ** 0.25
LN_EPS = 1e-5
RMS_EPS = 1e-6

TP = BATCH * SEQ
TS = DEC_BATCH * DEC_SEQ
T = TP + TS
N_SEG = 1 + DEC_BATCH

LANES = 128
HALF = LANES // 2
VMEM_LIMIT = 56 * 1024 * 1024

F32 = jnp.float32
BF16 = jnp.bfloat16


def _dot(a, b):
    return jnp.dot(a, b, preferred_element_type=F32)


def _dot_nt(a, b):
    return lax.dot_general(a, b, (((1,), (1,)), ((), ())), preferred_element_type=F32)


def _dot_tn(a, b):
    return lax.dot_general(a, b, (((0,), (0,)), ((), ())), preferred_element_type=F32)


def _silu(x):
    return x * jax.nn.sigmoid(x)


def _layer_norm(z, g, b):
    mu = jnp.mean(z, axis=-1, keepdims=True)
    zc = z - mu
    var = jnp.mean(zc * zc, axis=-1, keepdims=True)
    return zc * lax.rsqrt(var + LN_EPS) * g + b


def _seg_of_tile(i, tm):
    n_p = TP // tm
    n_s = DEC_SEQ // tm
    return jnp.where(i < n_p, 0, 1 + lax.div(jnp.maximum(i - n_p, 0), n_s))


def _params(n_axes):
    return pltpu.CompilerParams(dimension_semantics=("arbitrary",) * n_axes, vmem_limit_bytes=VMEM_LIMIT)


def _resident(shape):
    nd = len(shape)
    return pl.BlockSpec(shape, lambda *_: (0,) * nd, pipeline_mode=pl.Buffered(1))


def _mod_body(c_ref, w_ref, b_ref, o_ref):
    c = c_ref[...]
    s = _silu(c).astype(BF16)
    o_ref[0] = _dot(s, w_ref[0].astype(BF16)) + b_ref[0]


def _mod_vectors(cs, w_mod, b_mod):
    tn = 1536
    n = N_MOD * D
    return pl.pallas_call(
        _mod_body,
        grid=(DEPTH, n // tn),
        in_specs=[
            pl.BlockSpec((8, D), lambda l, j: (0, 0)),
            pl.BlockSpec((1, D, tn), lambda l, j: (l, 0, j)),
            pl.BlockSpec((1, 1, tn), lambda l, j: (l, 0, j)),
        ],
        out_specs=pl.BlockSpec((1, 8, tn), lambda l, j: (l, 0, j)),
        out_shape=jax.ShapeDtypeStruct((DEPTH, 8, n), F32),
        compiler_params=_params(2),
        name="mod_vectors",
    )(cs, w_mod, b_mod.reshape(DEPTH, 1, n))


def _ffn_body(*refs, n_x, n_o, tm):
    x_refs = refs[:n_x]
    m_ref, w1_ref, w3_ref, w2_ref, g_ref, b_ref = refs[n_x:n_x + 6]
    o_refs = refs[n_x + 6:]

    def compute(x_ref, o_ref):
        x = x_ref[...]
        m = m_ref[0]
        shift, scale, gate = m[:, :D], m[:, D:2 * D], m[:, 2 * D:]
        h = (x * (1.0 + scale) + shift).astype(BF16)
        a = _dot(h, w1_ref[...])
        b = _dot(h, w3_ref[...])
        g = (_silu(a) * b).astype(BF16)
        y = _dot(g, w2_ref[...])
        z = ALPHA * x + (0.5 * gate) * y
        o_ref[...] = _layer_norm(z, g_ref[...], b_ref[...])

    if n_x == 1 and n_o == 1:
        compute(x_refs[0], o_refs[0])
    else:
        in_prompt = pl.program_id(0) < TP // tm
        pl.when(in_prompt)(lambda: compute(x_refs[0], o_refs[0]))
        pl.when(jnp.logical_not(in_prompt))(lambda: compute(x_refs[-1], o_refs[-1]))


def _group_specs(split, tm):
    if not split:
        return [pl.BlockSpec((tm, D), lambda i: (i, 0))]
    n_p = TP // tm
    return [pl.BlockSpec((tm, D), lambda i: (jnp.minimum(i, n_p - 1), 0)),
            pl.BlockSpec((tm, D), lambda i: (jnp.maximum(i - n_p, 0), 0))]


def _ffn_sublayer(xs, mod3, w1, w3, w2, ln_g, ln_b, split_out=False, tm=256):
    n_x, n_o = len(xs), 2 if split_out else 1
    out_shape = ([jax.ShapeDtypeStruct((TP, D), F32), jax.ShapeDtypeStruct((TS, D), F32)] if split_out
                 else [jax.ShapeDtypeStruct((T, D), F32)])
    return pl.pallas_call(
        functools.partial(_ffn_body, n_x=n_x, n_o=n_o, tm=tm),
        grid=(T // tm,),
        in_specs=_group_specs(n_x == 2, tm) + [
            pl.BlockSpec((1, 1, 3 * D), lambda i: (_seg_of_tile(i, tm), 0, 0)),
            _resident((D, D_FF)),
            _resident((D, D_FF)),
            _resident((D_FF, D)),
            _resident((1, D)),
            _resident((1, D)),
        ],
        out_specs=_group_specs(split_out, tm),
        out_shape=out_shape,
        compiler_params=_params(1),
        name="ffn_sublayer",
    )(*xs, mod3, w1, w3, w2, ln_g.reshape(1, D), ln_b.reshape(1, D))


PF_AQ, PF_FF, PF_FB, PF_AG = 0, 512, 1024, 1536
PF_BQ, PF_BK, PF_BG, PF_LAF, PF_LAB = 2048, 2304, 2560, 3072, 3328
PF_W = 3584
PB_AV, PB_BV = 0, 512
PB_W = 1024
WM_AQ, WM_FF, WM_FB, WM_AG, WM_BQ, WM_BK, WM_BG, WM_AI, WM_BV = 0, 512, 1024, 1536, 2048, 2304, 2560, 3072, 3584
WM_W = 4096


def _log_sigmoid(x):
    return jnp.minimum(x, 0.0) - jnp.log(1.0 + jnp.exp(-jnp.abs(x)))


def _inproj_body(x_ref, m_ref, w_ref, wz_ref, gu_ref, gb_ref, lb_ref, pf_ref, pb_ref, *, layer_e):
    x = x_ref[...]
    m = m_ref[0]
    shift, scale = m[:, :D], m[:, D:]
    h = (x * (1.0 + scale) + shift).astype(BF16)

    def proj(off, width):
        return _dot(h, w_ref[:, off:off + width])

    def lower_bound(d):
        l = lb_ref[d]
        e = jnp.exp(l - jnp.max(l, axis=0, keepdims=True))
        sm = e / jnp.sum(e, axis=0, keepdims=True)
        return jnp.sum(sm[:layer_e + 1], axis=0, keepdims=True)

    pf_ref[:, PF_AQ:PF_AQ + A_W] = proj(WM_AQ, A_W)
    for d, (wm, pf) in enumerate(((WM_FF, PF_FF), (WM_FB, PF_FB))):
        lb = lower_bound(d)
        pf_ref[:, pf:pf + A_W] = lb + (1.0 - lb) * jax.nn.sigmoid(proj(wm, A_W))
    pf_ref[:, PF_AG:PF_AG + A_W] = _silu(proj(WM_AG, A_W))
    pf_ref[:, PF_BQ:PF_BQ + B_QK] = proj(WM_BQ, B_QK) * (B_DK ** -0.5)
    pf_ref[:, PF_BK:PF_BK + B_QK] = proj(WM_BK, B_QK)
    pf_ref[:, PF_BG:PF_BG + B_V] = _silu(proj(WM_BG, B_V))
    pb_ref[:, PB_AV:PB_AV + A_W] = _silu(proj(WM_AI, A_W)).astype(BF16)
    pb_ref[:, PB_BV:PB_BV + B_V] = proj(WM_BV, B_V).astype(BF16)
    z = _dot(h, wz_ref[...]).astype(BF16)
    pre = _dot(z, gu_ref[...]) + gb_ref[...]
    pf_ref[:, PF_LAF:PF_LAF + 2 * B_QK] = _log_sigmoid(pre) * (1.0 / GLA_TAU)


def _inproj(x, mod2, wmain, wz, gup, gb, hgrn_lb, layer_e, tm=256):
    n_l = hgrn_lb.shape[1]
    return pl.pallas_call(
        functools.partial(_inproj_body, layer_e=layer_e),
        grid=(T // tm,),
        in_specs=[
            pl.BlockSpec((tm, D), lambda i: (i, 0)),
            pl.BlockSpec((1, 1, 2 * D), lambda i: (_seg_of_tile(i, tm), 0, 0)),
            _resident((D, WM_W)),
            _resident((D, LANES)),
            _resident((LANES, 2 * B_QK)),
            _resident((1, 2 * B_QK)),
            _resident((2, n_l, A_W)),
        ],
        out_specs=[
            pl.BlockSpec((tm, PF_W), lambda i: (i, 0)),
            pl.BlockSpec((tm, PB_W), lambda i: (i, 0)),
        ],
        out_shape=[jax.ShapeDtypeStruct((T, PF_W), F32), jax.ShapeDtypeStruct((T, PB_W), BF16)],
        compiler_params=_params(1),
        name="ab_inproj",
    )(x, mod2, wmain, wz, gup, gb, hgrn_lb)


SCAN_UNROLL = 2


def _prefix_rows(x):
    row = lax.broadcasted_iota(jnp.int32, x.shape, 0)
    s = 1
    while s < x.shape[0]:
        x = x + jnp.where(row >= s, pltpu.roll(x, s, axis=0), 0.0)
        s *= 2
    return x


def _scan_body(*refs, seq_len, pair, has_s0, has_prev, emit_state):
    n = seq_len // CHUNK
    nh = 2 if pair else 1
    it = iter(refs)
    q_ref = next(it)
    if pair:
        k_ref, laf_ref, lab_ref = next(it), next(it), next(it)
    else:
        ff_ref, fb_ref = next(it), next(it)
    g_ref, v_ref, nw_ref = next(it), next(it), next(it)
    s0_ref = next(it) if has_s0 else None
    if has_prev:
        next(it)
    o_ref = next(it)
    st_ref = next(it) if emit_state else None
    qd_s, oi_s, kv_s, dec_s, sb_s = it

    row = lax.broadcasted_iota(jnp.int32, (CHUNK, CHUNK), 0)
    col = lax.broadcasted_iota(jnp.int32, (CHUNK, CHUNK), 1)
    tril = row >= col
    triu = row <= col
    lane = lax.broadcasted_iota(jnp.int32, (1, LANES), 1)
    lane2 = lax.broadcasted_iota(jnp.int32, (1, 2 * LANES), 1)
    if pair:
        masks = [lane < HALF, lane >= HALF]
        masks2 = [(lane2 % LANES) < HALF, (lane2 % LANES) >= HALF]
    else:
        masks, masks2 = [None], [None]

    def pick(mask, x):
        return x if mask is None else jnp.where(mask, x, jnp.zeros_like(x))

    def rows_of(c):
        return pl.ds(pl.multiple_of(c * CHUNK, CHUNK), CHUNK)

    def loop(body):
        if n <= SCAN_UNROLL:
            for c in range(n):
                body(c)
        else:
            def fbody(i, carry):
                for u in range(SCAN_UNROLL):
                    body(i * SCAN_UNROLL + u)
                return carry
            lax.fori_loop(0, n // SCAN_UNROLL, fbody, 0)

    def phase1(c):
        rows = rows_of(c)
        q = q_ref[rows, :]
        if pair:
            k_f = k_b = k_ref[rows, :]
            la_f, la_b = laf_ref[rows, :], lab_ref[rows, :]
        else:
            f_f, f_b = ff_ref[rows, :], fb_ref[rows, :]
            k_f, k_b = 1.0 - f_f, 1.0 - f_b
            la_f, la_b = jnp.log(f_f), jnp.log(f_b)
        cs = _prefix_rows(jnp.concatenate([la_f, la_b], axis=1))
        cf, cbi = cs[:, :LANES], cs[:, LANES:]
        tot_f, tot_b = cf[CHUNK - 1:CHUNK, :], cbi[CHUNK - 1:CHUNK, :]
        rb = tot_b - cbi + la_b
        ref_f, ref_b = cf[CHUNK // 2 - 1:CHUNK // 2, :], rb[CHUNK // 2:CHUNK // 2 + 1, :]
        qtf = q * jnp.exp(cf - ref_f)
        qtb = q * jnp.exp(rb - ref_b)
        ktf = k_f * jnp.exp(ref_f - cf)
        ktb = k_b * jnp.exp(ref_b - rb)
        qd = jnp.concatenate([qtf * jnp.exp(ref_f), qtb * jnp.exp(ref_b)], axis=1).astype(BF16)
        ku = jnp.concatenate([ktf * jnp.exp(tot_f - ref_f), ktb * jnp.exp(tot_b - ref_b)], axis=1).astype(BF16)
        qd_s[rows, :] = qd
        qt = jnp.concatenate([qtf, qtb], axis=0).astype(BF16)
        kt = jnp.concatenate([ktf, ktb], axis=0).astype(BF16)
        kv = None
        for hh in range(nh):
            v = v_ref[rows, hh * LANES:(hh + 1) * LANES]
            sc = _dot_nt(pick(masks[hh], qt), kt)
            att = jnp.where(tril, sc[:CHUNK, :CHUNK], 0.0) + jnp.where(triu, sc[CHUNK:, CHUNK:], 0.0)
            oi_s[rows, hh * LANES:(hh + 1) * LANES] = _dot(att.astype(BF16), v)
            kv_h = _dot_tn(v, ku)
            kv = kv_h if kv is None else jnp.where(masks2[0], kv, kv_h)
        kv_s[c] = kv
        dec_s[c] = jnp.exp(jnp.concatenate([tot_f, tot_b], axis=1))

    loop(phase1)

    def recurrence(d, reverse):
        cols = slice(d * LANES, (d + 1) * LANES)
        st0 = s0_ref[0, d, 0].T if has_s0 else jnp.zeros((LANES, LANES), F32)

        def step(c, st):
            sb_s[c, :, cols] = st.astype(BF16)
            return st * dec_s[c, :, cols] + kv_s[c, :, cols]

        if n <= 8:
            st = st0
            for c in (range(n - 1, -1, -1) if reverse else range(n)):
                st = step(c, st)
        else:
            st = lax.fori_loop(0, n, lambda i, st: step(n - 1 - i if reverse else i, st), st0)
        if emit_state:
            st_ref[0, d, 0] = st.T

    recurrence(0, False)
    recurrence(1, True)

    nw = nw_ref[...]

    def phase2(c):
        rows = rows_of(c)
        qcat = qd_s[rows, :]
        scat = sb_s[c]
        for hh in range(nh):
            cols = slice(hh * LANES, (hh + 1) * LANES)
            o = oi_s[rows, cols] + _dot_nt(pick(masks2[hh], qcat), scat)
            o = o * lax.rsqrt(jnp.mean(o * o, axis=-1, keepdims=True) + RMS_EPS) * nw
            o_ref[rows, cols] = (o * g_ref[rows, cols]).astype(BF16)

    loop(phase2)


def _scan(pf, pb, norm_w, s0, prev, *, prompt, pair):
    seq_len = SEQ if prompt else DEC_SEQ
    nseq = BATCH if prompt else DEC_BATCH
    row_off = 0 if prompt else TP // DEC_SEQ
    units = B_HEADS // 2 if pair else A_HEADS
    nh = 2 if pair else 1
    n = seq_len // CHUNK
    has_s0 = s0 is not None
    has_prev = prev is not None
    emit_state = prompt

    def colspec(off, width=LANES):
        base = off // width
        return pl.BlockSpec((seq_len, width), lambda s, u: (s + row_off, base + u))

    if pair:
        in_specs = [colspec(PF_BQ), colspec(PF_BK), colspec(PF_LAF), colspec(PF_LAB),
                    colspec(PF_BG, 2 * LANES), colspec(PB_BV, 2 * LANES)]
        args = [pf, pf, pf, pf, pf, pb]
    else:
        in_specs = [colspec(PF_AQ), colspec(PF_FF), colspec(PF_FB), colspec(PF_AG), colspec(PB_AV)]
        args = [pf, pf, pf, pf, pb]
    in_specs.append(pl.BlockSpec((1, LANES), lambda s, u: (0, 0)))
    args.append(norm_w.reshape(1, LANES))
    state_spec = pl.BlockSpec((1, 2, 1, LANES, LANES), lambda s, u: (s, 0, u, 0, 0))
    if has_s0:
        in_specs.append(state_spec)
        args.append(s0)
    aliases = {}
    if has_prev:
        aliases[len(args)] = 0
        in_specs.append(pl.BlockSpec(memory_space=pl.ANY))
        args.append(prev)
    out_specs = [pl.BlockSpec((seq_len, nh * LANES), lambda s, u: (s + row_off, u))]
    out_shape = [jax.ShapeDtypeStruct((T, units * nh * LANES), BF16)]
    if emit_state:
        out_specs.append(state_spec)
        out_shape.append(jax.ShapeDtypeStruct((nseq, 2, units, LANES, LANES), F32))
    scratch = [
        pltpu.VMEM((seq_len, 2 * LANES), BF16),
        pltpu.VMEM((seq_len, nh * LANES), F32),
        pltpu.VMEM((n, LANES, 2 * LANES), F32),
        pltpu.VMEM((n, 1, 2 * LANES), F32),
        pltpu.VMEM((n, LANES, 2 * LANES), BF16),
    ]
    return pl.pallas_call(
        functools.partial(_scan_body, seq_len=seq_len, pair=pair, has_s0=has_s0, has_prev=has_prev,
                          emit_state=emit_state),
        grid=(nseq, units),
        in_specs=in_specs,
        out_specs=out_specs,
        out_shape=out_shape,
        input_output_aliases=aliases,
        scratch_shapes=scratch,
        compiler_params=_params(2),
        name=f"scan_{'p' if prompt else 's'}_{'gla' if pair else 'hgrn'}",
    )(*args)


def _outproj_body(*refs, n_lhs):
    lhs = refs[:n_lhs]
    ws = refs[n_lhs:2 * n_lhs]
    x_ref, m_ref, g_ref, b_ref, o_ref = refs[2 * n_lhs:]
    y = _dot(lhs[0][...], ws[0][...])
    for a_ref, w_ref in zip(lhs[1:], ws[1:]):
        y = y + _dot(a_ref[...], w_ref[...])
    z = ALPHA * x_ref[...] + m_ref[0] * y
    o_ref[...] = _layer_norm(z, g_ref[...], b_ref[...])


def _outproj(lhs, ws, x, gate, ln_g, ln_b, tm=256):
    n_lhs = len(lhs)
    in_specs = [pl.BlockSpec((tm, a.shape[1]), lambda i: (i, 0)) for a in lhs]
    in_specs += [_resident(w.shape) for w in ws]
    in_specs += [
        pl.BlockSpec((tm, D), lambda i: (i, 0)),
        pl.BlockSpec((1, 1, D), lambda i: (_seg_of_tile(i, tm), 0, 0)),
        _resident((1, D)),
        _resident((1, D)),
    ]
    return pl.pallas_call(
        functools.partial(_outproj_body, n_lhs=n_lhs),
        grid=(T // tm,),
        in_specs=in_specs,
        out_specs=pl.BlockSpec((tm, D), lambda i: (i, 0)),
        out_shape=jax.ShapeDtypeStruct((T, D), F32),
        compiler_params=_params(1),
        name="mixer_outproj",
    )(*lhs, *ws, x, gate, ln_g.reshape(1, D), ln_b.reshape(1, D))


def _rope_partner(x):
    lane = lax.broadcasted_iota(jnp.int32, x.shape, 1)
    first_half = (lane % (2 * ROPE_FREQS)) < ROPE_FREQS
    return jnp.where(first_half, pltpu.roll(x, LANES - ROPE_FREQS, axis=1), pltpu.roll(x, ROPE_FREQS, axis=1))


def _qkv_body(x_ref, m_ref, w_ref, cos_ref, sin_ref, q_ref, k_ref, v_ref):
    x = x_ref[...]
    m = m_ref[0]
    shift, scale = m[:, :D], m[:, D:]
    h = (x * (1.0 + scale) + shift).astype(BF16)
    cos, sin = cos_ref[...], sin_ref[...]

    def rope(z):
        return z * cos + _rope_partner(z) * sin

    for j in range(C_Q // LANES):
        z = _dot(h, w_ref[:, j * LANES:(j + 1) * LANES])
        q_ref[:, j * LANES:(j + 1) * LANES] = (rope(z) * (C_HEAD_DIM ** -0.5)).astype(BF16)
    for j in range(C_KV // LANES):
        z = _dot(h, w_ref[:, C_Q + j * LANES:C_Q + (j + 1) * LANES])
        k_ref[:, j * LANES:(j + 1) * LANES] = rope(z)
    v_ref[...] = _dot(h, w_ref[:, C_Q + C_KV:])


def _qkv(x, mod2, w, cos, sin, tm=256):
    return pl.pallas_call(
        _qkv_body,
        grid=(T // tm,),
        in_specs=[
            pl.BlockSpec((tm, D), lambda i: (i, 0)),
            pl.BlockSpec((1, 1, 2 * D), lambda i: (_seg_of_tile(i, tm), 0, 0)),
            _resident((D, C_Q + 2 * C_KV)),
            pl.BlockSpec((tm, LANES), lambda i: (i, 0)),
            pl.BlockSpec((tm, LANES), lambda i: (i, 0)),
        ],
        out_specs=[
            pl.BlockSpec((tm, C_Q), lambda i: (i, 0)),
            pl.BlockSpec((tm, C_KV), lambda i: (i, 0)),
            pl.BlockSpec((tm, C_KV), lambda i: (i, 0)),
        ],
        out_shape=[jax.ShapeDtypeStruct((T, C_Q), BF16), jax.ShapeDtypeStruct((T, C_KV), F32),
                   jax.ShapeDtypeStruct((T, C_KV), F32)],
        compiler_params=_params(1),
        name="c_qkv",
    )(x, mod2, w, cos, sin)


def _rope_tables():
    t = np.arange(DEC_SEQ)
    pos = np.stack([t // GRID_W, t % GRID_W], axis=1).astype(np.float32)
    inv = (ROPE_BASE ** (-np.arange(ROPE_FREQS, dtype=np.float32) / ROPE_FREQS)).astype(np.float32)
    d = np.arange(C_HEAD_DIM)
    axis = d // (2 * ROPE_FREQS)
    ang = pos[:, axis] * inv[d % ROPE_FREQS][None, :]
    sign = np.where((d % (2 * ROPE_FREQS)) < ROPE_FREQS, -1.0, 1.0)[None, :]
    cos_h, sin_h = np.cos(ang), np.sin(ang) * sign
    reps = LANES // C_HEAD_DIM
    cos_s = np.tile(np.tile(cos_h, (1, reps)), (DEC_BATCH, 1))
    sin_s = np.tile(np.tile(sin_h, (1, reps)), (DEC_BATCH, 1))
    cos = np.concatenate([np.ones((TP, LANES)), cos_s], axis=0).astype(np.float32)
    sin = np.concatenate([np.zeros((TP, LANES)), sin_s], axis=0).astype(np.float32)
    return jnp.asarray(cos), jnp.asarray(sin)


def _dup_head(blk, half):
    lane = lax.broadcasted_iota(jnp.int32, blk.shape, 1)
    keep = (lane >= HALF).astype(jnp.int32) == half
    return jnp.where(keep, blk, pltpu.roll(blk, HALF, axis=1))


def _stack_group(q_ref, sink_ref):
    tq = q_ref.shape[0]
    lane = lax.broadcasted_iota(jnp.int32, (1, LANES), 1)
    qs, sinks = [], []
    for g in range(C_GROUPS):
        blk = q_ref[:, (g // 2) * LANES:(g // 2 + 1) * LANES]
        keep = (lane < HALF) if g % 2 == 0 else (lane >= HALF)
        qs.append(jnp.where(keep, blk, jnp.zeros_like(blk)))
        sinks.append(jnp.broadcast_to(sink_ref[:, g * C_HEAD_DIM:g * C_HEAD_DIM + 1], (tq, 1)))
    return jnp.concatenate(qs, axis=0), jnp.concatenate(sinks, axis=0)


def _unstack_group(o4, o_ref):
    tq = o_ref.shape[0]
    lo = lax.broadcasted_iota(jnp.int32, (1, LANES), 1) < HALF
    for jb in range(C_GROUPS // 2):
        a = o4[(2 * jb) * tq:(2 * jb + 1) * tq]
        b = o4[(2 * jb + 1) * tq:(2 * jb + 2) * tq]
        o_ref[:, jb * LANES:(jb + 1) * LANES] = jnp.where(lo, a, b).astype(BF16)


def _values_and_ones(blk, half):
    v2 = _dup_head(blk, half).astype(BF16)
    return jnp.concatenate([v2, jnp.ones_like(v2)], axis=1)


def _ctx_attn_body(q_ref, k_ref, v_ref, sink_ref, o_ref):
    half = pl.program_id(1) % 2
    k2 = _dup_head(k_ref[...], half).astype(BF16)
    v2 = _values_and_ones(v_ref[...], half)
    q4, sink = _stack_group(q_ref, sink_ref)
    s = _dot_nt(q4, k2)
    mx = jnp.maximum(jnp.max(s, axis=-1, keepdims=True), sink)
    od = _dot(jnp.exp(s - mx).astype(BF16), v2)
    den = od[:, LANES:] + jnp.exp(sink - mx)
    _unstack_group(od[:, :LANES] / den, o_ref)


def _ctx_attn(q, k, v, sink_cols):
    gw = C_GROUPS * C_HEAD_DIM
    return pl.pallas_call(
        _ctx_attn_body,
        grid=(BATCH, C_KV_HEADS),
        in_specs=[
            pl.BlockSpec((SEQ, gw), lambda b, h: (b, h)),
            pl.BlockSpec((SEQ, LANES), lambda b, h: (b, h // 2)),
            pl.BlockSpec((SEQ, LANES), lambda b, h: (b, h // 2)),
            pl.BlockSpec((1, gw), lambda b, h: (0, h)),
        ],
        out_specs=pl.BlockSpec((SEQ, gw), lambda b, h: (b, h)),
        out_shape=jax.ShapeDtypeStruct((T, C_Q), BF16),
        compiler_params=_params(2),
        name="c_attn_ctx",
    )(q, k, v, sink_cols)


LAT_TQ = 128
LAT_WIN = LAT_TQ + 2 * WINDOW


def _lat_attn_body(q_ref, k_ref, v_ref, ck_ref, cv_ref, sink_ref, prev_ref, o_ref, k2_s, v2_s, ck2_s, cv2_s):
    del prev_ref
    half = pl.program_id(1) % 2
    j = pl.program_id(2)

    @pl.when(j == 0)
    def _():
        k2_s[...] = _dup_head(k_ref[...], half).astype(BF16)
        v2_s[...] = _values_and_ones(v_ref[...], half)
        ck2_s[...] = _dup_head(ck_ref[...], half).astype(BF16)
        cv2_s[...] = _values_and_ones(cv_ref[...], half)

    start = pl.multiple_of(jnp.clip(j * LAT_TQ - WINDOW, 0, DEC_SEQ - LAT_WIN), LANES)
    kl = k2_s[pl.ds(start, LAT_WIN), :]
    vl = v2_s[pl.ds(start, LAT_WIN), :]
    q4, sink = _stack_group(q_ref, sink_ref)
    rows = C_GROUPS * LAT_TQ
    qpos = j * LAT_TQ + lax.broadcasted_iota(jnp.int32, (rows, LAT_WIN), 0) % LAT_TQ
    kpos = start + lax.broadcasted_iota(jnp.int32, (rows, LAT_WIN), 1)
    sl = jnp.where(jnp.abs(qpos - kpos) <= WINDOW, _dot_nt(q4, kl), -jnp.inf)
    sc = _dot_nt(q4, ck2_s[...])
    mx = jnp.maximum(jnp.maximum(jnp.max(sl, axis=-1, keepdims=True), jnp.max(sc, axis=-1, keepdims=True)), sink)
    el = jnp.exp(sl - mx).astype(BF16)
    ec = jnp.exp(sc - mx).astype(BF16)
    od = _dot(el, vl) + _dot(ec, cv2_s[...])
    den = od[:, LANES:] + jnp.exp(sink - mx)
    _unstack_group(od[:, :LANES] / den, o_ref)


def _lat_attn(q, k, v, ck, cv, sink_cols, o_prompt):
    gw = C_GROUPS * C_HEAD_DIM
    nq = DEC_SEQ // LAT_TQ
    q_off = TP // LAT_TQ
    kv_off = TP // DEC_SEQ
    return pl.pallas_call(
        _lat_attn_body,
        grid=(DEC_BATCH, C_KV_HEADS, nq),
        in_specs=[
            pl.BlockSpec((LAT_TQ, gw), lambda b, h, j: (q_off + b * nq + j, h)),
            pl.BlockSpec((DEC_SEQ, LANES), lambda b, h, j: (kv_off + b, h // 2)),
            pl.BlockSpec((DEC_SEQ, LANES), lambda b, h, j: (kv_off + b, h // 2)),
            pl.BlockSpec((PAST_LEN, LANES), lambda b, h, j: (b, h // 2)),
            pl.BlockSpec((PAST_LEN, LANES), lambda b, h, j: (b, h // 2)),
            pl.BlockSpec((1, gw), lambda b, h, j: (0, h)),
            pl.BlockSpec(memory_space=pl.ANY),
        ],
        out_specs=pl.BlockSpec((LAT_TQ, gw), lambda b, h, j: (q_off + b * nq + j, h)),
        out_shape=jax.ShapeDtypeStruct((T, C_Q), BF16),
        input_output_aliases={6: 0},
        scratch_shapes=[
            pltpu.VMEM((DEC_SEQ, LANES), BF16), pltpu.VMEM((DEC_SEQ, 2 * LANES), BF16),
            pltpu.VMEM((PAST_LEN, LANES), BF16), pltpu.VMEM((PAST_LEN, 2 * LANES), BF16),
        ],
        compiler_params=_params(3),
        name="c_attn_latent",
    )(q, k, v, ck, cv, sink_cols, o_prompt)


def _seg_rows(mod_l, lo, hi):
    return mod_l[:N_SEG, lo * D:hi * D].reshape(N_SEG, 1, (hi - lo) * D)


def kernel(x_prompt, x_sample, state_hgrn, state_gla, cache_k, cache_v, c, c_ctx, w_mod, b_mod, ln_g, ln_b,
           ffn_w1, ffn_w3, ffn_w2, w_in_ab, hgrn_lb, gla_gate_up, gla_gate_b, norm_a, norm_b, w_out_ab,
           w_qkv_c, sink_c, w_out_c):
    cs = jnp.zeros((8, D), F32).at[0].set(c_ctx).at[1:1 + DEC_BATCH].set(c)
    mod = _mod_vectors(cs, w_mod, b_mod)
    w1, w3, w2 = ffn_w1.astype(BF16), ffn_w3.astype(BF16), ffn_w2.astype(BF16)

    def ffn(xs, layer, sub, mod_lo, split_out=False):
        return _ffn_sublayer(xs, _seg_rows(mod[layer], mod_lo, mod_lo + 3), w1[layer, sub], w3[layer, sub],
                             w2[layer, sub], ln_g[layer, 2 * sub], ln_b[layer, 2 * sub], split_out=split_out)

    (x,) = ffn([x_prompt.reshape(TP, D), x_sample.reshape(TS, D)], 0, 0, 0)
    w_in = w_in_ab[0]
    o_aq, o_ai, o_ff, o_fb, o_ag = 0, A_W, 2 * A_W, 3 * A_W, 4 * A_W
    o_bq = 5 * A_W
    o_bk, o_bv = o_bq + B_QK, o_bq + 2 * B_QK
    o_bg = o_bv + B_V
    o_z = o_bg + B_V
    order = [(o_aq, A_W), (o_ff, A_W), (o_fb, A_W), (o_ag, A_W), (o_bq, B_QK), (o_bk, B_QK), (o_bg, B_V),
             (o_ai, A_W), (o_bv, B_V)]
    wmain = jnp.concatenate([w_in[:, o:o + w] for o, w in order], axis=1).astype(BF16)
    wz = jnp.pad(w_in[:, o_z:o_z + 2 * GATE_RANK], ((0, 0), (0, LANES - 2 * GATE_RANK))).astype(BF16)
    gup = jnp.zeros((LANES, 2 * B_QK), F32)
    gup = gup.at[:GATE_RANK, :B_QK].set(gla_gate_up[0, 0]).at[GATE_RANK:2 * GATE_RANK, B_QK:].set(gla_gate_up[0, 1])
    gb = gla_gate_b[0].reshape(1, 2 * B_QK)
    pf, pb = _inproj(x, _seg_rows(mod[0], 3, 5), wmain, wz, gup.astype(BF16), gb, hgrn_lb, 0)

    s0_a = state_hgrn[:, 0]
    s0_b = state_gla[:, 0].reshape(DEC_BATCH, 2, B_HEADS // 2, LANES, B_DV)
    oa, st_a = _scan(pf, pb, norm_a[0], None, None, prompt=True, pair=False)
    ob, st_b = _scan(pf, pb, norm_b[0], None, None, prompt=True, pair=True)
    (oa,) = _scan(pf, pb, norm_a[0], s0_a, oa, prompt=False, pair=False)
    (ob,) = _scan(pf, pb, norm_b[0], s0_b, ob, prompt=False, pair=True)
    w_out = w_out_ab[0].astype(BF16)
    x = _outproj([oa, ob], [w_out[:A_W], w_out[A_W:]], x, _seg_rows(mod[0], 5, 6), ln_g[0, 1], ln_b[0, 1])
    (x,) = ffn([x], 0, 1, 6)
    new_hgrn = st_a.reshape(BATCH, 1, 2, A_HEADS, A_DK, A_DV)
    new_gla = st_b.reshape(BATCH, 1, 2, B_HEADS, B_DK, B_DV)

    (x,) = ffn([x], 1, 0, 0)
    cos, sin = _rope_tables()
    q, k, v = _qkv(x, _seg_rows(mod[1], 3, 5), w_qkv_c[0].astype(BF16), cos, sin)
    sink_cols = jnp.repeat(sink_c[0], C_HEAD_DIM).reshape(1, C_Q)
    o = _ctx_attn(q, k, v, sink_cols)
    ck = cache_k[:, 0].reshape(DEC_BATCH * PAST_LEN, C_KV)
    cv = cache_v[:, 0].reshape(DEC_BATCH * PAST_LEN, C_KV)
    o = _lat_attn(q, k, v, ck, cv, sink_cols, o)
    x = _outproj([o], [w_out_c[0].astype(BF16)], x, _seg_rows(mod[1], 5, 6), ln_g[1, 1], ln_b[1, 1])
    y_p, y_s = ffn([x], 1, 1, 6, split_out=True)
    new_k = k[:TP].reshape(BATCH, 1, SEQ, C_KV_HEADS, C_HEAD_DIM)
    new_v = v[:TP].reshape(BATCH, 1, SEQ, C_KV_HEADS, C_HEAD_DIM)

    return (y_p.reshape(BATCH, SEQ, D), y_s.reshape(DEC_BATCH, DEC_SEQ, D), new_hgrn, new_gla, new_k, new_v)
```

```python
import functools
import math

import jax
import jax.numpy as jnp
import numpy as np
from jax import lax
from jax.experimental import pallas as pl
from jax.experimental.pallas import tpu as pltpu

D = 1024
BATCH, SEQ = 16, 256
DEC_BATCH, DEC_SEQ = 2, 2048
PAST_LEN = 512
GRID_W = 64
D_FF = 2816
N_MOD = 9
A_HEADS, A_DK, A_DV = 4, 128, 128
A_W = A_HEADS * A_DK
B_HEADS, B_DK, B_DV = 4, 64, 128
B_QK = B_HEADS * B_DK
B_V = B_HEADS * B_DV
GATE_RANK = 16
GLA_TAU = 16.0
CHUNK = 128
C_HEADS, C_KV_HEADS, C_HEAD_DIM = 16, 4, 64
C_GROUPS = C_HEADS // C_KV_HEADS
C_Q = C_HEADS * C_HEAD_DIM
C_KV = C_KV_HEADS * C_HEAD_DIM
WINDOW = 128
ROPE_FREQS = C_HEAD_DIM // 4
ROPE_BASE = 10000.0
DEPTH = 2
ALPHA = (2.0 * DEPTH) ** 0.25
LN_EPS = 1e-5
RMS_EPS = 1e-6

TP = BATCH * SEQ
TS = DEC_BATCH * DEC_SEQ
T = TP + TS
N_SEG = 1 + DEC_BATCH

LANES = 128
HALF = LANES // 2
FFN_TM = 512
PROJ_TM = SEQ
OUT_TM = 512
VMEM_LIMIT = 56 * 1024 * 1024

F32 = jnp.float32
BF16 = jnp.bfloat16


def _dot(a, b):
    return jnp.dot(a, b, preferred_element_type=F32)


def _dot_nt(a, b):
    return lax.dot_general(a, b, (((1,), (1,)), ((), ())), preferred_element_type=F32)


def _dot_tn(a, b):
    return lax.dot_general(a, b, (((0,), (0,)), ((), ())), preferred_element_type=F32)


def _silu(x):
    return x * jax.nn.sigmoid(x)


def _layer_norm(z, g, b):
    mu = jnp.mean(z, axis=-1, keepdims=True)
    zc = z - mu
    var = jnp.mean(zc * zc, axis=-1, keepdims=True)
    return zc * lax.rsqrt(var + LN_EPS) * g + b


def _seg_of_tile(i, tm):
    n_p = TP // tm
    n_s = DEC_SEQ // tm
    return jnp.where(i < n_p, 0, 1 + lax.div(jnp.maximum(i - n_p, 0), n_s))


def _params(n_axes):
    return pltpu.CompilerParams(dimension_semantics=("arbitrary",) * n_axes, vmem_limit_bytes=VMEM_LIMIT)


def _resident(shape):
    nd = len(shape)
    return pl.BlockSpec(shape, lambda *_: (0,) * nd, pipeline_mode=pl.Buffered(1))


def _resident_slice(shape, lead):
    block = (None,) * len(lead) + tuple(shape)
    return pl.BlockSpec(block, lambda *_: tuple(lead) + (0,) * len(shape), pipeline_mode=pl.Buffered(1))


def _mod_specs(layer, cols, tm):
    return [pl.BlockSpec((None, None, 1, D), functools.partial(
        lambda i, c: (layer, _seg_of_tile(i, tm), 0, c), c=c)) for c in cols]


def _ln_specs(layer, idx):
    return [_resident_slice((1, D), (layer, idx))] * 2


def _mod_body(c_ref, w_ref, b_ref, o_ref):
    c = c_ref[...]
    s = _silu(c).astype(BF16)
    o_ref[0] = _dot(s, w_ref[0].astype(BF16)) + b_ref[0]


def _mod_vectors(cs, w_mod, b_mod):
    tn = 1536
    n = N_MOD * D
    return pl.pallas_call(
        _mod_body,
        grid=(DEPTH, n // tn),
        in_specs=[
            pl.BlockSpec((8, D), lambda l, j: (0, 0)),
            pl.BlockSpec((1, D, tn), lambda l, j: (l, 0, j)),
            pl.BlockSpec((1, 1, tn), lambda l, j: (l, 0, j)),
        ],
        out_specs=pl.BlockSpec((1, 8, tn), lambda l, j: (l, 0, j)),
        out_shape=jax.ShapeDtypeStruct((DEPTH, 8, n), F32),
        compiler_params=_params(2),
        name="mod_vectors",
    )(cs, w_mod, b_mod.reshape(DEPTH, 1, n))


def _ffn_body(*refs, n_x, n_o, tm):
    x_refs = refs[:n_x]
    shift_ref, scale_ref, gate_ref, w1_ref, w3_ref, w2_ref, g_ref, b_ref = refs[n_x:n_x + 8]
    o_refs = refs[n_x + 8:]

    def compute(x_ref, o_ref):
        x = x_ref[...]
        shift, scale, gate = shift_ref[...], scale_ref[...], gate_ref[...]
        h = (x * (1.0 + scale) + shift).astype(BF16)
        a = _dot(h, w1_ref[...])
        b = _dot(h, w3_ref[...])
        g = (_silu(a) * b).astype(BF16)
        y = _dot(g, w2_ref[...])
        z = ALPHA * x + (0.5 * gate) * y
        o_ref[...] = _layer_norm(z, g_ref[...], b_ref[...])

    if n_x == 1 and n_o == 1:
        compute(x_refs[0], o_refs[0])
    else:
        in_prompt = pl.program_id(0) < TP // tm
        pl.when(in_prompt)(lambda: compute(x_refs[0], o_refs[0]))
        pl.when(jnp.logical_not(in_prompt))(lambda: compute(x_refs[-1], o_refs[-1]))


def _group_specs(split, tm):
    if not split:
        return [pl.BlockSpec((tm, D), lambda i: (i, 0))]
    n_p = TP // tm
    return [pl.BlockSpec((tm, D), lambda i: (jnp.minimum(i, n_p - 1), 0)),
            pl.BlockSpec((tm, D), lambda i: (jnp.maximum(i - n_p, 0), 0))]


def _ffn_sublayer(xs, mod, w1, w3, w2, ln_g, ln_b, layer, sub, split_out=False, tm=FFN_TM):
    n_x, n_o = len(xs), 2 if split_out else 1
    out_shape = ([jax.ShapeDtypeStruct((TP, D), F32), jax.ShapeDtypeStruct((TS, D), F32)] if split_out
                 else [jax.ShapeDtypeStruct((T, D), F32)])
    mod_lo = 6 * sub
    return pl.pallas_call(
        functools.partial(_ffn_body, n_x=n_x, n_o=n_o, tm=tm),
        grid=(T // tm,),
        in_specs=_group_specs(n_x == 2, tm) + _mod_specs(layer, (mod_lo, mod_lo + 1, mod_lo + 2), tm) + [
            _resident_slice((D, D_FF), (layer, sub)),
            _resident_slice((D, D_FF), (layer, sub)),
            _resident_slice((D_FF, D), (layer, sub)),
        ] + _ln_specs(layer, 2 * sub),
        out_specs=_group_specs(split_out, tm),
        out_shape=out_shape,
        compiler_params=_params(1),
        name="ffn_sublayer",
    )(*xs, mod, mod, mod, w1, w3, w2, ln_g, ln_b)


PF_AQ, PF_FF, PF_FB, PF_AG = 0, 512, 1024, 1536
PF_BQ, PF_BK, PF_BG, PF_LAF, PF_LAB = 2048, 2304, 2560, 3072, 3328
PF_W = 3584
PB_AV, PB_BV = 0, 512
PB_W = 1024
WM_AQ, WM_FF, WM_FB, WM_AG, WM_BQ, WM_BK, WM_BG, WM_AI, WM_BV = 0, 512, 1024, 1536, 2048, 2304, 2560, 3072, 3584
WM_W = 4096


def _log_sigmoid(x):
    return jnp.minimum(x, 0.0) - jnp.log(1.0 + jnp.exp(-jnp.abs(x)))


def _inproj_body(x_ref, shift_ref, scale_ref, w_ref, wz_ref, gu_ref, gb_ref, lb_ref, pf_ref, pb_ref, *, layer_e):
    h = (x_ref[...] * (1.0 + scale_ref[...]) + shift_ref[...]).astype(BF16)

    def proj(off, width):
        return _dot(h, w_ref[:, off:off + width])

    def lower_bound(d):
        l = lb_ref[d]
        e = jnp.exp(l - jnp.max(l, axis=0, keepdims=True))
        sm = e / jnp.sum(e, axis=0, keepdims=True)
        return jnp.sum(sm[:layer_e + 1], axis=0, keepdims=True)

    pf_ref[:, PF_AQ:PF_AQ + A_W] = proj(WM_AQ, A_W)
    for d, (wm, pf) in enumerate(((WM_FF, PF_FF), (WM_FB, PF_FB))):
        lb = lower_bound(d)
        pf_ref[:, pf:pf + A_W] = lb + (1.0 - lb) * jax.nn.sigmoid(proj(wm, A_W))
    pf_ref[:, PF_AG:PF_AG + A_W] = _silu(proj(WM_AG, A_W))
    pf_ref[:, PF_BQ:PF_BQ + B_QK] = proj(WM_BQ, B_QK) * (B_DK ** -0.5)
    pf_ref[:, PF_BK:PF_BK + B_QK] = proj(WM_BK, B_QK)
    pf_ref[:, PF_BG:PF_BG + B_V] = _silu(proj(WM_BG, B_V))
    pb_ref[:, PB_AV:PB_AV + A_W] = _silu(proj(WM_AI, A_W)).astype(BF16)
    pb_ref[:, PB_BV:PB_BV + B_V] = proj(WM_BV, B_V).astype(BF16)
    z = _dot(h, wz_ref[...]).astype(BF16)
    pre = _dot(z, gu_ref[...]) + gb_ref[...]
    pf_ref[:, PF_LAF:PF_LAF + 2 * B_QK] = _log_sigmoid(pre) * (1.0 / GLA_TAU)


def _inproj(x, mod, wmain, wz, gup, gb, hgrn_lb, layer, layer_e, tm=PROJ_TM):
    n_l = hgrn_lb.shape[1]
    return pl.pallas_call(
        functools.partial(_inproj_body, layer_e=layer_e),
        grid=(T // tm,),
        in_specs=[pl.BlockSpec((tm, D), lambda i: (i, 0))] + _mod_specs(layer, (3, 4), tm) + [
            _resident((D, WM_W)),
            _resident((D, LANES)),
            _resident((LANES, 2 * B_QK)),
            _resident((1, 2 * B_QK)),
            _resident((2, n_l, A_W)),
        ],
        out_specs=[
            pl.BlockSpec((tm, PF_W), lambda i: (i, 0)),
            pl.BlockSpec((tm, PB_W), lambda i: (i, 0)),
        ],
        out_shape=[jax.ShapeDtypeStruct((T, PF_W), F32), jax.ShapeDtypeStruct((T, PB_W), BF16)],
        compiler_params=_params(1),
        name="ab_inproj",
    )(x, mod, mod, wmain, wz, gup, gb, hgrn_lb)


SCAN_UNROLL = 2


def _prefix_rows(x):
    row = lax.broadcasted_iota(jnp.int32, x.shape, 0)
    s = 1
    while s < x.shape[0]:
        x = x + jnp.where(row >= s, pltpu.roll(x, s, axis=0), 0.0)
        s *= 2
    return x


def _scan_body(*refs, seq_len, pair, has_s0, has_prev, emit_state):
    n = seq_len // CHUNK
    nh = 2 if pair else 1
    it = iter(refs)
    q_ref = next(it)
    if pair:
        k_ref, laf_ref, lab_ref = next(it), next(it), next(it)
    else:
        ff_ref, fb_ref = next(it), next(it)
    g_ref, v_ref, nw_ref = next(it), next(it), next(it)
    s0_ref = next(it) if has_s0 else None
    if has_prev:
        next(it)
    o_ref = next(it)
    st_ref = next(it) if emit_state else None
    qd_s, oi_s, kv_s, dec_s, sb_s = it

    row = lax.broadcasted_iota(jnp.int32, (CHUNK, CHUNK), 0)
    col = lax.broadcasted_iota(jnp.int32, (CHUNK, CHUNK), 1)
    tril = row >= col
    triu = row <= col
    lane = lax.broadcasted_iota(jnp.int32, (1, LANES), 1)
    lane2 = lax.broadcasted_iota(jnp.int32, (1, 2 * LANES), 1)
    if pair:
        masks = [lane < HALF, lane >= HALF]
        masks2 = [(lane2 % LANES) < HALF, (lane2 % LANES) >= HALF]
    else:
        masks, masks2 = [None], [None]

    def pick(mask, x):
        return x if mask is None else jnp.where(mask, x, jnp.zeros_like(x))

    def rows_of(c):
        return pl.ds(pl.multiple_of(c * CHUNK, CHUNK), CHUNK)

    def loop(body):
        if n <= SCAN_UNROLL:
            for c in range(n):
                body(c)
        else:
            def fbody(i, carry):
                for u in range(SCAN_UNROLL):
                    body(i * SCAN_UNROLL + u)
                return carry
            lax.fori_loop(0, n // SCAN_UNROLL, fbody, 0)

    def phase1(c):
        rows = rows_of(c)
        q = q_ref[rows, :]
        if pair:
            k_f = k_b = k_ref[rows, :]
            la_f, la_b = laf_ref[rows, :], lab_ref[rows, :]
        else:
            f_f, f_b = ff_ref[rows, :], fb_ref[rows, :]
            k_f, k_b = 1.0 - f_f, 1.0 - f_b
            la_f, la_b = jnp.log(f_f), jnp.log(f_b)
        cs = _prefix_rows(jnp.concatenate([la_f, la_b], axis=1))
        cf, cbi = cs[:, :LANES], cs[:, LANES:]
        tot_f, tot_b = cf[CHUNK - 1:CHUNK, :], cbi[CHUNK - 1:CHUNK, :]
        rb = tot_b - cbi + la_b
        ref_f, ref_b = cf[CHUNK // 2 - 1:CHUNK // 2, :], rb[CHUNK // 2:CHUNK // 2 + 1, :]
        qtf = q * jnp.exp(cf - ref_f)
        qtb = q * jnp.exp(rb - ref_b)
        ktf = k_f * jnp.exp(ref_f - cf)
        ktb = k_b * jnp.exp(ref_b - rb)
        qd = jnp.concatenate([qtf * jnp.exp(ref_f), qtb * jnp.exp(ref_b)], axis=1).astype(BF16)
        ku = jnp.concatenate([ktf * jnp.exp(tot_f - ref_f), ktb * jnp.exp(tot_b - ref_b)], axis=1).astype(BF16)
        qd_s[rows, :] = qd
        qt = jnp.concatenate([qtf, qtb], axis=0).astype(BF16)
        kt = jnp.concatenate([ktf, ktb], axis=0).astype(BF16)
        kv = None
        for hh in range(nh):
            v = v_ref[rows, hh * LANES:(hh + 1) * LANES]
            sc = _dot_nt(pick(masks[hh], qt), kt)
            att = jnp.where(tril, sc[:CHUNK, :CHUNK], 0.0) + jnp.where(triu, sc[CHUNK:, CHUNK:], 0.0)
            oi_s[rows, hh * LANES:(hh + 1) * LANES] = _dot(att.astype(BF16), v)
            kv_h = _dot_tn(v, ku)
            kv = kv_h if kv is None else jnp.where(masks2[0], kv, kv_h)
        kv_s[c] = kv
        dec_s[c] = jnp.exp(jnp.concatenate([tot_f, tot_b], axis=1))

    loop(phase1)

    def recurrence(d, reverse):
        cols = slice(d * LANES, (d + 1) * LANES)
        st0 = s0_ref[0, d, 0].T if has_s0 else jnp.zeros((LANES, LANES), F32)

        def step(c, st):
            sb_s[c, :, cols] = st.astype(BF16)
            return st * dec_s[c, :, cols] + kv_s[c, :, cols]

        if n <= 8:
            st = st0
            for c in (range(n - 1, -1, -1) if reverse else range(n)):
                st = step(c, st)
        else:
            st = lax.fori_loop(0, n, lambda i, st: step(n - 1 - i if reverse else i, st), st0)
        if emit_state:
            st_ref[0, d, 0] = st.T

    recurrence(0, False)
    recurrence(1, True)

    nw = nw_ref[...]

    def phase2(c):
        rows = rows_of(c)
        qcat = qd_s[rows, :]
        scat = sb_s[c]
        for hh in range(nh):
            cols = slice(hh * LANES, (hh + 1) * LANES)
            o = oi_s[rows, cols] + _dot_nt(pick(masks2[hh], qcat), scat)
            o = o * lax.rsqrt(jnp.mean(o * o, axis=-1, keepdims=True) + RMS_EPS) * nw
            o_ref[rows, cols] = (o * g_ref[rows, cols]).astype(BF16)

    loop(phase2)


def _scan(pf, pb, norm_w, s0, prev, *, prompt, pair):
    seq_len = SEQ if prompt else DEC_SEQ
    nseq = BATCH if prompt else DEC_BATCH
    row_off = 0 if prompt else TP // DEC_SEQ
    units = B_HEADS // 2 if pair else A_HEADS
    nh = 2 if pair else 1
    n = seq_len // CHUNK
    has_s0 = s0 is not None
    has_prev = prev is not None
    emit_state = prompt

    def colspec(off, width=LANES):
        base = off // width
        return pl.BlockSpec((seq_len, width), lambda s, u: (s + row_off, base + u))

    if pair:
        in_specs = [colspec(PF_BQ), colspec(PF_BK), colspec(PF_LAF), colspec(PF_LAB),
                    colspec(PF_BG, 2 * LANES), colspec(PB_BV, 2 * LANES)]
        args = [pf, pf, pf, pf, pf, pb]
    else:
        in_specs = [colspec(PF_AQ), colspec(PF_FF), colspec(PF_FB), colspec(PF_AG), colspec(PB_AV)]
        args = [pf, pf, pf, pf, pb]
    in_specs.append(pl.BlockSpec((1, LANES), lambda s, u: (0, 0)))
    args.append(norm_w.reshape(1, LANES))
    state_spec = pl.BlockSpec((1, 2, 1, LANES, LANES), lambda s, u: (s, 0, u, 0, 0))
    if has_s0:
        in_specs.append(state_spec)
        args.append(s0)
    aliases = {}
    if has_prev:
        aliases[len(args)] = 0
        in_specs.append(pl.BlockSpec(memory_space=pl.ANY))
        args.append(prev)
    out_specs = [pl.BlockSpec((seq_len, nh * LANES), lambda s, u: (s + row_off, u))]
    out_shape = [jax.ShapeDtypeStruct((T, units * nh * LANES), BF16)]
    if emit_state:
        out_specs.append(state_spec)
        out_shape.append(jax.ShapeDtypeStruct((nseq, 2, units, LANES, LANES), F32))
    scratch = [
        pltpu.VMEM((seq_len, 2 * LANES), BF16),
        pltpu.VMEM((seq_len, nh * LANES), F32),
        pltpu.VMEM((n, LANES, 2 * LANES), F32),
        pltpu.VMEM((n, 1, 2 * LANES), F32),
        pltpu.VMEM((n, LANES, 2 * LANES), BF16),
    ]
    return pl.pallas_call(
        functools.partial(_scan_body, seq_len=seq_len, pair=pair, has_s0=has_s0, has_prev=has_prev,
                          emit_state=emit_state),
        grid=(nseq, units),
        in_specs=in_specs,
        out_specs=out_specs,
        out_shape=out_shape,
        input_output_aliases=aliases,
        scratch_shapes=scratch,
        compiler_params=_params(2),
        name=f"scan_{'p' if prompt else 's'}_{'gla' if pair else 'hgrn'}",
    )(*args)


def _outproj_body(*refs, n_lhs):
    lhs = refs[:n_lhs]
    ws = refs[n_lhs:2 * n_lhs]
    x_ref, m_ref, g_ref, b_ref, o_ref = refs[2 * n_lhs:]
    y = _dot(lhs[0][...], ws[0][...])
    for a_ref, w_ref in zip(lhs[1:], ws[1:]):
        y = y + _dot(a_ref[...], w_ref[...])
    z = ALPHA * x_ref[...] + m_ref[...] * y
    o_ref[...] = _layer_norm(z, g_ref[...], b_ref[...])


def _outproj(lhs, ws, x, mod, ln_g, ln_b, layer, tm=OUT_TM):
    n_lhs = len(lhs)
    in_specs = [pl.BlockSpec((tm, a.shape[1]), lambda i: (i, 0)) for a in lhs]
    in_specs += [_resident(w.shape) for w in ws]
    in_specs += [pl.BlockSpec((tm, D), lambda i: (i, 0))] + _mod_specs(layer, (5,), tm) + _ln_specs(layer, 1)
    return pl.pallas_call(
        functools.partial(_outproj_body, n_lhs=n_lhs),
        grid=(T // tm,),
        in_specs=in_specs,
        out_specs=pl.BlockSpec((tm, D), lambda i: (i, 0)),
        out_shape=jax.ShapeDtypeStruct((T, D), F32),
        compiler_params=_params(1),
        name="mixer_outproj",
    )(*lhs, *ws, x, mod, ln_g, ln_b)


def _rope_partner(x):
    lane = lax.broadcasted_iota(jnp.int32, x.shape, 1)
    first_half = (lane % (2 * ROPE_FREQS)) < ROPE_FREQS
    return jnp.where(first_half, pltpu.roll(x, LANES - ROPE_FREQS, axis=1), pltpu.roll(x, ROPE_FREQS, axis=1))


def _qkv_body(x_ref, shift_ref, scale_ref, w_ref, wkvt_ref, cos_ref, sin_ref,
              q_ref, k_ref, v_ref, kt_ref, vt_ref):
    h = (x_ref[...] * (1.0 + scale_ref[...]) + shift_ref[...]).astype(BF16)
    qscale = C_HEAD_DIM ** -0.5
    v_ref[...] = _dot(h, w_ref[:, C_Q + C_KV:]).astype(BF16)
    in_prompt = pl.program_id(0) < TP // PROJ_TM

    @pl.when(in_prompt)
    def _():
        for j in range(C_Q // LANES):
            cols = slice(j * LANES, (j + 1) * LANES)
            q_ref[:, cols] = (_dot(h, w_ref[:, cols]) * qscale).astype(BF16)
        kt_ref[...] = _dot_nt(wkvt_ref[:C_KV, :], h)
        vt_ref[...] = _dot_nt(wkvt_ref[C_KV:, :], h)

    @pl.when(jnp.logical_not(in_prompt))
    def _():
        cos, sin = cos_ref[...], sin_ref[...]

        def rope(z):
            return z * cos + _rope_partner(z) * sin

        for j in range(C_Q // LANES):
            cols = slice(j * LANES, (j + 1) * LANES)
            q_ref[:, cols] = (rope(_dot(h, w_ref[:, cols])) * qscale).astype(BF16)
        for j in range(C_KV // LANES):
            cols = slice(j * LANES, (j + 1) * LANES)
            k_ref[:, cols] = rope(_dot(h, w_ref[:, C_Q + j * LANES:C_Q + (j + 1) * LANES])).astype(BF16)


def _qkv(x, mod, w, wkvt, cos, sin, layer):
    tm = PROJ_TM
    n_p = TP // tm
    lat = lambda i: (jnp.maximum(i - n_p, 0), 0)
    ctx = lambda i: (jnp.minimum(i, n_p - 1), 0, 0)
    return pl.pallas_call(
        _qkv_body,
        grid=(T // tm,),
        in_specs=[pl.BlockSpec((tm, D), lambda i: (i, 0))] + _mod_specs(layer, (3, 4), tm) + [
            _resident((D, C_Q + 2 * C_KV)),
            _resident((2 * C_KV, D)),
            pl.BlockSpec((tm, LANES), lat),
            pl.BlockSpec((tm, LANES), lat),
        ],
        out_specs=[
            pl.BlockSpec((tm, C_Q), lambda i: (i, 0)),
            pl.BlockSpec((tm, C_KV), lat),
            pl.BlockSpec((tm, C_KV), lambda i: (i, 0)),
            pl.BlockSpec((None, C_KV, SEQ), ctx),
            pl.BlockSpec((None, C_KV, SEQ), ctx),
        ],
        out_shape=[jax.ShapeDtypeStruct((T, C_Q), BF16), jax.ShapeDtypeStruct((TS, C_KV), BF16),
                   jax.ShapeDtypeStruct((T, C_KV), BF16),
                   jax.ShapeDtypeStruct((BATCH, C_KV, SEQ), F32), jax.ShapeDtypeStruct((BATCH, C_KV, SEQ), F32)],
        compiler_params=_params(1),
        name="c_qkv",
    )(x, mod, mod, w, wkvt, cos, sin)


def _rope_tables():
    t = np.arange(DEC_SEQ)
    pos = np.stack([t // GRID_W, t % GRID_W], axis=1).astype(np.float32)
    inv = (ROPE_BASE ** (-np.arange(ROPE_FREQS, dtype=np.float32) / ROPE_FREQS)).astype(np.float32)
    d = np.arange(C_HEAD_DIM)
    axis = d // (2 * ROPE_FREQS)
    ang = pos[:, axis] * inv[d % ROPE_FREQS][None, :]
    sign = np.where((d % (2 * ROPE_FREQS)) < ROPE_FREQS, -1.0, 1.0)[None, :]
    cos_h, sin_h = np.cos(ang), np.sin(ang) * sign
    reps = LANES // C_HEAD_DIM
    cos = np.tile(np.tile(cos_h, (1, reps)), (DEC_BATCH, 1)).astype(np.float32)
    sin = np.tile(np.tile(sin_h, (1, reps)), (DEC_BATCH, 1)).astype(np.float32)
    return jnp.asarray(cos), jnp.asarray(sin)


def _dup_head(blk, half):
    lane = lax.broadcasted_iota(jnp.int32, blk.shape, 1)
    keep = (lane >= HALF).astype(jnp.int32) == half
    return jnp.where(keep, blk, pltpu.roll(blk, HALF, axis=1))


def _stack_group(q_ref, sink_ref):
    tq = q_ref.shape[0]
    lane = lax.broadcasted_iota(jnp.int32, (1, LANES), 1)
    qs, sinks = [], []
    for g in range(C_GROUPS):
        blk = q_ref[:, (g // 2) * LANES:(g // 2 + 1) * LANES]
        keep = (lane < HALF) if g % 2 == 0 else (lane >= HALF)
        qs.append(jnp.where(keep, blk, jnp.zeros_like(blk)))
        sinks.append(jnp.broadcast_to(sink_ref[:, g * C_HEAD_DIM:g * C_HEAD_DIM + 1], (tq, 1)))
    return jnp.concatenate(qs, axis=0), jnp.concatenate(sinks, axis=0)


def _unstack_group(o4, o_ref):
    tq = o_ref.shape[0]
    lo = lax.broadcasted_iota(jnp.int32, (1, LANES), 1) < HALF
    for jb in range(C_GROUPS // 2):
        a = o4[(2 * jb) * tq:(2 * jb + 1) * tq]
        b = o4[(2 * jb + 1) * tq:(2 * jb + 2) * tq]
        o_ref[:, jb * LANES:(jb + 1) * LANES] = jnp.where(lo, a, b).astype(BF16)


def _values_and_ones(blk, half):
    v2 = _dup_head(blk, half).astype(BF16)
    return jnp.concatenate([v2, jnp.ones_like(v2)], axis=1)


def _ctx_attn_body(q_ref, kt_ref, v_ref, sink_ref, o_ref):
    half = pl.program_id(1) % 2
    kt = kt_ref[pl.ds(pl.multiple_of(half * C_HEAD_DIM, C_HEAD_DIM), C_HEAD_DIM), :].astype(BF16)
    v2 = _values_and_ones(v_ref[...], half)
    q4, sink = _stack_group(q_ref, sink_ref)
    s = _dot(q4, jnp.concatenate([kt, kt], axis=0))
    mx = jnp.maximum(jnp.max(s, axis=-1, keepdims=True), sink)
    od = _dot(jnp.exp(s - mx).astype(BF16), v2)
    den = od[:, LANES:] + jnp.exp(sink - mx)
    _unstack_group(od[:, :LANES] / den, o_ref)


def _ctx_attn(q, kt, v, sink_cols):
    gw = C_GROUPS * C_HEAD_DIM
    return pl.pallas_call(
        _ctx_attn_body,
        grid=(BATCH, C_KV_HEADS),
        in_specs=[
            pl.BlockSpec((SEQ, gw), lambda b, h: (b, h)),
            pl.BlockSpec((None, LANES, SEQ), lambda b, h: (b, h // 2, 0)),
            pl.BlockSpec((SEQ, LANES), lambda b, h: (b, h // 2)),
            pl.BlockSpec((1, gw), lambda b, h: (0, h)),
        ],
        out_specs=pl.BlockSpec((SEQ, gw), lambda b, h: (b, h)),
        out_shape=jax.ShapeDtypeStruct((T, C_Q), BF16),
        compiler_params=_params(2),
        name="c_attn_ctx",
    )(q, kt, v, sink_cols)


LAT_TQ = 128
LAT_WIN = LAT_TQ + 2 * WINDOW


def _lat_attn_body(q_ref, k_ref, v_ref, ck_ref, cv_ref, sink_ref, prev_ref, o_ref, k2_s, v2_s, ck2_s, cv2_s):
    del prev_ref
    half = pl.program_id(1) % 2
    j = pl.program_id(2)

    @pl.when(j == 0)
    def _():
        k2_s[...] = _dup_head(k_ref[...], half).astype(BF16)
        v2_s[...] = _values_and_ones(v_ref[...], half)
        ck2_s[...] = _dup_head(ck_ref[...], half).astype(BF16)
        cv2_s[...] = _values_and_ones(cv_ref[...], half)

    start = pl.multiple_of(jnp.clip(j * LAT_TQ - WINDOW, 0, DEC_SEQ - LAT_WIN), LANES)
    kl = k2_s[pl.ds(start, LAT_WIN), :]
    vl = v2_s[pl.ds(start, LAT_WIN), :]
    q4, sink = _stack_group(q_ref, sink_ref)
    rows = C_GROUPS * LAT_TQ
    qpos = j * LAT_TQ + lax.broadcasted_iota(jnp.int32, (rows, LAT_WIN), 0) % LAT_TQ
    kpos = start + lax.broadcasted_iota(jnp.int32, (rows, LAT_WIN), 1)
    sl = jnp.where(jnp.abs(qpos - kpos) <= WINDOW, _dot_nt(q4, kl), -jnp.inf)
    sc = _dot_nt(q4, ck2_s[...])
    mx = jnp.maximum(jnp.maximum(jnp.max(sl, axis=-1, keepdims=True), jnp.max(sc, axis=-1, keepdims=True)), sink)
    el = jnp.exp(sl - mx).astype(BF16)
    ec = jnp.exp(sc - mx).astype(BF16)
    od = _dot(el, vl) + _dot(ec, cv2_s[...])
    den = od[:, LANES:] + jnp.exp(sink - mx)
    _unstack_group(od[:, :LANES] / den, o_ref)


def _lat_attn(q, k, v, ck, cv, sink_cols, o_prompt):
    gw = C_GROUPS * C_HEAD_DIM
    nq = DEC_SEQ // LAT_TQ
    q_off = TP // LAT_TQ
    kv_off = TP // DEC_SEQ
    return pl.pallas_call(
        _lat_attn_body,
        grid=(DEC_BATCH, C_KV_HEADS, nq),
        in_specs=[
            pl.BlockSpec((LAT_TQ, gw), lambda b, h, j: (q_off + b * nq + j, h)),
            pl.BlockSpec((DEC_SEQ, LANES), lambda b, h, j: (b, h // 2)),
            pl.BlockSpec((DEC_SEQ, LANES), lambda b, h, j: (kv_off + b, h // 2)),
            pl.BlockSpec((PAST_LEN, LANES), lambda b, h, j: (b, h // 2)),
            pl.BlockSpec((PAST_LEN, LANES), lambda b, h, j: (b, h // 2)),
            pl.BlockSpec((1, gw), lambda b, h, j: (0, h)),
            pl.BlockSpec(memory_space=pl.ANY),
        ],
        out_specs=pl.BlockSpec((LAT_TQ, gw), lambda b, h, j: (q_off + b * nq + j, h)),
        out_shape=jax.ShapeDtypeStruct((T, C_Q), BF16),
        input_output_aliases={6: 0},
        scratch_shapes=[
            pltpu.VMEM((DEC_SEQ, LANES), BF16), pltpu.VMEM((DEC_SEQ, 2 * LANES), BF16),
            pltpu.VMEM((PAST_LEN, LANES), BF16), pltpu.VMEM((PAST_LEN, 2 * LANES), BF16),
        ],
        compiler_params=_params(3),
        name="c_attn_latent",
    )(q, k, v, ck, cv, sink_cols, o_prompt)


def kernel(x_prompt, x_sample, state_hgrn, state_gla, cache_k, cache_v, c, c_ctx, w_mod, b_mod, ln_g, ln_b,
           ffn_w1, ffn_w3, ffn_w2, w_in_ab, hgrn_lb, gla_gate_up, gla_gate_b, norm_a, norm_b, w_out_ab,
           w_qkv_c, sink_c, w_out_c):
    cs = jnp.zeros((8, D), F32).at[0].set(c_ctx).at[1:1 + DEC_BATCH].set(c)
    mod = _mod_vectors(cs, w_mod, b_mod).reshape(DEPTH, 8, 1, N_MOD * D)
    w1, w3, w2 = ffn_w1.astype(BF16), ffn_w3.astype(BF16), ffn_w2.astype(BF16)
    ln_g, ln_b = ln_g.reshape(DEPTH, 3, 1, D), ln_b.reshape(DEPTH, 3, 1, D)

    def ffn(xs, layer, sub, split_out=False):
        return _ffn_sublayer(xs, mod, w1, w3, w2, ln_g, ln_b, layer, sub, split_out=split_out)

    (x,) = ffn([x_prompt.reshape(TP, D), x_sample.reshape(TS, D)], 0, 0)
    w_in = w_in_ab[0]
    o_aq, o_ai, o_ff, o_fb, o_ag = 0, A_W, 2 * A_W, 3 * A_W, 4 * A_W
    o_bq = 5 * A_W
    o_bk, o_bv = o_bq + B_QK, o_bq + 2 * B_QK
    o_bg = o_bv + B_V
    o_z = o_bg + B_V
    order = [(o_aq, A_W), (o_ff, A_W), (o_fb, A_W), (o_ag, A_W), (o_bq, B_QK), (o_bk, B_QK), (o_bg, B_V),
             (o_ai, A_W), (o_bv, B_V)]
    wmain = jnp.concatenate([w_in[:, o:o + w] for o, w in order], axis=1).astype(BF16)
    wz = jnp.pad(w_in[:, o_z:o_z + 2 * GATE_RANK], ((0, 0), (0, LANES - 2 * GATE_RANK))).astype(BF16)
    gup = jnp.zeros((LANES, 2 * B_QK), F32)
    gup = gup.at[:GATE_RANK, :B_QK].set(gla_gate_up[0, 0]).at[GATE_RANK:2 * GATE_RANK, B_QK:].set(gla_gate_up[0, 1])
    gb = gla_gate_b[0].reshape(1, 2 * B_QK)
    pf, pb = _inproj(x, mod, wmain, wz, gup.astype(BF16), gb, hgrn_lb, 0, 0)

    s0_a = state_hgrn[:, 0]
    s0_b = state_gla[:, 0].reshape(DEC_BATCH, 2, B_HEADS // 2, LANES, B_DV)
    oa, st_a = _scan(pf, pb, norm_a[0], None, None, prompt=True, pair=False)
    ob, st_b = _scan(pf, pb, norm_b[0], None, None, prompt=True, pair=True)
    (oa,) = _scan(pf, pb, norm_a[0], s0_a, oa, prompt=False, pair=False)
    (ob,) = _scan(pf, pb, norm_b[0], s0_b, ob, prompt=False, pair=True)
    w_out = w_out_ab[0].astype(BF16)
    x = _outproj([oa, ob], [w_out[:A_W], w_out[A_W:]], x, mod, ln_g, ln_b, 0)
    (x,) = ffn([x], 0, 1)
    new_hgrn = st_a.reshape(BATCH, 1, 2, A_HEADS, A_DK, A_DV)
    new_gla = st_b.reshape(BATCH, 1, 2, B_HEADS, B_DK, B_DV)

    (x,) = ffn([x], 1, 0)
    cos, sin = _rope_tables()
    w_qkv = w_qkv_c[0].astype(BF16)
    q, k, v, kt, vt = _qkv(x, mod, w_qkv, w_qkv[:, C_Q:].T, cos, sin, 1)
    sink_cols = jnp.repeat(sink_c[0], C_HEAD_DIM).reshape(1, C_Q)
    o = _ctx_attn(q, kt, v, sink_cols)
    ck = cache_k[:, 0].reshape(DEC_BATCH * PAST_LEN, C_KV)
    cv = cache_v[:, 0].reshape(DEC_BATCH * PAST_LEN, C_KV)
    o = _lat_attn(q, k, v, ck, cv, sink_cols, o)
    x = _outproj([o], [w_out_c[0].astype(BF16)], x, mod, ln_g, ln_b, 1)
    y_p, y_s = ffn([x], 1, 1, split_out=True)

    def cache_layout(zt):
        return zt.reshape(BATCH, 1, C_KV_HEADS, C_HEAD_DIM, SEQ).transpose(0, 1, 4, 2, 3)

    new_k, new_v = cache_layout(kt), cache_layout(vt)

    return (y_p.reshape(BATCH, SEQ, D), y_s.reshape(DEC_BATCH, DEC_SEQ, D), new_hgrn, new_gla, new_k, new_v)
```

```python
import functools
import math

import jax
import jax.numpy as jnp
import numpy as np
from jax import lax
from jax.experimental import pallas as pl
from jax.experimental.pallas import tpu as pltpu

D = 1024
BATCH, SEQ = 16, 256
DEC_BATCH, DEC_SEQ = 2, 2048
PAST_LEN = 512
GRID_W = 64
D_FF = 2816
N_MOD = 9
A_HEADS, A_DK, A_DV = 4, 128, 128
A_W = A_HEADS * A_DK
B_HEADS, B_DK, B_DV = 4, 64, 128
B_QK = B_HEADS * B_DK
B_V = B_HEADS * B_DV
GATE_RANK = 16
GLA_TAU = 16.0
CHUNK = 128
C_HEADS, C_KV_HEADS, C_HEAD_DIM = 16, 4, 64
C_GROUPS = C_HEADS // C_KV_HEADS
C_Q = C_HEADS * C_HEAD_DIM
C_KV = C_KV_HEADS * C_HEAD_DIM
WINDOW = 128
ROPE_FREQS = C_HEAD_DIM // 4
ROPE_BASE = 10000.0
DEPTH = 2
ALPHA = (2.0 * DEPTH) ** 0.25
LN_EPS = 1e-5
RMS_EPS = 1e-6

TP = BATCH * SEQ
TS = DEC_BATCH * DEC_SEQ
T = TP + TS
N_SEG = 1 + DEC_BATCH

LANES = 128
HALF = LANES // 2
FFN_TM = 512
PROJ_TM = SEQ
OUT_TM = 512
VMEM_LIMIT = 56 * 1024 * 1024

F32 = jnp.float32
BF16 = jnp.bfloat16


def _dot(a, b):
    return jnp.dot(a, b, preferred_element_type=F32)


def _dot_nt(a, b):
    return lax.dot_general(a, b, (((1,), (1,)), ((), ())), preferred_element_type=F32)


def _dot_tn(a, b):
    return lax.dot_general(a, b, (((0,), (0,)), ((), ())), preferred_element_type=F32)


def _silu(x):
    return x * jax.nn.sigmoid(x)


def _layer_norm(z, g, b):
    mu = jnp.mean(z, axis=-1, keepdims=True)
    zc = z - mu
    var = jnp.mean(zc * zc, axis=-1, keepdims=True)
    return zc * lax.rsqrt(var + LN_EPS) * g + b


def _seg_of_tile(i, tm):
    n_p = TP // tm
    n_s = DEC_SEQ // tm
    return jnp.where(i < n_p, 0, 1 + lax.div(jnp.maximum(i - n_p, 0), n_s))


def _params(n_axes):
    return pltpu.CompilerParams(dimension_semantics=("arbitrary",) * n_axes, vmem_limit_bytes=VMEM_LIMIT)


def _resident(shape):
    nd = len(shape)
    return pl.BlockSpec(shape, lambda *_: (0,) * nd, pipeline_mode=pl.Buffered(1))


def _resident_slice(shape, lead):
    block = (None,) * len(lead) + tuple(shape)
    return pl.BlockSpec(block, lambda *_: tuple(lead) + (0,) * len(shape), pipeline_mode=pl.Buffered(1))


def _mod_specs(layer, cols, tm):
    return [pl.BlockSpec((None, None, 1, D), functools.partial(
        lambda i, c: (layer, _seg_of_tile(i, tm), 0, c), c=c)) for c in cols]


def _ln_specs(layer, idx):
    return [_resident_slice((1, D), (layer, idx))] * 2


def _mod_body(c_ref, w_ref, b_ref, o_ref):
    c = c_ref[...]
    s = _silu(c).astype(BF16)
    o_ref[0] = _dot(s, w_ref[0].astype(BF16)) + b_ref[0]


def _mod_vectors(cs, w_mod, b_mod):
    tn = 1536
    n = N_MOD * D
    return pl.pallas_call(
        _mod_body,
        grid=(DEPTH, n // tn),
        in_specs=[
            pl.BlockSpec((8, D), lambda l, j: (0, 0)),
            pl.BlockSpec((1, D, tn), lambda l, j: (l, 0, j)),
            pl.BlockSpec((1, 1, tn), lambda l, j: (l, 0, j)),
        ],
        out_specs=pl.BlockSpec((1, 8, tn), lambda l, j: (l, 0, j)),
        out_shape=jax.ShapeDtypeStruct((DEPTH, 8, n), F32),
        compiler_params=_params(2),
        name="mod_vectors",
    )(cs, w_mod, b_mod.reshape(DEPTH, 1, n))


def _ffn_body(*refs, n_x, n_o, tm):
    x_refs = refs[:n_x]
    shift_ref, scale_ref, gate_ref, w1_ref, w3_ref, w2_ref, g_ref, b_ref = refs[n_x:n_x + 8]
    o_refs = refs[n_x + 8:]

    def compute(x_ref, o_ref):
        x = x_ref[...]
        shift, scale, gate = shift_ref[...], scale_ref[...], gate_ref[...]
        h = (x * (1.0 + scale) + shift).astype(BF16)
        a = _dot(h, w1_ref[...])
        b = _dot(h, w3_ref[...])
        g = (_silu(a) * b).astype(BF16)
        y = _dot(g, w2_ref[...])
        z = ALPHA * x + (0.5 * gate) * y
        o_ref[...] = _layer_norm(z, g_ref[...], b_ref[...])

    if n_x == 1 and n_o == 1:
        compute(x_refs[0], o_refs[0])
    else:
        in_prompt = pl.program_id(0) < TP // tm
        pl.when(in_prompt)(lambda: compute(x_refs[0], o_refs[0]))
        pl.when(jnp.logical_not(in_prompt))(lambda: compute(x_refs[-1], o_refs[-1]))


def _group_specs(split, tm, width=D):
    if not split:
        return [pl.BlockSpec((tm, width), lambda i: (i, 0))]
    n_p = TP // tm
    return [pl.BlockSpec((tm, width), lambda i: (jnp.minimum(i, n_p - 1), 0)),
            pl.BlockSpec((tm, width), lambda i: (jnp.maximum(i - n_p, 0), 0))]


def _ffn_sublayer(xs, mod, w1, w3, w2, ln_g, ln_b, layer, sub, split_out=False, tm=FFN_TM):
    n_x, n_o = len(xs), 2 if split_out else 1
    out_shape = ([jax.ShapeDtypeStruct((TP, D), F32), jax.ShapeDtypeStruct((TS, D), F32)] if split_out
                 else [jax.ShapeDtypeStruct((T, D), F32)])
    mod_lo = 6 * sub
    return pl.pallas_call(
        functools.partial(_ffn_body, n_x=n_x, n_o=n_o, tm=tm),
        grid=(T // tm,),
        in_specs=_group_specs(n_x == 2, tm) + _mod_specs(layer, (mod_lo, mod_lo + 1, mod_lo + 2), tm) + [
            _resident_slice((D, D_FF), (layer, sub)),
            _resident_slice((D, D_FF), (layer, sub)),
            _resident_slice((D_FF, D), (layer, sub)),
        ] + _ln_specs(layer, 2 * sub),
        out_specs=_group_specs(split_out, tm),
        out_shape=out_shape,
        compiler_params=_params(1),
        name="ffn_sublayer",
    )(*xs, mod, mod, mod, w1, w3, w2, ln_g, ln_b)


PF_AQ, PF_FF, PF_FB, PF_AG = 0, 512, 1024, 1536
PF_BQ, PF_BK, PF_BG, PF_LAF, PF_LAB = 2048, 2304, 2560, 3072, 3328
PF_W = 3584
PB_AV, PB_BV = 0, 512
PB_W = 1024
WM_AQ, WM_FF, WM_FB, WM_AG, WM_BQ, WM_BK, WM_BG, WM_AI, WM_BV = 0, 512, 1024, 1536, 2048, 2304, 2560, 3072, 3584
WM_W = 4096


def _log_sigmoid(x):
    return jnp.minimum(x, 0.0) - jnp.log(1.0 + jnp.exp(-jnp.abs(x)))


def _inproj_body(x_ref, shift_ref, scale_ref, w_ref, wz_ref, gu_ref, gb_ref, lb_ref, pf_ref, pb_ref, *, layer_e):
    h = (x_ref[...] * (1.0 + scale_ref[...]) + shift_ref[...]).astype(BF16)

    def proj(off, width):
        return _dot(h, w_ref[:, off:off + width])

    def lower_bound(d):
        l = lb_ref[d]
        e = jnp.exp(l - jnp.max(l, axis=0, keepdims=True))
        sm = e / jnp.sum(e, axis=0, keepdims=True)
        return jnp.sum(sm[:layer_e + 1], axis=0, keepdims=True)

    pf_ref[:, PF_AQ:PF_AQ + A_W] = proj(WM_AQ, A_W)
    for d, (wm, pf) in enumerate(((WM_FF, PF_FF), (WM_FB, PF_FB))):
        lb = lower_bound(d)
        pf_ref[:, pf:pf + A_W] = lb + (1.0 - lb) * jax.nn.sigmoid(proj(wm, A_W))
    pf_ref[:, PF_AG:PF_AG + A_W] = _silu(proj(WM_AG, A_W))
    pf_ref[:, PF_BQ:PF_BQ + B_QK] = proj(WM_BQ, B_QK) * (B_DK ** -0.5)
    pf_ref[:, PF_BK:PF_BK + B_QK] = proj(WM_BK, B_QK)
    pf_ref[:, PF_BG:PF_BG + B_V] = _silu(proj(WM_BG, B_V))
    pb_ref[:, PB_AV:PB_AV + A_W] = _silu(proj(WM_AI, A_W)).astype(BF16)
    pb_ref[:, PB_BV:PB_BV + B_V] = proj(WM_BV, B_V).astype(BF16)
    z = _dot(h, wz_ref[...]).astype(BF16)
    pre = _dot(z, gu_ref[...]) + gb_ref[...]
    pf_ref[:, PF_LAF:PF_LAF + 2 * B_QK] = _log_sigmoid(pre) * (1.0 / GLA_TAU)


def _inproj(x, mod, wmain, wz, gup, gb, hgrn_lb, layer, layer_e, tm=PROJ_TM):
    n_l = hgrn_lb.shape[1]
    return pl.pallas_call(
        functools.partial(_inproj_body, layer_e=layer_e),
        grid=(T // tm,),
        in_specs=[pl.BlockSpec((tm, D), lambda i: (i, 0))] + _mod_specs(layer, (3, 4), tm) + [
            _resident((D, WM_W)),
            _resident((D, LANES)),
            _resident((LANES, 2 * B_QK)),
            _resident((1, 2 * B_QK)),
            _resident((2, n_l, A_W)),
        ],
        out_specs=[
            pl.BlockSpec((tm, PF_W), lambda i: (i, 0)),
            pl.BlockSpec((tm, PB_W), lambda i: (i, 0)),
        ],
        out_shape=[jax.ShapeDtypeStruct((T, PF_W), F32), jax.ShapeDtypeStruct((T, PB_W), BF16)],
        compiler_params=_params(1),
        name="ab_inproj",
    )(x, mod, mod, wmain, wz, gup, gb, hgrn_lb)


SCAN_UNROLL = 2


def _prefix_rows(x):
    row = lax.broadcasted_iota(jnp.int32, x.shape, 0)
    s = 1
    while s < x.shape[0]:
        x = x + jnp.where(row >= s, pltpu.roll(x, s, axis=0), 0.0)
        s *= 2
    return x


def _scan_body(*refs, seq_len, pair, has_s0, emit_state):
    n = seq_len // CHUNK
    nh = 2 if pair else 1
    it = iter(refs)
    q_ref = next(it)
    if pair:
        k_ref, laf_ref, lab_ref = next(it), next(it), next(it)
    else:
        ff_ref, fb_ref = next(it), next(it)
    g_ref, v_ref, nw_ref = next(it), next(it), next(it)
    s0_ref = next(it) if has_s0 else None
    o_ref = next(it)
    st_ref = next(it) if emit_state else None
    qd_s, oi_s, kv_s, dec_s, sb_s = it

    row = lax.broadcasted_iota(jnp.int32, (CHUNK, CHUNK), 0)
    col = lax.broadcasted_iota(jnp.int32, (CHUNK, CHUNK), 1)
    tril = row >= col
    triu = row <= col
    lane = lax.broadcasted_iota(jnp.int32, (1, LANES), 1)
    lane2 = lax.broadcasted_iota(jnp.int32, (1, 2 * LANES), 1)
    if pair:
        masks = [lane < HALF, lane >= HALF]
        masks2 = [(lane2 % LANES) < HALF, (lane2 % LANES) >= HALF]
    else:
        masks, masks2 = [None], [None]

    def pick(mask, x):
        return x if mask is None else jnp.where(mask, x, jnp.zeros_like(x))

    def rows_of(c):
        return pl.ds(pl.multiple_of(c * CHUNK, CHUNK), CHUNK)

    def loop(body):
        if n <= SCAN_UNROLL:
            for c in range(n):
                body(c)
        else:
            def fbody(i, carry):
                for u in range(SCAN_UNROLL):
                    body(i * SCAN_UNROLL + u)
                return carry
            lax.fori_loop(0, n // SCAN_UNROLL, fbody, 0)

    def phase1(c):
        rows = rows_of(c)
        q = q_ref[rows, :]
        if pair:
            k_f = k_b = k_ref[rows, :]
            la_f, la_b = laf_ref[rows, :], lab_ref[rows, :]
        else:
            f_f, f_b = ff_ref[rows, :], fb_ref[rows, :]
            k_f, k_b = 1.0 - f_f, 1.0 - f_b
            la_f, la_b = jnp.log(f_f), jnp.log(f_b)
        cs = _prefix_rows(jnp.concatenate([la_f, la_b], axis=1))
        cf, cbi = cs[:, :LANES], cs[:, LANES:]
        tot_f, tot_b = cf[CHUNK - 1:CHUNK, :], cbi[CHUNK - 1:CHUNK, :]
        rb = tot_b - cbi + la_b
        ref_f, ref_b = cf[CHUNK // 2 - 1:CHUNK // 2, :], rb[CHUNK // 2:CHUNK // 2 + 1, :]
        qtf = q * jnp.exp(cf - ref_f)
        qtb = q * jnp.exp(rb - ref_b)
        ktf = k_f * jnp.exp(ref_f - cf)
        ktb = k_b * jnp.exp(ref_b - rb)
        qd = jnp.concatenate([qtf * jnp.exp(ref_f), qtb * jnp.exp(ref_b)], axis=1).astype(BF16)
        ku = jnp.concatenate([ktf * jnp.exp(tot_f - ref_f), ktb * jnp.exp(tot_b - ref_b)], axis=1).astype(BF16)
        qd_s[rows, :] = qd
        qt = jnp.concatenate([qtf, qtb], axis=0).astype(BF16)
        kt = jnp.concatenate([ktf, ktb], axis=0).astype(BF16)
        kv = None
        for hh in range(nh):
            v = v_ref[rows, hh * LANES:(hh + 1) * LANES]
            sc = _dot_nt(pick(masks[hh], qt), kt)
            att = jnp.where(tril, sc[:CHUNK, :CHUNK], 0.0) + jnp.where(triu, sc[CHUNK:, CHUNK:], 0.0)
            oi_s[rows, hh * LANES:(hh + 1) * LANES] = _dot(att.astype(BF16), v)
            kv_h = _dot_tn(v, ku)
            kv = kv_h if kv is None else jnp.where(masks2[0], kv, kv_h)
        kv_s[c] = kv
        dec_s[c] = jnp.exp(jnp.concatenate([tot_f, tot_b], axis=1))

    loop(phase1)

    def recurrence(d, reverse):
        cols = slice(d * LANES, (d + 1) * LANES)
        st0 = s0_ref[0, d, 0].T if has_s0 else jnp.zeros((LANES, LANES), F32)

        def step(c, st):
            sb_s[c, :, cols] = st.astype(BF16)
            return st * dec_s[c, :, cols] + kv_s[c, :, cols]

        if n <= 8:
            st = st0
            for c in (range(n - 1, -1, -1) if reverse else range(n)):
                st = step(c, st)
        else:
            st = lax.fori_loop(0, n, lambda i, st: step(n - 1 - i if reverse else i, st), st0)
        if emit_state:
            st_ref[0, d, 0] = st.T

    recurrence(0, False)
    recurrence(1, True)

    nw = nw_ref[...]

    def phase2(c):
        rows = rows_of(c)
        qcat = qd_s[rows, :]
        scat = sb_s[c]
        for hh in range(nh):
            cols = slice(hh * LANES, (hh + 1) * LANES)
            o = oi_s[rows, cols] + _dot_nt(pick(masks2[hh], qcat), scat)
            o = o * lax.rsqrt(jnp.mean(o * o, axis=-1, keepdims=True) + RMS_EPS) * nw
            o_ref[rows, cols] = (o * g_ref[rows, cols]).astype(BF16)

    loop(phase2)


def _scan(pf, pb, norm_w, s0, *, prompt, pair):
    seq_len = SEQ if prompt else DEC_SEQ
    nseq = BATCH if prompt else DEC_BATCH
    row_off = 0 if prompt else TP // DEC_SEQ
    units = B_HEADS // 2 if pair else A_HEADS
    nh = 2 if pair else 1
    n = seq_len // CHUNK
    has_s0 = s0 is not None
    emit_state = prompt

    def colspec(off, width=LANES):
        base = off // width
        return pl.BlockSpec((seq_len, width), lambda s, u: (s + row_off, base + u))

    if pair:
        in_specs = [colspec(PF_BQ), colspec(PF_BK), colspec(PF_LAF), colspec(PF_LAB),
                    colspec(PF_BG, 2 * LANES), colspec(PB_BV, 2 * LANES)]
        args = [pf, pf, pf, pf, pf, pb]
    else:
        in_specs = [colspec(PF_AQ), colspec(PF_FF), colspec(PF_FB), colspec(PF_AG), colspec(PB_AV)]
        args = [pf, pf, pf, pf, pb]
    in_specs.append(pl.BlockSpec((1, LANES), lambda s, u: (0, 0)))
    args.append(norm_w.reshape(1, LANES))
    state_spec = pl.BlockSpec((1, 2, 1, LANES, LANES), lambda s, u: (s, 0, u, 0, 0))
    if has_s0:
        in_specs.append(state_spec)
        args.append(s0)
    out_specs = [pl.BlockSpec((seq_len, nh * LANES), lambda s, u: (s, u))]
    out_shape = [jax.ShapeDtypeStruct((nseq * seq_len, units * nh * LANES), BF16)]
    if emit_state:
        out_specs.append(state_spec)
        out_shape.append(jax.ShapeDtypeStruct((nseq, 2, units, LANES, LANES), F32))
    scratch = [
        pltpu.VMEM((seq_len, 2 * LANES), BF16),
        pltpu.VMEM((seq_len, nh * LANES), F32),
        pltpu.VMEM((n, LANES, 2 * LANES), F32),
        pltpu.VMEM((n, 1, 2 * LANES), F32),
        pltpu.VMEM((n, LANES, 2 * LANES), BF16),
    ]
    return pl.pallas_call(
        functools.partial(_scan_body, seq_len=seq_len, pair=pair, has_s0=has_s0, emit_state=emit_state),
        grid=(nseq, units),
        in_specs=in_specs,
        out_specs=out_specs,
        out_shape=out_shape,
        scratch_shapes=scratch,
        compiler_params=_params(2),
        name=f"scan_{'p' if prompt else 's'}_{'gla' if pair else 'hgrn'}",
    )(*args)


def _outproj_body(*refs, n_lhs, tm):
    lhs = refs[:2 * n_lhs]
    ws = refs[2 * n_lhs:3 * n_lhs]
    x_ref, m_ref, g_ref, b_ref, o_ref = refs[3 * n_lhs:]

    def compute(group):
        y = _dot(lhs[group][...], ws[0][...])
        for j in range(1, n_lhs):
            y = y + _dot(lhs[2 * j + group][...], ws[j][...])
        z = ALPHA * x_ref[...] + m_ref[...] * y
        o_ref[...] = _layer_norm(z, g_ref[...], b_ref[...])

    in_prompt = pl.program_id(0) < TP // tm
    pl.when(in_prompt)(lambda: compute(0))
    pl.when(jnp.logical_not(in_prompt))(lambda: compute(1))


def _outproj(lhs, ws, x, mod, ln_g, ln_b, layer, tm=OUT_TM):
    n_lhs = len(lhs)
    in_specs, args = [], []
    for a_p, a_s in lhs:
        in_specs += _group_specs(True, tm, a_p.shape[1])
        args += [a_p, a_s]
    in_specs += [_resident(w.shape) for w in ws]
    in_specs += [pl.BlockSpec((tm, D), lambda i: (i, 0))] + _mod_specs(layer, (5,), tm) + _ln_specs(layer, 1)
    return pl.pallas_call(
        functools.partial(_outproj_body, n_lhs=n_lhs, tm=tm),
        grid=(T // tm,),
        in_specs=in_specs,
        out_specs=pl.BlockSpec((tm, D), lambda i: (i, 0)),
        out_shape=jax.ShapeDtypeStruct((T, D), F32),
        compiler_params=_params(1),
        name="mixer_outproj",
    )(*args, *ws, x, mod, ln_g, ln_b)


def _rope_partner(x):
    lane = lax.broadcasted_iota(jnp.int32, x.shape, 1)
    first_half = (lane % (2 * ROPE_FREQS)) < ROPE_FREQS
    return jnp.where(first_half, pltpu.roll(x, LANES - ROPE_FREQS, axis=1), pltpu.roll(x, ROPE_FREQS, axis=1))


def _qkv_body(x_ref, shift_ref, scale_ref, w_ref, wkvt_ref, cos_ref, sin_ref,
              q_ref, k_ref, v_ref, kt_ref, vt_ref):
    h = (x_ref[...] * (1.0 + scale_ref[...]) + shift_ref[...]).astype(BF16)
    qscale = C_HEAD_DIM ** -0.5
    v_ref[...] = _dot(h, w_ref[:, C_Q + C_KV:]).astype(BF16)
    in_prompt = pl.program_id(0) < TP // PROJ_TM

    @pl.when(in_prompt)
    def _():
        q_ref[...] = (_dot(h, w_ref[:, :C_Q]) * qscale).astype(BF16)
        kt_ref[...] = _dot_nt(wkvt_ref[:C_KV, :], h)
        vt_ref[...] = _dot_nt(wkvt_ref[C_KV:, :], h)

    @pl.when(jnp.logical_not(in_prompt))
    def _():
        cos, sin = cos_ref[...], sin_ref[...]
        zq = _dot(h, w_ref[:, :C_Q])
        zk = _dot(h, w_ref[:, C_Q:C_Q + C_KV])

        def rope(z):
            return z * cos + _rope_partner(z) * sin

        for j in range(C_Q // LANES):
            cols = slice(j * LANES, (j + 1) * LANES)
            q_ref[:, cols] = (rope(zq[:, cols]) * qscale).astype(BF16)
        for j in range(C_KV // LANES):
            cols = slice(j * LANES, (j + 1) * LANES)
            k_ref[:, cols] = rope(zk[:, cols]).astype(BF16)


def _qkv(x, mod, w, wkvt, cos, sin, layer):
    tm = PROJ_TM
    n_p = TP // tm
    lat = lambda i: (jnp.maximum(i - n_p, 0), 0)
    ctx = lambda i: (jnp.minimum(i, n_p - 1), 0, 0)
    return pl.pallas_call(
        _qkv_body,
        grid=(T // tm,),
        in_specs=[pl.BlockSpec((tm, D), lambda i: (i, 0))] + _mod_specs(layer, (3, 4), tm) + [
            _resident((D, C_Q + 2 * C_KV)),
            _resident((2 * C_KV, D)),
            pl.BlockSpec((tm, LANES), lat),
            pl.BlockSpec((tm, LANES), lat),
        ],
        out_specs=[
            pl.BlockSpec((tm, C_Q), lambda i: (i, 0)),
            pl.BlockSpec((tm, C_KV), lat),
            pl.BlockSpec((tm, C_KV), lambda i: (i, 0)),
            pl.BlockSpec((None, C_KV, SEQ), ctx),
            pl.BlockSpec((None, C_KV, SEQ), ctx),
        ],
        out_shape=[jax.ShapeDtypeStruct((T, C_Q), BF16), jax.ShapeDtypeStruct((TS, C_KV), BF16),
                   jax.ShapeDtypeStruct((T, C_KV), BF16),
                   jax.ShapeDtypeStruct((BATCH, C_KV, SEQ), F32), jax.ShapeDtypeStruct((BATCH, C_KV, SEQ), F32)],
        compiler_params=_params(1),
        name="c_qkv",
    )(x, mod, mod, w, wkvt, cos, sin)


def _rope_tables():
    t = np.arange(DEC_SEQ)
    pos = np.stack([t // GRID_W, t % GRID_W], axis=1).astype(np.float32)
    inv = (ROPE_BASE ** (-np.arange(ROPE_FREQS, dtype=np.float32) / ROPE_FREQS)).astype(np.float32)
    d = np.arange(C_HEAD_DIM)
    axis = d // (2 * ROPE_FREQS)
    ang = pos[:, axis] * inv[d % ROPE_FREQS][None, :]
    sign = np.where((d % (2 * ROPE_FREQS)) < ROPE_FREQS, -1.0, 1.0)[None, :]
    cos_h, sin_h = np.cos(ang), np.sin(ang) * sign
    reps = LANES // C_HEAD_DIM
    cos = np.tile(np.tile(cos_h, (1, reps)), (DEC_BATCH, 1)).astype(np.float32)
    sin = np.tile(np.tile(sin_h, (1, reps)), (DEC_BATCH, 1)).astype(np.float32)
    return jnp.asarray(cos), jnp.asarray(sin)


def _dup_head(blk, half):
    lane = lax.broadcasted_iota(jnp.int32, blk.shape, 1)
    keep = (lane >= HALF).astype(jnp.int32) == half
    return jnp.where(keep, blk, pltpu.roll(blk, HALF, axis=1))


def _pair_values(v, half):
    lo = lax.broadcasted_iota(jnp.int32, v.shape, 1) < HALF
    v2 = _dup_head(v.astype(F32), half)
    top = jnp.concatenate([jnp.where(lo, v2, 0.0), jnp.where(lo, 1.0, 0.0)], axis=1)
    bot = jnp.concatenate([jnp.where(lo, 0.0, v2), jnp.where(lo, 0.0, 1.0)], axis=1)
    return top.astype(BF16), bot.astype(BF16)


def _pair_softmax_out(scores_a, scores_b, values_a, values_b, sink_a, sink_b):
    lo = lax.broadcasted_iota(jnp.int32, (1, LANES), 1) < HALF
    od, sink_terms = None, []
    for scores, values, sink in ((scores_a, values_a, sink_a), (scores_b, values_b, sink_b)):
        mx = sink
        for s in scores:
            mx = jnp.maximum(mx, jnp.max(s, axis=-1, keepdims=True))
        for s, v in zip(scores, values):
            part = _dot(jnp.exp(s - mx).astype(BF16), v)
            od = part if od is None else od + part
        sink_terms.append(jnp.exp(sink - mx))
    den = od[:, LANES:] + jnp.where(lo, sink_terms[0], sink_terms[1])
    return od[:, :LANES] / den


def _pair_sinks(sink_ref, jb):
    return (sink_ref[:, jb * LANES:jb * LANES + 1], sink_ref[:, jb * LANES + HALF:jb * LANES + HALF + 1])


def _ctx_attn_body(q_ref, kt_ref, v_ref, sink_ref, o_ref):
    half = pl.program_id(1) % 2
    kt = kt_ref[pl.ds(pl.multiple_of(half * C_HEAD_DIM, C_HEAD_DIM), C_HEAD_DIM), :]
    zero = jnp.zeros_like(kt)
    kt_a = jnp.concatenate([kt, zero], axis=0).astype(BF16)
    kt_b = jnp.concatenate([zero, kt], axis=0).astype(BF16)
    v_top, v_bot = _pair_values(v_ref[...], half)
    for jb in range(C_GROUPS // 2):
        q = q_ref[:, jb * LANES:(jb + 1) * LANES]
        sink_a, sink_b = _pair_sinks(sink_ref, jb)
        o = _pair_softmax_out([_dot(q, kt_a)], [_dot(q, kt_b)], [v_top], [v_bot], sink_a, sink_b)
        o_ref[:, jb * LANES:(jb + 1) * LANES] = o.astype(BF16)


def _ctx_attn(q, kt, v, sink_cols):
    gw = C_GROUPS * C_HEAD_DIM
    return pl.pallas_call(
        _ctx_attn_body,
        grid=(BATCH, C_KV_HEADS),
        in_specs=[
            pl.BlockSpec((SEQ, gw), lambda b, h: (b, h)),
            pl.BlockSpec((None, LANES, SEQ), lambda b, h: (b, h // 2, 0)),
            pl.BlockSpec((SEQ, LANES), lambda b, h: (b, h // 2)),
            pl.BlockSpec((1, gw), lambda b, h: (0, h)),
        ],
        out_specs=pl.BlockSpec((SEQ, gw), lambda b, h: (b, h)),
        out_shape=jax.ShapeDtypeStruct((TP, C_Q), BF16),
        compiler_params=_params(2),
        name="c_attn_ctx",
    )(q, kt, v, sink_cols)


LAT_TQ = 128
LAT_WIN = LAT_TQ + 2 * WINDOW


def _lat_attn_body(q_ref, k_ref, v_ref, ck_ref, cv_ref, sink_ref, o_ref,
                   ka_s, kb_s, va_s, vb_s, cka_s, ckb_s, cva_s, cvb_s):
    half = pl.program_id(1) % 2
    j = pl.program_id(2)

    @pl.when(j == 0)
    def _():
        for src, a_s, b_s in ((k_ref, ka_s, kb_s), (ck_ref, cka_s, ckb_s)):
            lo = lax.broadcasted_iota(jnp.int32, src.shape, 1) < HALF
            k2 = _dup_head(src[...].astype(F32), half)
            a_s[...] = jnp.where(lo, k2, 0.0).astype(BF16)
            b_s[...] = jnp.where(lo, 0.0, k2).astype(BF16)
        va_s[...], vb_s[...] = _pair_values(v_ref[...], half)
        cva_s[...], cvb_s[...] = _pair_values(cv_ref[...], half)

    start = pl.multiple_of(jnp.clip(j * LAT_TQ - WINDOW, 0, DEC_SEQ - LAT_WIN), LANES)
    win = pl.ds(start, LAT_WIN)
    qpos = j * LAT_TQ + lax.broadcasted_iota(jnp.int32, (LAT_TQ, LAT_WIN), 0)
    kpos = start + lax.broadcasted_iota(jnp.int32, (LAT_TQ, LAT_WIN), 1)
    band = jnp.abs(qpos - kpos) <= WINDOW
    for jb in range(C_GROUPS // 2):
        q = q_ref[:, jb * LANES:(jb + 1) * LANES]
        sink_a, sink_b = _pair_sinks(sink_ref, jb)
        scores = [[jnp.where(band, _dot_nt(q, kl[win, :]), -jnp.inf), _dot_nt(q, kc[...])]
                  for kl, kc in ((ka_s, cka_s), (kb_s, ckb_s))]
        o = _pair_softmax_out(scores[0], scores[1], [va_s[win, :], cva_s[...]], [vb_s[win, :], cvb_s[...]],
                              sink_a, sink_b)
        o_ref[:, jb * LANES:(jb + 1) * LANES] = o.astype(BF16)


def _lat_attn(q, k, v, ck, cv, sink_cols):
    gw = C_GROUPS * C_HEAD_DIM
    nq = DEC_SEQ // LAT_TQ
    q_off = TP // LAT_TQ
    kv_off = TP // DEC_SEQ
    return pl.pallas_call(
        _lat_attn_body,
        grid=(DEC_BATCH, C_KV_HEADS, nq),
        in_specs=[
            pl.BlockSpec((LAT_TQ, gw), lambda b, h, j: (q_off + b * nq + j, h)),
            pl.BlockSpec((DEC_SEQ, LANES), lambda b, h, j: (b, h // 2)),
            pl.BlockSpec((DEC_SEQ, LANES), lambda b, h, j: (kv_off + b, h // 2)),
            pl.BlockSpec((PAST_LEN, LANES), lambda b, h, j: (b, h // 2)),
            pl.BlockSpec((PAST_LEN, LANES), lambda b, h, j: (b, h // 2)),
            pl.BlockSpec((1, gw), lambda b, h, j: (0, h)),
        ],
        out_specs=pl.BlockSpec((LAT_TQ, gw), lambda b, h, j: (b * nq + j, h)),
        out_shape=jax.ShapeDtypeStruct((TS, C_Q), BF16),
        scratch_shapes=[
            pltpu.VMEM((DEC_SEQ, LANES), BF16), pltpu.VMEM((DEC_SEQ, LANES), BF16),
            pltpu.VMEM((DEC_SEQ, 2 * LANES), BF16), pltpu.VMEM((DEC_SEQ, 2 * LANES), BF16),
            pltpu.VMEM((PAST_LEN, LANES), BF16), pltpu.VMEM((PAST_LEN, LANES), BF16),
            pltpu.VMEM((PAST_LEN, 2 * LANES), BF16), pltpu.VMEM((PAST_LEN, 2 * LANES), BF16),
        ],
        compiler_params=_params(3),
        name="c_attn_latent",
    )(q, k, v, ck, cv, sink_cols)


def kernel(x_prompt, x_sample, state_hgrn, state_gla, cache_k, cache_v, c, c_ctx, w_mod, b_mod, ln_g, ln_b,
           ffn_w1, ffn_w3, ffn_w2, w_in_ab, hgrn_lb, gla_gate_up, gla_gate_b, norm_a, norm_b, w_out_ab,
           w_qkv_c, sink_c, w_out_c):
    cs = jnp.zeros((8, D), F32).at[0].set(c_ctx).at[1:1 + DEC_BATCH].set(c)
    mod = _mod_vectors(cs, w_mod, b_mod).reshape(DEPTH, 8, 1, N_MOD * D)
    w1, w3, w2 = ffn_w1.astype(BF16), ffn_w3.astype(BF16), ffn_w2.astype(BF16)
    ln_g, ln_b = ln_g.reshape(DEPTH, 3, 1, D), ln_b.reshape(DEPTH, 3, 1, D)

    def ffn(xs, layer, sub, split_out=False):
        return _ffn_sublayer(xs, mod, w1, w3, w2, ln_g, ln_b, layer, sub, split_out=split_out)

    (x,) = ffn([x_prompt.reshape(TP, D), x_sample.reshape(TS, D)], 0, 0)
    w_in = w_in_ab[0]
    o_aq, o_ai, o_ff, o_fb, o_ag = 0, A_W, 2 * A_W, 3 * A_W, 4 * A_W
    o_bq = 5 * A_W
    o_bk, o_bv = o_bq + B_QK, o_bq + 2 * B_QK
    o_bg = o_bv + B_V
    o_z = o_bg + B_V
    order = [(o_aq, A_W), (o_ff, A_W), (o_fb, A_W), (o_ag, A_W), (o_bq, B_QK), (o_bk, B_QK), (o_bg, B_V),
             (o_ai, A_W), (o_bv, B_V)]
    wmain = jnp.concatenate([w_in[:, o:o + w] for o, w in order], axis=1).astype(BF16)
    wz = jnp.pad(w_in[:, o_z:o_z + 2 * GATE_RANK], ((0, 0), (0, LANES - 2 * GATE_RANK))).astype(BF16)
    gup = jnp.zeros((LANES, 2 * B_QK), F32)
    gup = gup.at[:GATE_RANK, :B_QK].set(gla_gate_up[0, 0]).at[GATE_RANK:2 * GATE_RANK, B_QK:].set(gla_gate_up[0, 1])
    gb = gla_gate_b[0].reshape(1, 2 * B_QK)
    pf, pb = _inproj(x, mod, wmain, wz, gup.astype(BF16), gb, hgrn_lb, 0, 0)

    s0_a = state_hgrn[:, 0]
    s0_b = state_gla[:, 0].reshape(DEC_BATCH, 2, B_HEADS // 2, LANES, B_DV)
    oa_p, st_a = _scan(pf, pb, norm_a[0], None, prompt=True, pair=False)
    ob_p, st_b = _scan(pf, pb, norm_b[0], None, prompt=True, pair=True)
    (oa_s,) = _scan(pf, pb, norm_a[0], s0_a, prompt=False, pair=False)
    (ob_s,) = _scan(pf, pb, norm_b[0], s0_b, prompt=False, pair=True)
    w_out = w_out_ab[0].astype(BF16)
    x = _outproj([(oa_p, oa_s), (ob_p, ob_s)], [w_out[:A_W], w_out[A_W:]], x, mod, ln_g, ln_b, 0)
    (x,) = ffn([x], 0, 1)
    new_hgrn = st_a.reshape(BATCH, 1, 2, A_HEADS, A_DK, A_DV)
    new_gla = st_b.reshape(BATCH, 1, 2, B_HEADS, B_DK, B_DV)

    (x,) = ffn([x], 1, 0)
    cos, sin = _rope_tables()
    w_qkv = w_qkv_c[0].astype(BF16)
    q, k, v, kt, vt = _qkv(x, mod, w_qkv, w_qkv[:, C_Q:].T, cos, sin, 1)
    sink_cols = jnp.repeat(sink_c[0], C_HEAD_DIM).reshape(1, C_Q)
    o_p = _ctx_attn(q, kt, v, sink_cols)
    ck = cache_k[:, 0].reshape(DEC_BATCH * PAST_LEN, C_KV)
    cv = cache_v[:, 0].reshape(DEC_BATCH * PAST_LEN, C_KV)
    o_s = _lat_attn(q, k, v, ck, cv, sink_cols)
    x = _outproj([(o_p, o_s)], [w_out_c[0].astype(BF16)], x, mod, ln_g, ln_b, 1)
    y_p, y_s = ffn([x], 1, 1, split_out=True)

    def cache_layout(zt):
        return zt.reshape(BATCH, 1, C_KV_HEADS, C_HEAD_DIM, SEQ).transpose(0, 1, 4, 2, 3)

    new_k, new_v = cache_layout(kt), cache_layout(vt)

    return (y_p.reshape(BATCH, SEQ, D), y_s.reshape(DEC_BATCH, DEC_SEQ, D), new_hgrn, new_gla, new_k, new_v)
```

```python
import functools
import math

import jax
import jax.numpy as jnp
import numpy as np
from jax import lax
from jax.experimental import pallas as pl
from jax.experimental.pallas import tpu as pltpu

D = 1024
BATCH, SEQ = 16, 256
DEC_BATCH, DEC_SEQ = 2, 2048
PAST_LEN = 512
GRID_W = 64
D_FF = 2816
N_MOD = 9
A_HEADS, A_DK, A_DV = 4, 128, 128
A_W = A_HEADS * A_DK
B_HEADS, B_DK, B_DV = 4, 64, 128
B_QK = B_HEADS * B_DK
B_V = B_HEADS * B_DV
GATE_RANK = 16
GLA_TAU = 16.0
CHUNK = 128
C_HEADS, C_KV_HEADS, C_HEAD_DIM = 16, 4, 64
C_GROUPS = C_HEADS // C_KV_HEADS
C_Q = C_HEADS * C_HEAD_DIM
C_KV = C_KV_HEADS * C_HEAD_DIM
WINDOW = 128
ROPE_FREQS = C_HEAD_DIM // 4
ROPE_BASE = 10000.0
DEPTH = 2
ALPHA = (2.0 * DEPTH) ** 0.25
LN_EPS = 1e-5
RMS_EPS = 1e-6

TP = BATCH * SEQ
TS = DEC_BATCH * DEC_SEQ
T = TP + TS
N_SEG = 1 + DEC_BATCH

LANES = 128
HALF = LANES // 2
FFN_TM = 512
PROJ_TM = SEQ
OUT_TM = 512
VMEM_LIMIT = 56 * 1024 * 1024

F32 = jnp.float32
BF16 = jnp.bfloat16


def _dot(a, b):
    return jnp.dot(a, b, preferred_element_type=F32)


def _dot_nt(a, b):
    return lax.dot_general(a, b, (((1,), (1,)), ((), ())), preferred_element_type=F32)


def _dot_tn(a, b):
    return lax.dot_general(a, b, (((0,), (0,)), ((), ())), preferred_element_type=F32)


def _silu(x):
    return x * jax.nn.sigmoid(x)


def _layer_norm(z, g, b):
    mu = jnp.mean(z, axis=-1, keepdims=True)
    zc = z - mu
    var = jnp.mean(zc * zc, axis=-1, keepdims=True)
    return zc * lax.rsqrt(var + LN_EPS) * g + b


def _seg_of_tile(i, tm):
    n_p = TP // tm
    n_s = DEC_SEQ // tm
    return jnp.where(i < n_p, 0, 1 + lax.div(jnp.maximum(i - n_p, 0), n_s))


def _params(n_axes):
    return pltpu.CompilerParams(dimension_semantics=("arbitrary",) * n_axes, vmem_limit_bytes=VMEM_LIMIT)


def _resident(shape):
    nd = len(shape)
    return pl.BlockSpec(shape, lambda *_: (0,) * nd, pipeline_mode=pl.Buffered(1))


def _resident_slice(shape, lead):
    block = (None,) * len(lead) + tuple(shape)
    return pl.BlockSpec(block, lambda *_: tuple(lead) + (0,) * len(shape), pipeline_mode=pl.Buffered(1))


def _mod_specs(layer, cols, tm):
    return [pl.BlockSpec((None, None, 1, D), functools.partial(
        lambda i, c: (layer, _seg_of_tile(i, tm), 0, c), c=c)) for c in cols]


def _ln_specs(layer, idx):
    return [_resident_slice((1, D), (layer, idx))] * 2


BF16_SUBLANES = 16


def _cast_plan(ws, lead, n_steps, step_of):
    in_specs, out_specs, out_shapes = [], [], []
    for w in ws:
        rows, cols = w.shape[len(lead):]
        blk = next(b for b in range(BF16_SUBLANES, rows + 1, BF16_SUBLANES)
                   if rows % b == 0 and rows // b <= n_steps)
        last = rows // blk - 1
        in_specs.append(pl.BlockSpec((None,) * len(lead) + (blk, cols), functools.partial(
            lambda *g, last: tuple(lead) + (jnp.minimum(step_of(*g), last), 0), last=last)))
        out_specs.append(pl.BlockSpec((blk, cols), functools.partial(
            lambda *g, last: (jnp.minimum(step_of(*g), last), 0), last=last)))
        out_shapes.append(jax.ShapeDtypeStruct((rows, cols), BF16))
    return in_specs, out_specs, out_shapes


def _hosting_casts(body, n_in, n_out, n_cast):
    def hosted(*refs, **kw):
        ins, refs = refs[:n_in], refs[n_in:]
        cast_in, refs = refs[:n_cast], refs[n_cast:]
        outs, refs = refs[:n_out], refs[n_out:]
        cast_out, scratch = refs[:n_cast], refs[n_cast:]
        for src, dst in zip(cast_in, cast_out):
            dst[...] = src[...].astype(BF16)
        body(*ins, *outs, *scratch, **kw)
    return hosted


def _mod_body(c_ref, w_ref, b_ref, o_ref):
    c = c_ref[...]
    s = _silu(c).astype(BF16)
    o_ref[0] = _dot(s, w_ref[0].astype(BF16)) + b_ref[0]


def _mod_vectors(cs, w_mod, b_mod):
    tn = 1536
    n = N_MOD * D
    return pl.pallas_call(
        _mod_body,
        grid=(DEPTH, n // tn),
        in_specs=[
            pl.BlockSpec((8, D), lambda l, j: (0, 0)),
            pl.BlockSpec((1, D, tn), lambda l, j: (l, 0, j)),
            pl.BlockSpec((1, 1, tn), lambda l, j: (l, 0, j)),
        ],
        out_specs=pl.BlockSpec((1, 8, tn), lambda l, j: (l, 0, j)),
        out_shape=jax.ShapeDtypeStruct((DEPTH, 8, n), F32),
        compiler_params=_params(2),
        name="mod_vectors",
    )(cs, w_mod, b_mod.reshape(DEPTH, 1, n))


def _ffn_body(*refs, n_x, n_o, tm):
    x_refs = refs[:n_x]
    shift_ref, scale_ref, gate_ref, w1_ref, w3_ref, w2_ref, g_ref, b_ref = refs[n_x:n_x + 8]
    o_refs = refs[n_x + 8:]

    def compute(x_ref, o_ref):
        x = x_ref[...]
        shift, scale, gate = shift_ref[...], scale_ref[...], gate_ref[...]
        h = (x * (1.0 + scale) + shift).astype(BF16)
        a = _dot(h, w1_ref[...])
        b = _dot(h, w3_ref[...])
        g = (_silu(a) * b).astype(BF16)
        y = _dot(g, w2_ref[...])
        z = ALPHA * x + (0.5 * gate) * y
        o_ref[...] = _layer_norm(z, g_ref[...], b_ref[...])

    if n_x == 1 and n_o == 1:
        compute(x_refs[0], o_refs[0])
    else:
        in_prompt = pl.program_id(0) < TP // tm
        pl.when(in_prompt)(lambda: compute(x_refs[0], o_refs[0]))
        pl.when(jnp.logical_not(in_prompt))(lambda: compute(x_refs[-1], o_refs[-1]))


def _group_specs(split, tm, width=D):
    if not split:
        return [pl.BlockSpec((tm, width), lambda i: (i, 0))]
    n_p = TP // tm
    return [pl.BlockSpec((tm, width), lambda i: (jnp.minimum(i, n_p - 1), 0)),
            pl.BlockSpec((tm, width), lambda i: (jnp.maximum(i - n_p, 0), 0))]


def _ffn_sublayer(xs, mod, w1, w3, w2, ln_g, ln_b, layer, sub, split_out=False, tm=FFN_TM):
    n_x, n_o = len(xs), 2 if split_out else 1
    out_shape = ([jax.ShapeDtypeStruct((TP, D), F32), jax.ShapeDtypeStruct((TS, D), F32)] if split_out
                 else [jax.ShapeDtypeStruct((T, D), F32)])
    mod_lo = 6 * sub
    return pl.pallas_call(
        functools.partial(_ffn_body, n_x=n_x, n_o=n_o, tm=tm),
        grid=(T // tm,),
        in_specs=_group_specs(n_x == 2, tm) + _mod_specs(layer, (mod_lo, mod_lo + 1, mod_lo + 2), tm) + [
            _resident((D, D_FF)),
            _resident((D, D_FF)),
            _resident((D_FF, D)),
        ] + _ln_specs(layer, 2 * sub),
        out_specs=_group_specs(split_out, tm),
        out_shape=out_shape,
        compiler_params=_params(1),
        name="ffn_sublayer",
    )(*xs, mod, mod, mod, w1, w3, w2, ln_g, ln_b)


PF_AQ, PF_FF, PF_FB, PF_AG = 0, 512, 1024, 1536
PF_BQ, PF_BK, PF_BG, PF_LAF, PF_LAB = 2048, 2304, 2560, 3072, 3328
PF_W = 3584
PB_AV, PB_BV = 0, 512
PB_W = 1024
WM_AQ, WM_FF, WM_FB, WM_AG, WM_BQ, WM_BK, WM_BG, WM_AI, WM_BV = 0, 512, 1024, 1536, 2048, 2304, 2560, 3072, 3584
WM_W = 4096


def _log_sigmoid(x):
    return jnp.minimum(x, 0.0) - jnp.log(1.0 + jnp.exp(-jnp.abs(x)))


def _inproj_body(x_ref, shift_ref, scale_ref, w_ref, wz_ref, gu_ref, gb_ref, lb_ref, pf_ref, pb_ref, *, layer_e):
    h = (x_ref[...] * (1.0 + scale_ref[...]) + shift_ref[...]).astype(BF16)

    def proj(off, width):
        return _dot(h, w_ref[:, off:off + width])

    def lower_bound(d):
        l = lb_ref[d]
        e = jnp.exp(l - jnp.max(l, axis=0, keepdims=True))
        sm = e / jnp.sum(e, axis=0, keepdims=True)
        return jnp.sum(sm[:layer_e + 1], axis=0, keepdims=True)

    pf_ref[:, PF_AQ:PF_AQ + A_W] = proj(WM_AQ, A_W)
    for d, (wm, pf) in enumerate(((WM_FF, PF_FF), (WM_FB, PF_FB))):
        lb = lower_bound(d)
        pf_ref[:, pf:pf + A_W] = lb + (1.0 - lb) * jax.nn.sigmoid(proj(wm, A_W))
    pf_ref[:, PF_AG:PF_AG + A_W] = _silu(proj(WM_AG, A_W))
    pf_ref[:, PF_BQ:PF_BQ + B_QK] = proj(WM_BQ, B_QK) * (B_DK ** -0.5)
    pf_ref[:, PF_BK:PF_BK + B_QK] = proj(WM_BK, B_QK)
    pf_ref[:, PF_BG:PF_BG + B_V] = _silu(proj(WM_BG, B_V))
    pb_ref[:, PB_AV:PB_AV + A_W] = _silu(proj(WM_AI, A_W)).astype(BF16)
    pb_ref[:, PB_BV:PB_BV + B_V] = proj(WM_BV, B_V).astype(BF16)
    z = _dot(h, wz_ref[...]).astype(BF16)
    pre = _dot(z, gu_ref[...]) + gb_ref[...]
    pf_ref[:, PF_LAF:PF_LAF + 2 * B_QK] = _log_sigmoid(pre) * (1.0 / GLA_TAU)


def _inproj(x, mod, wmain, wz, gup, gb, hgrn_lb, layer, layer_e, tm=PROJ_TM):
    n_l = hgrn_lb.shape[1]
    return pl.pallas_call(
        functools.partial(_inproj_body, layer_e=layer_e),
        grid=(T // tm,),
        in_specs=[pl.BlockSpec((tm, D), lambda i: (i, 0))] + _mod_specs(layer, (3, 4), tm) + [
            _resident((D, WM_W)),
            _resident((D, LANES)),
            _resident((LANES, 2 * B_QK)),
            _resident((1, 2 * B_QK)),
            _resident((2, n_l, A_W)),
        ],
        out_specs=[
            pl.BlockSpec((tm, PF_W), lambda i: (i, 0)),
            pl.BlockSpec((tm, PB_W), lambda i: (i, 0)),
        ],
        out_shape=[jax.ShapeDtypeStruct((T, PF_W), F32), jax.ShapeDtypeStruct((T, PB_W), BF16)],
        compiler_params=_params(1),
        name="ab_inproj",
    )(x, mod, mod, wmain, wz, gup, gb, hgrn_lb)


SCAN_PROMPT_SEQS = 4
SCAN_UNROLL = 4


def _prefix_rows(x):
    row = lax.broadcasted_iota(jnp.int32, x.shape, 0)
    s = 1
    while s < x.shape[0]:
        x = x + jnp.where(row >= s, pltpu.roll(x, s, axis=0), 0.0)
        s *= 2
    return x


def _scan_body(*refs, seq_len, seqs, pair, has_s0, emit_state):
    n = seq_len // CHUNK
    n_all = seqs * n
    nh = 2 if pair else 1
    it = iter(refs)
    q_ref = next(it)
    if pair:
        k_ref, laf_ref, lab_ref = next(it), next(it), next(it)
    else:
        ff_ref, fb_ref = next(it), next(it)
    g_ref, v_ref, nw_ref = next(it), next(it), next(it)
    s0_ref = next(it) if has_s0 else None
    o_ref = next(it)
    st_ref = next(it) if emit_state else None
    qd_s, oi_s, kv_s, dec_s, sb_s = it

    row = lax.broadcasted_iota(jnp.int32, (CHUNK, CHUNK), 0)
    col = lax.broadcasted_iota(jnp.int32, (CHUNK, CHUNK), 1)
    tril = row >= col
    triu = row <= col
    lane = lax.broadcasted_iota(jnp.int32, (1, LANES), 1)
    lane2 = lax.broadcasted_iota(jnp.int32, (1, 2 * LANES), 1)
    if pair:
        masks = [lane < HALF, lane >= HALF]
        masks2 = [(lane2 % LANES) < HALF, (lane2 % LANES) >= HALF]
    else:
        masks, masks2 = [None], [None]

    def pick(mask, x):
        return x if mask is None else jnp.where(mask, x, jnp.zeros_like(x))

    def rows_of(c):
        return pl.ds(pl.multiple_of(c * CHUNK, CHUNK), CHUNK)

    def loop(body):
        if n_all <= SCAN_UNROLL:
            for c in range(n_all):
                body(c)
        else:
            def fbody(i, carry):
                for u in range(SCAN_UNROLL):
                    body(i * SCAN_UNROLL + u)
                return carry
            lax.fori_loop(0, n_all // SCAN_UNROLL, fbody, 0)

    def phase1(c):
        rows = rows_of(c)
        q = q_ref[rows, :]
        if pair:
            k_f = k_b = k_ref[rows, :]
            la_f, la_b = laf_ref[rows, :], lab_ref[rows, :]
        else:
            f_f, f_b = ff_ref[rows, :], fb_ref[rows, :]
            k_f, k_b = 1.0 - f_f, 1.0 - f_b
            la_f, la_b = jnp.log(f_f), jnp.log(f_b)
        cs = _prefix_rows(jnp.concatenate([la_f, la_b], axis=1))
        cf, cbi = cs[:, :LANES], cs[:, LANES:]
        tot_f, tot_b = cf[CHUNK - 1:CHUNK, :], cbi[CHUNK - 1:CHUNK, :]
        rb = tot_b - cbi + la_b
        ref_f, ref_b = cf[CHUNK // 2 - 1:CHUNK // 2, :], rb[CHUNK // 2:CHUNK // 2 + 1, :]
        qtf = q * jnp.exp(cf - ref_f)
        qtb = q * jnp.exp(rb - ref_b)
        ktf = k_f * jnp.exp(ref_f - cf)
        ktb = k_b * jnp.exp(ref_b - rb)
        qd = jnp.concatenate([qtf * jnp.exp(ref_f), qtb * jnp.exp(ref_b)], axis=1).astype(BF16)
        ku = jnp.concatenate([ktf * jnp.exp(tot_f - ref_f), ktb * jnp.exp(tot_b - ref_b)], axis=1).astype(BF16)
        qd_s[rows, :] = qd
        qt = jnp.concatenate([qtf, qtb], axis=0).astype(BF16)
        kt = jnp.concatenate([ktf, ktb], axis=0).astype(BF16)
        kv = None
        for hh in range(nh):
            v = v_ref[rows, hh * LANES:(hh + 1) * LANES]
            sc = _dot_nt(pick(masks[hh], qt), kt)
            att = jnp.where(tril, sc[:CHUNK, :CHUNK], 0.0) + jnp.where(triu, sc[CHUNK:, CHUNK:], 0.0)
            oi_s[rows, hh * LANES:(hh + 1) * LANES] = _dot(att.astype(BF16), v)
            kv_h = _dot_tn(v, ku)
            kv = kv_h if kv is None else jnp.where(masks2[0], kv, kv_h)
        kv_s[c] = kv
        dec_s[c] = jnp.exp(jnp.concatenate([tot_f, tot_b], axis=1))

    loop(phase1)

    def recurrence(sq, d, reverse):
        cols = slice(d * LANES, (d + 1) * LANES)
        c0 = sq * n
        st0 = s0_ref[sq, d, 0].T if has_s0 else jnp.zeros((LANES, LANES), F32)

        def step(c, st):
            sb_s[c, :, cols] = st.astype(BF16)
            return st * dec_s[c, :, cols] + kv_s[c, :, cols]

        if n <= 8:
            st = st0
            for c in (range(n - 1, -1, -1) if reverse else range(n)):
                st = step(c0 + c, st)
        else:
            st = lax.fori_loop(0, n, lambda i, st: step(c0 + (n - 1 - i if reverse else i), st), st0)
        if emit_state:
            st_ref[sq, d, 0] = st.T

    for sq in range(seqs):
        recurrence(sq, 0, False)
        recurrence(sq, 1, True)

    nw = nw_ref[...]

    def phase2(c):
        rows = rows_of(c)
        qcat = qd_s[rows, :]
        scat = sb_s[c]
        for hh in range(nh):
            cols = slice(hh * LANES, (hh + 1) * LANES)
            o = oi_s[rows, cols] + _dot_nt(pick(masks2[hh], qcat), scat)
            o = o * lax.rsqrt(jnp.mean(o * o, axis=-1, keepdims=True) + RMS_EPS) * nw
            o_ref[rows, cols] = (o * g_ref[rows, cols]).astype(BF16)

    loop(phase2)


def _scan(pf, pb, norm_w, s0, *, prompt, pair, casts=None):
    seq_len = SEQ if prompt else DEC_SEQ
    nseq = BATCH if prompt else DEC_BATCH
    seqs = SCAN_PROMPT_SEQS if prompt else 1
    rows = seqs * seq_len
    row_off = 0 if prompt else TP // rows
    units = B_HEADS // 2 if pair else A_HEADS
    nh = 2 if pair else 1
    n_all = rows // CHUNK
    has_s0 = s0 is not None
    emit_state = prompt

    def colspec(off, width=LANES):
        base = off // width
        return pl.BlockSpec((rows, width), lambda s, u: (s + row_off, base + u))

    if pair:
        in_specs = [colspec(PF_BQ), colspec(PF_BK), colspec(PF_LAF), colspec(PF_LAB),
                    colspec(PF_BG, 2 * LANES), colspec(PB_BV, 2 * LANES)]
        args = [pf, pf, pf, pf, pf, pb]
    else:
        in_specs = [colspec(PF_AQ), colspec(PF_FF), colspec(PF_FB), colspec(PF_AG), colspec(PB_AV)]
        args = [pf, pf, pf, pf, pb]
    in_specs.append(pl.BlockSpec((1, LANES), lambda s, u: (0, 0)))
    args.append(norm_w.reshape(1, LANES))
    state_spec = pl.BlockSpec((seqs, 2, 1, LANES, LANES), lambda s, u: (s, 0, u, 0, 0))
    if has_s0:
        in_specs.append(state_spec)
        args.append(s0)
    out_specs = [pl.BlockSpec((rows, nh * LANES), lambda s, u: (s, u))]
    out_shape = [jax.ShapeDtypeStruct((nseq * seq_len, units * nh * LANES), BF16)]
    if emit_state:
        out_specs.append(state_spec)
        out_shape.append(jax.ShapeDtypeStruct((nseq, 2, units, LANES, LANES), F32))
    scratch = [
        pltpu.VMEM((rows, 2 * LANES), BF16),
        pltpu.VMEM((rows, nh * LANES), F32),
        pltpu.VMEM((n_all, LANES, 2 * LANES), F32),
        pltpu.VMEM((n_all, 1, 2 * LANES), F32),
        pltpu.VMEM((n_all, LANES, 2 * LANES), BF16),
    ]
    body = functools.partial(_scan_body, seq_len=seq_len, seqs=seqs, pair=pair, has_s0=has_s0,
                             emit_state=emit_state)
    grid = (nseq // seqs, units)
    if casts is not None:
        ws, lead = casts
        c_in, c_out, c_shapes = _cast_plan(ws, lead, grid[0] * grid[1], lambda s, u: s * units + u)
        body = _hosting_casts(body, len(in_specs), len(out_specs), len(ws))
        in_specs, out_specs, out_shape = in_specs + c_in, out_specs + c_out, out_shape + c_shapes
        args = args + list(ws)
    return pl.pallas_call(
        body,
        grid=grid,
        in_specs=in_specs,
        out_specs=out_specs,
        out_shape=out_shape,
        scratch_shapes=scratch,
        compiler_params=_params(2),
        name=f"scan_{'p' if prompt else 's'}_{'gla' if pair else 'hgrn'}",
    )(*args)


def _outproj_body(*refs, n_lhs, tm):
    lhs = refs[:2 * n_lhs]
    ws = refs[2 * n_lhs:3 * n_lhs]
    x_ref, m_ref, g_ref, b_ref, o_ref = refs[3 * n_lhs:]

    def compute(group):
        y = _dot(lhs[group][...], ws[0][...])
        for j in range(1, n_lhs):
            y = y + _dot(lhs[2 * j + group][...], ws[j][...])
        z = ALPHA * x_ref[...] + m_ref[...] * y
        o_ref[...] = _layer_norm(z, g_ref[...], b_ref[...])

    in_prompt = pl.program_id(0) < TP // tm
    pl.when(in_prompt)(lambda: compute(0))
    pl.when(jnp.logical_not(in_prompt))(lambda: compute(1))


def _outproj(lhs, ws, x, mod, ln_g, ln_b, layer, tm=OUT_TM):
    n_lhs = len(lhs)
    in_specs, args = [], []
    for a_p, a_s in lhs:
        in_specs += _group_specs(True, tm, a_p.shape[1])
        args += [a_p, a_s]
    in_specs += [_resident(w.shape) for w in ws]
    in_specs += [pl.BlockSpec((tm, D), lambda i: (i, 0))] + _mod_specs(layer, (5,), tm) + _ln_specs(layer, 1)
    return pl.pallas_call(
        functools.partial(_outproj_body, n_lhs=n_lhs, tm=tm),
        grid=(T // tm,),
        in_specs=in_specs,
        out_specs=pl.BlockSpec((tm, D), lambda i: (i, 0)),
        out_shape=jax.ShapeDtypeStruct((T, D), F32),
        compiler_params=_params(1),
        name="mixer_outproj",
    )(*args, *ws, x, mod, ln_g, ln_b)


def _rope_partner(x):
    lane = lax.broadcasted_iota(jnp.int32, x.shape, 1)
    first_half = (lane % (2 * ROPE_FREQS)) < ROPE_FREQS
    return jnp.where(first_half, pltpu.roll(x, LANES - ROPE_FREQS, axis=1), pltpu.roll(x, ROPE_FREQS, axis=1))


def _qkv_body(x_ref, shift_ref, scale_ref, w_ref, wkvt_ref, cos_ref, sin_ref,
              q_ref, k_ref, v_ref, kt_ref, vt_ref):
    h = (x_ref[...] * (1.0 + scale_ref[...]) + shift_ref[...]).astype(BF16)
    qscale = C_HEAD_DIM ** -0.5
    v_ref[...] = _dot(h, w_ref[:, C_Q + C_KV:]).astype(BF16)
    in_prompt = pl.program_id(0) < TP // PROJ_TM

    @pl.when(in_prompt)
    def _():
        q_ref[...] = (_dot(h, w_ref[:, :C_Q]) * qscale).astype(BF16)
        kt_ref[...] = _dot_nt(wkvt_ref[:C_KV, :], h)
        vt_ref[...] = _dot_nt(wkvt_ref[C_KV:, :], h)

    @pl.when(jnp.logical_not(in_prompt))
    def _():
        cos, sin = cos_ref[...], sin_ref[...]
        zq = _dot(h, w_ref[:, :C_Q])
        zk = _dot(h, w_ref[:, C_Q:C_Q + C_KV])

        def rope(z):
            return z * cos + _rope_partner(z) * sin

        for j in range(C_Q // LANES):
            cols = slice(j * LANES, (j + 1) * LANES)
            q_ref[:, cols] = (rope(zq[:, cols]) * qscale).astype(BF16)
        for j in range(C_KV // LANES):
            cols = slice(j * LANES, (j + 1) * LANES)
            k_ref[:, cols] = rope(zk[:, cols]).astype(BF16)


def _qkv(x, mod, w, wkvt, cos, sin, layer):
    tm = PROJ_TM
    n_p = TP // tm
    lat = lambda i: (jnp.maximum(i - n_p, 0), 0)
    ctx = lambda i: (jnp.minimum(i, n_p - 1), 0, 0)
    return pl.pallas_call(
        _qkv_body,
        grid=(T // tm,),
        in_specs=[pl.BlockSpec((tm, D), lambda i: (i, 0))] + _mod_specs(layer, (3, 4), tm) + [
            _resident((D, C_Q + 2 * C_KV)),
            _resident((2 * C_KV, D)),
            pl.BlockSpec((tm, LANES), lat),
            pl.BlockSpec((tm, LANES), lat),
        ],
        out_specs=[
            pl.BlockSpec((tm, C_Q), lambda i: (i, 0)),
            pl.BlockSpec((tm, C_KV), lat),
            pl.BlockSpec((tm, C_KV), lambda i: (i, 0)),
            pl.BlockSpec((None, C_KV, SEQ), ctx),
            pl.BlockSpec((None, C_KV, SEQ), ctx),
        ],
        out_shape=[jax.ShapeDtypeStruct((T, C_Q), BF16), jax.ShapeDtypeStruct((TS, C_KV), BF16),
                   jax.ShapeDtypeStruct((T, C_KV), BF16),
                   jax.ShapeDtypeStruct((BATCH, C_KV, SEQ), F32), jax.ShapeDtypeStruct((BATCH, C_KV, SEQ), F32)],
        compiler_params=_params(1),
        name="c_qkv",
    )(x, mod, mod, w, wkvt, cos, sin)


def _rope_tables():
    t = np.arange(DEC_SEQ)
    pos = np.stack([t // GRID_W, t % GRID_W], axis=1).astype(np.float32)
    inv = (ROPE_BASE ** (-np.arange(ROPE_FREQS, dtype=np.float32) / ROPE_FREQS)).astype(np.float32)
    d = np.arange(C_HEAD_DIM)
    axis = d // (2 * ROPE_FREQS)
    ang = pos[:, axis] * inv[d % ROPE_FREQS][None, :]
    sign = np.where((d % (2 * ROPE_FREQS)) < ROPE_FREQS, -1.0, 1.0)[None, :]
    cos_h, sin_h = np.cos(ang), np.sin(ang) * sign
    reps = LANES // C_HEAD_DIM
    cos = np.tile(np.tile(cos_h, (1, reps)), (DEC_BATCH, 1)).astype(np.float32)
    sin = np.tile(np.tile(sin_h, (1, reps)), (DEC_BATCH, 1)).astype(np.float32)
    return jnp.asarray(cos), jnp.asarray(sin)


def _dup_head(blk, half):
    lane = lax.broadcasted_iota(jnp.int32, blk.shape, 1)
    keep = (lane >= HALF).astype(jnp.int32) == half
    return jnp.where(keep, blk, pltpu.roll(blk, HALF, axis=1))


def _pair_values(v, half):
    lo = lax.broadcasted_iota(jnp.int32, v.shape, 1) < HALF
    v2 = _dup_head(v.astype(F32), half)
    top = jnp.concatenate([jnp.where(lo, v2, 0.0), jnp.where(lo, 1.0, 0.0)], axis=1)
    bot = jnp.concatenate([jnp.where(lo, 0.0, v2), jnp.where(lo, 0.0, 1.0)], axis=1)
    return top.astype(BF16), bot.astype(BF16)


def _pair_softmax_out(scores_a, scores_b, values_a, values_b, sink_a, sink_b):
    lo = lax.broadcasted_iota(jnp.int32, (1, LANES), 1) < HALF
    od, sink_terms = None, []
    for scores, values, sink in ((scores_a, values_a, sink_a), (scores_b, values_b, sink_b)):
        mx = sink
        for s in scores:
            mx = jnp.maximum(mx, jnp.max(s, axis=-1, keepdims=True))
        for s, v in zip(scores, values):
            part = _dot(jnp.exp(s - mx).astype(BF16), v)
            od = part if od is None else od + part
        sink_terms.append(jnp.exp(sink - mx))
    den = od[:, LANES:] + jnp.where(lo, sink_terms[0], sink_terms[1])
    return od[:, :LANES] / den


def _pair_sinks(sink_ref, jb):
    return (sink_ref[:, jb * LANES:jb * LANES + 1], sink_ref[:, jb * LANES + HALF:jb * LANES + HALF + 1])


def _ctx_attn_body(q_ref, kt_ref, v_ref, sink_ref, o_ref):
    half = pl.program_id(1) % 2
    kt = kt_ref[pl.ds(pl.multiple_of(half * C_HEAD_DIM, C_HEAD_DIM), C_HEAD_DIM), :]
    zero = jnp.zeros_like(kt)
    kt_a = jnp.concatenate([kt, zero], axis=0).astype(BF16)
    kt_b = jnp.concatenate([zero, kt], axis=0).astype(BF16)
    v_top, v_bot = _pair_values(v_ref[...], half)
    for jb in range(C_GROUPS // 2):
        q = q_ref[:, jb * LANES:(jb + 1) * LANES]
        sink_a, sink_b = _pair_sinks(sink_ref, jb)
        o = _pair_softmax_out([_dot(q, kt_a)], [_dot(q, kt_b)], [v_top], [v_bot], sink_a, sink_b)
        o_ref[:, jb * LANES:(jb + 1) * LANES] = o.astype(BF16)


def _ctx_attn(q, kt, v, sink_cols, casts=None):
    gw = C_GROUPS * C_HEAD_DIM
    body = _ctx_attn_body
    in_specs = [
        pl.BlockSpec((SEQ, gw), lambda b, h: (b, h)),
        pl.BlockSpec((None, LANES, SEQ), lambda b, h: (b, h // 2, 0)),
        pl.BlockSpec((SEQ, LANES), lambda b, h: (b, h // 2)),
        pl.BlockSpec((1, gw), lambda b, h: (0, h)),
    ]
    out_specs = [pl.BlockSpec((SEQ, gw), lambda b, h: (b, h))]
    out_shape = [jax.ShapeDtypeStruct((TP, C_Q), BF16)]
    args = [q, kt, v, sink_cols]
    if casts is not None:
        ws, lead = casts
        c_in, c_out, c_shapes = _cast_plan(ws, lead, BATCH * C_KV_HEADS, lambda b, h: b * C_KV_HEADS + h)
        body = _hosting_casts(body, len(in_specs), len(out_specs), len(ws))
        in_specs, out_specs, out_shape = in_specs + c_in, out_specs + c_out, out_shape + c_shapes
        args = args + list(ws)
    return pl.pallas_call(
        body,
        grid=(BATCH, C_KV_HEADS),
        in_specs=in_specs,
        out_specs=out_specs,
        out_shape=out_shape,
        compiler_params=_params(2),
        name="c_attn_ctx",
    )(*args)


LAT_TQ = 128
LAT_WIN = LAT_TQ + 2 * WINDOW


def _lat_attn_body(q_ref, k_ref, v_ref, ck_ref, cv_ref, sink_ref, o_ref,
                   ka_s, kb_s, va_s, vb_s, cka_s, ckb_s, cva_s, cvb_s):
    half = pl.program_id(1) % 2
    j = pl.program_id(2)

    @pl.when(j == 0)
    def _():
        for src, a_s, b_s in ((k_ref, ka_s, kb_s), (ck_ref, cka_s, ckb_s)):
            lo = lax.broadcasted_iota(jnp.int32, src.shape, 1) < HALF
            k2 = _dup_head(src[...].astype(F32), half)
            a_s[...] = jnp.where(lo, k2, 0.0).astype(BF16)
            b_s[...] = jnp.where(lo, 0.0, k2).astype(BF16)
        va_s[...], vb_s[...] = _pair_values(v_ref[...], half)
        cva_s[...], cvb_s[...] = _pair_values(cv_ref[...], half)

    start = pl.multiple_of(jnp.clip(j * LAT_TQ - WINDOW, 0, DEC_SEQ - LAT_WIN), LANES)
    win = pl.ds(start, LAT_WIN)
    qpos = j * LAT_TQ + lax.broadcasted_iota(jnp.int32, (LAT_TQ, LAT_WIN), 0)
    kpos = start + lax.broadcasted_iota(jnp.int32, (LAT_TQ, LAT_WIN), 1)
    band = jnp.abs(qpos - kpos) <= WINDOW
    for jb in range(C_GROUPS // 2):
        q = q_ref[:, jb * LANES:(jb + 1) * LANES]
        sink_a, sink_b = _pair_sinks(sink_ref, jb)
        scores = [[jnp.where(band, _dot_nt(q, kl[win, :]), -jnp.inf), _dot_nt(q, kc[...])]
                  for kl, kc in ((ka_s, cka_s), (kb_s, ckb_s))]
        o = _pair_softmax_out(scores[0], scores[1], [va_s[win, :], cva_s[...]], [vb_s[win, :], cvb_s[...]],
                              sink_a, sink_b)
        o_ref[:, jb * LANES:(jb + 1) * LANES] = o.astype(BF16)


def _lat_attn(q, k, v, ck, cv, sink_cols):
    gw = C_GROUPS * C_HEAD_DIM
    nq = DEC_SEQ // LAT_TQ
    q_off = TP // LAT_TQ
    kv_off = TP // DEC_SEQ
    return pl.pallas_call(
        _lat_attn_body,
        grid=(DEC_BATCH, C_KV_HEADS, nq),
        in_specs=[
            pl.BlockSpec((LAT_TQ, gw), lambda b, h, j: (q_off + b * nq + j, h)),
            pl.BlockSpec((DEC_SEQ, LANES), lambda b, h, j: (b, h // 2)),
            pl.BlockSpec((DEC_SEQ, LANES), lambda b, h, j: (kv_off + b, h // 2)),
            pl.BlockSpec((PAST_LEN, LANES), lambda b, h, j: (b, h // 2)),
            pl.BlockSpec((PAST_LEN, LANES), lambda b, h, j: (b, h // 2)),
            pl.BlockSpec((1, gw), lambda b, h, j: (0, h)),
        ],
        out_specs=pl.BlockSpec((LAT_TQ, gw), lambda b, h, j: (b * nq + j, h)),
        out_shape=jax.ShapeDtypeStruct((TS, C_Q), BF16),
        scratch_shapes=[
            pltpu.VMEM((DEC_SEQ, LANES), BF16), pltpu.VMEM((DEC_SEQ, LANES), BF16),
            pltpu.VMEM((DEC_SEQ, 2 * LANES), BF16), pltpu.VMEM((DEC_SEQ, 2 * LANES), BF16),
            pltpu.VMEM((PAST_LEN, LANES), BF16), pltpu.VMEM((PAST_LEN, LANES), BF16),
            pltpu.VMEM((PAST_LEN, 2 * LANES), BF16), pltpu.VMEM((PAST_LEN, 2 * LANES), BF16),
        ],
        compiler_params=_params(3),
        name="c_attn_latent",
    )(q, k, v, ck, cv, sink_cols)


def kernel(x_prompt, x_sample, state_hgrn, state_gla, cache_k, cache_v, c, c_ctx, w_mod, b_mod, ln_g, ln_b,
           ffn_w1, ffn_w3, ffn_w2, w_in_ab, hgrn_lb, gla_gate_up, gla_gate_b, norm_a, norm_b, w_out_ab,
           w_qkv_c, sink_c, w_out_c):
    cs = jnp.zeros((8, D), F32).at[0].set(c_ctx).at[1:1 + DEC_BATCH].set(c)
    mod = _mod_vectors(cs, w_mod, b_mod).reshape(DEPTH, 8, 1, N_MOD * D)
    ffn_ws = (ffn_w1, ffn_w3, ffn_w2)
    ln_g, ln_b = ln_g.reshape(DEPTH, 3, 1, D), ln_b.reshape(DEPTH, 3, 1, D)

    def ffn(xs, ws, layer, sub, split_out=False):
        return _ffn_sublayer(xs, mod, *ws, ln_g, ln_b, layer, sub, split_out=split_out)

    (x,) = ffn([x_prompt.reshape(TP, D), x_sample.reshape(TS, D)], [w[0, 0].astype(BF16) for w in ffn_ws], 0, 0)
    w_in = w_in_ab[0]
    o_aq, o_ai, o_ff, o_fb, o_ag = 0, A_W, 2 * A_W, 3 * A_W, 4 * A_W
    o_bq = 5 * A_W
    o_bk, o_bv = o_bq + B_QK, o_bq + 2 * B_QK
    o_bg = o_bv + B_V
    o_z = o_bg + B_V
    order = [(o_aq, A_W), (o_ff, A_W), (o_fb, A_W), (o_ag, A_W), (o_bq, B_QK), (o_bk, B_QK), (o_bg, B_V),
             (o_ai, A_W), (o_bv, B_V)]
    wmain = jnp.concatenate([w_in[:, o:o + w] for o, w in order], axis=1).astype(BF16)
    wz = jnp.pad(w_in[:, o_z:o_z + 2 * GATE_RANK], ((0, 0), (0, LANES - 2 * GATE_RANK))).astype(BF16)
    gup = jnp.zeros((LANES, 2 * B_QK), F32)
    gup = gup.at[:GATE_RANK, :B_QK].set(gla_gate_up[0, 0]).at[GATE_RANK:2 * GATE_RANK, B_QK:].set(gla_gate_up[0, 1])
    gb = gla_gate_b[0].reshape(1, 2 * B_QK)
    pf, pb = _inproj(x, mod, wmain, wz, gup.astype(BF16), gb, hgrn_lb, 0, 0)

    s0_a = state_hgrn[:, 0]
    s0_b = state_gla[:, 0].reshape(DEC_BATCH, 2, B_HEADS // 2, LANES, B_DV)
    oa_p, st_a, *ws_01 = _scan(pf, pb, norm_a[0], None, prompt=True, pair=False, casts=(ffn_ws, (0, 1)))
    ob_p, st_b, *ws_10 = _scan(pf, pb, norm_b[0], None, prompt=True, pair=True, casts=(ffn_ws, (1, 0)))
    (oa_s,) = _scan(pf, pb, norm_a[0], s0_a, prompt=False, pair=False)
    (ob_s,) = _scan(pf, pb, norm_b[0], s0_b, prompt=False, pair=True)
    w_out = w_out_ab[0].astype(BF16)
    x = _outproj([(oa_p, oa_s), (ob_p, ob_s)], [w_out[:A_W], w_out[A_W:]], x, mod, ln_g, ln_b, 0)
    (x,) = ffn([x], ws_01, 0, 1)
    new_hgrn = st_a.reshape(BATCH, 1, 2, A_HEADS, A_DK, A_DV)
    new_gla = st_b.reshape(BATCH, 1, 2, B_HEADS, B_DK, B_DV)

    (x,) = ffn([x], ws_10, 1, 0)
    cos, sin = _rope_tables()
    w_qkv = w_qkv_c[0].astype(BF16)
    q, k, v, kt, vt = _qkv(x, mod, w_qkv, w_qkv[:, C_Q:].T, cos, sin, 1)
    sink_cols = jnp.repeat(sink_c[0], C_HEAD_DIM).reshape(1, C_Q)
    o_p, *ws_11 = _ctx_attn(q, kt, v, sink_cols, casts=(ffn_ws, (1, 1)))
    ck = cache_k[:, 0].reshape(DEC_BATCH * PAST_LEN, C_KV)
    cv = cache_v[:, 0].reshape(DEC_BATCH * PAST_LEN, C_KV)
    o_s = _lat_attn(q, k, v, ck, cv, sink_cols)
    x = _outproj([(o_p, o_s)], [w_out_c[0].astype(BF16)], x, mod, ln_g, ln_b, 1)
    y_p, y_s = ffn([x], ws_11, 1, 1, split_out=True)

    def cache_layout(zt):
        return zt.reshape(BATCH, 1, C_KV_HEADS, C_HEAD_DIM, SEQ).transpose(0, 1, 4, 2, 3)

    new_k, new_v = cache_layout(kt), cache_layout(vt)

    return (y_p.reshape(BATCH, SEQ, D), y_s.reshape(DEC_BATCH, DEC_SEQ, D), new_hgrn, new_gla, new_k, new_v)
```

```python
import functools
import math

import jax
import jax.numpy as jnp
import numpy as np
from jax import lax
from jax.experimental import pallas as pl
from jax.experimental.pallas import tpu as pltpu

D = 1024
BATCH, SEQ = 16, 256
DEC_BATCH, DEC_SEQ = 2, 2048
PAST_LEN = 512
GRID_W = 64
D_FF = 2816
N_MOD = 9
A_HEADS, A_DK, A_DV = 4, 128, 128
A_W = A_HEADS * A_DK
B_HEADS, B_DK, B_DV = 4, 64, 128
B_QK = B_HEADS * B_DK
B_V = B_HEADS * B_DV
GATE_RANK = 16
GLA_TAU = 16.0
CHUNK = 128
C_HEADS, C_KV_HEADS, C_HEAD_DIM = 16, 4, 64
C_GROUPS = C_HEADS // C_KV_HEADS
C_Q = C_HEADS * C_HEAD_DIM
C_KV = C_KV_HEADS * C_HEAD_DIM
WINDOW = 128
ROPE_FREQS = C_HEAD_DIM // 4
ROPE_BASE = 10000.0
DEPTH = 2
ALPHA = (2.0 * DEPTH) ** 0.25
LN_EPS = 1e-5
RMS_EPS = 1e-6

TP = BATCH * SEQ
TS = DEC_BATCH * DEC_SEQ
T = TP + TS
N_SEG = 1 + DEC_BATCH

LANES = 128
HALF = LANES // 2
FFN_TM = 512
PROJ_TM = SEQ
OUT_TM = 512
VMEM_LIMIT = 56 * 1024 * 1024

F32 = jnp.float32
BF16 = jnp.bfloat16


def _dot(a, b):
    return jnp.dot(a, b, preferred_element_type=F32)


def _dot_nt(a, b):
    return lax.dot_general(a, b, (((1,), (1,)), ((), ())), preferred_element_type=F32)


def _dot_tn(a, b):
    return lax.dot_general(a, b, (((0,), (0,)), ((), ())), preferred_element_type=F32)


def _silu(x):
    return x * jax.nn.sigmoid(x)


def _layer_norm(z, g, b):
    mu = jnp.mean(z, axis=-1, keepdims=True)
    zc = z - mu
    var = jnp.mean(zc * zc, axis=-1, keepdims=True)
    return zc * lax.rsqrt(var + LN_EPS) * g + b


def _seg_of_tile(i, tm):
    n_p = TP // tm
    n_s = DEC_SEQ // tm
    return jnp.where(i < n_p, 0, 1 + lax.div(jnp.maximum(i - n_p, 0), n_s))


def _params(n_axes):
    return pltpu.CompilerParams(dimension_semantics=("arbitrary",) * n_axes, vmem_limit_bytes=VMEM_LIMIT)


def _resident(shape):
    nd = len(shape)
    return pl.BlockSpec(shape, lambda *_: (0,) * nd, pipeline_mode=pl.Buffered(1))


def _resident_slice(shape, lead):
    block = (None,) * len(lead) + tuple(shape)
    return pl.BlockSpec(block, lambda *_: tuple(lead) + (0,) * len(shape), pipeline_mode=pl.Buffered(1))


def _mod_specs(layer, cols, tm):
    return [pl.BlockSpec((None, None, 1, D), functools.partial(
        lambda i, c: (layer, _seg_of_tile(i, tm), 0, c), c=c)) for c in cols]


def _ln_specs(layer, idx):
    return [_resident_slice((1, D), (layer, idx))] * 2


BF16_SUBLANES = 16


def _cast_plan(ws, lead, n_steps, step_of):
    in_specs, out_specs, out_shapes = [], [], []
    for w in ws:
        rows, cols = w.shape[len(lead):]
        blk = next(b for b in range(BF16_SUBLANES, rows + 1, BF16_SUBLANES)
                   if rows % b == 0 and rows // b <= n_steps)
        last = rows // blk - 1
        in_specs.append(pl.BlockSpec((None,) * len(lead) + (blk, cols), functools.partial(
            lambda *g, last: tuple(lead) + (jnp.minimum(step_of(*g), last), 0), last=last)))
        out_specs.append(pl.BlockSpec((blk, cols), functools.partial(
            lambda *g, last: (jnp.minimum(step_of(*g), last), 0), last=last)))
        out_shapes.append(jax.ShapeDtypeStruct((rows, cols), BF16))
    return in_specs, out_specs, out_shapes


def _hosting_casts(body, n_in, n_out, n_cast):
    def hosted(*refs, **kw):
        ins, refs = refs[:n_in], refs[n_in:]
        cast_in, refs = refs[:n_cast], refs[n_cast:]
        outs, refs = refs[:n_out], refs[n_out:]
        cast_out, scratch = refs[:n_cast], refs[n_cast:]
        for src, dst in zip(cast_in, cast_out):
            dst[...] = src[...].astype(BF16)
        body(*ins, *outs, *scratch, **kw)
    return hosted


def _mod_body(c_ref, w_ref, b_ref, o_ref):
    c = c_ref[...]
    s = _silu(c).astype(BF16)
    o_ref[0] = _dot(s, w_ref[0].astype(BF16)) + b_ref[0]


def _mod_vectors(cs, w_mod, b_mod):
    tn = 1536
    n = N_MOD * D
    return pl.pallas_call(
        _mod_body,
        grid=(DEPTH, n // tn),
        in_specs=[
            pl.BlockSpec((8, D), lambda l, j: (0, 0)),
            pl.BlockSpec((1, D, tn), lambda l, j: (l, 0, j)),
            pl.BlockSpec((1, 1, tn), lambda l, j: (l, 0, j)),
        ],
        out_specs=pl.BlockSpec((1, 8, tn), lambda l, j: (l, 0, j)),
        out_shape=jax.ShapeDtypeStruct((DEPTH, 8, n), F32),
        compiler_params=_params(2),
        name="mod_vectors",
    )(cs, w_mod, b_mod.reshape(DEPTH, 1, n))


def _ffn_body(*refs, n_x, n_o, tm):
    x_refs = refs[:n_x]
    shift_ref, scale_ref, gate_ref, w1_ref, w3_ref, w2_ref, g_ref, b_ref = refs[n_x:n_x + 8]
    o_refs = refs[n_x + 8:]

    def compute(x_ref, o_ref):
        x = x_ref[...]
        shift, scale, gate = shift_ref[...], scale_ref[...], gate_ref[...]
        h = (x * (1.0 + scale) + shift).astype(BF16)
        a = _dot(h, w1_ref[...])
        b = _dot(h, w3_ref[...])
        g = (_silu(a) * b).astype(BF16)
        y = _dot(g, w2_ref[...])
        z = ALPHA * x + (0.5 * gate) * y
        o_ref[...] = _layer_norm(z, g_ref[...], b_ref[...])

    if n_x == 1 and n_o == 1:
        compute(x_refs[0], o_refs[0])
    else:
        in_prompt = pl.program_id(0) < TP // tm
        pl.when(in_prompt)(lambda: compute(x_refs[0], o_refs[0]))
        pl.when(jnp.logical_not(in_prompt))(lambda: compute(x_refs[-1], o_refs[-1]))


def _group_specs(split, tm, width=D):
    if not split:
        return [pl.BlockSpec((tm, width), lambda i: (i, 0))]
    n_p = TP // tm
    return [pl.BlockSpec((tm, width), lambda i: (jnp.minimum(i, n_p - 1), 0)),
            pl.BlockSpec((tm, width), lambda i: (jnp.maximum(i - n_p, 0), 0))]


def _ffn_sublayer(xs, mod, w1, w3, w2, ln_g, ln_b, layer, sub, split_out=False, casts=None, tm=FFN_TM):
    n_x, n_o = len(xs), 2 if split_out else 1
    out_shape = ([jax.ShapeDtypeStruct((TP, D), F32), jax.ShapeDtypeStruct((TS, D), F32)] if split_out
                 else [jax.ShapeDtypeStruct((T, D), F32)])
    mod_lo = 6 * sub
    body = functools.partial(_ffn_body, n_x=n_x, n_o=n_o, tm=tm)
    in_specs = _group_specs(n_x == 2, tm) + _mod_specs(layer, (mod_lo, mod_lo + 1, mod_lo + 2), tm) + [
        _resident((D, D_FF)),
        _resident((D, D_FF)),
        _resident((D_FF, D)),
    ] + _ln_specs(layer, 2 * sub)
    out_specs = _group_specs(split_out, tm)
    args = [*xs, mod, mod, mod, w1, w3, w2, ln_g, ln_b]
    if casts is not None:
        ws, lead = casts
        c_in, c_out, c_shapes = _cast_plan(ws, lead, T // tm, lambda i: i)
        body = _hosting_casts(body, len(in_specs), len(out_specs), len(ws))
        in_specs, out_specs, out_shape = in_specs + c_in, out_specs + c_out, out_shape + c_shapes
        args = args + list(ws)
    return pl.pallas_call(
        body,
        grid=(T // tm,),
        in_specs=in_specs,
        out_specs=out_specs,
        out_shape=out_shape,
        compiler_params=_params(1),
        name="ffn_sublayer",
    )(*args)


PF_AQ, PF_FF, PF_FB, PF_AG = 0, 512, 1024, 1536
PF_BQ, PF_BK, PF_BG, PF_LAF, PF_LAB = 2048, 2304, 2560, 3072, 3328
PF_W = 3584
PB_AV, PB_BV = 0, 512
PB_W = 1024
WM_AQ, WM_FF, WM_FB, WM_AG, WM_BQ, WM_BK, WM_BG, WM_AI, WM_BV = 0, 512, 1024, 1536, 2048, 2304, 2560, 3072, 3584
WM_W = 4096


def _log_sigmoid(x):
    return jnp.minimum(x, 0.0) - jnp.log(1.0 + jnp.exp(-jnp.abs(x)))


def _inproj_body(x_ref, shift_ref, scale_ref, w_ref, wz_ref, gu_ref, gb_ref, lb_ref, pf_ref, pb_ref, *, layer_e):
    h = (x_ref[...] * (1.0 + scale_ref[...]) + shift_ref[...]).astype(BF16)

    def proj(off, width):
        return _dot(h, w_ref[:, off:off + width])

    def lower_bound(d):
        l = lb_ref[d]
        e = jnp.exp(l - jnp.max(l, axis=0, keepdims=True))
        sm = e / jnp.sum(e, axis=0, keepdims=True)
        return jnp.sum(sm[:layer_e + 1], axis=0, keepdims=True)

    pf_ref[:, PF_AQ:PF_AQ + A_W] = proj(WM_AQ, A_W)
    for d, (wm, pf) in enumerate(((WM_FF, PF_FF), (WM_FB, PF_FB))):
        lb = lower_bound(d)
        pf_ref[:, pf:pf + A_W] = lb + (1.0 - lb) * jax.nn.sigmoid(proj(wm, A_W))
    pf_ref[:, PF_AG:PF_AG + A_W] = _silu(proj(WM_AG, A_W))
    pf_ref[:, PF_BQ:PF_BQ + B_QK] = proj(WM_BQ, B_QK) * (B_DK ** -0.5)
    pf_ref[:, PF_BK:PF_BK + B_QK] = proj(WM_BK, B_QK)
    pf_ref[:, PF_BG:PF_BG + B_V] = _silu(proj(WM_BG, B_V))
    pb_ref[:, PB_AV:PB_AV + A_W] = _silu(proj(WM_AI, A_W)).astype(BF16)
    pb_ref[:, PB_BV:PB_BV + B_V] = proj(WM_BV, B_V).astype(BF16)
    z = _dot(h, wz_ref[...]).astype(BF16)
    pre = _dot(z, gu_ref[...]) + gb_ref[...]
    pf_ref[:, PF_LAF:PF_LAF + 2 * B_QK] = _log_sigmoid(pre) * (1.0 / GLA_TAU)


def _inproj(x, mod, wmain, wz, gup, gb, hgrn_lb, layer, layer_e, tm=PROJ_TM):
    n_l = hgrn_lb.shape[1]
    return pl.pallas_call(
        functools.partial(_inproj_body, layer_e=layer_e),
        grid=(T // tm,),
        in_specs=[pl.BlockSpec((tm, D), lambda i: (i, 0))] + _mod_specs(layer, (3, 4), tm) + [
            _resident((D, WM_W)),
            _resident((D, LANES)),
            _resident((LANES, 2 * B_QK)),
            _resident((1, 2 * B_QK)),
            _resident((2, n_l, A_W)),
        ],
        out_specs=[
            pl.BlockSpec((tm, PF_W), lambda i: (i, 0)),
            pl.BlockSpec((tm, PB_W), lambda i: (i, 0)),
        ],
        out_shape=[jax.ShapeDtypeStruct((T, PF_W), F32), jax.ShapeDtypeStruct((T, PB_W), BF16)],
        compiler_params=_params(1),
        name="ab_inproj",
    )(x, mod, mod, wmain, wz, gup, gb, hgrn_lb)


SCAN_PROMPT_SEQS = 4
SCAN_UNROLL = 4


def _prefix_rows(x):
    row = lax.broadcasted_iota(jnp.int32, x.shape, 0)
    s = 1
    while s < x.shape[0]:
        x = x + jnp.where(row >= s, pltpu.roll(x, s, axis=0), 0.0)
        s *= 2
    return x


def _scan_body(*refs, seq_len, seqs, pair, has_s0, emit_state):
    n = seq_len // CHUNK
    n_all = seqs * n
    nh = 2 if pair else 1
    it = iter(refs)
    q_ref = next(it)
    if pair:
        k_ref, laf_ref, lab_ref = next(it), next(it), next(it)
    else:
        ff_ref, fb_ref = next(it), next(it)
    g_ref, v_ref, nw_ref = next(it), next(it), next(it)
    s0_ref = next(it) if has_s0 else None
    o_ref = next(it)
    st_ref = next(it) if emit_state else None
    qd_s, oi_s, kv_s, dec_s, sb_s = it

    row = lax.broadcasted_iota(jnp.int32, (CHUNK, CHUNK), 0)
    col = lax.broadcasted_iota(jnp.int32, (CHUNK, CHUNK), 1)
    tril = row >= col
    triu = row <= col
    lane = lax.broadcasted_iota(jnp.int32, (1, LANES), 1)
    lane2 = lax.broadcasted_iota(jnp.int32, (1, 2 * LANES), 1)
    if pair:
        masks = [lane < HALF, lane >= HALF]
        masks2 = [(lane2 % LANES) < HALF, (lane2 % LANES) >= HALF]
    else:
        masks, masks2 = [None], [None]

    def pick(mask, x):
        return x if mask is None else jnp.where(mask, x, jnp.zeros_like(x))

    def rows_of(c):
        return pl.ds(pl.multiple_of(c * CHUNK, CHUNK), CHUNK)

    def loop(body):
        if n_all <= SCAN_UNROLL:
            for c in range(n_all):
                body(c)
        else:
            def fbody(i, carry):
                for u in range(SCAN_UNROLL):
                    body(i * SCAN_UNROLL + u)
                return carry
            lax.fori_loop(0, n_all // SCAN_UNROLL, fbody, 0)

    def phase1(c):
        rows = rows_of(c)
        q = q_ref[rows, :]
        if pair:
            k_f = k_b = k_ref[rows, :]
            la_f, la_b = laf_ref[rows, :], lab_ref[rows, :]
        else:
            f_f, f_b = ff_ref[rows, :], fb_ref[rows, :]
            k_f, k_b = 1.0 - f_f, 1.0 - f_b
            la_f, la_b = jnp.log(f_f), jnp.log(f_b)
        cs = _prefix_rows(jnp.concatenate([la_f, la_b], axis=1))
        cf, cbi = cs[:, :LANES], cs[:, LANES:]
        tot_f, tot_b = cf[CHUNK - 1:CHUNK, :], cbi[CHUNK - 1:CHUNK, :]
        rb = tot_b - cbi + la_b
        ref_f, ref_b = cf[CHUNK // 2 - 1:CHUNK // 2, :], rb[CHUNK // 2:CHUNK // 2 + 1, :]
        qtf = q * jnp.exp(cf - ref_f)
        qtb = q * jnp.exp(rb - ref_b)
        ktf = k_f * jnp.exp(ref_f - cf)
        ktb = k_b * jnp.exp(ref_b - rb)
        qd = jnp.concatenate([qtf * jnp.exp(ref_f), qtb * jnp.exp(ref_b)], axis=1).astype(BF16)
        ku = jnp.concatenate([ktf * jnp.exp(tot_f - ref_f), ktb * jnp.exp(tot_b - ref_b)], axis=1).astype(BF16)
        qd_s[rows, :] = qd
        qt = jnp.concatenate([qtf, qtb], axis=0).astype(BF16)
        kt = jnp.concatenate([ktf, ktb], axis=0).astype(BF16)
        kv = None
        for hh in range(nh):
            v = v_ref[rows, hh * LANES:(hh + 1) * LANES]
            sc = _dot_nt(pick(masks[hh], qt), kt)
            att = jnp.where(tril, sc[:CHUNK, :CHUNK], 0.0) + jnp.where(triu, sc[CHUNK:, CHUNK:], 0.0)
            oi_s[rows, hh * LANES:(hh + 1) * LANES] = _dot(att.astype(BF16), v)
            kv_h = _dot_tn(v, ku)
            kv = kv_h if kv is None else jnp.where(masks2[0], kv, kv_h)
        kv_s[c] = kv
        dec_s[c] = jnp.exp(jnp.concatenate([tot_f, tot_b], axis=1))

    loop(phase1)

    def recurrence(sq, d, reverse):
        cols = slice(d * LANES, (d + 1) * LANES)
        c0 = sq * n
        st0 = s0_ref[sq, d, 0].T if has_s0 else jnp.zeros((LANES, LANES), F32)

        def step(c, st):
            sb_s[c, :, cols] = st.astype(BF16)
            return st * dec_s[c, :, cols] + kv_s[c, :, cols]

        if n <= 8:
            st = st0
            for c in (range(n - 1, -1, -1) if reverse else range(n)):
                st = step(c0 + c, st)
        else:
            st = lax.fori_loop(0, n, lambda i, st: step(c0 + (n - 1 - i if reverse else i), st), st0)
        if emit_state:
            st_ref[sq, d, 0] = st.T

    for sq in range(seqs):
        recurrence(sq, 0, False)
        recurrence(sq, 1, True)

    nw = nw_ref[...]

    def phase2(c):
        rows = rows_of(c)
        qcat = qd_s[rows, :]
        scat = sb_s[c]
        for hh in range(nh):
            cols = slice(hh * LANES, (hh + 1) * LANES)
            o = oi_s[rows, cols] + _dot_nt(pick(masks2[hh], qcat), scat)
            o = o * lax.rsqrt(jnp.mean(o * o, axis=-1, keepdims=True) + RMS_EPS) * nw
            o_ref[rows, cols] = (o * g_ref[rows, cols]).astype(BF16)

    loop(phase2)


def _scan(pf, pb, norm_w, s0, *, prompt, pair):
    seq_len = SEQ if prompt else DEC_SEQ
    nseq = BATCH if prompt else DEC_BATCH
    seqs = SCAN_PROMPT_SEQS if prompt else 1
    rows = seqs * seq_len
    row_off = 0 if prompt else TP // rows
    units = B_HEADS // 2 if pair else A_HEADS
    nh = 2 if pair else 1
    n_all = rows // CHUNK
    has_s0 = s0 is not None
    emit_state = prompt

    def colspec(off, width=LANES):
        base = off // width
        return pl.BlockSpec((rows, width), lambda s, u: (s + row_off, base + u))

    if pair:
        in_specs = [colspec(PF_BQ), colspec(PF_BK), colspec(PF_LAF), colspec(PF_LAB),
                    colspec(PF_BG, 2 * LANES), colspec(PB_BV, 2 * LANES)]
        args = [pf, pf, pf, pf, pf, pb]
    else:
        in_specs = [colspec(PF_AQ), colspec(PF_FF), colspec(PF_FB), colspec(PF_AG), colspec(PB_AV)]
        args = [pf, pf, pf, pf, pb]
    in_specs.append(pl.BlockSpec((1, LANES), lambda s, u: (0, 0)))
    args.append(norm_w.reshape(1, LANES))
    state_spec = pl.BlockSpec((seqs, 2, 1, LANES, LANES), lambda s, u: (s, 0, u, 0, 0))
    if has_s0:
        in_specs.append(state_spec)
        args.append(s0)
    out_specs = [pl.BlockSpec((rows, nh * LANES), lambda s, u: (s, u))]
    out_shape = [jax.ShapeDtypeStruct((nseq * seq_len, units * nh * LANES), BF16)]
    if emit_state:
        out_specs.append(state_spec)
        out_shape.append(jax.ShapeDtypeStruct((nseq, 2, units, LANES, LANES), F32))
    scratch = [
        pltpu.VMEM((rows, 2 * LANES), BF16),
        pltpu.VMEM((rows, nh * LANES), F32),
        pltpu.VMEM((n_all, LANES, 2 * LANES), F32),
        pltpu.VMEM((n_all, 1, 2 * LANES), F32),
        pltpu.VMEM((n_all, LANES, 2 * LANES), BF16),
    ]
    return pl.pallas_call(
        functools.partial(_scan_body, seq_len=seq_len, seqs=seqs, pair=pair, has_s0=has_s0,
                          emit_state=emit_state),
        grid=(nseq // seqs, units),
        in_specs=in_specs,
        out_specs=out_specs,
        out_shape=out_shape,
        scratch_shapes=scratch,
        compiler_params=_params(2),
        name=f"scan_{'p' if prompt else 's'}_{'gla' if pair else 'hgrn'}",
    )(*args)


def _outproj_body(*refs, n_lhs, tm):
    lhs = refs[:2 * n_lhs]
    ws = refs[2 * n_lhs:3 * n_lhs]
    x_ref, m_ref, g_ref, b_ref, o_ref = refs[3 * n_lhs:]

    def compute(group):
        y = _dot(lhs[group][...], ws[0][...])
        for j in range(1, n_lhs):
            y = y + _dot(lhs[2 * j + group][...], ws[j][...])
        z = ALPHA * x_ref[...] + m_ref[...] * y
        o_ref[...] = _layer_norm(z, g_ref[...], b_ref[...])

    in_prompt = pl.program_id(0) < TP // tm
    pl.when(in_prompt)(lambda: compute(0))
    pl.when(jnp.logical_not(in_prompt))(lambda: compute(1))


def _outproj(lhs, ws, x, mod, ln_g, ln_b, layer, tm=OUT_TM):
    n_lhs = len(lhs)
    in_specs, args = [], []
    for a_p, a_s in lhs:
        in_specs += _group_specs(True, tm, a_p.shape[1])
        args += [a_p, a_s]
    in_specs += [_resident(w.shape) for w in ws]
    in_specs += [pl.BlockSpec((tm, D), lambda i: (i, 0))] + _mod_specs(layer, (5,), tm) + _ln_specs(layer, 1)
    return pl.pallas_call(
        functools.partial(_outproj_body, n_lhs=n_lhs, tm=tm),
        grid=(T // tm,),
        in_specs=in_specs,
        out_specs=pl.BlockSpec((tm, D), lambda i: (i, 0)),
        out_shape=jax.ShapeDtypeStruct((T, D), F32),
        compiler_params=_params(1),
        name="mixer_outproj",
    )(*args, *ws, x, mod, ln_g, ln_b)


def _rope_partner(x):
    lane = lax.broadcasted_iota(jnp.int32, x.shape, 1)
    first_half = (lane % (2 * ROPE_FREQS)) < ROPE_FREQS
    return jnp.where(first_half, pltpu.roll(x, LANES - ROPE_FREQS, axis=1), pltpu.roll(x, ROPE_FREQS, axis=1))


def _qkv_body(x_ref, shift_ref, scale_ref, w_ref, wkvt_ref, cos_ref, sin_ref,
              q_ref, k_ref, v_ref, kt_ref, vt_ref):
    h = (x_ref[...] * (1.0 + scale_ref[...]) + shift_ref[...]).astype(BF16)
    qscale = C_HEAD_DIM ** -0.5
    v_ref[...] = _dot(h, w_ref[:, C_Q + C_KV:]).astype(BF16)
    in_prompt = pl.program_id(0) < TP // PROJ_TM

    @pl.when(in_prompt)
    def _():
        q_ref[...] = (_dot(h, w_ref[:, :C_Q]) * qscale).astype(BF16)
        kt_ref[...] = _dot_nt(wkvt_ref[:C_KV, :], h)
        vt_ref[...] = _dot_nt(wkvt_ref[C_KV:, :], h)

    @pl.when(jnp.logical_not(in_prompt))
    def _():
        cos, sin = cos_ref[...], sin_ref[...]
        zq = _dot(h, w_ref[:, :C_Q])
        zk = _dot(h, w_ref[:, C_Q:C_Q + C_KV])

        def rope(z):
            return z * cos + _rope_partner(z) * sin

        for j in range(C_Q // LANES):
            cols = slice(j * LANES, (j + 1) * LANES)
            q_ref[:, cols] = (rope(zq[:, cols]) * qscale).astype(BF16)
        for j in range(C_KV // LANES):
            cols = slice(j * LANES, (j + 1) * LANES)
            k_ref[:, cols] = rope(zk[:, cols]).astype(BF16)


def _qkv(x, mod, w, wkvt, cos, sin, layer):
    tm = PROJ_TM
    n_p = TP // tm
    lat = lambda i: (jnp.maximum(i - n_p, 0), 0)
    ctx = lambda i: (jnp.minimum(i, n_p - 1), 0, 0)
    return pl.pallas_call(
        _qkv_body,
        grid=(T // tm,),
        in_specs=[pl.BlockSpec((tm, D), lambda i: (i, 0))] + _mod_specs(layer, (3, 4), tm) + [
            _resident((D, C_Q + 2 * C_KV)),
            _resident((2 * C_KV, D)),
            pl.BlockSpec((tm, LANES), lat),
            pl.BlockSpec((tm, LANES), lat),
        ],
        out_specs=[
            pl.BlockSpec((tm, C_Q), lambda i: (i, 0)),
            pl.BlockSpec((tm, C_KV), lat),
            pl.BlockSpec((tm, C_KV), lambda i: (i, 0)),
            pl.BlockSpec((None, C_KV, SEQ), ctx),
            pl.BlockSpec((None, C_KV, SEQ), ctx),
        ],
        out_shape=[jax.ShapeDtypeStruct((T, C_Q), BF16), jax.ShapeDtypeStruct((TS, C_KV), BF16),
                   jax.ShapeDtypeStruct((T, C_KV), BF16),
                   jax.ShapeDtypeStruct((BATCH, C_KV, SEQ), F32), jax.ShapeDtypeStruct((BATCH, C_KV, SEQ), F32)],
        compiler_params=_params(1),
        name="c_qkv",
    )(x, mod, mod, w, wkvt, cos, sin)


def _rope_tables():
    t = np.arange(DEC_SEQ)
    pos = np.stack([t // GRID_W, t % GRID_W], axis=1).astype(np.float32)
    inv = (ROPE_BASE ** (-np.arange(ROPE_FREQS, dtype=np.float32) / ROPE_FREQS)).astype(np.float32)
    d = np.arange(C_HEAD_DIM)
    axis = d // (2 * ROPE_FREQS)
    ang = pos[:, axis] * inv[d % ROPE_FREQS][None, :]
    sign = np.where((d % (2 * ROPE_FREQS)) < ROPE_FREQS, -1.0, 1.0)[None, :]
    cos_h, sin_h = np.cos(ang), np.sin(ang) * sign
    reps = LANES // C_HEAD_DIM
    cos = np.tile(np.tile(cos_h, (1, reps)), (DEC_BATCH, 1)).astype(np.float32)
    sin = np.tile(np.tile(sin_h, (1, reps)), (DEC_BATCH, 1)).astype(np.float32)
    return jnp.asarray(cos), jnp.asarray(sin)


def _dup_head(blk, half):
    lane = lax.broadcasted_iota(jnp.int32, blk.shape, 1)
    keep = (lane >= HALF).astype(jnp.int32) == half
    return jnp.where(keep, blk, pltpu.roll(blk, HALF, axis=1))


def _pair_values(v, half):
    lo = lax.broadcasted_iota(jnp.int32, v.shape, 1) < HALF
    v2 = _dup_head(v.astype(F32), half)
    top = jnp.concatenate([jnp.where(lo, v2, 0.0), jnp.where(lo, 1.0, 0.0)], axis=1)
    bot = jnp.concatenate([jnp.where(lo, 0.0, v2), jnp.where(lo, 0.0, 1.0)], axis=1)
    return top.astype(BF16), bot.astype(BF16)


def _pair_softmax_out(scores_a, scores_b, values_a, values_b, sink_a, sink_b):
    lo = lax.broadcasted_iota(jnp.int32, (1, LANES), 1) < HALF
    od, sink_terms = None, []
    for scores, values, sink in ((scores_a, values_a, sink_a), (scores_b, values_b, sink_b)):
        mx = sink
        for s in scores:
            mx = jnp.maximum(mx, jnp.max(s, axis=-1, keepdims=True))
        for s, v in zip(scores, values):
            part = _dot(jnp.exp(s - mx).astype(BF16), v)
            od = part if od is None else od + part
        sink_terms.append(jnp.exp(sink - mx))
    den = od[:, LANES:] + jnp.where(lo, sink_terms[0], sink_terms[1])
    return od[:, :LANES] / den


def _pair_sinks(sink_ref, jb):
    return (sink_ref[:, jb * LANES:jb * LANES + 1], sink_ref[:, jb * LANES + HALF:jb * LANES + HALF + 1])


def _ctx_attn_body(q_ref, kt_ref, v_ref, sink_ref, o_ref):
    half = pl.program_id(1) % 2
    kt = kt_ref[pl.ds(pl.multiple_of(half * C_HEAD_DIM, C_HEAD_DIM), C_HEAD_DIM), :]
    zero = jnp.zeros_like(kt)
    kt_a = jnp.concatenate([kt, zero], axis=0).astype(BF16)
    kt_b = jnp.concatenate([zero, kt], axis=0).astype(BF16)
    v_top, v_bot = _pair_values(v_ref[...], half)
    for jb in range(C_GROUPS // 2):
        q = q_ref[:, jb * LANES:(jb + 1) * LANES]
        sink_a, sink_b = _pair_sinks(sink_ref, jb)
        o = _pair_softmax_out([_dot(q, kt_a)], [_dot(q, kt_b)], [v_top], [v_bot], sink_a, sink_b)
        o_ref[:, jb * LANES:(jb + 1) * LANES] = o.astype(BF16)


def _ctx_attn(q, kt, v, sink_cols):
    gw = C_GROUPS * C_HEAD_DIM
    return pl.pallas_call(
        _ctx_attn_body,
        grid=(BATCH, C_KV_HEADS),
        in_specs=[
            pl.BlockSpec((SEQ, gw), lambda b, h: (b, h)),
            pl.BlockSpec((None, LANES, SEQ), lambda b, h: (b, h // 2, 0)),
            pl.BlockSpec((SEQ, LANES), lambda b, h: (b, h // 2)),
            pl.BlockSpec((1, gw), lambda b, h: (0, h)),
        ],
        out_specs=pl.BlockSpec((SEQ, gw), lambda b, h: (b, h)),
        out_shape=jax.ShapeDtypeStruct((TP, C_Q), BF16),
        compiler_params=_params(2),
        name="c_attn_ctx",
    )(q, kt, v, sink_cols)


LAT_TQ = 128
LAT_WIN = LAT_TQ + 2 * WINDOW


def _lat_attn_body(q_ref, k_ref, v_ref, ck_ref, cv_ref, sink_ref, o_ref,
                   ka_s, kb_s, va_s, vb_s, cka_s, ckb_s, cva_s, cvb_s):
    half = pl.program_id(1) % 2
    j = pl.program_id(2)

    @pl.when(j == 0)
    def _():
        for src, a_s, b_s in ((k_ref, ka_s, kb_s), (ck_ref, cka_s, ckb_s)):
            lo = lax.broadcasted_iota(jnp.int32, src.shape, 1) < HALF
            k2 = _dup_head(src[...].astype(F32), half)
            a_s[...] = jnp.where(lo, k2, 0.0).astype(BF16)
            b_s[...] = jnp.where(lo, 0.0, k2).astype(BF16)
        va_s[...], vb_s[...] = _pair_values(v_ref[...], half)
        cva_s[...], cvb_s[...] = _pair_values(cv_ref[...], half)

    start = pl.multiple_of(jnp.clip(j * LAT_TQ - WINDOW, 0, DEC_SEQ - LAT_WIN), LANES)
    win = pl.ds(start, LAT_WIN)
    qpos = j * LAT_TQ + lax.broadcasted_iota(jnp.int32, (LAT_TQ, LAT_WIN), 0)
    kpos = start + lax.broadcasted_iota(jnp.int32, (LAT_TQ, LAT_WIN), 1)
    band = jnp.abs(qpos - kpos) <= WINDOW
    for jb in range(C_GROUPS // 2):
        q = q_ref[:, jb * LANES:(jb + 1) * LANES]
        sink_a, sink_b = _pair_sinks(sink_ref, jb)
        scores = [[jnp.where(band, _dot_nt(q, kl[win, :]), -jnp.inf), _dot_nt(q, kc[...])]
                  for kl, kc in ((ka_s, cka_s), (kb_s, ckb_s))]
        o = _pair_softmax_out(scores[0], scores[1], [va_s[win, :], cva_s[...]], [vb_s[win, :], cvb_s[...]],
                              sink_a, sink_b)
        o_ref[:, jb * LANES:(jb + 1) * LANES] = o.astype(BF16)


def _lat_attn(q, k, v, ck, cv, sink_cols):
    gw = C_GROUPS * C_HEAD_DIM
    nq = DEC_SEQ // LAT_TQ
    q_off = TP // LAT_TQ
    kv_off = TP // DEC_SEQ
    return pl.pallas_call(
        _lat_attn_body,
        grid=(DEC_BATCH, C_KV_HEADS, nq),
        in_specs=[
            pl.BlockSpec((LAT_TQ, gw), lambda b, h, j: (q_off + b * nq + j, h)),
            pl.BlockSpec((DEC_SEQ, LANES), lambda b, h, j: (b, h // 2)),
            pl.BlockSpec((DEC_SEQ, LANES), lambda b, h, j: (kv_off + b, h // 2)),
            pl.BlockSpec((PAST_LEN, LANES), lambda b, h, j: (b, h // 2)),
            pl.BlockSpec((PAST_LEN, LANES), lambda b, h, j: (b, h // 2)),
            pl.BlockSpec((1, gw), lambda b, h, j: (0, h)),
        ],
        out_specs=pl.BlockSpec((LAT_TQ, gw), lambda b, h, j: (b * nq + j, h)),
        out_shape=jax.ShapeDtypeStruct((TS, C_Q), BF16),
        scratch_shapes=[
            pltpu.VMEM((DEC_SEQ, LANES), BF16), pltpu.VMEM((DEC_SEQ, LANES), BF16),
            pltpu.VMEM((DEC_SEQ, 2 * LANES), BF16), pltpu.VMEM((DEC_SEQ, 2 * LANES), BF16),
            pltpu.VMEM((PAST_LEN, LANES), BF16), pltpu.VMEM((PAST_LEN, LANES), BF16),
            pltpu.VMEM((PAST_LEN, 2 * LANES), BF16), pltpu.VMEM((PAST_LEN, 2 * LANES), BF16),
        ],
        compiler_params=_params(3),
        name="c_attn_latent",
    )(q, k, v, ck, cv, sink_cols)


def kernel(x_prompt, x_sample, state_hgrn, state_gla, cache_k, cache_v, c, c_ctx, w_mod, b_mod, ln_g, ln_b,
           ffn_w1, ffn_w3, ffn_w2, w_in_ab, hgrn_lb, gla_gate_up, gla_gate_b, norm_a, norm_b, w_out_ab,
           w_qkv_c, sink_c, w_out_c):
    cs = jnp.zeros((8, D), F32).at[0].set(c_ctx).at[1:1 + DEC_BATCH].set(c)
    mod = _mod_vectors(cs, w_mod, b_mod).reshape(DEPTH, 8, 1, N_MOD * D)
    ffn_ws = (ffn_w1, ffn_w3, ffn_w2)
    ln_g, ln_b = ln_g.reshape(DEPTH, 3, 1, D), ln_b.reshape(DEPTH, 3, 1, D)

    def ffn(xs, ws, layer, sub, split_out=False, cast_next=None):
        casts = None if cast_next is None else (ffn_ws, cast_next)
        return _ffn_sublayer(xs, mod, *ws, ln_g, ln_b, layer, sub, split_out=split_out, casts=casts)

    x, *ws_01 = ffn([x_prompt.reshape(TP, D), x_sample.reshape(TS, D)], [w[0, 0].astype(BF16) for w in ffn_ws],
                    0, 0, cast_next=(0, 1))
    w_in = w_in_ab[0]
    o_aq, o_ai, o_ff, o_fb, o_ag = 0, A_W, 2 * A_W, 3 * A_W, 4 * A_W
    o_bq = 5 * A_W
    o_bk, o_bv = o_bq + B_QK, o_bq + 2 * B_QK
    o_bg = o_bv + B_V
    o_z = o_bg + B_V
    order = [(o_aq, A_W), (o_ff, A_W), (o_fb, A_W), (o_ag, A_W), (o_bq, B_QK), (o_bk, B_QK), (o_bg, B_V),
             (o_ai, A_W), (o_bv, B_V)]
    wmain = jnp.concatenate([w_in[:, o:o + w] for o, w in order], axis=1).astype(BF16)
    wz = jnp.pad(w_in[:, o_z:o_z + 2 * GATE_RANK], ((0, 0), (0, LANES - 2 * GATE_RANK))).astype(BF16)
    gup = jnp.zeros((LANES, 2 * B_QK), F32)
    gup = gup.at[:GATE_RANK, :B_QK].set(gla_gate_up[0, 0]).at[GATE_RANK:2 * GATE_RANK, B_QK:].set(gla_gate_up[0, 1])
    gb = gla_gate_b[0].reshape(1, 2 * B_QK)
    pf, pb = _inproj(x, mod, wmain, wz, gup.astype(BF16), gb, hgrn_lb, 0, 0)

    s0_a = state_hgrn[:, 0]
    s0_b = state_gla[:, 0].reshape(DEC_BATCH, 2, B_HEADS // 2, LANES, B_DV)
    oa_p, st_a = _scan(pf, pb, norm_a[0], None, prompt=True, pair=False)
    ob_p, st_b = _scan(pf, pb, norm_b[0], None, prompt=True, pair=True)
    (oa_s,) = _scan(pf, pb, norm_a[0], s0_a, prompt=False, pair=False)
    (ob_s,) = _scan(pf, pb, norm_b[0], s0_b, prompt=False, pair=True)
    w_out = w_out_ab[0].astype(BF16)
    x = _outproj([(oa_p, oa_s), (ob_p, ob_s)], [w_out[:A_W], w_out[A_W:]], x, mod, ln_g, ln_b, 0)
    x, *ws_10 = ffn([x], ws_01, 0, 1, cast_next=(1, 0))
    new_hgrn = st_a.reshape(BATCH, 1, 2, A_HEADS, A_DK, A_DV)
    new_gla = st_b.reshape(BATCH, 1, 2, B_HEADS, B_DK, B_DV)

    x, *ws_11 = ffn([x], ws_10, 1, 0, cast_next=(1, 1))
    cos, sin = _rope_tables()
    w_qkv = w_qkv_c[0].astype(BF16)
    q, k, v, kt, vt = _qkv(x, mod, w_qkv, w_qkv[:, C_Q:].T, cos, sin, 1)
    sink_cols = jnp.repeat(sink_c[0], C_HEAD_DIM).reshape(1, C_Q)
    o_p = _ctx_attn(q, kt, v, sink_cols)
    ck = cache_k[:, 0].reshape(DEC_BATCH * PAST_LEN, C_KV)
    cv = cache_v[:, 0].reshape(DEC_BATCH * PAST_LEN, C_KV)
    o_s = _lat_attn(q, k, v, ck, cv, sink_cols)
    x = _outproj([(o_p, o_s)], [w_out_c[0].astype(BF16)], x, mod, ln_g, ln_b, 1)
    y_p, y_s = ffn([x], ws_11, 1, 1, split_out=True)

    def cache_layout(zt):
        return zt.reshape(BATCH, 1, C_KV_HEADS, C_HEAD_DIM, SEQ).transpose(0, 1, 4, 2, 3)

    new_k, new_v = cache_layout(kt), cache_layout(vt)

    return (y_p.reshape(BATCH, SEQ, D), y_s.reshape(DEC_BATCH, DEC_SEQ, D), new_hgrn, new_gla, new_k, new_v)
```

```python
import functools
import math

import jax
import jax.numpy as jnp
import numpy as np
from jax import lax
from jax.experimental import pallas as pl
from jax.experimental.pallas import tpu as pltpu

D = 1024
BATCH, SEQ = 16, 256
DEC_BATCH, DEC_SEQ = 2, 2048
PAST_LEN = 512
GRID_W = 64
D_FF = 2816
N_MOD = 9
A_HEADS, A_DK, A_DV = 4, 128, 128
A_W = A_HEADS * A_DK
B_HEADS, B_DK, B_DV = 4, 64, 128
B_QK = B_HEADS * B_DK
B_V = B_HEADS * B_DV
GATE_RANK = 16
GLA_TAU = 16.0
CHUNK = 128
C_HEADS, C_KV_HEADS, C_HEAD_DIM = 16, 4, 64
C_GROUPS = C_HEADS // C_KV_HEADS
C_Q = C_HEADS * C_HEAD_DIM
C_KV = C_KV_HEADS * C_HEAD_DIM
WINDOW = 128
ROPE_FREQS = C_HEAD_DIM // 4
ROPE_BASE = 10000.0
DEPTH = 2
ALPHA = (2.0 * DEPTH) ** 0.25
LN_EPS = 1e-5
RMS_EPS = 1e-6

TP = BATCH * SEQ
TS = DEC_BATCH * DEC_SEQ
T = TP + TS
N_SEG = 1 + DEC_BATCH

LANES = 128
HALF = LANES // 2
FFN_TM = 512
PROJ_TM = SEQ
OUT_TM = 512
VMEM_LIMIT = 56 * 1024 * 1024

F32 = jnp.float32
BF16 = jnp.bfloat16


def _dot(a, b):
    return jnp.dot(a, b, preferred_element_type=F32)


def _dot_nt(a, b):
    return lax.dot_general(a, b, (((1,), (1,)), ((), ())), preferred_element_type=F32)


def _dot_tn(a, b):
    return lax.dot_general(a, b, (((0,), (0,)), ((), ())), preferred_element_type=F32)


def _silu(x):
    return x * jax.nn.sigmoid(x)


def _layer_norm(z, g, b):
    mu = jnp.mean(z, axis=-1, keepdims=True)
    zc = z - mu
    var = jnp.mean(zc * zc, axis=-1, keepdims=True)
    return zc * lax.rsqrt(var + LN_EPS) * g + b


def _seg_of_tile(i, tm):
    n_p = TP // tm
    n_s = DEC_SEQ // tm
    return jnp.where(i < n_p, 0, 1 + lax.div(jnp.maximum(i - n_p, 0), n_s))


def _params(n_axes):
    return pltpu.CompilerParams(dimension_semantics=("arbitrary",) * n_axes, vmem_limit_bytes=VMEM_LIMIT)


def _resident(shape):
    nd = len(shape)
    return pl.BlockSpec(shape, lambda *_: (0,) * nd, pipeline_mode=pl.Buffered(1))


def _resident_slice(shape, lead):
    block = (None,) * len(lead) + tuple(shape)
    return pl.BlockSpec(block, lambda *_: tuple(lead) + (0,) * len(shape), pipeline_mode=pl.Buffered(1))


def _mod_specs(layer, cols, tm):
    return [pl.BlockSpec((None, None, 1, D), functools.partial(
        lambda i, c: (layer, _seg_of_tile(i, tm), 0, c), c=c)) for c in cols]


def _ln_specs(layer, idx):
    return [_resident_slice((1, D), (layer, idx))] * 2


BF16_SUBLANES = 16


def _cast_plan(ws, lead, n_steps, step_of):
    in_specs, out_specs, out_shapes = [], [], []
    for w in ws:
        rows, cols = w.shape[len(lead):]
        blk = next(b for b in range(BF16_SUBLANES, rows + 1, BF16_SUBLANES)
                   if rows % b == 0 and rows // b <= n_steps)
        last = rows // blk - 1
        in_specs.append(pl.BlockSpec((None,) * len(lead) + (blk, cols), functools.partial(
            lambda *g, last: tuple(lead) + (jnp.minimum(step_of(*g), last), 0), last=last)))
        out_specs.append(pl.BlockSpec((blk, cols), functools.partial(
            lambda *g, last: (jnp.minimum(step_of(*g), last), 0), last=last)))
        out_shapes.append(jax.ShapeDtypeStruct((rows, cols), BF16))
    return in_specs, out_specs, out_shapes


def _hosting_casts(body, n_in, n_out, n_cast):
    def hosted(*refs, **kw):
        ins, refs = refs[:n_in], refs[n_in:]
        cast_in, refs = refs[:n_cast], refs[n_cast:]
        outs, refs = refs[:n_out], refs[n_out:]
        cast_out, scratch = refs[:n_cast], refs[n_cast:]
        for src, dst in zip(cast_in, cast_out):
            dst[...] = src[...].astype(BF16)
        body(*ins, *outs, *scratch, **kw)
    return hosted


def _mod_body(c_ref, w_ref, b_ref, o_ref):
    c = c_ref[...]
    s = _silu(c).astype(BF16)
    o_ref[0] = _dot(s, w_ref[0].astype(BF16)) + b_ref[0]


def _mod_vectors(cs, w_mod, b_mod):
    tn = 1536
    n = N_MOD * D
    return pl.pallas_call(
        _mod_body,
        grid=(DEPTH, n // tn),
        in_specs=[
            pl.BlockSpec((8, D), lambda l, j: (0, 0)),
            pl.BlockSpec((1, D, tn), lambda l, j: (l, 0, j)),
            pl.BlockSpec((1, 1, tn), lambda l, j: (l, 0, j)),
        ],
        out_specs=pl.BlockSpec((1, 8, tn), lambda l, j: (l, 0, j)),
        out_shape=jax.ShapeDtypeStruct((DEPTH, 8, n), F32),
        compiler_params=_params(2),
        name="mod_vectors",
    )(cs, w_mod, b_mod.reshape(DEPTH, 1, n))


def _ffn_body(*refs, n_x, n_o, tm):
    x_refs = refs[:n_x]
    shift_ref, scale_ref, gate_ref, w1_ref, w3_ref, w2_ref, g_ref, b_ref = refs[n_x:n_x + 8]
    o_refs = refs[n_x + 8:]

    def compute(x_ref, o_ref):
        x = x_ref[...]
        shift, scale, gate = shift_ref[...], scale_ref[...], gate_ref[...]
        h = (x * (1.0 + scale) + shift).astype(BF16)
        a = _dot(h, w1_ref[...])
        b = _dot(h, w3_ref[...])
        g = (_silu(a) * b).astype(BF16)
        y = _dot(g, w2_ref[...])
        z = ALPHA * x + (0.5 * gate) * y
        o_ref[...] = _layer_norm(z, g_ref[...], b_ref[...])

    if n_x == 1 and n_o == 1:
        compute(x_refs[0], o_refs[0])
    else:
        in_prompt = pl.program_id(0) < TP // tm
        pl.when(in_prompt)(lambda: compute(x_refs[0], o_refs[0]))
        pl.when(jnp.logical_not(in_prompt))(lambda: compute(x_refs[-1], o_refs[-1]))


def _group_specs(split, tm, width=D):
    if not split:
        return [pl.BlockSpec((tm, width), lambda i: (i, 0))]
    n_p = TP // tm
    return [pl.BlockSpec((tm, width), lambda i: (jnp.minimum(i, n_p - 1), 0)),
            pl.BlockSpec((tm, width), lambda i: (jnp.maximum(i - n_p, 0), 0))]


def _ffn_sublayer(xs, mod, w1, w3, w2, ln_g, ln_b, layer, sub, split_out=False, casts=None, tm=FFN_TM):
    n_x, n_o = len(xs), 2 if split_out else 1
    out_shape = ([jax.ShapeDtypeStruct((TP, D), F32), jax.ShapeDtypeStruct((TS, D), F32)] if split_out
                 else [jax.ShapeDtypeStruct((T, D), F32)])
    mod_lo = 6 * sub
    body = functools.partial(_ffn_body, n_x=n_x, n_o=n_o, tm=tm)
    in_specs = _group_specs(n_x == 2, tm) + _mod_specs(layer, (mod_lo, mod_lo + 1, mod_lo + 2), tm) + [
        _resident((D, D_FF)),
        _resident((D, D_FF)),
        _resident((D_FF, D)),
    ] + _ln_specs(layer, 2 * sub)
    out_specs = _group_specs(split_out, tm)
    args = [*xs, mod, mod, mod, w1, w3, w2, ln_g, ln_b]
    if casts is not None:
        ws, lead = casts
        c_in, c_out, c_shapes = _cast_plan(ws, lead, T // tm, lambda i: i)
        body = _hosting_casts(body, len(in_specs), len(out_specs), len(ws))
        in_specs, out_specs, out_shape = in_specs + c_in, out_specs + c_out, out_shape + c_shapes
        args = args + list(ws)
    return pl.pallas_call(
        body,
        grid=(T // tm,),
        in_specs=in_specs,
        out_specs=out_specs,
        out_shape=out_shape,
        compiler_params=_params(1),
        name="ffn_sublayer",
    )(*args)


PF_AQ, PF_FF, PF_FB, PF_AG = 0, 512, 1024, 1536
PF_BQ, PF_BK, PF_BG, PF_LAF, PF_LAB = 2048, 2304, 2560, 3072, 3328
PF_W = 3584
PB_AV, PB_BV = 0, 512
PB_W = 1024
WM_AQ, WM_FF, WM_FB, WM_AG, WM_BQ, WM_BK, WM_BG, WM_AI, WM_BV = 0, 512, 1024, 1536, 2048, 2304, 2560, 3072, 3584
WM_W = 4096


def _log_sigmoid(x):
    return jnp.minimum(x, 0.0) - jnp.log(1.0 + jnp.exp(-jnp.abs(x)))


def _inproj_body(x_ref, shift_ref, scale_ref, w_ref, wz_ref, gu_ref, gb_ref, lb_ref, pf_ref, pb_ref, *, layer_e):
    h = (x_ref[...] * (1.0 + scale_ref[...]) + shift_ref[...]).astype(BF16)

    def proj(off, width):
        return _dot(h, w_ref[:, off:off + width])

    def lower_bound(d):
        l = lb_ref[d]
        e = jnp.exp(l - jnp.max(l, axis=0, keepdims=True))
        sm = e / jnp.sum(e, axis=0, keepdims=True)
        return jnp.sum(sm[:layer_e + 1], axis=0, keepdims=True)

    pf_ref[:, PF_AQ:PF_AQ + A_W] = proj(WM_AQ, A_W)
    for d, (wm, pf) in enumerate(((WM_FF, PF_FF), (WM_FB, PF_FB))):
        lb = lower_bound(d)
        pf_ref[:, pf:pf + A_W] = lb + (1.0 - lb) * jax.nn.sigmoid(proj(wm, A_W))
    pf_ref[:, PF_AG:PF_AG + A_W] = _silu(proj(WM_AG, A_W))
    pf_ref[:, PF_BQ:PF_BQ + B_QK] = proj(WM_BQ, B_QK) * (B_DK ** -0.5)
    pf_ref[:, PF_BK:PF_BK + B_QK] = proj(WM_BK, B_QK)
    pf_ref[:, PF_BG:PF_BG + B_V] = _silu(proj(WM_BG, B_V))
    pb_ref[:, PB_AV:PB_AV + A_W] = _silu(proj(WM_AI, A_W)).astype(BF16)
    pb_ref[:, PB_BV:PB_BV + B_V] = proj(WM_BV, B_V).astype(BF16)
    z = _dot(h, wz_ref[...]).astype(BF16)
    pre = _dot(z, gu_ref[...]) + gb_ref[...]
    pf_ref[:, PF_LAF:PF_LAF + 2 * B_QK] = _log_sigmoid(pre) * (1.0 / GLA_TAU)


def _inproj(x, mod, wmain, wz, gup, gb, hgrn_lb, layer, layer_e, tm=PROJ_TM):
    n_l = hgrn_lb.shape[1]
    return pl.pallas_call(
        functools.partial(_inproj_body, layer_e=layer_e),
        grid=(T // tm,),
        in_specs=[pl.BlockSpec((tm, D), lambda i: (i, 0))] + _mod_specs(layer, (3, 4), tm) + [
            _resident((D, WM_W)),
            _resident((D, LANES)),
            _resident((LANES, 2 * B_QK)),
            _resident((1, 2 * B_QK)),
            _resident((2, n_l, A_W)),
        ],
        out_specs=[
            pl.BlockSpec((tm, PF_W), lambda i: (i, 0)),
            pl.BlockSpec((tm, PB_W), lambda i: (i, 0)),
        ],
        out_shape=[jax.ShapeDtypeStruct((T, PF_W), F32), jax.ShapeDtypeStruct((T, PB_W), BF16)],
        compiler_params=_params(1),
        name="ab_inproj",
    )(x, mod, mod, wmain, wz, gup, gb, hgrn_lb)


SCAN_PROMPT_SEQS = 4
SCAN_UNROLL = 4


def _prefix_rows(x):
    row = lax.broadcasted_iota(jnp.int32, x.shape, 0)
    s = 1
    while s < x.shape[0]:
        x = x + jnp.where(row >= s, pltpu.roll(x, s, axis=0), 0.0)
        s *= 2
    return x


def _scan_body(*refs, seq_len, seqs, pair, has_s0, emit_state):
    n = seq_len // CHUNK
    n_all = seqs * n
    nh = 2 if pair else 1
    it = iter(refs)
    q_ref = next(it)
    if pair:
        k_ref, laf_ref, lab_ref = next(it), next(it), next(it)
    else:
        ff_ref, fb_ref = next(it), next(it)
    g_ref, v_ref, nw_ref = next(it), next(it), next(it)
    s0_ref = next(it) if has_s0 else None
    o_ref = next(it)
    st_ref = next(it) if emit_state else None
    qd_s, oi_s, kv_s, dec_s, sb_s = it

    row = lax.broadcasted_iota(jnp.int32, (CHUNK, CHUNK), 0)
    col = lax.broadcasted_iota(jnp.int32, (CHUNK, CHUNK), 1)
    tril = row >= col
    triu = row <= col
    lane = lax.broadcasted_iota(jnp.int32, (1, LANES), 1)
    lane2 = lax.broadcasted_iota(jnp.int32, (1, 2 * LANES), 1)
    if pair:
        masks = [lane < HALF, lane >= HALF]
        masks2 = [(lane2 % LANES) < HALF, (lane2 % LANES) >= HALF]
    else:
        masks, masks2 = [None], [None]

    def pick(mask, x):
        return x if mask is None else jnp.where(mask, x, jnp.zeros_like(x))

    def rows_of(c):
        return pl.ds(pl.multiple_of(c * CHUNK, CHUNK), CHUNK)

    def loop(body):
        if n_all <= SCAN_UNROLL:
            for c in range(n_all):
                body(c)
        else:
            def fbody(i, carry):
                for u in range(SCAN_UNROLL):
                    body(i * SCAN_UNROLL + u)
                return carry
            lax.fori_loop(0, n_all // SCAN_UNROLL, fbody, 0)

    def phase1(c):
        rows = rows_of(c)
        q = q_ref[rows, :]
        if pair:
            k_f = k_b = k_ref[rows, :]
            la_f, la_b = laf_ref[rows, :], lab_ref[rows, :]
        else:
            f_f, f_b = ff_ref[rows, :], fb_ref[rows, :]
            k_f, k_b = 1.0 - f_f, 1.0 - f_b
            la_f, la_b = jnp.log(f_f), jnp.log(f_b)
        cs = _prefix_rows(jnp.concatenate([la_f, la_b], axis=1))
        cf, cbi = cs[:, :LANES], cs[:, LANES:]
        tot_f, tot_b = cf[CHUNK - 1:CHUNK, :], cbi[CHUNK - 1:CHUNK, :]
        rb = tot_b - cbi + la_b
        ref_f, ref_b = cf[CHUNK // 2 - 1:CHUNK // 2, :], rb[CHUNK // 2:CHUNK // 2 + 1, :]
        qtf = q * jnp.exp(cf - ref_f)
        qtb = q * jnp.exp(rb - ref_b)
        ktf = k_f * jnp.exp(ref_f - cf)
        ktb = k_b * jnp.exp(ref_b - rb)
        qd = jnp.concatenate([qtf * jnp.exp(ref_f), qtb * jnp.exp(ref_b)], axis=1).astype(BF16)
        ku = jnp.concatenate([ktf * jnp.exp(tot_f - ref_f), ktb * jnp.exp(tot_b - ref_b)], axis=1).astype(BF16)
        qd_s[rows, :] = qd
        qt = jnp.concatenate([qtf, qtb], axis=0).astype(BF16)
        kt = jnp.concatenate([ktf, ktb], axis=0).astype(BF16)
        kv = None
        for hh in range(nh):
            v = v_ref[rows, hh * LANES:(hh + 1) * LANES]
            sc = _dot_nt(pick(masks[hh], qt), kt)
            att = jnp.where(tril, sc[:CHUNK, :CHUNK], 0.0) + jnp.where(triu, sc[CHUNK:, CHUNK:], 0.0)
            oi_s[rows, hh * LANES:(hh + 1) * LANES] = _dot(att.astype(BF16), v)
            kv_h = _dot_tn(v, ku)
            kv = kv_h if kv is None else jnp.where(masks2[0], kv, kv_h)
        kv_s[c] = kv
        dec_s[c] = jnp.exp(jnp.concatenate([tot_f, tot_b], axis=1))

    loop(phase1)

    def recurrence(sq, d, reverse):
        cols = slice(d * LANES, (d + 1) * LANES)
        c0 = sq * n
        st0 = s0_ref[sq, d, 0].T if has_s0 else jnp.zeros((LANES, LANES), F32)

        def step(c, st):
            sb_s[c, :, cols] = st.astype(BF16)
            return st * dec_s[c, :, cols] + kv_s[c, :, cols]

        if n <= 8:
            st = st0
            for c in (range(n - 1, -1, -1) if reverse else range(n)):
                st = step(c0 + c, st)
        else:
            st = lax.fori_loop(0, n, lambda i, st: step(c0 + (n - 1 - i if reverse else i), st), st0)
        if emit_state:
            st_ref[sq, d, 0] = st.T

    for sq in range(seqs):
        recurrence(sq, 0, False)
        recurrence(sq, 1, True)

    nw = nw_ref[...]

    def phase2(c):
        rows = rows_of(c)
        qcat = qd_s[rows, :]
        scat = sb_s[c]
        for hh in range(nh):
            cols = slice(hh * LANES, (hh + 1) * LANES)
            o = oi_s[rows, cols] + _dot_nt(pick(masks2[hh], qcat), scat)
            o = o * lax.rsqrt(jnp.mean(o * o, axis=-1, keepdims=True) + RMS_EPS) * nw
            o_ref[rows, cols] = (o * g_ref[rows, cols]).astype(BF16)

    loop(phase2)


def _scan(pf, pb, norm_w, s0, *, prompt, pair):
    seq_len = SEQ if prompt else DEC_SEQ
    nseq = BATCH if prompt else DEC_BATCH
    seqs = SCAN_PROMPT_SEQS if prompt else 1
    rows = seqs * seq_len
    row_off = 0 if prompt else TP // rows
    units = B_HEADS // 2 if pair else A_HEADS
    nh = 2 if pair else 1
    n_all = rows // CHUNK
    has_s0 = s0 is not None
    emit_state = prompt

    def colspec(off, width=LANES):
        base = off // width
        return pl.BlockSpec((rows, width), lambda s, u: (s + row_off, base + u))

    if pair:
        in_specs = [colspec(PF_BQ), colspec(PF_BK), colspec(PF_LAF), colspec(PF_LAB),
                    colspec(PF_BG, 2 * LANES), colspec(PB_BV, 2 * LANES)]
        args = [pf, pf, pf, pf, pf, pb]
    else:
        in_specs = [colspec(PF_AQ), colspec(PF_FF), colspec(PF_FB), colspec(PF_AG), colspec(PB_AV)]
        args = [pf, pf, pf, pf, pb]
    in_specs.append(pl.BlockSpec((1, LANES), lambda s, u: (0, 0)))
    args.append(norm_w.reshape(1, LANES))
    state_spec = pl.BlockSpec((seqs, 2, 1, LANES, LANES), lambda s, u: (s, 0, u, 0, 0))
    if has_s0:
        in_specs.append(state_spec)
        args.append(s0)
    out_specs = [pl.BlockSpec((rows, nh * LANES), lambda s, u: (s, u))]
    out_shape = [jax.ShapeDtypeStruct((nseq * seq_len, units * nh * LANES), BF16)]
    if emit_state:
        out_specs.append(state_spec)
        out_shape.append(jax.ShapeDtypeStruct((nseq, 2, units, LANES, LANES), F32))
    scratch = [
        pltpu.VMEM((rows, 2 * LANES), BF16),
        pltpu.VMEM((rows, nh * LANES), F32),
        pltpu.VMEM((n_all, LANES, 2 * LANES), F32),
        pltpu.VMEM((n_all, 1, 2 * LANES), F32),
        pltpu.VMEM((n_all, LANES, 2 * LANES), BF16),
    ]
    return pl.pallas_call(
        functools.partial(_scan_body, seq_len=seq_len, seqs=seqs, pair=pair, has_s0=has_s0,
                          emit_state=emit_state),
        grid=(nseq // seqs, units),
        in_specs=in_specs,
        out_specs=out_specs,
        out_shape=out_shape,
        scratch_shapes=scratch,
        compiler_params=_params(2),
        name=f"scan_{'p' if prompt else 's'}_{'gla' if pair else 'hgrn'}",
    )(*args)


def _outproj_body(*refs, n_lhs, tm):
    lhs = refs[:2 * n_lhs]
    ws = refs[2 * n_lhs:3 * n_lhs]
    x_ref, m_ref, g_ref, b_ref, o_ref = refs[3 * n_lhs:]

    def compute(group):
        y = _dot(lhs[group][...], ws[0][...])
        for j in range(1, n_lhs):
            y = y + _dot(lhs[2 * j + group][...], ws[j][...])
        z = ALPHA * x_ref[...] + m_ref[...] * y
        o_ref[...] = _layer_norm(z, g_ref[...], b_ref[...])

    in_prompt = pl.program_id(0) < TP // tm
    pl.when(in_prompt)(lambda: compute(0))
    pl.when(jnp.logical_not(in_prompt))(lambda: compute(1))


def _outproj(lhs, ws, x, mod, ln_g, ln_b, layer, tm=OUT_TM):
    n_lhs = len(lhs)
    in_specs, args = [], []
    for a_p, a_s in lhs:
        in_specs += _group_specs(True, tm, a_p.shape[1])
        args += [a_p, a_s]
    in_specs += [_resident(w.shape) for w in ws]
    in_specs += [pl.BlockSpec((tm, D), lambda i: (i, 0))] + _mod_specs(layer, (5,), tm) + _ln_specs(layer, 1)
    return pl.pallas_call(
        functools.partial(_outproj_body, n_lhs=n_lhs, tm=tm),
        grid=(T // tm,),
        in_specs=in_specs,
        out_specs=pl.BlockSpec((tm, D), lambda i: (i, 0)),
        out_shape=jax.ShapeDtypeStruct((T, D), F32),
        compiler_params=_params(1),
        name="mixer_outproj",
    )(*args, *ws, x, mod, ln_g, ln_b)


def _rope_partner(x):
    lane = lax.broadcasted_iota(jnp.int32, x.shape, 1)
    first_half = (lane % (2 * ROPE_FREQS)) < ROPE_FREQS
    return jnp.where(first_half, pltpu.roll(x, LANES - ROPE_FREQS, axis=1), pltpu.roll(x, ROPE_FREQS, axis=1))


def _qkv_body(x_ref, shift_ref, scale_ref, w_ref, wkvt_ref, cos_ref, sin_ref,
              q_ref, k_ref, v_ref, kt_ref, vt_ref):
    h = (x_ref[...] * (1.0 + scale_ref[...]) + shift_ref[...]).astype(BF16)
    qscale = C_HEAD_DIM ** -0.5
    v_ref[...] = _dot(h, w_ref[:, C_Q + C_KV:]).astype(BF16)
    in_prompt = pl.program_id(0) < TP // PROJ_TM

    @pl.when(in_prompt)
    def _():
        q_ref[...] = (_dot(h, w_ref[:, :C_Q]) * qscale).astype(BF16)
        kt_ref[...] = _dot_nt(wkvt_ref[:C_KV, :], h)
        vt_ref[...] = _dot_nt(wkvt_ref[C_KV:, :], h)

    @pl.when(jnp.logical_not(in_prompt))
    def _():
        cos, sin = cos_ref[...], sin_ref[...]
        zq = _dot(h, w_ref[:, :C_Q])
        zk = _dot(h, w_ref[:, C_Q:C_Q + C_KV])

        def rope(z):
            return z * cos + _rope_partner(z) * sin

        for j in range(C_Q // LANES):
            cols = slice(j * LANES, (j + 1) * LANES)
            q_ref[:, cols] = (rope(zq[:, cols]) * qscale).astype(BF16)
        for j in range(C_KV // LANES):
            cols = slice(j * LANES, (j + 1) * LANES)
            k_ref[:, cols] = rope(zk[:, cols]).astype(BF16)


def _qkv(x, mod, w, wkvt, cos, sin, layer):
    tm = PROJ_TM
    n_p = TP // tm
    lat = lambda i: (jnp.maximum(i - n_p, 0), 0)
    ctx = lambda i: (jnp.minimum(i, n_p - 1), 0, 0)
    return pl.pallas_call(
        _qkv_body,
        grid=(T // tm,),
        in_specs=[pl.BlockSpec((tm, D), lambda i: (i, 0))] + _mod_specs(layer, (3, 4), tm) + [
            _resident((D, C_Q + 2 * C_KV)),
            _resident((2 * C_KV, D)),
            pl.BlockSpec((tm, LANES), lat),
            pl.BlockSpec((tm, LANES), lat),
        ],
        out_specs=[
            pl.BlockSpec((tm, C_Q), lambda i: (i, 0)),
            pl.BlockSpec((tm, C_KV), lat),
            pl.BlockSpec((tm, C_KV), lambda i: (i, 0)),
            pl.BlockSpec((None, C_KV, SEQ), ctx),
            pl.BlockSpec((None, C_KV, SEQ), ctx),
        ],
        out_shape=[jax.ShapeDtypeStruct((T, C_Q), BF16), jax.ShapeDtypeStruct((TS, C_KV), BF16),
                   jax.ShapeDtypeStruct((T, C_KV), BF16),
                   jax.ShapeDtypeStruct((BATCH, C_KV, SEQ), F32), jax.ShapeDtypeStruct((BATCH, C_KV, SEQ), F32)],
        compiler_params=_params(1),
        name="c_qkv",
    )(x, mod, mod, w, wkvt, cos, sin)


def _rope_tables():
    t = np.arange(DEC_SEQ)
    pos = np.stack([t // GRID_W, t % GRID_W], axis=1).astype(np.float32)
    inv = (ROPE_BASE ** (-np.arange(ROPE_FREQS, dtype=np.float32) / ROPE_FREQS)).astype(np.float32)
    d = np.arange(C_HEAD_DIM)
    axis = d // (2 * ROPE_FREQS)
    ang = pos[:, axis] * inv[d % ROPE_FREQS][None, :]
    sign = np.where((d % (2 * ROPE_FREQS)) < ROPE_FREQS, -1.0, 1.0)[None, :]
    cos_h, sin_h = np.cos(ang), np.sin(ang) * sign
    reps = LANES // C_HEAD_DIM
    cos = np.tile(np.tile(cos_h, (1, reps)), (DEC_BATCH, 1)).astype(np.float32)
    sin = np.tile(np.tile(sin_h, (1, reps)), (DEC_BATCH, 1)).astype(np.float32)
    return jnp.asarray(cos), jnp.asarray(sin)


def _dup_head(blk, half):
    lane = lax.broadcasted_iota(jnp.int32, blk.shape, 1)
    keep = (lane >= HALF).astype(jnp.int32) == half
    return jnp.where(keep, blk, pltpu.roll(blk, HALF, axis=1))


def _pair_values(v, half):
    lo = lax.broadcasted_iota(jnp.int32, v.shape, 1) < HALF
    v2 = _dup_head(v.astype(F32), half)
    top = jnp.concatenate([jnp.where(lo, v2, 0.0), jnp.where(lo, 1.0, 0.0)], axis=1)
    bot = jnp.concatenate([jnp.where(lo, 0.0, v2), jnp.where(lo, 0.0, 1.0)], axis=1)
    return top.astype(BF16), bot.astype(BF16)


def _pair_softmax_out(scores_a, scores_b, values_a, values_b, sink_a, sink_b):
    lo = lax.broadcasted_iota(jnp.int32, (1, LANES), 1) < HALF
    od, sink_terms = None, []
    for scores, values, sink in ((scores_a, values_a, sink_a), (scores_b, values_b, sink_b)):
        mx = sink
        for s in scores:
            mx = jnp.maximum(mx, jnp.max(s, axis=-1, keepdims=True))
        for s, v in zip(scores, values):
            part = _dot(jnp.exp(s - mx).astype(BF16), v)
            od = part if od is None else od + part
        sink_terms.append(jnp.exp(sink - mx))
    den = od[:, LANES:] + jnp.where(lo, sink_terms[0], sink_terms[1])
    return od[:, :LANES] / den


def _pair_sinks(sink_ref, jb):
    return (sink_ref[:, jb * LANES:jb * LANES + 1], sink_ref[:, jb * LANES + HALF:jb * LANES + HALF + 1])


CTX_TILES = BATCH * C_KV_HEADS
PIPE_LAG = 2


def _ctx_tile(g, lag):
    t = jnp.clip(g - lag, 0, CTX_TILES - 1)
    return t // C_KV_HEADS, t % C_KV_HEADS


def _ctx_attn_body(q_ref, kt_ref, v_ref, sink_ref, o_ref, s_s, e_s, st_s):
    g = pl.program_id(0)
    lo = lax.broadcasted_iota(jnp.int32, (1, LANES), 1) < HALF
    n_pairs = C_GROUPS // 2

    @pl.when(g == 0)
    def _():
        s_s[...] = jnp.zeros_like(s_s)
        e_s[...] = jnp.zeros_like(e_s)
        st_s[...] = jnp.ones_like(st_s)

    def stages(cur, prev):
        v_top, v_bot = _pair_values(v_ref[...], _ctx_tile(g, 2)[1] % 2)
        for jb in range(n_pairs):
            od = _dot(e_s[cur, 2 * jb], v_top) + _dot(e_s[cur, 2 * jb + 1], v_bot)
            o_ref[:, jb * LANES:(jb + 1) * LANES] = (od[:, :LANES] / (od[:, LANES:] + st_s[cur, jb])).astype(BF16)
        for jb in range(n_pairs):
            terms = []
            for hh, sink in enumerate(_pair_sinks(sink_ref, jb)):
                s = s_s[prev, 2 * jb + hh]
                mx = jnp.maximum(jnp.max(s, axis=-1, keepdims=True), sink)
                e_s[prev, 2 * jb + hh] = jnp.exp(s - mx).astype(BF16)
                terms.append(jnp.exp(sink - mx))
            st_s[prev, jb] = jnp.where(lo, terms[0], terms[1])
        half = _ctx_tile(g, 0)[1] % 2
        kt = kt_ref[pl.ds(pl.multiple_of(half * C_HEAD_DIM, C_HEAD_DIM), C_HEAD_DIM), :]
        zero = jnp.zeros_like(kt)
        kt_a = jnp.concatenate([kt, zero], axis=0).astype(BF16)
        kt_b = jnp.concatenate([zero, kt], axis=0).astype(BF16)
        for jb in range(n_pairs):
            q = q_ref[:, jb * LANES:(jb + 1) * LANES]
            s_s[cur, 2 * jb] = _dot(q, kt_a)
            s_s[cur, 2 * jb + 1] = _dot(q, kt_b)

    pl.when(g % 2 == 0)(lambda: stages(0, 1))
    pl.when(g % 2 == 1)(lambda: stages(1, 0))


def _ctx_attn(q, kt, v, sink_cols):
    gw = C_GROUPS * C_HEAD_DIM

    def at(lag, fn):
        return lambda g: fn(*_ctx_tile(g, lag))

    return pl.pallas_call(
        _ctx_attn_body,
        grid=(CTX_TILES + PIPE_LAG,),
        in_specs=[
            pl.BlockSpec((SEQ, gw), at(0, lambda b, h: (b, h))),
            pl.BlockSpec((None, LANES, SEQ), at(0, lambda b, h: (b, h // 2, 0))),
            pl.BlockSpec((SEQ, LANES), at(2, lambda b, h: (b, h // 2))),
            pl.BlockSpec((1, gw), at(1, lambda b, h: (0, h))),
        ],
        out_specs=pl.BlockSpec((SEQ, gw), at(2, lambda b, h: (b, h))),
        out_shape=jax.ShapeDtypeStruct((TP, C_Q), BF16),
        scratch_shapes=[
            pltpu.VMEM((2, C_GROUPS, SEQ, SEQ), F32),
            pltpu.VMEM((2, C_GROUPS, SEQ, SEQ), BF16),
            pltpu.VMEM((2, C_GROUPS // 2, SEQ, LANES), F32),
        ],
        compiler_params=_params(1),
        name="c_attn_ctx",
    )(q, kt, v, sink_cols)


LAT_TQ = 128
LAT_WIN = LAT_TQ + 2 * WINDOW
LAT_NQ = DEC_SEQ // LAT_TQ
LAT_TILES = DEC_BATCH * C_KV_HEADS * LAT_NQ


def _lat_tile(g, lag):
    t = jnp.clip(g - lag, 0, LAT_TILES - 1)
    bh = t // LAT_NQ
    return bh // C_KV_HEADS, bh % C_KV_HEADS, t % LAT_NQ


def _lat_window(j):
    return pl.multiple_of(jnp.clip(j * LAT_TQ - WINDOW, 0, DEC_SEQ - LAT_WIN), LANES)


def _lat_attn_body(q_ref, k_ref, v_ref, ck_ref, cv_ref, sink_ref, o_ref,
                   ka_s, kb_s, cka_s, ckb_s, va_s, vb_s, cva_s, cvb_s, sl_s, sc_s, el_s, ec_s, st_s):
    g = pl.program_id(0)
    lo = lax.broadcasted_iota(jnp.int32, (1, LANES), 1) < HALF
    n_pairs = C_GROUPS // 2
    b_a, h_a, j_a = _lat_tile(g, 0)
    b_c, h_c, j_c = _lat_tile(g, 2)
    v_slot_a = (b_a * C_KV_HEADS + h_a) % 2
    v_slot_c = (b_c * C_KV_HEADS + h_c) % 2

    @pl.when(g == 0)
    def _():
        sl_s[...] = jnp.zeros_like(sl_s)
        sc_s[...] = jnp.zeros_like(sc_s)
        el_s[...] = jnp.zeros_like(el_s)
        ec_s[...] = jnp.zeros_like(ec_s)
        st_s[...] = jnp.ones_like(st_s)

    @pl.when(jnp.logical_and(j_a == 0, g < LAT_TILES))
    def _():
        half = h_a % 2
        for src, a_s, b_s in ((k_ref, ka_s, kb_s), (ck_ref, cka_s, ckb_s)):
            lo2 = lax.broadcasted_iota(jnp.int32, src.shape, 1) < HALF
            k2 = _dup_head(src[...].astype(F32), half)
            a_s[...] = jnp.where(lo2, k2, 0.0).astype(BF16)
            b_s[...] = jnp.where(lo2, 0.0, k2).astype(BF16)
        va_s[v_slot_a], vb_s[v_slot_a] = _pair_values(v_ref[...], half)
        cva_s[v_slot_a], cvb_s[v_slot_a] = _pair_values(cv_ref[...], half)

    def stages(cur, prev):
        win_c = pl.ds(_lat_window(j_c), LAT_WIN)
        v_loc = (va_s[v_slot_c, win_c, :], vb_s[v_slot_c, win_c, :])
        v_ctx = (cva_s[v_slot_c], cvb_s[v_slot_c])
        for jb in range(n_pairs):
            od = None
            for hh in range(2):
                part = _dot(el_s[cur, 2 * jb + hh], v_loc[hh]) + _dot(ec_s[cur, 2 * jb + hh], v_ctx[hh])
                od = part if od is None else od + part
            o_ref[:, jb * LANES:(jb + 1) * LANES] = (od[:, :LANES] / (od[:, LANES:] + st_s[cur, jb])).astype(BF16)
        for jb in range(n_pairs):
            terms = []
            for hh, sink in enumerate(_pair_sinks(sink_ref, jb)):
                s_loc, s_ctx = sl_s[prev, 2 * jb + hh], sc_s[prev, 2 * jb + hh]
                mx = jnp.maximum(jnp.maximum(jnp.max(s_loc, axis=-1, keepdims=True),
                                             jnp.max(s_ctx, axis=-1, keepdims=True)), sink)
                el_s[prev, 2 * jb + hh] = jnp.exp(s_loc - mx).astype(BF16)
                ec_s[prev, 2 * jb + hh] = jnp.exp(s_ctx - mx).astype(BF16)
                terms.append(jnp.exp(sink - mx))
            st_s[prev, jb] = jnp.where(lo, terms[0], terms[1])
        start = _lat_window(j_a)
        win = pl.ds(start, LAT_WIN)
        qpos = j_a * LAT_TQ + lax.broadcasted_iota(jnp.int32, (LAT_TQ, LAT_WIN), 0)
        kpos = start + lax.broadcasted_iota(jnp.int32, (LAT_TQ, LAT_WIN), 1)
        band = jnp.abs(qpos - kpos) <= WINDOW
        for jb in range(n_pairs):
            q = q_ref[:, jb * LANES:(jb + 1) * LANES]
            for hh, (kl, kc) in enumerate(((ka_s, cka_s), (kb_s, ckb_s))):
                sl_s[cur, 2 * jb + hh] = jnp.where(band, _dot_nt(q, kl[win, :]), -jnp.inf)
                sc_s[cur, 2 * jb + hh] = _dot_nt(q, kc[...])

    pl.when(g % 2 == 0)(lambda: stages(0, 1))
    pl.when(g % 2 == 1)(lambda: stages(1, 0))


def _lat_attn(q, k, v, ck, cv, sink_cols):
    gw = C_GROUPS * C_HEAD_DIM
    q_off = TP // LAT_TQ
    kv_off = TP // DEC_SEQ

    def at(lag, fn):
        return lambda g: fn(*_lat_tile(g, lag))

    return pl.pallas_call(
        _lat_attn_body,
        grid=(LAT_TILES + PIPE_LAG,),
        in_specs=[
            pl.BlockSpec((LAT_TQ, gw), at(0, lambda b, h, j: (q_off + b * LAT_NQ + j, h))),
            pl.BlockSpec((DEC_SEQ, LANES), at(0, lambda b, h, j: (b, h // 2))),
            pl.BlockSpec((DEC_SEQ, LANES), at(0, lambda b, h, j: (kv_off + b, h // 2))),
            pl.BlockSpec((PAST_LEN, LANES), at(0, lambda b, h, j: (b, h // 2))),
            pl.BlockSpec((PAST_LEN, LANES), at(0, lambda b, h, j: (b, h // 2))),
            pl.BlockSpec((1, gw), at(1, lambda b, h, j: (0, h))),
        ],
        out_specs=pl.BlockSpec((LAT_TQ, gw), at(2, lambda b, h, j: (b * LAT_NQ + j, h))),
        out_shape=jax.ShapeDtypeStruct((TS, C_Q), BF16),
        scratch_shapes=[
            pltpu.VMEM((DEC_SEQ, LANES), BF16), pltpu.VMEM((DEC_SEQ, LANES), BF16),
            pltpu.VMEM((PAST_LEN, LANES), BF16), pltpu.VMEM((PAST_LEN, LANES), BF16),
            pltpu.VMEM((2, DEC_SEQ, 2 * LANES), BF16), pltpu.VMEM((2, DEC_SEQ, 2 * LANES), BF16),
            pltpu.VMEM((2, PAST_LEN, 2 * LANES), BF16), pltpu.VMEM((2, PAST_LEN, 2 * LANES), BF16),
            pltpu.VMEM((2, C_GROUPS, LAT_TQ, LAT_WIN), F32), pltpu.VMEM((2, C_GROUPS, LAT_TQ, PAST_LEN), F32),
            pltpu.VMEM((2, C_GROUPS, LAT_TQ, LAT_WIN), BF16), pltpu.VMEM((2, C_GROUPS, LAT_TQ, PAST_LEN), BF16),
            pltpu.VMEM((2, C_GROUPS // 2, LAT_TQ, LANES), F32),
        ],
        compiler_params=_params(1),
        name="c_attn_latent",
    )(q, k, v, ck, cv, sink_cols)


def kernel(x_prompt, x_sample, state_hgrn, state_gla, cache_k, cache_v, c, c_ctx, w_mod, b_mod, ln_g, ln_b,
           ffn_w1, ffn_w3, ffn_w2, w_in_ab, hgrn_lb, gla_gate_up, gla_gate_b, norm_a, norm_b, w_out_ab,
           w_qkv_c, sink_c, w_out_c):
    cs = jnp.zeros((8, D), F32).at[0].set(c_ctx).at[1:1 + DEC_BATCH].set(c)
    mod = _mod_vectors(cs, w_mod, b_mod).reshape(DEPTH, 8, 1, N_MOD * D)
    ffn_ws = (ffn_w1, ffn_w3, ffn_w2)
    ln_g, ln_b = ln_g.reshape(DEPTH, 3, 1, D), ln_b.reshape(DEPTH, 3, 1, D)

    def ffn(xs, ws, layer, sub, split_out=False, cast_next=None):
        casts = None if cast_next is None else (ffn_ws, cast_next)
        return _ffn_sublayer(xs, mod, *ws, ln_g, ln_b, layer, sub, split_out=split_out, casts=casts)

    x, *ws_01 = ffn([x_prompt.reshape(TP, D), x_sample.reshape(TS, D)], [w[0, 0].astype(BF16) for w in ffn_ws],
                    0, 0, cast_next=(0, 1))
    w_in = w_in_ab[0]
    o_aq, o_ai, o_ff, o_fb, o_ag = 0, A_W, 2 * A_W, 3 * A_W, 4 * A_W
    o_bq = 5 * A_W
    o_bk, o_bv = o_bq + B_QK, o_bq + 2 * B_QK
    o_bg = o_bv + B_V
    o_z = o_bg + B_V
    order = [(o_aq, A_W), (o_ff, A_W), (o_fb, A_W), (o_ag, A_W), (o_bq, B_QK), (o_bk, B_QK), (o_bg, B_V),
             (o_ai, A_W), (o_bv, B_V)]
    wmain = jnp.concatenate([w_in[:, o:o + w] for o, w in order], axis=1).astype(BF16)
    wz = jnp.pad(w_in[:, o_z:o_z + 2 * GATE_RANK], ((0, 0), (0, LANES - 2 * GATE_RANK))).astype(BF16)
    gup = jnp.zeros((LANES, 2 * B_QK), F32)
    gup = gup.at[:GATE_RANK, :B_QK].set(gla_gate_up[0, 0]).at[GATE_RANK:2 * GATE_RANK, B_QK:].set(gla_gate_up[0, 1])
    gb = gla_gate_b[0].reshape(1, 2 * B_QK)
    pf, pb = _inproj(x, mod, wmain, wz, gup.astype(BF16), gb, hgrn_lb, 0, 0)

    s0_a = state_hgrn[:, 0]
    s0_b = state_gla[:, 0].reshape(DEC_BATCH, 2, B_HEADS // 2, LANES, B_DV)
    oa_p, st_a = _scan(pf, pb, norm_a[0], None, prompt=True, pair=False)
    ob_p, st_b = _scan(pf, pb, norm_b[0], None, prompt=True, pair=True)
    (oa_s,) = _scan(pf, pb, norm_a[0], s0_a, prompt=False, pair=False)
    (ob_s,) = _scan(pf, pb, norm_b[0], s0_b, prompt=False, pair=True)
    w_out = w_out_ab[0].astype(BF16)
    x = _outproj([(oa_p, oa_s), (ob_p, ob_s)], [w_out[:A_W], w_out[A_W:]], x, mod, ln_g, ln_b, 0)
    x, *ws_10 = ffn([x], ws_01, 0, 1, cast_next=(1, 0))
    new_hgrn = st_a.reshape(BATCH, 1, 2, A_HEADS, A_DK, A_DV)
    new_gla = st_b.reshape(BATCH, 1, 2, B_HEADS, B_DK, B_DV)

    x, *ws_11 = ffn([x], ws_10, 1, 0, cast_next=(1, 1))
    cos, sin = _rope_tables()
    w_qkv = w_qkv_c[0].astype(BF16)
    q, k, v, kt, vt = _qkv(x, mod, w_qkv, w_qkv[:, C_Q:].T, cos, sin, 1)
    sink_cols = jnp.repeat(sink_c[0], C_HEAD_DIM).reshape(1, C_Q)
    o_p = _ctx_attn(q, kt, v, sink_cols)
    ck = cache_k[:, 0].reshape(DEC_BATCH * PAST_LEN, C_KV)
    cv = cache_v[:, 0].reshape(DEC_BATCH * PAST_LEN, C_KV)
    o_s = _lat_attn(q, k, v, ck, cv, sink_cols)
    x = _outproj([(o_p, o_s)], [w_out_c[0].astype(BF16)], x, mod, ln_g, ln_b, 1)
    y_p, y_s = ffn([x], ws_11, 1, 1, split_out=True)

    def cache_layout(zt):
        return zt.reshape(BATCH, 1, C_KV_HEADS, C_HEAD_DIM, SEQ).transpose(0, 1, 4, 2, 3)

    new_k, new_v = cache_layout(kt), cache_layout(vt)

    return (y_p.reshape(BATCH, SEQ, D), y_s.reshape(DEC_BATCH, DEC_SEQ, D), new_hgrn, new_gla, new_k, new_v)
```

```python
import functools
import math

import jax
import jax.numpy as jnp
import numpy as np
from jax import lax
from jax.experimental import pallas as pl
from jax.experimental.pallas import tpu as pltpu

D = 1024
BATCH, SEQ = 16, 256
DEC_BATCH, DEC_SEQ = 2, 2048
PAST_LEN = 512
GRID_W = 64
D_FF = 2816
N_MOD = 9
A_HEADS, A_DK, A_DV = 4, 128, 128
A_W = A_HEADS * A_DK
B_HEADS, B_DK, B_DV = 4, 64, 128
B_QK = B_HEADS * B_DK
B_V = B_HEADS * B_DV
GATE_RANK = 16
GLA_TAU = 16.0
CHUNK = 128
C_HEADS, C_KV_HEADS, C_HEAD_DIM = 16, 4, 64
C_GROUPS = C_HEADS // C_KV_HEADS
C_Q = C_HEADS * C_HEAD_DIM
C_KV = C_KV_HEADS * C_HEAD_DIM
WINDOW = 128
ROPE_FREQS = C_HEAD_DIM // 4
ROPE_BASE = 10000.0
DEPTH = 2
ALPHA = (2.0 * DEPTH) ** 0.25
LN_EPS = 1e-5
RMS_EPS = 1e-6

TP = BATCH * SEQ
TS = DEC_BATCH * DEC_SEQ
T = TP + TS
N_SEG = 1 + DEC_BATCH

LANES = 128
HALF = LANES // 2
FFN_TM = 1024
FFN_TM_SPLIT = 512
FFN_SUB = 256
PROJ_TM = SEQ
INPROJ_TM = 512
PROJ_SUB = 256
OUT_TM = 512
VMEM_LIMIT = 56 * 1024 * 1024

F32 = jnp.float32
BF16 = jnp.bfloat16


def _dot(a, b):
    return jnp.dot(a, b, preferred_element_type=F32)


def _dot_nt(a, b):
    return lax.dot_general(a, b, (((1,), (1,)), ((), ())), preferred_element_type=F32)


def _dot_tn(a, b):
    return lax.dot_general(a, b, (((0,), (0,)), ((), ())), preferred_element_type=F32)


def _silu(x):
    return x * jax.nn.sigmoid(x)


def _layer_norm(z, g, b):
    mu = jnp.mean(z, axis=-1, keepdims=True)
    zc = z - mu
    var = jnp.mean(zc * zc, axis=-1, keepdims=True)
    return zc * lax.rsqrt(var + LN_EPS) * g + b


def _seg_of_tile(i, tm):
    n_p = TP // tm
    n_s = DEC_SEQ // tm
    return jnp.where(i < n_p, 0, 1 + lax.div(jnp.maximum(i - n_p, 0), n_s))


def _params(n_axes):
    return pltpu.CompilerParams(dimension_semantics=("arbitrary",) * n_axes, vmem_limit_bytes=VMEM_LIMIT)


def _resident(shape):
    nd = len(shape)
    return pl.BlockSpec(shape, lambda *_: (0,) * nd, pipeline_mode=pl.Buffered(1))


def _resident_slice(shape, lead):
    block = (None,) * len(lead) + tuple(shape)
    return pl.BlockSpec(block, lambda *_: tuple(lead) + (0,) * len(shape), pipeline_mode=pl.Buffered(1))


def _mod_specs(layer, cols, tm):
    return [pl.BlockSpec((None, None, 1, D), functools.partial(
        lambda i, c: (layer, _seg_of_tile(i, tm), 0, c), c=c)) for c in cols]


def _ln_specs(layer, idx):
    return [_resident_slice((1, D), (layer, idx))] * 2


BF16_SUBLANES = 16


def _cast_plan(ws, lead, n_steps, step_of):
    in_specs, out_specs, out_shapes = [], [], []
    for w in ws:
        rows, cols = w.shape[len(lead):]
        blk = next(b for b in range(BF16_SUBLANES, rows + 1, BF16_SUBLANES)
                   if rows % b == 0 and rows // b <= n_steps)
        last = rows // blk - 1
        in_specs.append(pl.BlockSpec((None,) * len(lead) + (blk, cols), functools.partial(
            lambda *g, last: tuple(lead) + (jnp.minimum(step_of(*g), last), 0), last=last)))
        out_specs.append(pl.BlockSpec((blk, cols), functools.partial(
            lambda *g, last: (jnp.minimum(step_of(*g), last), 0), last=last)))
        out_shapes.append(jax.ShapeDtypeStruct((rows, cols), BF16))
    return in_specs, out_specs, out_shapes


def _hosting_casts(body, n_in, n_out, n_cast):
    def hosted(*refs, **kw):
        ins, refs = refs[:n_in], refs[n_in:]
        cast_in, refs = refs[:n_cast], refs[n_cast:]
        outs, refs = refs[:n_out], refs[n_out:]
        cast_out, scratch = refs[:n_cast], refs[n_cast:]
        for src, dst in zip(cast_in, cast_out):
            dst[...] = src[...].astype(BF16)
        body(*ins, *outs, *scratch, **kw)
    return hosted


def _mod_body(c_ref, w_ref, b_ref, o_ref):
    c = c_ref[...]
    s = _silu(c).astype(BF16)
    o_ref[0] = _dot(s, w_ref[0].astype(BF16)) + b_ref[0]


def _mod_vectors(cs, w_mod, b_mod):
    tn = 1536
    n = N_MOD * D
    return pl.pallas_call(
        _mod_body,
        grid=(DEPTH, n // tn),
        in_specs=[
            pl.BlockSpec((8, D), lambda l, j: (0, 0)),
            pl.BlockSpec((1, D, tn), lambda l, j: (l, 0, j)),
            pl.BlockSpec((1, 1, tn), lambda l, j: (l, 0, j)),
        ],
        out_specs=pl.BlockSpec((1, 8, tn), lambda l, j: (l, 0, j)),
        out_shape=jax.ShapeDtypeStruct((DEPTH, 8, n), F32),
        compiler_params=_params(2),
        name="mod_vectors",
    )(cs, w_mod, b_mod.reshape(DEPTH, 1, n))


def _ffn_body(*refs, n_x, n_o, tm):
    x_refs = refs[:n_x]
    shift_ref, scale_ref, gate_ref, w1_ref, w3_ref, w2_ref, g_ref, b_ref = refs[n_x:n_x + 8]
    o_refs = refs[n_x + 8:]

    def compute(x_ref, o_ref):
        shift, scale, gate = shift_ref[...], scale_ref[...], gate_ref[...]
        for r in range(0, tm, FFN_SUB):
            rows = slice(r, r + FFN_SUB)
            x = x_ref[rows, :]
            h = (x * (1.0 + scale) + shift).astype(BF16)
            a = _dot(h, w1_ref[...])
            b = _dot(h, w3_ref[...])
            g = (_silu(a) * b).astype(BF16)
            y = _dot(g, w2_ref[...])
            z = ALPHA * x + (0.5 * gate) * y
            o_ref[rows, :] = _layer_norm(z, g_ref[...], b_ref[...])

    if n_x == 1 and n_o == 1:
        compute(x_refs[0], o_refs[0])
    else:
        in_prompt = pl.program_id(0) < TP // tm
        pl.when(in_prompt)(lambda: compute(x_refs[0], o_refs[0]))
        pl.when(jnp.logical_not(in_prompt))(lambda: compute(x_refs[-1], o_refs[-1]))


def _group_specs(split, tm, width=D):
    if not split:
        return [pl.BlockSpec((tm, width), lambda i: (i, 0))]
    n_p = TP // tm
    return [pl.BlockSpec((tm, width), lambda i: (jnp.minimum(i, n_p - 1), 0)),
            pl.BlockSpec((tm, width), lambda i: (jnp.maximum(i - n_p, 0), 0))]


def _ffn_sublayer(xs, mod, w1, w3, w2, ln_g, ln_b, layer, sub, split_out=False, casts=None):
    n_x, n_o = len(xs), 2 if split_out else 1
    tm = FFN_TM if n_x == n_o == 1 else FFN_TM_SPLIT
    out_shape = ([jax.ShapeDtypeStruct((TP, D), F32), jax.ShapeDtypeStruct((TS, D), F32)] if split_out
                 else [jax.ShapeDtypeStruct((T, D), F32)])
    mod_lo = 6 * sub
    body = functools.partial(_ffn_body, n_x=n_x, n_o=n_o, tm=tm)
    in_specs = _group_specs(n_x == 2, tm) + _mod_specs(layer, (mod_lo, mod_lo + 1, mod_lo + 2), tm) + [
        _resident((D, D_FF)),
        _resident((D, D_FF)),
        _resident((D_FF, D)),
    ] + _ln_specs(layer, 2 * sub)
    out_specs = _group_specs(split_out, tm)
    args = [*xs, mod, mod, mod, w1, w3, w2, ln_g, ln_b]
    if casts is not None:
        ws, lead = casts
        c_in, c_out, c_shapes = _cast_plan(ws, lead, T // tm, lambda i: i)
        body = _hosting_casts(body, len(in_specs), len(out_specs), len(ws))
        in_specs, out_specs, out_shape = in_specs + c_in, out_specs + c_out, out_shape + c_shapes
        args = args + list(ws)
    return pl.pallas_call(
        body,
        grid=(T // tm,),
        in_specs=in_specs,
        out_specs=out_specs,
        out_shape=out_shape,
        compiler_params=_params(1),
        name="ffn_sublayer",
    )(*args)


PF_AQ, PF_FF, PF_FB, PF_AG = 0, 512, 1024, 1536
PF_BQ, PF_BK, PF_BG, PF_LAF, PF_LAB = 2048, 2304, 2560, 3072, 3328
PF_W = 3584
PB_AV, PB_BV = 0, 512
PB_W = 1024
WM_AQ, WM_FF, WM_FB, WM_AG, WM_BQ, WM_BK, WM_BG, WM_AI, WM_BV = 0, 512, 1024, 1536, 2048, 2304, 2560, 3072, 3584
WM_W = 4096


def _log_sigmoid(x):
    return jnp.minimum(x, 0.0) - jnp.log(1.0 + jnp.exp(-jnp.abs(x)))


def _inproj_body(x_ref, shift_ref, scale_ref, w_ref, wz_ref, gu_ref, gb_ref, lb_ref, pf_ref, pb_ref, *, layer_e):
    def lower_bound(d):
        l = lb_ref[d]
        e = jnp.exp(l - jnp.max(l, axis=0, keepdims=True))
        sm = e / jnp.sum(e, axis=0, keepdims=True)
        return jnp.sum(sm[:layer_e + 1], axis=0, keepdims=True)

    lbs = [lower_bound(0), lower_bound(1)]
    for r in range(0, x_ref.shape[0], PROJ_SUB):
        rows = slice(r, r + PROJ_SUB)
        h = (x_ref[rows, :] * (1.0 + scale_ref[...]) + shift_ref[...]).astype(BF16)

        def proj(off, width):
            return _dot(h, w_ref[:, off:off + width])

        pf_ref[rows, PF_AQ:PF_AQ + A_W] = proj(WM_AQ, A_W)
        for lb, wm, pf in ((lbs[0], WM_FF, PF_FF), (lbs[1], WM_FB, PF_FB)):
            pf_ref[rows, pf:pf + A_W] = lb + (1.0 - lb) * jax.nn.sigmoid(proj(wm, A_W))
        pf_ref[rows, PF_AG:PF_AG + A_W] = _silu(proj(WM_AG, A_W))
        pf_ref[rows, PF_BQ:PF_BQ + B_QK] = proj(WM_BQ, B_QK) * (B_DK ** -0.5)
        pf_ref[rows, PF_BK:PF_BK + B_QK] = proj(WM_BK, B_QK)
        pf_ref[rows, PF_BG:PF_BG + B_V] = _silu(proj(WM_BG, B_V))
        pb_ref[rows, PB_AV:PB_AV + A_W] = _silu(proj(WM_AI, A_W)).astype(BF16)
        pb_ref[rows, PB_BV:PB_BV + B_V] = proj(WM_BV, B_V).astype(BF16)
        z = _dot(h, wz_ref[...]).astype(BF16)
        pre = _dot(z, gu_ref[...]) + gb_ref[...]
        pf_ref[rows, PF_LAF:PF_LAF + 2 * B_QK] = _log_sigmoid(pre) * (1.0 / GLA_TAU)


def _inproj(x, mod, wmain, wz, gup, gb, hgrn_lb, layer, layer_e, tm=INPROJ_TM):
    n_l = hgrn_lb.shape[1]
    return pl.pallas_call(
        functools.partial(_inproj_body, layer_e=layer_e),
        grid=(T // tm,),
        in_specs=[pl.BlockSpec((tm, D), lambda i: (i, 0))] + _mod_specs(layer, (3, 4), tm) + [
            _resident((D, WM_W)),
            _resident((D, LANES)),
            _resident((LANES, 2 * B_QK)),
            _resident((1, 2 * B_QK)),
            _resident((2, n_l, A_W)),
        ],
        out_specs=[
            pl.BlockSpec((tm, PF_W), lambda i: (i, 0)),
            pl.BlockSpec((tm, PB_W), lambda i: (i, 0)),
        ],
        out_shape=[jax.ShapeDtypeStruct((T, PF_W), F32), jax.ShapeDtypeStruct((T, PB_W), BF16)],
        compiler_params=_params(1),
        name="ab_inproj",
    )(x, mod, mod, wmain, wz, gup, gb, hgrn_lb)


SCAN_PROMPT_SEQS = 4
SCAN_UNROLL = 4


def _prefix_rows(x):
    row = lax.broadcasted_iota(jnp.int32, x.shape, 0)
    s = 1
    while s < x.shape[0]:
        x = x + jnp.where(row >= s, pltpu.roll(x, s, axis=0), 0.0)
        s *= 2
    return x


def _scan_body(*refs, seq_len, seqs, pair, has_s0, emit_state):
    n = seq_len // CHUNK
    n_all = seqs * n
    nh = 2 if pair else 1
    it = iter(refs)
    q_ref = next(it)
    if pair:
        k_ref, laf_ref, lab_ref = next(it), next(it), next(it)
    else:
        ff_ref, fb_ref = next(it), next(it)
    g_ref, v_ref, nw_ref = next(it), next(it), next(it)
    s0_ref = next(it) if has_s0 else None
    o_ref = next(it)
    st_ref = next(it) if emit_state else None
    qd_s, oi_s, kv_s, dec_s, sb_s = it

    row = lax.broadcasted_iota(jnp.int32, (CHUNK, CHUNK), 0)
    col = lax.broadcasted_iota(jnp.int32, (CHUNK, CHUNK), 1)
    tril = row >= col
    triu = row <= col
    lane = lax.broadcasted_iota(jnp.int32, (1, LANES), 1)
    lane2 = lax.broadcasted_iota(jnp.int32, (1, 2 * LANES), 1)
    if pair:
        masks = [lane < HALF, lane >= HALF]
        masks2 = [(lane2 % LANES) < HALF, (lane2 % LANES) >= HALF]
    else:
        masks, masks2 = [None], [None]

    def pick(mask, x):
        return x if mask is None else jnp.where(mask, x, jnp.zeros_like(x))

    def rows_of(c):
        return pl.ds(pl.multiple_of(c * CHUNK, CHUNK), CHUNK)

    def loop(body):
        if n_all <= SCAN_UNROLL:
            for c in range(n_all):
                body(c)
        else:
            def fbody(i, carry):
                for u in range(SCAN_UNROLL):
                    body(i * SCAN_UNROLL + u)
                return carry
            lax.fori_loop(0, n_all // SCAN_UNROLL, fbody, 0)

    def phase1(c):
        rows = rows_of(c)
        q = q_ref[rows, :]
        if pair:
            k_f = k_b = k_ref[rows, :]
            la_f, la_b = laf_ref[rows, :], lab_ref[rows, :]
        else:
            f_f, f_b = ff_ref[rows, :], fb_ref[rows, :]
            k_f, k_b = 1.0 - f_f, 1.0 - f_b
            la_f, la_b = jnp.log(f_f), jnp.log(f_b)
        cs = _prefix_rows(jnp.concatenate([la_f, la_b], axis=1))
        cf, cbi = cs[:, :LANES], cs[:, LANES:]
        tot_f, tot_b = cf[CHUNK - 1:CHUNK, :], cbi[CHUNK - 1:CHUNK, :]
        rb = tot_b - cbi + la_b
        ref_f, ref_b = cf[CHUNK // 2 - 1:CHUNK // 2, :], rb[CHUNK // 2:CHUNK // 2 + 1, :]
        qtf = q * jnp.exp(cf - ref_f)
        qtb = q * jnp.exp(rb - ref_b)
        ktf = k_f * jnp.exp(ref_f - cf)
        ktb = k_b * jnp.exp(ref_b - rb)
        qd = jnp.concatenate([qtf * jnp.exp(ref_f), qtb * jnp.exp(ref_b)], axis=1).astype(BF16)
        ku = jnp.concatenate([ktf * jnp.exp(tot_f - ref_f), ktb * jnp.exp(tot_b - ref_b)], axis=1).astype(BF16)
        qd_s[rows, :] = qd
        qt = jnp.concatenate([qtf, qtb], axis=0).astype(BF16)
        kt = jnp.concatenate([ktf, ktb], axis=0).astype(BF16)
        kv = None
        for hh in range(nh):
            v = v_ref[rows, hh * LANES:(hh + 1) * LANES]
            sc = _dot_nt(pick(masks[hh], qt), kt)
            att = jnp.where(tril, sc[:CHUNK, :CHUNK], 0.0) + jnp.where(triu, sc[CHUNK:, CHUNK:], 0.0)
            oi_s[rows, hh * LANES:(hh + 1) * LANES] = _dot(att.astype(BF16), v)
            kv_h = _dot_tn(v, ku)
            kv = kv_h if kv is None else jnp.where(masks2[0], kv, kv_h)
        kv_s[c] = kv
        dec_s[c] = jnp.exp(jnp.concatenate([tot_f, tot_b], axis=1))

    loop(phase1)

    def recurrence(sq, d, reverse):
        cols = slice(d * LANES, (d + 1) * LANES)
        c0 = sq * n
        st0 = s0_ref[sq, d, 0].T if has_s0 else jnp.zeros((LANES, LANES), F32)

        def step(c, st):
            sb_s[c, :, cols] = st.astype(BF16)
            return st * dec_s[c, :, cols] + kv_s[c, :, cols]

        if n <= 8:
            st = st0
            for c in (range(n - 1, -1, -1) if reverse else range(n)):
                st = step(c0 + c, st)
        else:
            st = lax.fori_loop(0, n, lambda i, st: step(c0 + (n - 1 - i if reverse else i), st), st0)
        if emit_state:
            st_ref[sq, d, 0] = st.T

    for sq in range(seqs):
        recurrence(sq, 0, False)
        recurrence(sq, 1, True)

    nw = nw_ref[...]

    def phase2(c):
        rows = rows_of(c)
        qcat = qd_s[rows, :]
        scat = sb_s[c]
        for hh in range(nh):
            cols = slice(hh * LANES, (hh + 1) * LANES)
            o = oi_s[rows, cols] + _dot_nt(pick(masks2[hh], qcat), scat)
            o = o * lax.rsqrt(jnp.mean(o * o, axis=-1, keepdims=True) + RMS_EPS) * nw
            o_ref[rows, cols] = (o * g_ref[rows, cols]).astype(BF16)

    loop(phase2)


def _scan(pf, pb, norm_w, s0, *, prompt, pair):
    seq_len = SEQ if prompt else DEC_SEQ
    nseq = BATCH if prompt else DEC_BATCH
    seqs = SCAN_PROMPT_SEQS if prompt else 1
    rows = seqs * seq_len
    row_off = 0 if prompt else TP // rows
    units = B_HEADS // 2 if pair else A_HEADS
    nh = 2 if pair else 1
    n_all = rows // CHUNK
    has_s0 = s0 is not None
    emit_state = prompt

    def colspec(off, width=LANES):
        base = off // width
        return pl.BlockSpec((rows, width), lambda s, u: (s + row_off, base + u))

    if pair:
        in_specs = [colspec(PF_BQ), colspec(PF_BK), colspec(PF_LAF), colspec(PF_LAB),
                    colspec(PF_BG, 2 * LANES), colspec(PB_BV, 2 * LANES)]
        args = [pf, pf, pf, pf, pf, pb]
    else:
        in_specs = [colspec(PF_AQ), colspec(PF_FF), colspec(PF_FB), colspec(PF_AG), colspec(PB_AV)]
        args = [pf, pf, pf, pf, pb]
    in_specs.append(pl.BlockSpec((1, LANES), lambda s, u: (0, 0)))
    args.append(norm_w.reshape(1, LANES))
    state_spec = pl.BlockSpec((seqs, 2, 1, LANES, LANES), lambda s, u: (s, 0, u, 0, 0))
    if has_s0:
        in_specs.append(state_spec)
        args.append(s0)
    out_specs = [pl.BlockSpec((rows, nh * LANES), lambda s, u: (s, u))]
    out_shape = [jax.ShapeDtypeStruct((nseq * seq_len, units * nh * LANES), BF16)]
    if emit_state:
        out_specs.append(state_spec)
        out_shape.append(jax.ShapeDtypeStruct((nseq, 2, units, LANES, LANES), F32))
    scratch = [
        pltpu.VMEM((rows, 2 * LANES), BF16),
        pltpu.VMEM((rows, nh * LANES), F32),
        pltpu.VMEM((n_all, LANES, 2 * LANES), F32),
        pltpu.VMEM((n_all, 1, 2 * LANES), F32),
        pltpu.VMEM((n_all, LANES, 2 * LANES), BF16),
    ]
    return pl.pallas_call(
        functools.partial(_scan_body, seq_len=seq_len, seqs=seqs, pair=pair, has_s0=has_s0,
                          emit_state=emit_state),
        grid=(nseq // seqs, units),
        in_specs=in_specs,
        out_specs=out_specs,
        out_shape=out_shape,
        scratch_shapes=scratch,
        compiler_params=_params(2),
        name=f"scan_{'p' if prompt else 's'}_{'gla' if pair else 'hgrn'}",
    )(*args)


def _outproj_body(*refs, n_lhs, tm):
    lhs = refs[:2 * n_lhs]
    ws = refs[2 * n_lhs:3 * n_lhs]
    x_ref, m_ref, g_ref, b_ref, o_ref = refs[3 * n_lhs:]

    def compute(group):
        y = _dot(lhs[group][...], ws[0][...])
        for j in range(1, n_lhs):
            y = y + _dot(lhs[2 * j + group][...], ws[j][...])
        z = ALPHA * x_ref[...] + m_ref[...] * y
        o_ref[...] = _layer_norm(z, g_ref[...], b_ref[...])

    in_prompt = pl.program_id(0) < TP // tm
    pl.when(in_prompt)(lambda: compute(0))
    pl.when(jnp.logical_not(in_prompt))(lambda: compute(1))


def _outproj(lhs, ws, x, mod, ln_g, ln_b, layer, tm=OUT_TM):
    n_lhs = len(lhs)
    in_specs, args = [], []
    for a_p, a_s in lhs:
        in_specs += _group_specs(True, tm, a_p.shape[1])
        args += [a_p, a_s]
    in_specs += [_resident(w.shape) for w in ws]
    in_specs += [pl.BlockSpec((tm, D), lambda i: (i, 0))] + _mod_specs(layer, (5,), tm) + _ln_specs(layer, 1)
    return pl.pallas_call(
        functools.partial(_outproj_body, n_lhs=n_lhs, tm=tm),
        grid=(T // tm,),
        in_specs=in_specs,
        out_specs=pl.BlockSpec((tm, D), lambda i: (i, 0)),
        out_shape=jax.ShapeDtypeStruct((T, D), F32),
        compiler_params=_params(1),
        name="mixer_outproj",
    )(*args, *ws, x, mod, ln_g, ln_b)


def _rope_partner(x):
    lane = lax.broadcasted_iota(jnp.int32, x.shape, 1)
    first_half = (lane % (2 * ROPE_FREQS)) < ROPE_FREQS
    return jnp.where(first_half, pltpu.roll(x, LANES - ROPE_FREQS, axis=1), pltpu.roll(x, ROPE_FREQS, axis=1))


def _qkv_body(x_ref, shift_ref, scale_ref, w_ref, wkvt_ref, cos_ref, sin_ref,
              q_ref, k_ref, v_ref, kt_ref, vt_ref):
    h = (x_ref[...] * (1.0 + scale_ref[...]) + shift_ref[...]).astype(BF16)
    qscale = C_HEAD_DIM ** -0.5
    v_ref[...] = _dot(h, w_ref[:, C_Q + C_KV:]).astype(BF16)
    in_prompt = pl.program_id(0) < TP // PROJ_TM

    @pl.when(in_prompt)
    def _():
        q_ref[...] = (_dot(h, w_ref[:, :C_Q]) * qscale).astype(BF16)
        kt_ref[...] = _dot_nt(wkvt_ref[:C_KV, :], h)
        vt_ref[...] = _dot_nt(wkvt_ref[C_KV:, :], h)

    @pl.when(jnp.logical_not(in_prompt))
    def _():
        cos, sin = cos_ref[...], sin_ref[...]
        zq = _dot(h, w_ref[:, :C_Q])
        zk = _dot(h, w_ref[:, C_Q:C_Q + C_KV])

        def rope(z):
            return z * cos + _rope_partner(z) * sin

        for j in range(C_Q // LANES):
            cols = slice(j * LANES, (j + 1) * LANES)
            q_ref[:, cols] = (rope(zq[:, cols]) * qscale).astype(BF16)
        for j in range(C_KV // LANES):
            cols = slice(j * LANES, (j + 1) * LANES)
            k_ref[:, cols] = rope(zk[:, cols]).astype(BF16)


def _qkv(x, mod, w, wkvt, cos, sin, layer):
    tm = PROJ_TM
    n_p = TP // tm
    lat = lambda i: (jnp.maximum(i - n_p, 0), 0)
    ctx = lambda i: (jnp.minimum(i, n_p - 1), 0, 0)
    return pl.pallas_call(
        _qkv_body,
        grid=(T // tm,),
        in_specs=[pl.BlockSpec((tm, D), lambda i: (i, 0))] + _mod_specs(layer, (3, 4), tm) + [
            _resident((D, C_Q + 2 * C_KV)),
            _resident((2 * C_KV, D)),
            pl.BlockSpec((tm, LANES), lat),
            pl.BlockSpec((tm, LANES), lat),
        ],
        out_specs=[
            pl.BlockSpec((tm, C_Q), lambda i: (i, 0)),
            pl.BlockSpec((tm, C_KV), lat),
            pl.BlockSpec((tm, C_KV), lambda i: (i, 0)),
            pl.BlockSpec((None, C_KV, SEQ), ctx),
            pl.BlockSpec((None, C_KV, SEQ), ctx),
        ],
        out_shape=[jax.ShapeDtypeStruct((T, C_Q), BF16), jax.ShapeDtypeStruct((TS, C_KV), BF16),
                   jax.ShapeDtypeStruct((T, C_KV), BF16),
                   jax.ShapeDtypeStruct((BATCH, C_KV, SEQ), F32), jax.ShapeDtypeStruct((BATCH, C_KV, SEQ), F32)],
        compiler_params=_params(1),
        name="c_qkv",
    )(x, mod, mod, w, wkvt, cos, sin)


def _rope_tables():
    t = np.arange(DEC_SEQ)
    pos = np.stack([t // GRID_W, t % GRID_W], axis=1).astype(np.float32)
    inv = (ROPE_BASE ** (-np.arange(ROPE_FREQS, dtype=np.float32) / ROPE_FREQS)).astype(np.float32)
    d = np.arange(C_HEAD_DIM)
    axis = d // (2 * ROPE_FREQS)
    ang = pos[:, axis] * inv[d % ROPE_FREQS][None, :]
    sign = np.where((d % (2 * ROPE_FREQS)) < ROPE_FREQS, -1.0, 1.0)[None, :]
    cos_h, sin_h = np.cos(ang), np.sin(ang) * sign
    reps = LANES // C_HEAD_DIM
    cos = np.tile(np.tile(cos_h, (1, reps)), (DEC_BATCH, 1)).astype(np.float32)
    sin = np.tile(np.tile(sin_h, (1, reps)), (DEC_BATCH, 1)).astype(np.float32)
    return jnp.asarray(cos), jnp.asarray(sin)


def _dup_head(blk, half):
    lane = lax.broadcasted_iota(jnp.int32, blk.shape, 1)
    keep = (lane >= HALF).astype(jnp.int32) == half
    return jnp.where(keep, blk, pltpu.roll(blk, HALF, axis=1))


def _pair_values(v, half):
    lo = lax.broadcasted_iota(jnp.int32, v.shape, 1) < HALF
    v2 = _dup_head(v.astype(F32), half)
    top = jnp.concatenate([jnp.where(lo, v2, 0.0), jnp.where(lo, 1.0, 0.0)], axis=1)
    bot = jnp.concatenate([jnp.where(lo, 0.0, v2), jnp.where(lo, 0.0, 1.0)], axis=1)
    return top.astype(BF16), bot.astype(BF16)


def _pair_softmax_out(scores_a, scores_b, values_a, values_b, sink_a, sink_b):
    lo = lax.broadcasted_iota(jnp.int32, (1, LANES), 1) < HALF
    od, sink_terms = None, []
    for scores, values, sink in ((scores_a, values_a, sink_a), (scores_b, values_b, sink_b)):
        mx = sink
        for s in scores:
            mx = jnp.maximum(mx, jnp.max(s, axis=-1, keepdims=True))
        for s, v in zip(scores, values):
            part = _dot(jnp.exp(s - mx).astype(BF16), v)
            od = part if od is None else od + part
        sink_terms.append(jnp.exp(sink - mx))
    den = od[:, LANES:] + jnp.where(lo, sink_terms[0], sink_terms[1])
    return od[:, :LANES] / den


def _pair_sinks(sink_ref, jb):
    return (sink_ref[:, jb * LANES:jb * LANES + 1], sink_ref[:, jb * LANES + HALF:jb * LANES + HALF + 1])


CTX_TILES = BATCH * C_KV_HEADS
PIPE_LAG = 2


def _ctx_tile(g, lag):
    t = jnp.clip(g - lag, 0, CTX_TILES - 1)
    return t // C_KV_HEADS, t % C_KV_HEADS


def _ctx_attn_body(q_ref, kt_ref, v_ref, sink_ref, o_ref, s_s, e_s, st_s):
    g = pl.program_id(0)
    lo = lax.broadcasted_iota(jnp.int32, (1, LANES), 1) < HALF
    n_pairs = C_GROUPS // 2

    @pl.when(g == 0)
    def _():
        s_s[...] = jnp.zeros_like(s_s)
        e_s[...] = jnp.zeros_like(e_s)
        st_s[...] = jnp.ones_like(st_s)

    def stages(cur, prev):
        v_top, v_bot = _pair_values(v_ref[...], _ctx_tile(g, 2)[1] % 2)
        for jb in range(n_pairs):
            od = _dot(e_s[cur, 2 * jb], v_top) + _dot(e_s[cur, 2 * jb + 1], v_bot)
            o_ref[:, jb * LANES:(jb + 1) * LANES] = (od[:, :LANES] / (od[:, LANES:] + st_s[cur, jb])).astype(BF16)
        for jb in range(n_pairs):
            terms = []
            for hh, sink in enumerate(_pair_sinks(sink_ref, jb)):
                s = s_s[prev, 2 * jb + hh]
                mx = jnp.maximum(jnp.max(s, axis=-1, keepdims=True), sink)
                e_s[prev, 2 * jb + hh] = jnp.exp(s - mx).astype(BF16)
                terms.append(jnp.exp(sink - mx))
            st_s[prev, jb] = jnp.where(lo, terms[0], terms[1])
        half = _ctx_tile(g, 0)[1] % 2
        kt = kt_ref[pl.ds(pl.multiple_of(half * C_HEAD_DIM, C_HEAD_DIM), C_HEAD_DIM), :]
        zero = jnp.zeros_like(kt)
        kt_a = jnp.concatenate([kt, zero], axis=0).astype(BF16)
        kt_b = jnp.concatenate([zero, kt], axis=0).astype(BF16)
        for jb in range(n_pairs):
            q = q_ref[:, jb * LANES:(jb + 1) * LANES]
            s_s[cur, 2 * jb] = _dot(q, kt_a)
            s_s[cur, 2 * jb + 1] = _dot(q, kt_b)

    pl.when(g % 2 == 0)(lambda: stages(0, 1))
    pl.when(g % 2 == 1)(lambda: stages(1, 0))


def _ctx_attn(q, kt, v, sink_cols):
    gw = C_GROUPS * C_HEAD_DIM

    def at(lag, fn):
        return lambda g: fn(*_ctx_tile(g, lag))

    return pl.pallas_call(
        _ctx_attn_body,
        grid=(CTX_TILES + PIPE_LAG,),
        in_specs=[
            pl.BlockSpec((SEQ, gw), at(0, lambda b, h: (b, h))),
            pl.BlockSpec((None, LANES, SEQ), at(0, lambda b, h: (b, h // 2, 0))),
            pl.BlockSpec((SEQ, LANES), at(2, lambda b, h: (b, h // 2))),
            pl.BlockSpec((1, gw), at(1, lambda b, h: (0, h))),
        ],
        out_specs=pl.BlockSpec((SEQ, gw), at(2, lambda b, h: (b, h))),
        out_shape=jax.ShapeDtypeStruct((TP, C_Q), BF16),
        scratch_shapes=[
            pltpu.VMEM((2, C_GROUPS, SEQ, SEQ), F32),
            pltpu.VMEM((2, C_GROUPS, SEQ, SEQ), BF16),
            pltpu.VMEM((2, C_GROUPS // 2, SEQ, LANES), F32),
        ],
        compiler_params=_params(1),
        name="c_attn_ctx",
    )(q, kt, v, sink_cols)


LAT_TQ = 128
LAT_WIN = LAT_TQ + 2 * WINDOW
LAT_NQ = DEC_SEQ // LAT_TQ
LAT_TILES = DEC_BATCH * C_KV_HEADS * LAT_NQ


def _lat_tile(g, lag):
    t = jnp.clip(g - lag, 0, LAT_TILES - 1)
    bh = t // LAT_NQ
    return bh // C_KV_HEADS, bh % C_KV_HEADS, t % LAT_NQ


def _lat_window(j):
    return pl.multiple_of(jnp.clip(j * LAT_TQ - WINDOW, 0, DEC_SEQ - LAT_WIN), LANES)


def _lat_attn_body(q_ref, k_ref, v_ref, ck_ref, cv_ref, sink_ref, o_ref,
                   ka_s, kb_s, cka_s, ckb_s, va_s, vb_s, cva_s, cvb_s, sl_s, sc_s, el_s, ec_s, st_s):
    g = pl.program_id(0)
    lo = lax.broadcasted_iota(jnp.int32, (1, LANES), 1) < HALF
    n_pairs = C_GROUPS // 2
    b_a, h_a, j_a = _lat_tile(g, 0)
    b_c, h_c, j_c = _lat_tile(g, 2)
    v_slot_a = (b_a * C_KV_HEADS + h_a) % 2
    v_slot_c = (b_c * C_KV_HEADS + h_c) % 2

    @pl.when(g == 0)
    def _():
        sl_s[...] = jnp.zeros_like(sl_s)
        sc_s[...] = jnp.zeros_like(sc_s)
        el_s[...] = jnp.zeros_like(el_s)
        ec_s[...] = jnp.zeros_like(ec_s)
        st_s[...] = jnp.ones_like(st_s)

    @pl.when(jnp.logical_and(j_a == 0, g < LAT_TILES))
    def _():
        half = h_a % 2
        for src, a_s, b_s in ((k_ref, ka_s, kb_s), (ck_ref, cka_s, ckb_s)):
            lo2 = lax.broadcasted_iota(jnp.int32, src.shape, 1) < HALF
            k2 = _dup_head(src[...].astype(F32), half)
            a_s[...] = jnp.where(lo2, k2, 0.0).astype(BF16)
            b_s[...] = jnp.where(lo2, 0.0, k2).astype(BF16)
        va_s[v_slot_a], vb_s[v_slot_a] = _pair_values(v_ref[...], half)
        cva_s[v_slot_a], cvb_s[v_slot_a] = _pair_values(cv_ref[...], half)

    def stages(cur, prev):
        win_c = pl.ds(_lat_window(j_c), LAT_WIN)
        v_loc = (va_s[v_slot_c, win_c, :], vb_s[v_slot_c, win_c, :])
        v_ctx = (cva_s[v_slot_c], cvb_s[v_slot_c])
        for jb in range(n_pairs):
            od = None
            for hh in range(2):
                part = _dot(el_s[cur, 2 * jb + hh], v_loc[hh]) + _dot(ec_s[cur, 2 * jb + hh], v_ctx[hh])
                od = part if od is None else od + part
            o_ref[:, jb * LANES:(jb + 1) * LANES] = (od[:, :LANES] / (od[:, LANES:] + st_s[cur, jb])).astype(BF16)
        for jb in range(n_pairs):
            terms = []
            for hh, sink in enumerate(_pair_sinks(sink_ref, jb)):
                s_loc, s_ctx = sl_s[prev, 2 * jb + hh], sc_s[prev, 2 * jb + hh]
                mx = jnp.maximum(jnp.maximum(jnp.max(s_loc, axis=-1, keepdims=True),
                                             jnp.max(s_ctx, axis=-1, keepdims=True)), sink)
                el_s[prev, 2 * jb + hh] = jnp.exp(s_loc - mx).astype(BF16)
                ec_s[prev, 2 * jb + hh] = jnp.exp(s_ctx - mx).astype(BF16)
                terms.append(jnp.exp(sink - mx))
            st_s[prev, jb] = jnp.where(lo, terms[0], terms[1])
        start = _lat_window(j_a)
        win = pl.ds(start, LAT_WIN)
        qpos = j_a * LAT_TQ + lax.broadcasted_iota(jnp.int32, (LAT_TQ, LAT_WIN), 0)
        kpos = start + lax.broadcasted_iota(jnp.int32, (LAT_TQ, LAT_WIN), 1)
        band = jnp.abs(qpos - kpos) <= WINDOW
        for jb in range(n_pairs):
            q = q_ref[:, jb * LANES:(jb + 1) * LANES]
            for hh, (kl, kc) in enumerate(((ka_s, cka_s), (kb_s, ckb_s))):
                sl_s[cur, 2 * jb + hh] = jnp.where(band, _dot_nt(q, kl[win, :]), -jnp.inf)
                sc_s[cur, 2 * jb + hh] = _dot_nt(q, kc[...])

    pl.when(g % 2 == 0)(lambda: stages(0, 1))
    pl.when(g % 2 == 1)(lambda: stages(1, 0))


def _lat_attn(q, k, v, ck, cv, sink_cols):
    gw = C_GROUPS * C_HEAD_DIM
    q_off = TP // LAT_TQ
    kv_off = TP // DEC_SEQ

    def at(lag, fn):
        return lambda g: fn(*_lat_tile(g, lag))

    return pl.pallas_call(
        _lat_attn_body,
        grid=(LAT_TILES + PIPE_LAG,),
        in_specs=[
            pl.BlockSpec((LAT_TQ, gw), at(0, lambda b, h, j: (q_off + b * LAT_NQ + j, h))),
            pl.BlockSpec((DEC_SEQ, LANES), at(0, lambda b, h, j: (b, h // 2))),
            pl.BlockSpec((DEC_SEQ, LANES), at(0, lambda b, h, j: (kv_off + b, h // 2))),
            pl.BlockSpec((PAST_LEN, LANES), at(0, lambda b, h, j: (b, h // 2))),
            pl.BlockSpec((PAST_LEN, LANES), at(0, lambda b, h, j: (b, h // 2))),
            pl.BlockSpec((1, gw), at(1, lambda b, h, j: (0, h))),
        ],
        out_specs=pl.BlockSpec((LAT_TQ, gw), at(2, lambda b, h, j: (b * LAT_NQ + j, h))),
        out_shape=jax.ShapeDtypeStruct((TS, C_Q), BF16),
        scratch_shapes=[
            pltpu.VMEM((DEC_SEQ, LANES), BF16), pltpu.VMEM((DEC_SEQ, LANES), BF16),
            pltpu.VMEM((PAST_LEN, LANES), BF16), pltpu.VMEM((PAST_LEN, LANES), BF16),
            pltpu.VMEM((2, DEC_SEQ, 2 * LANES), BF16), pltpu.VMEM((2, DEC_SEQ, 2 * LANES), BF16),
            pltpu.VMEM((2, PAST_LEN, 2 * LANES), BF16), pltpu.VMEM((2, PAST_LEN, 2 * LANES), BF16),
            pltpu.VMEM((2, C_GROUPS, LAT_TQ, LAT_WIN), F32), pltpu.VMEM((2, C_GROUPS, LAT_TQ, PAST_LEN), F32),
            pltpu.VMEM((2, C_GROUPS, LAT_TQ, LAT_WIN), BF16), pltpu.VMEM((2, C_GROUPS, LAT_TQ, PAST_LEN), BF16),
            pltpu.VMEM((2, C_GROUPS // 2, LAT_TQ, LANES), F32),
        ],
        compiler_params=_params(1),
        name="c_attn_latent",
    )(q, k, v, ck, cv, sink_cols)


def kernel(x_prompt, x_sample, state_hgrn, state_gla, cache_k, cache_v, c, c_ctx, w_mod, b_mod, ln_g, ln_b,
           ffn_w1, ffn_w3, ffn_w2, w_in_ab, hgrn_lb, gla_gate_up, gla_gate_b, norm_a, norm_b, w_out_ab,
           w_qkv_c, sink_c, w_out_c):
    cs = jnp.zeros((8, D), F32).at[0].set(c_ctx).at[1:1 + DEC_BATCH].set(c)
    mod = _mod_vectors(cs, w_mod, b_mod).reshape(DEPTH, 8, 1, N_MOD * D)
    ffn_ws = (ffn_w1, ffn_w3, ffn_w2)
    ln_g, ln_b = ln_g.reshape(DEPTH, 3, 1, D), ln_b.reshape(DEPTH, 3, 1, D)

    def ffn(xs, ws, layer, sub, split_out=False, cast_next=None):
        casts = None if cast_next is None else (ffn_ws, cast_next)
        return _ffn_sublayer(xs, mod, *ws, ln_g, ln_b, layer, sub, split_out=split_out, casts=casts)

    x, *ws_01 = ffn([x_prompt.reshape(TP, D), x_sample.reshape(TS, D)], [w[0, 0].astype(BF16) for w in ffn_ws],
                    0, 0, cast_next=(0, 1))
    w_in = w_in_ab[0]
    o_aq, o_ai, o_ff, o_fb, o_ag = 0, A_W, 2 * A_W, 3 * A_W, 4 * A_W
    o_bq = 5 * A_W
    o_bk, o_bv = o_bq + B_QK, o_bq + 2 * B_QK
    o_bg = o_bv + B_V
    o_z = o_bg + B_V
    order = [(o_aq, A_W), (o_ff, A_W), (o_fb, A_W), (o_ag, A_W), (o_bq, B_QK), (o_bk, B_QK), (o_bg, B_V),
             (o_ai, A_W), (o_bv, B_V)]
    wmain = jnp.concatenate([w_in[:, o:o + w] for o, w in order], axis=1).astype(BF16)
    wz = jnp.pad(w_in[:, o_z:o_z + 2 * GATE_RANK], ((0, 0), (0, LANES - 2 * GATE_RANK))).astype(BF16)
    gup = jnp.zeros((LANES, 2 * B_QK), F32)
    gup = gup.at[:GATE_RANK, :B_QK].set(gla_gate_up[0, 0]).at[GATE_RANK:2 * GATE_RANK, B_QK:].set(gla_gate_up[0, 1])
    gb = gla_gate_b[0].reshape(1, 2 * B_QK)
    pf, pb = _inproj(x, mod, wmain, wz, gup.astype(BF16), gb, hgrn_lb, 0, 0)

    s0_a = state_hgrn[:, 0]
    s0_b = state_gla[:, 0].reshape(DEC_BATCH, 2, B_HEADS // 2, LANES, B_DV)
    oa_p, st_a = _scan(pf, pb, norm_a[0], None, prompt=True, pair=False)
    ob_p, st_b = _scan(pf, pb, norm_b[0], None, prompt=True, pair=True)
    (oa_s,) = _scan(pf, pb, norm_a[0], s0_a, prompt=False, pair=False)
    (ob_s,) = _scan(pf, pb, norm_b[0], s0_b, prompt=False, pair=True)
    w_out = w_out_ab[0].astype(BF16)
    x = _outproj([(oa_p, oa_s), (ob_p, ob_s)], [w_out[:A_W], w_out[A_W:]], x, mod, ln_g, ln_b, 0)
    x, *ws_10 = ffn([x], ws_01, 0, 1, cast_next=(1, 0))
    new_hgrn = st_a.reshape(BATCH, 1, 2, A_HEADS, A_DK, A_DV)
    new_gla = st_b.reshape(BATCH, 1, 2, B_HEADS, B_DK, B_DV)

    x, *ws_11 = ffn([x], ws_10, 1, 0, cast_next=(1, 1))
    cos, sin = _rope_tables()
    w_qkv = w_qkv_c[0].astype(BF16)
    q, k, v, kt, vt = _qkv(x, mod, w_qkv, w_qkv[:, C_Q:].T, cos, sin, 1)
    sink_cols = jnp.repeat(sink_c[0], C_HEAD_DIM).reshape(1, C_Q)
    o_p = _ctx_attn(q, kt, v, sink_cols)
    ck = cache_k[:, 0].reshape(DEC_BATCH * PAST_LEN, C_KV)
    cv = cache_v[:, 0].reshape(DEC_BATCH * PAST_LEN, C_KV)
    o_s = _lat_attn(q, k, v, ck, cv, sink_cols)
    x = _outproj([(o_p, o_s)], [w_out_c[0].astype(BF16)], x, mod, ln_g, ln_b, 1)
    y_p, y_s = ffn([x], ws_11, 1, 1, split_out=True)

    def cache_layout(zt):
        return zt.reshape(BATCH, 1, C_KV_HEADS, C_HEAD_DIM, SEQ).transpose(0, 1, 4, 2, 3)

    new_k, new_v = cache_layout(kt), cache_layout(vt)

    return (y_p.reshape(BATCH, SEQ, D), y_s.reshape(DEC_BATCH, DEC_SEQ, D), new_hgrn, new_gla, new_k, new_v)
```

```python
import functools
import math

import jax
import jax.numpy as jnp
import numpy as np
from jax import lax
from jax.experimental import pallas as pl
from jax.experimental.pallas import tpu as pltpu

D = 1024
BATCH, SEQ = 16, 256
DEC_BATCH, DEC_SEQ = 2, 2048
PAST_LEN = 512
GRID_W = 64
D_FF = 2816
N_MOD = 9
A_HEADS, A_DK, A_DV = 4, 128, 128
A_W = A_HEADS * A_DK
B_HEADS, B_DK, B_DV = 4, 64, 128
B_QK = B_HEADS * B_DK
B_V = B_HEADS * B_DV
GATE_RANK = 16
GLA_TAU = 16.0
CHUNK = 128
C_HEADS, C_KV_HEADS, C_HEAD_DIM = 16, 4, 64
C_GROUPS = C_HEADS // C_KV_HEADS
C_Q = C_HEADS * C_HEAD_DIM
C_KV = C_KV_HEADS * C_HEAD_DIM
WINDOW = 128
ROPE_FREQS = C_HEAD_DIM // 4
ROPE_BASE = 10000.0
DEPTH = 2
ALPHA = (2.0 * DEPTH) ** 0.25
LN_EPS = 1e-5
RMS_EPS = 1e-6

TP = BATCH * SEQ
TS = DEC_BATCH * DEC_SEQ
T = TP + TS
N_SEG = 1 + DEC_BATCH

LANES = 128
HALF = LANES // 2
FFN_TM = 512
FFN_SUB = 256
PROJ_TM = 2 * SEQ
INPROJ_TM = 512
PROJ_SUB = 256
OUT_TM = 1024
VMEM_LIMIT = 56 * 1024 * 1024

F32 = jnp.float32
BF16 = jnp.bfloat16


def _dot(a, b):
    return jnp.dot(a, b, preferred_element_type=F32)


def _dot_nt(a, b):
    return lax.dot_general(a, b, (((1,), (1,)), ((), ())), preferred_element_type=F32)


def _dot_tn(a, b):
    return lax.dot_general(a, b, (((0,), (0,)), ((), ())), preferred_element_type=F32)


def _silu(x):
    return x * jax.nn.sigmoid(x)


def _layer_norm(z, g, b):
    mu = jnp.mean(z, axis=-1, keepdims=True)
    zc = z - mu
    var = jnp.mean(zc * zc, axis=-1, keepdims=True)
    return zc * lax.rsqrt(var + LN_EPS) * g + b


def _seg_of_tile(i, tm):
    n_p = TP // tm
    n_s = DEC_SEQ // tm
    return jnp.where(i < n_p, 0, 1 + lax.div(jnp.maximum(i - n_p, 0), n_s))


def _params(n_axes):
    return pltpu.CompilerParams(dimension_semantics=("arbitrary",) * n_axes, vmem_limit_bytes=VMEM_LIMIT)


def _resident(shape):
    nd = len(shape)
    return pl.BlockSpec(shape, lambda *_: (0,) * nd, pipeline_mode=pl.Buffered(1))


def _resident_slice(shape, lead):
    block = (None,) * len(lead) + tuple(shape)
    return pl.BlockSpec(block, lambda *_: tuple(lead) + (0,) * len(shape), pipeline_mode=pl.Buffered(1))


def _mod_specs(layer, cols, tm, tile_of=lambda i: i):
    return [pl.BlockSpec((None, None, 1, D), functools.partial(
        lambda i, c: (layer, _seg_of_tile(tile_of(i), tm), 0, c), c=c)) for c in cols]


def _ln_specs(layer, idx):
    return [_resident_slice((1, D), (layer, idx))] * 2


BF16_SUBLANES = 16


def _cast_plan(ws, lead, n_steps, step_of):
    in_specs, out_specs, out_shapes = [], [], []
    for w in ws:
        rows, cols = w.shape[len(lead):]
        blk = next(b for b in range(BF16_SUBLANES, rows + 1, BF16_SUBLANES)
                   if rows % b == 0 and rows // b <= n_steps)
        last = rows // blk - 1
        in_specs.append(pl.BlockSpec((None,) * len(lead) + (blk, cols), functools.partial(
            lambda *g, last: tuple(lead) + (jnp.minimum(step_of(*g), last), 0), last=last)))
        out_specs.append(pl.BlockSpec((blk, cols), functools.partial(
            lambda *g, last: (jnp.minimum(step_of(*g), last), 0), last=last)))
        out_shapes.append(jax.ShapeDtypeStruct((rows, cols), BF16))
    return in_specs, out_specs, out_shapes


def _hosting_casts(body, n_in, n_out, n_cast):
    def hosted(*refs, **kw):
        ins, refs = refs[:n_in], refs[n_in:]
        cast_in, refs = refs[:n_cast], refs[n_cast:]
        outs, refs = refs[:n_out], refs[n_out:]
        cast_out, scratch = refs[:n_cast], refs[n_cast:]
        for src, dst in zip(cast_in, cast_out):
            dst[...] = src[...].astype(BF16)
        body(*ins, *outs, *scratch, **kw)
    return hosted


def _mod_body(c_ref, w_ref, b_ref, o_ref):
    c = c_ref[...]
    s = _silu(c).astype(BF16)
    o_ref[0] = _dot(s, w_ref[0].astype(BF16)) + b_ref[0]


def _mod_vectors(cs, w_mod, b_mod):
    tn = 1536
    n = N_MOD * D
    return pl.pallas_call(
        _mod_body,
        grid=(DEPTH, n // tn),
        in_specs=[
            pl.BlockSpec((8, D), lambda l, j: (0, 0)),
            pl.BlockSpec((1, D, tn), lambda l, j: (l, 0, j)),
            pl.BlockSpec((1, 1, tn), lambda l, j: (l, 0, j)),
        ],
        out_specs=pl.BlockSpec((1, 8, tn), lambda l, j: (l, 0, j)),
        out_shape=jax.ShapeDtypeStruct((DEPTH, 8, n), F32),
        compiler_params=_params(2),
        name="mod_vectors",
    )(cs, w_mod, b_mod.reshape(DEPTH, 1, n))


def _ffn_body(*refs, n_x, n_o, tm):
    xa_refs, xb_refs = refs[:n_x], refs[n_x:2 * n_x]
    shift_ref, scale_ref, gate_ref, w1_ref, w3_ref, w2_ref, g_ref, b_ref = refs[2 * n_x:2 * n_x + 8]
    o_refs = refs[2 * n_x + 8:2 * n_x + 8 + n_o]
    y_s, = refs[2 * n_x + 8 + n_o:]
    n_tiles, n_p = T // tm, TP // tm
    g = pl.program_id(0)

    def stage_b(group):
        x_ref, o_ref = xb_refs[group % n_x], o_refs[group % n_o]
        gate = 0.5 * gate_ref[...]
        for r in range(0, tm, FFN_SUB):
            rows = slice(r, r + FFN_SUB)
            z = ALPHA * x_ref[rows, :] + gate * y_s[rows, :]
            o_ref[rows, :] = _layer_norm(z, g_ref[...], b_ref[...])

    def stage_a(group):
        x_ref = xa_refs[group % n_x]
        shift, scale = shift_ref[...], scale_ref[...]
        for r in range(0, tm, FFN_SUB):
            rows = slice(r, r + FFN_SUB)
            h = (x_ref[rows, :] * (1.0 + scale) + shift).astype(BF16)
            a = _dot(h, w1_ref[...])
            b = _dot(h, w3_ref[...])
            y_s[rows, :] = _dot((_silu(a) * b).astype(BF16), w2_ref[...])

    split = n_x > 1 or n_o > 1
    pl.when(g == 0)(lambda: stage_a(0))
    pl.when(g == n_tiles)(lambda: stage_b(1))
    for ga, gb in (((0, 0), (1, 0), (1, 1)) if split else ((0, 0),)):
        cond = jnp.logical_and(g >= 1, g < n_tiles)
        if split:
            cond = jnp.logical_and(cond, (g >= n_p) if ga else (g < n_p))
            cond = jnp.logical_and(cond, (g - 1 >= n_p) if gb else (g - 1 < n_p))

        @pl.when(cond)
        def _(ga=ga, gb=gb):
            stage_b(gb)
            stage_a(ga)


def _group_specs(split, tm, width=D, tile_of=lambda i: i):
    if not split:
        return [pl.BlockSpec((tm, width), lambda i: (tile_of(i), 0))]
    n_p = TP // tm
    return [pl.BlockSpec((tm, width), lambda i: (jnp.minimum(tile_of(i), n_p - 1), 0)),
            pl.BlockSpec((tm, width), lambda i: (jnp.maximum(tile_of(i) - n_p, 0), 0))]


def _ffn_sublayer(xs, mod, w1, w3, w2, ln_g, ln_b, layer, sub, split_out=False, casts=None):
    n_x, n_o = len(xs), 2 if split_out else 1
    tm = FFN_TM
    out_shape = ([jax.ShapeDtypeStruct((TP, D), F32), jax.ShapeDtypeStruct((TS, D), F32)] if split_out
                 else [jax.ShapeDtypeStruct((T, D), F32)])
    mod_lo = 6 * sub
    n_tiles = T // tm
    tile_a = lambda i: jnp.minimum(i, n_tiles - 1)
    tile_b = lambda i: jnp.maximum(i - 1, 0)
    body = functools.partial(_ffn_body, n_x=n_x, n_o=n_o, tm=tm)
    in_specs = (_group_specs(n_x == 2, tm, tile_of=tile_a) + _group_specs(n_x == 2, tm, tile_of=tile_b)
                + _mod_specs(layer, (mod_lo, mod_lo + 1), tm, tile_a) + _mod_specs(layer, (mod_lo + 2,), tm, tile_b)
                + [_resident((D, D_FF)), _resident((D, D_FF)), _resident((D_FF, D))] + _ln_specs(layer, 2 * sub))
    out_specs = _group_specs(split_out, tm, tile_of=tile_b)
    args = [*xs, *xs, mod, mod, mod, w1, w3, w2, ln_g, ln_b]
    if casts is not None:
        ws, lead = casts
        c_in, c_out, c_shapes = _cast_plan(ws, lead, n_tiles, lambda i: i)
        body = _hosting_casts(body, len(in_specs), len(out_specs), len(ws))
        in_specs, out_specs, out_shape = in_specs + c_in, out_specs + c_out, out_shape + c_shapes
        args = args + list(ws)
    return pl.pallas_call(
        body,
        grid=(n_tiles + 1,),
        in_specs=in_specs,
        out_specs=out_specs,
        out_shape=out_shape,
        scratch_shapes=[pltpu.VMEM((tm, D), F32)],
        compiler_params=_params(1),
        name="ffn_sublayer",
    )(*args)


PF_AQ, PF_FF, PF_FB, PF_AG = 0, 512, 1024, 1536
PF_BQ, PF_BK, PF_BG, PF_LAF, PF_LAB = 2048, 2304, 2560, 3072, 3328
PF_W = 3584
PB_AV, PB_BV = 0, 512
PB_W = 1024
WM_AQ, WM_FF, WM_FB, WM_AG, WM_BQ, WM_BK, WM_BG, WM_AI, WM_BV = 0, 512, 1024, 1536, 2048, 2304, 2560, 3072, 3584
WM_W = 4096


def _log_sigmoid(x):
    return jnp.minimum(x, 0.0) - jnp.log(1.0 + jnp.exp(-jnp.abs(x)))


def _inproj_body(x_ref, shift_ref, scale_ref, w_ref, wz_ref, gu_ref, gb_ref, lb_ref, pf_ref, pb_ref, *, layer_e):
    def lower_bound(d):
        l = lb_ref[d]
        e = jnp.exp(l - jnp.max(l, axis=0, keepdims=True))
        sm = e / jnp.sum(e, axis=0, keepdims=True)
        return jnp.sum(sm[:layer_e + 1], axis=0, keepdims=True)

    lbs = [lower_bound(0), lower_bound(1)]
    for r in range(0, x_ref.shape[0], PROJ_SUB):
        rows = slice(r, r + PROJ_SUB)
        h = (x_ref[rows, :] * (1.0 + scale_ref[...]) + shift_ref[...]).astype(BF16)

        def proj(off, width):
            return _dot(h, w_ref[:, off:off + width])

        pf_ref[rows, PF_AQ:PF_AQ + A_W] = proj(WM_AQ, A_W)
        for lb, wm, pf in ((lbs[0], WM_FF, PF_FF), (lbs[1], WM_FB, PF_FB)):
            pf_ref[rows, pf:pf + A_W] = lb + (1.0 - lb) * jax.nn.sigmoid(proj(wm, A_W))
        pf_ref[rows, PF_AG:PF_AG + A_W] = _silu(proj(WM_AG, A_W))
        pf_ref[rows, PF_BQ:PF_BQ + B_QK] = proj(WM_BQ, B_QK) * (B_DK ** -0.5)
        pf_ref[rows, PF_BK:PF_BK + B_QK] = proj(WM_BK, B_QK)
        pf_ref[rows, PF_BG:PF_BG + B_V] = _silu(proj(WM_BG, B_V))
        pb_ref[rows, PB_AV:PB_AV + A_W] = _silu(proj(WM_AI, A_W)).astype(BF16)
        pb_ref[rows, PB_BV:PB_BV + B_V] = proj(WM_BV, B_V).astype(BF16)
        z = _dot(h, wz_ref[...]).astype(BF16)
        pre = _dot(z, gu_ref[...]) + gb_ref[...]
        pf_ref[rows, PF_LAF:PF_LAF + 2 * B_QK] = _log_sigmoid(pre) * (1.0 / GLA_TAU)


def _inproj(x, mod, wmain, wz, gup, gb, hgrn_lb, layer, layer_e, tm=INPROJ_TM):
    n_l = hgrn_lb.shape[1]
    return pl.pallas_call(
        functools.partial(_inproj_body, layer_e=layer_e),
        grid=(T // tm,),
        in_specs=[pl.BlockSpec((tm, D), lambda i: (i, 0))] + _mod_specs(layer, (3, 4), tm) + [
            _resident((D, WM_W)),
            _resident((D, LANES)),
            _resident((LANES, 2 * B_QK)),
            _resident((1, 2 * B_QK)),
            _resident((2, n_l, A_W)),
        ],
        out_specs=[
            pl.BlockSpec((tm, PF_W), lambda i: (i, 0)),
            pl.BlockSpec((tm, PB_W), lambda i: (i, 0)),
        ],
        out_shape=[jax.ShapeDtypeStruct((T, PF_W), F32), jax.ShapeDtypeStruct((T, PB_W), BF16)],
        compiler_params=_params(1),
        name="ab_inproj",
    )(x, mod, mod, wmain, wz, gup, gb, hgrn_lb)


SCAN_PROMPT_SEQS = 4
SCAN_UNROLL = 4


def _prefix_rows(x):
    row = lax.broadcasted_iota(jnp.int32, x.shape, 0)
    s = 1
    while s < x.shape[0]:
        x = x + jnp.where(row >= s, pltpu.roll(x, s, axis=0), 0.0)
        s *= 2
    return x


def _scan_body(*refs, seq_len, seqs, pair, has_s0, emit_state):
    n = seq_len // CHUNK
    n_all = seqs * n
    nh = 2 if pair else 1
    it = iter(refs)
    q_ref = next(it)
    if pair:
        k_ref, laf_ref, lab_ref = next(it), next(it), next(it)
    else:
        ff_ref, fb_ref = next(it), next(it)
    g_ref, v_ref, nw_ref = next(it), next(it), next(it)
    s0_ref = next(it) if has_s0 else None
    o_ref = next(it)
    st_ref = next(it) if emit_state else None
    qd_s, oi_s, kv_s, dec_s, sb_s = it

    row = lax.broadcasted_iota(jnp.int32, (CHUNK, CHUNK), 0)
    col = lax.broadcasted_iota(jnp.int32, (CHUNK, CHUNK), 1)
    tril = row >= col
    triu = row <= col
    lane = lax.broadcasted_iota(jnp.int32, (1, LANES), 1)
    lane2 = lax.broadcasted_iota(jnp.int32, (1, 2 * LANES), 1)
    if pair:
        masks = [lane < HALF, lane >= HALF]
        masks2 = [(lane2 % LANES) < HALF, (lane2 % LANES) >= HALF]
    else:
        masks, masks2 = [None], [None]

    def pick(mask, x):
        return x if mask is None else jnp.where(mask, x, jnp.zeros_like(x))

    def rows_of(c):
        return pl.ds(pl.multiple_of(c * CHUNK, CHUNK), CHUNK)

    def loop(body):
        if n_all <= SCAN_UNROLL:
            for c in range(n_all):
                body(c)
        else:
            def fbody(i, carry):
                for u in range(SCAN_UNROLL):
                    body(i * SCAN_UNROLL + u)
                return carry
            lax.fori_loop(0, n_all // SCAN_UNROLL, fbody, 0)

    def phase1(c):
        rows = rows_of(c)
        q = q_ref[rows, :]
        if pair:
            k_f = k_b = k_ref[rows, :]
            la_f, la_b = laf_ref[rows, :], lab_ref[rows, :]
        else:
            f_f, f_b = ff_ref[rows, :], fb_ref[rows, :]
            k_f, k_b = 1.0 - f_f, 1.0 - f_b
            la_f, la_b = jnp.log(f_f), jnp.log(f_b)
        cs = _prefix_rows(jnp.concatenate([la_f, la_b], axis=1))
        cf, cbi = cs[:, :LANES], cs[:, LANES:]
        tot_f, tot_b = cf[CHUNK - 1:CHUNK, :], cbi[CHUNK - 1:CHUNK, :]
        rb = tot_b - cbi + la_b
        ref_f, ref_b = cf[CHUNK // 2 - 1:CHUNK // 2, :], rb[CHUNK // 2:CHUNK // 2 + 1, :]
        qtf = q * jnp.exp(cf - ref_f)
        qtb = q * jnp.exp(rb - ref_b)
        ktf = k_f * jnp.exp(ref_f - cf)
        ktb = k_b * jnp.exp(ref_b - rb)
        qd = jnp.concatenate([qtf * jnp.exp(ref_f), qtb * jnp.exp(ref_b)], axis=1).astype(BF16)
        ku = jnp.concatenate([ktf * jnp.exp(tot_f - ref_f), ktb * jnp.exp(tot_b - ref_b)], axis=1).astype(BF16)
        qd_s[rows, :] = qd
        qt = jnp.concatenate([qtf, qtb], axis=0).astype(BF16)
        kt = jnp.concatenate([ktf, ktb], axis=0).astype(BF16)
        kv = None
        for hh in range(nh):
            v = v_ref[rows, hh * LANES:(hh + 1) * LANES]
            sc = _dot_nt(pick(masks[hh], qt), kt)
            att = jnp.where(tril, sc[:CHUNK, :CHUNK], 0.0) + jnp.where(triu, sc[CHUNK:, CHUNK:], 0.0)
            oi_s[rows, hh * LANES:(hh + 1) * LANES] = _dot(att.astype(BF16), v)
            kv_h = _dot_tn(v, ku)
            kv = kv_h if kv is None else jnp.where(masks2[0], kv, kv_h)
        kv_s[c] = kv
        dec_s[c] = jnp.exp(jnp.concatenate([tot_f, tot_b], axis=1))

    loop(phase1)

    def recurrence(sq, d, reverse):
        cols = slice(d * LANES, (d + 1) * LANES)
        c0 = sq * n
        st0 = s0_ref[sq, d, 0].T if has_s0 else jnp.zeros((LANES, LANES), F32)

        def step(c, st):
            sb_s[c, :, cols] = st.astype(BF16)
            return st * dec_s[c, :, cols] + kv_s[c, :, cols]

        if n <= 8:
            st = st0
            for c in (range(n - 1, -1, -1) if reverse else range(n)):
                st = step(c0 + c, st)
        else:
            st = lax.fori_loop(0, n, lambda i, st: step(c0 + (n - 1 - i if reverse else i), st), st0)
        if emit_state:
            st_ref[sq, d, 0] = st.T

    for sq in range(seqs):
        recurrence(sq, 0, False)
        recurrence(sq, 1, True)

    nw = nw_ref[...]

    def phase2(c):
        rows = rows_of(c)
        qcat = qd_s[rows, :]
        scat = sb_s[c]
        for hh in range(nh):
            cols = slice(hh * LANES, (hh + 1) * LANES)
            o = oi_s[rows, cols] + _dot_nt(pick(masks2[hh], qcat), scat)
            o = o * lax.rsqrt(jnp.mean(o * o, axis=-1, keepdims=True) + RMS_EPS) * nw
            o_ref[rows, cols] = (o * g_ref[rows, cols]).astype(BF16)

    loop(phase2)


def _scan(pf, pb, norm_w, s0, *, prompt, pair):
    seq_len = SEQ if prompt else DEC_SEQ
    nseq = BATCH if prompt else DEC_BATCH
    seqs = SCAN_PROMPT_SEQS if prompt else 1
    rows = seqs * seq_len
    row_off = 0 if prompt else TP // rows
    units = B_HEADS // 2 if pair else A_HEADS
    nh = 2 if pair else 1
    n_all = rows // CHUNK
    has_s0 = s0 is not None
    emit_state = prompt

    def colspec(off, width=LANES):
        base = off // width
        return pl.BlockSpec((rows, width), lambda s, u: (s + row_off, base + u))

    if pair:
        in_specs = [colspec(PF_BQ), colspec(PF_BK), colspec(PF_LAF), colspec(PF_LAB),
                    colspec(PF_BG, 2 * LANES), colspec(PB_BV, 2 * LANES)]
        args = [pf, pf, pf, pf, pf, pb]
    else:
        in_specs = [colspec(PF_AQ), colspec(PF_FF), colspec(PF_FB), colspec(PF_AG), colspec(PB_AV)]
        args = [pf, pf, pf, pf, pb]
    in_specs.append(pl.BlockSpec((1, LANES), lambda s, u: (0, 0)))
    args.append(norm_w.reshape(1, LANES))
    state_spec = pl.BlockSpec((seqs, 2, 1, LANES, LANES), lambda s, u: (s, 0, u, 0, 0))
    if has_s0:
        in_specs.append(state_spec)
        args.append(s0)
    out_specs = [pl.BlockSpec((rows, nh * LANES), lambda s, u: (s, u))]
    out_shape = [jax.ShapeDtypeStruct((nseq * seq_len, units * nh * LANES), BF16)]
    if emit_state:
        out_specs.append(state_spec)
        out_shape.append(jax.ShapeDtypeStruct((nseq, 2, units, LANES, LANES), F32))
    scratch = [
        pltpu.VMEM((rows, 2 * LANES), BF16),
        pltpu.VMEM((rows, nh * LANES), F32),
        pltpu.VMEM((n_all, LANES, 2 * LANES), F32),
        pltpu.VMEM((n_all, 1, 2 * LANES), F32),
        pltpu.VMEM((n_all, LANES, 2 * LANES), BF16),
    ]
    return pl.pallas_call(
        functools.partial(_scan_body, seq_len=seq_len, seqs=seqs, pair=pair, has_s0=has_s0,
                          emit_state=emit_state),
        grid=(nseq // seqs, units),
        in_specs=in_specs,
        out_specs=out_specs,
        out_shape=out_shape,
        scratch_shapes=scratch,
        compiler_params=_params(2),
        name=f"scan_{'p' if prompt else 's'}_{'gla' if pair else 'hgrn'}",
    )(*args)


def _outproj_body(*refs, n_lhs, tm):
    lhs = refs[:2 * n_lhs]
    ws = refs[2 * n_lhs:3 * n_lhs]
    x_ref, m_ref, g_ref, b_ref, o_ref = refs[3 * n_lhs:]

    def compute(group):
        for r in range(0, tm, PROJ_SUB):
            rows = slice(r, r + PROJ_SUB)
            y = _dot(lhs[group][rows, :], ws[0][...])
            for j in range(1, n_lhs):
                y = y + _dot(lhs[2 * j + group][rows, :], ws[j][...])
            z = ALPHA * x_ref[rows, :] + m_ref[...] * y
            o_ref[rows, :] = _layer_norm(z, g_ref[...], b_ref[...])

    in_prompt = pl.program_id(0) < TP // tm
    pl.when(in_prompt)(lambda: compute(0))
    pl.when(jnp.logical_not(in_prompt))(lambda: compute(1))


def _outproj(lhs, ws, x, mod, ln_g, ln_b, layer, tm=OUT_TM):
    n_lhs = len(lhs)
    in_specs, args = [], []
    for a_p, a_s in lhs:
        in_specs += _group_specs(True, tm, a_p.shape[1])
        args += [a_p, a_s]
    in_specs += [_resident(w.shape) for w in ws]
    in_specs += [pl.BlockSpec((tm, D), lambda i: (i, 0))] + _mod_specs(layer, (5,), tm) + _ln_specs(layer, 1)
    return pl.pallas_call(
        functools.partial(_outproj_body, n_lhs=n_lhs, tm=tm),
        grid=(T // tm,),
        in_specs=in_specs,
        out_specs=pl.BlockSpec((tm, D), lambda i: (i, 0)),
        out_shape=jax.ShapeDtypeStruct((T, D), F32),
        compiler_params=_params(1),
        name="mixer_outproj",
    )(*args, *ws, x, mod, ln_g, ln_b)


def _rope_partner(x):
    lane = lax.broadcasted_iota(jnp.int32, x.shape, 1)
    first_half = (lane % (2 * ROPE_FREQS)) < ROPE_FREQS
    return jnp.where(first_half, pltpu.roll(x, LANES - ROPE_FREQS, axis=1), pltpu.roll(x, ROPE_FREQS, axis=1))


def _qkv_body(x_ref, shift_ref, scale_ref, w_ref, wkvt_ref, cos_ref, sin_ref,
              q_ref, k_ref, v_ref, kt_ref, vt_ref):
    qscale = C_HEAD_DIM ** -0.5
    in_prompt = pl.program_id(0) < TP // PROJ_TM

    def modulated(rows):
        return (x_ref[rows, :] * (1.0 + scale_ref[...]) + shift_ref[...]).astype(BF16)

    @pl.when(in_prompt)
    def _():
        for sq in range(PROJ_TM // SEQ):
            rows = slice(sq * SEQ, (sq + 1) * SEQ)
            h = modulated(rows)
            q_ref[rows, :] = (_dot(h, w_ref[:, :C_Q]) * qscale).astype(BF16)
            v_ref[rows, :] = _dot(h, w_ref[:, C_Q + C_KV:]).astype(BF16)
            kt_ref[sq] = _dot_nt(wkvt_ref[:C_KV, :], h)
            vt_ref[sq] = _dot_nt(wkvt_ref[C_KV:, :], h)

    @pl.when(jnp.logical_not(in_prompt))
    def _():
        for sq in range(PROJ_TM // SEQ):
            rows = slice(sq * SEQ, (sq + 1) * SEQ)
            h = modulated(rows)
            cos, sin = cos_ref[rows, :], sin_ref[rows, :]
            zq = _dot(h, w_ref[:, :C_Q])
            zk = _dot(h, w_ref[:, C_Q:C_Q + C_KV])
            v_ref[rows, :] = _dot(h, w_ref[:, C_Q + C_KV:]).astype(BF16)

            def rope(z, cos=cos, sin=sin):
                return z * cos + _rope_partner(z) * sin

            for j in range(C_Q // LANES):
                cols = slice(j * LANES, (j + 1) * LANES)
                q_ref[rows, cols] = (rope(zq[:, cols]) * qscale).astype(BF16)
            for j in range(C_KV // LANES):
                cols = slice(j * LANES, (j + 1) * LANES)
                k_ref[rows, cols] = rope(zk[:, cols]).astype(BF16)


def _qkv(x, mod, w, wkvt, cos, sin, layer):
    tm = PROJ_TM
    n_p = TP // tm
    lat = lambda i: (jnp.maximum(i - n_p, 0), 0)
    ctx = lambda i: (jnp.minimum(i, n_p - 1), 0, 0)
    return pl.pallas_call(
        _qkv_body,
        grid=(T // tm,),
        in_specs=[pl.BlockSpec((tm, D), lambda i: (i, 0))] + _mod_specs(layer, (3, 4), tm) + [
            _resident((D, C_Q + 2 * C_KV)),
            _resident((2 * C_KV, D)),
            pl.BlockSpec((tm, LANES), lat),
            pl.BlockSpec((tm, LANES), lat),
        ],
        out_specs=[
            pl.BlockSpec((tm, C_Q), lambda i: (i, 0)),
            pl.BlockSpec((tm, C_KV), lat),
            pl.BlockSpec((tm, C_KV), lambda i: (i, 0)),
            pl.BlockSpec((tm // SEQ, C_KV, SEQ), ctx),
            pl.BlockSpec((tm // SEQ, C_KV, SEQ), ctx),
        ],
        out_shape=[jax.ShapeDtypeStruct((T, C_Q), BF16), jax.ShapeDtypeStruct((TS, C_KV), BF16),
                   jax.ShapeDtypeStruct((T, C_KV), BF16),
                   jax.ShapeDtypeStruct((BATCH, C_KV, SEQ), F32), jax.ShapeDtypeStruct((BATCH, C_KV, SEQ), F32)],
        compiler_params=_params(1),
        name="c_qkv",
    )(x, mod, mod, w, wkvt, cos, sin)


def _rope_tables():
    t = np.arange(DEC_SEQ)
    pos = np.stack([t // GRID_W, t % GRID_W], axis=1).astype(np.float32)
    inv = (ROPE_BASE ** (-np.arange(ROPE_FREQS, dtype=np.float32) / ROPE_FREQS)).astype(np.float32)
    d = np.arange(C_HEAD_DIM)
    axis = d // (2 * ROPE_FREQS)
    ang = pos[:, axis] * inv[d % ROPE_FREQS][None, :]
    sign = np.where((d % (2 * ROPE_FREQS)) < ROPE_FREQS, -1.0, 1.0)[None, :]
    cos_h, sin_h = np.cos(ang), np.sin(ang) * sign
    reps = LANES // C_HEAD_DIM
    cos = np.tile(np.tile(cos_h, (1, reps)), (DEC_BATCH, 1)).astype(np.float32)
    sin = np.tile(np.tile(sin_h, (1, reps)), (DEC_BATCH, 1)).astype(np.float32)
    return jnp.asarray(cos), jnp.asarray(sin)


def _dup_head(blk, half):
    lane = lax.broadcasted_iota(jnp.int32, blk.shape, 1)
    keep = (lane >= HALF).astype(jnp.int32) == half
    return jnp.where(keep, blk, pltpu.roll(blk, HALF, axis=1))


def _pair_values(v, half):
    lo = lax.broadcasted_iota(jnp.int32, v.shape, 1) < HALF
    v2 = _dup_head(v.astype(F32), half)
    top = jnp.concatenate([jnp.where(lo, v2, 0.0), jnp.where(lo, 1.0, 0.0)], axis=1)
    bot = jnp.concatenate([jnp.where(lo, 0.0, v2), jnp.where(lo, 0.0, 1.0)], axis=1)
    return top.astype(BF16), bot.astype(BF16)


def _softmax_weights(scores, sink):
    mx = sink
    for s in scores:
        mx = jnp.maximum(mx, jnp.max(s, axis=-1, keepdims=True))
    mxb = jnp.broadcast_to(mx, (mx.shape[0], LANES))
    weights = [jnp.exp(s - jnp.concatenate([mxb] * (s.shape[1] // LANES), axis=1)).astype(BF16) for s in scores]
    return weights, jnp.exp(sink - mxb)


def _pair_sinks(sink_ref, jb):
    return (sink_ref[:, jb * LANES:jb * LANES + 1], sink_ref[:, jb * LANES + HALF:jb * LANES + HALF + 1])


CTX_TILES = BATCH * C_KV_HEADS
PIPE_LAG = 2


def _ctx_tile(g, lag):
    t = jnp.clip(g - lag, 0, CTX_TILES - 1)
    return t // C_KV_HEADS, t % C_KV_HEADS


def _ctx_attn_body(q_ref, kt_ref, v_ref, sink_ref, o_ref, s_s, e_s, st_s):
    g = pl.program_id(0)
    lo = lax.broadcasted_iota(jnp.int32, (1, LANES), 1) < HALF
    n_pairs = C_GROUPS // 2

    @pl.when(g == 0)
    def _():
        s_s[...] = jnp.zeros_like(s_s)
        e_s[...] = jnp.zeros_like(e_s)
        st_s[...] = jnp.ones_like(st_s)

    def stages(cur, prev):
        v_top, v_bot = _pair_values(v_ref[...], _ctx_tile(g, 2)[1] % 2)
        for jb in range(n_pairs):
            od = _dot(e_s[cur, 2 * jb], v_top) + _dot(e_s[cur, 2 * jb + 1], v_bot)
            o_ref[:, jb * LANES:(jb + 1) * LANES] = (od[:, :LANES] / (od[:, LANES:] + st_s[cur, jb])).astype(BF16)
        for jb in range(n_pairs):
            terms = []
            for hh, sink in enumerate(_pair_sinks(sink_ref, jb)):
                (e,), term = _softmax_weights([s_s[prev, 2 * jb + hh]], sink)
                e_s[prev, 2 * jb + hh] = e
                terms.append(term)
            st_s[prev, jb] = jnp.where(lo, terms[0], terms[1])
        half = _ctx_tile(g, 0)[1] % 2
        kt = kt_ref[pl.ds(pl.multiple_of(half * C_HEAD_DIM, C_HEAD_DIM), C_HEAD_DIM), :]
        zero = jnp.zeros_like(kt)
        kt_a = jnp.concatenate([kt, zero], axis=0).astype(BF16)
        kt_b = jnp.concatenate([zero, kt], axis=0).astype(BF16)
        for jb in range(n_pairs):
            q = q_ref[:, jb * LANES:(jb + 1) * LANES]
            s_s[cur, 2 * jb] = _dot(q, kt_a)
            s_s[cur, 2 * jb + 1] = _dot(q, kt_b)

    pl.when(g % 2 == 0)(lambda: stages(0, 1))
    pl.when(g % 2 == 1)(lambda: stages(1, 0))


def _ctx_attn(q, kt, v, sink_cols):
    gw = C_GROUPS * C_HEAD_DIM

    def at(lag, fn):
        return lambda g: fn(*_ctx_tile(g, lag))

    return pl.pallas_call(
        _ctx_attn_body,
        grid=(CTX_TILES + PIPE_LAG,),
        in_specs=[
            pl.BlockSpec((SEQ, gw), at(0, lambda b, h: (b, h))),
            pl.BlockSpec((None, LANES, SEQ), at(0, lambda b, h: (b, h // 2, 0))),
            pl.BlockSpec((SEQ, LANES), at(2, lambda b, h: (b, h // 2))),
            pl.BlockSpec((1, gw), at(1, lambda b, h: (0, h))),
        ],
        out_specs=pl.BlockSpec((SEQ, gw), at(2, lambda b, h: (b, h))),
        out_shape=jax.ShapeDtypeStruct((TP, C_Q), BF16),
        scratch_shapes=[
            pltpu.VMEM((2, C_GROUPS, SEQ, SEQ), F32),
            pltpu.VMEM((2, C_GROUPS, SEQ, SEQ), BF16),
            pltpu.VMEM((2, C_GROUPS // 2, SEQ, LANES), F32),
        ],
        compiler_params=_params(1),
        name="c_attn_ctx",
    )(q, kt, v, sink_cols)


LAT_TQ = 128
LAT_WIN = LAT_TQ + 2 * WINDOW
LAT_NQ = DEC_SEQ // LAT_TQ
LAT_TILES = DEC_BATCH * C_KV_HEADS * LAT_NQ


def _lat_tile(g, lag):
    t = jnp.clip(g - lag, 0, LAT_TILES - 1)
    bh = t // LAT_NQ
    return bh // C_KV_HEADS, bh % C_KV_HEADS, t % LAT_NQ


def _lat_window(j):
    return pl.multiple_of(jnp.clip(j * LAT_TQ - WINDOW, 0, DEC_SEQ - LAT_WIN), LANES)


def _lat_attn_body(q_ref, k_ref, v_ref, ck_ref, cv_ref, sink_ref, o_ref,
                   ka_s, kb_s, cka_s, ckb_s, va_s, vb_s, cva_s, cvb_s, sl_s, sc_s, el_s, ec_s, st_s):
    g = pl.program_id(0)
    lo = lax.broadcasted_iota(jnp.int32, (1, LANES), 1) < HALF
    n_pairs = C_GROUPS // 2
    b_a, h_a, j_a = _lat_tile(g, 0)
    b_c, h_c, j_c = _lat_tile(g, 2)
    v_slot_a = (b_a * C_KV_HEADS + h_a) % 2
    v_slot_c = (b_c * C_KV_HEADS + h_c) % 2

    @pl.when(g == 0)
    def _():
        sl_s[...] = jnp.zeros_like(sl_s)
        sc_s[...] = jnp.zeros_like(sc_s)
        el_s[...] = jnp.zeros_like(el_s)
        ec_s[...] = jnp.zeros_like(ec_s)
        st_s[...] = jnp.ones_like(st_s)

    @pl.when(jnp.logical_and(j_a == 0, g < LAT_TILES))
    def _():
        half = h_a % 2
        for src, a_s, b_s in ((k_ref, ka_s, kb_s), (ck_ref, cka_s, ckb_s)):
            lo2 = lax.broadcasted_iota(jnp.int32, src.shape, 1) < HALF
            k2 = _dup_head(src[...].astype(F32), half)
            a_s[...] = jnp.where(lo2, k2, 0.0).astype(BF16)
            b_s[...] = jnp.where(lo2, 0.0, k2).astype(BF16)
        va_s[v_slot_a], vb_s[v_slot_a] = _pair_values(v_ref[...], half)
        cva_s[v_slot_a], cvb_s[v_slot_a] = _pair_values(cv_ref[...], half)

    def stages(cur, prev):
        win_c = pl.ds(_lat_window(j_c), LAT_WIN)
        v_loc = (va_s[v_slot_c, win_c, :], vb_s[v_slot_c, win_c, :])
        v_ctx = (cva_s[v_slot_c], cvb_s[v_slot_c])
        for jb in range(n_pairs):
            od = None
            for hh in range(2):
                part = _dot(el_s[cur, 2 * jb + hh], v_loc[hh]) + _dot(ec_s[cur, 2 * jb + hh], v_ctx[hh])
                od = part if od is None else od + part
            o_ref[:, jb * LANES:(jb + 1) * LANES] = (od[:, :LANES] / (od[:, LANES:] + st_s[cur, jb])).astype(BF16)
        for jb in range(n_pairs):
            terms = []
            for hh, sink in enumerate(_pair_sinks(sink_ref, jb)):
                (e_loc, e_ctx), term = _softmax_weights([sl_s[prev, 2 * jb + hh], sc_s[prev, 2 * jb + hh]], sink)
                el_s[prev, 2 * jb + hh] = e_loc
                ec_s[prev, 2 * jb + hh] = e_ctx
                terms.append(term)
            st_s[prev, jb] = jnp.where(lo, terms[0], terms[1])
        start = _lat_window(j_a)
        win = pl.ds(start, LAT_WIN)
        qpos = j_a * LAT_TQ + lax.broadcasted_iota(jnp.int32, (LAT_TQ, LAT_WIN), 0)
        kpos = start + lax.broadcasted_iota(jnp.int32, (LAT_TQ, LAT_WIN), 1)
        band = jnp.abs(qpos - kpos) <= WINDOW
        for jb in range(n_pairs):
            q = q_ref[:, jb * LANES:(jb + 1) * LANES]
            for hh, (kl, kc) in enumerate(((ka_s, cka_s), (kb_s, ckb_s))):
                sl_s[cur, 2 * jb + hh] = jnp.where(band, _dot_nt(q, kl[win, :]), -jnp.inf)
                sc_s[cur, 2 * jb + hh] = _dot_nt(q, kc[...])

    pl.when(g % 2 == 0)(lambda: stages(0, 1))
    pl.when(g % 2 == 1)(lambda: stages(1, 0))


def _lat_attn(q, k, v, ck, cv, sink_cols):
    gw = C_GROUPS * C_HEAD_DIM
    q_off = TP // LAT_TQ
    kv_off = TP // DEC_SEQ

    def at(lag, fn):
        return lambda g: fn(*_lat_tile(g, lag))

    return pl.pallas_call(
        _lat_attn_body,
        grid=(LAT_TILES + PIPE_LAG,),
        in_specs=[
            pl.BlockSpec((LAT_TQ, gw), at(0, lambda b, h, j: (q_off + b * LAT_NQ + j, h))),
            pl.BlockSpec((DEC_SEQ, LANES), at(0, lambda b, h, j: (b, h // 2))),
            pl.BlockSpec((DEC_SEQ, LANES), at(0, lambda b, h, j: (kv_off + b, h // 2))),
            pl.BlockSpec((PAST_LEN, LANES), at(0, lambda b, h, j: (b, h // 2))),
            pl.BlockSpec((PAST_LEN, LANES), at(0, lambda b, h, j: (b, h // 2))),
            pl.BlockSpec((1, gw), at(1, lambda b, h, j: (0, h))),
        ],
        out_specs=pl.BlockSpec((LAT_TQ, gw), at(2, lambda b, h, j: (b * LAT_NQ + j, h))),
        out_shape=jax.ShapeDtypeStruct((TS, C_Q), BF16),
        scratch_shapes=[
            pltpu.VMEM((DEC_SEQ, LANES), BF16), pltpu.VMEM((DEC_SEQ, LANES), BF16),
            pltpu.VMEM((PAST_LEN, LANES), BF16), pltpu.VMEM((PAST_LEN, LANES), BF16),
            pltpu.VMEM((2, DEC_SEQ, 2 * LANES), BF16), pltpu.VMEM((2, DEC_SEQ, 2 * LANES), BF16),
            pltpu.VMEM((2, PAST_LEN, 2 * LANES), BF16), pltpu.VMEM((2, PAST_LEN, 2 * LANES), BF16),
            pltpu.VMEM((2, C_GROUPS, LAT_TQ, LAT_WIN), F32), pltpu.VMEM((2, C_GROUPS, LAT_TQ, PAST_LEN), F32),
            pltpu.VMEM((2, C_GROUPS, LAT_TQ, LAT_WIN), BF16), pltpu.VMEM((2, C_GROUPS, LAT_TQ, PAST_LEN), BF16),
            pltpu.VMEM((2, C_GROUPS // 2, LAT_TQ, LANES), F32),
        ],
        compiler_params=_params(1),
        name="c_attn_latent",
    )(q, k, v, ck, cv, sink_cols)


def kernel(x_prompt, x_sample, state_hgrn, state_gla, cache_k, cache_v, c, c_ctx, w_mod, b_mod, ln_g, ln_b,
           ffn_w1, ffn_w3, ffn_w2, w_in_ab, hgrn_lb, gla_gate_up, gla_gate_b, norm_a, norm_b, w_out_ab,
           w_qkv_c, sink_c, w_out_c):
    cs = jnp.zeros((8, D), F32).at[0].set(c_ctx).at[1:1 + DEC_BATCH].set(c)
    mod = _mod_vectors(cs, w_mod, b_mod).reshape(DEPTH, 8, 1, N_MOD * D)
    ffn_ws = (ffn_w1, ffn_w3, ffn_w2)
    ln_g, ln_b = ln_g.reshape(DEPTH, 3, 1, D), ln_b.reshape(DEPTH, 3, 1, D)

    def ffn(xs, ws, layer, sub, split_out=False, cast_next=None):
        casts = None if cast_next is None else (ffn_ws, cast_next)
        return _ffn_sublayer(xs, mod, *ws, ln_g, ln_b, layer, sub, split_out=split_out, casts=casts)

    x, *ws_01 = ffn([x_prompt.reshape(TP, D), x_sample.reshape(TS, D)], [w[0, 0].astype(BF16) for w in ffn_ws],
                    0, 0, cast_next=(0, 1))
    w_in = w_in_ab[0]
    o_aq, o_ai, o_ff, o_fb, o_ag = 0, A_W, 2 * A_W, 3 * A_W, 4 * A_W
    o_bq = 5 * A_W
    o_bk, o_bv = o_bq + B_QK, o_bq + 2 * B_QK
    o_bg = o_bv + B_V
    o_z = o_bg + B_V
    order = [(o_aq, A_W), (o_ff, A_W), (o_fb, A_W), (o_ag, A_W), (o_bq, B_QK), (o_bk, B_QK), (o_bg, B_V),
             (o_ai, A_W), (o_bv, B_V)]
    wmain = jnp.concatenate([w_in[:, o:o + w] for o, w in order], axis=1).astype(BF16)
    wz = jnp.pad(w_in[:, o_z:o_z + 2 * GATE_RANK], ((0, 0), (0, LANES - 2 * GATE_RANK))).astype(BF16)
    gup = jnp.zeros((LANES, 2 * B_QK), F32)
    gup = gup.at[:GATE_RANK, :B_QK].set(gla_gate_up[0, 0]).at[GATE_RANK:2 * GATE_RANK, B_QK:].set(gla_gate_up[0, 1])
    gb = gla_gate_b[0].reshape(1, 2 * B_QK)
    pf, pb = _inproj(x, mod, wmain, wz, gup.astype(BF16), gb, hgrn_lb, 0, 0)

    s0_a = state_hgrn[:, 0]
    s0_b = state_gla[:, 0].reshape(DEC_BATCH, 2, B_HEADS // 2, LANES, B_DV)
    oa_p, st_a = _scan(pf, pb, norm_a[0], None, prompt=True, pair=False)
    ob_p, st_b = _scan(pf, pb, norm_b[0], None, prompt=True, pair=True)
    (oa_s,) = _scan(pf, pb, norm_a[0], s0_a, prompt=False, pair=False)
    (ob_s,) = _scan(pf, pb, norm_b[0], s0_b, prompt=False, pair=True)
    w_out = w_out_ab[0].astype(BF16)
    x = _outproj([(oa_p, oa_s), (ob_p, ob_s)], [w_out[:A_W], w_out[A_W:]], x, mod, ln_g, ln_b, 0)
    x, *ws_10 = ffn([x], ws_01, 0, 1, cast_next=(1, 0))
    new_hgrn = st_a.reshape(BATCH, 1, 2, A_HEADS, A_DK, A_DV)
    new_gla = st_b.reshape(BATCH, 1, 2, B_HEADS, B_DK, B_DV)

    x, *ws_11 = ffn([x], ws_10, 1, 0, cast_next=(1, 1))
    cos, sin = _rope_tables()
    w_qkv = w_qkv_c[0].astype(BF16)
    q, k, v, kt, vt = _qkv(x, mod, w_qkv, w_qkv[:, C_Q:].T, cos, sin, 1)
    sink_cols = jnp.repeat(sink_c[0], C_HEAD_DIM).reshape(1, C_Q)
    o_p = _ctx_attn(q, kt, v, sink_cols)
    ck = cache_k[:, 0].reshape(DEC_BATCH * PAST_LEN, C_KV)
    cv = cache_v[:, 0].reshape(DEC_BATCH * PAST_LEN, C_KV)
    o_s = _lat_attn(q, k, v, ck, cv, sink_cols)
    x = _outproj([(o_p, o_s)], [w_out_c[0].astype(BF16)], x, mod, ln_g, ln_b, 1)
    y_p, y_s = ffn([x], ws_11, 1, 1, split_out=True)

    def cache_layout(zt):
        return zt.reshape(BATCH, 1, C_KV_HEADS, C_HEAD_DIM, SEQ).transpose(0, 1, 4, 2, 3)

    new_k, new_v = cache_layout(kt), cache_layout(vt)

    return (y_p.reshape(BATCH, SEQ, D), y_s.reshape(DEC_BATCH, DEC_SEQ, D), new_hgrn, new_gla, new_k, new_v)
```

```python
import functools
import math

import jax
import jax.numpy as jnp
import numpy as np
from jax import lax
from jax.experimental import pallas as pl
from jax.experimental.pallas import tpu as pltpu

D = 1024
BATCH, SEQ = 16, 256
DEC_BATCH, DEC_SEQ = 2, 2048
PAST_LEN = 512
GRID_W = 64
D_FF = 2816
N_MOD = 9
A_HEADS, A_DK, A_DV = 4, 128, 128
A_W = A_HEADS * A_DK
B_HEADS, B_DK, B_DV = 4, 64, 128
B_QK = B_HEADS * B_DK
B_V = B_HEADS * B_DV
GATE_RANK = 16
GLA_TAU = 16.0
CHUNK = 128
C_HEADS, C_KV_HEADS, C_HEAD_DIM = 16, 4, 64
C_GROUPS = C_HEADS // C_KV_HEADS
C_Q = C_HEADS * C_HEAD_DIM
C_KV = C_KV_HEADS * C_HEAD_DIM
WINDOW = 128
ROPE_FREQS = C_HEAD_DIM // 4
ROPE_BASE = 10000.0
DEPTH = 2
ALPHA = (2.0 * DEPTH) ** 0.25
LN_EPS = 1e-5
RMS_EPS = 1e-6

TP = BATCH * SEQ
TS = DEC_BATCH * DEC_SEQ
T = TP + TS
N_SEG = 1 + DEC_BATCH

LANES = 128
HALF = LANES // 2
FFN_TM = 1024
FFN_TM_SPLIT = 512
FFN_SUB = 256
PROJ_TM = 2 * SEQ
INPROJ_TM = 512
PROJ_SUB = 256
OUT_TM = 1024
VMEM_LIMIT = 56 * 1024 * 1024

F32 = jnp.float32
BF16 = jnp.bfloat16


def _dot(a, b):
    return jnp.dot(a, b, preferred_element_type=F32)


def _dot_nt(a, b):
    return lax.dot_general(a, b, (((1,), (1,)), ((), ())), preferred_element_type=F32)


def _dot_tn(a, b):
    return lax.dot_general(a, b, (((0,), (0,)), ((), ())), preferred_element_type=F32)


def _silu(x):
    return x * jax.nn.sigmoid(x)


def _layer_norm(z, g, b):
    mu = jnp.mean(z, axis=-1, keepdims=True)
    zc = z - mu
    var = jnp.mean(zc * zc, axis=-1, keepdims=True)
    return zc * lax.rsqrt(var + LN_EPS) * g + b


def _seg_of_tile(i, tm):
    n_p = TP // tm
    n_s = DEC_SEQ // tm
    return jnp.where(i < n_p, 0, 1 + lax.div(jnp.maximum(i - n_p, 0), n_s))


def _params(n_axes):
    return pltpu.CompilerParams(dimension_semantics=("arbitrary",) * n_axes, vmem_limit_bytes=VMEM_LIMIT)


def _resident(shape):
    nd = len(shape)
    return pl.BlockSpec(shape, lambda *_: (0,) * nd, pipeline_mode=pl.Buffered(1))


def _resident_slice(shape, lead):
    block = (None,) * len(lead) + tuple(shape)
    return pl.BlockSpec(block, lambda *_: tuple(lead) + (0,) * len(shape), pipeline_mode=pl.Buffered(1))


def _mod_specs(layer, cols, tm):
    return [pl.BlockSpec((None, None, 1, D), functools.partial(
        lambda i, c: (layer, _seg_of_tile(i, tm), 0, c), c=c)) for c in cols]


def _ln_specs(layer, idx):
    return [_resident_slice((1, D), (layer, idx))] * 2


BF16_SUBLANES = 16


def _cast_plan(ws, lead, n_steps, step_of):
    in_specs, out_specs, out_shapes = [], [], []
    for w in ws:
        rows, cols = w.shape[len(lead):]
        blk = next(b for b in range(BF16_SUBLANES, rows + 1, BF16_SUBLANES)
                   if rows % b == 0 and rows // b <= n_steps)
        last = rows // blk - 1
        in_specs.append(pl.BlockSpec((None,) * len(lead) + (blk, cols), functools.partial(
            lambda *g, last: tuple(lead) + (jnp.minimum(step_of(*g), last), 0), last=last)))
        out_specs.append(pl.BlockSpec((blk, cols), functools.partial(
            lambda *g, last: (jnp.minimum(step_of(*g), last), 0), last=last)))
        out_shapes.append(jax.ShapeDtypeStruct((rows, cols), BF16))
    return in_specs, out_specs, out_shapes


def _hosting_casts(body, n_in, n_out, n_cast):
    def hosted(*refs, **kw):
        ins, refs = refs[:n_in], refs[n_in:]
        cast_in, refs = refs[:n_cast], refs[n_cast:]
        outs, refs = refs[:n_out], refs[n_out:]
        cast_out, scratch = refs[:n_cast], refs[n_cast:]
        for src, dst in zip(cast_in, cast_out):
            dst[...] = src[...].astype(BF16)
        body(*ins, *outs, *scratch, **kw)
    return hosted


def _mod_body(c_ref, w_ref, b_ref, o_ref):
    c = c_ref[...]
    s = _silu(c).astype(BF16)
    o_ref[0] = _dot(s, w_ref[0].astype(BF16)) + b_ref[0]


def _mod_vectors(cs, w_mod, b_mod):
    tn = 1536
    n = N_MOD * D
    return pl.pallas_call(
        _mod_body,
        grid=(DEPTH, n // tn),
        in_specs=[
            pl.BlockSpec((8, D), lambda l, j: (0, 0)),
            pl.BlockSpec((1, D, tn), lambda l, j: (l, 0, j)),
            pl.BlockSpec((1, 1, tn), lambda l, j: (l, 0, j)),
        ],
        out_specs=pl.BlockSpec((1, 8, tn), lambda l, j: (l, 0, j)),
        out_shape=jax.ShapeDtypeStruct((DEPTH, 8, n), F32),
        compiler_params=_params(2),
        name="mod_vectors",
    )(cs, w_mod, b_mod.reshape(DEPTH, 1, n))


def _ffn_body(*refs, n_x, n_o, tm):
    x_refs = refs[:n_x]
    shift_ref, scale_ref, gate_ref, w1_ref, w3_ref, w2_ref, g_ref, b_ref = refs[n_x:n_x + 8]
    o_refs = refs[n_x + 8:]

    def compute(x_ref, o_ref):
        shift, scale, gate = shift_ref[...], scale_ref[...], gate_ref[...]
        for r in range(0, tm, FFN_SUB):
            rows = slice(r, r + FFN_SUB)
            x = x_ref[rows, :]
            h = (x * (1.0 + scale) + shift).astype(BF16)
            a = _dot(h, w1_ref[...])
            b = _dot(h, w3_ref[...])
            g = (_silu(a) * b).astype(BF16)
            y = _dot(g, w2_ref[...])
            z = ALPHA * x + (0.5 * gate) * y
            o_ref[rows, :] = _layer_norm(z, g_ref[...], b_ref[...])

    if n_x == 1 and n_o == 1:
        compute(x_refs[0], o_refs[0])
    else:
        in_prompt = pl.program_id(0) < TP // tm
        pl.when(in_prompt)(lambda: compute(x_refs[0], o_refs[0]))
        pl.when(jnp.logical_not(in_prompt))(lambda: compute(x_refs[-1], o_refs[-1]))


def _group_specs(split, tm, width=D):
    if not split:
        return [pl.BlockSpec((tm, width), lambda i: (i, 0))]
    n_p = TP // tm
    return [pl.BlockSpec((tm, width), lambda i: (jnp.minimum(i, n_p - 1), 0)),
            pl.BlockSpec((tm, width), lambda i: (jnp.maximum(i - n_p, 0), 0))]


def _ffn_sublayer(xs, mod, w1, w3, w2, ln_g, ln_b, layer, sub, split_out=False, casts=None):
    n_x, n_o = len(xs), 2 if split_out else 1
    tm = FFN_TM if n_x == n_o == 1 else FFN_TM_SPLIT
    out_shape = ([jax.ShapeDtypeStruct((TP, D), F32), jax.ShapeDtypeStruct((TS, D), F32)] if split_out
                 else [jax.ShapeDtypeStruct((T, D), F32)])
    mod_lo = 6 * sub
    body = functools.partial(_ffn_body, n_x=n_x, n_o=n_o, tm=tm)
    in_specs = _group_specs(n_x == 2, tm) + _mod_specs(layer, (mod_lo, mod_lo + 1, mod_lo + 2), tm) + [
        _resident((D, D_FF)),
        _resident((D, D_FF)),
        _resident((D_FF, D)),
    ] + _ln_specs(layer, 2 * sub)
    out_specs = _group_specs(split_out, tm)
    args = [*xs, mod, mod, mod, w1, w3, w2, ln_g, ln_b]
    if casts is not None:
        ws, lead = casts
        c_in, c_out, c_shapes = _cast_plan(ws, lead, T // tm, lambda i: i)
        body = _hosting_casts(body, len(in_specs), len(out_specs), len(ws))
        in_specs, out_specs, out_shape = in_specs + c_in, out_specs + c_out, out_shape + c_shapes
        args = args + list(ws)
    return pl.pallas_call(
        body,
        grid=(T // tm,),
        in_specs=in_specs,
        out_specs=out_specs,
        out_shape=out_shape,
        compiler_params=_params(1),
        name="ffn_sublayer",
    )(*args)


PF_AQ, PF_FF, PF_FB, PF_AG = 0, 512, 1024, 1536
PF_BQ, PF_BK, PF_BG, PF_LAF, PF_LAB = 2048, 2304, 2560, 3072, 3328
PF_W = 3584
PB_AV, PB_BV = 0, 512
PB_W = 1024
WM_AQ, WM_FF, WM_FB, WM_AG, WM_BQ, WM_BK, WM_BG, WM_AI, WM_BV = 0, 512, 1024, 1536, 2048, 2304, 2560, 3072, 3584
WM_W = 4096


def _log_sigmoid(x):
    return jnp.minimum(x, 0.0) - jnp.log(1.0 + jnp.exp(-jnp.abs(x)))


def _inproj_body(x_ref, shift_ref, scale_ref, w_ref, wz_ref, gu_ref, gb_ref, lb_ref, pf_ref, pb_ref, *, layer_e):
    def lower_bound(d):
        l = lb_ref[d]
        e = jnp.exp(l - jnp.max(l, axis=0, keepdims=True))
        sm = e / jnp.sum(e, axis=0, keepdims=True)
        return jnp.sum(sm[:layer_e + 1], axis=0, keepdims=True)

    lbs = [lower_bound(0), lower_bound(1)]
    for r in range(0, x_ref.shape[0], PROJ_SUB):
        rows = slice(r, r + PROJ_SUB)
        h = (x_ref[rows, :] * (1.0 + scale_ref[...]) + shift_ref[...]).astype(BF16)

        def proj(off, width):
            return _dot(h, w_ref[:, off:off + width])

        pf_ref[rows, PF_AQ:PF_AQ + A_W] = proj(WM_AQ, A_W)
        for lb, wm, pf in ((lbs[0], WM_FF, PF_FF), (lbs[1], WM_FB, PF_FB)):
            pf_ref[rows, pf:pf + A_W] = lb + (1.0 - lb) * jax.nn.sigmoid(proj(wm, A_W))
        pf_ref[rows, PF_AG:PF_AG + A_W] = _silu(proj(WM_AG, A_W))
        pf_ref[rows, PF_BQ:PF_BQ + B_QK] = proj(WM_BQ, B_QK) * (B_DK ** -0.5)
        pf_ref[rows, PF_BK:PF_BK + B_QK] = proj(WM_BK, B_QK)
        pf_ref[rows, PF_BG:PF_BG + B_V] = _silu(proj(WM_BG, B_V))
        pb_ref[rows, PB_AV:PB_AV + A_W] = _silu(proj(WM_AI, A_W)).astype(BF16)
        pb_ref[rows, PB_BV:PB_BV + B_V] = proj(WM_BV, B_V).astype(BF16)
        z = _dot(h, wz_ref[...]).astype(BF16)
        pre = _dot(z, gu_ref[...]) + gb_ref[...]
        pf_ref[rows, PF_LAF:PF_LAF + 2 * B_QK] = _log_sigmoid(pre) * (1.0 / GLA_TAU)


def _inproj(x, mod, wmain, wz, gup, gb, hgrn_lb, layer, layer_e, tm=INPROJ_TM):
    n_l = hgrn_lb.shape[1]
    return pl.pallas_call(
        functools.partial(_inproj_body, layer_e=layer_e),
        grid=(T // tm,),
        in_specs=[pl.BlockSpec((tm, D), lambda i: (i, 0))] + _mod_specs(layer, (3, 4), tm) + [
            _resident((D, WM_W)),
            _resident((D, LANES)),
            _resident((LANES, 2 * B_QK)),
            _resident((1, 2 * B_QK)),
            _resident((2, n_l, A_W)),
        ],
        out_specs=[
            pl.BlockSpec((tm, PF_W), lambda i: (i, 0)),
            pl.BlockSpec((tm, PB_W), lambda i: (i, 0)),
        ],
        out_shape=[jax.ShapeDtypeStruct((T, PF_W), F32), jax.ShapeDtypeStruct((T, PB_W), BF16)],
        compiler_params=_params(1),
        name="ab_inproj",
    )(x, mod, mod, wmain, wz, gup, gb, hgrn_lb)


SCAN_PROMPT_SEQS = 4
SCAN_UNROLL = 8


def _prefix_rows(x):
    row = lax.broadcasted_iota(jnp.int32, x.shape, 0)
    s = 1
    while s < x.shape[0]:
        x = x + jnp.where(row >= s, pltpu.roll(x, s, axis=0), 0.0)
        s *= 2
    return x


def _scan_body(*refs, seq_len, seqs, pair, has_s0, emit_state):
    n = seq_len // CHUNK
    n_all = seqs * n
    nh = 2 if pair else 1
    it = iter(refs)
    q_ref = next(it)
    if pair:
        k_ref, laf_ref, lab_ref = next(it), next(it), next(it)
    else:
        ff_ref, fb_ref = next(it), next(it)
    g_ref, v_ref, nw_ref = next(it), next(it), next(it)
    s0_ref = next(it) if has_s0 else None
    o_ref = next(it)
    st_ref = next(it) if emit_state else None
    qd_s, oi_s, kv_s, dec_s, sb_s = it

    row = lax.broadcasted_iota(jnp.int32, (CHUNK, CHUNK), 0)
    col = lax.broadcasted_iota(jnp.int32, (CHUNK, CHUNK), 1)
    tril = row >= col
    triu = row <= col
    lane = lax.broadcasted_iota(jnp.int32, (1, LANES), 1)
    lane2 = lax.broadcasted_iota(jnp.int32, (1, 2 * LANES), 1)
    if pair:
        masks = [lane < HALF, lane >= HALF]
        masks2 = [(lane2 % LANES) < HALF, (lane2 % LANES) >= HALF]
    else:
        masks, masks2 = [None], [None]

    def pick(mask, x):
        return x if mask is None else jnp.where(mask, x, jnp.zeros_like(x))

    def rows_of(c):
        return pl.ds(pl.multiple_of(c * CHUNK, CHUNK), CHUNK)

    def loop(body):
        if n_all <= SCAN_UNROLL:
            for c in range(n_all):
                body(c)
        else:
            def fbody(i, carry):
                for u in range(SCAN_UNROLL):
                    body(i * SCAN_UNROLL + u)
                return carry
            lax.fori_loop(0, n_all // SCAN_UNROLL, fbody, 0)

    def phase1(c):
        rows = rows_of(c)
        q = q_ref[rows, :]
        if pair:
            k_f = k_b = k_ref[rows, :]
            la_f, la_b = laf_ref[rows, :], lab_ref[rows, :]
        else:
            f_f, f_b = ff_ref[rows, :], fb_ref[rows, :]
            k_f, k_b = 1.0 - f_f, 1.0 - f_b
            la_f, la_b = jnp.log(f_f), jnp.log(f_b)
        cs = _prefix_rows(jnp.concatenate([la_f, la_b], axis=1))
        cf, cbi = cs[:, :LANES], cs[:, LANES:]
        tot_f, tot_b = cf[CHUNK - 1:CHUNK, :], cbi[CHUNK - 1:CHUNK, :]
        rb = tot_b - cbi + la_b
        ref_f, ref_b = cf[CHUNK // 2 - 1:CHUNK // 2, :], rb[CHUNK // 2:CHUNK // 2 + 1, :]
        qtf = q * jnp.exp(cf - ref_f)
        qtb = q * jnp.exp(rb - ref_b)
        ktf = k_f * jnp.exp(ref_f - cf)
        ktb = k_b * jnp.exp(ref_b - rb)
        qd = jnp.concatenate([qtf * jnp.exp(ref_f), qtb * jnp.exp(ref_b)], axis=1).astype(BF16)
        ku = jnp.concatenate([ktf * jnp.exp(tot_f - ref_f), ktb * jnp.exp(tot_b - ref_b)], axis=1).astype(BF16)
        qd_s[rows, :] = qd
        qt = jnp.concatenate([qtf, qtb], axis=0).astype(BF16)
        kt = jnp.concatenate([ktf, ktb], axis=0).astype(BF16)
        kv = None
        for hh in range(nh):
            v = v_ref[rows, hh * LANES:(hh + 1) * LANES]
            sc = _dot_nt(pick(masks[hh], qt), kt)
            att = jnp.where(tril, sc[:CHUNK, :CHUNK], 0.0) + jnp.where(triu, sc[CHUNK:, CHUNK:], 0.0)
            oi_s[rows, hh * LANES:(hh + 1) * LANES] = _dot(att.astype(BF16), v)
            kv_h = _dot_tn(v, ku)
            kv = kv_h if kv is None else jnp.where(masks2[0], kv, kv_h)
        kv_s[c] = kv
        dec_s[c] = jnp.exp(jnp.concatenate([tot_f, tot_b], axis=1))

    loop(phase1)

    def recurrence(sq, d, reverse):
        cols = slice(d * LANES, (d + 1) * LANES)
        c0 = sq * n
        st0 = s0_ref[sq, d, 0].T if has_s0 else jnp.zeros((LANES, LANES), F32)

        def step(c, st):
            sb_s[c, :, cols] = st.astype(BF16)
            return st * dec_s[c, :, cols] + kv_s[c, :, cols]

        if n <= 8:
            st = st0
            for c in (range(n - 1, -1, -1) if reverse else range(n)):
                st = step(c0 + c, st)
        else:
            st = lax.fori_loop(0, n, lambda i, st: step(c0 + (n - 1 - i if reverse else i), st), st0)
        if emit_state:
            st_ref[sq, d, 0] = st.T

    for sq in range(seqs):
        recurrence(sq, 0, False)
        recurrence(sq, 1, True)

    nw = nw_ref[...]

    def phase2(c):
        rows = rows_of(c)
        qcat = qd_s[rows, :]
        scat = sb_s[c]
        for hh in range(nh):
            cols = slice(hh * LANES, (hh + 1) * LANES)
            o = oi_s[rows, cols] + _dot_nt(pick(masks2[hh], qcat), scat)
            o = o * lax.rsqrt(jnp.mean(o * o, axis=-1, keepdims=True) + RMS_EPS) * nw
            o_ref[rows, cols] = (o * g_ref[rows, cols]).astype(BF16)

    loop(phase2)


def _scan(pf, pb, norm_w, s0, *, prompt, pair):
    seq_len = SEQ if prompt else DEC_SEQ
    nseq = BATCH if prompt else DEC_BATCH
    seqs = SCAN_PROMPT_SEQS if prompt else 1
    rows = seqs * seq_len
    row_off = 0 if prompt else TP // rows
    units = B_HEADS // 2 if pair else A_HEADS
    nh = 2 if pair else 1
    n_all = rows // CHUNK
    has_s0 = s0 is not None
    emit_state = prompt

    def colspec(off, width=LANES):
        base = off // width
        return pl.BlockSpec((rows, width), lambda s, u: (s + row_off, base + u))

    if pair:
        in_specs = [colspec(PF_BQ), colspec(PF_BK), colspec(PF_LAF), colspec(PF_LAB),
                    colspec(PF_BG, 2 * LANES), colspec(PB_BV, 2 * LANES)]
        args = [pf, pf, pf, pf, pf, pb]
    else:
        in_specs = [colspec(PF_AQ), colspec(PF_FF), colspec(PF_FB), colspec(PF_AG), colspec(PB_AV)]
        args = [pf, pf, pf, pf, pb]
    in_specs.append(pl.BlockSpec((1, LANES), lambda s, u: (0, 0)))
    args.append(norm_w.reshape(1, LANES))
    state_spec = pl.BlockSpec((seqs, 2, 1, LANES, LANES), lambda s, u: (s, 0, u, 0, 0))
    if has_s0:
        in_specs.append(state_spec)
        args.append(s0)
    out_specs = [pl.BlockSpec((rows, nh * LANES), lambda s, u: (s, u))]
    out_shape = [jax.ShapeDtypeStruct((nseq * seq_len, units * nh * LANES), BF16)]
    if emit_state:
        out_specs.append(state_spec)
        out_shape.append(jax.ShapeDtypeStruct((nseq, 2, units, LANES, LANES), F32))
    scratch = [
        pltpu.VMEM((rows, 2 * LANES), BF16),
        pltpu.VMEM((rows, nh * LANES), F32),
        pltpu.VMEM((n_all, LANES, 2 * LANES), F32),
        pltpu.VMEM((n_all, 1, 2 * LANES), F32),
        pltpu.VMEM((n_all, LANES, 2 * LANES), BF16),
    ]
    return pl.pallas_call(
        functools.partial(_scan_body, seq_len=seq_len, seqs=seqs, pair=pair, has_s0=has_s0,
                          emit_state=emit_state),
        grid=(nseq // seqs, units),
        in_specs=in_specs,
        out_specs=out_specs,
        out_shape=out_shape,
        scratch_shapes=scratch,
        compiler_params=_params(2),
        name=f"scan_{'p' if prompt else 's'}_{'gla' if pair else 'hgrn'}",
    )(*args)


def _outproj_body(*refs, n_lhs, tm):
    lhs = refs[:2 * n_lhs]
    ws = refs[2 * n_lhs:3 * n_lhs]
    x_ref, m_ref, g_ref, b_ref, o_ref = refs[3 * n_lhs:]

    def compute(group):
        for r in range(0, tm, PROJ_SUB):
            rows = slice(r, r + PROJ_SUB)
            y = _dot(lhs[group][rows, :], ws[0][...])
            for j in range(1, n_lhs):
                y = y + _dot(lhs[2 * j + group][rows, :], ws[j][...])
            z = ALPHA * x_ref[rows, :] + m_ref[...] * y
            o_ref[rows, :] = _layer_norm(z, g_ref[...], b_ref[...])

    in_prompt = pl.program_id(0) < TP // tm
    pl.when(in_prompt)(lambda: compute(0))
    pl.when(jnp.logical_not(in_prompt))(lambda: compute(1))


def _outproj(lhs, ws, x, mod, ln_g, ln_b, layer, tm=OUT_TM):
    n_lhs = len(lhs)
    in_specs, args = [], []
    for a_p, a_s in lhs:
        in_specs += _group_specs(True, tm, a_p.shape[1])
        args += [a_p, a_s]
    in_specs += [_resident(w.shape) for w in ws]
    in_specs += [pl.BlockSpec((tm, D), lambda i: (i, 0))] + _mod_specs(layer, (5,), tm) + _ln_specs(layer, 1)
    return pl.pallas_call(
        functools.partial(_outproj_body, n_lhs=n_lhs, tm=tm),
        grid=(T // tm,),
        in_specs=in_specs,
        out_specs=pl.BlockSpec((tm, D), lambda i: (i, 0)),
        out_shape=jax.ShapeDtypeStruct((T, D), F32),
        compiler_params=_params(1),
        name="mixer_outproj",
    )(*args, *ws, x, mod, ln_g, ln_b)


def _rope_partner(x):
    lane = lax.broadcasted_iota(jnp.int32, x.shape, 1)
    first_half = (lane % (2 * ROPE_FREQS)) < ROPE_FREQS
    return jnp.where(first_half, pltpu.roll(x, LANES - ROPE_FREQS, axis=1), pltpu.roll(x, ROPE_FREQS, axis=1))


def _qkv_body(x_ref, shift_ref, scale_ref, w_ref, wkvt_ref, cos_ref, sin_ref,
              q_ref, k_ref, v_ref, kt_ref, vt_ref):
    qscale = C_HEAD_DIM ** -0.5
    in_prompt = pl.program_id(0) < TP // PROJ_TM

    def modulated(rows):
        return (x_ref[rows, :] * (1.0 + scale_ref[...]) + shift_ref[...]).astype(BF16)

    @pl.when(in_prompt)
    def _():
        for sq in range(PROJ_TM // SEQ):
            rows = slice(sq * SEQ, (sq + 1) * SEQ)
            h = modulated(rows)
            q_ref[rows, :] = (_dot(h, w_ref[:, :C_Q]) * qscale).astype(BF16)
            v_ref[rows, :] = _dot(h, w_ref[:, C_Q + C_KV:]).astype(BF16)
            kt_ref[sq] = _dot_nt(wkvt_ref[:C_KV, :], h)
            vt_ref[sq] = _dot_nt(wkvt_ref[C_KV:, :], h)

    @pl.when(jnp.logical_not(in_prompt))
    def _():
        for sq in range(PROJ_TM // SEQ):
            rows = slice(sq * SEQ, (sq + 1) * SEQ)
            h = modulated(rows)
            cos, sin = cos_ref[rows, :], sin_ref[rows, :]
            zq = _dot(h, w_ref[:, :C_Q])
            zk = _dot(h, w_ref[:, C_Q:C_Q + C_KV])
            v_ref[rows, :] = _dot(h, w_ref[:, C_Q + C_KV:]).astype(BF16)

            def rope(z, cos=cos, sin=sin):
                return z * cos + _rope_partner(z) * sin

            for j in range(C_Q // LANES):
                cols = slice(j * LANES, (j + 1) * LANES)
                q_ref[rows, cols] = (rope(zq[:, cols]) * qscale).astype(BF16)
            for j in range(C_KV // LANES):
                cols = slice(j * LANES, (j + 1) * LANES)
                k_ref[rows, cols] = rope(zk[:, cols]).astype(BF16)


def _qkv(x, mod, w, wkvt, cos, sin, layer):
    tm = PROJ_TM
    n_p = TP // tm
    lat = lambda i: (jnp.maximum(i - n_p, 0), 0)
    ctx = lambda i: (jnp.minimum(i, n_p - 1), 0, 0)
    return pl.pallas_call(
        _qkv_body,
        grid=(T // tm,),
        in_specs=[pl.BlockSpec((tm, D), lambda i: (i, 0))] + _mod_specs(layer, (3, 4), tm) + [
            _resident((D, C_Q + 2 * C_KV)),
            _resident((2 * C_KV, D)),
            pl.BlockSpec((tm, LANES), lat),
            pl.BlockSpec((tm, LANES), lat),
        ],
        out_specs=[
            pl.BlockSpec((tm, C_Q), lambda i: (i, 0)),
            pl.BlockSpec((tm, C_KV), lat),
            pl.BlockSpec((tm, C_KV), lambda i: (i, 0)),
            pl.BlockSpec((tm // SEQ, C_KV, SEQ), ctx),
            pl.BlockSpec((tm // SEQ, C_KV, SEQ), ctx),
        ],
        out_shape=[jax.ShapeDtypeStruct((T, C_Q), BF16), jax.ShapeDtypeStruct((TS, C_KV), BF16),
                   jax.ShapeDtypeStruct((T, C_KV), BF16),
                   jax.ShapeDtypeStruct((BATCH, C_KV, SEQ), F32), jax.ShapeDtypeStruct((BATCH, C_KV, SEQ), F32)],
        compiler_params=_params(1),
        name="c_qkv",
    )(x, mod, mod, w, wkvt, cos, sin)


def _rope_tables():
    t = np.arange(DEC_SEQ)
    pos = np.stack([t // GRID_W, t % GRID_W], axis=1).astype(np.float32)
    inv = (ROPE_BASE ** (-np.arange(ROPE_FREQS, dtype=np.float32) / ROPE_FREQS)).astype(np.float32)
    d = np.arange(C_HEAD_DIM)
    axis = d // (2 * ROPE_FREQS)
    ang = pos[:, axis] * inv[d % ROPE_FREQS][None, :]
    sign = np.where((d % (2 * ROPE_FREQS)) < ROPE_FREQS, -1.0, 1.0)[None, :]
    cos_h, sin_h = np.cos(ang), np.sin(ang) * sign
    reps = LANES // C_HEAD_DIM
    cos = np.tile(np.tile(cos_h, (1, reps)), (DEC_BATCH, 1)).astype(np.float32)
    sin = np.tile(np.tile(sin_h, (1, reps)), (DEC_BATCH, 1)).astype(np.float32)
    return jnp.asarray(cos), jnp.asarray(sin)


def _dup_head(blk, half):
    lane = lax.broadcasted_iota(jnp.int32, blk.shape, 1)
    keep = (lane >= HALF).astype(jnp.int32) == half
    return jnp.where(keep, blk, pltpu.roll(blk, HALF, axis=1))


def _pair_values(v, half):
    lo = lax.broadcasted_iota(jnp.int32, v.shape, 1) < HALF
    v2 = _dup_head(v.astype(F32), half)
    top = jnp.concatenate([jnp.where(lo, v2, 0.0), jnp.where(lo, 1.0, 0.0)], axis=1)
    bot = jnp.concatenate([jnp.where(lo, 0.0, v2), jnp.where(lo, 0.0, 1.0)], axis=1)
    return top.astype(BF16), bot.astype(BF16)


def _softmax_weights(scores, sink):
    mx = sink
    for s in scores:
        mx = jnp.maximum(mx, jnp.max(s, axis=-1, keepdims=True))
    return [jnp.exp(s - mx).astype(BF16) for s in scores], jnp.exp(sink - mx)


def _pair_sinks(sink_ref, jb):
    return (sink_ref[:, jb * LANES:jb * LANES + 1], sink_ref[:, jb * LANES + HALF:jb * LANES + HALF + 1])


CTX_TILES = BATCH * C_KV_HEADS
PIPE_LAG = 2


def _ctx_tile(g, lag):
    t = jnp.clip(g - lag, 0, CTX_TILES - 1)
    return t // C_KV_HEADS, t % C_KV_HEADS


def _ctx_attn_body(q_ref, kt_ref, v_ref, sink_ref, o_ref, s_s, e_s, st_s):
    g = pl.program_id(0)
    lo = lax.broadcasted_iota(jnp.int32, (1, LANES), 1) < HALF
    n_pairs = C_GROUPS // 2

    @pl.when(g == 0)
    def _():
        s_s[...] = jnp.zeros_like(s_s)
        e_s[...] = jnp.zeros_like(e_s)
        st_s[...] = jnp.ones_like(st_s)

    def stages(cur, prev):
        v_top, v_bot = _pair_values(v_ref[...], _ctx_tile(g, 2)[1] % 2)
        for jb in range(n_pairs):
            od = _dot(e_s[cur, 2 * jb], v_top) + _dot(e_s[cur, 2 * jb + 1], v_bot)
            o_ref[:, jb * LANES:(jb + 1) * LANES] = (od[:, :LANES] / (od[:, LANES:] + st_s[cur, jb])).astype(BF16)
        for jb in range(n_pairs):
            terms = []
            for hh, sink in enumerate(_pair_sinks(sink_ref, jb)):
                (e,), term = _softmax_weights([s_s[prev, 2 * jb + hh]], sink)
                e_s[prev, 2 * jb + hh] = e
                terms.append(term)
            st_s[prev, jb] = jnp.where(lo, terms[0], terms[1])
        half = _ctx_tile(g, 0)[1] % 2
        kt = kt_ref[pl.ds(pl.multiple_of(half * C_HEAD_DIM, C_HEAD_DIM), C_HEAD_DIM), :]
        zero = jnp.zeros_like(kt)
        kt_a = jnp.concatenate([kt, zero], axis=0).astype(BF16)
        kt_b = jnp.concatenate([zero, kt], axis=0).astype(BF16)
        for jb in range(n_pairs):
            q = q_ref[:, jb * LANES:(jb + 1) * LANES]
            s_s[cur, 2 * jb] = _dot(q, kt_a)
            s_s[cur, 2 * jb + 1] = _dot(q, kt_b)

    pl.when(g % 2 == 0)(lambda: stages(0, 1))
    pl.when(g % 2 == 1)(lambda: stages(1, 0))


def _ctx_attn(q, kt, v, sink_cols):
    gw = C_GROUPS * C_HEAD_DIM

    def at(lag, fn):
        return lambda g: fn(*_ctx_tile(g, lag))

    return pl.pallas_call(
        _ctx_attn_body,
        grid=(CTX_TILES + PIPE_LAG,),
        in_specs=[
            pl.BlockSpec((SEQ, gw), at(0, lambda b, h: (b, h))),
            pl.BlockSpec((None, LANES, SEQ), at(0, lambda b, h: (b, h // 2, 0))),
            pl.BlockSpec((SEQ, LANES), at(2, lambda b, h: (b, h // 2))),
            pl.BlockSpec((1, gw), at(1, lambda b, h: (0, h))),
        ],
        out_specs=pl.BlockSpec((SEQ, gw), at(2, lambda b, h: (b, h))),
        out_shape=jax.ShapeDtypeStruct((TP, C_Q), BF16),
        scratch_shapes=[
            pltpu.VMEM((2, C_GROUPS, SEQ, SEQ), F32),
            pltpu.VMEM((2, C_GROUPS, SEQ, SEQ), BF16),
            pltpu.VMEM((2, C_GROUPS // 2, SEQ, LANES), F32),
        ],
        compiler_params=_params(1),
        name="c_attn_ctx",
    )(q, kt, v, sink_cols)


LAT_TQ = 128
LAT_WIN = LAT_TQ + 2 * WINDOW
LAT_NQ = DEC_SEQ // LAT_TQ
LAT_TILES = DEC_BATCH * C_KV_HEADS * LAT_NQ


def _lat_tile(g, lag):
    t = jnp.clip(g - lag, 0, LAT_TILES - 1)
    bh = t // LAT_NQ
    return bh // C_KV_HEADS, bh % C_KV_HEADS, t % LAT_NQ


def _lat_window(j):
    return pl.multiple_of(jnp.clip(j * LAT_TQ - WINDOW, 0, DEC_SEQ - LAT_WIN), LANES)


def _lat_attn_body(q_ref, k_ref, v_ref, ck_ref, cv_ref, sink_ref, o_ref,
                   ka_s, kb_s, cka_s, ckb_s, va_s, vb_s, cva_s, cvb_s, sl_s, sc_s, el_s, ec_s, st_s):
    g = pl.program_id(0)
    lo = lax.broadcasted_iota(jnp.int32, (1, LANES), 1) < HALF
    n_pairs = C_GROUPS // 2
    b_a, h_a, j_a = _lat_tile(g, 0)
    b_c, h_c, j_c = _lat_tile(g, 2)
    v_slot_a = (b_a * C_KV_HEADS + h_a) % 2
    v_slot_c = (b_c * C_KV_HEADS + h_c) % 2

    @pl.when(g == 0)
    def _():
        sl_s[...] = jnp.zeros_like(sl_s)
        sc_s[...] = jnp.zeros_like(sc_s)
        el_s[...] = jnp.zeros_like(el_s)
        ec_s[...] = jnp.zeros_like(ec_s)
        st_s[...] = jnp.ones_like(st_s)

    @pl.when(jnp.logical_and(j_a == 0, g < LAT_TILES))
    def _():
        half = h_a % 2
        for src, a_s, b_s in ((k_ref, ka_s, kb_s), (ck_ref, cka_s, ckb_s)):
            lo2 = lax.broadcasted_iota(jnp.int32, src.shape, 1) < HALF
            k2 = _dup_head(src[...].astype(F32), half)
            a_s[...] = jnp.where(lo2, k2, 0.0).astype(BF16)
            b_s[...] = jnp.where(lo2, 0.0, k2).astype(BF16)
        va_s[v_slot_a], vb_s[v_slot_a] = _pair_values(v_ref[...], half)
        cva_s[v_slot_a], cvb_s[v_slot_a] = _pair_values(cv_ref[...], half)

    def stages(cur, prev):
        win_c = pl.ds(_lat_window(j_c), LAT_WIN)
        v_loc = (va_s[v_slot_c, win_c, :], vb_s[v_slot_c, win_c, :])
        v_ctx = (cva_s[v_slot_c], cvb_s[v_slot_c])
        for jb in range(n_pairs):
            od = None
            for hh in range(2):
                part = _dot(el_s[cur, 2 * jb + hh], v_loc[hh]) + _dot(ec_s[cur, 2 * jb + hh], v_ctx[hh])
                od = part if od is None else od + part
            o_ref[:, jb * LANES:(jb + 1) * LANES] = (od[:, :LANES] / (od[:, LANES:] + st_s[cur, jb])).astype(BF16)
        for jb in range(n_pairs):
            terms = []
            for hh, sink in enumerate(_pair_sinks(sink_ref, jb)):
                (e_loc, e_ctx), term = _softmax_weights([sl_s[prev, 2 * jb + hh], sc_s[prev, 2 * jb + hh]], sink)
                el_s[prev, 2 * jb + hh] = e_loc
                ec_s[prev, 2 * jb + hh] = e_ctx
                terms.append(term)
            st_s[prev, jb] = jnp.where(lo, terms[0], terms[1])
        start = _lat_window(j_a)
        win = pl.ds(start, LAT_WIN)
        qpos = j_a * LAT_TQ + lax.broadcasted_iota(jnp.int32, (LAT_TQ, LAT_WIN), 0)
        kpos = start + lax.broadcasted_iota(jnp.int32, (LAT_TQ, LAT_WIN), 1)
        band = jnp.abs(qpos - kpos) <= WINDOW
        for jb in range(n_pairs):
            q = q_ref[:, jb * LANES:(jb + 1) * LANES]
            for hh, (kl, kc) in enumerate(((ka_s, cka_s), (kb_s, ckb_s))):
                sl_s[cur, 2 * jb + hh] = jnp.where(band, _dot_nt(q, kl[win, :]), -jnp.inf)
                sc_s[cur, 2 * jb + hh] = _dot_nt(q, kc[...])

    pl.when(g % 2 == 0)(lambda: stages(0, 1))
    pl.when(g % 2 == 1)(lambda: stages(1, 0))


def _lat_attn(q, k, v, ck, cv, sink_cols):
    gw = C_GROUPS * C_HEAD_DIM
    q_off = TP // LAT_TQ
    kv_off = TP // DEC_SEQ

    def at(lag, fn):
        return lambda g: fn(*_lat_tile(g, lag))

    return pl.pallas_call(
        _lat_attn_body,
        grid=(LAT_TILES + PIPE_LAG,),
        in_specs=[
            pl.BlockSpec((LAT_TQ, gw), at(0, lambda b, h, j: (q_off + b * LAT_NQ + j, h))),
            pl.BlockSpec((DEC_SEQ, LANES), at(0, lambda b, h, j: (b, h // 2))),
            pl.BlockSpec((DEC_SEQ, LANES), at(0, lambda b, h, j: (kv_off + b, h // 2))),
            pl.BlockSpec((PAST_LEN, LANES), at(0, lambda b, h, j: (b, h // 2))),
            pl.BlockSpec((PAST_LEN, LANES), at(0, lambda b, h, j: (b, h // 2))),
            pl.BlockSpec((1, gw), at(1, lambda b, h, j: (0, h))),
        ],
        out_specs=pl.BlockSpec((LAT_TQ, gw), at(2, lambda b, h, j: (b * LAT_NQ + j, h))),
        out_shape=jax.ShapeDtypeStruct((TS, C_Q), BF16),
        scratch_shapes=[
            pltpu.VMEM((DEC_SEQ, LANES), BF16), pltpu.VMEM((DEC_SEQ, LANES), BF16),
            pltpu.VMEM((PAST_LEN, LANES), BF16), pltpu.VMEM((PAST_LEN, LANES), BF16),
            pltpu.VMEM((2, DEC_SEQ, 2 * LANES), BF16), pltpu.VMEM((2, DEC_SEQ, 2 * LANES), BF16),
            pltpu.VMEM((2, PAST_LEN, 2 * LANES), BF16), pltpu.VMEM((2, PAST_LEN, 2 * LANES), BF16),
            pltpu.VMEM((2, C_GROUPS, LAT_TQ, LAT_WIN), F32), pltpu.VMEM((2, C_GROUPS, LAT_TQ, PAST_LEN), F32),
            pltpu.VMEM((2, C_GROUPS, LAT_TQ, LAT_WIN), BF16), pltpu.VMEM((2, C_GROUPS, LAT_TQ, PAST_LEN), BF16),
            pltpu.VMEM((2, C_GROUPS // 2, LAT_TQ, LANES), F32),
        ],
        compiler_params=_params(1),
        name="c_attn_latent",
    )(q, k, v, ck, cv, sink_cols)


def kernel(x_prompt, x_sample, state_hgrn, state_gla, cache_k, cache_v, c, c_ctx, w_mod, b_mod, ln_g, ln_b,
           ffn_w1, ffn_w3, ffn_w2, w_in_ab, hgrn_lb, gla_gate_up, gla_gate_b, norm_a, norm_b, w_out_ab,
           w_qkv_c, sink_c, w_out_c):
    cs = jnp.zeros((8, D), F32).at[0].set(c_ctx).at[1:1 + DEC_BATCH].set(c)
    mod = _mod_vectors(cs, w_mod, b_mod).reshape(DEPTH, 8, 1, N_MOD * D)
    ffn_ws = (ffn_w1, ffn_w3, ffn_w2)
    ln_g, ln_b = ln_g.reshape(DEPTH, 3, 1, D), ln_b.reshape(DEPTH, 3, 1, D)

    def ffn(xs, ws, layer, sub, split_out=False, cast_next=None):
        casts = None if cast_next is None else (ffn_ws, cast_next)
        return _ffn_sublayer(xs, mod, *ws, ln_g, ln_b, layer, sub, split_out=split_out, casts=casts)

    x, *ws_01 = ffn([x_prompt.reshape(TP, D), x_sample.reshape(TS, D)], [w[0, 0].astype(BF16) for w in ffn_ws],
                    0, 0, cast_next=(0, 1))
    w_in = w_in_ab[0]
    o_aq, o_ai, o_ff, o_fb, o_ag = 0, A_W, 2 * A_W, 3 * A_W, 4 * A_W
    o_bq = 5 * A_W
    o_bk, o_bv = o_bq + B_QK, o_bq + 2 * B_QK
    o_bg = o_bv + B_V
    o_z = o_bg + B_V
    order = [(o_aq, A_W), (o_ff, A_W), (o_fb, A_W), (o_ag, A_W), (o_bq, B_QK), (o_bk, B_QK), (o_bg, B_V),
             (o_ai, A_W), (o_bv, B_V)]
    wmain = jnp.concatenate([w_in[:, o:o + w] for o, w in order], axis=1).astype(BF16)
    wz = jnp.pad(w_in[:, o_z:o_z + 2 * GATE_RANK], ((0, 0), (0, LANES - 2 * GATE_RANK))).astype(BF16)
    gup = jnp.zeros((LANES, 2 * B_QK), F32)
    gup = gup.at[:GATE_RANK, :B_QK].set(gla_gate_up[0, 0]).at[GATE_RANK:2 * GATE_RANK, B_QK:].set(gla_gate_up[0, 1])
    gb = gla_gate_b[0].reshape(1, 2 * B_QK)
    pf, pb = _inproj(x, mod, wmain, wz, gup.astype(BF16), gb, hgrn_lb, 0, 0)

    s0_a = state_hgrn[:, 0]
    s0_b = state_gla[:, 0].reshape(DEC_BATCH, 2, B_HEADS // 2, LANES, B_DV)
    oa_p, st_a = _scan(pf, pb, norm_a[0], None, prompt=True, pair=False)
    ob_p, st_b = _scan(pf, pb, norm_b[0], None, prompt=True, pair=True)
    (oa_s,) = _scan(pf, pb, norm_a[0], s0_a, prompt=False, pair=False)
    (ob_s,) = _scan(pf, pb, norm_b[0], s0_b, prompt=False, pair=True)
    w_out = w_out_ab[0].astype(BF16)
    x = _outproj([(oa_p, oa_s), (ob_p, ob_s)], [w_out[:A_W], w_out[A_W:]], x, mod, ln_g, ln_b, 0)
    x, *ws_10 = ffn([x], ws_01, 0, 1, cast_next=(1, 0))
    new_hgrn = st_a.reshape(BATCH, 1, 2, A_HEADS, A_DK, A_DV)
    new_gla = st_b.reshape(BATCH, 1, 2, B_HEADS, B_DK, B_DV)

    x, *ws_11 = ffn([x], ws_10, 1, 0, cast_next=(1, 1))
    cos, sin = _rope_tables()
    w_qkv = w_qkv_c[0].astype(BF16)
    q, k, v, kt, vt = _qkv(x, mod, w_qkv, w_qkv[:, C_Q:].T, cos, sin, 1)
    sink_cols = jnp.repeat(sink_c[0], C_HEAD_DIM).reshape(1, C_Q)
    o_p = _ctx_attn(q, kt, v, sink_cols)
    ck = cache_k[:, 0].reshape(DEC_BATCH * PAST_LEN, C_KV)
    cv = cache_v[:, 0].reshape(DEC_BATCH * PAST_LEN, C_KV)
    o_s = _lat_attn(q, k, v, ck, cv, sink_cols)
    x = _outproj([(o_p, o_s)], [w_out_c[0].astype(BF16)], x, mod, ln_g, ln_b, 1)
    y_p, y_s = ffn([x], ws_11, 1, 1, split_out=True)

    def cache_layout(zt):
        return zt.reshape(BATCH, 1, C_KV_HEADS, C_HEAD_DIM, SEQ).transpose(0, 1, 4, 2, 3)

    new_k, new_v = cache_layout(kt), cache_layout(vt)

    return (y_p.reshape(BATCH, SEQ, D), y_s.reshape(DEC_BATCH, DEC_SEQ, D), new_hgrn, new_gla, new_k, new_v)
```

```python
import functools
import math

import jax
import jax.numpy as jnp
import numpy as np
from jax import lax
from jax.experimental import pallas as pl
from jax.experimental.pallas import tpu as pltpu

D = 1024
BATCH, SEQ = 16, 256
DEC_BATCH, DEC_SEQ = 2, 2048
PAST_LEN = 512
GRID_W = 64
D_FF = 2816
N_MOD = 9
A_HEADS, A_DK, A_DV = 4, 128, 128
A_W = A_HEADS * A_DK
B_HEADS, B_DK, B_DV = 4, 64, 128
B_QK = B_HEADS * B_DK
B_V = B_HEADS * B_DV
GATE_RANK = 16
GLA_TAU = 16.0
CHUNK = 128
C_HEADS, C_KV_HEADS, C_HEAD_DIM = 16, 4, 64
C_GROUPS = C_HEADS // C_KV_HEADS
C_Q = C_HEADS * C_HEAD_DIM
C_KV = C_KV_HEADS * C_HEAD_DIM
WINDOW = 128
ROPE_FREQS = C_HEAD_DIM // 4
ROPE_BASE = 10000.0
DEPTH = 2
ALPHA = (2.0 * DEPTH) ** 0.25
LN_EPS = 1e-5
RMS_EPS = 1e-6

TP = BATCH * SEQ
TS = DEC_BATCH * DEC_SEQ
T = TP + TS
N_SEG = 1 + DEC_BATCH

LANES = 128
HALF = LANES // 2
FFN_TM = 1024
FFN_TM_SPLIT = 512
FFN_SUB = 256
PROJ_TM = 2 * SEQ
INPROJ_TM = 512
PROJ_SUB = 256
OUT_TM = 1024
VMEM_LIMIT = 56 * 1024 * 1024

F32 = jnp.float32
BF16 = jnp.bfloat16


def _dot(a, b):
    return jnp.dot(a, b, preferred_element_type=F32)


def _dot_nt(a, b):
    return lax.dot_general(a, b, (((1,), (1,)), ((), ())), preferred_element_type=F32)


def _dot_tn(a, b):
    return lax.dot_general(a, b, (((0,), (0,)), ((), ())), preferred_element_type=F32)


def _silu(x):
    return x * jax.nn.sigmoid(x)


def _layer_norm(z, g, b):
    mu = jnp.mean(z, axis=-1, keepdims=True)
    zc = z - mu
    var = jnp.mean(zc * zc, axis=-1, keepdims=True)
    return zc * lax.rsqrt(var + LN_EPS) * g + b


def _seg_of_tile(i, tm):
    n_p = TP // tm
    n_s = DEC_SEQ // tm
    return jnp.where(i < n_p, 0, 1 + lax.div(jnp.maximum(i - n_p, 0), n_s))


def _params(n_axes):
    return pltpu.CompilerParams(dimension_semantics=("arbitrary",) * n_axes, vmem_limit_bytes=VMEM_LIMIT)


def _resident(shape):
    nd = len(shape)
    return pl.BlockSpec(shape, lambda *_: (0,) * nd, pipeline_mode=pl.Buffered(1))


def _resident_slice(shape, lead):
    block = (None,) * len(lead) + tuple(shape)
    return pl.BlockSpec(block, lambda *_: tuple(lead) + (0,) * len(shape), pipeline_mode=pl.Buffered(1))


def _mod_specs(layer, cols, tm):
    return [pl.BlockSpec((None, None, 1, D), functools.partial(
        lambda i, c: (layer, _seg_of_tile(i, tm), 0, c), c=c)) for c in cols]


def _ln_specs(layer, idx):
    return [_resident_slice((1, D), (layer, idx))] * 2


BF16_SUBLANES = 16


def _cast_plan(ws, lead, n_steps, step_of):
    in_specs, out_specs, out_shapes = [], [], []
    for w in ws:
        rows, cols = w.shape[len(lead):]
        blk = next(b for b in range(BF16_SUBLANES, rows + 1, BF16_SUBLANES)
                   if rows % b == 0 and rows // b <= n_steps)
        last = rows // blk - 1
        in_specs.append(pl.BlockSpec((None,) * len(lead) + (blk, cols), functools.partial(
            lambda *g, last: tuple(lead) + (jnp.minimum(step_of(*g), last), 0), last=last)))
        out_specs.append(pl.BlockSpec((blk, cols), functools.partial(
            lambda *g, last: (jnp.minimum(step_of(*g), last), 0), last=last)))
        out_shapes.append(jax.ShapeDtypeStruct((rows, cols), BF16))
    return in_specs, out_specs, out_shapes


def _hosting_casts(body, n_in, n_out, n_cast):
    def hosted(*refs, **kw):
        ins, refs = refs[:n_in], refs[n_in:]
        cast_in, refs = refs[:n_cast], refs[n_cast:]
        outs, refs = refs[:n_out], refs[n_out:]
        cast_out, scratch = refs[:n_cast], refs[n_cast:]
        for src, dst in zip(cast_in, cast_out):
            dst[...] = src[...].astype(BF16)
        body(*ins, *outs, *scratch, **kw)
    return hosted


def _mod_body(c_ref, w_ref, b_ref, o_ref):
    c = c_ref[...]
    s = _silu(c).astype(BF16)
    o_ref[0] = _dot(s, w_ref[0].astype(BF16)) + b_ref[0]


def _mod_vectors(cs, w_mod, b_mod):
    tn = 1536
    n = N_MOD * D
    return pl.pallas_call(
        _mod_body,
        grid=(DEPTH, n // tn),
        in_specs=[
            pl.BlockSpec((8, D), lambda l, j: (0, 0)),
            pl.BlockSpec((1, D, tn), lambda l, j: (l, 0, j)),
            pl.BlockSpec((1, 1, tn), lambda l, j: (l, 0, j)),
        ],
        out_specs=pl.BlockSpec((1, 8, tn), lambda l, j: (l, 0, j)),
        out_shape=jax.ShapeDtypeStruct((DEPTH, 8, n), F32),
        compiler_params=_params(2),
        name="mod_vectors",
    )(cs, w_mod, b_mod.reshape(DEPTH, 1, n))


def _ffn_body(*refs, n_x, n_o, tm):
    x_refs = refs[:n_x]
    shift_ref, scale_ref, gate_ref, w1_ref, w3_ref, w2_ref, g_ref, b_ref = refs[n_x:n_x + 8]
    o_refs = refs[n_x + 8:]

    def compute(x_ref, o_ref):
        shift, scale, gate = shift_ref[...], scale_ref[...], gate_ref[...]
        for r in range(0, tm, FFN_SUB):
            rows = slice(r, r + FFN_SUB)
            x = x_ref[rows, :]
            h = (x * (1.0 + scale) + shift).astype(BF16)
            a = _dot(h, w1_ref[...])
            b = _dot(h, w3_ref[...])
            g = (_silu(a) * b).astype(BF16)
            y = _dot(g, w2_ref[...])
            z = ALPHA * x + (0.5 * gate) * y
            o_ref[rows, :] = _layer_norm(z, g_ref[...], b_ref[...])

    if n_x == 1 and n_o == 1:
        compute(x_refs[0], o_refs[0])
    else:
        in_prompt = pl.program_id(0) < TP // tm
        pl.when(in_prompt)(lambda: compute(x_refs[0], o_refs[0]))
        pl.when(jnp.logical_not(in_prompt))(lambda: compute(x_refs[-1], o_refs[-1]))


def _group_specs(split, tm, width=D):
    if not split:
        return [pl.BlockSpec((tm, width), lambda i: (i, 0))]
    n_p = TP // tm
    return [pl.BlockSpec((tm, width), lambda i: (jnp.minimum(i, n_p - 1), 0)),
            pl.BlockSpec((tm, width), lambda i: (jnp.maximum(i - n_p, 0), 0))]


def _ffn_sublayer(xs, mod, w1, w3, w2, ln_g, ln_b, layer, sub, split_out=False, casts=None):
    n_x, n_o = len(xs), 2 if split_out else 1
    tm = FFN_TM if n_x == n_o == 1 else FFN_TM_SPLIT
    out_shape = ([jax.ShapeDtypeStruct((TP, D), F32), jax.ShapeDtypeStruct((TS, D), F32)] if split_out
                 else [jax.ShapeDtypeStruct((T, D), F32)])
    mod_lo = 6 * sub
    body = functools.partial(_ffn_body, n_x=n_x, n_o=n_o, tm=tm)
    in_specs = _group_specs(n_x == 2, tm) + _mod_specs(layer, (mod_lo, mod_lo + 1, mod_lo + 2), tm) + [
        _resident((D, D_FF)),
        _resident((D, D_FF)),
        _resident((D_FF, D)),
    ] + _ln_specs(layer, 2 * sub)
    out_specs = _group_specs(split_out, tm)
    args = [*xs, mod, mod, mod, w1, w3, w2, ln_g, ln_b]
    if casts is not None:
        ws, lead = casts
        c_in, c_out, c_shapes = _cast_plan(ws, lead, T // tm, lambda i: i)
        body = _hosting_casts(body, len(in_specs), len(out_specs), len(ws))
        in_specs, out_specs, out_shape = in_specs + c_in, out_specs + c_out, out_shape + c_shapes
        args = args + list(ws)
    return pl.pallas_call(
        body,
        grid=(T // tm,),
        in_specs=in_specs,
        out_specs=out_specs,
        out_shape=out_shape,
        compiler_params=_params(1),
        name="ffn_sublayer",
    )(*args)


PF_AQ, PF_FF, PF_FB, PF_AG = 0, 512, 1024, 1536
PF_BQ, PF_BK, PF_BG, PF_LAF, PF_LAB = 2048, 2304, 2560, 3072, 3328
PF_W = 3584
PB_AV, PB_BV = 0, 512
PB_W = 1024
WM_AQ, WM_FF, WM_FB, WM_AG, WM_BQ, WM_BK, WM_BG, WM_AI, WM_BV = 0, 512, 1024, 1536, 2048, 2304, 2560, 3072, 3584
WM_W = 4096


def _log_sigmoid(x):
    return jnp.minimum(x, 0.0) - jnp.log(1.0 + jnp.exp(-jnp.abs(x)))


def _inproj_body(x_ref, shift_ref, scale_ref, w_ref, wz_ref, gu_ref, gb_ref, lb_ref, pf_ref, pb_ref, *, layer_e):
    def lower_bound(d):
        l = lb_ref[d]
        e = jnp.exp(l - jnp.max(l, axis=0, keepdims=True))
        sm = e / jnp.sum(e, axis=0, keepdims=True)
        return jnp.sum(sm[:layer_e + 1], axis=0, keepdims=True)

    lbs = [lower_bound(0), lower_bound(1)]
    for r in range(0, x_ref.shape[0], PROJ_SUB):
        rows = slice(r, r + PROJ_SUB)
        h = (x_ref[rows, :] * (1.0 + scale_ref[...]) + shift_ref[...]).astype(BF16)

        def proj(off, width):
            return _dot(h, w_ref[:, off:off + width])

        pf_ref[rows, PF_AQ:PF_AQ + A_W] = proj(WM_AQ, A_W)
        for lb, wm, pf in ((lbs[0], WM_FF, PF_FF), (lbs[1], WM_FB, PF_FB)):
            pf_ref[rows, pf:pf + A_W] = lb + (1.0 - lb) * jax.nn.sigmoid(proj(wm, A_W))
        pf_ref[rows, PF_AG:PF_AG + A_W] = _silu(proj(WM_AG, A_W))
        pf_ref[rows, PF_BQ:PF_BQ + B_QK] = proj(WM_BQ, B_QK) * (B_DK ** -0.5)
        pf_ref[rows, PF_BK:PF_BK + B_QK] = proj(WM_BK, B_QK)
        pf_ref[rows, PF_BG:PF_BG + B_V] = _silu(proj(WM_BG, B_V))
        pb_ref[rows, PB_AV:PB_AV + A_W] = _silu(proj(WM_AI, A_W)).astype(BF16)
        pb_ref[rows, PB_BV:PB_BV + B_V] = proj(WM_BV, B_V).astype(BF16)
        z = _dot(h, wz_ref[...]).astype(BF16)
        pre = _dot(z, gu_ref[...]) + gb_ref[...]
        pf_ref[rows, PF_LAF:PF_LAF + 2 * B_QK] = _log_sigmoid(pre) * (1.0 / GLA_TAU)


def _inproj(x, mod, wmain, wz, gup, gb, hgrn_lb, layer, layer_e, tm=INPROJ_TM):
    n_l = hgrn_lb.shape[1]
    return pl.pallas_call(
        functools.partial(_inproj_body, layer_e=layer_e),
        grid=(T // tm,),
        in_specs=[pl.BlockSpec((tm, D), lambda i: (i, 0))] + _mod_specs(layer, (3, 4), tm) + [
            _resident((D, WM_W)),
            _resident((D, LANES)),
            _resident((LANES, 2 * B_QK)),
            _resident((1, 2 * B_QK)),
            _resident((2, n_l, A_W)),
        ],
        out_specs=[
            pl.BlockSpec((tm, PF_W), lambda i: (i, 0)),
            pl.BlockSpec((tm, PB_W), lambda i: (i, 0)),
        ],
        out_shape=[jax.ShapeDtypeStruct((T, PF_W), F32), jax.ShapeDtypeStruct((T, PB_W), BF16)],
        compiler_params=_params(1),
        name="ab_inproj",
    )(x, mod, mod, wmain, wz, gup, gb, hgrn_lb)


SCAN_PROMPT_SEQS = 4
SCAN_UNROLL = 8


def _prefix_rows(x):
    row = lax.broadcasted_iota(jnp.int32, x.shape, 0)
    s = 1
    while s < x.shape[0]:
        x = x + jnp.where(row >= s, pltpu.roll(x, s, axis=0), 0.0)
        s *= 2
    return x


def _scan_body(*refs, seq_len, seqs, pair, has_s0, emit_state):
    n = seq_len // CHUNK
    n_all = seqs * n
    nh = 2 if pair else 1
    it = iter(refs)
    q_ref = next(it)
    if pair:
        k_ref, laf_ref, lab_ref = next(it), next(it), next(it)
    else:
        ff_ref, fb_ref = next(it), next(it)
    g_ref, v_ref, nw_ref = next(it), next(it), next(it)
    s0_ref = next(it) if has_s0 else None
    o_ref = next(it)
    st_ref = next(it) if emit_state else None
    qd_s, oi_s, kv_s, dec_s, sb_s = it

    row = lax.broadcasted_iota(jnp.int32, (CHUNK, CHUNK), 0)
    col = lax.broadcasted_iota(jnp.int32, (CHUNK, CHUNK), 1)
    tril = row >= col
    triu = row <= col
    lane = lax.broadcasted_iota(jnp.int32, (1, LANES), 1)
    lane2 = lax.broadcasted_iota(jnp.int32, (1, 2 * LANES), 1)
    if pair:
        masks = [lane < HALF, lane >= HALF]
        masks2 = [(lane2 % LANES) < HALF, (lane2 % LANES) >= HALF]
    else:
        masks, masks2 = [None], [None]

    def pick(mask, x):
        return x if mask is None else jnp.where(mask, x, jnp.zeros_like(x))

    def rows_of(c):
        return pl.ds(pl.multiple_of(c * CHUNK, CHUNK), CHUNK)

    def loop(body):
        if n_all <= SCAN_UNROLL:
            for c in range(n_all):
                body(c)
        else:
            def fbody(i, carry):
                for u in range(SCAN_UNROLL):
                    body(i * SCAN_UNROLL + u)
                return carry
            lax.fori_loop(0, n_all // SCAN_UNROLL, fbody, 0)

    def phase1(c):
        rows = rows_of(c)
        q = q_ref[rows, :]
        if pair:
            k_f = k_b = k_ref[rows, :]
            la_f, la_b = laf_ref[rows, :], lab_ref[rows, :]
        else:
            f_f, f_b = ff_ref[rows, :], fb_ref[rows, :]
            k_f, k_b = 1.0 - f_f, 1.0 - f_b
            la_f, la_b = jnp.log(f_f), jnp.log(f_b)
        cs = _prefix_rows(jnp.concatenate([la_f, la_b], axis=1))
        cf, cbi = cs[:, :LANES], cs[:, LANES:]
        tot_f, tot_b = cf[CHUNK - 1:CHUNK, :], cbi[CHUNK - 1:CHUNK, :]
        rb = tot_b - cbi + la_b
        ref_f, ref_b = cf[CHUNK // 2 - 1:CHUNK // 2, :], rb[CHUNK // 2:CHUNK // 2 + 1, :]
        qtf = q * jnp.exp(cf - ref_f)
        qtb = q * jnp.exp(rb - ref_b)
        ktf = k_f * jnp.exp(ref_f - cf)
        ktb = k_b * jnp.exp(ref_b - rb)
        qd = jnp.concatenate([qtf * jnp.exp(ref_f), qtb * jnp.exp(ref_b)], axis=1).astype(BF16)
        ku = jnp.concatenate([ktf * jnp.exp(tot_f - ref_f), ktb * jnp.exp(tot_b - ref_b)], axis=1).astype(BF16)
        qd_s[rows, :] = qd
        qt = jnp.concatenate([qtf, qtb], axis=0).astype(BF16)
        kt = jnp.concatenate([ktf, ktb], axis=0).astype(BF16)
        kv = None
        for hh in range(nh):
            v = v_ref[rows, hh * LANES:(hh + 1) * LANES]
            sc = _dot_nt(pick(masks[hh], qt), kt)
            att = jnp.where(tril, sc[:CHUNK, :CHUNK], 0.0) + jnp.where(triu, sc[CHUNK:, CHUNK:], 0.0)
            oi_s[rows, hh * LANES:(hh + 1) * LANES] = _dot(att.astype(BF16), v)
            kv_h = _dot_tn(v, ku)
            kv = kv_h if kv is None else jnp.where(masks2[0], kv, kv_h)
        kv_s[c] = kv
        dec_s[c] = jnp.exp(jnp.concatenate([tot_f, tot_b], axis=1))

    loop(phase1)

    def recurrence(sq, d, reverse):
        cols = slice(d * LANES, (d + 1) * LANES)
        c0 = sq * n
        st0 = s0_ref[sq, d, 0].T if has_s0 else jnp.zeros((LANES, LANES), F32)

        def step(c, st):
            sb_s[c, :, cols] = st.astype(BF16)
            return st * dec_s[c, :, cols] + kv_s[c, :, cols]

        if n <= 8:
            st = st0
            for c in (range(n - 1, -1, -1) if reverse else range(n)):
                st = step(c0 + c, st)
        else:
            st = lax.fori_loop(0, n, lambda i, st: step(c0 + (n - 1 - i if reverse else i), st), st0)
        if emit_state:
            st_ref[sq, d, 0] = st.T

    for sq in range(seqs):
        recurrence(sq, 0, False)
        recurrence(sq, 1, True)

    nw = nw_ref[...]

    def phase2(c):
        rows = rows_of(c)
        qcat = qd_s[rows, :]
        scat = sb_s[c]
        for hh in range(nh):
            cols = slice(hh * LANES, (hh + 1) * LANES)
            o = oi_s[rows, cols] + _dot_nt(pick(masks2[hh], qcat), scat)
            o = o * lax.rsqrt(jnp.mean(o * o, axis=-1, keepdims=True) + RMS_EPS) * nw
            o_ref[rows, cols] = (o * g_ref[rows, cols]).astype(BF16)

    loop(phase2)


def _scan(pf, pb, norm_w, s0, *, prompt, pair):
    seq_len = SEQ if prompt else DEC_SEQ
    nseq = BATCH if prompt else DEC_BATCH
    seqs = SCAN_PROMPT_SEQS if prompt else 1
    rows = seqs * seq_len
    row_off = 0 if prompt else TP // rows
    units = B_HEADS // 2 if pair else A_HEADS
    nh = 2 if pair else 1
    n_all = rows // CHUNK
    has_s0 = s0 is not None
    emit_state = prompt

    def colspec(off, width=LANES):
        base = off // width
        return pl.BlockSpec((rows, width), lambda s, u: (s + row_off, base + u))

    if pair:
        in_specs = [colspec(PF_BQ), colspec(PF_BK), colspec(PF_LAF), colspec(PF_LAB),
                    colspec(PF_BG, 2 * LANES), colspec(PB_BV, 2 * LANES)]
        args = [pf, pf, pf, pf, pf, pb]
    else:
        in_specs = [colspec(PF_AQ), colspec(PF_FF), colspec(PF_FB), colspec(PF_AG), colspec(PB_AV)]
        args = [pf, pf, pf, pf, pb]
    in_specs.append(pl.BlockSpec((1, LANES), lambda s, u: (0, 0)))
    args.append(norm_w.reshape(1, LANES))
    state_spec = pl.BlockSpec((seqs, 2, 1, LANES, LANES), lambda s, u: (s, 0, u, 0, 0))
    if has_s0:
        in_specs.append(state_spec)
        args.append(s0)
    out_specs = [pl.BlockSpec((rows, nh * LANES), lambda s, u: (s, u))]
    out_shape = [jax.ShapeDtypeStruct((nseq * seq_len, units * nh * LANES), BF16)]
    if emit_state:
        out_specs.append(state_spec)
        out_shape.append(jax.ShapeDtypeStruct((nseq, 2, units, LANES, LANES), F32))
    scratch = [
        pltpu.VMEM((rows, 2 * LANES), BF16),
        pltpu.VMEM((rows, nh * LANES), F32),
        pltpu.VMEM((n_all, LANES, 2 * LANES), F32),
        pltpu.VMEM((n_all, 1, 2 * LANES), F32),
        pltpu.VMEM((n_all, LANES, 2 * LANES), BF16),
    ]
    return pl.pallas_call(
        functools.partial(_scan_body, seq_len=seq_len, seqs=seqs, pair=pair, has_s0=has_s0,
                          emit_state=emit_state),
        grid=(nseq // seqs, units),
        in_specs=in_specs,
        out_specs=out_specs,
        out_shape=out_shape,
        scratch_shapes=scratch,
        compiler_params=_params(2),
        name=f"scan_{'p' if prompt else 's'}_{'gla' if pair else 'hgrn'}",
    )(*args)


def _outproj_body(*refs, n_lhs, tm):
    lhs = refs[:2 * n_lhs]
    ws = refs[2 * n_lhs:3 * n_lhs]
    x_ref, m_ref, g_ref, b_ref, o_ref = refs[3 * n_lhs:]

    def compute(group):
        for r in range(0, tm, PROJ_SUB):
            rows = slice(r, r + PROJ_SUB)
            y = _dot(lhs[group][rows, :], ws[0][...])
            for j in range(1, n_lhs):
                y = y + _dot(lhs[2 * j + group][rows, :], ws[j][...])
            z = ALPHA * x_ref[rows, :] + m_ref[...] * y
            o_ref[rows, :] = _layer_norm(z, g_ref[...], b_ref[...])

    in_prompt = pl.program_id(0) < TP // tm
    pl.when(in_prompt)(lambda: compute(0))
    pl.when(jnp.logical_not(in_prompt))(lambda: compute(1))


def _outproj(lhs, ws, x, mod, ln_g, ln_b, layer, tm=OUT_TM):
    n_lhs = len(lhs)
    in_specs, args = [], []
    for a_p, a_s in lhs:
        in_specs += _group_specs(True, tm, a_p.shape[1])
        args += [a_p, a_s]
    in_specs += [_resident(w.shape) for w in ws]
    in_specs += [pl.BlockSpec((tm, D), lambda i: (i, 0))] + _mod_specs(layer, (5,), tm) + _ln_specs(layer, 1)
    return pl.pallas_call(
        functools.partial(_outproj_body, n_lhs=n_lhs, tm=tm),
        grid=(T // tm,),
        in_specs=in_specs,
        out_specs=pl.BlockSpec((tm, D), lambda i: (i, 0)),
        out_shape=jax.ShapeDtypeStruct((T, D), F32),
        compiler_params=_params(1),
        name="mixer_outproj",
    )(*args, *ws, x, mod, ln_g, ln_b)


def _rope_partner(x):
    lane = lax.broadcasted_iota(jnp.int32, x.shape, 1)
    first_half = (lane % (2 * ROPE_FREQS)) < ROPE_FREQS
    return jnp.where(first_half, pltpu.roll(x, LANES - ROPE_FREQS, axis=1), pltpu.roll(x, ROPE_FREQS, axis=1))


def _qkv_body(x_ref, shift_ref, scale_ref, w_ref, wkvt_ref, cos_ref, sin_ref,
              q_ref, k_ref, v_ref, kt_ref, vt_ref):
    qscale = C_HEAD_DIM ** -0.5
    in_prompt = pl.program_id(0) < TP // PROJ_TM

    def modulated(rows):
        return (x_ref[rows, :] * (1.0 + scale_ref[...]) + shift_ref[...]).astype(BF16)

    @pl.when(in_prompt)
    def _():
        for sq in range(PROJ_TM // SEQ):
            rows = slice(sq * SEQ, (sq + 1) * SEQ)
            h = modulated(rows)
            q_ref[rows, :] = (_dot(h, w_ref[:, :C_Q]) * qscale).astype(BF16)
            v_ref[rows, :] = _dot(h, w_ref[:, C_Q + C_KV:]).astype(BF16)
            kt_ref[sq] = _dot_nt(wkvt_ref[:C_KV, :], h)
            vt_ref[sq] = _dot_nt(wkvt_ref[C_KV:, :], h)

    @pl.when(jnp.logical_not(in_prompt))
    def _():
        for sq in range(PROJ_TM // SEQ):
            rows = slice(sq * SEQ, (sq + 1) * SEQ)
            h = modulated(rows)
            cos, sin = cos_ref[rows, :], sin_ref[rows, :]
            zq = _dot(h, w_ref[:, :C_Q])
            zk = _dot(h, w_ref[:, C_Q:C_Q + C_KV])
            v_ref[rows, :] = _dot(h, w_ref[:, C_Q + C_KV:]).astype(BF16)

            def rope(z, cos=cos, sin=sin):
                return z * cos + _rope_partner(z) * sin

            for j in range(C_Q // LANES):
                cols = slice(j * LANES, (j + 1) * LANES)
                q_ref[rows, cols] = (rope(zq[:, cols]) * qscale).astype(BF16)
            for j in range(C_KV // LANES):
                cols = slice(j * LANES, (j + 1) * LANES)
                k_ref[rows, cols] = rope(zk[:, cols]).astype(BF16)


def _qkv(x, mod, w, wkvt, cos, sin, layer):
    tm = PROJ_TM
    n_p = TP // tm
    lat = lambda i: (jnp.maximum(i - n_p, 0), 0)
    ctx = lambda i: (jnp.minimum(i, n_p - 1), 0, 0)
    return pl.pallas_call(
        _qkv_body,
        grid=(T // tm,),
        in_specs=[pl.BlockSpec((tm, D), lambda i: (i, 0))] + _mod_specs(layer, (3, 4), tm) + [
            _resident((D, C_Q + 2 * C_KV)),
            _resident((2 * C_KV, D)),
            pl.BlockSpec((tm, LANES), lat),
            pl.BlockSpec((tm, LANES), lat),
        ],
        out_specs=[
            pl.BlockSpec((tm, C_Q), lambda i: (i, 0)),
            pl.BlockSpec((tm, C_KV), lat),
            pl.BlockSpec((tm, C_KV), lambda i: (i, 0)),
            pl.BlockSpec((tm // SEQ, C_KV, SEQ), ctx),
            pl.BlockSpec((tm // SEQ, C_KV, SEQ), ctx),
        ],
        out_shape=[jax.ShapeDtypeStruct((T, C_Q), BF16), jax.ShapeDtypeStruct((TS, C_KV), BF16),
                   jax.ShapeDtypeStruct((T, C_KV), BF16),
                   jax.ShapeDtypeStruct((BATCH, C_KV, SEQ), F32), jax.ShapeDtypeStruct((BATCH, C_KV, SEQ), F32)],
        compiler_params=_params(1),
        name="c_qkv",
    )(x, mod, mod, w, wkvt, cos, sin)


def _rope_tables():
    t = np.arange(DEC_SEQ)
    pos = np.stack([t // GRID_W, t % GRID_W], axis=1).astype(np.float32)
    inv = (ROPE_BASE ** (-np.arange(ROPE_FREQS, dtype=np.float32) / ROPE_FREQS)).astype(np.float32)
    d = np.arange(C_HEAD_DIM)
    axis = d // (2 * ROPE_FREQS)
    ang = pos[:, axis] * inv[d % ROPE_FREQS][None, :]
    sign = np.where((d % (2 * ROPE_FREQS)) < ROPE_FREQS, -1.0, 1.0)[None, :]
    cos_h, sin_h = np.cos(ang), np.sin(ang) * sign
    reps = LANES // C_HEAD_DIM
    cos = np.tile(np.tile(cos_h, (1, reps)), (DEC_BATCH, 1)).astype(np.float32)
    sin = np.tile(np.tile(sin_h, (1, reps)), (DEC_BATCH, 1)).astype(np.float32)
    return jnp.asarray(cos), jnp.asarray(sin)


def _dup_head(blk, half):
    lane = lax.broadcasted_iota(jnp.int32, blk.shape, 1)
    keep = (lane >= HALF).astype(jnp.int32) == half
    return jnp.where(keep, blk, pltpu.roll(blk, HALF, axis=1))


def _pair_values(v, half):
    lo = lax.broadcasted_iota(jnp.int32, v.shape, 1) < HALF
    v2 = _dup_head(v.astype(F32), half)
    top = jnp.concatenate([jnp.where(lo, v2, 0.0), jnp.where(lo, 1.0, 0.0)], axis=1)
    bot = jnp.concatenate([jnp.where(lo, 0.0, v2), jnp.where(lo, 0.0, 1.0)], axis=1)
    return top.astype(BF16), bot.astype(BF16)


def _softmax_weights(scores, sink):
    mx = sink
    for s in scores:
        mx = jnp.maximum(mx, jnp.max(s, axis=-1, keepdims=True))
    return [jnp.exp(s - mx).astype(BF16) for s in scores], jnp.exp(sink - mx)


def _pair_sinks(sink_ref, jb):
    return (sink_ref[:, jb * LANES:jb * LANES + 1], sink_ref[:, jb * LANES + HALF:jb * LANES + HALF + 1])


KV_PER_BLOCK = LANES // C_HEAD_DIM
CTX_TILES = BATCH * C_KV_HEADS // KV_PER_BLOCK
CTX_BLOCKS = KV_PER_BLOCK * C_GROUPS // 2
PIPE_LAG = 2


def _ctx_tile(g, lag):
    t = jnp.clip(g - lag, 0, CTX_TILES - 1)
    n = C_KV_HEADS // KV_PER_BLOCK
    return t // n, t % n


def _ctx_attn_body(q_ref, kt_ref, v_ref, sink_ref, o_ref, s_s, e_s, st_s):
    g = pl.program_id(0)
    lo = lax.broadcasted_iota(jnp.int32, (1, LANES), 1) < HALF
    pairs_per_kv = C_GROUPS // 2

    @pl.when(g == 0)
    def _():
        s_s[...] = jnp.zeros_like(s_s)
        e_s[...] = jnp.zeros_like(e_s)
        st_s[...] = jnp.ones_like(st_s)

    def stages(cur, prev):
        v = v_ref[...]
        for kh in range(KV_PER_BLOCK):
            v_top, v_bot = _pair_values(v, kh)
            for jb in range(kh * pairs_per_kv, (kh + 1) * pairs_per_kv):
                od = _dot(e_s[cur, 2 * jb], v_top) + _dot(e_s[cur, 2 * jb + 1], v_bot)
                o_ref[:, jb * LANES:(jb + 1) * LANES] = (od[:, :LANES] / (od[:, LANES:] + st_s[cur, jb])).astype(BF16)
        for jb in range(CTX_BLOCKS):
            terms = []
            for hh, sink in enumerate(_pair_sinks(sink_ref, jb)):
                (e,), term = _softmax_weights([s_s[prev, 2 * jb + hh]], sink)
                e_s[prev, 2 * jb + hh] = e
                terms.append(term)
            st_s[prev, jb] = jnp.where(lo, terms[0], terms[1])
        for kh in range(KV_PER_BLOCK):
            kt = kt_ref[kh * C_HEAD_DIM:(kh + 1) * C_HEAD_DIM, :]
            zero = jnp.zeros_like(kt)
            kt_a = jnp.concatenate([kt, zero], axis=0).astype(BF16)
            kt_b = jnp.concatenate([zero, kt], axis=0).astype(BF16)
            for jb in range(kh * pairs_per_kv, (kh + 1) * pairs_per_kv):
                q = q_ref[:, jb * LANES:(jb + 1) * LANES]
                s_s[cur, 2 * jb] = _dot(q, kt_a)
                s_s[cur, 2 * jb + 1] = _dot(q, kt_b)

    pl.when(g % 2 == 0)(lambda: stages(0, 1))
    pl.when(g % 2 == 1)(lambda: stages(1, 0))


def _ctx_attn(q, kt, v, sink_cols):
    gw = CTX_BLOCKS * LANES

    def at(lag, fn):
        return lambda g: fn(*_ctx_tile(g, lag))

    return pl.pallas_call(
        _ctx_attn_body,
        grid=(CTX_TILES + PIPE_LAG,),
        in_specs=[
            pl.BlockSpec((SEQ, gw), at(0, lambda b, p: (b, p))),
            pl.BlockSpec((None, LANES, SEQ), at(0, lambda b, p: (b, p, 0))),
            pl.BlockSpec((SEQ, LANES), at(2, lambda b, p: (b, p))),
            pl.BlockSpec((1, gw), at(1, lambda b, p: (0, p))),
        ],
        out_specs=pl.BlockSpec((SEQ, gw), at(2, lambda b, p: (b, p))),
        out_shape=jax.ShapeDtypeStruct((TP, C_Q), BF16),
        scratch_shapes=[
            pltpu.VMEM((2, 2 * CTX_BLOCKS, SEQ, SEQ), F32),
            pltpu.VMEM((2, 2 * CTX_BLOCKS, SEQ, SEQ), BF16),
            pltpu.VMEM((2, CTX_BLOCKS, SEQ, LANES), F32),
        ],
        compiler_params=_params(1),
        name="c_attn_ctx",
    )(q, kt, v, sink_cols)


LAT_TQ = 256
LAT_WIN = LAT_TQ + 2 * WINDOW
LAT_NQ = DEC_SEQ // LAT_TQ
LAT_TILES = DEC_BATCH * C_KV_HEADS * LAT_NQ


def _lat_tile(g, lag):
    t = jnp.clip(g - lag, 0, LAT_TILES - 1)
    bh = t // LAT_NQ
    return bh // C_KV_HEADS, bh % C_KV_HEADS, t % LAT_NQ


def _lat_window(j):
    return pl.multiple_of(jnp.clip(j * LAT_TQ - WINDOW, 0, DEC_SEQ - LAT_WIN), LANES)


def _lat_attn_body(q_ref, k_ref, v_ref, ck_ref, cv_ref, sink_ref, o_ref,
                   ka_s, kb_s, cka_s, ckb_s, va_s, vb_s, cva_s, cvb_s, sl_s, sc_s, el_s, ec_s, st_s):
    g = pl.program_id(0)
    lo = lax.broadcasted_iota(jnp.int32, (1, LANES), 1) < HALF
    n_pairs = C_GROUPS // 2
    b_a, h_a, j_a = _lat_tile(g, 0)
    b_c, h_c, j_c = _lat_tile(g, 2)
    v_slot_a = (b_a * C_KV_HEADS + h_a) % 2
    v_slot_c = (b_c * C_KV_HEADS + h_c) % 2

    @pl.when(g == 0)
    def _():
        sl_s[...] = jnp.zeros_like(sl_s)
        sc_s[...] = jnp.zeros_like(sc_s)
        el_s[...] = jnp.zeros_like(el_s)
        ec_s[...] = jnp.zeros_like(ec_s)
        st_s[...] = jnp.ones_like(st_s)

    @pl.when(jnp.logical_and(j_a == 0, g < LAT_TILES))
    def _():
        half = h_a % 2
        for src, a_s, b_s in ((k_ref, ka_s, kb_s), (ck_ref, cka_s, ckb_s)):
            lo2 = lax.broadcasted_iota(jnp.int32, src.shape, 1) < HALF
            k2 = _dup_head(src[...].astype(F32), half)
            a_s[...] = jnp.where(lo2, k2, 0.0).astype(BF16)
            b_s[...] = jnp.where(lo2, 0.0, k2).astype(BF16)
        va_s[v_slot_a], vb_s[v_slot_a] = _pair_values(v_ref[...], half)
        cva_s[v_slot_a], cvb_s[v_slot_a] = _pair_values(cv_ref[...], half)

    def stages(cur, prev):
        win_c = pl.ds(_lat_window(j_c), LAT_WIN)
        v_loc = (va_s[v_slot_c, win_c, :], vb_s[v_slot_c, win_c, :])
        v_ctx = (cva_s[v_slot_c], cvb_s[v_slot_c])
        for jb in range(n_pairs):
            od = None
            for hh in range(2):
                part = _dot(el_s[cur, 2 * jb + hh], v_loc[hh]) + _dot(ec_s[cur, 2 * jb + hh], v_ctx[hh])
                od = part if od is None else od + part
            o_ref[:, jb * LANES:(jb + 1) * LANES] = (od[:, :LANES] / (od[:, LANES:] + st_s[cur, jb])).astype(BF16)
        for jb in range(n_pairs):
            terms = []
            for hh, sink in enumerate(_pair_sinks(sink_ref, jb)):
                (e_loc, e_ctx), term = _softmax_weights([sl_s[prev, 2 * jb + hh], sc_s[prev, 2 * jb + hh]], sink)
                el_s[prev, 2 * jb + hh] = e_loc
                ec_s[prev, 2 * jb + hh] = e_ctx
                terms.append(term)
            st_s[prev, jb] = jnp.where(lo, terms[0], terms[1])
        start = _lat_window(j_a)
        win = pl.ds(start, LAT_WIN)
        qpos = j_a * LAT_TQ + lax.broadcasted_iota(jnp.int32, (LAT_TQ, LAT_WIN), 0)
        kpos = start + lax.broadcasted_iota(jnp.int32, (LAT_TQ, LAT_WIN), 1)
        band = jnp.abs(qpos - kpos) <= WINDOW
        for jb in range(n_pairs):
            q = q_ref[:, jb * LANES:(jb + 1) * LANES]
            for hh, (kl, kc) in enumerate(((ka_s, cka_s), (kb_s, ckb_s))):
                sl_s[cur, 2 * jb + hh] = jnp.where(band, _dot_nt(q, kl[win, :]), -jnp.inf)
                sc_s[cur, 2 * jb + hh] = _dot_nt(q, kc[...])

    pl.when(g % 2 == 0)(lambda: stages(0, 1))
    pl.when(g % 2 == 1)(lambda: stages(1, 0))


def _lat_attn(q, k, v, ck, cv, sink_cols):
    gw = C_GROUPS * C_HEAD_DIM
    q_off = TP // LAT_TQ
    kv_off = TP // DEC_SEQ

    def at(lag, fn):
        return lambda g: fn(*_lat_tile(g, lag))

    return pl.pallas_call(
        _lat_attn_body,
        grid=(LAT_TILES + PIPE_LAG,),
        in_specs=[
            pl.BlockSpec((LAT_TQ, gw), at(0, lambda b, h, j: (q_off + b * LAT_NQ + j, h))),
            pl.BlockSpec((DEC_SEQ, LANES), at(0, lambda b, h, j: (b, h // 2))),
            pl.BlockSpec((DEC_SEQ, LANES), at(0, lambda b, h, j: (kv_off + b, h // 2))),
            pl.BlockSpec((PAST_LEN, LANES), at(0, lambda b, h, j: (b, h // 2))),
            pl.BlockSpec((PAST_LEN, LANES), at(0, lambda b, h, j: (b, h // 2))),
            pl.BlockSpec((1, gw), at(1, lambda b, h, j: (0, h))),
        ],
        out_specs=pl.BlockSpec((LAT_TQ, gw), at(2, lambda b, h, j: (b * LAT_NQ + j, h))),
        out_shape=jax.ShapeDtypeStruct((TS, C_Q), BF16),
        scratch_shapes=[
            pltpu.VMEM((DEC_SEQ, LANES), BF16), pltpu.VMEM((DEC_SEQ, LANES), BF16),
            pltpu.VMEM((PAST_LEN, LANES), BF16), pltpu.VMEM((PAST_LEN, LANES), BF16),
            pltpu.VMEM((2, DEC_SEQ, 2 * LANES), BF16), pltpu.VMEM((2, DEC_SEQ, 2 * LANES), BF16),
            pltpu.VMEM((2, PAST_LEN, 2 * LANES), BF16), pltpu.VMEM((2, PAST_LEN, 2 * LANES), BF16),
            pltpu.VMEM((2, C_GROUPS, LAT_TQ, LAT_WIN), F32), pltpu.VMEM((2, C_GROUPS, LAT_TQ, PAST_LEN), F32),
            pltpu.VMEM((2, C_GROUPS, LAT_TQ, LAT_WIN), BF16), pltpu.VMEM((2, C_GROUPS, LAT_TQ, PAST_LEN), BF16),
            pltpu.VMEM((2, C_GROUPS // 2, LAT_TQ, LANES), F32),
        ],
        compiler_params=_params(1),
        name="c_attn_latent",
    )(q, k, v, ck, cv, sink_cols)


def kernel(x_prompt, x_sample, state_hgrn, state_gla, cache_k, cache_v, c, c_ctx, w_mod, b_mod, ln_g, ln_b,
           ffn_w1, ffn_w3, ffn_w2, w_in_ab, hgrn_lb, gla_gate_up, gla_gate_b, norm_a, norm_b, w_out_ab,
           w_qkv_c, sink_c, w_out_c):
    cs = jnp.zeros((8, D), F32).at[0].set(c_ctx).at[1:1 + DEC_BATCH].set(c)
    mod = _mod_vectors(cs, w_mod, b_mod).reshape(DEPTH, 8, 1, N_MOD * D)
    ffn_ws = (ffn_w1, ffn_w3, ffn_w2)
    ln_g, ln_b = ln_g.reshape(DEPTH, 3, 1, D), ln_b.reshape(DEPTH, 3, 1, D)

    def ffn(xs, ws, layer, sub, split_out=False, cast_next=None):
        casts = None if cast_next is None else (ffn_ws, cast_next)
        return _ffn_sublayer(xs, mod, *ws, ln_g, ln_b, layer, sub, split_out=split_out, casts=casts)

    x, *ws_01 = ffn([x_prompt.reshape(TP, D), x_sample.reshape(TS, D)], [w[0, 0].astype(BF16) for w in ffn_ws],
                    0, 0, cast_next=(0, 1))
    w_in = w_in_ab[0]
    o_aq, o_ai, o_ff, o_fb, o_ag = 0, A_W, 2 * A_W, 3 * A_W, 4 * A_W
    o_bq = 5 * A_W
    o_bk, o_bv = o_bq + B_QK, o_bq + 2 * B_QK
    o_bg = o_bv + B_V
    o_z = o_bg + B_V
    order = [(o_aq, A_W), (o_ff, A_W), (o_fb, A_W), (o_ag, A_W), (o_bq, B_QK), (o_bk, B_QK), (o_bg, B_V),
             (o_ai, A_W), (o_bv, B_V)]
    wmain = jnp.concatenate([w_in[:, o:o + w] for o, w in order], axis=1).astype(BF16)
    wz = jnp.pad(w_in[:, o_z:o_z + 2 * GATE_RANK], ((0, 0), (0, LANES - 2 * GATE_RANK))).astype(BF16)
    gup = jnp.zeros((LANES, 2 * B_QK), F32)
    gup = gup.at[:GATE_RANK, :B_QK].set(gla_gate_up[0, 0]).at[GATE_RANK:2 * GATE_RANK, B_QK:].set(gla_gate_up[0, 1])
    gb = gla_gate_b[0].reshape(1, 2 * B_QK)
    pf, pb = _inproj(x, mod, wmain, wz, gup.astype(BF16), gb, hgrn_lb, 0, 0)

    s0_a = state_hgrn[:, 0]
    s0_b = state_gla[:, 0].reshape(DEC_BATCH, 2, B_HEADS // 2, LANES, B_DV)
    oa_p, st_a = _scan(pf, pb, norm_a[0], None, prompt=True, pair=False)
    ob_p, st_b = _scan(pf, pb, norm_b[0], None, prompt=True, pair=True)
    (oa_s,) = _scan(pf, pb, norm_a[0], s0_a, prompt=False, pair=False)
    (ob_s,) = _scan(pf, pb, norm_b[0], s0_b, prompt=False, pair=True)
    w_out = w_out_ab[0].astype(BF16)
    x = _outproj([(oa_p, oa_s), (ob_p, ob_s)], [w_out[:A_W], w_out[A_W:]], x, mod, ln_g, ln_b, 0)
    x, *ws_10 = ffn([x], ws_01, 0, 1, cast_next=(1, 0))
    new_hgrn = st_a.reshape(BATCH, 1, 2, A_HEADS, A_DK, A_DV)
    new_gla = st_b.reshape(BATCH, 1, 2, B_HEADS, B_DK, B_DV)

    x, *ws_11 = ffn([x], ws_10, 1, 0, cast_next=(1, 1))
    cos, sin = _rope_tables()
    w_qkv = w_qkv_c[0].astype(BF16)
    q, k, v, kt, vt = _qkv(x, mod, w_qkv, w_qkv[:, C_Q:].T, cos, sin, 1)
    sink_cols = jnp.repeat(sink_c[0], C_HEAD_DIM).reshape(1, C_Q)
    o_p = _ctx_attn(q, kt, v, sink_cols)
    ck = cache_k[:, 0].reshape(DEC_BATCH * PAST_LEN, C_KV)
    cv = cache_v[:, 0].reshape(DEC_BATCH * PAST_LEN, C_KV)
    o_s = _lat_attn(q, k, v, ck, cv, sink_cols)
    x = _outproj([(o_p, o_s)], [w_out_c[0].astype(BF16)], x, mod, ln_g, ln_b, 1)
    y_p, y_s = ffn([x], ws_11, 1, 1, split_out=True)

    def cache_layout(zt):
        return zt.reshape(BATCH, 1, C_KV_HEADS, C_HEAD_DIM, SEQ).transpose(0, 1, 4, 2, 3)

    new_k, new_v = cache_layout(kt), cache_layout(vt)

    return (y_p.reshape(BATCH, SEQ, D), y_s.reshape(DEC_BATCH, DEC_SEQ, D), new_hgrn, new_gla, new_k, new_v)
```

```python
import functools
import math

import jax
import jax.numpy as jnp
import numpy as np
from jax import lax
from jax.experimental import pallas as pl
from jax.experimental.pallas import tpu as pltpu

D = 1024
BATCH, SEQ = 16, 256
DEC_BATCH, DEC_SEQ = 2, 2048
PAST_LEN = 512
GRID_W = 64
D_FF = 2816
N_MOD = 9
A_HEADS, A_DK, A_DV = 4, 128, 128
A_W = A_HEADS * A_DK
B_HEADS, B_DK, B_DV = 4, 64, 128
B_QK = B_HEADS * B_DK
B_V = B_HEADS * B_DV
GATE_RANK = 16
GLA_TAU = 16.0
CHUNK = 128
C_HEADS, C_KV_HEADS, C_HEAD_DIM = 16, 4, 64
C_GROUPS = C_HEADS // C_KV_HEADS
C_Q = C_HEADS * C_HEAD_DIM
C_KV = C_KV_HEADS * C_HEAD_DIM
WINDOW = 128
ROPE_FREQS = C_HEAD_DIM // 4
ROPE_BASE = 10000.0
DEPTH = 2
ALPHA = (2.0 * DEPTH) ** 0.25
LN_EPS = 1e-5
RMS_EPS = 1e-6

TP = BATCH * SEQ
TS = DEC_BATCH * DEC_SEQ
T = TP + TS
N_SEG = 1 + DEC_BATCH

LANES = 128
HALF = LANES // 2
FFN_TM = 1024
FFN_TM_SPLIT = 512
FFN_SUB = 256
PROJ_TM = 2 * SEQ
INPROJ_TM = 512
PROJ_SUB = 256
OUT_TM = 1024
VMEM_LIMIT = 56 * 1024 * 1024

F32 = jnp.float32
BF16 = jnp.bfloat16


def _dot(a, b):
    return jnp.dot(a, b, preferred_element_type=F32)


def _dot_nt(a, b):
    return lax.dot_general(a, b, (((1,), (1,)), ((), ())), preferred_element_type=F32)


def _dot_tn(a, b):
    return lax.dot_general(a, b, (((0,), (0,)), ((), ())), preferred_element_type=F32)


def _silu(x):
    return x * jax.nn.sigmoid(x)


def _layer_norm(z, g, b):
    mu = jnp.mean(z, axis=-1, keepdims=True)
    zc = z - mu
    var = jnp.mean(zc * zc, axis=-1, keepdims=True)
    return zc * lax.rsqrt(var + LN_EPS) * g + b


def _seg_of_tile(i, tm):
    n_p = TP // tm
    n_s = DEC_SEQ // tm
    return jnp.where(i < n_p, 0, 1 + lax.div(jnp.maximum(i - n_p, 0), n_s))


def _params(n_axes):
    return pltpu.CompilerParams(dimension_semantics=("arbitrary",) * n_axes, vmem_limit_bytes=VMEM_LIMIT)


def _resident(shape):
    nd = len(shape)
    return pl.BlockSpec(shape, lambda *_: (0,) * nd, pipeline_mode=pl.Buffered(1))


def _resident_slice(shape, lead):
    block = (None,) * len(lead) + tuple(shape)
    return pl.BlockSpec(block, lambda *_: tuple(lead) + (0,) * len(shape), pipeline_mode=pl.Buffered(1))


def _mod_specs(layer, cols, tm):
    return [pl.BlockSpec((None, None, 1, D), functools.partial(
        lambda i, c: (layer, _seg_of_tile(i, tm), 0, c), c=c)) for c in cols]


def _ln_specs(layer, idx):
    return [_resident_slice((1, D), (layer, idx))] * 2


BF16_SUBLANES = 16


def _cast_plan(ws, lead, n_steps, step_of):
    in_specs, out_specs, out_shapes = [], [], []
    for w in ws:
        rows, cols = w.shape[len(lead):]
        blk = next(b for b in range(BF16_SUBLANES, rows + 1, BF16_SUBLANES)
                   if rows % b == 0 and rows // b <= n_steps)
        last = rows // blk - 1
        in_specs.append(pl.BlockSpec((None,) * len(lead) + (blk, cols), functools.partial(
            lambda *g, last: tuple(lead) + (jnp.minimum(step_of(*g), last), 0), last=last)))
        out_specs.append(pl.BlockSpec((blk, cols), functools.partial(
            lambda *g, last: (jnp.minimum(step_of(*g), last), 0), last=last)))
        out_shapes.append(jax.ShapeDtypeStruct((rows, cols), BF16))
    return in_specs, out_specs, out_shapes


def _hosting_casts(body, n_in, n_out, n_cast):
    def hosted(*refs, **kw):
        ins, refs = refs[:n_in], refs[n_in:]
        cast_in, refs = refs[:n_cast], refs[n_cast:]
        outs, refs = refs[:n_out], refs[n_out:]
        cast_out, scratch = refs[:n_cast], refs[n_cast:]
        for src, dst in zip(cast_in, cast_out):
            dst[...] = src[...].astype(BF16)
        body(*ins, *outs, *scratch, **kw)
    return hosted


def _mod_body(c_ref, w_ref, b_ref, o_ref):
    c = c_ref[...]
    s = _silu(c).astype(BF16)
    o_ref[0] = _dot(s, w_ref[0].astype(BF16)) + b_ref[0]


def _mod_vectors(cs, w_mod, b_mod):
    tn = 1536
    n = N_MOD * D
    return pl.pallas_call(
        _mod_body,
        grid=(DEPTH, n // tn),
        in_specs=[
            pl.BlockSpec((8, D), lambda l, j: (0, 0)),
            pl.BlockSpec((1, D, tn), lambda l, j: (l, 0, j)),
            pl.BlockSpec((1, 1, tn), lambda l, j: (l, 0, j)),
        ],
        out_specs=pl.BlockSpec((1, 8, tn), lambda l, j: (l, 0, j)),
        out_shape=jax.ShapeDtypeStruct((DEPTH, 8, n), F32),
        compiler_params=_params(2),
        name="mod_vectors",
    )(cs, w_mod, b_mod.reshape(DEPTH, 1, n))


def _ffn_body(*refs, n_x, n_o, tm):
    x_refs = refs[:n_x]
    shift_ref, scale_ref, gate_ref, w1_ref, w3_ref, w2_ref, g_ref, b_ref = refs[n_x:n_x + 8]
    o_refs = refs[n_x + 8:]

    def compute(x_ref, o_ref):
        shift, scale, gate = shift_ref[...], scale_ref[...], gate_ref[...]
        for r in range(0, tm, FFN_SUB):
            rows = slice(r, r + FFN_SUB)
            x = x_ref[rows, :]
            h = (x * (1.0 + scale) + shift).astype(BF16)
            a = _dot(h, w1_ref[...])
            b = _dot(h, w3_ref[...])
            g = (_silu(a) * b).astype(BF16)
            y = _dot(g, w2_ref[...])
            z = ALPHA * x + (0.5 * gate) * y
            o_ref[rows, :] = _layer_norm(z, g_ref[...], b_ref[...])

    if n_x == 1 and n_o == 1:
        compute(x_refs[0], o_refs[0])
    else:
        in_prompt = pl.program_id(0) < TP // tm
        pl.when(in_prompt)(lambda: compute(x_refs[0], o_refs[0]))
        pl.when(jnp.logical_not(in_prompt))(lambda: compute(x_refs[-1], o_refs[-1]))


def _group_specs(split, tm, width=D):
    if not split:
        return [pl.BlockSpec((tm, width), lambda i: (i, 0))]
    n_p = TP // tm
    return [pl.BlockSpec((tm, width), lambda i: (jnp.minimum(i, n_p - 1), 0)),
            pl.BlockSpec((tm, width), lambda i: (jnp.maximum(i - n_p, 0), 0))]


def _ffn_sublayer(xs, mod, w1, w3, w2, ln_g, ln_b, layer, sub, split_out=False, casts=None):
    n_x, n_o = len(xs), 2 if split_out else 1
    tm = FFN_TM if n_x == n_o == 1 else FFN_TM_SPLIT
    out_shape = ([jax.ShapeDtypeStruct((TP, D), F32), jax.ShapeDtypeStruct((TS, D), F32)] if split_out
                 else [jax.ShapeDtypeStruct((T, D), F32)])
    mod_lo = 6 * sub
    body = functools.partial(_ffn_body, n_x=n_x, n_o=n_o, tm=tm)
    in_specs = _group_specs(n_x == 2, tm) + _mod_specs(layer, (mod_lo, mod_lo + 1, mod_lo + 2), tm) + [
        _resident((D, D_FF)),
        _resident((D, D_FF)),
        _resident((D_FF, D)),
    ] + _ln_specs(layer, 2 * sub)
    out_specs = _group_specs(split_out, tm)
    args = [*xs, mod, mod, mod, w1, w3, w2, ln_g, ln_b]
    if casts is not None:
        ws, lead = casts
        c_in, c_out, c_shapes = _cast_plan(ws, lead, T // tm, lambda i: i)
        body = _hosting_casts(body, len(in_specs), len(out_specs), len(ws))
        in_specs, out_specs, out_shape = in_specs + c_in, out_specs + c_out, out_shape + c_shapes
        args = args + list(ws)
    return pl.pallas_call(
        body,
        grid=(T // tm,),
        in_specs=in_specs,
        out_specs=out_specs,
        out_shape=out_shape,
        compiler_params=_params(1),
        name="ffn_sublayer",
    )(*args)


PF_AQ, PF_FF, PF_FB, PF_AG = 0, 512, 1024, 1536
PF_BQ, PF_BK, PF_BG, PF_LAF, PF_LAB = 2048, 2304, 2560, 3072, 3328
PF_W = 3584
PB_AV, PB_BV = 0, 512
PB_W = 1024
WM_AQ, WM_FF, WM_FB, WM_AG, WM_BQ, WM_BK, WM_BG, WM_AI, WM_BV = 0, 512, 1024, 1536, 2048, 2304, 2560, 3072, 3584
WM_W = 4096


def _log_sigmoid(x):
    return jnp.minimum(x, 0.0) - jnp.log(1.0 + jnp.exp(-jnp.abs(x)))


def _inproj_body(x_ref, shift_ref, scale_ref, w_ref, wz_ref, gu_ref, gb_ref, lb_ref, pf_ref, pb_ref, *, layer_e):
    def lower_bound(d):
        l = lb_ref[d]
        e = jnp.exp(l - jnp.max(l, axis=0, keepdims=True))
        sm = e / jnp.sum(e, axis=0, keepdims=True)
        return jnp.sum(sm[:layer_e + 1], axis=0, keepdims=True)

    lbs = [lower_bound(0), lower_bound(1)]
    for r in range(0, x_ref.shape[0], PROJ_SUB):
        rows = slice(r, r + PROJ_SUB)
        h = (x_ref[rows, :] * (1.0 + scale_ref[...]) + shift_ref[...]).astype(BF16)

        def proj(off, width):
            return _dot(h, w_ref[:, off:off + width])

        pf_ref[rows, PF_AQ:PF_AQ + A_W] = proj(WM_AQ, A_W)
        for lb, wm, pf in ((lbs[0], WM_FF, PF_FF), (lbs[1], WM_FB, PF_FB)):
            pf_ref[rows, pf:pf + A_W] = lb + (1.0 - lb) * jax.nn.sigmoid(proj(wm, A_W))
        pf_ref[rows, PF_AG:PF_AG + A_W] = _silu(proj(WM_AG, A_W))
        pf_ref[rows, PF_BQ:PF_BQ + B_QK] = proj(WM_BQ, B_QK) * (B_DK ** -0.5)
        pf_ref[rows, PF_BK:PF_BK + B_QK] = proj(WM_BK, B_QK)
        pf_ref[rows, PF_BG:PF_BG + B_V] = _silu(proj(WM_BG, B_V))
        pb_ref[rows, PB_AV:PB_AV + A_W] = _silu(proj(WM_AI, A_W)).astype(BF16)
        pb_ref[rows, PB_BV:PB_BV + B_V] = proj(WM_BV, B_V).astype(BF16)
        z = _dot(h, wz_ref[...]).astype(BF16)
        pre = _dot(z, gu_ref[...]) + gb_ref[...]
        pf_ref[rows, PF_LAF:PF_LAF + 2 * B_QK] = _log_sigmoid(pre) * (1.0 / GLA_TAU)


def _inproj(x, mod, wmain, wz, gup, gb, hgrn_lb, layer, layer_e, tm=INPROJ_TM):
    n_l = hgrn_lb.shape[1]
    return pl.pallas_call(
        functools.partial(_inproj_body, layer_e=layer_e),
        grid=(T // tm,),
        in_specs=[pl.BlockSpec((tm, D), lambda i: (i, 0))] + _mod_specs(layer, (3, 4), tm) + [
            _resident((D, WM_W)),
            _resident((D, LANES)),
            _resident((LANES, 2 * B_QK)),
            _resident((1, 2 * B_QK)),
            _resident((2, n_l, A_W)),
        ],
        out_specs=[
            pl.BlockSpec((tm, PF_W), lambda i: (i, 0)),
            pl.BlockSpec((tm, PB_W), lambda i: (i, 0)),
        ],
        out_shape=[jax.ShapeDtypeStruct((T, PF_W), F32), jax.ShapeDtypeStruct((T, PB_W), BF16)],
        compiler_params=_params(1),
        name="ab_inproj",
    )(x, mod, mod, wmain, wz, gup, gb, hgrn_lb)


SCAN_PROMPT_SEQS = 4
SCAN_UNROLL = 8


def _prefix_rows(x):
    row = lax.broadcasted_iota(jnp.int32, x.shape, 0)
    s = 1
    while s < x.shape[0]:
        x = x + jnp.where(row >= s, pltpu.roll(x, s, axis=0), 0.0)
        s *= 2
    return x


def _scan_body(*refs, seq_len, seqs, pair, has_s0, emit_state):
    n = seq_len // CHUNK
    n_all = seqs * n
    nh = 2 if pair else 1
    it = iter(refs)
    q_ref = next(it)
    if pair:
        k_ref, laf_ref, lab_ref = next(it), next(it), next(it)
    else:
        ff_ref, fb_ref = next(it), next(it)
    g_ref, v_ref, nw_ref = next(it), next(it), next(it)
    s0_ref = next(it) if has_s0 else None
    o_ref = next(it)
    st_ref = next(it) if emit_state else None
    qd_s, oi_s, kv_s, dec_s, sb_s = it

    row = lax.broadcasted_iota(jnp.int32, (CHUNK, CHUNK), 0)
    col = lax.broadcasted_iota(jnp.int32, (CHUNK, CHUNK), 1)
    tril = row >= col
    triu = row <= col
    lane = lax.broadcasted_iota(jnp.int32, (1, LANES), 1)
    lane2 = lax.broadcasted_iota(jnp.int32, (1, 2 * LANES), 1)
    if pair:
        masks = [lane < HALF, lane >= HALF]
        masks2 = [(lane2 % LANES) < HALF, (lane2 % LANES) >= HALF]
    else:
        masks, masks2 = [None], [None]

    def pick(mask, x):
        return x if mask is None else jnp.where(mask, x, jnp.zeros_like(x))

    def rows_of(c):
        return pl.ds(pl.multiple_of(c * CHUNK, CHUNK), CHUNK)

    def loop(body):
        if n_all <= SCAN_UNROLL:
            for c in range(n_all):
                body(c)
        else:
            def fbody(i, carry):
                for u in range(SCAN_UNROLL):
                    body(i * SCAN_UNROLL + u)
                return carry
            lax.fori_loop(0, n_all // SCAN_UNROLL, fbody, 0)

    def phase1(c):
        rows = rows_of(c)
        q = q_ref[rows, :]
        if pair:
            k_f = k_b = k_ref[rows, :]
            la_f, la_b = laf_ref[rows, :], lab_ref[rows, :]
        else:
            f_f, f_b = ff_ref[rows, :], fb_ref[rows, :]
            k_f, k_b = 1.0 - f_f, 1.0 - f_b
            la_f, la_b = jnp.log(f_f), jnp.log(f_b)
        cs = _prefix_rows(jnp.concatenate([la_f, la_b], axis=1))
        cf, cbi = cs[:, :LANES], cs[:, LANES:]
        tot_f, tot_b = cf[CHUNK - 1:CHUNK, :], cbi[CHUNK - 1:CHUNK, :]
        rb = tot_b - cbi + la_b
        ref_f, ref_b = cf[CHUNK // 2 - 1:CHUNK // 2, :], rb[CHUNK // 2:CHUNK // 2 + 1, :]
        qtf = q * jnp.exp(cf - ref_f)
        qtb = q * jnp.exp(rb - ref_b)
        ktf = k_f * jnp.exp(ref_f - cf)
        ktb = k_b * jnp.exp(ref_b - rb)
        qd = jnp.concatenate([qtf * jnp.exp(ref_f), qtb * jnp.exp(ref_b)], axis=1).astype(BF16)
        ku = jnp.concatenate([ktf * jnp.exp(tot_f - ref_f), ktb * jnp.exp(tot_b - ref_b)], axis=1).astype(BF16)
        qd_s[rows, :] = qd
        qt = jnp.concatenate([qtf, qtb], axis=0).astype(BF16)
        kt = jnp.concatenate([ktf, ktb], axis=0).astype(BF16)
        kv = None
        for hh in range(nh):
            v = v_ref[rows, hh * LANES:(hh + 1) * LANES]
            sc = _dot_nt(pick(masks[hh], qt), kt)
            att = jnp.where(tril, sc[:CHUNK, :CHUNK], 0.0) + jnp.where(triu, sc[CHUNK:, CHUNK:], 0.0)
            oi_s[rows, hh * LANES:(hh + 1) * LANES] = _dot(att.astype(BF16), v)
            kv_h = _dot_tn(v, ku)
            kv = kv_h if kv is None else jnp.where(masks2[0], kv, kv_h)
        kv_s[c] = kv
        dec_s[c] = jnp.exp(jnp.concatenate([tot_f, tot_b], axis=1))

    loop(phase1)

    def recurrence(sq, d, reverse):
        cols = slice(d * LANES, (d + 1) * LANES)
        c0 = sq * n
        st0 = s0_ref[sq, d, 0].T if has_s0 else jnp.zeros((LANES, LANES), F32)

        def step(c, st):
            sb_s[c, :, cols] = st.astype(BF16)
            return st * dec_s[c, :, cols] + kv_s[c, :, cols]

        if n <= 8:
            st = st0
            for c in (range(n - 1, -1, -1) if reverse else range(n)):
                st = step(c0 + c, st)
        else:
            st = lax.fori_loop(0, n, lambda i, st: step(c0 + (n - 1 - i if reverse else i), st), st0)
        if emit_state:
            st_ref[sq, d, 0] = st.T

    for sq in range(seqs):
        recurrence(sq, 0, False)
        recurrence(sq, 1, True)

    nw = nw_ref[...]

    def phase2(c):
        rows = rows_of(c)
        qcat = qd_s[rows, :]
        scat = sb_s[c]
        for hh in range(nh):
            cols = slice(hh * LANES, (hh + 1) * LANES)
            o = oi_s[rows, cols] + _dot_nt(pick(masks2[hh], qcat), scat)
            o = o * lax.rsqrt(jnp.mean(o * o, axis=-1, keepdims=True) + RMS_EPS) * nw
            o_ref[rows, cols] = (o * g_ref[rows, cols]).astype(BF16)

    loop(phase2)


def _scan(pf, pb, norm_w, s0, *, prompt, pair):
    seq_len = SEQ if prompt else DEC_SEQ
    nseq = BATCH if prompt else DEC_BATCH
    seqs = SCAN_PROMPT_SEQS if prompt else 1
    rows = seqs * seq_len
    row_off = 0 if prompt else TP // rows
    units = B_HEADS // 2 if pair else A_HEADS
    nh = 2 if pair else 1
    n_all = rows // CHUNK
    has_s0 = s0 is not None
    emit_state = prompt

    def colspec(off, width=LANES):
        base = off // width
        return pl.BlockSpec((rows, width), lambda s, u: (s + row_off, base + u))

    if pair:
        in_specs = [colspec(PF_BQ), colspec(PF_BK), colspec(PF_LAF), colspec(PF_LAB),
                    colspec(PF_BG, 2 * LANES), colspec(PB_BV, 2 * LANES)]
        args = [pf, pf, pf, pf, pf, pb]
    else:
        in_specs = [colspec(PF_AQ), colspec(PF_FF), colspec(PF_FB), colspec(PF_AG), colspec(PB_AV)]
        args = [pf, pf, pf, pf, pb]
    in_specs.append(pl.BlockSpec((1, LANES), lambda s, u: (0, 0)))
    args.append(norm_w.reshape(1, LANES))
    state_spec = pl.BlockSpec((seqs, 2, 1, LANES, LANES), lambda s, u: (s, 0, u, 0, 0))
    if has_s0:
        in_specs.append(state_spec)
        args.append(s0)
    out_specs = [pl.BlockSpec((rows, nh * LANES), lambda s, u: (s, u))]
    out_shape = [jax.ShapeDtypeStruct((nseq * seq_len, units * nh * LANES), BF16)]
    if emit_state:
        out_specs.append(state_spec)
        out_shape.append(jax.ShapeDtypeStruct((nseq, 2, units, LANES, LANES), F32))
    scratch = [
        pltpu.VMEM((rows, 2 * LANES), BF16),
        pltpu.VMEM((rows, nh * LANES), F32),
        pltpu.VMEM((n_all, LANES, 2 * LANES), F32),
        pltpu.VMEM((n_all, 1, 2 * LANES), F32),
        pltpu.VMEM((n_all, LANES, 2 * LANES), BF16),
    ]
    return pl.pallas_call(
        functools.partial(_scan_body, seq_len=seq_len, seqs=seqs, pair=pair, has_s0=has_s0,
                          emit_state=emit_state),
        grid=(nseq // seqs, units),
        in_specs=in_specs,
        out_specs=out_specs,
        out_shape=out_shape,
        scratch_shapes=scratch,
        compiler_params=_params(2),
        name=f"scan_{'p' if prompt else 's'}_{'gla' if pair else 'hgrn'}",
    )(*args)


def _outproj_body(*refs, n_lhs, tm):
    lhs = refs[:2 * n_lhs]
    ws = refs[2 * n_lhs:3 * n_lhs]
    x_ref, m_ref, g_ref, b_ref, o_ref = refs[3 * n_lhs:]

    def compute(group):
        for r in range(0, tm, PROJ_SUB):
            rows = slice(r, r + PROJ_SUB)
            y = _dot(lhs[group][rows, :], ws[0][...])
            for j in range(1, n_lhs):
                y = y + _dot(lhs[2 * j + group][rows, :], ws[j][...])
            z = ALPHA * x_ref[rows, :] + m_ref[...] * y
            o_ref[rows, :] = _layer_norm(z, g_ref[...], b_ref[...])

    in_prompt = pl.program_id(0) < TP // tm
    pl.when(in_prompt)(lambda: compute(0))
    pl.when(jnp.logical_not(in_prompt))(lambda: compute(1))


def _outproj(lhs, ws, x, mod, ln_g, ln_b, layer, tm=OUT_TM):
    n_lhs = len(lhs)
    in_specs, args = [], []
    for a_p, a_s in lhs:
        in_specs += _group_specs(True, tm, a_p.shape[1])
        args += [a_p, a_s]
    in_specs += [_resident(w.shape) for w in ws]
    in_specs += [pl.BlockSpec((tm, D), lambda i: (i, 0))] + _mod_specs(layer, (5,), tm) + _ln_specs(layer, 1)
    return pl.pallas_call(
        functools.partial(_outproj_body, n_lhs=n_lhs, tm=tm),
        grid=(T // tm,),
        in_specs=in_specs,
        out_specs=pl.BlockSpec((tm, D), lambda i: (i, 0)),
        out_shape=jax.ShapeDtypeStruct((T, D), F32),
        compiler_params=_params(1),
        name="mixer_outproj",
    )(*args, *ws, x, mod, ln_g, ln_b)


def _rope_partner(x):
    lane = lax.broadcasted_iota(jnp.int32, x.shape, 1)
    first_half = (lane % (2 * ROPE_FREQS)) < ROPE_FREQS
    return jnp.where(first_half, pltpu.roll(x, LANES - ROPE_FREQS, axis=1), pltpu.roll(x, ROPE_FREQS, axis=1))


def _qkv_body(x_ref, shift_ref, scale_ref, w_ref, wkvt_ref, cos_ref, sin_ref,
              q_ref, k_ref, v_ref, kt_ref, vt_ref):
    qscale = C_HEAD_DIM ** -0.5
    in_prompt = pl.program_id(0) < TP // PROJ_TM

    def modulated(rows):
        return (x_ref[rows, :] * (1.0 + scale_ref[...]) + shift_ref[...]).astype(BF16)

    @pl.when(in_prompt)
    def _():
        for sq in range(PROJ_TM // SEQ):
            rows = slice(sq * SEQ, (sq + 1) * SEQ)
            h = modulated(rows)
            q_ref[rows, :] = (_dot(h, w_ref[:, :C_Q]) * qscale).astype(BF16)
            v_ref[rows, :] = _dot(h, w_ref[:, C_Q + C_KV:]).astype(BF16)
            kt_ref[sq] = _dot_nt(wkvt_ref[:C_KV, :], h)
            vt_ref[sq] = _dot_nt(wkvt_ref[C_KV:, :], h)

    @pl.when(jnp.logical_not(in_prompt))
    def _():
        for sq in range(PROJ_TM // SEQ):
            rows = slice(sq * SEQ, (sq + 1) * SEQ)
            h = modulated(rows)
            cos, sin = cos_ref[rows, :], sin_ref[rows, :]
            zq = _dot(h, w_ref[:, :C_Q])
            zk = _dot(h, w_ref[:, C_Q:C_Q + C_KV])
            v_ref[rows, :] = _dot(h, w_ref[:, C_Q + C_KV:]).astype(BF16)

            def rope(z, cos=cos, sin=sin):
                return z * cos + _rope_partner(z) * sin

            for j in range(C_Q // LANES):
                cols = slice(j * LANES, (j + 1) * LANES)
                q_ref[rows, cols] = (rope(zq[:, cols]) * qscale).astype(BF16)
            for j in range(C_KV // LANES):
                cols = slice(j * LANES, (j + 1) * LANES)
                k_ref[rows, cols] = rope(zk[:, cols]).astype(BF16)


def _qkv(x, mod, w, wkvt, cos, sin, layer):
    tm = PROJ_TM
    n_p = TP // tm
    lat = lambda i: (jnp.maximum(i - n_p, 0), 0)
    ctx = lambda i: (jnp.minimum(i, n_p - 1), 0, 0)
    return pl.pallas_call(
        _qkv_body,
        grid=(T // tm,),
        in_specs=[pl.BlockSpec((tm, D), lambda i: (i, 0))] + _mod_specs(layer, (3, 4), tm) + [
            _resident((D, C_Q + 2 * C_KV)),
            _resident((2 * C_KV, D)),
            pl.BlockSpec((tm, LANES), lat),
            pl.BlockSpec((tm, LANES), lat),
        ],
        out_specs=[
            pl.BlockSpec((tm, C_Q), lambda i: (i, 0)),
            pl.BlockSpec((tm, C_KV), lat),
            pl.BlockSpec((tm, C_KV), lambda i: (i, 0)),
            pl.BlockSpec((tm // SEQ, C_KV, SEQ), ctx),
            pl.BlockSpec((tm // SEQ, C_KV, SEQ), ctx),
        ],
        out_shape=[jax.ShapeDtypeStruct((T, C_Q), BF16), jax.ShapeDtypeStruct((TS, C_KV), BF16),
                   jax.ShapeDtypeStruct((T, C_KV), BF16),
                   jax.ShapeDtypeStruct((BATCH, C_KV, SEQ), F32), jax.ShapeDtypeStruct((BATCH, C_KV, SEQ), F32)],
        compiler_params=_params(1),
        name="c_qkv",
    )(x, mod, mod, w, wkvt, cos, sin)


def _rope_tables():
    t = np.arange(DEC_SEQ)
    pos = np.stack([t // GRID_W, t % GRID_W], axis=1).astype(np.float32)
    inv = (ROPE_BASE ** (-np.arange(ROPE_FREQS, dtype=np.float32) / ROPE_FREQS)).astype(np.float32)
    d = np.arange(C_HEAD_DIM)
    axis = d // (2 * ROPE_FREQS)
    ang = pos[:, axis] * inv[d % ROPE_FREQS][None, :]
    sign = np.where((d % (2 * ROPE_FREQS)) < ROPE_FREQS, -1.0, 1.0)[None, :]
    cos_h, sin_h = np.cos(ang), np.sin(ang) * sign
    reps = LANES // C_HEAD_DIM
    cos = np.tile(np.tile(cos_h, (1, reps)), (DEC_BATCH, 1)).astype(np.float32)
    sin = np.tile(np.tile(sin_h, (1, reps)), (DEC_BATCH, 1)).astype(np.float32)
    return jnp.asarray(cos), jnp.asarray(sin)


def _dup_head(blk, half):
    lane = lax.broadcasted_iota(jnp.int32, blk.shape, 1)
    keep = (lane >= HALF).astype(jnp.int32) == half
    return jnp.where(keep, blk, pltpu.roll(blk, HALF, axis=1))


def _pair_values(v, half):
    lo = lax.broadcasted_iota(jnp.int32, v.shape, 1) < HALF
    v2 = _dup_head(v.astype(F32), half)
    top = jnp.concatenate([jnp.where(lo, v2, 0.0), jnp.where(lo, 1.0, 0.0)], axis=1)
    bot = jnp.concatenate([jnp.where(lo, 0.0, v2), jnp.where(lo, 0.0, 1.0)], axis=1)
    return top.astype(BF16), bot.astype(BF16)


def _softmax_weights(scores, sink):
    mx = sink
    for s in scores:
        mx = jnp.maximum(mx, jnp.max(s, axis=-1, keepdims=True))
    return [jnp.exp(s - mx).astype(BF16) for s in scores], jnp.exp(sink - mx)


def _pair_sinks(sink_ref, jb):
    return (sink_ref[:, jb * LANES:jb * LANES + 1], sink_ref[:, jb * LANES + HALF:jb * LANES + HALF + 1])


KV_PER_BLOCK = LANES // C_HEAD_DIM
CTX_TILES = BATCH * C_KV_HEADS // KV_PER_BLOCK
CTX_BLOCKS = KV_PER_BLOCK * C_GROUPS // 2
PIPE_LAG = 2


def _ctx_tile(g, lag):
    t = jnp.clip(g - lag, 0, CTX_TILES - 1)
    n = C_KV_HEADS // KV_PER_BLOCK
    return t // n, t % n


def _ctx_attn_body(q_ref, kt_ref, v_ref, sink_ref, o_ref, s_s, e_s, st_s):
    g = pl.program_id(0)
    lo = lax.broadcasted_iota(jnp.int32, (1, LANES), 1) < HALF
    pairs_per_kv = C_GROUPS // 2

    @pl.when(g == 0)
    def _():
        s_s[...] = jnp.zeros_like(s_s)
        e_s[...] = jnp.zeros_like(e_s)
        st_s[...] = jnp.ones_like(st_s)

    def stages(cur, prev):
        v = v_ref[...]
        for kh in range(KV_PER_BLOCK):
            v_top, v_bot = _pair_values(v, kh)
            for jb in range(kh * pairs_per_kv, (kh + 1) * pairs_per_kv):
                od = _dot(e_s[cur, 2 * jb], v_top) + _dot(e_s[cur, 2 * jb + 1], v_bot)
                o_ref[:, jb * LANES:(jb + 1) * LANES] = (od[:, :LANES] / (od[:, LANES:] + st_s[cur, jb])).astype(BF16)
        for jb in range(CTX_BLOCKS):
            terms = []
            for hh, sink in enumerate(_pair_sinks(sink_ref, jb)):
                (e,), term = _softmax_weights([s_s[prev, 2 * jb + hh]], sink)
                e_s[prev, 2 * jb + hh] = e
                terms.append(term)
            st_s[prev, jb] = jnp.where(lo, terms[0], terms[1])
        for kh in range(KV_PER_BLOCK):
            kt = kt_ref[kh * C_HEAD_DIM:(kh + 1) * C_HEAD_DIM, :]
            zero = jnp.zeros_like(kt)
            kt_a = jnp.concatenate([kt, zero], axis=0).astype(BF16)
            kt_b = jnp.concatenate([zero, kt], axis=0).astype(BF16)
            for jb in range(kh * pairs_per_kv, (kh + 1) * pairs_per_kv):
                q = q_ref[:, jb * LANES:(jb + 1) * LANES]
                s_s[cur, 2 * jb] = _dot(q, kt_a)
                s_s[cur, 2 * jb + 1] = _dot(q, kt_b)

    pl.when(g % 2 == 0)(lambda: stages(0, 1))
    pl.when(g % 2 == 1)(lambda: stages(1, 0))


def _ctx_attn(q, kt, v, sink_cols):
    gw = CTX_BLOCKS * LANES

    def at(lag, fn):
        return lambda g: fn(*_ctx_tile(g, lag))

    return pl.pallas_call(
        _ctx_attn_body,
        grid=(CTX_TILES + PIPE_LAG,),
        in_specs=[
            pl.BlockSpec((SEQ, gw), at(0, lambda b, p: (b, p))),
            pl.BlockSpec((None, LANES, SEQ), at(0, lambda b, p: (b, p, 0))),
            pl.BlockSpec((SEQ, LANES), at(2, lambda b, p: (b, p))),
            pl.BlockSpec((1, gw), at(1, lambda b, p: (0, p))),
        ],
        out_specs=pl.BlockSpec((SEQ, gw), at(2, lambda b, p: (b, p))),
        out_shape=jax.ShapeDtypeStruct((TP, C_Q), BF16),
        scratch_shapes=[
            pltpu.VMEM((2, 2 * CTX_BLOCKS, SEQ, SEQ), F32),
            pltpu.VMEM((2, 2 * CTX_BLOCKS, SEQ, SEQ), BF16),
            pltpu.VMEM((2, CTX_BLOCKS, SEQ, LANES), F32),
        ],
        compiler_params=_params(1),
        name="c_attn_ctx",
    )(q, kt, v, sink_cols)


LAT_TQ = 128
LAT_WIN = LAT_TQ + 2 * WINDOW
LAT_NQ = DEC_SEQ // LAT_TQ
LAT_KVB = C_KV_HEADS // KV_PER_BLOCK
LAT_TILES = DEC_BATCH * LAT_KVB * LAT_NQ


def _lat_tile(g, lag):
    t = jnp.clip(g - lag, 0, LAT_TILES - 1)
    bp = t // LAT_NQ
    return bp // LAT_KVB, bp % LAT_KVB, t % LAT_NQ


def _lat_window(j):
    return pl.multiple_of(jnp.clip(j * LAT_TQ - WINDOW, 0, DEC_SEQ - LAT_WIN), LANES)


def _lat_attn_body(q_ref, k_ref, v_ref, ck_ref, cv_ref, sink_ref, o_ref,
                   ka_s, kb_s, cka_s, ckb_s, va_s, vb_s, cva_s, cvb_s, sl_s, sc_s, el_s, ec_s, st_s):
    g = pl.program_id(0)
    lo = lax.broadcasted_iota(jnp.int32, (1, LANES), 1) < HALF
    pairs_per_kv = C_GROUPS // 2
    b_a, p_a, j_a = _lat_tile(g, 0)
    b_c, p_c, j_c = _lat_tile(g, 2)
    v_slot_a = (b_a * LAT_KVB + p_a) % 2
    v_slot_c = (b_c * LAT_KVB + p_c) % 2

    @pl.when(g == 0)
    def _():
        sl_s[...] = jnp.zeros_like(sl_s)
        sc_s[...] = jnp.zeros_like(sc_s)
        el_s[...] = jnp.zeros_like(el_s)
        ec_s[...] = jnp.zeros_like(ec_s)
        st_s[...] = jnp.ones_like(st_s)

    @pl.when(jnp.logical_and(j_a == 0, g < LAT_TILES))
    def _():
        for kh in range(KV_PER_BLOCK):
            for src, a_s, b_s in ((k_ref, ka_s, kb_s), (ck_ref, cka_s, ckb_s)):
                lo2 = lax.broadcasted_iota(jnp.int32, src.shape, 1) < HALF
                k2 = _dup_head(src[...].astype(F32), kh)
                a_s[kh] = jnp.where(lo2, k2, 0.0).astype(BF16)
                b_s[kh] = jnp.where(lo2, 0.0, k2).astype(BF16)
            va_s[v_slot_a, kh], vb_s[v_slot_a, kh] = _pair_values(v_ref[...], kh)
            cva_s[v_slot_a, kh], cvb_s[v_slot_a, kh] = _pair_values(cv_ref[...], kh)

    def stages(cur, prev):
        win_c = pl.ds(_lat_window(j_c), LAT_WIN)
        for kh in range(KV_PER_BLOCK):
            v_loc = (va_s[v_slot_c, kh, win_c, :], vb_s[v_slot_c, kh, win_c, :])
            v_ctx = (cva_s[v_slot_c, kh], cvb_s[v_slot_c, kh])
            for jb in range(kh * pairs_per_kv, (kh + 1) * pairs_per_kv):
                od = None
                for hh in range(2):
                    part = _dot(el_s[cur, 2 * jb + hh], v_loc[hh]) + _dot(ec_s[cur, 2 * jb + hh], v_ctx[hh])
                    od = part if od is None else od + part
                o_ref[:, jb * LANES:(jb + 1) * LANES] = (od[:, :LANES] / (od[:, LANES:] + st_s[cur, jb])).astype(BF16)
        for jb in range(CTX_BLOCKS):
            terms = []
            for hh, sink in enumerate(_pair_sinks(sink_ref, jb)):
                (e_loc, e_ctx), term = _softmax_weights([sl_s[prev, 2 * jb + hh], sc_s[prev, 2 * jb + hh]], sink)
                el_s[prev, 2 * jb + hh] = e_loc
                ec_s[prev, 2 * jb + hh] = e_ctx
                terms.append(term)
            st_s[prev, jb] = jnp.where(lo, terms[0], terms[1])
        start = _lat_window(j_a)
        win = pl.ds(start, LAT_WIN)
        qpos = j_a * LAT_TQ + lax.broadcasted_iota(jnp.int32, (LAT_TQ, LAT_WIN), 0)
        kpos = start + lax.broadcasted_iota(jnp.int32, (LAT_TQ, LAT_WIN), 1)
        band = jnp.abs(qpos - kpos) <= WINDOW
        for kh in range(KV_PER_BLOCK):
            for jb in range(kh * pairs_per_kv, (kh + 1) * pairs_per_kv):
                q = q_ref[:, jb * LANES:(jb + 1) * LANES]
                for hh, (kl, kc) in enumerate(((ka_s, cka_s), (kb_s, ckb_s))):
                    sl_s[cur, 2 * jb + hh] = jnp.where(band, _dot_nt(q, kl[kh, win, :]), -jnp.inf)
                    sc_s[cur, 2 * jb + hh] = _dot_nt(q, kc[kh])

    pl.when(g % 2 == 0)(lambda: stages(0, 1))
    pl.when(g % 2 == 1)(lambda: stages(1, 0))


def _lat_attn(q, k, v, ck, cv, sink_cols):
    gw = CTX_BLOCKS * LANES
    q_off = TP // LAT_TQ
    kv_off = TP // DEC_SEQ
    kvb, n_heads = KV_PER_BLOCK, 2 * CTX_BLOCKS

    def at(lag, fn):
        return lambda g: fn(*_lat_tile(g, lag))

    return pl.pallas_call(
        _lat_attn_body,
        grid=(LAT_TILES + PIPE_LAG,),
        in_specs=[
            pl.BlockSpec((LAT_TQ, gw), at(0, lambda b, p, j: (q_off + b * LAT_NQ + j, p))),
            pl.BlockSpec((DEC_SEQ, LANES), at(0, lambda b, p, j: (b, p))),
            pl.BlockSpec((DEC_SEQ, LANES), at(0, lambda b, p, j: (kv_off + b, p))),
            pl.BlockSpec((PAST_LEN, LANES), at(0, lambda b, p, j: (b, p))),
            pl.BlockSpec((PAST_LEN, LANES), at(0, lambda b, p, j: (b, p))),
            pl.BlockSpec((1, gw), at(1, lambda b, p, j: (0, p))),
        ],
        out_specs=pl.BlockSpec((LAT_TQ, gw), at(2, lambda b, p, j: (b * LAT_NQ + j, p))),
        out_shape=jax.ShapeDtypeStruct((TS, C_Q), BF16),
        scratch_shapes=[
            pltpu.VMEM((kvb, DEC_SEQ, LANES), BF16), pltpu.VMEM((kvb, DEC_SEQ, LANES), BF16),
            pltpu.VMEM((kvb, PAST_LEN, LANES), BF16), pltpu.VMEM((kvb, PAST_LEN, LANES), BF16),
            pltpu.VMEM((2, kvb, DEC_SEQ, 2 * LANES), BF16), pltpu.VMEM((2, kvb, DEC_SEQ, 2 * LANES), BF16),
            pltpu.VMEM((2, kvb, PAST_LEN, 2 * LANES), BF16), pltpu.VMEM((2, kvb, PAST_LEN, 2 * LANES), BF16),
            pltpu.VMEM((2, n_heads, LAT_TQ, LAT_WIN), F32), pltpu.VMEM((2, n_heads, LAT_TQ, PAST_LEN), F32),
            pltpu.VMEM((2, n_heads, LAT_TQ, LAT_WIN), BF16), pltpu.VMEM((2, n_heads, LAT_TQ, PAST_LEN), BF16),
            pltpu.VMEM((2, CTX_BLOCKS, LAT_TQ, LANES), F32),
        ],
        compiler_params=_params(1),
        name="c_attn_latent",
    )(q, k, v, ck, cv, sink_cols)


def kernel(x_prompt, x_sample, state_hgrn, state_gla, cache_k, cache_v, c, c_ctx, w_mod, b_mod, ln_g, ln_b,
           ffn_w1, ffn_w3, ffn_w2, w_in_ab, hgrn_lb, gla_gate_up, gla_gate_b, norm_a, norm_b, w_out_ab,
           w_qkv_c, sink_c, w_out_c):
    cs = jnp.zeros((8, D), F32).at[0].set(c_ctx).at[1:1 + DEC_BATCH].set(c)
    mod = _mod_vectors(cs, w_mod, b_mod).reshape(DEPTH, 8, 1, N_MOD * D)
    ffn_ws = (ffn_w1, ffn_w3, ffn_w2)
    ln_g, ln_b = ln_g.reshape(DEPTH, 3, 1, D), ln_b.reshape(DEPTH, 3, 1, D)

    def ffn(xs, ws, layer, sub, split_out=False, cast_next=None):
        casts = None if cast_next is None else (ffn_ws, cast_next)
        return _ffn_sublayer(xs, mod, *ws, ln_g, ln_b, layer, sub, split_out=split_out, casts=casts)

    x, *ws_01 = ffn([x_prompt.reshape(TP, D), x_sample.reshape(TS, D)], [w[0, 0].astype(BF16) for w in ffn_ws],
                    0, 0, cast_next=(0, 1))
    w_in = w_in_ab[0]
    o_aq, o_ai, o_ff, o_fb, o_ag = 0, A_W, 2 * A_W, 3 * A_W, 4 * A_W
    o_bq = 5 * A_W
    o_bk, o_bv = o_bq + B_QK, o_bq + 2 * B_QK
    o_bg = o_bv + B_V
    o_z = o_bg + B_V
    order = [(o_aq, A_W), (o_ff, A_W), (o_fb, A_W), (o_ag, A_W), (o_bq, B_QK), (o_bk, B_QK), (o_bg, B_V),
             (o_ai, A_W), (o_bv, B_V)]
    wmain = jnp.concatenate([w_in[:, o:o + w] for o, w in order], axis=1).astype(BF16)
    wz = jnp.pad(w_in[:, o_z:o_z + 2 * GATE_RANK], ((0, 0), (0, LANES - 2 * GATE_RANK))).astype(BF16)
    gup = jnp.zeros((LANES, 2 * B_QK), F32)
    gup = gup.at[:GATE_RANK, :B_QK].set(gla_gate_up[0, 0]).at[GATE_RANK:2 * GATE_RANK, B_QK:].set(gla_gate_up[0, 1])
    gb = gla_gate_b[0].reshape(1, 2 * B_QK)
    pf, pb = _inproj(x, mod, wmain, wz, gup.astype(BF16), gb, hgrn_lb, 0, 0)

    s0_a = state_hgrn[:, 0]
    s0_b = state_gla[:, 0].reshape(DEC_BATCH, 2, B_HEADS // 2, LANES, B_DV)
    oa_p, st_a = _scan(pf, pb, norm_a[0], None, prompt=True, pair=False)
    ob_p, st_b = _scan(pf, pb, norm_b[0], None, prompt=True, pair=True)
    (oa_s,) = _scan(pf, pb, norm_a[0], s0_a, prompt=False, pair=False)
    (ob_s,) = _scan(pf, pb, norm_b[0], s0_b, prompt=False, pair=True)
    w_out = w_out_ab[0].astype(BF16)
    x = _outproj([(oa_p, oa_s), (ob_p, ob_s)], [w_out[:A_W], w_out[A_W:]], x, mod, ln_g, ln_b, 0)
    x, *ws_10 = ffn([x], ws_01, 0, 1, cast_next=(1, 0))
    new_hgrn = st_a.reshape(BATCH, 1, 2, A_HEADS, A_DK, A_DV)
    new_gla = st_b.reshape(BATCH, 1, 2, B_HEADS, B_DK, B_DV)

    x, *ws_11 = ffn([x], ws_10, 1, 0, cast_next=(1, 1))
    cos, sin = _rope_tables()
    w_qkv = w_qkv_c[0].astype(BF16)
    q, k, v, kt, vt = _qkv(x, mod, w_qkv, w_qkv[:, C_Q:].T, cos, sin, 1)
    sink_cols = jnp.repeat(sink_c[0], C_HEAD_DIM).reshape(1, C_Q)
    o_p = _ctx_attn(q, kt, v, sink_cols)
    ck = cache_k[:, 0].reshape(DEC_BATCH * PAST_LEN, C_KV)
    cv = cache_v[:, 0].reshape(DEC_BATCH * PAST_LEN, C_KV)
    o_s = _lat_attn(q, k, v, ck, cv, sink_cols)
    x = _outproj([(o_p, o_s)], [w_out_c[0].astype(BF16)], x, mod, ln_g, ln_b, 1)
    y_p, y_s = ffn([x], ws_11, 1, 1, split_out=True)

    def cache_layout(zt):
        return zt.reshape(BATCH, 1, C_KV_HEADS, C_HEAD_DIM, SEQ).transpose(0, 1, 4, 2, 3)

    new_k, new_v = cache_layout(kt), cache_layout(vt)

    return (y_p.reshape(BATCH, SEQ, D), y_s.reshape(DEC_BATCH, DEC_SEQ, D), new_hgrn, new_gla, new_k, new_v)
```

```python
import functools
import math

import jax
import jax.numpy as jnp
import numpy as np
from jax import lax
from jax.experimental import pallas as pl
from jax.experimental.pallas import tpu as pltpu

D = 1024
BATCH, SEQ = 16, 256
DEC_BATCH, DEC_SEQ = 2, 2048
PAST_LEN = 512
GRID_W = 64
D_FF = 2816
N_MOD = 9
A_HEADS, A_DK, A_DV = 4, 128, 128
A_W = A_HEADS * A_DK
B_HEADS, B_DK, B_DV = 4, 64, 128
B_QK = B_HEADS * B_DK
B_V = B_HEADS * B_DV
GATE_RANK = 16
GLA_TAU = 16.0
CHUNK = 128
C_HEADS, C_KV_HEADS, C_HEAD_DIM = 16, 4, 64
C_GROUPS = C_HEADS // C_KV_HEADS
C_Q = C_HEADS * C_HEAD_DIM
C_KV = C_KV_HEADS * C_HEAD_DIM
WINDOW = 128
ROPE_FREQS = C_HEAD_DIM // 4
ROPE_BASE = 10000.0
DEPTH = 2
ALPHA = (2.0 * DEPTH) ** 0.25
LN_EPS = 1e-5
RMS_EPS = 1e-6

TP = BATCH * SEQ
TS = DEC_BATCH * DEC_SEQ
T = TP + TS
N_SEG = 1 + DEC_BATCH

LANES = 128
HALF = LANES // 2
FFN_TM = 1024
FFN_TM_SPLIT = 512
FFN_SUB = 256
PROJ_TM = 2 * SEQ
INPROJ_TM = 512
PROJ_SUB = 256
OUT_TM = 1024
VMEM_LIMIT = 56 * 1024 * 1024

F32 = jnp.float32
BF16 = jnp.bfloat16


def _dot(a, b):
    return jnp.dot(a, b, preferred_element_type=F32)


def _dot_nt(a, b):
    return lax.dot_general(a, b, (((1,), (1,)), ((), ())), preferred_element_type=F32)


def _dot_tn(a, b):
    return lax.dot_general(a, b, (((0,), (0,)), ((), ())), preferred_element_type=F32)


def _silu(x):
    return x * jax.nn.sigmoid(x)


def _layer_norm(z, g, b):
    mu = jnp.mean(z, axis=-1, keepdims=True)
    zc = z - mu
    var = jnp.mean(zc * zc, axis=-1, keepdims=True)
    return zc * lax.rsqrt(var + LN_EPS) * g + b


def _seg_of_tile(i, tm):
    n_p = TP // tm
    n_s = DEC_SEQ // tm
    return jnp.where(i < n_p, 0, 1 + lax.div(jnp.maximum(i - n_p, 0), n_s))


def _params(n_axes):
    return pltpu.CompilerParams(dimension_semantics=("arbitrary",) * n_axes, vmem_limit_bytes=VMEM_LIMIT)


def _resident(shape):
    nd = len(shape)
    return pl.BlockSpec(shape, lambda *_: (0,) * nd, pipeline_mode=pl.Buffered(1))


def _resident_slice(shape, lead):
    block = (None,) * len(lead) + tuple(shape)
    return pl.BlockSpec(block, lambda *_: tuple(lead) + (0,) * len(shape), pipeline_mode=pl.Buffered(1))


def _mod_specs(layer, cols, tm):
    return [pl.BlockSpec((None, None, 1, D), functools.partial(
        lambda i, c: (layer, _seg_of_tile(i, tm), 0, c), c=c)) for c in cols]


def _ln_specs(layer, idx):
    return [_resident_slice((1, D), (layer, idx))] * 2


BF16_SUBLANES = 16


def _cast_plan(ws, lead, n_steps, step_of):
    in_specs, out_specs, out_shapes = [], [], []
    for w in ws:
        rows, cols = w.shape[len(lead):]
        blk = next(b for b in range(BF16_SUBLANES, rows + 1, BF16_SUBLANES)
                   if rows % b == 0 and rows // b <= n_steps)
        last = rows // blk - 1
        in_specs.append(pl.BlockSpec((None,) * len(lead) + (blk, cols), functools.partial(
            lambda *g, last: tuple(lead) + (jnp.minimum(step_of(*g), last), 0), last=last)))
        out_specs.append(pl.BlockSpec((blk, cols), functools.partial(
            lambda *g, last: (jnp.minimum(step_of(*g), last), 0), last=last)))
        out_shapes.append(jax.ShapeDtypeStruct((rows, cols), BF16))
    return in_specs, out_specs, out_shapes


def _hosting_casts(body, n_in, n_out, n_cast):
    def hosted(*refs, **kw):
        ins, refs = refs[:n_in], refs[n_in:]
        cast_in, refs = refs[:n_cast], refs[n_cast:]
        outs, refs = refs[:n_out], refs[n_out:]
        cast_out, scratch = refs[:n_cast], refs[n_cast:]
        for src, dst in zip(cast_in, cast_out):
            dst[...] = src[...].astype(BF16)
        body(*ins, *outs, *scratch, **kw)
    return hosted


def _mod_body(c_ref, w_ref, b_ref, o_ref):
    c = c_ref[...]
    s = _silu(c).astype(BF16)
    o_ref[0] = _dot(s, w_ref[0].astype(BF16)) + b_ref[0]


def _mod_vectors(cs, w_mod, b_mod):
    tn = 1536
    n = N_MOD * D
    return pl.pallas_call(
        _mod_body,
        grid=(DEPTH, n // tn),
        in_specs=[
            pl.BlockSpec((8, D), lambda l, j: (0, 0)),
            pl.BlockSpec((1, D, tn), lambda l, j: (l, 0, j)),
            pl.BlockSpec((1, 1, tn), lambda l, j: (l, 0, j)),
        ],
        out_specs=pl.BlockSpec((1, 8, tn), lambda l, j: (l, 0, j)),
        out_shape=jax.ShapeDtypeStruct((DEPTH, 8, n), F32),
        compiler_params=_params(2),
        name="mod_vectors",
    )(cs, w_mod, b_mod.reshape(DEPTH, 1, n))


def _ffn_body(*refs, n_x, n_o, n_mix, tm):
    x_refs, refs = refs[:n_x], refs[n_x:]
    mix_refs, refs = refs[:2 * n_mix], refs[2 * n_mix:]
    wmix_refs, refs = refs[:n_mix], refs[n_mix:]
    if n_mix:
        (mgate_ref, mg_ref, mb_ref), refs = refs[:3], refs[3:]
    shift_ref, scale_ref, gate_ref, w1_ref, w3_ref, w2_ref, g_ref, b_ref = refs[:8]
    o_refs = refs[8:]

    def compute(group):
        x_ref, o_ref = x_refs[group % n_x], o_refs[group % n_o]
        shift, scale, gate = shift_ref[...], scale_ref[...], gate_ref[...]
        for r in range(0, tm, FFN_SUB):
            rows = slice(r, r + FFN_SUB)
            x = x_ref[rows, :]
            if n_mix:
                ym = _dot(mix_refs[group][rows, :], wmix_refs[0][...])
                for j in range(1, n_mix):
                    ym = ym + _dot(mix_refs[2 * j + group][rows, :], wmix_refs[j][...])
                x = _layer_norm(ALPHA * x + mgate_ref[...] * ym, mg_ref[...], mb_ref[...])
            h = (x * (1.0 + scale) + shift).astype(BF16)
            a = _dot(h, w1_ref[...])
            b = _dot(h, w3_ref[...])
            g = (_silu(a) * b).astype(BF16)
            y = _dot(g, w2_ref[...])
            z = ALPHA * x + (0.5 * gate) * y
            o_ref[rows, :] = _layer_norm(z, g_ref[...], b_ref[...])

    if n_x == 1 and n_o == 1 and n_mix == 0:
        compute(0)
    else:
        in_prompt = pl.program_id(0) < TP // tm
        pl.when(in_prompt)(lambda: compute(0))
        pl.when(jnp.logical_not(in_prompt))(lambda: compute(1))


def _group_specs(split, tm, width=D):
    if not split:
        return [pl.BlockSpec((tm, width), lambda i: (i, 0))]
    n_p = TP // tm
    return [pl.BlockSpec((tm, width), lambda i: (jnp.minimum(i, n_p - 1), 0)),
            pl.BlockSpec((tm, width), lambda i: (jnp.maximum(i - n_p, 0), 0))]


def _ffn_sublayer(xs, mod, w1, w3, w2, ln_g, ln_b, layer, sub, split_out=False, casts=None, mix=None):
    n_x, n_o = len(xs), 2 if split_out else 1
    n_mix = 0 if mix is None else len(mix[0])
    tm = FFN_TM if n_x == n_o == 1 and n_mix == 0 else FFN_TM_SPLIT
    out_shape = ([jax.ShapeDtypeStruct((TP, D), F32), jax.ShapeDtypeStruct((TS, D), F32)] if split_out
                 else [jax.ShapeDtypeStruct((T, D), F32)])
    mod_lo = 6 * sub
    body = functools.partial(_ffn_body, n_x=n_x, n_o=n_o, n_mix=n_mix, tm=tm)
    in_specs, args = _group_specs(n_x == 2, tm), list(xs)
    if n_mix:
        for a_p, a_s in mix[0]:
            in_specs += _group_specs(True, tm, a_p.shape[1])
            args += [a_p, a_s]
        in_specs += [_resident(w.shape) for w in mix[1]] + _mod_specs(layer, (5,), tm) + _ln_specs(layer, 1)
        args += [*mix[1], mod, ln_g, ln_b]
    in_specs += _mod_specs(layer, (mod_lo, mod_lo + 1, mod_lo + 2), tm) + [
        _resident((D, D_FF)),
        _resident((D, D_FF)),
        _resident((D_FF, D)),
    ] + _ln_specs(layer, 2 * sub)
    out_specs = _group_specs(split_out, tm)
    args += [mod, mod, mod, w1, w3, w2, ln_g, ln_b]
    if casts is not None:
        ws, lead = casts
        c_in, c_out, c_shapes = _cast_plan(ws, lead, T // tm, lambda i: i)
        body = _hosting_casts(body, len(in_specs), len(out_specs), len(ws))
        in_specs, out_specs, out_shape = in_specs + c_in, out_specs + c_out, out_shape + c_shapes
        args = args + list(ws)
    return pl.pallas_call(
        body,
        grid=(T // tm,),
        in_specs=in_specs,
        out_specs=out_specs,
        out_shape=out_shape,
        compiler_params=_params(1),
        name="ffn_sublayer",
    )(*args)


PF_AQ, PF_FF, PF_FB, PF_AG = 0, 512, 1024, 1536
PF_BQ, PF_BK, PF_BG, PF_LAF, PF_LAB = 2048, 2304, 2560, 3072, 3328
PF_W = 3584
PB_AV, PB_BV = 0, 512
PB_W = 1024
WM_AQ, WM_FF, WM_FB, WM_AG, WM_BQ, WM_BK, WM_BG, WM_AI, WM_BV = 0, 512, 1024, 1536, 2048, 2304, 2560, 3072, 3584
WM_W = 4096


def _log_sigmoid(x):
    return jnp.minimum(x, 0.0) - jnp.log(1.0 + jnp.exp(-jnp.abs(x)))


def _inproj_body(x_ref, shift_ref, scale_ref, w_ref, wz_ref, gu_ref, gb_ref, lb_ref, pf_ref, pb_ref, *, layer_e):
    def lower_bound(d):
        l = lb_ref[d]
        e = jnp.exp(l - jnp.max(l, axis=0, keepdims=True))
        sm = e / jnp.sum(e, axis=0, keepdims=True)
        return jnp.sum(sm[:layer_e + 1], axis=0, keepdims=True)

    lbs = [lower_bound(0), lower_bound(1)]
    for r in range(0, x_ref.shape[0], PROJ_SUB):
        rows = slice(r, r + PROJ_SUB)
        h = (x_ref[rows, :] * (1.0 + scale_ref[...]) + shift_ref[...]).astype(BF16)

        def proj(off, width):
            return _dot(h, w_ref[:, off:off + width])

        pf_ref[rows, PF_AQ:PF_AQ + A_W] = proj(WM_AQ, A_W)
        for lb, wm, pf in ((lbs[0], WM_FF, PF_FF), (lbs[1], WM_FB, PF_FB)):
            pf_ref[rows, pf:pf + A_W] = lb + (1.0 - lb) * jax.nn.sigmoid(proj(wm, A_W))
        pf_ref[rows, PF_AG:PF_AG + A_W] = _silu(proj(WM_AG, A_W))
        pf_ref[rows, PF_BQ:PF_BQ + B_QK] = proj(WM_BQ, B_QK) * (B_DK ** -0.5)
        pf_ref[rows, PF_BK:PF_BK + B_QK] = proj(WM_BK, B_QK)
        pf_ref[rows, PF_BG:PF_BG + B_V] = _silu(proj(WM_BG, B_V))
        pb_ref[rows, PB_AV:PB_AV + A_W] = _silu(proj(WM_AI, A_W)).astype(BF16)
        pb_ref[rows, PB_BV:PB_BV + B_V] = proj(WM_BV, B_V).astype(BF16)
        z = _dot(h, wz_ref[...]).astype(BF16)
        pre = _dot(z, gu_ref[...]) + gb_ref[...]
        pf_ref[rows, PF_LAF:PF_LAF + 2 * B_QK] = _log_sigmoid(pre) * (1.0 / GLA_TAU)


def _inproj(x, mod, wmain, wz, gup, gb, hgrn_lb, layer, layer_e, tm=INPROJ_TM):
    n_l = hgrn_lb.shape[1]
    return pl.pallas_call(
        functools.partial(_inproj_body, layer_e=layer_e),
        grid=(T // tm,),
        in_specs=[pl.BlockSpec((tm, D), lambda i: (i, 0))] + _mod_specs(layer, (3, 4), tm) + [
            _resident((D, WM_W)),
            _resident((D, LANES)),
            _resident((LANES, 2 * B_QK)),
            _resident((1, 2 * B_QK)),
            _resident((2, n_l, A_W)),
        ],
        out_specs=[
            pl.BlockSpec((tm, PF_W), lambda i: (i, 0)),
            pl.BlockSpec((tm, PB_W), lambda i: (i, 0)),
        ],
        out_shape=[jax.ShapeDtypeStruct((T, PF_W), F32), jax.ShapeDtypeStruct((T, PB_W), BF16)],
        compiler_params=_params(1),
        name="ab_inproj",
    )(x, mod, mod, wmain, wz, gup, gb, hgrn_lb)


SCAN_PROMPT_SEQS = 4
SCAN_UNROLL = 8


def _prefix_rows(x):
    row = lax.broadcasted_iota(jnp.int32, x.shape, 0)
    s = 1
    while s < x.shape[0]:
        x = x + jnp.where(row >= s, pltpu.roll(x, s, axis=0), 0.0)
        s *= 2
    return x


def _scan_body(*refs, seq_len, seqs, pair, has_s0, emit_state):
    n = seq_len // CHUNK
    n_all = seqs * n
    nh = 2 if pair else 1
    it = iter(refs)
    q_ref = next(it)
    if pair:
        k_ref, laf_ref, lab_ref = next(it), next(it), next(it)
    else:
        ff_ref, fb_ref = next(it), next(it)
    g_ref, v_ref, nw_ref = next(it), next(it), next(it)
    s0_ref = next(it) if has_s0 else None
    o_ref = next(it)
    st_ref = next(it) if emit_state else None
    qd_s, oi_s, kv_s, dec_s, sb_s = it

    row = lax.broadcasted_iota(jnp.int32, (CHUNK, CHUNK), 0)
    col = lax.broadcasted_iota(jnp.int32, (CHUNK, CHUNK), 1)
    tril = row >= col
    triu = row <= col
    lane = lax.broadcasted_iota(jnp.int32, (1, LANES), 1)
    lane2 = lax.broadcasted_iota(jnp.int32, (1, 2 * LANES), 1)
    if pair:
        masks = [lane < HALF, lane >= HALF]
        masks2 = [(lane2 % LANES) < HALF, (lane2 % LANES) >= HALF]
    else:
        masks, masks2 = [None], [None]

    def pick(mask, x):
        return x if mask is None else jnp.where(mask, x, jnp.zeros_like(x))

    def rows_of(c):
        return pl.ds(pl.multiple_of(c * CHUNK, CHUNK), CHUNK)

    def loop(body):
        if n_all <= SCAN_UNROLL:
            for c in range(n_all):
                body(c)
        else:
            def fbody(i, carry):
                for u in range(SCAN_UNROLL):
                    body(i * SCAN_UNROLL + u)
                return carry
            lax.fori_loop(0, n_all // SCAN_UNROLL, fbody, 0)

    def phase1(c):
        rows = rows_of(c)
        q = q_ref[rows, :]
        if pair:
            k_f = k_b = k_ref[rows, :]
            la_f, la_b = laf_ref[rows, :], lab_ref[rows, :]
        else:
            f_f, f_b = ff_ref[rows, :], fb_ref[rows, :]
            k_f, k_b = 1.0 - f_f, 1.0 - f_b
            la_f, la_b = jnp.log(f_f), jnp.log(f_b)
        cs = _prefix_rows(jnp.concatenate([la_f, la_b], axis=1))
        cf, cbi = cs[:, :LANES], cs[:, LANES:]
        tot_f, tot_b = cf[CHUNK - 1:CHUNK, :], cbi[CHUNK - 1:CHUNK, :]
        rb = tot_b - cbi + la_b
        ref_f, ref_b = cf[CHUNK // 2 - 1:CHUNK // 2, :], rb[CHUNK // 2:CHUNK // 2 + 1, :]
        qtf = q * jnp.exp(cf - ref_f)
        qtb = q * jnp.exp(rb - ref_b)
        ktf = k_f * jnp.exp(ref_f - cf)
        ktb = k_b * jnp.exp(ref_b - rb)
        qd = jnp.concatenate([qtf * jnp.exp(ref_f), qtb * jnp.exp(ref_b)], axis=1).astype(BF16)
        ku = jnp.concatenate([ktf * jnp.exp(tot_f - ref_f), ktb * jnp.exp(tot_b - ref_b)], axis=1).astype(BF16)
        qd_s[rows, :] = qd
        qt = jnp.concatenate([qtf, qtb], axis=0).astype(BF16)
        kt = jnp.concatenate([ktf, ktb], axis=0).astype(BF16)
        kv = None
        for hh in range(nh):
            v = v_ref[rows, hh * LANES:(hh + 1) * LANES]
            sc = _dot_nt(pick(masks[hh], qt), kt)
            att = jnp.where(tril, sc[:CHUNK, :CHUNK], 0.0) + jnp.where(triu, sc[CHUNK:, CHUNK:], 0.0)
            oi_s[rows, hh * LANES:(hh + 1) * LANES] = _dot(att.astype(BF16), v)
            kv_h = _dot_tn(v, ku)
            kv = kv_h if kv is None else jnp.where(masks2[0], kv, kv_h)
        kv_s[c] = kv
        dec_s[c] = jnp.exp(jnp.concatenate([tot_f, tot_b], axis=1))

    loop(phase1)

    def recurrence(sq, d, reverse):
        cols = slice(d * LANES, (d + 1) * LANES)
        c0 = sq * n
        st0 = s0_ref[sq, d, 0].T if has_s0 else jnp.zeros((LANES, LANES), F32)

        def step(c, st):
            sb_s[c, :, cols] = st.astype(BF16)
            return st * dec_s[c, :, cols] + kv_s[c, :, cols]

        if n <= 8:
            st = st0
            for c in (range(n - 1, -1, -1) if reverse else range(n)):
                st = step(c0 + c, st)
        else:
            st = lax.fori_loop(0, n, lambda i, st: step(c0 + (n - 1 - i if reverse else i), st), st0)
        if emit_state:
            st_ref[sq, d, 0] = st.T

    for sq in range(seqs):
        recurrence(sq, 0, False)
        recurrence(sq, 1, True)

    nw = nw_ref[...]

    def phase2(c):
        rows = rows_of(c)
        qcat = qd_s[rows, :]
        scat = sb_s[c]
        for hh in range(nh):
            cols = slice(hh * LANES, (hh + 1) * LANES)
            o = oi_s[rows, cols] + _dot_nt(pick(masks2[hh], qcat), scat)
            o = o * lax.rsqrt(jnp.mean(o * o, axis=-1, keepdims=True) + RMS_EPS) * nw
            o_ref[rows, cols] = (o * g_ref[rows, cols]).astype(BF16)

    loop(phase2)


def _scan(pf, pb, norm_w, s0, *, prompt, pair):
    seq_len = SEQ if prompt else DEC_SEQ
    nseq = BATCH if prompt else DEC_BATCH
    seqs = SCAN_PROMPT_SEQS if prompt else 1
    rows = seqs * seq_len
    row_off = 0 if prompt else TP // rows
    units = B_HEADS // 2 if pair else A_HEADS
    nh = 2 if pair else 1
    n_all = rows // CHUNK
    has_s0 = s0 is not None
    emit_state = prompt

    def colspec(off, width=LANES):
        base = off // width
        return pl.BlockSpec((rows, width), lambda s, u: (s + row_off, base + u))

    if pair:
        in_specs = [colspec(PF_BQ), colspec(PF_BK), colspec(PF_LAF), colspec(PF_LAB),
                    colspec(PF_BG, 2 * LANES), colspec(PB_BV, 2 * LANES)]
        args = [pf, pf, pf, pf, pf, pb]
    else:
        in_specs = [colspec(PF_AQ), colspec(PF_FF), colspec(PF_FB), colspec(PF_AG), colspec(PB_AV)]
        args = [pf, pf, pf, pf, pb]
    in_specs.append(pl.BlockSpec((1, LANES), lambda s, u: (0, 0)))
    args.append(norm_w.reshape(1, LANES))
    state_spec = pl.BlockSpec((seqs, 2, 1, LANES, LANES), lambda s, u: (s, 0, u, 0, 0))
    if has_s0:
        in_specs.append(state_spec)
        args.append(s0)
    out_specs = [pl.BlockSpec((rows, nh * LANES), lambda s, u: (s, u))]
    out_shape = [jax.ShapeDtypeStruct((nseq * seq_len, units * nh * LANES), BF16)]
    if emit_state:
        out_specs.append(state_spec)
        out_shape.append(jax.ShapeDtypeStruct((nseq, 2, units, LANES, LANES), F32))
    scratch = [
        pltpu.VMEM((rows, 2 * LANES), BF16),
        pltpu.VMEM((rows, nh * LANES), F32),
        pltpu.VMEM((n_all, LANES, 2 * LANES), F32),
        pltpu.VMEM((n_all, 1, 2 * LANES), F32),
        pltpu.VMEM((n_all, LANES, 2 * LANES), BF16),
    ]
    return pl.pallas_call(
        functools.partial(_scan_body, seq_len=seq_len, seqs=seqs, pair=pair, has_s0=has_s0,
                          emit_state=emit_state),
        grid=(nseq // seqs, units),
        in_specs=in_specs,
        out_specs=out_specs,
        out_shape=out_shape,
        scratch_shapes=scratch,
        compiler_params=_params(2),
        name=f"scan_{'p' if prompt else 's'}_{'gla' if pair else 'hgrn'}",
    )(*args)


def _outproj_body(*refs, n_lhs, tm):
    lhs = refs[:2 * n_lhs]
    ws = refs[2 * n_lhs:3 * n_lhs]
    x_ref, m_ref, g_ref, b_ref, o_ref = refs[3 * n_lhs:]

    def compute(group):
        for r in range(0, tm, PROJ_SUB):
            rows = slice(r, r + PROJ_SUB)
            y = _dot(lhs[group][rows, :], ws[0][...])
            for j in range(1, n_lhs):
                y = y + _dot(lhs[2 * j + group][rows, :], ws[j][...])
            z = ALPHA * x_ref[rows, :] + m_ref[...] * y
            o_ref[rows, :] = _layer_norm(z, g_ref[...], b_ref[...])

    in_prompt = pl.program_id(0) < TP // tm
    pl.when(in_prompt)(lambda: compute(0))
    pl.when(jnp.logical_not(in_prompt))(lambda: compute(1))


def _outproj(lhs, ws, x, mod, ln_g, ln_b, layer, tm=OUT_TM):
    n_lhs = len(lhs)
    in_specs, args = [], []
    for a_p, a_s in lhs:
        in_specs += _group_specs(True, tm, a_p.shape[1])
        args += [a_p, a_s]
    in_specs += [_resident(w.shape) for w in ws]
    in_specs += [pl.BlockSpec((tm, D), lambda i: (i, 0))] + _mod_specs(layer, (5,), tm) + _ln_specs(layer, 1)
    return pl.pallas_call(
        functools.partial(_outproj_body, n_lhs=n_lhs, tm=tm),
        grid=(T // tm,),
        in_specs=in_specs,
        out_specs=pl.BlockSpec((tm, D), lambda i: (i, 0)),
        out_shape=jax.ShapeDtypeStruct((T, D), F32),
        compiler_params=_params(1),
        name="mixer_outproj",
    )(*args, *ws, x, mod, ln_g, ln_b)


def _rope_partner(x):
    lane = lax.broadcasted_iota(jnp.int32, x.shape, 1)
    first_half = (lane % (2 * ROPE_FREQS)) < ROPE_FREQS
    return jnp.where(first_half, pltpu.roll(x, LANES - ROPE_FREQS, axis=1), pltpu.roll(x, ROPE_FREQS, axis=1))


def _qkv_body(x_ref, shift_ref, scale_ref, w_ref, wkvt_ref, cos_ref, sin_ref,
              q_ref, k_ref, v_ref, kt_ref, vt_ref):
    qscale = C_HEAD_DIM ** -0.5
    in_prompt = pl.program_id(0) < TP // PROJ_TM

    def modulated(rows):
        return (x_ref[rows, :] * (1.0 + scale_ref[...]) + shift_ref[...]).astype(BF16)

    @pl.when(in_prompt)
    def _():
        for sq in range(PROJ_TM // SEQ):
            rows = slice(sq * SEQ, (sq + 1) * SEQ)
            h = modulated(rows)
            q_ref[rows, :] = (_dot(h, w_ref[:, :C_Q]) * qscale).astype(BF16)
            v_ref[rows, :] = _dot(h, w_ref[:, C_Q + C_KV:]).astype(BF16)
            kt_ref[sq] = _dot_nt(wkvt_ref[:C_KV, :], h)
            vt_ref[sq] = _dot_nt(wkvt_ref[C_KV:, :], h)

    @pl.when(jnp.logical_not(in_prompt))
    def _():
        for sq in range(PROJ_TM // SEQ):
            rows = slice(sq * SEQ, (sq + 1) * SEQ)
            h = modulated(rows)
            cos, sin = cos_ref[rows, :], sin_ref[rows, :]
            zq = _dot(h, w_ref[:, :C_Q])
            zk = _dot(h, w_ref[:, C_Q:C_Q + C_KV])
            v_ref[rows, :] = _dot(h, w_ref[:, C_Q + C_KV:]).astype(BF16)

            def rope(z, cos=cos, sin=sin):
                return z * cos + _rope_partner(z) * sin

            for j in range(C_Q // LANES):
                cols = slice(j * LANES, (j + 1) * LANES)
                q_ref[rows, cols] = (rope(zq[:, cols]) * qscale).astype(BF16)
            for j in range(C_KV // LANES):
                cols = slice(j * LANES, (j + 1) * LANES)
                k_ref[rows, cols] = rope(zk[:, cols]).astype(BF16)


def _qkv(x, mod, w, wkvt, cos, sin, layer):
    tm = PROJ_TM
    n_p = TP // tm
    lat = lambda i: (jnp.maximum(i - n_p, 0), 0)
    ctx = lambda i: (jnp.minimum(i, n_p - 1), 0, 0)
    return pl.pallas_call(
        _qkv_body,
        grid=(T // tm,),
        in_specs=[pl.BlockSpec((tm, D), lambda i: (i, 0))] + _mod_specs(layer, (3, 4), tm) + [
            _resident((D, C_Q + 2 * C_KV)),
            _resident((2 * C_KV, D)),
            pl.BlockSpec((tm, LANES), lat),
            pl.BlockSpec((tm, LANES), lat),
        ],
        out_specs=[
            pl.BlockSpec((tm, C_Q), lambda i: (i, 0)),
            pl.BlockSpec((tm, C_KV), lat),
            pl.BlockSpec((tm, C_KV), lambda i: (i, 0)),
            pl.BlockSpec((tm // SEQ, C_KV, SEQ), ctx),
            pl.BlockSpec((tm // SEQ, C_KV, SEQ), ctx),
        ],
        out_shape=[jax.ShapeDtypeStruct((T, C_Q), BF16), jax.ShapeDtypeStruct((TS, C_KV), BF16),
                   jax.ShapeDtypeStruct((T, C_KV), BF16),
                   jax.ShapeDtypeStruct((BATCH, C_KV, SEQ), F32), jax.ShapeDtypeStruct((BATCH, C_KV, SEQ), F32)],
        compiler_params=_params(1),
        name="c_qkv",
    )(x, mod, mod, w, wkvt, cos, sin)


def _rope_tables():
    t = np.arange(DEC_SEQ)
    pos = np.stack([t // GRID_W, t % GRID_W], axis=1).astype(np.float32)
    inv = (ROPE_BASE ** (-np.arange(ROPE_FREQS, dtype=np.float32) / ROPE_FREQS)).astype(np.float32)
    d = np.arange(C_HEAD_DIM)
    axis = d // (2 * ROPE_FREQS)
    ang = pos[:, axis] * inv[d % ROPE_FREQS][None, :]
    sign = np.where((d % (2 * ROPE_FREQS)) < ROPE_FREQS, -1.0, 1.0)[None, :]
    cos_h, sin_h = np.cos(ang), np.sin(ang) * sign
    reps = LANES // C_HEAD_DIM
    cos = np.tile(np.tile(cos_h, (1, reps)), (DEC_BATCH, 1)).astype(np.float32)
    sin = np.tile(np.tile(sin_h, (1, reps)), (DEC_BATCH, 1)).astype(np.float32)
    return jnp.asarray(cos), jnp.asarray(sin)


def _dup_head(blk, half):
    lane = lax.broadcasted_iota(jnp.int32, blk.shape, 1)
    keep = (lane >= HALF).astype(jnp.int32) == half
    return jnp.where(keep, blk, pltpu.roll(blk, HALF, axis=1))


def _pair_values(v, half):
    lo = lax.broadcasted_iota(jnp.int32, v.shape, 1) < HALF
    v2 = _dup_head(v.astype(F32), half)
    top = jnp.concatenate([jnp.where(lo, v2, 0.0), jnp.where(lo, 1.0, 0.0)], axis=1)
    bot = jnp.concatenate([jnp.where(lo, 0.0, v2), jnp.where(lo, 0.0, 1.0)], axis=1)
    return top.astype(BF16), bot.astype(BF16)


def _softmax_weights(scores, sink):
    mx = sink
    for s in scores:
        mx = jnp.maximum(mx, jnp.max(s, axis=-1, keepdims=True))
    return [jnp.exp(s - mx).astype(BF16) for s in scores], jnp.exp(sink - mx)


def _pair_sinks(sink_ref, jb):
    return (sink_ref[:, jb * LANES:jb * LANES + 1], sink_ref[:, jb * LANES + HALF:jb * LANES + HALF + 1])


KV_PER_BLOCK = LANES // C_HEAD_DIM
CTX_TILES = BATCH * C_KV_HEADS // KV_PER_BLOCK
CTX_BLOCKS = KV_PER_BLOCK * C_GROUPS // 2
PIPE_LAG = 2


def _ctx_tile(g, lag):
    t = jnp.clip(g - lag, 0, CTX_TILES - 1)
    n = C_KV_HEADS // KV_PER_BLOCK
    return t // n, t % n


def _ctx_attn_body(q_ref, kt_ref, v_ref, sink_ref, o_ref, s_s, e_s, st_s):
    g = pl.program_id(0)
    lo = lax.broadcasted_iota(jnp.int32, (1, LANES), 1) < HALF
    pairs_per_kv = C_GROUPS // 2

    @pl.when(g == 0)
    def _():
        s_s[...] = jnp.zeros_like(s_s)
        e_s[...] = jnp.zeros_like(e_s)
        st_s[...] = jnp.ones_like(st_s)

    def stages(cur, prev):
        v = v_ref[...]
        for kh in range(KV_PER_BLOCK):
            v_top, v_bot = _pair_values(v, kh)
            for jb in range(kh * pairs_per_kv, (kh + 1) * pairs_per_kv):
                od = _dot(e_s[cur, 2 * jb], v_top) + _dot(e_s[cur, 2 * jb + 1], v_bot)
                o_ref[:, jb * LANES:(jb + 1) * LANES] = (od[:, :LANES] / (od[:, LANES:] + st_s[cur, jb])).astype(BF16)
        for jb in range(CTX_BLOCKS):
            terms = []
            for hh, sink in enumerate(_pair_sinks(sink_ref, jb)):
                (e,), term = _softmax_weights([s_s[prev, 2 * jb + hh]], sink)
                e_s[prev, 2 * jb + hh] = e
                terms.append(term)
            st_s[prev, jb] = jnp.where(lo, terms[0], terms[1])
        for kh in range(KV_PER_BLOCK):
            kt = kt_ref[kh * C_HEAD_DIM:(kh + 1) * C_HEAD_DIM, :]
            zero = jnp.zeros_like(kt)
            kt_a = jnp.concatenate([kt, zero], axis=0).astype(BF16)
            kt_b = jnp.concatenate([zero, kt], axis=0).astype(BF16)
            for jb in range(kh * pairs_per_kv, (kh + 1) * pairs_per_kv):
                q = q_ref[:, jb * LANES:(jb + 1) * LANES]
                s_s[cur, 2 * jb] = _dot(q, kt_a)
                s_s[cur, 2 * jb + 1] = _dot(q, kt_b)

    pl.when(g % 2 == 0)(lambda: stages(0, 1))
    pl.when(g % 2 == 1)(lambda: stages(1, 0))


def _ctx_attn(q, kt, v, sink_cols):
    gw = CTX_BLOCKS * LANES

    def at(lag, fn):
        return lambda g: fn(*_ctx_tile(g, lag))

    return pl.pallas_call(
        _ctx_attn_body,
        grid=(CTX_TILES + PIPE_LAG,),
        in_specs=[
            pl.BlockSpec((SEQ, gw), at(0, lambda b, p: (b, p))),
            pl.BlockSpec((None, LANES, SEQ), at(0, lambda b, p: (b, p, 0))),
            pl.BlockSpec((SEQ, LANES), at(2, lambda b, p: (b, p))),
            pl.BlockSpec((1, gw), at(1, lambda b, p: (0, p))),
        ],
        out_specs=pl.BlockSpec((SEQ, gw), at(2, lambda b, p: (b, p))),
        out_shape=jax.ShapeDtypeStruct((TP, C_Q), BF16),
        scratch_shapes=[
            pltpu.VMEM((2, 2 * CTX_BLOCKS, SEQ, SEQ), F32),
            pltpu.VMEM((2, 2 * CTX_BLOCKS, SEQ, SEQ), BF16),
            pltpu.VMEM((2, CTX_BLOCKS, SEQ, LANES), F32),
        ],
        compiler_params=_params(1),
        name="c_attn_ctx",
    )(q, kt, v, sink_cols)


LAT_TQ = 128
LAT_WIN = LAT_TQ + 2 * WINDOW
LAT_NQ = DEC_SEQ // LAT_TQ
LAT_KVB = C_KV_HEADS // KV_PER_BLOCK
LAT_TILES = DEC_BATCH * LAT_KVB * LAT_NQ


def _lat_tile(g, lag):
    t = jnp.clip(g - lag, 0, LAT_TILES - 1)
    bp = t // LAT_NQ
    return bp // LAT_KVB, bp % LAT_KVB, t % LAT_NQ


def _lat_window(j):
    return pl.multiple_of(jnp.clip(j * LAT_TQ - WINDOW, 0, DEC_SEQ - LAT_WIN), LANES)


def _lat_attn_body(q_ref, k_ref, v_ref, ck_ref, cv_ref, sink_ref, o_ref,
                   ka_s, kb_s, cka_s, ckb_s, va_s, vb_s, cva_s, cvb_s, sl_s, sc_s, el_s, ec_s, st_s):
    g = pl.program_id(0)
    lo = lax.broadcasted_iota(jnp.int32, (1, LANES), 1) < HALF
    pairs_per_kv = C_GROUPS // 2
    b_a, p_a, j_a = _lat_tile(g, 0)
    b_c, p_c, j_c = _lat_tile(g, 2)
    v_slot_a = (b_a * LAT_KVB + p_a) % 2
    v_slot_c = (b_c * LAT_KVB + p_c) % 2

    @pl.when(g == 0)
    def _():
        sl_s[...] = jnp.zeros_like(sl_s)
        sc_s[...] = jnp.zeros_like(sc_s)
        el_s[...] = jnp.zeros_like(el_s)
        ec_s[...] = jnp.zeros_like(ec_s)
        st_s[...] = jnp.ones_like(st_s)

    @pl.when(jnp.logical_and(j_a == 0, g < LAT_TILES))
    def _():
        for kh in range(KV_PER_BLOCK):
            for src, a_s, b_s in ((k_ref, ka_s, kb_s), (ck_ref, cka_s, ckb_s)):
                lo2 = lax.broadcasted_iota(jnp.int32, src.shape, 1) < HALF
                k2 = _dup_head(src[...].astype(F32), kh)
                a_s[kh] = jnp.where(lo2, k2, 0.0).astype(BF16)
                b_s[kh] = jnp.where(lo2, 0.0, k2).astype(BF16)
            va_s[v_slot_a, kh], vb_s[v_slot_a, kh] = _pair_values(v_ref[...], kh)
            cva_s[v_slot_a, kh], cvb_s[v_slot_a, kh] = _pair_values(cv_ref[...], kh)

    def stages(cur, prev):
        win_c = pl.ds(_lat_window(j_c), LAT_WIN)
        for kh in range(KV_PER_BLOCK):
            v_loc = (va_s[v_slot_c, kh, win_c, :], vb_s[v_slot_c, kh, win_c, :])
            v_ctx = (cva_s[v_slot_c, kh], cvb_s[v_slot_c, kh])
            for jb in range(kh * pairs_per_kv, (kh + 1) * pairs_per_kv):
                od = None
                for hh in range(2):
                    part = _dot(el_s[cur, 2 * jb + hh], v_loc[hh]) + _dot(ec_s[cur, 2 * jb + hh], v_ctx[hh])
                    od = part if od is None else od + part
                o_ref[:, jb * LANES:(jb + 1) * LANES] = (od[:, :LANES] / (od[:, LANES:] + st_s[cur, jb])).astype(BF16)
        for jb in range(CTX_BLOCKS):
            terms = []
            for hh, sink in enumerate(_pair_sinks(sink_ref, jb)):
                (e_loc, e_ctx), term = _softmax_weights([sl_s[prev, 2 * jb + hh], sc_s[prev, 2 * jb + hh]], sink)
                el_s[prev, 2 * jb + hh] = e_loc
                ec_s[prev, 2 * jb + hh] = e_ctx
                terms.append(term)
            st_s[prev, jb] = jnp.where(lo, terms[0], terms[1])
        start = _lat_window(j_a)
        win = pl.ds(start, LAT_WIN)
        qpos = j_a * LAT_TQ + lax.broadcasted_iota(jnp.int32, (LAT_TQ, LAT_WIN), 0)
        kpos = start + lax.broadcasted_iota(jnp.int32, (LAT_TQ, LAT_WIN), 1)
        band = jnp.abs(qpos - kpos) <= WINDOW
        for kh in range(KV_PER_BLOCK):
            for jb in range(kh * pairs_per_kv, (kh + 1) * pairs_per_kv):
                q = q_ref[:, jb * LANES:(jb + 1) * LANES]
                for hh, (kl, kc) in enumerate(((ka_s, cka_s), (kb_s, ckb_s))):
                    sl_s[cur, 2 * jb + hh] = jnp.where(band, _dot_nt(q, kl[kh, win, :]), -jnp.inf)
                    sc_s[cur, 2 * jb + hh] = _dot_nt(q, kc[kh])

    pl.when(g % 2 == 0)(lambda: stages(0, 1))
    pl.when(g % 2 == 1)(lambda: stages(1, 0))


def _lat_attn(q, k, v, ck, cv, sink_cols):
    gw = CTX_BLOCKS * LANES
    q_off = TP // LAT_TQ
    kv_off = TP // DEC_SEQ
    kvb, n_heads = KV_PER_BLOCK, 2 * CTX_BLOCKS

    def at(lag, fn):
        return lambda g: fn(*_lat_tile(g, lag))

    return pl.pallas_call(
        _lat_attn_body,
        grid=(LAT_TILES + PIPE_LAG,),
        in_specs=[
            pl.BlockSpec((LAT_TQ, gw), at(0, lambda b, p, j: (q_off + b * LAT_NQ + j, p))),
            pl.BlockSpec((DEC_SEQ, LANES), at(0, lambda b, p, j: (b, p))),
            pl.BlockSpec((DEC_SEQ, LANES), at(0, lambda b, p, j: (kv_off + b, p))),
            pl.BlockSpec((PAST_LEN, LANES), at(0, lambda b, p, j: (b, p))),
            pl.BlockSpec((PAST_LEN, LANES), at(0, lambda b, p, j: (b, p))),
            pl.BlockSpec((1, gw), at(1, lambda b, p, j: (0, p))),
        ],
        out_specs=pl.BlockSpec((LAT_TQ, gw), at(2, lambda b, p, j: (b * LAT_NQ + j, p))),
        out_shape=jax.ShapeDtypeStruct((TS, C_Q), BF16),
        scratch_shapes=[
            pltpu.VMEM((kvb, DEC_SEQ, LANES), BF16), pltpu.VMEM((kvb, DEC_SEQ, LANES), BF16),
            pltpu.VMEM((kvb, PAST_LEN, LANES), BF16), pltpu.VMEM((kvb, PAST_LEN, LANES), BF16),
            pltpu.VMEM((2, kvb, DEC_SEQ, 2 * LANES), BF16), pltpu.VMEM((2, kvb, DEC_SEQ, 2 * LANES), BF16),
            pltpu.VMEM((2, kvb, PAST_LEN, 2 * LANES), BF16), pltpu.VMEM((2, kvb, PAST_LEN, 2 * LANES), BF16),
            pltpu.VMEM((2, n_heads, LAT_TQ, LAT_WIN), F32), pltpu.VMEM((2, n_heads, LAT_TQ, PAST_LEN), F32),
            pltpu.VMEM((2, n_heads, LAT_TQ, LAT_WIN), BF16), pltpu.VMEM((2, n_heads, LAT_TQ, PAST_LEN), BF16),
            pltpu.VMEM((2, CTX_BLOCKS, LAT_TQ, LANES), F32),
        ],
        compiler_params=_params(1),
        name="c_attn_latent",
    )(q, k, v, ck, cv, sink_cols)


def kernel(x_prompt, x_sample, state_hgrn, state_gla, cache_k, cache_v, c, c_ctx, w_mod, b_mod, ln_g, ln_b,
           ffn_w1, ffn_w3, ffn_w2, w_in_ab, hgrn_lb, gla_gate_up, gla_gate_b, norm_a, norm_b, w_out_ab,
           w_qkv_c, sink_c, w_out_c):
    cs = jnp.zeros((8, D), F32).at[0].set(c_ctx).at[1:1 + DEC_BATCH].set(c)
    mod = _mod_vectors(cs, w_mod, b_mod).reshape(DEPTH, 8, 1, N_MOD * D)
    ffn_ws = (ffn_w1, ffn_w3, ffn_w2)
    ln_g, ln_b = ln_g.reshape(DEPTH, 3, 1, D), ln_b.reshape(DEPTH, 3, 1, D)

    def ffn(xs, ws, layer, sub, split_out=False, cast_next=None, mix=None):
        casts = None if cast_next is None else (ffn_ws, cast_next)
        return _ffn_sublayer(xs, mod, *ws, ln_g, ln_b, layer, sub, split_out=split_out, casts=casts, mix=mix)

    x, *ws_01 = ffn([x_prompt.reshape(TP, D), x_sample.reshape(TS, D)], [w[0, 0].astype(BF16) for w in ffn_ws],
                    0, 0, cast_next=(0, 1))
    w_in = w_in_ab[0]
    o_aq, o_ai, o_ff, o_fb, o_ag = 0, A_W, 2 * A_W, 3 * A_W, 4 * A_W
    o_bq = 5 * A_W
    o_bk, o_bv = o_bq + B_QK, o_bq + 2 * B_QK
    o_bg = o_bv + B_V
    o_z = o_bg + B_V
    order = [(o_aq, A_W), (o_ff, A_W), (o_fb, A_W), (o_ag, A_W), (o_bq, B_QK), (o_bk, B_QK), (o_bg, B_V),
             (o_ai, A_W), (o_bv, B_V)]
    wmain = jnp.concatenate([w_in[:, o:o + w] for o, w in order], axis=1).astype(BF16)
    wz = jnp.pad(w_in[:, o_z:o_z + 2 * GATE_RANK], ((0, 0), (0, LANES - 2 * GATE_RANK))).astype(BF16)
    gup = jnp.zeros((LANES, 2 * B_QK), F32)
    gup = gup.at[:GATE_RANK, :B_QK].set(gla_gate_up[0, 0]).at[GATE_RANK:2 * GATE_RANK, B_QK:].set(gla_gate_up[0, 1])
    gb = gla_gate_b[0].reshape(1, 2 * B_QK)
    pf, pb = _inproj(x, mod, wmain, wz, gup.astype(BF16), gb, hgrn_lb, 0, 0)

    s0_a = state_hgrn[:, 0]
    s0_b = state_gla[:, 0].reshape(DEC_BATCH, 2, B_HEADS // 2, LANES, B_DV)
    oa_p, st_a = _scan(pf, pb, norm_a[0], None, prompt=True, pair=False)
    ob_p, st_b = _scan(pf, pb, norm_b[0], None, prompt=True, pair=True)
    (oa_s,) = _scan(pf, pb, norm_a[0], s0_a, prompt=False, pair=False)
    (ob_s,) = _scan(pf, pb, norm_b[0], s0_b, prompt=False, pair=True)
    w_out = w_out_ab[0].astype(BF16)
    x, *ws_10 = ffn([x], ws_01, 0, 1, cast_next=(1, 0),
                    mix=([(oa_p, oa_s), (ob_p, ob_s)], [w_out[:A_W], w_out[A_W:]]))
    new_hgrn = st_a.reshape(BATCH, 1, 2, A_HEADS, A_DK, A_DV)
    new_gla = st_b.reshape(BATCH, 1, 2, B_HEADS, B_DK, B_DV)

    x, *ws_11 = ffn([x], ws_10, 1, 0, cast_next=(1, 1))
    cos, sin = _rope_tables()
    w_qkv = w_qkv_c[0].astype(BF16)
    q, k, v, kt, vt = _qkv(x, mod, w_qkv, w_qkv[:, C_Q:].T, cos, sin, 1)
    sink_cols = jnp.repeat(sink_c[0], C_HEAD_DIM).reshape(1, C_Q)
    o_p = _ctx_attn(q, kt, v, sink_cols)
    ck = cache_k[:, 0].reshape(DEC_BATCH * PAST_LEN, C_KV)
    cv = cache_v[:, 0].reshape(DEC_BATCH * PAST_LEN, C_KV)
    o_s = _lat_attn(q, k, v, ck, cv, sink_cols)
    y_p, y_s = ffn([x], ws_11, 1, 1, split_out=True, mix=([(o_p, o_s)], [w_out_c[0].astype(BF16)]))

    def cache_layout(zt):
        return zt.reshape(BATCH, 1, C_KV_HEADS, C_HEAD_DIM, SEQ).transpose(0, 1, 4, 2, 3)

    new_k, new_v = cache_layout(kt), cache_layout(vt)

    return (y_p.reshape(BATCH, SEQ, D), y_s.reshape(DEC_BATCH, DEC_SEQ, D), new_hgrn, new_gla, new_k, new_v)
```

```python
import functools
import math

import jax
import jax.numpy as jnp
import numpy as np
from jax import lax
from jax.experimental import pallas as pl
from jax.experimental.pallas import tpu as pltpu

D = 1024
BATCH, SEQ = 16, 256
DEC_BATCH, DEC_SEQ = 2, 2048
PAST_LEN = 512
GRID_W = 64
D_FF = 2816
N_MOD = 9
A_HEADS, A_DK, A_DV = 4, 128, 128
A_W = A_HEADS * A_DK
B_HEADS, B_DK, B_DV = 4, 64, 128
B_QK = B_HEADS * B_DK
B_V = B_HEADS * B_DV
GATE_RANK = 16
GLA_TAU = 16.0
CHUNK = 128
C_HEADS, C_KV_HEADS, C_HEAD_DIM = 16, 4, 64
C_GROUPS = C_HEADS // C_KV_HEADS
C_Q = C_HEADS * C_HEAD_DIM
C_KV = C_KV_HEADS * C_HEAD_DIM
WINDOW = 128
ROPE_FREQS = C_HEAD_DIM // 4
ROPE_BASE = 10000.0
DEPTH = 2
ALPHA = (2.0 * DEPTH) ** 0.25
LN_EPS = 1e-5
RMS_EPS = 1e-6

TP = BATCH * SEQ
TS = DEC_BATCH * DEC_SEQ
T = TP + TS
N_SEG = 1 + DEC_BATCH

LANES = 128
HALF = LANES // 2
FFN_TM = 1024
FFN_TM_SPLIT = 512
FFN_SUB = 512
PROJ_TM = 2 * SEQ
INPROJ_TM = 512
PROJ_SUB = 256
OUT_TM = 1024
VMEM_LIMIT = 60 * 1024 * 1024

F32 = jnp.float32
BF16 = jnp.bfloat16


def _dot(a, b):
    return jnp.dot(a, b, preferred_element_type=F32)


def _dot_nt(a, b):
    return lax.dot_general(a, b, (((1,), (1,)), ((), ())), preferred_element_type=F32)


def _dot_tn(a, b):
    return lax.dot_general(a, b, (((0,), (0,)), ((), ())), preferred_element_type=F32)


def _silu(x):
    return x * jax.nn.sigmoid(x)


def _layer_norm(z, g, b):
    mu = jnp.mean(z, axis=-1, keepdims=True)
    zc = z - mu
    var = jnp.mean(zc * zc, axis=-1, keepdims=True)
    return zc * lax.rsqrt(var + LN_EPS) * g + b


def _seg_of_tile(i, tm):
    n_p = TP // tm
    n_s = DEC_SEQ // tm
    return jnp.where(i < n_p, 0, 1 + lax.div(jnp.maximum(i - n_p, 0), n_s))


def _params(n_axes):
    return pltpu.CompilerParams(dimension_semantics=("arbitrary",) * n_axes, vmem_limit_bytes=VMEM_LIMIT)


def _resident(shape):
    nd = len(shape)
    return pl.BlockSpec(shape, lambda *_: (0,) * nd, pipeline_mode=pl.Buffered(1))


def _resident_slice(shape, lead):
    block = (None,) * len(lead) + tuple(shape)
    return pl.BlockSpec(block, lambda *_: tuple(lead) + (0,) * len(shape), pipeline_mode=pl.Buffered(1))


def _mod_specs(layer, cols, tm):
    return [pl.BlockSpec((None, None, 1, D), functools.partial(
        lambda i, c: (layer, _seg_of_tile(i, tm), 0, c), c=c)) for c in cols]


def _ln_specs(layer, idx):
    return [_resident_slice((1, D), (layer, idx))] * 2


BF16_SUBLANES = 16


def _cast_plan(ws, lead, n_steps, step_of):
    in_specs, out_specs, out_shapes = [], [], []
    for w in ws:
        rows, cols = w.shape[len(lead):]
        blk = next(b for b in range(BF16_SUBLANES, rows + 1, BF16_SUBLANES)
                   if rows % b == 0 and rows // b <= n_steps)
        last = rows // blk - 1
        in_specs.append(pl.BlockSpec((None,) * len(lead) + (blk, cols), functools.partial(
            lambda *g, last: tuple(lead) + (jnp.minimum(step_of(*g), last), 0), last=last)))
        out_specs.append(pl.BlockSpec((blk, cols), functools.partial(
            lambda *g, last: (jnp.minimum(step_of(*g), last), 0), last=last)))
        out_shapes.append(jax.ShapeDtypeStruct((rows, cols), BF16))
    return in_specs, out_specs, out_shapes


def _hosting_casts(body, n_in, n_out, n_cast):
    def hosted(*refs, **kw):
        ins, refs = refs[:n_in], refs[n_in:]
        cast_in, refs = refs[:n_cast], refs[n_cast:]
        outs, refs = refs[:n_out], refs[n_out:]
        cast_out, scratch = refs[:n_cast], refs[n_cast:]
        for src, dst in zip(cast_in, cast_out):
            dst[...] = src[...].astype(BF16)
        body(*ins, *outs, *scratch, **kw)
    return hosted


def _mod_body(c_ref, w_ref, b_ref, o_ref):
    c = c_ref[...]
    s = _silu(c).astype(BF16)
    o_ref[0] = _dot(s, w_ref[0].astype(BF16)) + b_ref[0]


def _mod_vectors(cs, w_mod, b_mod):
    tn = 1536
    n = N_MOD * D
    return pl.pallas_call(
        _mod_body,
        grid=(DEPTH, n // tn),
        in_specs=[
            pl.BlockSpec((8, D), lambda l, j: (0, 0)),
            pl.BlockSpec((1, D, tn), lambda l, j: (l, 0, j)),
            pl.BlockSpec((1, 1, tn), lambda l, j: (l, 0, j)),
        ],
        out_specs=pl.BlockSpec((1, 8, tn), lambda l, j: (l, 0, j)),
        out_shape=jax.ShapeDtypeStruct((DEPTH, 8, n), F32),
        compiler_params=_params(2),
        name="mod_vectors",
    )(cs, w_mod, b_mod.reshape(DEPTH, 1, n))


def _ffn_body(*refs, n_x, n_o, tm):
    x_refs = refs[:n_x]
    shift_ref, scale_ref, gate_ref, w1_ref, w3_ref, w2_ref, g_ref, b_ref = refs[n_x:n_x + 8]
    o_refs = refs[n_x + 8:]

    def compute(x_ref, o_ref):
        shift, scale, gate = shift_ref[...], scale_ref[...], gate_ref[...]
        for r in range(0, tm, FFN_SUB):
            rows = slice(r, r + FFN_SUB)
            x = x_ref[rows, :]
            h = (x * (1.0 + scale) + shift).astype(BF16)
            a = _dot(h, w1_ref[...])
            b = _dot(h, w3_ref[...])
            g = (_silu(a) * b).astype(BF16)
            y = _dot(g, w2_ref[...])
            z = ALPHA * x + (0.5 * gate) * y
            o_ref[rows, :] = _layer_norm(z, g_ref[...], b_ref[...])

    if n_x == 1 and n_o == 1:
        compute(x_refs[0], o_refs[0])
    else:
        in_prompt = pl.program_id(0) < TP // tm
        pl.when(in_prompt)(lambda: compute(x_refs[0], o_refs[0]))
        pl.when(jnp.logical_not(in_prompt))(lambda: compute(x_refs[-1], o_refs[-1]))


def _group_specs(split, tm, width=D):
    if not split:
        return [pl.BlockSpec((tm, width), lambda i: (i, 0))]
    n_p = TP // tm
    return [pl.BlockSpec((tm, width), lambda i: (jnp.minimum(i, n_p - 1), 0)),
            pl.BlockSpec((tm, width), lambda i: (jnp.maximum(i - n_p, 0), 0))]


def _ffn_sublayer(xs, mod, w1, w3, w2, ln_g, ln_b, layer, sub, split_out=False, casts=None):
    n_x, n_o = len(xs), 2 if split_out else 1
    tm = FFN_TM if n_x == n_o == 1 else FFN_TM_SPLIT
    out_shape = ([jax.ShapeDtypeStruct((TP, D), F32), jax.ShapeDtypeStruct((TS, D), F32)] if split_out
                 else [jax.ShapeDtypeStruct((T, D), F32)])
    mod_lo = 6 * sub
    body = functools.partial(_ffn_body, n_x=n_x, n_o=n_o, tm=tm)
    in_specs = _group_specs(n_x == 2, tm) + _mod_specs(layer, (mod_lo, mod_lo + 1, mod_lo + 2), tm) + [
        _resident((D, D_FF)),
        _resident((D, D_FF)),
        _resident((D_FF, D)),
    ] + _ln_specs(layer, 2 * sub)
    out_specs = _group_specs(split_out, tm)
    args = [*xs, mod, mod, mod, w1, w3, w2, ln_g, ln_b]
    if casts is not None:
        ws, lead = casts
        c_in, c_out, c_shapes = _cast_plan(ws, lead, T // tm, lambda i: i)
        body = _hosting_casts(body, len(in_specs), len(out_specs), len(ws))
        in_specs, out_specs, out_shape = in_specs + c_in, out_specs + c_out, out_shape + c_shapes
        args = args + list(ws)
    return pl.pallas_call(
        body,
        grid=(T // tm,),
        in_specs=in_specs,
        out_specs=out_specs,
        out_shape=out_shape,
        compiler_params=_params(1),
        name="ffn_sublayer",
    )(*args)


PF_AQ, PF_FF, PF_FB, PF_AG = 0, 512, 1024, 1536
PF_BQ, PF_BK, PF_BG, PF_LAF, PF_LAB = 2048, 2304, 2560, 3072, 3328
PF_W = 3584
PB_AV, PB_BV = 0, 512
PB_W = 1024
WM_AQ, WM_FF, WM_FB, WM_AG, WM_BQ, WM_BK, WM_BG, WM_AI, WM_BV = 0, 512, 1024, 1536, 2048, 2304, 2560, 3072, 3584
WM_W = 4096


def _log_sigmoid(x):
    return jnp.minimum(x, 0.0) - jnp.log(1.0 + jnp.exp(-jnp.abs(x)))


def _inproj_body(x_ref, shift_ref, scale_ref, w_ref, wz_ref, gu_ref, gb_ref, lb_ref, pf_ref, pb_ref, *, layer_e):
    def lower_bound(d):
        l = lb_ref[d]
        e = jnp.exp(l - jnp.max(l, axis=0, keepdims=True))
        sm = e / jnp.sum(e, axis=0, keepdims=True)
        return jnp.sum(sm[:layer_e + 1], axis=0, keepdims=True)

    lbs = [lower_bound(0), lower_bound(1)]
    for r in range(0, x_ref.shape[0], PROJ_SUB):
        rows = slice(r, r + PROJ_SUB)
        h = (x_ref[rows, :] * (1.0 + scale_ref[...]) + shift_ref[...]).astype(BF16)

        def proj(off, width):
            return _dot(h, w_ref[:, off:off + width])

        pf_ref[rows, PF_AQ:PF_AQ + A_W] = proj(WM_AQ, A_W)
        for lb, wm, pf in ((lbs[0], WM_FF, PF_FF), (lbs[1], WM_FB, PF_FB)):
            pf_ref[rows, pf:pf + A_W] = lb + (1.0 - lb) * jax.nn.sigmoid(proj(wm, A_W))
        pf_ref[rows, PF_AG:PF_AG + A_W] = _silu(proj(WM_AG, A_W))
        pf_ref[rows, PF_BQ:PF_BQ + B_QK] = proj(WM_BQ, B_QK) * (B_DK ** -0.5)
        pf_ref[rows, PF_BK:PF_BK + B_QK] = proj(WM_BK, B_QK)
        pf_ref[rows, PF_BG:PF_BG + B_V] = _silu(proj(WM_BG, B_V))
        pb_ref[rows, PB_AV:PB_AV + A_W] = _silu(proj(WM_AI, A_W)).astype(BF16)
        pb_ref[rows, PB_BV:PB_BV + B_V] = proj(WM_BV, B_V).astype(BF16)
        z = _dot(h, wz_ref[...]).astype(BF16)
        pre = _dot(z, gu_ref[...]) + gb_ref[...]
        pf_ref[rows, PF_LAF:PF_LAF + 2 * B_QK] = _log_sigmoid(pre) * (1.0 / GLA_TAU)


def _inproj(x, mod, wmain, wz, gup, gb, hgrn_lb, layer, layer_e, tm=INPROJ_TM):
    n_l = hgrn_lb.shape[1]
    return pl.pallas_call(
        functools.partial(_inproj_body, layer_e=layer_e),
        grid=(T // tm,),
        in_specs=[pl.BlockSpec((tm, D), lambda i: (i, 0))] + _mod_specs(layer, (3, 4), tm) + [
            _resident((D, WM_W)),
            _resident((D, LANES)),
            _resident((LANES, 2 * B_QK)),
            _resident((1, 2 * B_QK)),
            _resident((2, n_l, A_W)),
        ],
        out_specs=[
            pl.BlockSpec((tm, PF_W), lambda i: (i, 0)),
            pl.BlockSpec((tm, PB_W), lambda i: (i, 0)),
        ],
        out_shape=[jax.ShapeDtypeStruct((T, PF_W), F32), jax.ShapeDtypeStruct((T, PB_W), BF16)],
        compiler_params=_params(1),
        name="ab_inproj",
    )(x, mod, mod, wmain, wz, gup, gb, hgrn_lb)


SCAN_PROMPT_SEQS = 4
SCAN_UNROLL = 8


def _prefix_rows(x):
    row = lax.broadcasted_iota(jnp.int32, x.shape, 0)
    s = 1
    while s < x.shape[0]:
        x = x + jnp.where(row >= s, pltpu.roll(x, s, axis=0), 0.0)
        s *= 2
    return x


def _scan_body(*refs, seq_len, seqs, pair, has_s0, emit_state):
    n = seq_len // CHUNK
    n_all = seqs * n
    nh = 2 if pair else 1
    it = iter(refs)
    q_ref = next(it)
    if pair:
        k_ref, laf_ref, lab_ref = next(it), next(it), next(it)
    else:
        ff_ref, fb_ref = next(it), next(it)
    g_ref, v_ref, nw_ref = next(it), next(it), next(it)
    s0_ref = next(it) if has_s0 else None
    o_ref = next(it)
    st_ref = next(it) if emit_state else None
    qd_s, oi_s, kv_s, dec_s, sb_s = it

    row = lax.broadcasted_iota(jnp.int32, (CHUNK, CHUNK), 0)
    col = lax.broadcasted_iota(jnp.int32, (CHUNK, CHUNK), 1)
    tril = row >= col
    triu = row <= col
    lane = lax.broadcasted_iota(jnp.int32, (1, LANES), 1)
    lane2 = lax.broadcasted_iota(jnp.int32, (1, 2 * LANES), 1)
    if pair:
        masks = [lane < HALF, lane >= HALF]
        masks2 = [(lane2 % LANES) < HALF, (lane2 % LANES) >= HALF]
    else:
        masks, masks2 = [None], [None]

    def pick(mask, x):
        return x if mask is None else jnp.where(mask, x, jnp.zeros_like(x))

    def rows_of(c):
        return pl.ds(pl.multiple_of(c * CHUNK, CHUNK), CHUNK)

    def loop(body):
        if n_all <= SCAN_UNROLL:
            for c in range(n_all):
                body(c)
        else:
            def fbody(i, carry):
                for u in range(SCAN_UNROLL):
                    body(i * SCAN_UNROLL + u)
                return carry
            lax.fori_loop(0, n_all // SCAN_UNROLL, fbody, 0)

    def phase1(c):
        rows = rows_of(c)
        q = q_ref[rows, :]
        if pair:
            k_f = k_b = k_ref[rows, :]
            la_f, la_b = laf_ref[rows, :], lab_ref[rows, :]
        else:
            f_f, f_b = ff_ref[rows, :], fb_ref[rows, :]
            k_f, k_b = 1.0 - f_f, 1.0 - f_b
            la_f, la_b = jnp.log(f_f), jnp.log(f_b)
        cs = _prefix_rows(jnp.concatenate([la_f, la_b], axis=1))
        cf, cbi = cs[:, :LANES], cs[:, LANES:]
        tot_f, tot_b = cf[CHUNK - 1:CHUNK, :], cbi[CHUNK - 1:CHUNK, :]
        rb = tot_b - cbi + la_b
        ref_f, ref_b = cf[CHUNK // 2 - 1:CHUNK // 2, :], rb[CHUNK // 2:CHUNK // 2 + 1, :]
        qtf = q * jnp.exp(cf - ref_f)
        qtb = q * jnp.exp(rb - ref_b)
        ktf = k_f * jnp.exp(ref_f - cf)
        ktb = k_b * jnp.exp(ref_b - rb)
        qd = jnp.concatenate([qtf * jnp.exp(ref_f), qtb * jnp.exp(ref_b)], axis=1).astype(BF16)
        ku = jnp.concatenate([ktf * jnp.exp(tot_f - ref_f), ktb * jnp.exp(tot_b - ref_b)], axis=1).astype(BF16)
        qd_s[rows, :] = qd
        qt = jnp.concatenate([qtf, qtb], axis=0).astype(BF16)
        kt = jnp.concatenate([ktf, ktb], axis=0).astype(BF16)
        kv = None
        for hh in range(nh):
            v = v_ref[rows, hh * LANES:(hh + 1) * LANES]
            sc = _dot_nt(pick(masks[hh], qt), kt)
            att = jnp.where(tril, sc[:CHUNK, :CHUNK], 0.0) + jnp.where(triu, sc[CHUNK:, CHUNK:], 0.0)
            oi_s[rows, hh * LANES:(hh + 1) * LANES] = _dot(att.astype(BF16), v)
            kv_h = _dot_tn(v, ku)
            kv = kv_h if kv is None else jnp.where(masks2[0], kv, kv_h)
        kv_s[c] = kv
        dec_s[c] = jnp.exp(jnp.concatenate([tot_f, tot_b], axis=1))

    loop(phase1)

    def recurrence(sq, d, reverse):
        cols = slice(d * LANES, (d + 1) * LANES)
        c0 = sq * n
        st0 = s0_ref[sq, d, 0].T if has_s0 else jnp.zeros((LANES, LANES), F32)

        def step(c, st):
            sb_s[c, :, cols] = st.astype(BF16)
            return st * dec_s[c, :, cols] + kv_s[c, :, cols]

        if n <= 8:
            st = st0
            for c in (range(n - 1, -1, -1) if reverse else range(n)):
                st = step(c0 + c, st)
        else:
            st = lax.fori_loop(0, n, lambda i, st: step(c0 + (n - 1 - i if reverse else i), st), st0)
        if emit_state:
            st_ref[sq, d, 0] = st.T

    for sq in range(seqs):
        recurrence(sq, 0, False)
        recurrence(sq, 1, True)

    nw = nw_ref[...]

    def phase2(c):
        rows = rows_of(c)
        qcat = qd_s[rows, :]
        scat = sb_s[c]
        for hh in range(nh):
            cols = slice(hh * LANES, (hh + 1) * LANES)
            o = oi_s[rows, cols] + _dot_nt(pick(masks2[hh], qcat), scat)
            o = o * lax.rsqrt(jnp.mean(o * o, axis=-1, keepdims=True) + RMS_EPS) * nw
            o_ref[rows, cols] = (o * g_ref[rows, cols]).astype(BF16)

    loop(phase2)


def _scan(pf, pb, norm_w, s0, *, prompt, pair):
    seq_len = SEQ if prompt else DEC_SEQ
    nseq = BATCH if prompt else DEC_BATCH
    seqs = SCAN_PROMPT_SEQS if prompt else 1
    rows = seqs * seq_len
    row_off = 0 if prompt else TP // rows
    units = B_HEADS // 2 if pair else A_HEADS
    nh = 2 if pair else 1
    n_all = rows // CHUNK
    has_s0 = s0 is not None
    emit_state = prompt

    def colspec(off, width=LANES):
        base = off // width
        return pl.BlockSpec((rows, width), lambda s, u: (s + row_off, base + u))

    if pair:
        in_specs = [colspec(PF_BQ), colspec(PF_BK), colspec(PF_LAF), colspec(PF_LAB),
                    colspec(PF_BG, 2 * LANES), colspec(PB_BV, 2 * LANES)]
        args = [pf, pf, pf, pf, pf, pb]
    else:
        in_specs = [colspec(PF_AQ), colspec(PF_FF), colspec(PF_FB), colspec(PF_AG), colspec(PB_AV)]
        args = [pf, pf, pf, pf, pb]
    in_specs.append(pl.BlockSpec((1, LANES), lambda s, u: (0, 0)))
    args.append(norm_w.reshape(1, LANES))
    state_spec = pl.BlockSpec((seqs, 2, 1, LANES, LANES), lambda s, u: (s, 0, u, 0, 0))
    if has_s0:
        in_specs.append(state_spec)
        args.append(s0)
    out_specs = [pl.BlockSpec((rows, nh * LANES), lambda s, u: (s, u))]
    out_shape = [jax.ShapeDtypeStruct((nseq * seq_len, units * nh * LANES), BF16)]
    if emit_state:
        out_specs.append(state_spec)
        out_shape.append(jax.ShapeDtypeStruct((nseq, 2, units, LANES, LANES), F32))
    scratch = [
        pltpu.VMEM((rows, 2 * LANES), BF16),
        pltpu.VMEM((rows, nh * LANES), F32),
        pltpu.VMEM((n_all, LANES, 2 * LANES), F32),
        pltpu.VMEM((n_all, 1, 2 * LANES), F32),
        pltpu.VMEM((n_all, LANES, 2 * LANES), BF16),
    ]
    return pl.pallas_call(
        functools.partial(_scan_body, seq_len=seq_len, seqs=seqs, pair=pair, has_s0=has_s0,
                          emit_state=emit_state),
        grid=(nseq // seqs, units),
        in_specs=in_specs,
        out_specs=out_specs,
        out_shape=out_shape,
        scratch_shapes=scratch,
        compiler_params=_params(2),
        name=f"scan_{'p' if prompt else 's'}_{'gla' if pair else 'hgrn'}",
    )(*args)


def _outproj_body(*refs, n_lhs, tm):
    lhs = refs[:2 * n_lhs]
    ws = refs[2 * n_lhs:3 * n_lhs]
    x_ref, m_ref, g_ref, b_ref, o_ref = refs[3 * n_lhs:]

    def compute(group):
        for r in range(0, tm, PROJ_SUB):
            rows = slice(r, r + PROJ_SUB)
            y = _dot(lhs[group][rows, :], ws[0][...])
            for j in range(1, n_lhs):
                y = y + _dot(lhs[2 * j + group][rows, :], ws[j][...])
            z = ALPHA * x_ref[rows, :] + m_ref[...] * y
            o_ref[rows, :] = _layer_norm(z, g_ref[...], b_ref[...])

    in_prompt = pl.program_id(0) < TP // tm
    pl.when(in_prompt)(lambda: compute(0))
    pl.when(jnp.logical_not(in_prompt))(lambda: compute(1))


def _outproj(lhs, ws, x, mod, ln_g, ln_b, layer, tm=OUT_TM):
    n_lhs = len(lhs)
    in_specs, args = [], []
    for a_p, a_s in lhs:
        in_specs += _group_specs(True, tm, a_p.shape[1])
        args += [a_p, a_s]
    in_specs += [_resident(w.shape) for w in ws]
    in_specs += [pl.BlockSpec((tm, D), lambda i: (i, 0))] + _mod_specs(layer, (5,), tm) + _ln_specs(layer, 1)
    return pl.pallas_call(
        functools.partial(_outproj_body, n_lhs=n_lhs, tm=tm),
        grid=(T // tm,),
        in_specs=in_specs,
        out_specs=pl.BlockSpec((tm, D), lambda i: (i, 0)),
        out_shape=jax.ShapeDtypeStruct((T, D), F32),
        compiler_params=_params(1),
        name="mixer_outproj",
    )(*args, *ws, x, mod, ln_g, ln_b)


def _rope_partner(x):
    lane = lax.broadcasted_iota(jnp.int32, x.shape, 1)
    first_half = (lane % (2 * ROPE_FREQS)) < ROPE_FREQS
    return jnp.where(first_half, pltpu.roll(x, LANES - ROPE_FREQS, axis=1), pltpu.roll(x, ROPE_FREQS, axis=1))


def _qkv_body(x_ref, shift_ref, scale_ref, w_ref, wkvt_ref, cos_ref, sin_ref,
              q_ref, k_ref, v_ref, kt_ref, vt_ref):
    qscale = C_HEAD_DIM ** -0.5
    in_prompt = pl.program_id(0) < TP // PROJ_TM

    def modulated(rows):
        return (x_ref[rows, :] * (1.0 + scale_ref[...]) + shift_ref[...]).astype(BF16)

    @pl.when(in_prompt)
    def _():
        for sq in range(PROJ_TM // SEQ):
            rows = slice(sq * SEQ, (sq + 1) * SEQ)
            h = modulated(rows)
            q_ref[rows, :] = (_dot(h, w_ref[:, :C_Q]) * qscale).astype(BF16)
            v_ref[rows, :] = _dot(h, w_ref[:, C_Q + C_KV:]).astype(BF16)
            kt_ref[sq] = _dot_nt(wkvt_ref[:C_KV, :], h)
            vt_ref[sq] = _dot_nt(wkvt_ref[C_KV:, :], h)

    @pl.when(jnp.logical_not(in_prompt))
    def _():
        for sq in range(PROJ_TM // SEQ):
            rows = slice(sq * SEQ, (sq + 1) * SEQ)
            h = modulated(rows)
            cos, sin = cos_ref[rows, :], sin_ref[rows, :]
            zq = _dot(h, w_ref[:, :C_Q])
            zk = _dot(h, w_ref[:, C_Q:C_Q + C_KV])
            v_ref[rows, :] = _dot(h, w_ref[:, C_Q + C_KV:]).astype(BF16)

            def rope(z, cos=cos, sin=sin):
                return z * cos + _rope_partner(z) * sin

            for j in range(C_Q // LANES):
                cols = slice(j * LANES, (j + 1) * LANES)
                q_ref[rows, cols] = (rope(zq[:, cols]) * qscale).astype(BF16)
            for j in range(C_KV // LANES):
                cols = slice(j * LANES, (j + 1) * LANES)
                k_ref[rows, cols] = rope(zk[:, cols]).astype(BF16)


def _qkv(x, mod, w, wkvt, cos, sin, layer):
    tm = PROJ_TM
    n_p = TP // tm
    lat = lambda i: (jnp.maximum(i - n_p, 0), 0)
    ctx = lambda i: (jnp.minimum(i, n_p - 1), 0, 0)
    return pl.pallas_call(
        _qkv_body,
        grid=(T // tm,),
        in_specs=[pl.BlockSpec((tm, D), lambda i: (i, 0))] + _mod_specs(layer, (3, 4), tm) + [
            _resident((D, C_Q + 2 * C_KV)),
            _resident((2 * C_KV, D)),
            pl.BlockSpec((tm, LANES), lat),
            pl.BlockSpec((tm, LANES), lat),
        ],
        out_specs=[
            pl.BlockSpec((tm, C_Q), lambda i: (i, 0)),
            pl.BlockSpec((tm, C_KV), lat),
            pl.BlockSpec((tm, C_KV), lambda i: (i, 0)),
            pl.BlockSpec((tm // SEQ, C_KV, SEQ), ctx),
            pl.BlockSpec((tm // SEQ, C_KV, SEQ), ctx),
        ],
        out_shape=[jax.ShapeDtypeStruct((T, C_Q), BF16), jax.ShapeDtypeStruct((TS, C_KV), BF16),
                   jax.ShapeDtypeStruct((T, C_KV), BF16),
                   jax.ShapeDtypeStruct((BATCH, C_KV, SEQ), F32), jax.ShapeDtypeStruct((BATCH, C_KV, SEQ), F32)],
        compiler_params=_params(1),
        name="c_qkv",
    )(x, mod, mod, w, wkvt, cos, sin)


def _rope_tables():
    t = np.arange(DEC_SEQ)
    pos = np.stack([t // GRID_W, t % GRID_W], axis=1).astype(np.float32)
    inv = (ROPE_BASE ** (-np.arange(ROPE_FREQS, dtype=np.float32) / ROPE_FREQS)).astype(np.float32)
    d = np.arange(C_HEAD_DIM)
    axis = d // (2 * ROPE_FREQS)
    ang = pos[:, axis] * inv[d % ROPE_FREQS][None, :]
    sign = np.where((d % (2 * ROPE_FREQS)) < ROPE_FREQS, -1.0, 1.0)[None, :]
    cos_h, sin_h = np.cos(ang), np.sin(ang) * sign
    reps = LANES // C_HEAD_DIM
    cos = np.tile(np.tile(cos_h, (1, reps)), (DEC_BATCH, 1)).astype(np.float32)
    sin = np.tile(np.tile(sin_h, (1, reps)), (DEC_BATCH, 1)).astype(np.float32)
    return jnp.asarray(cos), jnp.asarray(sin)


def _dup_head(blk, half):
    lane = lax.broadcasted_iota(jnp.int32, blk.shape, 1)
    keep = (lane >= HALF).astype(jnp.int32) == half
    return jnp.where(keep, blk, pltpu.roll(blk, HALF, axis=1))


def _pair_values(v, half):
    lo = lax.broadcasted_iota(jnp.int32, v.shape, 1) < HALF
    v2 = _dup_head(v.astype(F32), half)
    top = jnp.concatenate([jnp.where(lo, v2, 0.0), jnp.where(lo, 1.0, 0.0)], axis=1)
    bot = jnp.concatenate([jnp.where(lo, 0.0, v2), jnp.where(lo, 0.0, 1.0)], axis=1)
    return top.astype(BF16), bot.astype(BF16)


def _softmax_weights(scores, sink):
    mx = sink
    for s in scores:
        mx = jnp.maximum(mx, jnp.max(s, axis=-1, keepdims=True))
    return [jnp.exp(s - mx).astype(BF16) for s in scores], jnp.exp(sink - mx)


def _pair_sinks(sink_ref, jb):
    return (sink_ref[:, jb * LANES:jb * LANES + 1], sink_ref[:, jb * LANES + HALF:jb * LANES + HALF + 1])


KV_PER_BLOCK = LANES // C_HEAD_DIM
TILE_KV = C_KV_HEADS
TILE_KVW = TILE_KV * C_HEAD_DIM
CTX_TILES = BATCH * C_KV_HEADS // TILE_KV
CTX_BLOCKS = TILE_KV * C_GROUPS // 2
PIPE_LAG = 2


def _kv_block(ref, kv):
    lb = kv // KV_PER_BLOCK
    return ref[:, lb * LANES:(lb + 1) * LANES], kv % KV_PER_BLOCK


def _ctx_tile(g, lag):
    t = jnp.clip(g - lag, 0, CTX_TILES - 1)
    n = C_KV_HEADS // TILE_KV
    return t // n, t % n


def _ctx_attn_body(q_ref, kt_ref, v_ref, sink_ref, o_ref, s_s, e_s, st_s):
    g = pl.program_id(0)
    lo = lax.broadcasted_iota(jnp.int32, (1, LANES), 1) < HALF
    pairs_per_kv = C_GROUPS // 2

    @pl.when(g == 0)
    def _():
        s_s[...] = jnp.zeros_like(s_s)
        e_s[...] = jnp.zeros_like(e_s)
        st_s[...] = jnp.ones_like(st_s)

    def stages(cur, prev):
        for kv in range(TILE_KV):
            v_top, v_bot = _pair_values(*_kv_block(v_ref, kv))
            for jb in range(kv * pairs_per_kv, (kv + 1) * pairs_per_kv):
                od = _dot(e_s[cur, 2 * jb], v_top) + _dot(e_s[cur, 2 * jb + 1], v_bot)
                o_ref[:, jb * LANES:(jb + 1) * LANES] = (od[:, :LANES] / (od[:, LANES:] + st_s[cur, jb])).astype(BF16)
        for jb in range(CTX_BLOCKS):
            terms = []
            for hh, sink in enumerate(_pair_sinks(sink_ref, jb)):
                (e,), term = _softmax_weights([s_s[prev, 2 * jb + hh]], sink)
                e_s[prev, 2 * jb + hh] = e
                terms.append(term)
            st_s[prev, jb] = jnp.where(lo, terms[0], terms[1])
        for kv in range(TILE_KV):
            kt = kt_ref[kv * C_HEAD_DIM:(kv + 1) * C_HEAD_DIM, :]
            zero = jnp.zeros_like(kt)
            kt_a = jnp.concatenate([kt, zero], axis=0).astype(BF16)
            kt_b = jnp.concatenate([zero, kt], axis=0).astype(BF16)
            for jb in range(kv * pairs_per_kv, (kv + 1) * pairs_per_kv):
                q = q_ref[:, jb * LANES:(jb + 1) * LANES]
                s_s[cur, 2 * jb] = _dot(q, kt_a)
                s_s[cur, 2 * jb + 1] = _dot(q, kt_b)

    pl.when(g % 2 == 0)(lambda: stages(0, 1))
    pl.when(g % 2 == 1)(lambda: stages(1, 0))


def _ctx_attn(q, kt, v, sink_cols):
    gw = CTX_BLOCKS * LANES

    def at(lag, fn):
        return lambda g: fn(*_ctx_tile(g, lag))

    return pl.pallas_call(
        _ctx_attn_body,
        grid=(CTX_TILES + PIPE_LAG,),
        in_specs=[
            pl.BlockSpec((SEQ, gw), at(0, lambda b, p: (b, p))),
            pl.BlockSpec((None, TILE_KVW, SEQ), at(0, lambda b, p: (b, p, 0))),
            pl.BlockSpec((SEQ, TILE_KVW), at(2, lambda b, p: (b, p))),
            pl.BlockSpec((1, gw), at(1, lambda b, p: (0, p))),
        ],
        out_specs=pl.BlockSpec((SEQ, gw), at(2, lambda b, p: (b, p))),
        out_shape=jax.ShapeDtypeStruct((TP, C_Q), BF16),
        scratch_shapes=[
            pltpu.VMEM((2, 2 * CTX_BLOCKS, SEQ, SEQ), F32),
            pltpu.VMEM((2, 2 * CTX_BLOCKS, SEQ, SEQ), BF16),
            pltpu.VMEM((2, CTX_BLOCKS, SEQ, LANES), F32),
        ],
        compiler_params=_params(1),
        name="c_attn_ctx",
    )(q, kt, v, sink_cols)


LAT_TQ = 128
LAT_WIN = LAT_TQ + 2 * WINDOW
LAT_NQ = DEC_SEQ // LAT_TQ
LAT_KVB = C_KV_HEADS // TILE_KV
LAT_TILES = DEC_BATCH * LAT_KVB * LAT_NQ


def _lat_tile(g, lag):
    t = jnp.clip(g - lag, 0, LAT_TILES - 1)
    bp = t // LAT_NQ
    return bp // LAT_KVB, bp % LAT_KVB, t % LAT_NQ


def _lat_window(j):
    return pl.multiple_of(jnp.clip(j * LAT_TQ - WINDOW, 0, DEC_SEQ - LAT_WIN), LANES)


def _lat_attn_body(q_ref, k_ref, v_ref, ck_ref, cv_ref, sink_ref, o_ref,
                   ka_s, kb_s, cka_s, ckb_s, va_s, vb_s, cva_s, cvb_s, sl_s, sc_s, el_s, ec_s, st_s):
    g = pl.program_id(0)
    lo = lax.broadcasted_iota(jnp.int32, (1, LANES), 1) < HALF
    pairs_per_kv = C_GROUPS // 2
    b_a, p_a, j_a = _lat_tile(g, 0)
    b_c, p_c, j_c = _lat_tile(g, 2)
    v_slot_a = (b_a * LAT_KVB + p_a) % 2
    v_slot_c = (b_c * LAT_KVB + p_c) % 2

    @pl.when(g == 0)
    def _():
        sl_s[...] = jnp.zeros_like(sl_s)
        sc_s[...] = jnp.zeros_like(sc_s)
        el_s[...] = jnp.zeros_like(el_s)
        ec_s[...] = jnp.zeros_like(ec_s)
        st_s[...] = jnp.ones_like(st_s)

    @pl.when(jnp.logical_and(j_a == 0, g < LAT_TILES))
    def _():
        for kv in range(TILE_KV):
            for src, a_s, b_s in ((k_ref, ka_s, kb_s), (ck_ref, cka_s, ckb_s)):
                blk, kh = _kv_block(src, kv)
                lo2 = lax.broadcasted_iota(jnp.int32, blk.shape, 1) < HALF
                k2 = _dup_head(blk.astype(F32), kh)
                a_s[kv] = jnp.where(lo2, k2, 0.0).astype(BF16)
                b_s[kv] = jnp.where(lo2, 0.0, k2).astype(BF16)
            va_s[v_slot_a, kv], vb_s[v_slot_a, kv] = _pair_values(*_kv_block(v_ref, kv))
            cva_s[v_slot_a, kv], cvb_s[v_slot_a, kv] = _pair_values(*_kv_block(cv_ref, kv))

    def stages(cur, prev):
        win_c = pl.ds(_lat_window(j_c), LAT_WIN)
        for kh in range(TILE_KV):
            v_loc = (va_s[v_slot_c, kh, win_c, :], vb_s[v_slot_c, kh, win_c, :])
            v_ctx = (cva_s[v_slot_c, kh], cvb_s[v_slot_c, kh])
            for jb in range(kh * pairs_per_kv, (kh + 1) * pairs_per_kv):
                od = None
                for hh in range(2):
                    part = _dot(el_s[cur, 2 * jb + hh], v_loc[hh]) + _dot(ec_s[cur, 2 * jb + hh], v_ctx[hh])
                    od = part if od is None else od + part
                o_ref[:, jb * LANES:(jb + 1) * LANES] = (od[:, :LANES] / (od[:, LANES:] + st_s[cur, jb])).astype(BF16)
        for jb in range(CTX_BLOCKS):
            terms = []
            for hh, sink in enumerate(_pair_sinks(sink_ref, jb)):
                (e_loc, e_ctx), term = _softmax_weights([sl_s[prev, 2 * jb + hh], sc_s[prev, 2 * jb + hh]], sink)
                el_s[prev, 2 * jb + hh] = e_loc
                ec_s[prev, 2 * jb + hh] = e_ctx
                terms.append(term)
            st_s[prev, jb] = jnp.where(lo, terms[0], terms[1])
        start = _lat_window(j_a)
        win = pl.ds(start, LAT_WIN)
        qpos = j_a * LAT_TQ + lax.broadcasted_iota(jnp.int32, (LAT_TQ, LAT_WIN), 0)
        kpos = start + lax.broadcasted_iota(jnp.int32, (LAT_TQ, LAT_WIN), 1)
        band = jnp.abs(qpos - kpos) <= WINDOW
        for kh in range(TILE_KV):
            for jb in range(kh * pairs_per_kv, (kh + 1) * pairs_per_kv):
                q = q_ref[:, jb * LANES:(jb + 1) * LANES]
                for hh, (kl, kc) in enumerate(((ka_s, cka_s), (kb_s, ckb_s))):
                    sl_s[cur, 2 * jb + hh] = jnp.where(band, _dot_nt(q, kl[kh, win, :]), -jnp.inf)
                    sc_s[cur, 2 * jb + hh] = _dot_nt(q, kc[kh])

    pl.when(g % 2 == 0)(lambda: stages(0, 1))
    pl.when(g % 2 == 1)(lambda: stages(1, 0))


def _lat_attn(q, k, v, ck, cv, sink_cols):
    gw = CTX_BLOCKS * LANES
    q_off = TP // LAT_TQ
    kv_off = TP // DEC_SEQ
    kvb, n_heads = TILE_KV, 2 * CTX_BLOCKS

    def at(lag, fn):
        return lambda g: fn(*_lat_tile(g, lag))

    return pl.pallas_call(
        _lat_attn_body,
        grid=(LAT_TILES + PIPE_LAG,),
        in_specs=[
            pl.BlockSpec((LAT_TQ, gw), at(0, lambda b, p, j: (q_off + b * LAT_NQ + j, p))),
            pl.BlockSpec((DEC_SEQ, TILE_KVW), at(0, lambda b, p, j: (b, p))),
            pl.BlockSpec((DEC_SEQ, TILE_KVW), at(0, lambda b, p, j: (kv_off + b, p))),
            pl.BlockSpec((PAST_LEN, TILE_KVW), at(0, lambda b, p, j: (b, p))),
            pl.BlockSpec((PAST_LEN, TILE_KVW), at(0, lambda b, p, j: (b, p))),
            pl.BlockSpec((1, gw), at(1, lambda b, p, j: (0, p))),
        ],
        out_specs=pl.BlockSpec((LAT_TQ, gw), at(2, lambda b, p, j: (b * LAT_NQ + j, p))),
        out_shape=jax.ShapeDtypeStruct((TS, C_Q), BF16),
        scratch_shapes=[
            pltpu.VMEM((kvb, DEC_SEQ, LANES), BF16), pltpu.VMEM((kvb, DEC_SEQ, LANES), BF16),
            pltpu.VMEM((kvb, PAST_LEN, LANES), BF16), pltpu.VMEM((kvb, PAST_LEN, LANES), BF16),
            pltpu.VMEM((2, kvb, DEC_SEQ, 2 * LANES), BF16), pltpu.VMEM((2, kvb, DEC_SEQ, 2 * LANES), BF16),
            pltpu.VMEM((2, kvb, PAST_LEN, 2 * LANES), BF16), pltpu.VMEM((2, kvb, PAST_LEN, 2 * LANES), BF16),
            pltpu.VMEM((2, n_heads, LAT_TQ, LAT_WIN), F32), pltpu.VMEM((2, n_heads, LAT_TQ, PAST_LEN), F32),
            pltpu.VMEM((2, n_heads, LAT_TQ, LAT_WIN), BF16), pltpu.VMEM((2, n_heads, LAT_TQ, PAST_LEN), BF16),
            pltpu.VMEM((2, CTX_BLOCKS, LAT_TQ, LANES), F32),
        ],
        compiler_params=_params(1),
        name="c_attn_latent",
    )(q, k, v, ck, cv, sink_cols)


def kernel(x_prompt, x_sample, state_hgrn, state_gla, cache_k, cache_v, c, c_ctx, w_mod, b_mod, ln_g, ln_b,
           ffn_w1, ffn_w3, ffn_w2, w_in_ab, hgrn_lb, gla_gate_up, gla_gate_b, norm_a, norm_b, w_out_ab,
           w_qkv_c, sink_c, w_out_c):
    cs = jnp.zeros((8, D), F32).at[0].set(c_ctx).at[1:1 + DEC_BATCH].set(c)
    mod = _mod_vectors(cs, w_mod, b_mod).reshape(DEPTH, 8, 1, N_MOD * D)
    ffn_ws = (ffn_w1, ffn_w3, ffn_w2)
    ln_g, ln_b = ln_g.reshape(DEPTH, 3, 1, D), ln_b.reshape(DEPTH, 3, 1, D)

    def ffn(xs, ws, layer, sub, split_out=False, cast_next=None):
        casts = None if cast_next is None else (ffn_ws, cast_next)
        return _ffn_sublayer(xs, mod, *ws, ln_g, ln_b, layer, sub, split_out=split_out, casts=casts)

    x, *ws_01 = ffn([x_prompt.reshape(TP, D), x_sample.reshape(TS, D)], [w[0, 0].astype(BF16) for w in ffn_ws],
                    0, 0, cast_next=(0, 1))
    w_in = w_in_ab[0]
    o_aq, o_ai, o_ff, o_fb, o_ag = 0, A_W, 2 * A_W, 3 * A_W, 4 * A_W
    o_bq = 5 * A_W
    o_bk, o_bv = o_bq + B_QK, o_bq + 2 * B_QK
    o_bg = o_bv + B_V
    o_z = o_bg + B_V
    order = [(o_aq, A_W), (o_ff, A_W), (o_fb, A_W), (o_ag, A_W), (o_bq, B_QK), (o_bk, B_QK), (o_bg, B_V),
             (o_ai, A_W), (o_bv, B_V)]
    wmain = jnp.concatenate([w_in[:, o:o + w] for o, w in order], axis=1).astype(BF16)
    wz = jnp.pad(w_in[:, o_z:o_z + 2 * GATE_RANK], ((0, 0), (0, LANES - 2 * GATE_RANK))).astype(BF16)
    gup = jnp.zeros((LANES, 2 * B_QK), F32)
    gup = gup.at[:GATE_RANK, :B_QK].set(gla_gate_up[0, 0]).at[GATE_RANK:2 * GATE_RANK, B_QK:].set(gla_gate_up[0, 1])
    gb = gla_gate_b[0].reshape(1, 2 * B_QK)
    pf, pb = _inproj(x, mod, wmain, wz, gup.astype(BF16), gb, hgrn_lb, 0, 0)

    s0_a = state_hgrn[:, 0]
    s0_b = state_gla[:, 0].reshape(DEC_BATCH, 2, B_HEADS // 2, LANES, B_DV)
    oa_p, st_a = _scan(pf, pb, norm_a[0], None, prompt=True, pair=False)
    ob_p, st_b = _scan(pf, pb, norm_b[0], None, prompt=True, pair=True)
    (oa_s,) = _scan(pf, pb, norm_a[0], s0_a, prompt=False, pair=False)
    (ob_s,) = _scan(pf, pb, norm_b[0], s0_b, prompt=False, pair=True)
    w_out = w_out_ab[0].astype(BF16)
    x = _outproj([(oa_p, oa_s), (ob_p, ob_s)], [w_out[:A_W], w_out[A_W:]], x, mod, ln_g, ln_b, 0)
    x, *ws_10 = ffn([x], ws_01, 0, 1, cast_next=(1, 0))
    new_hgrn = st_a.reshape(BATCH, 1, 2, A_HEADS, A_DK, A_DV)
    new_gla = st_b.reshape(BATCH, 1, 2, B_HEADS, B_DK, B_DV)

    x, *ws_11 = ffn([x], ws_10, 1, 0, cast_next=(1, 1))
    cos, sin = _rope_tables()
    w_qkv = w_qkv_c[0].astype(BF16)
    q, k, v, kt, vt = _qkv(x, mod, w_qkv, w_qkv[:, C_Q:].T, cos, sin, 1)
    sink_cols = jnp.repeat(sink_c[0], C_HEAD_DIM).reshape(1, C_Q)
    o_p = _ctx_attn(q, kt, v, sink_cols)
    ck = cache_k[:, 0].reshape(DEC_BATCH * PAST_LEN, C_KV)
    cv = cache_v[:, 0].reshape(DEC_BATCH * PAST_LEN, C_KV)
    o_s = _lat_attn(q, k, v, ck, cv, sink_cols)
    x = _outproj([(o_p, o_s)], [w_out_c[0].astype(BF16)], x, mod, ln_g, ln_b, 1)
    y_p, y_s = ffn([x], ws_11, 1, 1, split_out=True)

    def cache_layout(zt):
        return zt.reshape(BATCH, 1, C_KV_HEADS, C_HEAD_DIM, SEQ).transpose(0, 1, 4, 2, 3)

    new_k, new_v = cache_layout(kt), cache_layout(vt)

    return (y_p.reshape(BATCH, SEQ, D), y_s.reshape(DEC_BATCH, DEC_SEQ, D), new_hgrn, new_gla, new_k, new_v)
```

```python
import functools
import math

import jax
import jax.numpy as jnp
import numpy as np
from jax import lax
from jax.experimental import pallas as pl
from jax.experimental.pallas import tpu as pltpu

D = 1024
BATCH, SEQ = 16, 256
DEC_BATCH, DEC_SEQ = 2, 2048
PAST_LEN = 512
GRID_W = 64
D_FF = 2816
N_MOD = 9
A_HEADS, A_DK, A_DV = 4, 128, 128
A_W = A_HEADS * A_DK
B_HEADS, B_DK, B_DV = 4, 64, 128
B_QK = B_HEADS * B_DK
B_V = B_HEADS * B_DV
GATE_RANK = 16
GLA_TAU = 16.0
CHUNK = 128
C_HEADS, C_KV_HEADS, C_HEAD_DIM = 16, 4, 64
C_GROUPS = C_HEADS // C_KV_HEADS
C_Q = C_HEADS * C_HEAD_DIM
C_KV = C_KV_HEADS * C_HEAD_DIM
WINDOW = 128
ROPE_FREQS = C_HEAD_DIM // 4
ROPE_BASE = 10000.0
DEPTH = 2
ALPHA = (2.0 * DEPTH) ** 0.25
LN_EPS = 1e-5
RMS_EPS = 1e-6

TP = BATCH * SEQ
TS = DEC_BATCH * DEC_SEQ
T = TP + TS
N_SEG = 1 + DEC_BATCH

LANES = 128
HALF = LANES // 2
FFN_TM = 1024
FFN_TM_SPLIT = 512
FFN_SUB = 256
PROJ_TM = 4 * SEQ
INPROJ_TM = 1024
PROJ_SUB = 256
OUT_TM = 1024
VMEM_LIMIT = 60 * 1024 * 1024

F32 = jnp.float32
BF16 = jnp.bfloat16


def _dot(a, b):
    return jnp.dot(a, b, preferred_element_type=F32)


def _dot_nt(a, b):
    return lax.dot_general(a, b, (((1,), (1,)), ((), ())), preferred_element_type=F32)


def _dot_tn(a, b):
    return lax.dot_general(a, b, (((0,), (0,)), ((), ())), preferred_element_type=F32)


def _silu(x):
    return x * jax.nn.sigmoid(x)


def _layer_norm(z, g, b):
    mu = jnp.mean(z, axis=-1, keepdims=True)
    zc = z - mu
    var = jnp.mean(zc * zc, axis=-1, keepdims=True)
    return zc * lax.rsqrt(var + LN_EPS) * g + b


def _seg_of_tile(i, tm):
    n_p = TP // tm
    n_s = DEC_SEQ // tm
    return jnp.where(i < n_p, 0, 1 + lax.div(jnp.maximum(i - n_p, 0), n_s))


def _params(n_axes):
    return pltpu.CompilerParams(dimension_semantics=("arbitrary",) * n_axes, vmem_limit_bytes=VMEM_LIMIT)


def _resident(shape):
    nd = len(shape)
    return pl.BlockSpec(shape, lambda *_: (0,) * nd, pipeline_mode=pl.Buffered(1))


def _resident_slice(shape, lead):
    block = (None,) * len(lead) + tuple(shape)
    return pl.BlockSpec(block, lambda *_: tuple(lead) + (0,) * len(shape), pipeline_mode=pl.Buffered(1))


def _mod_specs(layer, cols, tm):
    return [pl.BlockSpec((None, None, 1, D), functools.partial(
        lambda i, c: (layer, _seg_of_tile(i, tm), 0, c), c=c)) for c in cols]


def _ln_specs(layer, idx):
    return [_resident_slice((1, D), (layer, idx))] * 2


BF16_SUBLANES = 16


def _cast_plan(ws, lead, n_steps, step_of):
    in_specs, out_specs, out_shapes = [], [], []
    for w in ws:
        rows, cols = w.shape[len(lead):]
        blk = next(b for b in range(BF16_SUBLANES, rows + 1, BF16_SUBLANES)
                   if rows % b == 0 and rows // b <= n_steps)
        last = rows // blk - 1
        in_specs.append(pl.BlockSpec((None,) * len(lead) + (blk, cols), functools.partial(
            lambda *g, last: tuple(lead) + (jnp.minimum(step_of(*g), last), 0), last=last)))
        out_specs.append(pl.BlockSpec((blk, cols), functools.partial(
            lambda *g, last: (jnp.minimum(step_of(*g), last), 0), last=last)))
        out_shapes.append(jax.ShapeDtypeStruct((rows, cols), BF16))
    return in_specs, out_specs, out_shapes


def _hosting_casts(body, n_in, n_out, n_cast):
    def hosted(*refs, **kw):
        ins, refs = refs[:n_in], refs[n_in:]
        cast_in, refs = refs[:n_cast], refs[n_cast:]
        outs, refs = refs[:n_out], refs[n_out:]
        cast_out, scratch = refs[:n_cast], refs[n_cast:]
        for src, dst in zip(cast_in, cast_out):
            dst[...] = src[...].astype(BF16)
        body(*ins, *outs, *scratch, **kw)
    return hosted


def _mod_body(c_ref, w_ref, b_ref, o_ref):
    c = c_ref[...]
    s = _silu(c).astype(BF16)
    o_ref[0] = _dot(s, w_ref[0].astype(BF16)) + b_ref[0]


def _mod_vectors(cs, w_mod, b_mod):
    tn = 1536
    n = N_MOD * D
    return pl.pallas_call(
        _mod_body,
        grid=(DEPTH, n // tn),
        in_specs=[
            pl.BlockSpec((8, D), lambda l, j: (0, 0)),
            pl.BlockSpec((1, D, tn), lambda l, j: (l, 0, j)),
            pl.BlockSpec((1, 1, tn), lambda l, j: (l, 0, j)),
        ],
        out_specs=pl.BlockSpec((1, 8, tn), lambda l, j: (l, 0, j)),
        out_shape=jax.ShapeDtypeStruct((DEPTH, 8, n), F32),
        compiler_params=_params(2),
        name="mod_vectors",
    )(cs, w_mod, b_mod.reshape(DEPTH, 1, n))


def _ffn_body(*refs, n_x, n_o, tm):
    x_refs = refs[:n_x]
    shift_ref, scale_ref, gate_ref, w1_ref, w3_ref, w2_ref, g_ref, b_ref = refs[n_x:n_x + 8]
    o_refs = refs[n_x + 8:]

    def compute(x_ref, o_ref):
        shift, scale, gate = shift_ref[...], scale_ref[...], gate_ref[...]
        for r in range(0, tm, FFN_SUB):
            rows = slice(r, r + FFN_SUB)
            x = x_ref[rows, :]
            h = (x * (1.0 + scale) + shift).astype(BF16)
            a = _dot(h, w1_ref[...])
            b = _dot(h, w3_ref[...])
            g = (_silu(a) * b).astype(BF16)
            y = _dot(g, w2_ref[...])
            z = ALPHA * x + (0.5 * gate) * y
            o_ref[rows, :] = _layer_norm(z, g_ref[...], b_ref[...])

    if n_x == 1 and n_o == 1:
        compute(x_refs[0], o_refs[0])
    else:
        in_prompt = pl.program_id(0) < TP // tm
        pl.when(in_prompt)(lambda: compute(x_refs[0], o_refs[0]))
        pl.when(jnp.logical_not(in_prompt))(lambda: compute(x_refs[-1], o_refs[-1]))


def _group_specs(split, tm, width=D):
    if not split:
        return [pl.BlockSpec((tm, width), lambda i: (i, 0))]
    n_p = TP // tm
    return [pl.BlockSpec((tm, width), lambda i: (jnp.minimum(i, n_p - 1), 0)),
            pl.BlockSpec((tm, width), lambda i: (jnp.maximum(i - n_p, 0), 0))]


def _ffn_sublayer(xs, mod, w1, w3, w2, ln_g, ln_b, layer, sub, split_out=False, casts=None):
    n_x, n_o = len(xs), 2 if split_out else 1
    tm = FFN_TM if n_x == n_o == 1 else FFN_TM_SPLIT
    out_shape = ([jax.ShapeDtypeStruct((TP, D), F32), jax.ShapeDtypeStruct((TS, D), F32)] if split_out
                 else [jax.ShapeDtypeStruct((T, D), F32)])
    mod_lo = 6 * sub
    body = functools.partial(_ffn_body, n_x=n_x, n_o=n_o, tm=tm)
    in_specs = _group_specs(n_x == 2, tm) + _mod_specs(layer, (mod_lo, mod_lo + 1, mod_lo + 2), tm) + [
        _resident((D, D_FF)),
        _resident((D, D_FF)),
        _resident((D_FF, D)),
    ] + _ln_specs(layer, 2 * sub)
    out_specs = _group_specs(split_out, tm)
    args = [*xs, mod, mod, mod, w1, w3, w2, ln_g, ln_b]
    if casts is not None:
        ws, lead = casts
        c_in, c_out, c_shapes = _cast_plan(ws, lead, T // tm, lambda i: i)
        body = _hosting_casts(body, len(in_specs), len(out_specs), len(ws))
        in_specs, out_specs, out_shape = in_specs + c_in, out_specs + c_out, out_shape + c_shapes
        args = args + list(ws)
    return pl.pallas_call(
        body,
        grid=(T // tm,),
        in_specs=in_specs,
        out_specs=out_specs,
        out_shape=out_shape,
        compiler_params=_params(1),
        name="ffn_sublayer",
    )(*args)


PF_AQ, PF_FF, PF_FB, PF_AG = 0, 512, 1024, 1536
PF_BQ, PF_BK, PF_BG, PF_LAF, PF_LAB = 2048, 2304, 2560, 3072, 3328
PF_W = 3584
PB_AV, PB_BV = 0, 512
PB_W = 1024
WM_AQ, WM_FF, WM_FB, WM_AG, WM_BQ, WM_BK, WM_BG, WM_AI, WM_BV = 0, 512, 1024, 1536, 2048, 2304, 2560, 3072, 3584
WM_W = 4096


def _log_sigmoid(x):
    return jnp.minimum(x, 0.0) - jnp.log(1.0 + jnp.exp(-jnp.abs(x)))


def _inproj_body(x_ref, shift_ref, scale_ref, w_ref, wz_ref, gu_ref, gb_ref, lb_ref, pf_ref, pb_ref, *, layer_e):
    def lower_bound(d):
        l = lb_ref[d]
        e = jnp.exp(l - jnp.max(l, axis=0, keepdims=True))
        sm = e / jnp.sum(e, axis=0, keepdims=True)
        return jnp.sum(sm[:layer_e + 1], axis=0, keepdims=True)

    lbs = [lower_bound(0), lower_bound(1)]
    for r in range(0, x_ref.shape[0], PROJ_SUB):
        rows = slice(r, r + PROJ_SUB)
        h = (x_ref[rows, :] * (1.0 + scale_ref[...]) + shift_ref[...]).astype(BF16)

        def proj(off, width):
            return _dot(h, w_ref[:, off:off + width])

        pf_ref[rows, PF_AQ:PF_AQ + A_W] = proj(WM_AQ, A_W)
        for lb, wm, pf in ((lbs[0], WM_FF, PF_FF), (lbs[1], WM_FB, PF_FB)):
            pf_ref[rows, pf:pf + A_W] = lb + (1.0 - lb) * jax.nn.sigmoid(proj(wm, A_W))
        pf_ref[rows, PF_AG:PF_AG + A_W] = _silu(proj(WM_AG, A_W))
        pf_ref[rows, PF_BQ:PF_BQ + B_QK] = proj(WM_BQ, B_QK) * (B_DK ** -0.5)
        pf_ref[rows, PF_BK:PF_BK + B_QK] = proj(WM_BK, B_QK)
        pf_ref[rows, PF_BG:PF_BG + B_V] = _silu(proj(WM_BG, B_V))
        pb_ref[rows, PB_AV:PB_AV + A_W] = _silu(proj(WM_AI, A_W)).astype(BF16)
        pb_ref[rows, PB_BV:PB_BV + B_V] = proj(WM_BV, B_V).astype(BF16)
        z = _dot(h, wz_ref[...]).astype(BF16)
        pre = _dot(z, gu_ref[...]) + gb_ref[...]
        pf_ref[rows, PF_LAF:PF_LAF + 2 * B_QK] = _log_sigmoid(pre) * (1.0 / GLA_TAU)


def _inproj(x, mod, wmain, wz, gup, gb, hgrn_lb, layer, layer_e, tm=INPROJ_TM):
    n_l = hgrn_lb.shape[1]
    return pl.pallas_call(
        functools.partial(_inproj_body, layer_e=layer_e),
        grid=(T // tm,),
        in_specs=[pl.BlockSpec((tm, D), lambda i: (i, 0))] + _mod_specs(layer, (3, 4), tm) + [
            _resident((D, WM_W)),
            _resident((D, LANES)),
            _resident((LANES, 2 * B_QK)),
            _resident((1, 2 * B_QK)),
            _resident((2, n_l, A_W)),
        ],
        out_specs=[
            pl.BlockSpec((tm, PF_W), lambda i: (i, 0)),
            pl.BlockSpec((tm, PB_W), lambda i: (i, 0)),
        ],
        out_shape=[jax.ShapeDtypeStruct((T, PF_W), F32), jax.ShapeDtypeStruct((T, PB_W), BF16)],
        compiler_params=_params(1),
        name="ab_inproj",
    )(x, mod, mod, wmain, wz, gup, gb, hgrn_lb)


SCAN_PROMPT_SEQS = 4
SCAN_UNROLL = 8


def _prefix_rows(x):
    row = lax.broadcasted_iota(jnp.int32, x.shape, 0)
    s = 1
    while s < x.shape[0]:
        x = x + jnp.where(row >= s, pltpu.roll(x, s, axis=0), 0.0)
        s *= 2
    return x


def _scan_body(*refs, seq_len, seqs, pair, has_s0, emit_state):
    n = seq_len // CHUNK
    n_all = seqs * n
    nh = 2 if pair else 1
    it = iter(refs)
    q_ref = next(it)
    if pair:
        k_ref, laf_ref, lab_ref = next(it), next(it), next(it)
    else:
        ff_ref, fb_ref = next(it), next(it)
    g_ref, v_ref, nw_ref = next(it), next(it), next(it)
    s0_ref = next(it) if has_s0 else None
    o_ref = next(it)
    st_ref = next(it) if emit_state else None
    qd_s, oi_s, kv_s, dec_s, sb_s = it

    row = lax.broadcasted_iota(jnp.int32, (CHUNK, CHUNK), 0)
    col = lax.broadcasted_iota(jnp.int32, (CHUNK, CHUNK), 1)
    tril = row >= col
    triu = row <= col
    lane = lax.broadcasted_iota(jnp.int32, (1, LANES), 1)
    lane2 = lax.broadcasted_iota(jnp.int32, (1, 2 * LANES), 1)
    if pair:
        masks = [lane < HALF, lane >= HALF]
        masks2 = [(lane2 % LANES) < HALF, (lane2 % LANES) >= HALF]
    else:
        masks, masks2 = [None], [None]

    def pick(mask, x):
        return x if mask is None else jnp.where(mask, x, jnp.zeros_like(x))

    def rows_of(c):
        return pl.ds(pl.multiple_of(c * CHUNK, CHUNK), CHUNK)

    def loop(body):
        if n_all <= SCAN_UNROLL:
            for c in range(n_all):
                body(c)
        else:
            def fbody(i, carry):
                for u in range(SCAN_UNROLL):
                    body(i * SCAN_UNROLL + u)
                return carry
            lax.fori_loop(0, n_all // SCAN_UNROLL, fbody, 0)

    def phase1(c):
        rows = rows_of(c)
        q = q_ref[rows, :]
        if pair:
            k_f = k_b = k_ref[rows, :]
            la_f, la_b = laf_ref[rows, :], lab_ref[rows, :]
        else:
            f_f, f_b = ff_ref[rows, :], fb_ref[rows, :]
            k_f, k_b = 1.0 - f_f, 1.0 - f_b
            la_f, la_b = jnp.log(f_f), jnp.log(f_b)
        cs = _prefix_rows(jnp.concatenate([la_f, la_b], axis=1))
        cf, cbi = cs[:, :LANES], cs[:, LANES:]
        tot_f, tot_b = cf[CHUNK - 1:CHUNK, :], cbi[CHUNK - 1:CHUNK, :]
        rb = tot_b - cbi + la_b
        ref_f, ref_b = cf[CHUNK // 2 - 1:CHUNK // 2, :], rb[CHUNK // 2:CHUNK // 2 + 1, :]
        qtf = q * jnp.exp(cf - ref_f)
        qtb = q * jnp.exp(rb - ref_b)
        ktf = k_f * jnp.exp(ref_f - cf)
        ktb = k_b * jnp.exp(ref_b - rb)
        qd = jnp.concatenate([qtf * jnp.exp(ref_f), qtb * jnp.exp(ref_b)], axis=1).astype(BF16)
        ku = jnp.concatenate([ktf * jnp.exp(tot_f - ref_f), ktb * jnp.exp(tot_b - ref_b)], axis=1).astype(BF16)
        qd_s[rows, :] = qd
        qt = jnp.concatenate([qtf, qtb], axis=0).astype(BF16)
        kt = jnp.concatenate([ktf, ktb], axis=0).astype(BF16)
        kv = None
        for hh in range(nh):
            v = v_ref[rows, hh * LANES:(hh + 1) * LANES]
            sc = _dot_nt(pick(masks[hh], qt), kt)
            att = jnp.where(tril, sc[:CHUNK, :CHUNK], 0.0) + jnp.where(triu, sc[CHUNK:, CHUNK:], 0.0)
            oi_s[rows, hh * LANES:(hh + 1) * LANES] = _dot(att.astype(BF16), v)
            kv_h = _dot_tn(v, ku)
            kv = kv_h if kv is None else jnp.where(masks2[0], kv, kv_h)
        kv_s[c] = kv
        dec_s[c] = jnp.exp(jnp.concatenate([tot_f, tot_b], axis=1))

    loop(phase1)

    def recurrence(sq, d, reverse):
        cols = slice(d * LANES, (d + 1) * LANES)
        c0 = sq * n
        st0 = s0_ref[sq, d, 0].T if has_s0 else jnp.zeros((LANES, LANES), F32)

        def step(c, st):
            sb_s[c, :, cols] = st.astype(BF16)
            return st * dec_s[c, :, cols] + kv_s[c, :, cols]

        if n <= 8:
            st = st0
            for c in (range(n - 1, -1, -1) if reverse else range(n)):
                st = step(c0 + c, st)
        else:
            st = lax.fori_loop(0, n, lambda i, st: step(c0 + (n - 1 - i if reverse else i), st), st0)
        if emit_state:
            st_ref[sq, d, 0] = st.T

    for sq in range(seqs):
        recurrence(sq, 0, False)
        recurrence(sq, 1, True)

    nw = nw_ref[...]

    def phase2(c):
        rows = rows_of(c)
        qcat = qd_s[rows, :]
        scat = sb_s[c]
        for hh in range(nh):
            cols = slice(hh * LANES, (hh + 1) * LANES)
            o = oi_s[rows, cols] + _dot_nt(pick(masks2[hh], qcat), scat)
            o = o * lax.rsqrt(jnp.mean(o * o, axis=-1, keepdims=True) + RMS_EPS) * nw
            o_ref[rows, cols] = (o * g_ref[rows, cols]).astype(BF16)

    loop(phase2)


def _scan(pf, pb, norm_w, s0, *, prompt, pair):
    seq_len = SEQ if prompt else DEC_SEQ
    nseq = BATCH if prompt else DEC_BATCH
    seqs = SCAN_PROMPT_SEQS if prompt else 1
    rows = seqs * seq_len
    row_off = 0 if prompt else TP // rows
    units = B_HEADS // 2 if pair else A_HEADS
    nh = 2 if pair else 1
    n_all = rows // CHUNK
    has_s0 = s0 is not None
    emit_state = prompt

    def colspec(off, width=LANES):
        base = off // width
        return pl.BlockSpec((rows, width), lambda s, u: (s + row_off, base + u))

    if pair:
        in_specs = [colspec(PF_BQ), colspec(PF_BK), colspec(PF_LAF), colspec(PF_LAB),
                    colspec(PF_BG, 2 * LANES), colspec(PB_BV, 2 * LANES)]
        args = [pf, pf, pf, pf, pf, pb]
    else:
        in_specs = [colspec(PF_AQ), colspec(PF_FF), colspec(PF_FB), colspec(PF_AG), colspec(PB_AV)]
        args = [pf, pf, pf, pf, pb]
    in_specs.append(pl.BlockSpec((1, LANES), lambda s, u: (0, 0)))
    args.append(norm_w.reshape(1, LANES))
    state_spec = pl.BlockSpec((seqs, 2, 1, LANES, LANES), lambda s, u: (s, 0, u, 0, 0))
    if has_s0:
        in_specs.append(state_spec)
        args.append(s0)
    out_specs = [pl.BlockSpec((rows, nh * LANES), lambda s, u: (s, u))]
    out_shape = [jax.ShapeDtypeStruct((nseq * seq_len, units * nh * LANES), BF16)]
    if emit_state:
        out_specs.append(state_spec)
        out_shape.append(jax.ShapeDtypeStruct((nseq, 2, units, LANES, LANES), F32))
    scratch = [
        pltpu.VMEM((rows, 2 * LANES), BF16),
        pltpu.VMEM((rows, nh * LANES), F32),
        pltpu.VMEM((n_all, LANES, 2 * LANES), F32),
        pltpu.VMEM((n_all, 1, 2 * LANES), F32),
        pltpu.VMEM((n_all, LANES, 2 * LANES), BF16),
    ]
    return pl.pallas_call(
        functools.partial(_scan_body, seq_len=seq_len, seqs=seqs, pair=pair, has_s0=has_s0,
                          emit_state=emit_state),
        grid=(nseq // seqs, units),
        in_specs=in_specs,
        out_specs=out_specs,
        out_shape=out_shape,
        scratch_shapes=scratch,
        compiler_params=_params(2),
        name=f"scan_{'p' if prompt else 's'}_{'gla' if pair else 'hgrn'}",
    )(*args)


def _outproj_body(*refs, n_lhs, tm):
    lhs = refs[:2 * n_lhs]
    ws = refs[2 * n_lhs:3 * n_lhs]
    x_ref, m_ref, g_ref, b_ref, o_ref = refs[3 * n_lhs:]

    def compute(group):
        for r in range(0, tm, PROJ_SUB):
            rows = slice(r, r + PROJ_SUB)
            y = _dot(lhs[group][rows, :], ws[0][...])
            for j in range(1, n_lhs):
                y = y + _dot(lhs[2 * j + group][rows, :], ws[j][...])
            z = ALPHA * x_ref[rows, :] + m_ref[...] * y
            o_ref[rows, :] = _layer_norm(z, g_ref[...], b_ref[...])

    in_prompt = pl.program_id(0) < TP // tm
    pl.when(in_prompt)(lambda: compute(0))
    pl.when(jnp.logical_not(in_prompt))(lambda: compute(1))


def _outproj(lhs, ws, x, mod, ln_g, ln_b, layer, tm=OUT_TM):
    n_lhs = len(lhs)
    in_specs, args = [], []
    for a_p, a_s in lhs:
        in_specs += _group_specs(True, tm, a_p.shape[1])
        args += [a_p, a_s]
    in_specs += [_resident(w.shape) for w in ws]
    in_specs += [pl.BlockSpec((tm, D), lambda i: (i, 0))] + _mod_specs(layer, (5,), tm) + _ln_specs(layer, 1)
    return pl.pallas_call(
        functools.partial(_outproj_body, n_lhs=n_lhs, tm=tm),
        grid=(T // tm,),
        in_specs=in_specs,
        out_specs=pl.BlockSpec((tm, D), lambda i: (i, 0)),
        out_shape=jax.ShapeDtypeStruct((T, D), F32),
        compiler_params=_params(1),
        name="mixer_outproj",
    )(*args, *ws, x, mod, ln_g, ln_b)


def _rope_partner(x):
    lane = lax.broadcasted_iota(jnp.int32, x.shape, 1)
    first_half = (lane % (2 * ROPE_FREQS)) < ROPE_FREQS
    return jnp.where(first_half, pltpu.roll(x, LANES - ROPE_FREQS, axis=1), pltpu.roll(x, ROPE_FREQS, axis=1))


def _qkv_body(x_ref, shift_ref, scale_ref, w_ref, wkvt_ref, cos_ref, sin_ref,
              q_ref, k_ref, v_ref, kt_ref, vt_ref):
    qscale = C_HEAD_DIM ** -0.5
    in_prompt = pl.program_id(0) < TP // PROJ_TM

    def modulated(rows):
        return (x_ref[rows, :] * (1.0 + scale_ref[...]) + shift_ref[...]).astype(BF16)

    @pl.when(in_prompt)
    def _():
        for sq in range(PROJ_TM // SEQ):
            rows = slice(sq * SEQ, (sq + 1) * SEQ)
            h = modulated(rows)
            q_ref[rows, :] = (_dot(h, w_ref[:, :C_Q]) * qscale).astype(BF16)
            v_ref[rows, :] = _dot(h, w_ref[:, C_Q + C_KV:]).astype(BF16)
            kt_ref[sq] = _dot_nt(wkvt_ref[:C_KV, :], h)
            vt_ref[sq] = _dot_nt(wkvt_ref[C_KV:, :], h)

    @pl.when(jnp.logical_not(in_prompt))
    def _():
        for sq in range(PROJ_TM // SEQ):
            rows = slice(sq * SEQ, (sq + 1) * SEQ)
            h = modulated(rows)
            cos, sin = cos_ref[rows, :], sin_ref[rows, :]
            zq = _dot(h, w_ref[:, :C_Q])
            zk = _dot(h, w_ref[:, C_Q:C_Q + C_KV])
            v_ref[rows, :] = _dot(h, w_ref[:, C_Q + C_KV:]).astype(BF16)

            def rope(z, cos=cos, sin=sin):
                return z * cos + _rope_partner(z) * sin

            for j in range(C_Q // LANES):
                cols = slice(j * LANES, (j + 1) * LANES)
                q_ref[rows, cols] = (rope(zq[:, cols]) * qscale).astype(BF16)
            for j in range(C_KV // LANES):
                cols = slice(j * LANES, (j + 1) * LANES)
                k_ref[rows, cols] = rope(zk[:, cols]).astype(BF16)


def _qkv(x, mod, w, wkvt, cos, sin, layer):
    tm = PROJ_TM
    n_p = TP // tm
    lat = lambda i: (jnp.maximum(i - n_p, 0), 0)
    ctx = lambda i: (jnp.minimum(i, n_p - 1), 0, 0)
    return pl.pallas_call(
        _qkv_body,
        grid=(T // tm,),
        in_specs=[pl.BlockSpec((tm, D), lambda i: (i, 0))] + _mod_specs(layer, (3, 4), tm) + [
            _resident((D, C_Q + 2 * C_KV)),
            _resident((2 * C_KV, D)),
            pl.BlockSpec((tm, LANES), lat),
            pl.BlockSpec((tm, LANES), lat),
        ],
        out_specs=[
            pl.BlockSpec((tm, C_Q), lambda i: (i, 0)),
            pl.BlockSpec((tm, C_KV), lat),
            pl.BlockSpec((tm, C_KV), lambda i: (i, 0)),
            pl.BlockSpec((tm // SEQ, C_KV, SEQ), ctx),
            pl.BlockSpec((tm // SEQ, C_KV, SEQ), ctx),
        ],
        out_shape=[jax.ShapeDtypeStruct((T, C_Q), BF16), jax.ShapeDtypeStruct((TS, C_KV), BF16),
                   jax.ShapeDtypeStruct((T, C_KV), BF16),
                   jax.ShapeDtypeStruct((BATCH, C_KV, SEQ), F32), jax.ShapeDtypeStruct((BATCH, C_KV, SEQ), F32)],
        compiler_params=_params(1),
        name="c_qkv",
    )(x, mod, mod, w, wkvt, cos, sin)


def _rope_tables():
    t = np.arange(DEC_SEQ)
    pos = np.stack([t // GRID_W, t % GRID_W], axis=1).astype(np.float32)
    inv = (ROPE_BASE ** (-np.arange(ROPE_FREQS, dtype=np.float32) / ROPE_FREQS)).astype(np.float32)
    d = np.arange(C_HEAD_DIM)
    axis = d // (2 * ROPE_FREQS)
    ang = pos[:, axis] * inv[d % ROPE_FREQS][None, :]
    sign = np.where((d % (2 * ROPE_FREQS)) < ROPE_FREQS, -1.0, 1.0)[None, :]
    cos_h, sin_h = np.cos(ang), np.sin(ang) * sign
    reps = LANES // C_HEAD_DIM
    cos = np.tile(np.tile(cos_h, (1, reps)), (DEC_BATCH, 1)).astype(np.float32)
    sin = np.tile(np.tile(sin_h, (1, reps)), (DEC_BATCH, 1)).astype(np.float32)
    return jnp.asarray(cos), jnp.asarray(sin)


def _dup_head(blk, half):
    lane = lax.broadcasted_iota(jnp.int32, blk.shape, 1)
    keep = (lane >= HALF).astype(jnp.int32) == half
    return jnp.where(keep, blk, pltpu.roll(blk, HALF, axis=1))


def _pair_values(v, half):
    lo = lax.broadcasted_iota(jnp.int32, v.shape, 1) < HALF
    v2 = _dup_head(v.astype(F32), half)
    top = jnp.concatenate([jnp.where(lo, v2, 0.0), jnp.where(lo, 1.0, 0.0)], axis=1)
    bot = jnp.concatenate([jnp.where(lo, 0.0, v2), jnp.where(lo, 0.0, 1.0)], axis=1)
    return top.astype(BF16), bot.astype(BF16)


def _softmax_weights(scores, sink):
    mx = sink
    for s in scores:
        mx = jnp.maximum(mx, jnp.max(s, axis=-1, keepdims=True))
    return [jnp.exp(s - mx).astype(BF16) for s in scores], jnp.exp(sink - mx)


def _pair_sinks(sink_ref, jb):
    return (sink_ref[:, jb * LANES:jb * LANES + 1], sink_ref[:, jb * LANES + HALF:jb * LANES + HALF + 1])


KV_PER_BLOCK = LANES // C_HEAD_DIM
TILE_KV = C_KV_HEADS
TILE_KVW = TILE_KV * C_HEAD_DIM
CTX_TILES = BATCH * C_KV_HEADS // TILE_KV
CTX_BLOCKS = TILE_KV * C_GROUPS // 2
PIPE_LAG = 2


def _kv_block(ref, kv):
    lb = kv // KV_PER_BLOCK
    return ref[:, lb * LANES:(lb + 1) * LANES], kv % KV_PER_BLOCK


def _ctx_tile(g, lag):
    t = jnp.clip(g - lag, 0, CTX_TILES - 1)
    n = C_KV_HEADS // TILE_KV
    return t // n, t % n


def _ctx_attn_body(q_ref, kt_ref, v_ref, sink_ref, o_ref, s_s, e_s, st_s):
    g = pl.program_id(0)
    lo = lax.broadcasted_iota(jnp.int32, (1, LANES), 1) < HALF
    pairs_per_kv = C_GROUPS // 2

    @pl.when(g == 0)
    def _():
        s_s[...] = jnp.zeros_like(s_s)
        e_s[...] = jnp.zeros_like(e_s)
        st_s[...] = jnp.ones_like(st_s)

    def stages(cur, prev):
        for kv in range(TILE_KV):
            v_top, v_bot = _pair_values(*_kv_block(v_ref, kv))
            for jb in range(kv * pairs_per_kv, (kv + 1) * pairs_per_kv):
                od = _dot(e_s[cur, 2 * jb], v_top) + _dot(e_s[cur, 2 * jb + 1], v_bot)
                o_ref[:, jb * LANES:(jb + 1) * LANES] = (od[:, :LANES] / (od[:, LANES:] + st_s[cur, jb])).astype(BF16)
        for jb in range(CTX_BLOCKS):
            terms = []
            for hh, sink in enumerate(_pair_sinks(sink_ref, jb)):
                (e,), term = _softmax_weights([s_s[prev, 2 * jb + hh]], sink)
                e_s[prev, 2 * jb + hh] = e
                terms.append(term)
            st_s[prev, jb] = jnp.where(lo, terms[0], terms[1])
        for kv in range(TILE_KV):
            kt = kt_ref[kv * C_HEAD_DIM:(kv + 1) * C_HEAD_DIM, :]
            zero = jnp.zeros_like(kt)
            kt_a = jnp.concatenate([kt, zero], axis=0).astype(BF16)
            kt_b = jnp.concatenate([zero, kt], axis=0).astype(BF16)
            for jb in range(kv * pairs_per_kv, (kv + 1) * pairs_per_kv):
                q = q_ref[:, jb * LANES:(jb + 1) * LANES]
                s_s[cur, 2 * jb] = _dot(q, kt_a)
                s_s[cur, 2 * jb + 1] = _dot(q, kt_b)

    pl.when(g % 2 == 0)(lambda: stages(0, 1))
    pl.when(g % 2 == 1)(lambda: stages(1, 0))


def _ctx_attn(q, kt, v, sink_cols):
    gw = CTX_BLOCKS * LANES

    def at(lag, fn):
        return lambda g: fn(*_ctx_tile(g, lag))

    return pl.pallas_call(
        _ctx_attn_body,
        grid=(CTX_TILES + PIPE_LAG,),
        in_specs=[
            pl.BlockSpec((SEQ, gw), at(0, lambda b, p: (b, p))),
            pl.BlockSpec((None, TILE_KVW, SEQ), at(0, lambda b, p: (b, p, 0))),
            pl.BlockSpec((SEQ, TILE_KVW), at(2, lambda b, p: (b, p))),
            pl.BlockSpec((1, gw), at(1, lambda b, p: (0, p))),
        ],
        out_specs=pl.BlockSpec((SEQ, gw), at(2, lambda b, p: (b, p))),
        out_shape=jax.ShapeDtypeStruct((TP, C_Q), BF16),
        scratch_shapes=[
            pltpu.VMEM((2, 2 * CTX_BLOCKS, SEQ, SEQ), F32),
            pltpu.VMEM((2, 2 * CTX_BLOCKS, SEQ, SEQ), BF16),
            pltpu.VMEM((2, CTX_BLOCKS, SEQ, LANES), F32),
        ],
        compiler_params=_params(1),
        name="c_attn_ctx",
    )(q, kt, v, sink_cols)


LAT_TQ = 128
LAT_WIN = LAT_TQ + 2 * WINDOW
LAT_NQ = DEC_SEQ // LAT_TQ
LAT_KVB = C_KV_HEADS // TILE_KV
LAT_TILES = DEC_BATCH * LAT_KVB * LAT_NQ


def _lat_tile(g, lag):
    t = jnp.clip(g - lag, 0, LAT_TILES - 1)
    bp = t // LAT_NQ
    return bp // LAT_KVB, bp % LAT_KVB, t % LAT_NQ


def _lat_window(j):
    return pl.multiple_of(jnp.clip(j * LAT_TQ - WINDOW, 0, DEC_SEQ - LAT_WIN), LANES)


def _lat_attn_body(q_ref, k_ref, v_ref, ck_ref, cv_ref, sink_ref, o_ref,
                   ka_s, kb_s, cka_s, ckb_s, va_s, vb_s, cva_s, cvb_s, sl_s, sc_s, el_s, ec_s, st_s):
    g = pl.program_id(0)
    lo = lax.broadcasted_iota(jnp.int32, (1, LANES), 1) < HALF
    pairs_per_kv = C_GROUPS // 2
    b_a, p_a, j_a = _lat_tile(g, 0)
    b_c, p_c, j_c = _lat_tile(g, 2)
    v_slot_a = (b_a * LAT_KVB + p_a) % 2
    v_slot_c = (b_c * LAT_KVB + p_c) % 2

    @pl.when(g == 0)
    def _():
        sl_s[...] = jnp.zeros_like(sl_s)
        sc_s[...] = jnp.zeros_like(sc_s)
        el_s[...] = jnp.zeros_like(el_s)
        ec_s[...] = jnp.zeros_like(ec_s)
        st_s[...] = jnp.ones_like(st_s)

    @pl.when(jnp.logical_and(j_a == 0, g < LAT_TILES))
    def _():
        for kv in range(TILE_KV):
            for src, a_s, b_s in ((k_ref, ka_s, kb_s), (ck_ref, cka_s, ckb_s)):
                blk, kh = _kv_block(src, kv)
                lo2 = lax.broadcasted_iota(jnp.int32, blk.shape, 1) < HALF
                k2 = _dup_head(blk.astype(F32), kh)
                a_s[kv] = jnp.where(lo2, k2, 0.0).astype(BF16)
                b_s[kv] = jnp.where(lo2, 0.0, k2).astype(BF16)
            va_s[v_slot_a, kv], vb_s[v_slot_a, kv] = _pair_values(*_kv_block(v_ref, kv))
            cva_s[v_slot_a, kv], cvb_s[v_slot_a, kv] = _pair_values(*_kv_block(cv_ref, kv))

    def stages(cur, prev):
        win_c = pl.ds(_lat_window(j_c), LAT_WIN)
        for kh in range(TILE_KV):
            v_loc = (va_s[v_slot_c, kh, win_c, :], vb_s[v_slot_c, kh, win_c, :])
            v_ctx = (cva_s[v_slot_c, kh], cvb_s[v_slot_c, kh])
            for jb in range(kh * pairs_per_kv, (kh + 1) * pairs_per_kv):
                od = None
                for hh in range(2):
                    part = _dot(el_s[cur, 2 * jb + hh], v_loc[hh]) + _dot(ec_s[cur, 2 * jb + hh], v_ctx[hh])
                    od = part if od is None else od + part
                o_ref[:, jb * LANES:(jb + 1) * LANES] = (od[:, :LANES] / (od[:, LANES:] + st_s[cur, jb])).astype(BF16)
        for jb in range(CTX_BLOCKS):
            terms = []
            for hh, sink in enumerate(_pair_sinks(sink_ref, jb)):
                (e_loc, e_ctx), term = _softmax_weights([sl_s[prev, 2 * jb + hh], sc_s[prev, 2 * jb + hh]], sink)
                el_s[prev, 2 * jb + hh] = e_loc
                ec_s[prev, 2 * jb + hh] = e_ctx
                terms.append(term)
            st_s[prev, jb] = jnp.where(lo, terms[0], terms[1])
        start = _lat_window(j_a)
        win = pl.ds(start, LAT_WIN)
        qpos = j_a * LAT_TQ + lax.broadcasted_iota(jnp.int32, (LAT_TQ, LAT_WIN), 0)
        kpos = start + lax.broadcasted_iota(jnp.int32, (LAT_TQ, LAT_WIN), 1)
        band = jnp.abs(qpos - kpos) <= WINDOW
        for kh in range(TILE_KV):
            for jb in range(kh * pairs_per_kv, (kh + 1) * pairs_per_kv):
                q = q_ref[:, jb * LANES:(jb + 1) * LANES]
                for hh, (kl, kc) in enumerate(((ka_s, cka_s), (kb_s, ckb_s))):
                    sl_s[cur, 2 * jb + hh] = jnp.where(band, _dot_nt(q, kl[kh, win, :]), -jnp.inf)
                    sc_s[cur, 2 * jb + hh] = _dot_nt(q, kc[kh])

    pl.when(g % 2 == 0)(lambda: stages(0, 1))
    pl.when(g % 2 == 1)(lambda: stages(1, 0))


def _lat_attn(q, k, v, ck, cv, sink_cols):
    gw = CTX_BLOCKS * LANES
    q_off = TP // LAT_TQ
    kv_off = TP // DEC_SEQ
    kvb, n_heads = TILE_KV, 2 * CTX_BLOCKS

    def at(lag, fn):
        return lambda g: fn(*_lat_tile(g, lag))

    return pl.pallas_call(
        _lat_attn_body,
        grid=(LAT_TILES + PIPE_LAG,),
        in_specs=[
            pl.BlockSpec((LAT_TQ, gw), at(0, lambda b, p, j: (q_off + b * LAT_NQ + j, p))),
            pl.BlockSpec((DEC_SEQ, TILE_KVW), at(0, lambda b, p, j: (b, p))),
            pl.BlockSpec((DEC_SEQ, TILE_KVW), at(0, lambda b, p, j: (kv_off + b, p))),
            pl.BlockSpec((PAST_LEN, TILE_KVW), at(0, lambda b, p, j: (b, p))),
            pl.BlockSpec((PAST_LEN, TILE_KVW), at(0, lambda b, p, j: (b, p))),
            pl.BlockSpec((1, gw), at(1, lambda b, p, j: (0, p))),
        ],
        out_specs=pl.BlockSpec((LAT_TQ, gw), at(2, lambda b, p, j: (b * LAT_NQ + j, p))),
        out_shape=jax.ShapeDtypeStruct((TS, C_Q), BF16),
        scratch_shapes=[
            pltpu.VMEM((kvb, DEC_SEQ, LANES), BF16), pltpu.VMEM((kvb, DEC_SEQ, LANES), BF16),
            pltpu.VMEM((kvb, PAST_LEN, LANES), BF16), pltpu.VMEM((kvb, PAST_LEN, LANES), BF16),
            pltpu.VMEM((2, kvb, DEC_SEQ, 2 * LANES), BF16), pltpu.VMEM((2, kvb, DEC_SEQ, 2 * LANES), BF16),
            pltpu.VMEM((2, kvb, PAST_LEN, 2 * LANES), BF16), pltpu.VMEM((2, kvb, PAST_LEN, 2 * LANES), BF16),
            pltpu.VMEM((2, n_heads, LAT_TQ, LAT_WIN), F32), pltpu.VMEM((2, n_heads, LAT_TQ, PAST_LEN), F32),
            pltpu.VMEM((2, n_heads, LAT_TQ, LAT_WIN), BF16), pltpu.VMEM((2, n_heads, LAT_TQ, PAST_LEN), BF16),
            pltpu.VMEM((2, CTX_BLOCKS, LAT_TQ, LANES), F32),
        ],
        compiler_params=_params(1),
        name="c_attn_latent",
    )(q, k, v, ck, cv, sink_cols)


def kernel(x_prompt, x_sample, state_hgrn, state_gla, cache_k, cache_v, c, c_ctx, w_mod, b_mod, ln_g, ln_b,
           ffn_w1, ffn_w3, ffn_w2, w_in_ab, hgrn_lb, gla_gate_up, gla_gate_b, norm_a, norm_b, w_out_ab,
           w_qkv_c, sink_c, w_out_c):
    cs = jnp.zeros((8, D), F32).at[0].set(c_ctx).at[1:1 + DEC_BATCH].set(c)
    mod = _mod_vectors(cs, w_mod, b_mod).reshape(DEPTH, 8, 1, N_MOD * D)
    ffn_ws = (ffn_w1, ffn_w3, ffn_w2)
    ln_g, ln_b = ln_g.reshape(DEPTH, 3, 1, D), ln_b.reshape(DEPTH, 3, 1, D)

    def ffn(xs, ws, layer, sub, split_out=False, cast_next=None):
        casts = None if cast_next is None else (ffn_ws, cast_next)
        return _ffn_sublayer(xs, mod, *ws, ln_g, ln_b, layer, sub, split_out=split_out, casts=casts)

    x, *ws_01 = ffn([x_prompt.reshape(TP, D), x_sample.reshape(TS, D)], [w[0, 0].astype(BF16) for w in ffn_ws],
                    0, 0, cast_next=(0, 1))
    w_in = w_in_ab[0]
    o_aq, o_ai, o_ff, o_fb, o_ag = 0, A_W, 2 * A_W, 3 * A_W, 4 * A_W
    o_bq = 5 * A_W
    o_bk, o_bv = o_bq + B_QK, o_bq + 2 * B_QK
    o_bg = o_bv + B_V
    o_z = o_bg + B_V
    order = [(o_aq, A_W), (o_ff, A_W), (o_fb, A_W), (o_ag, A_W), (o_bq, B_QK), (o_bk, B_QK), (o_bg, B_V),
             (o_ai, A_W), (o_bv, B_V)]
    wmain = jnp.concatenate([w_in[:, o:o + w] for o, w in order], axis=1).astype(BF16)
    wz = jnp.pad(w_in[:, o_z:o_z + 2 * GATE_RANK], ((0, 0), (0, LANES - 2 * GATE_RANK))).astype(BF16)
    gup = jnp.zeros((LANES, 2 * B_QK), F32)
    gup = gup.at[:GATE_RANK, :B_QK].set(gla_gate_up[0, 0]).at[GATE_RANK:2 * GATE_RANK, B_QK:].set(gla_gate_up[0, 1])
    gb = gla_gate_b[0].reshape(1, 2 * B_QK)
    pf, pb = _inproj(x, mod, wmain, wz, gup.astype(BF16), gb, hgrn_lb, 0, 0)

    s0_a = state_hgrn[:, 0]
    s0_b = state_gla[:, 0].reshape(DEC_BATCH, 2, B_HEADS // 2, LANES, B_DV)
    oa_p, st_a = _scan(pf, pb, norm_a[0], None, prompt=True, pair=False)
    ob_p, st_b = _scan(pf, pb, norm_b[0], None, prompt=True, pair=True)
    (oa_s,) = _scan(pf, pb, norm_a[0], s0_a, prompt=False, pair=False)
    (ob_s,) = _scan(pf, pb, norm_b[0], s0_b, prompt=False, pair=True)
    w_out = w_out_ab[0].astype(BF16)
    x = _outproj([(oa_p, oa_s), (ob_p, ob_s)], [w_out[:A_W], w_out[A_W:]], x, mod, ln_g, ln_b, 0)
    x, *ws_10 = ffn([x], ws_01, 0, 1, cast_next=(1, 0))
    new_hgrn = st_a.reshape(BATCH, 1, 2, A_HEADS, A_DK, A_DV)
    new_gla = st_b.reshape(BATCH, 1, 2, B_HEADS, B_DK, B_DV)

    x, *ws_11 = ffn([x], ws_10, 1, 0, cast_next=(1, 1))
    cos, sin = _rope_tables()
    w_qkv = w_qkv_c[0].astype(BF16)
    q, k, v, kt, vt = _qkv(x, mod, w_qkv, w_qkv[:, C_Q:].T, cos, sin, 1)
    sink_cols = jnp.repeat(sink_c[0], C_HEAD_DIM).reshape(1, C_Q)
    o_p = _ctx_attn(q, kt, v, sink_cols)
    ck = cache_k[:, 0].reshape(DEC_BATCH * PAST_LEN, C_KV)
    cv = cache_v[:, 0].reshape(DEC_BATCH * PAST_LEN, C_KV)
    o_s = _lat_attn(q, k, v, ck, cv, sink_cols)
    x = _outproj([(o_p, o_s)], [w_out_c[0].astype(BF16)], x, mod, ln_g, ln_b, 1)
    y_p, y_s = ffn([x], ws_11, 1, 1, split_out=True)

    def cache_layout(zt):
        return zt.reshape(BATCH, 1, C_KV_HEADS, C_HEAD_DIM, SEQ).transpose(0, 1, 4, 2, 3)

    new_k, new_v = cache_layout(kt), cache_layout(vt)

    return (y_p.reshape(BATCH, SEQ, D), y_s.reshape(DEC_BATCH, DEC_SEQ, D), new_hgrn, new_gla, new_k, new_v)
```

```python
import functools
import math

import jax
import jax.numpy as jnp
import numpy as np
from jax import lax
from jax.experimental import pallas as pl
from jax.experimental.pallas import tpu as pltpu

D = 1024
BATCH, SEQ = 16, 256
DEC_BATCH, DEC_SEQ = 2, 2048
PAST_LEN = 512
GRID_W = 64
D_FF = 2816
N_MOD = 9
A_HEADS, A_DK, A_DV = 4, 128, 128
A_W = A_HEADS * A_DK
B_HEADS, B_DK, B_DV = 4, 64, 128
B_QK = B_HEADS * B_DK
B_V = B_HEADS * B_DV
GATE_RANK = 16
GLA_TAU = 16.0
CHUNK = 128
C_HEADS, C_KV_HEADS, C_HEAD_DIM = 16, 4, 64
C_GROUPS = C_HEADS // C_KV_HEADS
C_Q = C_HEADS * C_HEAD_DIM
C_KV = C_KV_HEADS * C_HEAD_DIM
WINDOW = 128
ROPE_FREQS = C_HEAD_DIM // 4
ROPE_BASE = 10000.0
DEPTH = 2
ALPHA = (2.0 * DEPTH) ** 0.25
LN_EPS = 1e-5
RMS_EPS = 1e-6

TP = BATCH * SEQ
TS = DEC_BATCH * DEC_SEQ
T = TP + TS
N_SEG = 1 + DEC_BATCH

LANES = 128
HALF = LANES // 2
FFN_TM = 1024
FFN_TM_SPLIT = 512
FFN_SUB = 256
PROJ_TM = 4 * SEQ
INPROJ_TM = 512
PROJ_SUB = 256
OUT_TM = 1024
VMEM_LIMIT = 60 * 1024 * 1024

F32 = jnp.float32
BF16 = jnp.bfloat16


def _dot(a, b):
    return jnp.dot(a, b, preferred_element_type=F32)


def _dot_nt(a, b):
    return lax.dot_general(a, b, (((1,), (1,)), ((), ())), preferred_element_type=F32)


def _dot_tn(a, b):
    return lax.dot_general(a, b, (((0,), (0,)), ((), ())), preferred_element_type=F32)


def _silu(x):
    return x * jax.nn.sigmoid(x)


def _layer_norm(z, g, b):
    mu = jnp.mean(z, axis=-1, keepdims=True)
    zc = z - mu
    var = jnp.mean(zc * zc, axis=-1, keepdims=True)
    return zc * lax.rsqrt(var + LN_EPS) * g + b


def _seg_of_tile(i, tm):
    n_p = TP // tm
    n_s = DEC_SEQ // tm
    return jnp.where(i < n_p, 0, 1 + lax.div(jnp.maximum(i - n_p, 0), n_s))


def _params(n_axes):
    return pltpu.CompilerParams(dimension_semantics=("arbitrary",) * n_axes, vmem_limit_bytes=VMEM_LIMIT)


def _resident(shape):
    nd = len(shape)
    return pl.BlockSpec(shape, lambda *_: (0,) * nd, pipeline_mode=pl.Buffered(1))


def _resident_slice(shape, lead):
    block = (None,) * len(lead) + tuple(shape)
    return pl.BlockSpec(block, lambda *_: tuple(lead) + (0,) * len(shape), pipeline_mode=pl.Buffered(1))


def _mod_specs(layer, cols, tm):
    return [pl.BlockSpec((None, None, 1, D), functools.partial(
        lambda i, c: (layer, _seg_of_tile(i, tm), 0, c), c=c)) for c in cols]


def _ln_specs(layer, idx):
    return [_resident_slice((1, D), (layer, idx))] * 2


BF16_SUBLANES = 16


def _cast_plan(ws, lead, n_steps, step_of):
    in_specs, out_specs, out_shapes = [], [], []
    for w in ws:
        rows, cols = w.shape[len(lead):]
        blk = next(b for b in range(BF16_SUBLANES, rows + 1, BF16_SUBLANES)
                   if rows % b == 0 and rows // b <= n_steps)
        last = rows // blk - 1
        in_specs.append(pl.BlockSpec((None,) * len(lead) + (blk, cols), functools.partial(
            lambda *g, last: tuple(lead) + (jnp.minimum(step_of(*g), last), 0), last=last)))
        out_specs.append(pl.BlockSpec((blk, cols), functools.partial(
            lambda *g, last: (jnp.minimum(step_of(*g), last), 0), last=last)))
        out_shapes.append(jax.ShapeDtypeStruct((rows, cols), BF16))
    return in_specs, out_specs, out_shapes


def _hosting_casts(body, n_in, n_out, n_cast):
    def hosted(*refs, **kw):
        ins, refs = refs[:n_in], refs[n_in:]
        cast_in, refs = refs[:n_cast], refs[n_cast:]
        outs, refs = refs[:n_out], refs[n_out:]
        cast_out, scratch = refs[:n_cast], refs[n_cast:]
        for src, dst in zip(cast_in, cast_out):
            dst[...] = src[...].astype(BF16)
        body(*ins, *outs, *scratch, **kw)
    return hosted


def _mod_body(c_ref, w_ref, b_ref, o_ref):
    c = c_ref[...]
    s = _silu(c).astype(BF16)
    o_ref[0] = _dot(s, w_ref[0].astype(BF16)) + b_ref[0]


def _mod_vectors(cs, w_mod, b_mod):
    tn = 1536
    n = N_MOD * D
    return pl.pallas_call(
        _mod_body,
        grid=(DEPTH, n // tn),
        in_specs=[
            pl.BlockSpec((8, D), lambda l, j: (0, 0)),
            pl.BlockSpec((1, D, tn), lambda l, j: (l, 0, j)),
            pl.BlockSpec((1, 1, tn), lambda l, j: (l, 0, j)),
        ],
        out_specs=pl.BlockSpec((1, 8, tn), lambda l, j: (l, 0, j)),
        out_shape=jax.ShapeDtypeStruct((DEPTH, 8, n), F32),
        compiler_params=_params(2),
        name="mod_vectors",
    )(cs, w_mod, b_mod.reshape(DEPTH, 1, n))


def _ffn_body(*refs, n_x, n_o, tm):
    x_refs = refs[:n_x]
    shift_ref, scale_ref, gate_ref, w1_ref, w3_ref, w2_ref, g_ref, b_ref = refs[n_x:n_x + 8]
    o_refs = refs[n_x + 8:]

    def compute(x_ref, o_ref):
        shift, scale, gate = shift_ref[...], scale_ref[...], gate_ref[...]
        for r in range(0, tm, FFN_SUB):
            rows = slice(r, r + FFN_SUB)
            x = x_ref[rows, :]
            h = (x * (1.0 + scale) + shift).astype(BF16)
            a = _dot(h, w1_ref[...])
            b = _dot(h, w3_ref[...])
            g = (_silu(a) * b).astype(BF16)
            y = _dot(g, w2_ref[...])
            z = ALPHA * x + (0.5 * gate) * y
            o_ref[rows, :] = _layer_norm(z, g_ref[...], b_ref[...])

    if n_x == 1 and n_o == 1:
        compute(x_refs[0], o_refs[0])
    else:
        in_prompt = pl.program_id(0) < TP // tm
        pl.when(in_prompt)(lambda: compute(x_refs[0], o_refs[0]))
        pl.when(jnp.logical_not(in_prompt))(lambda: compute(x_refs[-1], o_refs[-1]))


def _group_specs(split, tm, width=D):
    if not split:
        return [pl.BlockSpec((tm, width), lambda i: (i, 0))]
    n_p = TP // tm
    return [pl.BlockSpec((tm, width), lambda i: (jnp.minimum(i, n_p - 1), 0)),
            pl.BlockSpec((tm, width), lambda i: (jnp.maximum(i - n_p, 0), 0))]


def _ffn_sublayer(xs, mod, w1, w3, w2, ln_g, ln_b, layer, sub, split_out=False, casts=None):
    n_x, n_o = len(xs), 2 if split_out else 1
    tm = FFN_TM if n_x == n_o == 1 else FFN_TM_SPLIT
    out_shape = ([jax.ShapeDtypeStruct((TP, D), F32), jax.ShapeDtypeStruct((TS, D), F32)] if split_out
                 else [jax.ShapeDtypeStruct((T, D), F32)])
    mod_lo = 6 * sub
    body = functools.partial(_ffn_body, n_x=n_x, n_o=n_o, tm=tm)
    in_specs = _group_specs(n_x == 2, tm) + _mod_specs(layer, (mod_lo, mod_lo + 1, mod_lo + 2), tm) + [
        _resident((D, D_FF)),
        _resident((D, D_FF)),
        _resident((D_FF, D)),
    ] + _ln_specs(layer, 2 * sub)
    out_specs = _group_specs(split_out, tm)
    args = [*xs, mod, mod, mod, w1, w3, w2, ln_g, ln_b]
    if casts is not None:
        ws, lead = casts
        c_in, c_out, c_shapes = _cast_plan(ws, lead, T // tm, lambda i: i)
        body = _hosting_casts(body, len(in_specs), len(out_specs), len(ws))
        in_specs, out_specs, out_shape = in_specs + c_in, out_specs + c_out, out_shape + c_shapes
        args = args + list(ws)
    return pl.pallas_call(
        body,
        grid=(T // tm,),
        in_specs=in_specs,
        out_specs=out_specs,
        out_shape=out_shape,
        compiler_params=_params(1),
        name="ffn_sublayer",
    )(*args)


PF_AQ, PF_FF, PF_FB, PF_AG = 0, 512, 1024, 1536
PF_BQ, PF_BK, PF_BG, PF_LAF, PF_LAB = 2048, 2304, 2560, 3072, 3328
PF_W = 3584
PB_AV, PB_BV = 0, 512
PB_W = 1024
WM_AQ, WM_FF, WM_FB, WM_AG, WM_BQ, WM_BK, WM_BG, WM_AI, WM_BV = 0, 512, 1024, 1536, 2048, 2304, 2560, 3072, 3584
WM_W = 4096


def _log_sigmoid(x):
    return jnp.minimum(x, 0.0) - jnp.log(1.0 + jnp.exp(-jnp.abs(x)))


def _inproj_body(x_ref, shift_ref, scale_ref, w_ref, wz_ref, gu_ref, gb_ref, lb_ref, pf_ref, pb_ref, *, layer_e):
    def lower_bound(d):
        l = lb_ref[d]
        e = jnp.exp(l - jnp.max(l, axis=0, keepdims=True))
        sm = e / jnp.sum(e, axis=0, keepdims=True)
        return jnp.sum(sm[:layer_e + 1], axis=0, keepdims=True)

    lbs = [lower_bound(0), lower_bound(1)]
    for r in range(0, x_ref.shape[0], PROJ_SUB):
        rows = slice(r, r + PROJ_SUB)
        h = (x_ref[rows, :] * (1.0 + scale_ref[...]) + shift_ref[...]).astype(BF16)

        def proj(off, width):
            return _dot(h, w_ref[:, off:off + width])

        pf_ref[rows, PF_AQ:PF_AQ + A_W] = proj(WM_AQ, A_W)
        for lb, wm, pf in ((lbs[0], WM_FF, PF_FF), (lbs[1], WM_FB, PF_FB)):
            pf_ref[rows, pf:pf + A_W] = lb + (1.0 - lb) * jax.nn.sigmoid(proj(wm, A_W))
        pf_ref[rows, PF_AG:PF_AG + A_W] = _silu(proj(WM_AG, A_W))
        pf_ref[rows, PF_BQ:PF_BQ + B_QK] = proj(WM_BQ, B_QK) * (B_DK ** -0.5)
        pf_ref[rows, PF_BK:PF_BK + B_QK] = proj(WM_BK, B_QK)
        pf_ref[rows, PF_BG:PF_BG + B_V] = _silu(proj(WM_BG, B_V))
        pb_ref[rows, PB_AV:PB_AV + A_W] = _silu(proj(WM_AI, A_W)).astype(BF16)
        pb_ref[rows, PB_BV:PB_BV + B_V] = proj(WM_BV, B_V).astype(BF16)
        z = _dot(h, wz_ref[...]).astype(BF16)
        pre = _dot(z, gu_ref[...]) + gb_ref[...]
        pf_ref[rows, PF_LAF:PF_LAF + 2 * B_QK] = _log_sigmoid(pre) * (1.0 / GLA_TAU)


def _inproj(x, mod, wmain, wz, gup, gb, hgrn_lb, layer, layer_e, tm=INPROJ_TM):
    n_l = hgrn_lb.shape[1]
    return pl.pallas_call(
        functools.partial(_inproj_body, layer_e=layer_e),
        grid=(T // tm,),
        in_specs=[pl.BlockSpec((tm, D), lambda i: (i, 0))] + _mod_specs(layer, (3, 4), tm) + [
            _resident((D, WM_W)),
            _resident((D, LANES)),
            _resident((LANES, 2 * B_QK)),
            _resident((1, 2 * B_QK)),
            _resident((2, n_l, A_W)),
        ],
        out_specs=[
            pl.BlockSpec((tm, PF_W), lambda i: (i, 0)),
            pl.BlockSpec((tm, PB_W), lambda i: (i, 0)),
        ],
        out_shape=[jax.ShapeDtypeStruct((T, PF_W), F32), jax.ShapeDtypeStruct((T, PB_W), BF16)],
        compiler_params=_params(1),
        name="ab_inproj",
    )(x, mod, mod, wmain, wz, gup, gb, hgrn_lb)


SCAN_PROMPT_SEQS = 4
SCAN_UNROLL = 8


def _prefix_rows(x):
    row = lax.broadcasted_iota(jnp.int32, x.shape, 0)
    s = 1
    while s < x.shape[0]:
        x = x + jnp.where(row >= s, pltpu.roll(x, s, axis=0), 0.0)
        s *= 2
    return x


def _scan_body(*refs, seq_len, seqs, pair, has_s0, emit_state):
    n = seq_len // CHUNK
    n_all = seqs * n
    nh = 2 if pair else 1
    it = iter(refs)
    q_ref = next(it)
    if pair:
        k_ref, laf_ref, lab_ref = next(it), next(it), next(it)
    else:
        ff_ref, fb_ref = next(it), next(it)
    g_ref, v_ref, nw_ref = next(it), next(it), next(it)
    s0_ref = next(it) if has_s0 else None
    o_ref = next(it)
    st_ref = next(it) if emit_state else None
    qd_s, oi_s, kv_s, dec_s, sb_s = it

    row = lax.broadcasted_iota(jnp.int32, (CHUNK, CHUNK), 0)
    col = lax.broadcasted_iota(jnp.int32, (CHUNK, CHUNK), 1)
    tril = row >= col
    triu = row <= col
    lane = lax.broadcasted_iota(jnp.int32, (1, LANES), 1)
    lane2 = lax.broadcasted_iota(jnp.int32, (1, 2 * LANES), 1)
    if pair:
        masks = [lane < HALF, lane >= HALF]
        masks2 = [(lane2 % LANES) < HALF, (lane2 % LANES) >= HALF]
    else:
        masks, masks2 = [None], [None]

    def pick(mask, x):
        return x if mask is None else jnp.where(mask, x, jnp.zeros_like(x))

    def rows_of(c):
        return pl.ds(pl.multiple_of(c * CHUNK, CHUNK), CHUNK)

    def loop(body):
        if n_all <= SCAN_UNROLL:
            for c in range(n_all):
                body(c)
        else:
            def fbody(i, carry):
                for u in range(SCAN_UNROLL):
                    body(i * SCAN_UNROLL + u)
                return carry
            lax.fori_loop(0, n_all // SCAN_UNROLL, fbody, 0)

    def phase1(c):
        rows = rows_of(c)
        q = q_ref[rows, :]
        if pair:
            k_f = k_b = k_ref[rows, :]
            la_f, la_b = laf_ref[rows, :], lab_ref[rows, :]
        else:
            f_f, f_b = ff_ref[rows, :], fb_ref[rows, :]
            k_f, k_b = 1.0 - f_f, 1.0 - f_b
            la_f, la_b = jnp.log(f_f), jnp.log(f_b)
        cs = _prefix_rows(jnp.concatenate([la_f, la_b], axis=1))
        cf, cbi = cs[:, :LANES], cs[:, LANES:]
        tot_f, tot_b = cf[CHUNK - 1:CHUNK, :], cbi[CHUNK - 1:CHUNK, :]
        rb = tot_b - cbi + la_b
        ref_f, ref_b = cf[CHUNK // 2 - 1:CHUNK // 2, :], rb[CHUNK // 2:CHUNK // 2 + 1, :]
        qtf = q * jnp.exp(cf - ref_f)
        qtb = q * jnp.exp(rb - ref_b)
        ktf = k_f * jnp.exp(ref_f - cf)
        ktb = k_b * jnp.exp(ref_b - rb)
        qd = jnp.concatenate([qtf * jnp.exp(ref_f), qtb * jnp.exp(ref_b)], axis=1).astype(BF16)
        ku = jnp.concatenate([ktf * jnp.exp(tot_f - ref_f), ktb * jnp.exp(tot_b - ref_b)], axis=1).astype(BF16)
        qd_s[rows, :] = qd
        qt = jnp.concatenate([qtf, qtb], axis=0).astype(BF16)
        kt = jnp.concatenate([ktf, ktb], axis=0).astype(BF16)
        kv = None
        for hh in range(nh):
            v = v_ref[rows, hh * LANES:(hh + 1) * LANES]
            sc = _dot_nt(pick(masks[hh], qt), kt)
            att = jnp.where(tril, sc[:CHUNK, :CHUNK], 0.0) + jnp.where(triu, sc[CHUNK:, CHUNK:], 0.0)
            oi_s[rows, hh * LANES:(hh + 1) * LANES] = _dot(att.astype(BF16), v)
            kv_h = _dot_tn(v, ku)
            kv = kv_h if kv is None else jnp.where(masks2[0], kv, kv_h)
        kv_s[c] = kv
        dec_s[c] = jnp.exp(jnp.concatenate([tot_f, tot_b], axis=1))

    loop(phase1)

    def recurrence(sq, d, reverse):
        cols = slice(d * LANES, (d + 1) * LANES)
        c0 = sq * n
        st0 = s0_ref[sq, d, 0].T if has_s0 else jnp.zeros((LANES, LANES), F32)

        def step(c, st):
            sb_s[c, :, cols] = st.astype(BF16)
            return st * dec_s[c, :, cols] + kv_s[c, :, cols]

        if n <= 8:
            st = st0
            for c in (range(n - 1, -1, -1) if reverse else range(n)):
                st = step(c0 + c, st)
        else:
            st = lax.fori_loop(0, n, lambda i, st: step(c0 + (n - 1 - i if reverse else i), st), st0)
        if emit_state:
            st_ref[sq, d, 0] = st.T

    for sq in range(seqs):
        recurrence(sq, 0, False)
        recurrence(sq, 1, True)

    nw = nw_ref[...]

    def phase2(c):
        rows = rows_of(c)
        qcat = qd_s[rows, :]
        scat = sb_s[c]
        for hh in range(nh):
            cols = slice(hh * LANES, (hh + 1) * LANES)
            o = oi_s[rows, cols] + _dot_nt(pick(masks2[hh], qcat), scat)
            o = o * lax.rsqrt(jnp.mean(o * o, axis=-1, keepdims=True) + RMS_EPS) * nw
            o_ref[rows, cols] = (o * g_ref[rows, cols]).astype(BF16)

    loop(phase2)


def _scan(pf, pb, norm_w, s0, *, prompt, pair):
    seq_len = SEQ if prompt else DEC_SEQ
    nseq = BATCH if prompt else DEC_BATCH
    seqs = SCAN_PROMPT_SEQS if prompt else 1
    rows = seqs * seq_len
    row_off = 0 if prompt else TP // rows
    units = B_HEADS // 2 if pair else A_HEADS
    nh = 2 if pair else 1
    n_all = rows // CHUNK
    has_s0 = s0 is not None
    emit_state = prompt

    def colspec(off, width=LANES):
        base = off // width
        return pl.BlockSpec((rows, width), lambda s, u: (s + row_off, base + u))

    if pair:
        in_specs = [colspec(PF_BQ), colspec(PF_BK), colspec(PF_LAF), colspec(PF_LAB),
                    colspec(PF_BG, 2 * LANES), colspec(PB_BV, 2 * LANES)]
        args = [pf, pf, pf, pf, pf, pb]
    else:
        in_specs = [colspec(PF_AQ), colspec(PF_FF), colspec(PF_FB), colspec(PF_AG), colspec(PB_AV)]
        args = [pf, pf, pf, pf, pb]
    in_specs.append(pl.BlockSpec((1, LANES), lambda s, u: (0, 0)))
    args.append(norm_w.reshape(1, LANES))
    state_spec = pl.BlockSpec((seqs, 2, 1, LANES, LANES), lambda s, u: (s, 0, u, 0, 0))
    if has_s0:
        in_specs.append(state_spec)
        args.append(s0)
    out_specs = [pl.BlockSpec((rows, nh * LANES), lambda s, u: (s, u))]
    out_shape = [jax.ShapeDtypeStruct((nseq * seq_len, units * nh * LANES), BF16)]
    if emit_state:
        out_specs.append(state_spec)
        out_shape.append(jax.ShapeDtypeStruct((nseq, 2, units, LANES, LANES), F32))
    scratch = [
        pltpu.VMEM((rows, 2 * LANES), BF16),
        pltpu.VMEM((rows, nh * LANES), F32),
        pltpu.VMEM((n_all, LANES, 2 * LANES), F32),
        pltpu.VMEM((n_all, 1, 2 * LANES), F32),
        pltpu.VMEM((n_all, LANES, 2 * LANES), BF16),
    ]
    return pl.pallas_call(
        functools.partial(_scan_body, seq_len=seq_len, seqs=seqs, pair=pair, has_s0=has_s0,
                          emit_state=emit_state),
        grid=(nseq // seqs, units),
        in_specs=in_specs,
        out_specs=out_specs,
        out_shape=out_shape,
        scratch_shapes=scratch,
        compiler_params=_params(2),
        name=f"scan_{'p' if prompt else 's'}_{'gla' if pair else 'hgrn'}",
    )(*args)


def _outproj_body(*refs, n_lhs, tm):
    lhs = refs[:2 * n_lhs]
    ws = refs[2 * n_lhs:3 * n_lhs]
    x_ref, m_ref, g_ref, b_ref, o_ref = refs[3 * n_lhs:]

    def compute(group):
        for r in range(0, tm, PROJ_SUB):
            rows = slice(r, r + PROJ_SUB)
            y = _dot(lhs[group][rows, :], ws[0][...])
            for j in range(1, n_lhs):
                y = y + _dot(lhs[2 * j + group][rows, :], ws[j][...])
            z = ALPHA * x_ref[rows, :] + m_ref[...] * y
            o_ref[rows, :] = _layer_norm(z, g_ref[...], b_ref[...])

    in_prompt = pl.program_id(0) < TP // tm
    pl.when(in_prompt)(lambda: compute(0))
    pl.when(jnp.logical_not(in_prompt))(lambda: compute(1))


def _outproj(lhs, ws, x, mod, ln_g, ln_b, layer, tm=OUT_TM):
    n_lhs = len(lhs)
    in_specs, args = [], []
    for a_p, a_s in lhs:
        in_specs += _group_specs(True, tm, a_p.shape[1])
        args += [a_p, a_s]
    in_specs += [_resident(w.shape) for w in ws]
    in_specs += [pl.BlockSpec((tm, D), lambda i: (i, 0))] + _mod_specs(layer, (5,), tm) + _ln_specs(layer, 1)
    return pl.pallas_call(
        functools.partial(_outproj_body, n_lhs=n_lhs, tm=tm),
        grid=(T // tm,),
        in_specs=in_specs,
        out_specs=pl.BlockSpec((tm, D), lambda i: (i, 0)),
        out_shape=jax.ShapeDtypeStruct((T, D), F32),
        compiler_params=_params(1),
        name="mixer_outproj",
    )(*args, *ws, x, mod, ln_g, ln_b)


def _rope_partner(x):
    lane = lax.broadcasted_iota(jnp.int32, x.shape, 1)
    first_half = (lane % (2 * ROPE_FREQS)) < ROPE_FREQS
    return jnp.where(first_half, pltpu.roll(x, LANES - ROPE_FREQS, axis=1), pltpu.roll(x, ROPE_FREQS, axis=1))


def _qkv_body(x_ref, shift_ref, scale_ref, w_ref, wkvt_ref, cos_ref, sin_ref,
              q_ref, k_ref, v_ref, kt_ref, vt_ref):
    qscale = C_HEAD_DIM ** -0.5
    in_prompt = pl.program_id(0) < TP // PROJ_TM

    def modulated(rows):
        return (x_ref[rows, :] * (1.0 + scale_ref[...]) + shift_ref[...]).astype(BF16)

    @pl.when(in_prompt)
    def _():
        for sq in range(PROJ_TM // SEQ):
            rows = slice(sq * SEQ, (sq + 1) * SEQ)
            h = modulated(rows)
            q_ref[rows, :] = (_dot(h, w_ref[:, :C_Q]) * qscale).astype(BF16)
            v_ref[rows, :] = _dot(h, w_ref[:, C_Q + C_KV:]).astype(BF16)
            kt_ref[sq] = _dot_nt(wkvt_ref[:C_KV, :], h)
            vt_ref[sq] = _dot_nt(wkvt_ref[C_KV:, :], h)

    @pl.when(jnp.logical_not(in_prompt))
    def _():
        for sq in range(PROJ_TM // SEQ):
            rows = slice(sq * SEQ, (sq + 1) * SEQ)
            h = modulated(rows)
            cos, sin = cos_ref[rows, :], sin_ref[rows, :]
            zq = _dot(h, w_ref[:, :C_Q])
            zk = _dot(h, w_ref[:, C_Q:C_Q + C_KV])
            v_ref[rows, :] = _dot(h, w_ref[:, C_Q + C_KV:]).astype(BF16)

            def rope(z, cos=cos, sin=sin):
                return z * cos + _rope_partner(z) * sin

            for j in range(C_Q // LANES):
                cols = slice(j * LANES, (j + 1) * LANES)
                q_ref[rows, cols] = (rope(zq[:, cols]) * qscale).astype(BF16)
            for j in range(C_KV // LANES):
                cols = slice(j * LANES, (j + 1) * LANES)
                k_ref[rows, cols] = rope(zk[:, cols]).astype(BF16)


def _qkv(x, mod, w, wkvt, cos, sin, layer):
    tm = PROJ_TM
    n_p = TP // tm
    lat = lambda i: (jnp.maximum(i - n_p, 0), 0)
    ctx = lambda i: (jnp.minimum(i, n_p - 1), 0, 0)
    return pl.pallas_call(
        _qkv_body,
        grid=(T // tm,),
        in_specs=[pl.BlockSpec((tm, D), lambda i: (i, 0))] + _mod_specs(layer, (3, 4), tm) + [
            _resident((D, C_Q + 2 * C_KV)),
            _resident((2 * C_KV, D)),
            pl.BlockSpec((tm, LANES), lat),
            pl.BlockSpec((tm, LANES), lat),
        ],
        out_specs=[
            pl.BlockSpec((tm, C_Q), lambda i: (i, 0)),
            pl.BlockSpec((tm, C_KV), lat),
            pl.BlockSpec((tm, C_KV), lambda i: (i, 0)),
            pl.BlockSpec((tm // SEQ, C_KV, SEQ), ctx),
            pl.BlockSpec((tm // SEQ, C_KV, SEQ), ctx),
        ],
        out_shape=[jax.ShapeDtypeStruct((T, C_Q), BF16), jax.ShapeDtypeStruct((TS, C_KV), BF16),
                   jax.ShapeDtypeStruct((T, C_KV), BF16),
                   jax.ShapeDtypeStruct((BATCH, C_KV, SEQ), F32), jax.ShapeDtypeStruct((BATCH, C_KV, SEQ), F32)],
        compiler_params=_params(1),
        name="c_qkv",
    )(x, mod, mod, w, wkvt, cos, sin)


def _rope_tables():
    t = np.arange(DEC_SEQ)
    pos = np.stack([t // GRID_W, t % GRID_W], axis=1).astype(np.float32)
    inv = (ROPE_BASE ** (-np.arange(ROPE_FREQS, dtype=np.float32) / ROPE_FREQS)).astype(np.float32)
    d = np.arange(C_HEAD_DIM)
    axis = d // (2 * ROPE_FREQS)
    ang = pos[:, axis] * inv[d % ROPE_FREQS][None, :]
    sign = np.where((d % (2 * ROPE_FREQS)) < ROPE_FREQS, -1.0, 1.0)[None, :]
    cos_h, sin_h = np.cos(ang), np.sin(ang) * sign
    reps = LANES // C_HEAD_DIM
    cos = np.tile(np.tile(cos_h, (1, reps)), (DEC_BATCH, 1)).astype(np.float32)
    sin = np.tile(np.tile(sin_h, (1, reps)), (DEC_BATCH, 1)).astype(np.float32)
    return jnp.asarray(cos), jnp.asarray(sin)


def _dup_head(blk, half):
    lane = lax.broadcasted_iota(jnp.int32, blk.shape, 1)
    keep = (lane >= HALF).astype(jnp.int32) == half
    return jnp.where(keep, blk, pltpu.roll(blk, HALF, axis=1))


def _pair_values(v, half):
    lo = lax.broadcasted_iota(jnp.int32, v.shape, 1) < HALF
    v2 = _dup_head(v.astype(F32), half)
    top = jnp.concatenate([jnp.where(lo, v2, 0.0), jnp.where(lo, 1.0, 0.0)], axis=1)
    bot = jnp.concatenate([jnp.where(lo, 0.0, v2), jnp.where(lo, 0.0, 1.0)], axis=1)
    return top.astype(BF16), bot.astype(BF16)


def _softmax_weights(scores, sink):
    mx = sink
    for s in scores:
        mx = jnp.maximum(mx, jnp.max(s, axis=-1, keepdims=True))
    return [jnp.exp(s - mx).astype(BF16) for s in scores], jnp.exp(sink - mx)


def _pair_sinks(sink_ref, jb):
    return (sink_ref[:, jb * LANES:jb * LANES + 1], sink_ref[:, jb * LANES + HALF:jb * LANES + HALF + 1])


KV_PER_BLOCK = LANES // C_HEAD_DIM
TILE_KV = C_KV_HEADS
TILE_KVW = TILE_KV * C_HEAD_DIM
CTX_TILES = BATCH * C_KV_HEADS // TILE_KV
CTX_BLOCKS = TILE_KV * C_GROUPS // 2
PIPE_LAG = 2


def _kv_block(ref, kv):
    lb = kv // KV_PER_BLOCK
    return ref[:, lb * LANES:(lb + 1) * LANES], kv % KV_PER_BLOCK


def _run_pipeline(g, n_tiles, stages):
    for cur in (0, 1):
        steady = jnp.logical_and(g % 2 == cur, jnp.logical_and(g >= PIPE_LAG, g < n_tiles))
        pl.when(steady)(functools.partial(stages, cur, 1 - cur, True, True, True))
    pl.when(g == 0)(functools.partial(stages, 0, 1, True, False, False))
    pl.when(g == 1)(functools.partial(stages, 1, 0, True, True, False))
    pl.when(g == n_tiles)(functools.partial(stages, n_tiles % 2, 1 - n_tiles % 2, False, True, True))
    pl.when(g == n_tiles + 1)(functools.partial(stages, 1 - n_tiles % 2, n_tiles % 2, False, False, True))


def _ctx_tile(g, lag):
    t = jnp.clip(g - lag, 0, CTX_TILES - 1)
    n = C_KV_HEADS // TILE_KV
    return t // n, t % n


def _ctx_attn_body(q_ref, kt_ref, v_ref, sink_ref, o_ref, s_s, e_s, st_s):
    g = pl.program_id(0)
    lo = lax.broadcasted_iota(jnp.int32, (1, LANES), 1) < HALF
    pairs_per_kv = C_GROUPS // 2

    def stages(cur, prev, scores, weights, values):
        for kv in range(TILE_KV if values else 0):
            v_top, v_bot = _pair_values(*_kv_block(v_ref, kv))
            for jb in range(kv * pairs_per_kv, (kv + 1) * pairs_per_kv):
                od = _dot(e_s[cur, 2 * jb], v_top) + _dot(e_s[cur, 2 * jb + 1], v_bot)
                o_ref[:, jb * LANES:(jb + 1) * LANES] = (od[:, :LANES] / (od[:, LANES:] + st_s[cur, jb])).astype(BF16)
        for jb in range(CTX_BLOCKS if weights else 0):
            terms = []
            for hh, sink in enumerate(_pair_sinks(sink_ref, jb)):
                (e,), term = _softmax_weights([s_s[prev, 2 * jb + hh]], sink)
                e_s[prev, 2 * jb + hh] = e
                terms.append(term)
            st_s[prev, jb] = jnp.where(lo, terms[0], terms[1])
        for kv in range(TILE_KV if scores else 0):
            kt = kt_ref[kv * C_HEAD_DIM:(kv + 1) * C_HEAD_DIM, :]
            zero = jnp.zeros_like(kt)
            kt_a = jnp.concatenate([kt, zero], axis=0).astype(BF16)
            kt_b = jnp.concatenate([zero, kt], axis=0).astype(BF16)
            for jb in range(kv * pairs_per_kv, (kv + 1) * pairs_per_kv):
                q = q_ref[:, jb * LANES:(jb + 1) * LANES]
                s_s[cur, 2 * jb] = _dot(q, kt_a)
                s_s[cur, 2 * jb + 1] = _dot(q, kt_b)

    _run_pipeline(g, CTX_TILES, stages)


def _ctx_attn(q, kt, v, sink_cols):
    gw = CTX_BLOCKS * LANES

    def at(lag, fn):
        return lambda g: fn(*_ctx_tile(g, lag))

    return pl.pallas_call(
        _ctx_attn_body,
        grid=(CTX_TILES + PIPE_LAG,),
        in_specs=[
            pl.BlockSpec((SEQ, gw), at(0, lambda b, p: (b, p))),
            pl.BlockSpec((None, TILE_KVW, SEQ), at(0, lambda b, p: (b, p, 0))),
            pl.BlockSpec((SEQ, TILE_KVW), at(2, lambda b, p: (b, p))),
            pl.BlockSpec((1, gw), at(1, lambda b, p: (0, p))),
        ],
        out_specs=pl.BlockSpec((SEQ, gw), at(2, lambda b, p: (b, p))),
        out_shape=jax.ShapeDtypeStruct((TP, C_Q), BF16),
        scratch_shapes=[
            pltpu.VMEM((2, 2 * CTX_BLOCKS, SEQ, SEQ), F32),
            pltpu.VMEM((2, 2 * CTX_BLOCKS, SEQ, SEQ), BF16),
            pltpu.VMEM((2, CTX_BLOCKS, SEQ, LANES), F32),
        ],
        compiler_params=_params(1),
        name="c_attn_ctx",
    )(q, kt, v, sink_cols)


LAT_TQ = 128
LAT_WIN = LAT_TQ + 2 * WINDOW
LAT_NQ = DEC_SEQ // LAT_TQ
LAT_KVB = C_KV_HEADS // TILE_KV
LAT_TILES = DEC_BATCH * LAT_KVB * LAT_NQ


def _lat_tile(g, lag):
    t = jnp.clip(g - lag, 0, LAT_TILES - 1)
    bp = t // LAT_NQ
    return bp // LAT_KVB, bp % LAT_KVB, t % LAT_NQ


def _lat_window(j):
    return pl.multiple_of(jnp.clip(j * LAT_TQ - WINDOW, 0, DEC_SEQ - LAT_WIN), LANES)


def _lat_attn_body(q_ref, k_ref, v_ref, ck_ref, cv_ref, sink_ref, o_ref,
                   ka_s, kb_s, cka_s, ckb_s, va_s, vb_s, cva_s, cvb_s, sl_s, sc_s, el_s, ec_s, st_s):
    g = pl.program_id(0)
    lo = lax.broadcasted_iota(jnp.int32, (1, LANES), 1) < HALF
    pairs_per_kv = C_GROUPS // 2
    b_a, p_a, j_a = _lat_tile(g, 0)
    b_c, p_c, j_c = _lat_tile(g, 2)
    v_slot_a = (b_a * LAT_KVB + p_a) % 2
    v_slot_c = (b_c * LAT_KVB + p_c) % 2

    @pl.when(jnp.logical_and(j_a == 0, g < LAT_TILES))
    def _():
        for kv in range(TILE_KV):
            for src, a_s, b_s in ((k_ref, ka_s, kb_s), (ck_ref, cka_s, ckb_s)):
                blk, kh = _kv_block(src, kv)
                lo2 = lax.broadcasted_iota(jnp.int32, blk.shape, 1) < HALF
                k2 = _dup_head(blk.astype(F32), kh)
                a_s[kv] = jnp.where(lo2, k2, 0.0).astype(BF16)
                b_s[kv] = jnp.where(lo2, 0.0, k2).astype(BF16)
            va_s[v_slot_a, kv], vb_s[v_slot_a, kv] = _pair_values(*_kv_block(v_ref, kv))
            cva_s[v_slot_a, kv], cvb_s[v_slot_a, kv] = _pair_values(*_kv_block(cv_ref, kv))

    def stages(cur, prev, scores, weights, values):
        win_c = pl.ds(_lat_window(j_c), LAT_WIN)
        for kv in range(TILE_KV if values else 0):
            v_loc = (va_s[v_slot_c, kv, win_c, :], vb_s[v_slot_c, kv, win_c, :])
            v_ctx = (cva_s[v_slot_c, kv], cvb_s[v_slot_c, kv])
            for jb in range(kv * pairs_per_kv, (kv + 1) * pairs_per_kv):
                od = None
                for hh in range(2):
                    part = _dot(el_s[cur, 2 * jb + hh], v_loc[hh]) + _dot(ec_s[cur, 2 * jb + hh], v_ctx[hh])
                    od = part if od is None else od + part
                o_ref[:, jb * LANES:(jb + 1) * LANES] = (od[:, :LANES] / (od[:, LANES:] + st_s[cur, jb])).astype(BF16)
        for jb in range(CTX_BLOCKS if weights else 0):
            terms = []
            for hh, sink in enumerate(_pair_sinks(sink_ref, jb)):
                (e_loc, e_ctx), term = _softmax_weights([sl_s[prev, 2 * jb + hh], sc_s[prev, 2 * jb + hh]], sink)
                el_s[prev, 2 * jb + hh] = e_loc
                ec_s[prev, 2 * jb + hh] = e_ctx
                terms.append(term)
            st_s[prev, jb] = jnp.where(lo, terms[0], terms[1])
        start = _lat_window(j_a)
        win = pl.ds(start, LAT_WIN)
        qpos = j_a * LAT_TQ + lax.broadcasted_iota(jnp.int32, (LAT_TQ, LAT_WIN), 0)
        kpos = start + lax.broadcasted_iota(jnp.int32, (LAT_TQ, LAT_WIN), 1)
        band = jnp.abs(qpos - kpos) <= WINDOW
        for kv in range(TILE_KV if scores else 0):
            for jb in range(kv * pairs_per_kv, (kv + 1) * pairs_per_kv):
                q = q_ref[:, jb * LANES:(jb + 1) * LANES]
                for hh, (kl, kc) in enumerate(((ka_s, cka_s), (kb_s, ckb_s))):
                    sl_s[cur, 2 * jb + hh] = jnp.where(band, _dot_nt(q, kl[kv, win, :]), -jnp.inf)
                    sc_s[cur, 2 * jb + hh] = _dot_nt(q, kc[kv])

    _run_pipeline(g, LAT_TILES, stages)


def _lat_attn(q, k, v, ck, cv, sink_cols):
    gw = CTX_BLOCKS * LANES
    q_off = TP // LAT_TQ
    kv_off = TP // DEC_SEQ
    kvb, n_heads = TILE_KV, 2 * CTX_BLOCKS

    def at(lag, fn):
        return lambda g: fn(*_lat_tile(g, lag))

    return pl.pallas_call(
        _lat_attn_body,
        grid=(LAT_TILES + PIPE_LAG,),
        in_specs=[
            pl.BlockSpec((LAT_TQ, gw), at(0, lambda b, p, j: (q_off + b * LAT_NQ + j, p))),
            pl.BlockSpec((DEC_SEQ, TILE_KVW), at(0, lambda b, p, j: (b, p))),
            pl.BlockSpec((DEC_SEQ, TILE_KVW), at(0, lambda b, p, j: (kv_off + b, p))),
            pl.BlockSpec((PAST_LEN, TILE_KVW), at(0, lambda b, p, j: (b, p))),
            pl.BlockSpec((PAST_LEN, TILE_KVW), at(0, lambda b, p, j: (b, p))),
            pl.BlockSpec((1, gw), at(1, lambda b, p, j: (0, p))),
        ],
        out_specs=pl.BlockSpec((LAT_TQ, gw), at(2, lambda b, p, j: (b * LAT_NQ + j, p))),
        out_shape=jax.ShapeDtypeStruct((TS, C_Q), BF16),
        scratch_shapes=[
            pltpu.VMEM((kvb, DEC_SEQ, LANES), BF16), pltpu.VMEM((kvb, DEC_SEQ, LANES), BF16),
            pltpu.VMEM((kvb, PAST_LEN, LANES), BF16), pltpu.VMEM((kvb, PAST_LEN, LANES), BF16),
            pltpu.VMEM((2, kvb, DEC_SEQ, 2 * LANES), BF16), pltpu.VMEM((2, kvb, DEC_SEQ, 2 * LANES), BF16),
            pltpu.VMEM((2, kvb, PAST_LEN, 2 * LANES), BF16), pltpu.VMEM((2, kvb, PAST_LEN, 2 * LANES), BF16),
            pltpu.VMEM((2, n_heads, LAT_TQ, LAT_WIN), F32), pltpu.VMEM((2, n_heads, LAT_TQ, PAST_LEN), F32),
            pltpu.VMEM((2, n_heads, LAT_TQ, LAT_WIN), BF16), pltpu.VMEM((2, n_heads, LAT_TQ, PAST_LEN), BF16),
            pltpu.VMEM((2, CTX_BLOCKS, LAT_TQ, LANES), F32),
        ],
        compiler_params=_params(1),
        name="c_attn_latent",
    )(q, k, v, ck, cv, sink_cols)


def kernel(x_prompt, x_sample, state_hgrn, state_gla, cache_k, cache_v, c, c_ctx, w_mod, b_mod, ln_g, ln_b,
           ffn_w1, ffn_w3, ffn_w2, w_in_ab, hgrn_lb, gla_gate_up, gla_gate_b, norm_a, norm_b, w_out_ab,
           w_qkv_c, sink_c, w_out_c):
    cs = jnp.zeros((8, D), F32).at[0].set(c_ctx).at[1:1 + DEC_BATCH].set(c)
    mod = _mod_vectors(cs, w_mod, b_mod).reshape(DEPTH, 8, 1, N_MOD * D)
    ffn_ws = (ffn_w1, ffn_w3, ffn_w2)
    ln_g, ln_b = ln_g.reshape(DEPTH, 3, 1, D), ln_b.reshape(DEPTH, 3, 1, D)

    def ffn(xs, ws, layer, sub, split_out=False, cast_next=None):
        casts = None if cast_next is None else (ffn_ws, cast_next)
        return _ffn_sublayer(xs, mod, *ws, ln_g, ln_b, layer, sub, split_out=split_out, casts=casts)

    x, *ws_01 = ffn([x_prompt.reshape(TP, D), x_sample.reshape(TS, D)], [w[0, 0].astype(BF16) for w in ffn_ws],
                    0, 0, cast_next=(0, 1))
    w_in = w_in_ab[0]
    o_aq, o_ai, o_ff, o_fb, o_ag = 0, A_W, 2 * A_W, 3 * A_W, 4 * A_W
    o_bq = 5 * A_W
    o_bk, o_bv = o_bq + B_QK, o_bq + 2 * B_QK
    o_bg = o_bv + B_V
    o_z = o_bg + B_V
    order = [(o_aq, A_W), (o_ff, A_W), (o_fb, A_W), (o_ag, A_W), (o_bq, B_QK), (o_bk, B_QK), (o_bg, B_V),
             (o_ai, A_W), (o_bv, B_V)]
    wmain = jnp.concatenate([w_in[:, o:o + w] for o, w in order], axis=1).astype(BF16)
    wz = jnp.pad(w_in[:, o_z:o_z + 2 * GATE_RANK], ((0, 0), (0, LANES - 2 * GATE_RANK))).astype(BF16)
    gup = jnp.zeros((LANES, 2 * B_QK), F32)
    gup = gup.at[:GATE_RANK, :B_QK].set(gla_gate_up[0, 0]).at[GATE_RANK:2 * GATE_RANK, B_QK:].set(gla_gate_up[0, 1])
    gb = gla_gate_b[0].reshape(1, 2 * B_QK)
    pf, pb = _inproj(x, mod, wmain, wz, gup.astype(BF16), gb, hgrn_lb, 0, 0)

    s0_a = state_hgrn[:, 0]
    s0_b = state_gla[:, 0].reshape(DEC_BATCH, 2, B_HEADS // 2, LANES, B_DV)
    oa_p, st_a = _scan(pf, pb, norm_a[0], None, prompt=True, pair=False)
    ob_p, st_b = _scan(pf, pb, norm_b[0], None, prompt=True, pair=True)
    (oa_s,) = _scan(pf, pb, norm_a[0], s0_a, prompt=False, pair=False)
    (ob_s,) = _scan(pf, pb, norm_b[0], s0_b, prompt=False, pair=True)
    w_out = w_out_ab[0].astype(BF16)
    x = _outproj([(oa_p, oa_s), (ob_p, ob_s)], [w_out[:A_W], w_out[A_W:]], x, mod, ln_g, ln_b, 0)
    x, *ws_10 = ffn([x], ws_01, 0, 1, cast_next=(1, 0))
    new_hgrn = st_a.reshape(BATCH, 1, 2, A_HEADS, A_DK, A_DV)
    new_gla = st_b.reshape(BATCH, 1, 2, B_HEADS, B_DK, B_DV)

    x, *ws_11 = ffn([x], ws_10, 1, 0, cast_next=(1, 1))
    cos, sin = _rope_tables()
    w_qkv = w_qkv_c[0].astype(BF16)
    q, k, v, kt, vt = _qkv(x, mod, w_qkv, w_qkv[:, C_Q:].T, cos, sin, 1)
    sink_cols = jnp.repeat(sink_c[0], C_HEAD_DIM).reshape(1, C_Q)
    o_p = _ctx_attn(q, kt, v, sink_cols)
    ck = cache_k[:, 0].reshape(DEC_BATCH * PAST_LEN, C_KV)
    cv = cache_v[:, 0].reshape(DEC_BATCH * PAST_LEN, C_KV)
    o_s = _lat_attn(q, k, v, ck, cv, sink_cols)
    x = _outproj([(o_p, o_s)], [w_out_c[0].astype(BF16)], x, mod, ln_g, ln_b, 1)
    y_p, y_s = ffn([x], ws_11, 1, 1, split_out=True)

    def cache_layout(zt):
        return zt.reshape(BATCH, 1, C_KV_HEADS, C_HEAD_DIM, SEQ).transpose(0, 1, 4, 2, 3)

    new_k, new_v = cache_layout(kt), cache_layout(vt)

    return (y_p.reshape(BATCH, SEQ, D), y_s.reshape(DEC_BATCH, DEC_SEQ, D), new_hgrn, new_gla, new_k, new_v)
```

```python
import functools
import math

import jax
import jax.numpy as jnp
import numpy as np
from jax import lax
from jax.experimental import pallas as pl
from jax.experimental.pallas import tpu as pltpu

D = 1024
BATCH, SEQ = 16, 256
DEC_BATCH, DEC_SEQ = 2, 2048
PAST_LEN = 512
GRID_W = 64
D_FF = 2816
N_MOD = 9
A_HEADS, A_DK, A_DV = 4, 128, 128
A_W = A_HEADS * A_DK
B_HEADS, B_DK, B_DV = 4, 64, 128
B_QK = B_HEADS * B_DK
B_V = B_HEADS * B_DV
GATE_RANK = 16
GLA_TAU = 16.0
CHUNK = 128
C_HEADS, C_KV_HEADS, C_HEAD_DIM = 16, 4, 64
C_GROUPS = C_HEADS // C_KV_HEADS
C_Q = C_HEADS * C_HEAD_DIM
C_KV = C_KV_HEADS * C_HEAD_DIM
WINDOW = 128
ROPE_FREQS = C_HEAD_DIM // 4
ROPE_BASE = 10000.0
DEPTH = 2
ALPHA = (2.0 * DEPTH) ** 0.25
LN_EPS = 1e-5
RMS_EPS = 1e-6

TP = BATCH * SEQ
TS = DEC_BATCH * DEC_SEQ
T = TP + TS
N_SEG = 1 + DEC_BATCH

LANES = 128
HALF = LANES // 2
FFN_TM = 1024
FFN_TM_SPLIT = 512
FFN_SUB = 256
PROJ_TM = 4 * SEQ
INPROJ_TM = 512
PROJ_SUB = 256
OUT_TM = 1024
VMEM_LIMIT = 60 * 1024 * 1024

F32 = jnp.float32
BF16 = jnp.bfloat16


def _dot(a, b):
    return jnp.dot(a, b, preferred_element_type=F32)


def _dot_nt(a, b):
    return lax.dot_general(a, b, (((1,), (1,)), ((), ())), preferred_element_type=F32)


def _dot_tn(a, b):
    return lax.dot_general(a, b, (((0,), (0,)), ((), ())), preferred_element_type=F32)


def _silu(x):
    return x * jax.nn.sigmoid(x)


def _layer_norm(z, g, b):
    mu = jnp.mean(z, axis=-1, keepdims=True)
    zc = z - mu
    var = jnp.mean(zc * zc, axis=-1, keepdims=True)
    return zc * lax.rsqrt(var + LN_EPS) * g + b


def _seg_of_tile(i, tm):
    n_p = TP // tm
    n_s = DEC_SEQ // tm
    return jnp.where(i < n_p, 0, 1 + lax.div(jnp.maximum(i - n_p, 0), n_s))


def _params(n_axes):
    return pltpu.CompilerParams(dimension_semantics=("arbitrary",) * n_axes, vmem_limit_bytes=VMEM_LIMIT)


def _resident(shape):
    nd = len(shape)
    return pl.BlockSpec(shape, lambda *_: (0,) * nd, pipeline_mode=pl.Buffered(1))


def _resident_slice(shape, lead):
    block = (None,) * len(lead) + tuple(shape)
    return pl.BlockSpec(block, lambda *_: tuple(lead) + (0,) * len(shape), pipeline_mode=pl.Buffered(1))


def _mod_specs(layer, cols, tm):
    return [pl.BlockSpec((None, None, 1, D), functools.partial(
        lambda i, c: (layer, _seg_of_tile(i, tm), 0, c), c=c)) for c in cols]


def _ln_specs(layer, idx):
    return [_resident_slice((1, D), (layer, idx))] * 2


BF16_SUBLANES = 16


def _cast_plan(ws, lead, n_steps, step_of):
    in_specs, out_specs, out_shapes = [], [], []
    for w in ws:
        rows, cols = w.shape[len(lead):]
        blk = next(b for b in range(BF16_SUBLANES, rows + 1, BF16_SUBLANES)
                   if rows % b == 0 and rows // b <= n_steps)
        last = rows // blk - 1
        in_specs.append(pl.BlockSpec((None,) * len(lead) + (blk, cols), functools.partial(
            lambda *g, last: tuple(lead) + (jnp.minimum(step_of(*g), last), 0), last=last)))
        out_specs.append(pl.BlockSpec((blk, cols), functools.partial(
            lambda *g, last: (jnp.minimum(step_of(*g), last), 0), last=last)))
        out_shapes.append(jax.ShapeDtypeStruct((rows, cols), BF16))
    return in_specs, out_specs, out_shapes


def _hosting_casts(body, n_in, n_out, n_cast):
    def hosted(*refs, **kw):
        ins, refs = refs[:n_in], refs[n_in:]
        cast_in, refs = refs[:n_cast], refs[n_cast:]
        outs, refs = refs[:n_out], refs[n_out:]
        cast_out, scratch = refs[:n_cast], refs[n_cast:]
        for src, dst in zip(cast_in, cast_out):
            dst[...] = src[...].astype(BF16)
        body(*ins, *outs, *scratch, **kw)
    return hosted


def _mod_body(c_ref, w_ref, b_ref, o_ref):
    c = c_ref[...]
    s = _silu(c).astype(BF16)
    o_ref[0] = _dot(s, w_ref[0].astype(BF16)) + b_ref[0]


def _mod_vectors(cs, w_mod, b_mod):
    tn = 1536
    n = N_MOD * D
    return pl.pallas_call(
        _mod_body,
        grid=(DEPTH, n // tn),
        in_specs=[
            pl.BlockSpec((8, D), lambda l, j: (0, 0)),
            pl.BlockSpec((1, D, tn), lambda l, j: (l, 0, j)),
            pl.BlockSpec((1, 1, tn), lambda l, j: (l, 0, j)),
        ],
        out_specs=pl.BlockSpec((1, 8, tn), lambda l, j: (l, 0, j)),
        out_shape=jax.ShapeDtypeStruct((DEPTH, 8, n), F32),
        compiler_params=_params(2),
        name="mod_vectors",
    )(cs, w_mod, b_mod.reshape(DEPTH, 1, n))


def _ffn_body(*refs, n_x, n_o, tm):
    x_refs = refs[:n_x]
    shift_ref, scale_ref, gate_ref, w1_ref, w3_ref, w2_ref, g_ref, b_ref = refs[n_x:n_x + 8]
    o_refs = refs[n_x + 8:]

    def compute(x_ref, o_ref):
        shift, scale, gate = shift_ref[...], scale_ref[...], gate_ref[...]
        for r in range(0, tm, FFN_SUB):
            rows = slice(r, r + FFN_SUB)
            x = x_ref[rows, :]
            h = (x * (1.0 + scale) + shift).astype(BF16)
            a = _dot(h, w1_ref[...])
            b = _dot(h, w3_ref[...])
            g = (_silu(a) * b).astype(BF16)
            y = _dot(g, w2_ref[...])
            z = ALPHA * x + (0.5 * gate) * y
            o_ref[rows, :] = _layer_norm(z, g_ref[...], b_ref[...])

    if n_x == 1 and n_o == 1:
        compute(x_refs[0], o_refs[0])
    else:
        in_prompt = pl.program_id(0) < TP // tm
        pl.when(in_prompt)(lambda: compute(x_refs[0], o_refs[0]))
        pl.when(jnp.logical_not(in_prompt))(lambda: compute(x_refs[-1], o_refs[-1]))


def _group_specs(split, tm, width=D):
    if not split:
        return [pl.BlockSpec((tm, width), lambda i: (i, 0))]
    n_p = TP // tm
    return [pl.BlockSpec((tm, width), lambda i: (jnp.minimum(i, n_p - 1), 0)),
            pl.BlockSpec((tm, width), lambda i: (jnp.maximum(i - n_p, 0), 0))]


def _ffn_sublayer(xs, mod, w1, w3, w2, ln_g, ln_b, layer, sub, split_out=False, casts=None):
    n_x, n_o = len(xs), 2 if split_out else 1
    tm = FFN_TM if n_x == n_o == 1 else FFN_TM_SPLIT
    out_shape = ([jax.ShapeDtypeStruct((TP, D), F32), jax.ShapeDtypeStruct((TS, D), F32)] if split_out
                 else [jax.ShapeDtypeStruct((T, D), F32)])
    mod_lo = 6 * sub
    body = functools.partial(_ffn_body, n_x=n_x, n_o=n_o, tm=tm)
    in_specs = _group_specs(n_x == 2, tm) + _mod_specs(layer, (mod_lo, mod_lo + 1, mod_lo + 2), tm) + [
        _resident((D, D_FF)),
        _resident((D, D_FF)),
        _resident((D_FF, D)),
    ] + _ln_specs(layer, 2 * sub)
    out_specs = _group_specs(split_out, tm)
    args = [*xs, mod, mod, mod, w1, w3, w2, ln_g, ln_b]
    if casts is not None:
        ws, lead = casts
        c_in, c_out, c_shapes = _cast_plan(ws, lead, T // tm, lambda i: i)
        body = _hosting_casts(body, len(in_specs), len(out_specs), len(ws))
        in_specs, out_specs, out_shape = in_specs + c_in, out_specs + c_out, out_shape + c_shapes
        args = args + list(ws)
    return pl.pallas_call(
        body,
        grid=(T // tm,),
        in_specs=in_specs,
        out_specs=out_specs,
        out_shape=out_shape,
        compiler_params=_params(1),
        name="ffn_sublayer",
    )(*args)


PF_AQ, PF_FF, PF_FB, PF_AG = 0, 512, 1024, 1536
PF_BQ, PF_BK, PF_BG, PF_LAF, PF_LAB = 2048, 2304, 2560, 3072, 3328
PF_W = 3584
PB_AV, PB_BV = 0, 512
PB_W = 1024
WM_AQ, WM_AI, WM_FF, WM_FB, WM_AG = 0, A_W, 2 * A_W, 3 * A_W, 4 * A_W
WM_BQ = 5 * A_W
WM_BK, WM_BV = WM_BQ + B_QK, WM_BQ + 2 * B_QK
WM_BG = WM_BV + B_V
WM_W = WM_BG + B_V


def _log_sigmoid(x):
    return jnp.minimum(x, 0.0) - jnp.log(1.0 + jnp.exp(-jnp.abs(x)))


def _inproj_body(x_ref, shift_ref, scale_ref, w_ref, wz_ref, gu_ref, gb_ref, lb_ref, pf_ref, pb_ref, *, layer_e):
    def lower_bound(d):
        l = lb_ref[d]
        e = jnp.exp(l - jnp.max(l, axis=0, keepdims=True))
        sm = e / jnp.sum(e, axis=0, keepdims=True)
        return jnp.sum(sm[:layer_e + 1], axis=0, keepdims=True)

    lbs = [lower_bound(0), lower_bound(1)]
    for r in range(0, x_ref.shape[0], PROJ_SUB):
        rows = slice(r, r + PROJ_SUB)
        h = (x_ref[rows, :] * (1.0 + scale_ref[...]) + shift_ref[...]).astype(BF16)

        def proj(off, width):
            return _dot(h, w_ref[:, off:off + width])

        pf_ref[rows, PF_AQ:PF_AQ + A_W] = proj(WM_AQ, A_W)
        for lb, wm, pf in ((lbs[0], WM_FF, PF_FF), (lbs[1], WM_FB, PF_FB)):
            pf_ref[rows, pf:pf + A_W] = lb + (1.0 - lb) * jax.nn.sigmoid(proj(wm, A_W))
        pf_ref[rows, PF_AG:PF_AG + A_W] = _silu(proj(WM_AG, A_W))
        pf_ref[rows, PF_BQ:PF_BQ + B_QK] = proj(WM_BQ, B_QK) * (B_DK ** -0.5)
        pf_ref[rows, PF_BK:PF_BK + B_QK] = proj(WM_BK, B_QK)
        pf_ref[rows, PF_BG:PF_BG + B_V] = _silu(proj(WM_BG, B_V))
        pb_ref[rows, PB_AV:PB_AV + A_W] = _silu(proj(WM_AI, A_W)).astype(BF16)
        pb_ref[rows, PB_BV:PB_BV + B_V] = proj(WM_BV, B_V).astype(BF16)
        z = _dot(h, wz_ref[...]).astype(BF16)
        pre = _dot(z, gu_ref[...]) + gb_ref[...]
        pf_ref[rows, PF_LAF:PF_LAF + 2 * B_QK] = _log_sigmoid(pre) * (1.0 / GLA_TAU)


def _inproj(x, mod, wmain, wz, gup, gb, hgrn_lb, layer, layer_e, tm=INPROJ_TM):
    n_l = hgrn_lb.shape[1]
    return pl.pallas_call(
        functools.partial(_inproj_body, layer_e=layer_e),
        grid=(T // tm,),
        in_specs=[pl.BlockSpec((tm, D), lambda i: (i, 0))] + _mod_specs(layer, (3, 4), tm) + [
            _resident((D, WM_W)),
            _resident((D, LANES)),
            _resident((LANES, 2 * B_QK)),
            _resident((1, 2 * B_QK)),
            _resident((2, n_l, A_W)),
        ],
        out_specs=[
            pl.BlockSpec((tm, PF_W), lambda i: (i, 0)),
            pl.BlockSpec((tm, PB_W), lambda i: (i, 0)),
        ],
        out_shape=[jax.ShapeDtypeStruct((T, PF_W), F32), jax.ShapeDtypeStruct((T, PB_W), BF16)],
        compiler_params=_params(1),
        name="ab_inproj",
    )(x, mod, mod, wmain, wz, gup, gb, hgrn_lb)


SCAN_PROMPT_SEQS = 8
SCAN_UNROLL = 8


def _prefix_rows(x):
    row = lax.broadcasted_iota(jnp.int32, x.shape, 0)
    s = 1
    while s < x.shape[0]:
        x = x + jnp.where(row >= s, pltpu.roll(x, s, axis=0), 0.0)
        s *= 2
    return x


def _scan_body(*refs, seq_len, seqs, pair, has_s0, emit_state):
    n = seq_len // CHUNK
    n_all = seqs * n
    nh = 2 if pair else 1
    it = iter(refs)
    q_ref = next(it)
    if pair:
        k_ref, laf_ref, lab_ref = next(it), next(it), next(it)
    else:
        ff_ref, fb_ref = next(it), next(it)
    g_ref, v_ref, nw_ref = next(it), next(it), next(it)
    s0_ref = next(it) if has_s0 else None
    o_ref = next(it)
    st_ref = next(it) if emit_state else None
    qd_s, oi_s, kv_s, dec_s, sb_s = it

    row = lax.broadcasted_iota(jnp.int32, (CHUNK, CHUNK), 0)
    col = lax.broadcasted_iota(jnp.int32, (CHUNK, CHUNK), 1)
    tril = row >= col
    triu = row <= col
    lane = lax.broadcasted_iota(jnp.int32, (1, LANES), 1)
    lane2 = lax.broadcasted_iota(jnp.int32, (1, 2 * LANES), 1)
    if pair:
        masks = [lane < HALF, lane >= HALF]
        masks2 = [(lane2 % LANES) < HALF, (lane2 % LANES) >= HALF]
    else:
        masks, masks2 = [None], [None]

    def pick(mask, x):
        return x if mask is None else jnp.where(mask, x, jnp.zeros_like(x))

    def rows_of(c):
        return pl.ds(pl.multiple_of(c * CHUNK, CHUNK), CHUNK)

    def loop(body):
        if n_all <= SCAN_UNROLL:
            for c in range(n_all):
                body(c)
        else:
            def fbody(i, carry):
                for u in range(SCAN_UNROLL):
                    body(i * SCAN_UNROLL + u)
                return carry
            lax.fori_loop(0, n_all // SCAN_UNROLL, fbody, 0)

    def phase1(c):
        rows = rows_of(c)
        q = q_ref[rows, :]
        if pair:
            k_f = k_b = k_ref[rows, :]
            la_f, la_b = laf_ref[rows, :], lab_ref[rows, :]
        else:
            f_f, f_b = ff_ref[rows, :], fb_ref[rows, :]
            k_f, k_b = 1.0 - f_f, 1.0 - f_b
            la_f, la_b = jnp.log(f_f), jnp.log(f_b)
        cs = _prefix_rows(jnp.concatenate([la_f, la_b], axis=1))
        cf, cbi = cs[:, :LANES], cs[:, LANES:]
        tot_f, tot_b = cf[CHUNK - 1:CHUNK, :], cbi[CHUNK - 1:CHUNK, :]
        rb = tot_b - cbi + la_b
        ref_f, ref_b = cf[CHUNK // 2 - 1:CHUNK // 2, :], rb[CHUNK // 2:CHUNK // 2 + 1, :]
        qtf = q * jnp.exp(cf - ref_f)
        qtb = q * jnp.exp(rb - ref_b)
        ktf = k_f * jnp.exp(ref_f - cf)
        ktb = k_b * jnp.exp(ref_b - rb)
        qd = jnp.concatenate([qtf * jnp.exp(ref_f), qtb * jnp.exp(ref_b)], axis=1).astype(BF16)
        ku = jnp.concatenate([ktf * jnp.exp(tot_f - ref_f), ktb * jnp.exp(tot_b - ref_b)], axis=1).astype(BF16)
        qd_s[rows, :] = qd
        qt = jnp.concatenate([qtf, qtb], axis=0).astype(BF16)
        kt = jnp.concatenate([ktf, ktb], axis=0).astype(BF16)
        kv = None
        for hh in range(nh):
            v = v_ref[rows, hh * LANES:(hh + 1) * LANES]
            sc = _dot_nt(pick(masks[hh], qt), kt)
            att = jnp.where(tril, sc[:CHUNK, :CHUNK], 0.0) + jnp.where(triu, sc[CHUNK:, CHUNK:], 0.0)
            oi_s[rows, hh * LANES:(hh + 1) * LANES] = _dot(att.astype(BF16), v)
            kv_h = _dot_tn(v, ku)
            kv = kv_h if kv is None else jnp.where(masks2[0], kv, kv_h)
        kv_s[c] = kv
        dec_s[c] = jnp.exp(jnp.concatenate([tot_f, tot_b], axis=1))

    loop(phase1)

    def recurrence(sq, d, reverse):
        cols = slice(d * LANES, (d + 1) * LANES)
        c0 = sq * n
        st0 = s0_ref[sq, d, 0].T if has_s0 else jnp.zeros((LANES, LANES), F32)

        def step(c, st):
            sb_s[c, :, cols] = st.astype(BF16)
            return st * dec_s[c, :, cols] + kv_s[c, :, cols]

        if n <= 8:
            st = st0
            for c in (range(n - 1, -1, -1) if reverse else range(n)):
                st = step(c0 + c, st)
        else:
            st = lax.fori_loop(0, n, lambda i, st: step(c0 + (n - 1 - i if reverse else i), st), st0)
        if emit_state:
            st_ref[sq, d, 0] = st.T

    for sq in range(seqs):
        recurrence(sq, 0, False)
        recurrence(sq, 1, True)

    nw = nw_ref[...]

    def phase2(c):
        rows = rows_of(c)
        qcat = qd_s[rows, :]
        scat = sb_s[c]
        for hh in range(nh):
            cols = slice(hh * LANES, (hh + 1) * LANES)
            o = oi_s[rows, cols] + _dot_nt(pick(masks2[hh], qcat), scat)
            o = o * lax.rsqrt(jnp.mean(o * o, axis=-1, keepdims=True) + RMS_EPS) * nw
            o_ref[rows, cols] = (o * g_ref[rows, cols]).astype(BF16)

    loop(phase2)


def _scan(pf, pb, norm_w, s0, *, prompt, pair):
    seq_len = SEQ if prompt else DEC_SEQ
    nseq = BATCH if prompt else DEC_BATCH
    seqs = SCAN_PROMPT_SEQS if prompt else 1
    rows = seqs * seq_len
    row_off = 0 if prompt else TP // rows
    units = B_HEADS // 2 if pair else A_HEADS
    nh = 2 if pair else 1
    n_all = rows // CHUNK
    has_s0 = s0 is not None
    emit_state = prompt

    def colspec(off, width=LANES):
        base = off // width
        return pl.BlockSpec((rows, width), lambda s, u: (s + row_off, base + u))

    if pair:
        in_specs = [colspec(PF_BQ), colspec(PF_BK), colspec(PF_LAF), colspec(PF_LAB),
                    colspec(PF_BG, 2 * LANES), colspec(PB_BV, 2 * LANES)]
        args = [pf, pf, pf, pf, pf, pb]
    else:
        in_specs = [colspec(PF_AQ), colspec(PF_FF), colspec(PF_FB), colspec(PF_AG), colspec(PB_AV)]
        args = [pf, pf, pf, pf, pb]
    in_specs.append(pl.BlockSpec((1, LANES), lambda s, u: (0, 0)))
    args.append(norm_w.reshape(1, LANES))
    state_spec = pl.BlockSpec((seqs, 2, 1, LANES, LANES), lambda s, u: (s, 0, u, 0, 0))
    if has_s0:
        in_specs.append(state_spec)
        args.append(s0)
    out_specs = [pl.BlockSpec((rows, nh * LANES), lambda s, u: (s, u))]
    out_shape = [jax.ShapeDtypeStruct((nseq * seq_len, units * nh * LANES), BF16)]
    if emit_state:
        out_specs.append(state_spec)
        out_shape.append(jax.ShapeDtypeStruct((nseq, 2, units, LANES, LANES), F32))
    scratch = [
        pltpu.VMEM((rows, 2 * LANES), BF16),
        pltpu.VMEM((rows, nh * LANES), F32),
        pltpu.VMEM((n_all, LANES, 2 * LANES), F32),
        pltpu.VMEM((n_all, 1, 2 * LANES), F32),
        pltpu.VMEM((n_all, LANES, 2 * LANES), BF16),
    ]
    return pl.pallas_call(
        functools.partial(_scan_body, seq_len=seq_len, seqs=seqs, pair=pair, has_s0=has_s0,
                          emit_state=emit_state),
        grid=(nseq // seqs, units),
        in_specs=in_specs,
        out_specs=out_specs,
        out_shape=out_shape,
        scratch_shapes=scratch,
        compiler_params=_params(2),
        name=f"scan_{'p' if prompt else 's'}_{'gla' if pair else 'hgrn'}",
    )(*args)


def _outproj_body(*refs, n_lhs, tm):
    lhs = refs[:2 * n_lhs]
    ws = refs[2 * n_lhs:3 * n_lhs]
    x_ref, m_ref, g_ref, b_ref, o_ref = refs[3 * n_lhs:]

    def compute(group):
        for r in range(0, tm, PROJ_SUB):
            rows = slice(r, r + PROJ_SUB)
            y = _dot(lhs[group][rows, :], ws[0][...])
            for j in range(1, n_lhs):
                y = y + _dot(lhs[2 * j + group][rows, :], ws[j][...])
            z = ALPHA * x_ref[rows, :] + m_ref[...] * y
            o_ref[rows, :] = _layer_norm(z, g_ref[...], b_ref[...])

    in_prompt = pl.program_id(0) < TP // tm
    pl.when(in_prompt)(lambda: compute(0))
    pl.when(jnp.logical_not(in_prompt))(lambda: compute(1))


def _outproj(lhs, ws, x, mod, ln_g, ln_b, layer, tm=OUT_TM):
    n_lhs = len(lhs)
    in_specs, args = [], []
    for a_p, a_s in lhs:
        in_specs += _group_specs(True, tm, a_p.shape[1])
        args += [a_p, a_s]
    in_specs += [_resident(w.shape) for w in ws]
    in_specs += [pl.BlockSpec((tm, D), lambda i: (i, 0))] + _mod_specs(layer, (5,), tm) + _ln_specs(layer, 1)
    return pl.pallas_call(
        functools.partial(_outproj_body, n_lhs=n_lhs, tm=tm),
        grid=(T // tm,),
        in_specs=in_specs,
        out_specs=pl.BlockSpec((tm, D), lambda i: (i, 0)),
        out_shape=jax.ShapeDtypeStruct((T, D), F32),
        compiler_params=_params(1),
        name="mixer_outproj",
    )(*args, *ws, x, mod, ln_g, ln_b)


def _rope_partner(x):
    lane = lax.broadcasted_iota(jnp.int32, x.shape, 1)
    first_half = (lane % (2 * ROPE_FREQS)) < ROPE_FREQS
    return jnp.where(first_half, pltpu.roll(x, LANES - ROPE_FREQS, axis=1), pltpu.roll(x, ROPE_FREQS, axis=1))


def _qkv_body(x_ref, shift_ref, scale_ref, w_ref, wkvt_ref, cos_ref, sin_ref,
              q_ref, k_ref, v_ref, kt_ref, vt_ref):
    qscale = C_HEAD_DIM ** -0.5
    in_prompt = pl.program_id(0) < TP // PROJ_TM

    def modulated(rows):
        return (x_ref[rows, :] * (1.0 + scale_ref[...]) + shift_ref[...]).astype(BF16)

    @pl.when(in_prompt)
    def _():
        for sq in range(PROJ_TM // SEQ):
            rows = slice(sq * SEQ, (sq + 1) * SEQ)
            h = modulated(rows)
            q_ref[rows, :] = (_dot(h, w_ref[:, :C_Q]) * qscale).astype(BF16)
            v_ref[rows, :] = _dot(h, w_ref[:, C_Q + C_KV:]).astype(BF16)
            kt_ref[sq] = _dot_nt(wkvt_ref[:C_KV, :], h)
            vt_ref[sq] = _dot_nt(wkvt_ref[C_KV:, :], h)

    @pl.when(jnp.logical_not(in_prompt))
    def _():
        for sq in range(PROJ_TM // SEQ):
            rows = slice(sq * SEQ, (sq + 1) * SEQ)
            h = modulated(rows)
            cos, sin = cos_ref[rows, :], sin_ref[rows, :]
            zq = _dot(h, w_ref[:, :C_Q])
            zk = _dot(h, w_ref[:, C_Q:C_Q + C_KV])
            v_ref[rows, :] = _dot(h, w_ref[:, C_Q + C_KV:]).astype(BF16)

            def rope(z, cos=cos, sin=sin):
                return z * cos + _rope_partner(z) * sin

            for j in range(C_Q // LANES):
                cols = slice(j * LANES, (j + 1) * LANES)
                q_ref[rows, cols] = (rope(zq[:, cols]) * qscale).astype(BF16)
            for j in range(C_KV // LANES):
                cols = slice(j * LANES, (j + 1) * LANES)
                k_ref[rows, cols] = rope(zk[:, cols]).astype(BF16)


def _qkv(x, mod, w, wkvt, cos, sin, layer):
    tm = PROJ_TM
    n_p = TP // tm
    lat = lambda i: (jnp.maximum(i - n_p, 0), 0)
    ctx = lambda i: (jnp.minimum(i, n_p - 1), 0, 0)
    return pl.pallas_call(
        _qkv_body,
        grid=(T // tm,),
        in_specs=[pl.BlockSpec((tm, D), lambda i: (i, 0))] + _mod_specs(layer, (3, 4), tm) + [
            _resident((D, C_Q + 2 * C_KV)),
            _resident((2 * C_KV, D)),
            pl.BlockSpec((tm, LANES), lat),
            pl.BlockSpec((tm, LANES), lat),
        ],
        out_specs=[
            pl.BlockSpec((tm, C_Q), lambda i: (i, 0)),
            pl.BlockSpec((tm, C_KV), lat),
            pl.BlockSpec((tm, C_KV), lambda i: (i, 0)),
            pl.BlockSpec((tm // SEQ, C_KV, SEQ), ctx),
            pl.BlockSpec((tm // SEQ, C_KV, SEQ), ctx),
        ],
        out_shape=[jax.ShapeDtypeStruct((T, C_Q), BF16), jax.ShapeDtypeStruct((TS, C_KV), BF16),
                   jax.ShapeDtypeStruct((T, C_KV), BF16),
                   jax.ShapeDtypeStruct((BATCH, C_KV, SEQ), F32), jax.ShapeDtypeStruct((BATCH, C_KV, SEQ), F32)],
        compiler_params=_params(1),
        name="c_qkv",
    )(x, mod, mod, w, wkvt, cos, sin)


def _rope_tables():
    t = np.arange(DEC_SEQ)
    pos = np.stack([t // GRID_W, t % GRID_W], axis=1).astype(np.float32)
    inv = (ROPE_BASE ** (-np.arange(ROPE_FREQS, dtype=np.float32) / ROPE_FREQS)).astype(np.float32)
    d = np.arange(C_HEAD_DIM)
    axis = d // (2 * ROPE_FREQS)
    ang = pos[:, axis] * inv[d % ROPE_FREQS][None, :]
    sign = np.where((d % (2 * ROPE_FREQS)) < ROPE_FREQS, -1.0, 1.0)[None, :]
    cos_h, sin_h = np.cos(ang), np.sin(ang) * sign
    reps = LANES // C_HEAD_DIM
    cos = np.tile(np.tile(cos_h, (1, reps)), (DEC_BATCH, 1)).astype(np.float32)
    sin = np.tile(np.tile(sin_h, (1, reps)), (DEC_BATCH, 1)).astype(np.float32)
    return jnp.asarray(cos), jnp.asarray(sin)


def _dup_head(blk, half):
    lane = lax.broadcasted_iota(jnp.int32, blk.shape, 1)
    keep = (lane >= HALF).astype(jnp.int32) == half
    return jnp.where(keep, blk, pltpu.roll(blk, HALF, axis=1))


def _pair_values(v, half):
    lo = lax.broadcasted_iota(jnp.int32, v.shape, 1) < HALF
    v2 = _dup_head(v.astype(F32), half)
    top = jnp.concatenate([jnp.where(lo, v2, 0.0), jnp.where(lo, 1.0, 0.0)], axis=1)
    bot = jnp.concatenate([jnp.where(lo, 0.0, v2), jnp.where(lo, 0.0, 1.0)], axis=1)
    return top.astype(BF16), bot.astype(BF16)


def _softmax_weights(scores, sink):
    mx = sink
    for s in scores:
        mx = jnp.maximum(mx, jnp.max(s, axis=-1, keepdims=True))
    return [jnp.exp(s - mx).astype(BF16) for s in scores], jnp.exp(sink - mx)


def _pair_sinks(sink_ref, jb):
    return (sink_ref[:, jb * LANES:jb * LANES + 1], sink_ref[:, jb * LANES + HALF:jb * LANES + HALF + 1])


KV_PER_BLOCK = LANES // C_HEAD_DIM
TILE_KV = C_KV_HEADS
TILE_KVW = TILE_KV * C_HEAD_DIM
CTX_TILES = BATCH * C_KV_HEADS // TILE_KV
CTX_BLOCKS = TILE_KV * C_GROUPS // 2
PIPE_LAG = 2


def _kv_block(ref, kv):
    lb = kv // KV_PER_BLOCK
    return ref[:, lb * LANES:(lb + 1) * LANES], kv % KV_PER_BLOCK


def _run_pipeline(g, n_tiles, stages):
    for cur in (0, 1):
        steady = jnp.logical_and(g % 2 == cur, jnp.logical_and(g >= PIPE_LAG, g < n_tiles))
        pl.when(steady)(functools.partial(stages, cur, 1 - cur, True, True, True))
    pl.when(g == 0)(functools.partial(stages, 0, 1, True, False, False))
    pl.when(g == 1)(functools.partial(stages, 1, 0, True, True, False))
    pl.when(g == n_tiles)(functools.partial(stages, n_tiles % 2, 1 - n_tiles % 2, False, True, True))
    pl.when(g == n_tiles + 1)(functools.partial(stages, 1 - n_tiles % 2, n_tiles % 2, False, False, True))


def _ctx_tile(g, lag):
    t = jnp.clip(g - lag, 0, CTX_TILES - 1)
    n = C_KV_HEADS // TILE_KV
    return t // n, t % n


def _ctx_attn_body(q_ref, kt_ref, v_ref, sink_ref, o_ref, s_s, e_s, st_s):
    g = pl.program_id(0)
    lo = lax.broadcasted_iota(jnp.int32, (1, LANES), 1) < HALF
    pairs_per_kv = C_GROUPS // 2

    def stages(cur, prev, scores, weights, values):
        for kv in range(TILE_KV if values else 0):
            v_top, v_bot = _pair_values(*_kv_block(v_ref, kv))
            for jb in range(kv * pairs_per_kv, (kv + 1) * pairs_per_kv):
                od = _dot(e_s[cur, 2 * jb], v_top) + _dot(e_s[cur, 2 * jb + 1], v_bot)
                o_ref[:, jb * LANES:(jb + 1) * LANES] = (od[:, :LANES] / (od[:, LANES:] + st_s[cur, jb])).astype(BF16)
        for jb in range(CTX_BLOCKS if weights else 0):
            terms = []
            for hh, sink in enumerate(_pair_sinks(sink_ref, jb)):
                (e,), term = _softmax_weights([s_s[prev, 2 * jb + hh]], sink)
                e_s[prev, 2 * jb + hh] = e
                terms.append(term)
            st_s[prev, jb] = jnp.where(lo, terms[0], terms[1])
        for kv in range(TILE_KV if scores else 0):
            kt = kt_ref[kv * C_HEAD_DIM:(kv + 1) * C_HEAD_DIM, :]
            zero = jnp.zeros_like(kt)
            kt_a = jnp.concatenate([kt, zero], axis=0).astype(BF16)
            kt_b = jnp.concatenate([zero, kt], axis=0).astype(BF16)
            for jb in range(kv * pairs_per_kv, (kv + 1) * pairs_per_kv):
                q = q_ref[:, jb * LANES:(jb + 1) * LANES]
                s_s[cur, 2 * jb] = _dot(q, kt_a)
                s_s[cur, 2 * jb + 1] = _dot(q, kt_b)

    _run_pipeline(g, CTX_TILES, stages)


def _ctx_attn(q, kt, v, sink_cols):
    gw = CTX_BLOCKS * LANES

    def at(lag, fn):
        return lambda g: fn(*_ctx_tile(g, lag))

    return pl.pallas_call(
        _ctx_attn_body,
        grid=(CTX_TILES + PIPE_LAG,),
        in_specs=[
            pl.BlockSpec((SEQ, gw), at(0, lambda b, p: (b, p))),
            pl.BlockSpec((None, TILE_KVW, SEQ), at(0, lambda b, p: (b, p, 0))),
            pl.BlockSpec((SEQ, TILE_KVW), at(2, lambda b, p: (b, p))),
            pl.BlockSpec((1, gw), at(1, lambda b, p: (0, p))),
        ],
        out_specs=pl.BlockSpec((SEQ, gw), at(2, lambda b, p: (b, p))),
        out_shape=jax.ShapeDtypeStruct((TP, C_Q), BF16),
        scratch_shapes=[
            pltpu.VMEM((2, 2 * CTX_BLOCKS, SEQ, SEQ), F32),
            pltpu.VMEM((2, 2 * CTX_BLOCKS, SEQ, SEQ), BF16),
            pltpu.VMEM((2, CTX_BLOCKS, SEQ, LANES), F32),
        ],
        compiler_params=_params(1),
        name="c_attn_ctx",
    )(q, kt, v, sink_cols)


LAT_TQ = 128
LAT_WIN = LAT_TQ + 2 * WINDOW
LAT_NQ = DEC_SEQ // LAT_TQ
LAT_KVB = C_KV_HEADS // TILE_KV
LAT_TILES = DEC_BATCH * LAT_KVB * LAT_NQ


def _lat_tile(g, lag):
    t = jnp.clip(g - lag, 0, LAT_TILES - 1)
    bp = t // LAT_NQ
    return bp // LAT_KVB, bp % LAT_KVB, t % LAT_NQ


def _lat_window(j):
    return pl.multiple_of(jnp.clip(j * LAT_TQ - WINDOW, 0, DEC_SEQ - LAT_WIN), LANES)


def _lat_attn_body(q_ref, k_ref, v_ref, ck_ref, cv_ref, sink_ref, o_ref,
                   ka_s, kb_s, cka_s, ckb_s, va_s, vb_s, cva_s, cvb_s, sl_s, sc_s, el_s, ec_s, st_s):
    g = pl.program_id(0)
    lo = lax.broadcasted_iota(jnp.int32, (1, LANES), 1) < HALF
    pairs_per_kv = C_GROUPS // 2
    b_a, p_a, j_a = _lat_tile(g, 0)
    b_c, p_c, j_c = _lat_tile(g, 2)
    v_slot_a = (b_a * LAT_KVB + p_a) % 2
    v_slot_c = (b_c * LAT_KVB + p_c) % 2

    @pl.when(jnp.logical_and(j_a == 0, g < LAT_TILES))
    def _():
        for kv in range(TILE_KV):
            for src, a_s, b_s in ((k_ref, ka_s, kb_s), (ck_ref, cka_s, ckb_s)):
                blk, kh = _kv_block(src, kv)
                lo2 = lax.broadcasted_iota(jnp.int32, blk.shape, 1) < HALF
                k2 = _dup_head(blk.astype(F32), kh)
                a_s[kv] = jnp.where(lo2, k2, 0.0).astype(BF16)
                b_s[kv] = jnp.where(lo2, 0.0, k2).astype(BF16)
            va_s[v_slot_a, kv], vb_s[v_slot_a, kv] = _pair_values(*_kv_block(v_ref, kv))
            cva_s[v_slot_a, kv], cvb_s[v_slot_a, kv] = _pair_values(*_kv_block(cv_ref, kv))

    def stages(cur, prev, scores, weights, values):
        win_c = pl.ds(_lat_window(j_c), LAT_WIN)
        for kv in range(TILE_KV if values else 0):
            v_loc = (va_s[v_slot_c, kv, win_c, :], vb_s[v_slot_c, kv, win_c, :])
            v_ctx = (cva_s[v_slot_c, kv], cvb_s[v_slot_c, kv])
            for jb in range(kv * pairs_per_kv, (kv + 1) * pairs_per_kv):
                od = None
                for hh in range(2):
                    part = _dot(el_s[cur, 2 * jb + hh], v_loc[hh]) + _dot(ec_s[cur, 2 * jb + hh], v_ctx[hh])
                    od = part if od is None else od + part
                o_ref[:, jb * LANES:(jb + 1) * LANES] = (od[:, :LANES] / (od[:, LANES:] + st_s[cur, jb])).astype(BF16)
        for jb in range(CTX_BLOCKS if weights else 0):
            terms = []
            for hh, sink in enumerate(_pair_sinks(sink_ref, jb)):
                (e_loc, e_ctx), term = _softmax_weights([sl_s[prev, 2 * jb + hh], sc_s[prev, 2 * jb + hh]], sink)
                el_s[prev, 2 * jb + hh] = e_loc
                ec_s[prev, 2 * jb + hh] = e_ctx
                terms.append(term)
            st_s[prev, jb] = jnp.where(lo, terms[0], terms[1])
        start = _lat_window(j_a)
        win = pl.ds(start, LAT_WIN)
        qpos = j_a * LAT_TQ + lax.broadcasted_iota(jnp.int32, (LAT_TQ, LAT_WIN), 0)
        kpos = start + lax.broadcasted_iota(jnp.int32, (LAT_TQ, LAT_WIN), 1)
        band = jnp.abs(qpos - kpos) <= WINDOW
        for kv in range(TILE_KV if scores else 0):
            for jb in range(kv * pairs_per_kv, (kv + 1) * pairs_per_kv):
                q = q_ref[:, jb * LANES:(jb + 1) * LANES]
                for hh, (kl, kc) in enumerate(((ka_s, cka_s), (kb_s, ckb_s))):
                    sl_s[cur, 2 * jb + hh] = jnp.where(band, _dot_nt(q, kl[kv, win, :]), -jnp.inf)
                    sc_s[cur, 2 * jb + hh] = _dot_nt(q, kc[kv])

    _run_pipeline(g, LAT_TILES, stages)


def _lat_attn(q, k, v, ck, cv, sink_cols):
    gw = CTX_BLOCKS * LANES
    q_off = TP // LAT_TQ
    kv_off = TP // DEC_SEQ
    kvb, n_heads = TILE_KV, 2 * CTX_BLOCKS

    def at(lag, fn):
        return lambda g: fn(*_lat_tile(g, lag))

    return pl.pallas_call(
        _lat_attn_body,
        grid=(LAT_TILES + PIPE_LAG,),
        in_specs=[
            pl.BlockSpec((LAT_TQ, gw), at(0, lambda b, p, j: (q_off + b * LAT_NQ + j, p))),
            pl.BlockSpec((DEC_SEQ, TILE_KVW), at(0, lambda b, p, j: (b, p))),
            pl.BlockSpec((DEC_SEQ, TILE_KVW), at(0, lambda b, p, j: (kv_off + b, p))),
            pl.BlockSpec((PAST_LEN, TILE_KVW), at(0, lambda b, p, j: (b, p))),
            pl.BlockSpec((PAST_LEN, TILE_KVW), at(0, lambda b, p, j: (b, p))),
            pl.BlockSpec((1, gw), at(1, lambda b, p, j: (0, p))),
        ],
        out_specs=pl.BlockSpec((LAT_TQ, gw), at(2, lambda b, p, j: (b * LAT_NQ + j, p))),
        out_shape=jax.ShapeDtypeStruct((TS, C_Q), BF16),
        scratch_shapes=[
            pltpu.VMEM((kvb, DEC_SEQ, LANES), BF16), pltpu.VMEM((kvb, DEC_SEQ, LANES), BF16),
            pltpu.VMEM((kvb, PAST_LEN, LANES), BF16), pltpu.VMEM((kvb, PAST_LEN, LANES), BF16),
            pltpu.VMEM((2, kvb, DEC_SEQ, 2 * LANES), BF16), pltpu.VMEM((2, kvb, DEC_SEQ, 2 * LANES), BF16),
            pltpu.VMEM((2, kvb, PAST_LEN, 2 * LANES), BF16), pltpu.VMEM((2, kvb, PAST_LEN, 2 * LANES), BF16),
            pltpu.VMEM((2, n_heads, LAT_TQ, LAT_WIN), F32), pltpu.VMEM((2, n_heads, LAT_TQ, PAST_LEN), F32),
            pltpu.VMEM((2, n_heads, LAT_TQ, LAT_WIN), BF16), pltpu.VMEM((2, n_heads, LAT_TQ, PAST_LEN), BF16),
            pltpu.VMEM((2, CTX_BLOCKS, LAT_TQ, LANES), F32),
        ],
        compiler_params=_params(1),
        name="c_attn_latent",
    )(q, k, v, ck, cv, sink_cols)


def kernel(x_prompt, x_sample, state_hgrn, state_gla, cache_k, cache_v, c, c_ctx, w_mod, b_mod, ln_g, ln_b,
           ffn_w1, ffn_w3, ffn_w2, w_in_ab, hgrn_lb, gla_gate_up, gla_gate_b, norm_a, norm_b, w_out_ab,
           w_qkv_c, sink_c, w_out_c):
    cs = jnp.zeros((8, D), F32).at[0].set(c_ctx).at[1:1 + DEC_BATCH].set(c)
    mod = _mod_vectors(cs, w_mod, b_mod).reshape(DEPTH, 8, 1, N_MOD * D)
    ffn_ws = (ffn_w1, ffn_w3, ffn_w2)
    ln_g, ln_b = ln_g.reshape(DEPTH, 3, 1, D), ln_b.reshape(DEPTH, 3, 1, D)

    def ffn(xs, ws, layer, sub, split_out=False, cast_next=None):
        casts = None if cast_next is None else (ffn_ws, cast_next)
        return _ffn_sublayer(xs, mod, *ws, ln_g, ln_b, layer, sub, split_out=split_out, casts=casts)

    x, *ws_01 = ffn([x_prompt.reshape(TP, D), x_sample.reshape(TS, D)], [w[0, 0].astype(BF16) for w in ffn_ws],
                    0, 0, cast_next=(0, 1))
    w_in = w_in_ab[0]
    wmain = w_in[:, :WM_W].astype(BF16)
    wz = jnp.pad(w_in[:, WM_W:WM_W + 2 * GATE_RANK], ((0, 0), (0, LANES - 2 * GATE_RANK))).astype(BF16)
    gup = jnp.zeros((LANES, 2 * B_QK), F32)
    gup = gup.at[:GATE_RANK, :B_QK].set(gla_gate_up[0, 0]).at[GATE_RANK:2 * GATE_RANK, B_QK:].set(gla_gate_up[0, 1])
    gb = gla_gate_b[0].reshape(1, 2 * B_QK)
    pf, pb = _inproj(x, mod, wmain, wz, gup.astype(BF16), gb, hgrn_lb, 0, 0)

    s0_a = state_hgrn[:, 0]
    s0_b = state_gla[:, 0].reshape(DEC_BATCH, 2, B_HEADS // 2, LANES, B_DV)
    oa_p, st_a = _scan(pf, pb, norm_a[0], None, prompt=True, pair=False)
    ob_p, st_b = _scan(pf, pb, norm_b[0], None, prompt=True, pair=True)
    (oa_s,) = _scan(pf, pb, norm_a[0], s0_a, prompt=False, pair=False)
    (ob_s,) = _scan(pf, pb, norm_b[0], s0_b, prompt=False, pair=True)
    w_out = w_out_ab[0].astype(BF16)
    x = _outproj([(oa_p, oa_s), (ob_p, ob_s)], [w_out[:A_W], w_out[A_W:]], x, mod, ln_g, ln_b, 0)
    x, *ws_10 = ffn([x], ws_01, 0, 1, cast_next=(1, 0))
    new_hgrn = st_a.reshape(BATCH, 1, 2, A_HEADS, A_DK, A_DV)
    new_gla = st_b.reshape(BATCH, 1, 2, B_HEADS, B_DK, B_DV)

    x, *ws_11 = ffn([x], ws_10, 1, 0, cast_next=(1, 1))
    cos, sin = _rope_tables()
    w_qkv = w_qkv_c[0].astype(BF16)
    q, k, v, kt, vt = _qkv(x, mod, w_qkv, w_qkv[:, C_Q:].T, cos, sin, 1)
    sink_cols = jnp.repeat(sink_c[0], C_HEAD_DIM).reshape(1, C_Q)
    o_p = _ctx_attn(q, kt, v, sink_cols)
    ck = cache_k[:, 0].reshape(DEC_BATCH * PAST_LEN, C_KV)
    cv = cache_v[:, 0].reshape(DEC_BATCH * PAST_LEN, C_KV)
    o_s = _lat_attn(q, k, v, ck, cv, sink_cols)
    x = _outproj([(o_p, o_s)], [w_out_c[0].astype(BF16)], x, mod, ln_g, ln_b, 1)
    y_p, y_s = ffn([x], ws_11, 1, 1, split_out=True)

    def cache_layout(zt):
        return zt.reshape(BATCH, 1, C_KV_HEADS, C_HEAD_DIM, SEQ).transpose(0, 1, 4, 2, 3)

    new_k, new_v = cache_layout(kt), cache_layout(vt)

    return (y_p.reshape(BATCH, SEQ, D), y_s.reshape(DEC_BATCH, DEC_SEQ, D), new_hgrn, new_gla, new_k, new_v)
```

```python
import functools

import jax
import jax.numpy as jnp
import numpy as np
from jax import lax
from jax.experimental import pallas as pl
from jax.experimental.pallas import tpu as pltpu

D = 1024
BATCH, SEQ = 16, 256
DEC_BATCH, DEC_SEQ = 2, 2048
PAST_LEN = 512
GRID_W = 64
D_FF = 2816
N_MOD = 9
A_HEADS, A_DK, A_DV = 4, 128, 128
A_W = A_HEADS * A_DK
B_HEADS, B_DK, B_DV = 4, 64, 128
B_QK = B_HEADS * B_DK
B_V = B_HEADS * B_DV
GATE_RANK = 16
GLA_TAU = 16.0
CHUNK = 128
C_HEADS, C_KV_HEADS, C_HEAD_DIM = 16, 4, 64
C_GROUPS = C_HEADS // C_KV_HEADS
C_Q = C_HEADS * C_HEAD_DIM
C_KV = C_KV_HEADS * C_HEAD_DIM
WINDOW = 128
ROPE_FREQS = C_HEAD_DIM // 4
ROPE_BASE = 10000.0
DEPTH = 2
ALPHA = (2.0 * DEPTH) ** 0.25
LN_EPS = 1e-5
RMS_EPS = 1e-6

TP = BATCH * SEQ
TS = DEC_BATCH * DEC_SEQ
T = TP + TS

LANES = 128
HALF = LANES // 2
FFN_TM = 1024
FFN_TM_SPLIT = 1024
FFN_SUB = 256
PROJ_TM = 4 * SEQ
INPROJ_TM = 512
PROJ_SUB = 256
OUT_TM = 1024
VMEM_LIMIT = 60 * 1024 * 1024

F32 = jnp.float32
BF16 = jnp.bfloat16


def _dot(a, b):
    return jnp.dot(a, b, preferred_element_type=F32)


def _dot_nt(a, b):
    return lax.dot_general(a, b, (((1,), (1,)), ((), ())), preferred_element_type=F32)


def _dot_tn(a, b):
    return lax.dot_general(a, b, (((0,), (0,)), ((), ())), preferred_element_type=F32)


def _silu(x):
    return x * jax.nn.sigmoid(x)


def _layer_norm(z, g, b):
    mu = jnp.mean(z, axis=-1, keepdims=True)
    zc = z - mu
    var = jnp.mean(zc * zc, axis=-1, keepdims=True)
    return zc * lax.rsqrt(var + LN_EPS) * g + b


def _seg_of_tile(i, tm):
    n_p = TP // tm
    n_s = DEC_SEQ // tm
    return jnp.where(i < n_p, 0, 1 + lax.div(jnp.maximum(i - n_p, 0), n_s))


def _params(n_axes):
    return pltpu.CompilerParams(dimension_semantics=("arbitrary",) * n_axes, vmem_limit_bytes=VMEM_LIMIT)


def _resident(shape):
    nd = len(shape)
    return pl.BlockSpec(shape, lambda *_: (0,) * nd, pipeline_mode=pl.Buffered(1))


def _resident_slice(shape, lead):
    block = (None,) * len(lead) + tuple(shape)
    return pl.BlockSpec(block, lambda *_: tuple(lead) + (0,) * len(shape), pipeline_mode=pl.Buffered(1))


def _mod_specs(layer, cols, tm):
    return [pl.BlockSpec((None, None, 1, D), functools.partial(
        lambda i, c: (layer, _seg_of_tile(i, tm), 0, c), c=c)) for c in cols]


def _ln_specs(layer, idx):
    return [_resident_slice((1, D), (layer, idx))] * 2


BF16_SUBLANES = 16


def _cast_plan(ws, lead, n_steps, step_of):
    in_specs, out_specs, out_shapes = [], [], []
    for w in ws:
        rows, cols = w.shape[len(lead):]
        blk = next(b for b in range(BF16_SUBLANES, rows + 1, BF16_SUBLANES)
                   if rows % b == 0 and rows // b <= n_steps)
        last = rows // blk - 1
        in_specs.append(pl.BlockSpec((None,) * len(lead) + (blk, cols), functools.partial(
            lambda *g, last: tuple(lead) + (jnp.minimum(step_of(*g), last), 0), last=last)))
        out_specs.append(pl.BlockSpec((blk, cols), functools.partial(
            lambda *g, last: (jnp.minimum(step_of(*g), last), 0), last=last)))
        out_shapes.append(jax.ShapeDtypeStruct((rows, cols), BF16))
    return in_specs, out_specs, out_shapes


def _hosting_casts(body, n_in, n_out, n_cast):
    def hosted(*refs, **kw):
        ins, refs = refs[:n_in], refs[n_in:]
        cast_in, refs = refs[:n_cast], refs[n_cast:]
        outs, refs = refs[:n_out], refs[n_out:]
        cast_out, scratch = refs[:n_cast], refs[n_cast:]
        for src, dst in zip(cast_in, cast_out):
            dst[...] = src[...].astype(BF16)
        body(*ins, *outs, *scratch, **kw)
    return hosted


def _mod_body(c_ref, w_ref, b_ref, o_ref):
    c = c_ref[...]
    s = _silu(c).astype(BF16)
    o_ref[0] = _dot(s, w_ref[0].astype(BF16)) + b_ref[0]


def _mod_vectors(cs, w_mod, b_mod):
    tn = 1536
    n = N_MOD * D
    return pl.pallas_call(
        _mod_body,
        grid=(DEPTH, n // tn),
        in_specs=[
            pl.BlockSpec((8, D), lambda l, j: (0, 0)),
            pl.BlockSpec((1, D, tn), lambda l, j: (l, 0, j)),
            pl.BlockSpec((1, 1, tn), lambda l, j: (l, 0, j)),
        ],
        out_specs=pl.BlockSpec((1, 8, tn), lambda l, j: (l, 0, j)),
        out_shape=jax.ShapeDtypeStruct((DEPTH, 8, n), F32),
        compiler_params=_params(2),
        name="mod_vectors",
    )(cs, w_mod, b_mod.reshape(DEPTH, 1, n))


def _ffn_body(*refs, n_x, n_o, tm):
    x_refs = refs[:n_x]
    shift_ref, scale_ref, gate_ref, w1_ref, w3_ref, w2_ref, g_ref, b_ref = refs[n_x:n_x + 8]
    o_refs = refs[n_x + 8:]

    def compute(x_ref, o_ref):
        shift, scale, gate = shift_ref[...], scale_ref[...], gate_ref[...]
        for r in range(0, tm, FFN_SUB):
            rows = slice(r, r + FFN_SUB)
            x = x_ref[rows, :]
            h = (x * (1.0 + scale) + shift).astype(BF16)
            a = _dot(h, w1_ref[...])
            b = _dot(h, w3_ref[...])
            g = (_silu(a) * b).astype(BF16)
            y = _dot(g, w2_ref[...])
            z = ALPHA * x + (0.5 * gate) * y
            o_ref[rows, :] = _layer_norm(z, g_ref[...], b_ref[...])

    if n_x == 1 and n_o == 1:
        compute(x_refs[0], o_refs[0])
    else:
        in_prompt = pl.program_id(0) < TP // tm
        pl.when(in_prompt)(lambda: compute(x_refs[0], o_refs[0]))
        pl.when(jnp.logical_not(in_prompt))(lambda: compute(x_refs[-1], o_refs[-1]))


def _group_specs(split, tm, width=D):
    if not split:
        return [pl.BlockSpec((tm, width), lambda i: (i, 0))]
    n_p = TP // tm
    return [pl.BlockSpec((tm, width), lambda i: (jnp.minimum(i, n_p - 1), 0)),
            pl.BlockSpec((tm, width), lambda i: (jnp.maximum(i - n_p, 0), 0))]


def _ffn_sublayer(xs, mod, w1, w3, w2, ln_g, ln_b, layer, sub, split_out=False, casts=None):
    n_x, n_o = len(xs), 2 if split_out else 1
    tm = FFN_TM if n_x == n_o == 1 else FFN_TM_SPLIT
    out_shape = ([jax.ShapeDtypeStruct((TP, D), F32), jax.ShapeDtypeStruct((TS, D), F32)] if split_out
                 else [jax.ShapeDtypeStruct((T, D), F32)])
    mod_lo = 6 * sub
    body = functools.partial(_ffn_body, n_x=n_x, n_o=n_o, tm=tm)
    in_specs = _group_specs(n_x == 2, tm) + _mod_specs(layer, (mod_lo, mod_lo + 1, mod_lo + 2), tm) + [
        _resident((D, D_FF)),
        _resident((D, D_FF)),
        _resident((D_FF, D)),
    ] + _ln_specs(layer, 2 * sub)
    out_specs = _group_specs(split_out, tm)
    args = [*xs, mod, mod, mod, w1, w3, w2, ln_g, ln_b]
    if casts is not None:
        ws, lead = casts
        c_in, c_out, c_shapes = _cast_plan(ws, lead, T // tm, lambda i: i)
        body = _hosting_casts(body, len(in_specs), len(out_specs), len(ws))
        in_specs, out_specs, out_shape = in_specs + c_in, out_specs + c_out, out_shape + c_shapes
        args = args + list(ws)
    return pl.pallas_call(
        body,
        grid=(T // tm,),
        in_specs=in_specs,
        out_specs=out_specs,
        out_shape=out_shape,
        compiler_params=_params(1),
        name="ffn_sublayer",
    )(*args)


PF_AQ, PF_FF, PF_FB, PF_AG = 0, 512, 1024, 1536
PF_BQ, PF_BK, PF_BG, PF_LAF, PF_LAB = 2048, 2304, 2560, 3072, 3328
PF_W = 3584
PB_AV, PB_BV = 0, 512
PB_W = 1024
WM_AQ, WM_FF, WM_FB, WM_AG, WM_BQ, WM_BK, WM_BG, WM_AI, WM_BV = 0, 512, 1024, 1536, 2048, 2304, 2560, 3072, 3584
WM_W = 4096


def _log_sigmoid(x):
    return jnp.minimum(x, 0.0) - jnp.log(1.0 + jnp.exp(-jnp.abs(x)))


def _inproj_body(x_ref, shift_ref, scale_ref, w_ref, wz_ref, gu_ref, gb_ref, lb_ref, pf_ref, pb_ref, *, layer_e):
    def lower_bound(d):
        l = lb_ref[d]
        e = jnp.exp(l - jnp.max(l, axis=0, keepdims=True))
        sm = e / jnp.sum(e, axis=0, keepdims=True)
        return jnp.sum(sm[:layer_e + 1], axis=0, keepdims=True)

    lbs = [lower_bound(0), lower_bound(1)]
    for r in range(0, x_ref.shape[0], PROJ_SUB):
        rows = slice(r, r + PROJ_SUB)
        h = (x_ref[rows, :] * (1.0 + scale_ref[...]) + shift_ref[...]).astype(BF16)

        def proj(off, width):
            return _dot(h, w_ref[:, off:off + width])

        pf_ref[rows, PF_AQ:PF_AQ + A_W] = proj(WM_AQ, A_W)
        for lb, wm, pf in ((lbs[0], WM_FF, PF_FF), (lbs[1], WM_FB, PF_FB)):
            pf_ref[rows, pf:pf + A_W] = lb + (1.0 - lb) * jax.nn.sigmoid(proj(wm, A_W))
        pf_ref[rows, PF_AG:PF_AG + A_W] = _silu(proj(WM_AG, A_W))
        pf_ref[rows, PF_BQ:PF_BQ + B_QK] = proj(WM_BQ, B_QK) * (B_DK ** -0.5)
        pf_ref[rows, PF_BK:PF_BK + B_QK] = proj(WM_BK, B_QK)
        pf_ref[rows, PF_BG:PF_BG + B_V] = _silu(proj(WM_BG, B_V))
        pb_ref[rows, PB_AV:PB_AV + A_W] = _silu(proj(WM_AI, A_W)).astype(BF16)
        pb_ref[rows, PB_BV:PB_BV + B_V] = proj(WM_BV, B_V).astype(BF16)
        z = _dot(h, wz_ref[...]).astype(BF16)
        pre = _dot(z, gu_ref[...]) + gb_ref[...]
        pf_ref[rows, PF_LAF:PF_LAF + 2 * B_QK] = _log_sigmoid(pre) * (1.0 / GLA_TAU)


def _inproj(x, mod, wmain, wz, gup, gb, hgrn_lb, layer, layer_e, tm=INPROJ_TM):
    n_l = hgrn_lb.shape[1]
    return pl.pallas_call(
        functools.partial(_inproj_body, layer_e=layer_e),
        grid=(T // tm,),
        in_specs=[pl.BlockSpec((tm, D), lambda i: (i, 0))] + _mod_specs(layer, (3, 4), tm) + [
            _resident((D, WM_W)),
            _resident((D, LANES)),
            _resident((LANES, 2 * B_QK)),
            _resident((1, 2 * B_QK)),
            _resident((2, n_l, A_W)),
        ],
        out_specs=[
            pl.BlockSpec((tm, PF_W), lambda i: (i, 0)),
            pl.BlockSpec((tm, PB_W), lambda i: (i, 0)),
        ],
        out_shape=[jax.ShapeDtypeStruct((T, PF_W), F32), jax.ShapeDtypeStruct((T, PB_W), BF16)],
        compiler_params=_params(1),
        name="ab_inproj",
    )(x, mod, mod, wmain, wz, gup, gb, hgrn_lb)


SCAN_PROMPT_SEQS = 4
SCAN_UNROLL = 8


def _prefix_rows(x):
    row = lax.broadcasted_iota(jnp.int32, x.shape, 0)
    s = 1
    while s < x.shape[0]:
        x = x + jnp.where(row >= s, pltpu.roll(x, s, axis=0), 0.0)
        s *= 2
    return x


def _scan_body(*refs, seq_len, seqs, pair, has_s0, emit_state):
    n = seq_len // CHUNK
    n_all = seqs * n
    nh = 2 if pair else 1
    it = iter(refs)
    q_ref = next(it)
    if pair:
        k_ref, laf_ref, lab_ref = next(it), next(it), next(it)
    else:
        ff_ref, fb_ref = next(it), next(it)
    g_ref, v_ref, nw_ref = next(it), next(it), next(it)
    s0_ref = next(it) if has_s0 else None
    o_ref = next(it)
    st_ref = next(it) if emit_state else None
    qd_s, oi_s, kv_s, dec_s, sb_s = it

    row = lax.broadcasted_iota(jnp.int32, (CHUNK, CHUNK), 0)
    col = lax.broadcasted_iota(jnp.int32, (CHUNK, CHUNK), 1)
    tril = row >= col
    triu = row <= col
    lane = lax.broadcasted_iota(jnp.int32, (1, LANES), 1)
    lane2 = lax.broadcasted_iota(jnp.int32, (1, 2 * LANES), 1)
    if pair:
        masks = [lane < HALF, lane >= HALF]
        masks2 = [(lane2 % LANES) < HALF, (lane2 % LANES) >= HALF]
    else:
        masks, masks2 = [None], [None]

    def pick(mask, x):
        return x if mask is None else jnp.where(mask, x, jnp.zeros_like(x))

    def rows_of(c):
        return pl.ds(pl.multiple_of(c * CHUNK, CHUNK), CHUNK)

    def loop(body):
        if n_all <= SCAN_UNROLL:
            for c in range(n_all):
                body(c)
        else:
            def fbody(i, carry):
                for u in range(SCAN_UNROLL):
                    body(i * SCAN_UNROLL + u)
                return carry
            lax.fori_loop(0, n_all // SCAN_UNROLL, fbody, 0)

    def phase1(c):
        rows = rows_of(c)
        q = q_ref[rows, :]
        if pair:
            k_f = k_b = k_ref[rows, :]
            la_f, la_b = laf_ref[rows, :], lab_ref[rows, :]
        else:
            f_f, f_b = ff_ref[rows, :], fb_ref[rows, :]
            k_f, k_b = 1.0 - f_f, 1.0 - f_b
            la_f, la_b = jnp.log(f_f), jnp.log(f_b)
        cs = _prefix_rows(jnp.concatenate([la_f, la_b], axis=1))
        cf, cbi = cs[:, :LANES], cs[:, LANES:]
        tot_f, tot_b = cf[CHUNK - 1:CHUNK, :], cbi[CHUNK - 1:CHUNK, :]
        rb = tot_b - cbi + la_b
        ref_f, ref_b = cf[CHUNK // 2 - 1:CHUNK // 2, :], rb[CHUNK // 2:CHUNK // 2 + 1, :]
        qtf = q * jnp.exp(cf - ref_f)
        qtb = q * jnp.exp(rb - ref_b)
        ktf = k_f * jnp.exp(ref_f - cf)
        ktb = k_b * jnp.exp(ref_b - rb)
        qd = jnp.concatenate([qtf * jnp.exp(ref_f), qtb * jnp.exp(ref_b)], axis=1).astype(BF16)
        ku = jnp.concatenate([ktf * jnp.exp(tot_f - ref_f), ktb * jnp.exp(tot_b - ref_b)], axis=1).astype(BF16)
        qd_s[rows, :] = qd
        qt = jnp.concatenate([qtf, qtb], axis=0).astype(BF16)
        kt = jnp.concatenate([ktf, ktb], axis=0).astype(BF16)
        kv = None
        for hh in range(nh):
            v = v_ref[rows, hh * LANES:(hh + 1) * LANES]
            sc = _dot_nt(pick(masks[hh], qt), kt)
            att = jnp.where(tril, sc[:CHUNK, :CHUNK], 0.0) + jnp.where(triu, sc[CHUNK:, CHUNK:], 0.0)
            oi_s[rows, hh * LANES:(hh + 1) * LANES] = _dot(att.astype(BF16), v)
            kv_h = _dot_tn(v, ku)
            kv = kv_h if kv is None else jnp.where(masks2[0], kv, kv_h)
        kv_s[c] = kv
        dec_s[c] = jnp.exp(jnp.concatenate([tot_f, tot_b], axis=1))

    loop(phase1)

    def recurrence(sq, d, reverse):
        cols = slice(d * LANES, (d + 1) * LANES)
        c0 = sq * n
        st0 = s0_ref[sq, d, 0].T if has_s0 else jnp.zeros((LANES, LANES), F32)

        def step(c, st):
            sb_s[c, :, cols] = st.astype(BF16)
            return st * dec_s[c, :, cols] + kv_s[c, :, cols]

        if n <= 8:
            st = st0
            for c in (range(n - 1, -1, -1) if reverse else range(n)):
                st = step(c0 + c, st)
        else:
            st = lax.fori_loop(0, n, lambda i, st: step(c0 + (n - 1 - i if reverse else i), st), st0)
        if emit_state:
            st_ref[sq, d, 0] = st.T

    for sq in range(seqs):
        recurrence(sq, 0, False)
        recurrence(sq, 1, True)

    nw = nw_ref[...]

    def phase2(c):
        rows = rows_of(c)
        qcat = qd_s[rows, :]
        scat = sb_s[c]
        for hh in range(nh):
            cols = slice(hh * LANES, (hh + 1) * LANES)
            o = oi_s[rows, cols] + _dot_nt(pick(masks2[hh], qcat), scat)
            o = o * lax.rsqrt(jnp.mean(o * o, axis=-1, keepdims=True) + RMS_EPS) * nw
            o_ref[rows, cols] = (o * g_ref[rows, cols]).astype(BF16)

    loop(phase2)


def _scan(pf, pb, norm_w, s0, *, prompt, pair):
    seq_len = SEQ if prompt else DEC_SEQ
    nseq = BATCH if prompt else DEC_BATCH
    seqs = SCAN_PROMPT_SEQS if prompt else 1
    rows = seqs * seq_len
    row_off = 0 if prompt else TP // rows
    units = B_HEADS // 2 if pair else A_HEADS
    nh = 2 if pair else 1
    n_all = rows // CHUNK
    has_s0 = s0 is not None
    emit_state = prompt

    def colspec(off, width=LANES):
        base = off // width
        return pl.BlockSpec((rows, width), lambda s, u: (s + row_off, base + u))

    if pair:
        in_specs = [colspec(PF_BQ), colspec(PF_BK), colspec(PF_LAF), colspec(PF_LAB),
                    colspec(PF_BG, 2 * LANES), colspec(PB_BV, 2 * LANES)]
        args = [pf, pf, pf, pf, pf, pb]
    else:
        in_specs = [colspec(PF_AQ), colspec(PF_FF), colspec(PF_FB), colspec(PF_AG), colspec(PB_AV)]
        args = [pf, pf, pf, pf, pb]
    in_specs.append(pl.BlockSpec((1, LANES), lambda s, u: (0, 0)))
    args.append(norm_w.reshape(1, LANES))
    state_spec = pl.BlockSpec((seqs, 2, 1, LANES, LANES), lambda s, u: (s, 0, u, 0, 0))
    if has_s0:
        in_specs.append(state_spec)
        args.append(s0)
    out_specs = [pl.BlockSpec((rows, nh * LANES), lambda s, u: (s, u))]
    out_shape = [jax.ShapeDtypeStruct((nseq * seq_len, units * nh * LANES), BF16)]
    if emit_state:
        out_specs.append(state_spec)
        out_shape.append(jax.ShapeDtypeStruct((nseq, 2, units, LANES, LANES), F32))
    scratch = [
        pltpu.VMEM((rows, 2 * LANES), BF16),
        pltpu.VMEM((rows, nh * LANES), F32),
        pltpu.VMEM((n_all, LANES, 2 * LANES), F32),
        pltpu.VMEM((n_all, 1, 2 * LANES), F32),
        pltpu.VMEM((n_all, LANES, 2 * LANES), BF16),
    ]
    return pl.pallas_call(
        functools.partial(_scan_body, seq_len=seq_len, seqs=seqs, pair=pair, has_s0=has_s0,
                          emit_state=emit_state),
        grid=(nseq // seqs, units),
        in_specs=in_specs,
        out_specs=out_specs,
        out_shape=out_shape,
        scratch_shapes=scratch,
        compiler_params=_params(2),
        name=f"scan_{'p' if prompt else 's'}_{'gla' if pair else 'hgrn'}",
    )(*args)


def _outproj_body(*refs, n_lhs, tm):
    lhs = refs[:2 * n_lhs]
    ws = refs[2 * n_lhs:3 * n_lhs]
    x_ref, m_ref, g_ref, b_ref, o_ref = refs[3 * n_lhs:]

    def compute(group):
        for r in range(0, tm, PROJ_SUB):
            rows = slice(r, r + PROJ_SUB)
            y = _dot(lhs[group][rows, :], ws[0][...])
            for j in range(1, n_lhs):
                y = y + _dot(lhs[2 * j + group][rows, :], ws[j][...])
            z = ALPHA * x_ref[rows, :] + m_ref[...] * y
            o_ref[rows, :] = _layer_norm(z, g_ref[...], b_ref[...])

    in_prompt = pl.program_id(0) < TP // tm
    pl.when(in_prompt)(lambda: compute(0))
    pl.when(jnp.logical_not(in_prompt))(lambda: compute(1))


def _outproj(lhs, ws, x, mod, ln_g, ln_b, layer, tm=OUT_TM):
    n_lhs = len(lhs)
    in_specs, args = [], []
    for a_p, a_s in lhs:
        in_specs += _group_specs(True, tm, a_p.shape[1])
        args += [a_p, a_s]
    in_specs += [_resident(w.shape) for w in ws]
    in_specs += [pl.BlockSpec((tm, D), lambda i: (i, 0))] + _mod_specs(layer, (5,), tm) + _ln_specs(layer, 1)
    return pl.pallas_call(
        functools.partial(_outproj_body, n_lhs=n_lhs, tm=tm),
        grid=(T // tm,),
        in_specs=in_specs,
        out_specs=pl.BlockSpec((tm, D), lambda i: (i, 0)),
        out_shape=jax.ShapeDtypeStruct((T, D), F32),
        compiler_params=_params(1),
        name="mixer_outproj",
    )(*args, *ws, x, mod, ln_g, ln_b)


def _rope_partner(x):
    lane = lax.broadcasted_iota(jnp.int32, x.shape, 1)
    first_half = (lane % (2 * ROPE_FREQS)) < ROPE_FREQS
    return jnp.where(first_half, pltpu.roll(x, LANES - ROPE_FREQS, axis=1), pltpu.roll(x, ROPE_FREQS, axis=1))


def _qkv_body(x_ref, shift_ref, scale_ref, w_ref, wkvt_ref, cos_ref, sin_ref,
              q_ref, k_ref, v_ref, kt_ref, vt_ref):
    qscale = C_HEAD_DIM ** -0.5
    in_prompt = pl.program_id(0) < TP // PROJ_TM

    def modulated(rows):
        return (x_ref[rows, :] * (1.0 + scale_ref[...]) + shift_ref[...]).astype(BF16)

    @pl.when(in_prompt)
    def _():
        for sq in range(PROJ_TM // SEQ):
            rows = slice(sq * SEQ, (sq + 1) * SEQ)
            h = modulated(rows)
            q_ref[rows, :] = (_dot(h, w_ref[:, :C_Q]) * qscale).astype(BF16)
            v_ref[rows, :] = _dot(h, w_ref[:, C_Q + C_KV:]).astype(BF16)
            kt_ref[sq] = _dot_nt(wkvt_ref[:C_KV, :], h)
            vt_ref[sq] = _dot_nt(wkvt_ref[C_KV:, :], h)

    @pl.when(jnp.logical_not(in_prompt))
    def _():
        for sq in range(PROJ_TM // SEQ):
            rows = slice(sq * SEQ, (sq + 1) * SEQ)
            h = modulated(rows)
            cos, sin = cos_ref[rows, :], sin_ref[rows, :]
            zq = _dot(h, w_ref[:, :C_Q])
            zk = _dot(h, w_ref[:, C_Q:C_Q + C_KV])
            v_ref[rows, :] = _dot(h, w_ref[:, C_Q + C_KV:]).astype(BF16)

            def rope(z, cos=cos, sin=sin):
                return z * cos + _rope_partner(z) * sin

            for j in range(C_Q // LANES):
                cols = slice(j * LANES, (j + 1) * LANES)
                q_ref[rows, cols] = (rope(zq[:, cols]) * qscale).astype(BF16)
            for j in range(C_KV // LANES):
                cols = slice(j * LANES, (j + 1) * LANES)
                k_ref[rows, cols] = rope(zk[:, cols]).astype(BF16)


def _qkv(x, mod, w, wkvt, cos, sin, layer):
    tm = PROJ_TM
    n_p = TP // tm
    lat = lambda i: (jnp.maximum(i - n_p, 0), 0)
    ctx = lambda i: (jnp.minimum(i, n_p - 1), 0, 0)
    return pl.pallas_call(
        _qkv_body,
        grid=(T // tm,),
        in_specs=[pl.BlockSpec((tm, D), lambda i: (i, 0))] + _mod_specs(layer, (3, 4), tm) + [
            _resident((D, C_Q + 2 * C_KV)),
            _resident((2 * C_KV, D)),
            pl.BlockSpec((tm, LANES), lat),
            pl.BlockSpec((tm, LANES), lat),
        ],
        out_specs=[
            pl.BlockSpec((tm, C_Q), lambda i: (i, 0)),
            pl.BlockSpec((tm, C_KV), lat),
            pl.BlockSpec((tm, C_KV), lambda i: (i, 0)),
            pl.BlockSpec((tm // SEQ, C_KV, SEQ), ctx),
            pl.BlockSpec((tm // SEQ, C_KV, SEQ), ctx),
        ],
        out_shape=[jax.ShapeDtypeStruct((T, C_Q), BF16), jax.ShapeDtypeStruct((TS, C_KV), BF16),
                   jax.ShapeDtypeStruct((T, C_KV), BF16),
                   jax.ShapeDtypeStruct((BATCH, C_KV, SEQ), F32), jax.ShapeDtypeStruct((BATCH, C_KV, SEQ), F32)],
        compiler_params=_params(1),
        name="c_qkv",
    )(x, mod, mod, w, wkvt, cos, sin)


def _rope_tables():
    t = np.arange(DEC_SEQ)
    pos = np.stack([t // GRID_W, t % GRID_W], axis=1).astype(np.float32)
    inv = (ROPE_BASE ** (-np.arange(ROPE_FREQS, dtype=np.float32) / ROPE_FREQS)).astype(np.float32)
    d = np.arange(C_HEAD_DIM)
    axis = d // (2 * ROPE_FREQS)
    ang = pos[:, axis] * inv[d % ROPE_FREQS][None, :]
    sign = np.where((d % (2 * ROPE_FREQS)) < ROPE_FREQS, -1.0, 1.0)[None, :]
    cos_h, sin_h = np.cos(ang), np.sin(ang) * sign
    reps = LANES // C_HEAD_DIM
    cos = np.tile(np.tile(cos_h, (1, reps)), (DEC_BATCH, 1)).astype(np.float32)
    sin = np.tile(np.tile(sin_h, (1, reps)), (DEC_BATCH, 1)).astype(np.float32)
    return jnp.asarray(cos), jnp.asarray(sin)


def _dup_head(blk, half):
    lane = lax.broadcasted_iota(jnp.int32, blk.shape, 1)
    keep = (lane >= HALF).astype(jnp.int32) == half
    return jnp.where(keep, blk, pltpu.roll(blk, HALF, axis=1))


def _pair_values(v, half):
    lo = lax.broadcasted_iota(jnp.int32, v.shape, 1) < HALF
    v2 = _dup_head(v.astype(F32), half)
    top = jnp.concatenate([jnp.where(lo, v2, 0.0), jnp.where(lo, 1.0, 0.0)], axis=1)
    bot = jnp.concatenate([jnp.where(lo, 0.0, v2), jnp.where(lo, 0.0, 1.0)], axis=1)
    return top.astype(BF16), bot.astype(BF16)


def _softmax_weights(scores, sink):
    mx = sink
    for s in scores:
        mx = jnp.maximum(mx, jnp.max(s, axis=-1, keepdims=True))
    return [jnp.exp(s - mx).astype(BF16) for s in scores], jnp.exp(sink - mx)


def _pair_sinks(sink_ref, jb):
    return (sink_ref[:, jb * LANES:jb * LANES + 1], sink_ref[:, jb * LANES + HALF:jb * LANES + HALF + 1])


KV_PER_BLOCK = LANES // C_HEAD_DIM
TILE_KV = C_KV_HEADS
TILE_KVW = TILE_KV * C_HEAD_DIM
CTX_TILES = BATCH * C_KV_HEADS // TILE_KV
CTX_BLOCKS = TILE_KV * C_GROUPS // 2
PIPE_LAG = 2


def _kv_block(ref, kv):
    lb = kv // KV_PER_BLOCK
    return ref[:, lb * LANES:(lb + 1) * LANES], kv % KV_PER_BLOCK


def _run_pipeline(g, n_tiles, stages):
    for cur in (0, 1):
        steady = jnp.logical_and(g % 2 == cur, jnp.logical_and(g >= PIPE_LAG, g < n_tiles))
        pl.when(steady)(functools.partial(stages, cur, 1 - cur, True, True, True))
    pl.when(g == 0)(functools.partial(stages, 0, 1, True, False, False))
    pl.when(g == 1)(functools.partial(stages, 1, 0, True, True, False))
    pl.when(g == n_tiles)(functools.partial(stages, n_tiles % 2, 1 - n_tiles % 2, False, True, True))
    pl.when(g == n_tiles + 1)(functools.partial(stages, 1 - n_tiles % 2, n_tiles % 2, False, False, True))


def _ctx_tile(g, lag):
    t = jnp.clip(g - lag, 0, CTX_TILES - 1)
    n = C_KV_HEADS // TILE_KV
    return t // n, t % n


def _ctx_attn_body(q_ref, kt_ref, v_ref, sink_ref, o_ref, s_s, e_s, st_s):
    g = pl.program_id(0)
    lo = lax.broadcasted_iota(jnp.int32, (1, LANES), 1) < HALF
    pairs_per_kv = C_GROUPS // 2

    def stages(cur, prev, scores, weights, values):
        for kv in range(TILE_KV if values else 0):
            v_top, v_bot = _pair_values(*_kv_block(v_ref, kv))
            for jb in range(kv * pairs_per_kv, (kv + 1) * pairs_per_kv):
                od = _dot(e_s[cur, 2 * jb], v_top) + _dot(e_s[cur, 2 * jb + 1], v_bot)
                o_ref[:, jb * LANES:(jb + 1) * LANES] = (od[:, :LANES] / (od[:, LANES:] + st_s[cur, jb])).astype(BF16)
        for jb in range(CTX_BLOCKS if weights else 0):
            terms = []
            for hh, sink in enumerate(_pair_sinks(sink_ref, jb)):
                (e,), term = _softmax_weights([s_s[prev, 2 * jb + hh]], sink)
                e_s[prev, 2 * jb + hh] = e
                terms.append(term)
            st_s[prev, jb] = jnp.where(lo, terms[0], terms[1])
        for kv in range(TILE_KV if scores else 0):
            kt = kt_ref[kv * C_HEAD_DIM:(kv + 1) * C_HEAD_DIM, :]
            zero = jnp.zeros_like(kt)
            kt_a = jnp.concatenate([kt, zero], axis=0).astype(BF16)
            kt_b = jnp.concatenate([zero, kt], axis=0).astype(BF16)
            for jb in range(kv * pairs_per_kv, (kv + 1) * pairs_per_kv):
                q = q_ref[:, jb * LANES:(jb + 1) * LANES]
                s_s[cur, 2 * jb] = _dot(q, kt_a)
                s_s[cur, 2 * jb + 1] = _dot(q, kt_b)

    _run_pipeline(g, CTX_TILES, stages)


def _ctx_attn(q, kt, v, sink_cols):
    gw = CTX_BLOCKS * LANES

    def at(lag, fn):
        return lambda g: fn(*_ctx_tile(g, lag))

    return pl.pallas_call(
        _ctx_attn_body,
        grid=(CTX_TILES + PIPE_LAG,),
        in_specs=[
            pl.BlockSpec((SEQ, gw), at(0, lambda b, p: (b, p))),
            pl.BlockSpec((None, TILE_KVW, SEQ), at(0, lambda b, p: (b, p, 0))),
            pl.BlockSpec((SEQ, TILE_KVW), at(2, lambda b, p: (b, p))),
            pl.BlockSpec((1, gw), at(1, lambda b, p: (0, p))),
        ],
        out_specs=pl.BlockSpec((SEQ, gw), at(2, lambda b, p: (b, p))),
        out_shape=jax.ShapeDtypeStruct((TP, C_Q), BF16),
        scratch_shapes=[
            pltpu.VMEM((2, 2 * CTX_BLOCKS, SEQ, SEQ), F32),
            pltpu.VMEM((2, 2 * CTX_BLOCKS, SEQ, SEQ), BF16),
            pltpu.VMEM((2, CTX_BLOCKS, SEQ, LANES), F32),
        ],
        compiler_params=_params(1),
        name="c_attn_ctx",
    )(q, kt, v, sink_cols)


LAT_TQ = 128
LAT_WIN = LAT_TQ + 2 * WINDOW
LAT_NQ = DEC_SEQ // LAT_TQ
LAT_KVB = C_KV_HEADS // TILE_KV
LAT_TILES = DEC_BATCH * LAT_KVB * LAT_NQ


def _lat_tile(g, lag):
    t = jnp.clip(g - lag, 0, LAT_TILES - 1)
    bp = t // LAT_NQ
    return bp // LAT_KVB, bp % LAT_KVB, t % LAT_NQ


def _lat_window(j):
    return pl.multiple_of(jnp.clip(j * LAT_TQ - WINDOW, 0, DEC_SEQ - LAT_WIN), LANES)


def _lat_attn_body(q_ref, k_ref, v_ref, ck_ref, cv_ref, sink_ref, o_ref,
                   ka_s, kb_s, cka_s, ckb_s, va_s, vb_s, cva_s, cvb_s, sl_s, sc_s, el_s, ec_s, st_s):
    g = pl.program_id(0)
    lo = lax.broadcasted_iota(jnp.int32, (1, LANES), 1) < HALF
    pairs_per_kv = C_GROUPS // 2
    b_a, p_a, j_a = _lat_tile(g, 0)
    b_c, p_c, j_c = _lat_tile(g, 2)
    v_slot_a = (b_a * LAT_KVB + p_a) % 2
    v_slot_c = (b_c * LAT_KVB + p_c) % 2

    @pl.when(jnp.logical_and(j_a == 0, g < LAT_TILES))
    def _():
        for kv in range(TILE_KV):
            for src, a_s, b_s in ((k_ref, ka_s, kb_s), (ck_ref, cka_s, ckb_s)):
                blk, kh = _kv_block(src, kv)
                lo2 = lax.broadcasted_iota(jnp.int32, blk.shape, 1) < HALF
                k2 = _dup_head(blk.astype(F32), kh)
                a_s[kv] = jnp.where(lo2, k2, 0.0).astype(BF16)
                b_s[kv] = jnp.where(lo2, 0.0, k2).astype(BF16)
            va_s[v_slot_a, kv], vb_s[v_slot_a, kv] = _pair_values(*_kv_block(v_ref, kv))
            cva_s[v_slot_a, kv], cvb_s[v_slot_a, kv] = _pair_values(*_kv_block(cv_ref, kv))

    def stages(cur, prev, scores, weights, values):
        win_c = pl.ds(_lat_window(j_c), LAT_WIN)
        for kv in range(TILE_KV if values else 0):
            v_loc = (va_s[v_slot_c, kv, win_c, :], vb_s[v_slot_c, kv, win_c, :])
            v_ctx = (cva_s[v_slot_c, kv], cvb_s[v_slot_c, kv])
            for jb in range(kv * pairs_per_kv, (kv + 1) * pairs_per_kv):
                od = None
                for hh in range(2):
                    part = _dot(el_s[cur, 2 * jb + hh], v_loc[hh]) + _dot(ec_s[cur, 2 * jb + hh], v_ctx[hh])
                    od = part if od is None else od + part
                o_ref[:, jb * LANES:(jb + 1) * LANES] = (od[:, :LANES] / (od[:, LANES:] + st_s[cur, jb])).astype(BF16)
        for jb in range(CTX_BLOCKS if weights else 0):
            terms = []
            for hh, sink in enumerate(_pair_sinks(sink_ref, jb)):
                (e_loc, e_ctx), term = _softmax_weights([sl_s[prev, 2 * jb + hh], sc_s[prev, 2 * jb + hh]], sink)
                el_s[prev, 2 * jb + hh] = e_loc
                ec_s[prev, 2 * jb + hh] = e_ctx
                terms.append(term)
            st_s[prev, jb] = jnp.where(lo, terms[0], terms[1])
        start = _lat_window(j_a)
        win = pl.ds(start, LAT_WIN)
        qpos = j_a * LAT_TQ + lax.broadcasted_iota(jnp.int32, (LAT_TQ, LAT_WIN), 0)
        kpos = start + lax.broadcasted_iota(jnp.int32, (LAT_TQ, LAT_WIN), 1)
        band = jnp.abs(qpos - kpos) <= WINDOW
        for kv in range(TILE_KV if scores else 0):
            for jb in range(kv * pairs_per_kv, (kv + 1) * pairs_per_kv):
                q = q_ref[:, jb * LANES:(jb + 1) * LANES]
                for hh, (kl, kc) in enumerate(((ka_s, cka_s), (kb_s, ckb_s))):
                    sl_s[cur, 2 * jb + hh] = jnp.where(band, _dot_nt(q, kl[kv, win, :]), -jnp.inf)
                    sc_s[cur, 2 * jb + hh] = _dot_nt(q, kc[kv])

    _run_pipeline(g, LAT_TILES, stages)


def _lat_attn(q, k, v, ck, cv, sink_cols):
    gw = CTX_BLOCKS * LANES
    q_off = TP // LAT_TQ
    kv_off = TP // DEC_SEQ
    kvb, n_heads = TILE_KV, 2 * CTX_BLOCKS

    def at(lag, fn):
        return lambda g: fn(*_lat_tile(g, lag))

    return pl.pallas_call(
        _lat_attn_body,
        grid=(LAT_TILES + PIPE_LAG,),
        in_specs=[
            pl.BlockSpec((LAT_TQ, gw), at(0, lambda b, p, j: (q_off + b * LAT_NQ + j, p))),
            pl.BlockSpec((DEC_SEQ, TILE_KVW), at(0, lambda b, p, j: (b, p))),
            pl.BlockSpec((DEC_SEQ, TILE_KVW), at(0, lambda b, p, j: (kv_off + b, p))),
            pl.BlockSpec((PAST_LEN, TILE_KVW), at(0, lambda b, p, j: (b, p))),
            pl.BlockSpec((PAST_LEN, TILE_KVW), at(0, lambda b, p, j: (b, p))),
            pl.BlockSpec((1, gw), at(1, lambda b, p, j: (0, p))),
        ],
        out_specs=pl.BlockSpec((LAT_TQ, gw), at(2, lambda b, p, j: (b * LAT_NQ + j, p))),
        out_shape=jax.ShapeDtypeStruct((TS, C_Q), BF16),
        scratch_shapes=[
            pltpu.VMEM((kvb, DEC_SEQ, LANES), BF16), pltpu.VMEM((kvb, DEC_SEQ, LANES), BF16),
            pltpu.VMEM((kvb, PAST_LEN, LANES), BF16), pltpu.VMEM((kvb, PAST_LEN, LANES), BF16),
            pltpu.VMEM((2, kvb, DEC_SEQ, 2 * LANES), BF16), pltpu.VMEM((2, kvb, DEC_SEQ, 2 * LANES), BF16),
            pltpu.VMEM((2, kvb, PAST_LEN, 2 * LANES), BF16), pltpu.VMEM((2, kvb, PAST_LEN, 2 * LANES), BF16),
            pltpu.VMEM((2, n_heads, LAT_TQ, LAT_WIN), F32), pltpu.VMEM((2, n_heads, LAT_TQ, PAST_LEN), F32),
            pltpu.VMEM((2, n_heads, LAT_TQ, LAT_WIN), BF16), pltpu.VMEM((2, n_heads, LAT_TQ, PAST_LEN), BF16),
            pltpu.VMEM((2, CTX_BLOCKS, LAT_TQ, LANES), F32),
        ],
        compiler_params=_params(1),
        name="c_attn_latent",
    )(q, k, v, ck, cv, sink_cols)


def kernel(x_prompt, x_sample, state_hgrn, state_gla, cache_k, cache_v, c, c_ctx, w_mod, b_mod, ln_g, ln_b,
           ffn_w1, ffn_w3, ffn_w2, w_in_ab, hgrn_lb, gla_gate_up, gla_gate_b, norm_a, norm_b, w_out_ab,
           w_qkv_c, sink_c, w_out_c):
    cs = jnp.zeros((8, D), F32).at[0].set(c_ctx).at[1:1 + DEC_BATCH].set(c)
    mod = _mod_vectors(cs, w_mod, b_mod).reshape(DEPTH, 8, 1, N_MOD * D)
    ffn_ws = (ffn_w1, ffn_w3, ffn_w2)
    ln_g, ln_b = ln_g.reshape(DEPTH, 3, 1, D), ln_b.reshape(DEPTH, 3, 1, D)

    def ffn(xs, ws, layer, sub, split_out=False, cast_next=None):
        casts = None if cast_next is None else (ffn_ws, cast_next)
        return _ffn_sublayer(xs, mod, *ws, ln_g, ln_b, layer, sub, split_out=split_out, casts=casts)

    x, *ws_01 = ffn([x_prompt.reshape(TP, D), x_sample.reshape(TS, D)], [w[0, 0].astype(BF16) for w in ffn_ws],
                    0, 0, cast_next=(0, 1))
    w_in = w_in_ab[0]
    o_aq, o_ai, o_ff, o_fb, o_ag = 0, A_W, 2 * A_W, 3 * A_W, 4 * A_W
    o_bq = 5 * A_W
    o_bk, o_bv = o_bq + B_QK, o_bq + 2 * B_QK
    o_bg = o_bv + B_V
    o_z = o_bg + B_V
    order = [(o_aq, A_W), (o_ff, A_W), (o_fb, A_W), (o_ag, A_W), (o_bq, B_QK), (o_bk, B_QK), (o_bg, B_V),
             (o_ai, A_W), (o_bv, B_V)]
    wmain = jnp.concatenate([w_in[:, o:o + w] for o, w in order], axis=1).astype(BF16)
    wz = jnp.pad(w_in[:, o_z:o_z + 2 * GATE_RANK], ((0, 0), (0, LANES - 2 * GATE_RANK))).astype(BF16)
    gup = jnp.zeros((LANES, 2 * B_QK), F32)
    gup = gup.at[:GATE_RANK, :B_QK].set(gla_gate_up[0, 0]).at[GATE_RANK:2 * GATE_RANK, B_QK:].set(gla_gate_up[0, 1])
    gb = gla_gate_b[0].reshape(1, 2 * B_QK)
    pf, pb = _inproj(x, mod, wmain, wz, gup.astype(BF16), gb, hgrn_lb, 0, 0)

    s0_a = state_hgrn[:, 0]
    s0_b = state_gla[:, 0].reshape(DEC_BATCH, 2, B_HEADS // 2, LANES, B_DV)
    oa_p, st_a = _scan(pf, pb, norm_a[0], None, prompt=True, pair=False)
    ob_p, st_b = _scan(pf, pb, norm_b[0], None, prompt=True, pair=True)
    (oa_s,) = _scan(pf, pb, norm_a[0], s0_a, prompt=False, pair=False)
    (ob_s,) = _scan(pf, pb, norm_b[0], s0_b, prompt=False, pair=True)
    w_out = w_out_ab[0].astype(BF16)
    x = _outproj([(oa_p, oa_s), (ob_p, ob_s)], [w_out[:A_W], w_out[A_W:]], x, mod, ln_g, ln_b, 0)
    x, *ws_10 = ffn([x], ws_01, 0, 1, cast_next=(1, 0))
    new_hgrn = st_a.reshape(BATCH, 1, 2, A_HEADS, A_DK, A_DV)
    new_gla = st_b.reshape(BATCH, 1, 2, B_HEADS, B_DK, B_DV)

    x, *ws_11 = ffn([x], ws_10, 1, 0, cast_next=(1, 1))
    cos, sin = _rope_tables()
    w_qkv = w_qkv_c[0].astype(BF16)
    q, k, v, kt, vt = _qkv(x, mod, w_qkv, w_qkv[:, C_Q:].T, cos, sin, 1)
    sink_cols = jnp.repeat(sink_c[0], C_HEAD_DIM).reshape(1, C_Q)
    o_p = _ctx_attn(q, kt, v, sink_cols)
    ck = cache_k[:, 0].reshape(DEC_BATCH * PAST_LEN, C_KV)
    cv = cache_v[:, 0].reshape(DEC_BATCH * PAST_LEN, C_KV)
    o_s = _lat_attn(q, k, v, ck, cv, sink_cols)
    x = _outproj([(o_p, o_s)], [w_out_c[0].astype(BF16)], x, mod, ln_g, ln_b, 1)
    y_p, y_s = ffn([x], ws_11, 1, 1, split_out=True)

    def cache_layout(zt):
        return zt.reshape(BATCH, 1, C_KV_HEADS, C_HEAD_DIM, SEQ).transpose(0, 1, 4, 2, 3)

    new_k, new_v = cache_layout(kt), cache_layout(vt)

    return (y_p.reshape(BATCH, SEQ, D), y_s.reshape(DEC_BATCH, DEC_SEQ, D), new_hgrn, new_gla, new_k, new_v)
```

```python
import functools

import jax
import jax.numpy as jnp
import numpy as np
from jax import lax
from jax.experimental import pallas as pl
from jax.experimental.pallas import tpu as pltpu

D = 1024
BATCH, SEQ = 16, 256
DEC_BATCH, DEC_SEQ = 2, 2048
PAST_LEN = 512
GRID_W = 64
D_FF = 2816
N_MOD = 9
A_HEADS, A_DK, A_DV = 4, 128, 128
A_W = A_HEADS * A_DK
B_HEADS, B_DK, B_DV = 4, 64, 128
B_QK = B_HEADS * B_DK
B_V = B_HEADS * B_DV
GATE_RANK = 16
GLA_TAU = 16.0
CHUNK = 128
C_HEADS, C_KV_HEADS, C_HEAD_DIM = 16, 4, 64
C_GROUPS = C_HEADS // C_KV_HEADS
C_Q = C_HEADS * C_HEAD_DIM
C_KV = C_KV_HEADS * C_HEAD_DIM
WINDOW = 128
ROPE_FREQS = C_HEAD_DIM // 4
ROPE_BASE = 10000.0
DEPTH = 2
ALPHA = (2.0 * DEPTH) ** 0.25
LN_EPS = 1e-5
RMS_EPS = 1e-6

TP = BATCH * SEQ
TS = DEC_BATCH * DEC_SEQ
T = TP + TS

LANES = 128
HALF = LANES // 2
FFN_TM = 1024
FFN_TM_SPLIT = 512
FFN_SUB = 256
PROJ_TM = 4 * SEQ
INPROJ_TM = 512
PROJ_SUB = 256
OUT_TM = 1024
VMEM_LIMIT = 60 * 1024 * 1024

F32 = jnp.float32
BF16 = jnp.bfloat16


def _dot(a, b):
    return jnp.dot(a, b, preferred_element_type=F32)


def _dot_nt(a, b):
    return lax.dot_general(a, b, (((1,), (1,)), ((), ())), preferred_element_type=F32)


def _dot_tn(a, b):
    return lax.dot_general(a, b, (((0,), (0,)), ((), ())), preferred_element_type=F32)


def _silu(x):
    return x * jax.nn.sigmoid(x)


def _layer_norm(z, g, b):
    mu = jnp.mean(z, axis=-1, keepdims=True)
    zc = z - mu
    var = jnp.mean(zc * zc, axis=-1, keepdims=True)
    return zc * lax.rsqrt(var + LN_EPS) * g + b


def _seg_of_tile(i, tm):
    n_p = TP // tm
    n_s = DEC_SEQ // tm
    return jnp.where(i < n_p, 0, 1 + lax.div(jnp.maximum(i - n_p, 0), n_s))


def _params(n_axes):
    return pltpu.CompilerParams(dimension_semantics=("arbitrary",) * n_axes, vmem_limit_bytes=VMEM_LIMIT)


def _resident(shape):
    nd = len(shape)
    return pl.BlockSpec(shape, lambda *_: (0,) * nd, pipeline_mode=pl.Buffered(1))


def _resident_slice(shape, lead):
    block = (None,) * len(lead) + tuple(shape)
    return pl.BlockSpec(block, lambda *_: tuple(lead) + (0,) * len(shape), pipeline_mode=pl.Buffered(1))


def _mod_specs(layer, cols, tm):
    return [pl.BlockSpec((None, None, 1, D), functools.partial(
        lambda i, c: (layer, _seg_of_tile(i, tm), 0, c), c=c)) for c in cols]


def _ln_specs(layer, idx):
    return [_resident_slice((1, D), (layer, idx))] * 2


BF16_SUBLANES = 16


def _cast_plan(ws, lead, n_steps, step_of):
    in_specs, out_specs, out_shapes = [], [], []
    for w in ws:
        rows, cols = w.shape[len(lead):]
        blk = next(b for b in range(BF16_SUBLANES, rows + 1, BF16_SUBLANES)
                   if rows % b == 0 and rows // b <= n_steps)
        last = rows // blk - 1
        in_specs.append(pl.BlockSpec((None,) * len(lead) + (blk, cols), functools.partial(
            lambda *g, last: tuple(lead) + (jnp.minimum(step_of(*g), last), 0), last=last)))
        out_specs.append(pl.BlockSpec((blk, cols), functools.partial(
            lambda *g, last: (jnp.minimum(step_of(*g), last), 0), last=last)))
        out_shapes.append(jax.ShapeDtypeStruct((rows, cols), BF16))
    return in_specs, out_specs, out_shapes


def _hosting_casts(body, n_in, n_out, n_cast):
    def hosted(*refs, **kw):
        ins, refs = refs[:n_in], refs[n_in:]
        cast_in, refs = refs[:n_cast], refs[n_cast:]
        outs, refs = refs[:n_out], refs[n_out:]
        cast_out, scratch = refs[:n_cast], refs[n_cast:]
        for src, dst in zip(cast_in, cast_out):
            dst[...] = src[...].astype(BF16)
        body(*ins, *outs, *scratch, **kw)
    return hosted


def _mod_body(c_ref, w_ref, b_ref, o_ref):
    c = c_ref[...]
    s = _silu(c).astype(BF16)
    o_ref[0] = _dot(s, w_ref[0].astype(BF16)) + b_ref[0]


def _mod_vectors(cs, w_mod, b_mod):
    tn = 1536
    n = N_MOD * D
    return pl.pallas_call(
        _mod_body,
        grid=(DEPTH, n // tn),
        in_specs=[
            pl.BlockSpec((8, D), lambda l, j: (0, 0)),
            pl.BlockSpec((1, D, tn), lambda l, j: (l, 0, j)),
            pl.BlockSpec((1, 1, tn), lambda l, j: (l, 0, j)),
        ],
        out_specs=pl.BlockSpec((1, 8, tn), lambda l, j: (l, 0, j)),
        out_shape=jax.ShapeDtypeStruct((DEPTH, 8, n), F32),
        compiler_params=_params(2),
        name="mod_vectors",
    )(cs, w_mod, b_mod.reshape(DEPTH, 1, n))


def _ffn_body(*refs, n_x, n_o, tm):
    x_refs = refs[:n_x]
    shift_ref, scale_ref, gate_ref, w1_ref, w3_ref, w2_ref, g_ref, b_ref = refs[n_x:n_x + 8]
    o_refs = refs[n_x + 8:]

    def compute(x_ref, o_ref):
        shift, scale, gate = shift_ref[...], scale_ref[...], gate_ref[...]
        for r in range(0, tm, FFN_SUB):
            rows = slice(r, r + FFN_SUB)
            x = x_ref[rows, :]
            h = (x * (1.0 + scale) + shift).astype(BF16)
            a = _dot(h, w1_ref[...])
            b = _dot(h, w3_ref[...])
            g = (_silu(a) * b).astype(BF16)
            y = _dot(g, w2_ref[...])
            z = ALPHA * x + (0.5 * gate) * y
            o_ref[rows, :] = _layer_norm(z, g_ref[...], b_ref[...])

    if n_x == 1 and n_o == 1:
        compute(x_refs[0], o_refs[0])
    else:
        in_prompt = pl.program_id(0) < TP // tm
        pl.when(in_prompt)(lambda: compute(x_refs[0], o_refs[0]))
        pl.when(jnp.logical_not(in_prompt))(lambda: compute(x_refs[-1], o_refs[-1]))


def _group_specs(split, tm, width=D):
    if not split:
        return [pl.BlockSpec((tm, width), lambda i: (i, 0))]
    n_p = TP // tm
    return [pl.BlockSpec((tm, width), lambda i: (jnp.minimum(i, n_p - 1), 0)),
            pl.BlockSpec((tm, width), lambda i: (jnp.maximum(i - n_p, 0), 0))]


def _ffn_sublayer(xs, mod, w1, w3, w2, ln_g, ln_b, layer, sub, split_out=False, casts=None):
    n_x, n_o = len(xs), 2 if split_out else 1
    tm = FFN_TM if n_x == n_o == 1 else FFN_TM_SPLIT
    out_shape = ([jax.ShapeDtypeStruct((TP, D), F32), jax.ShapeDtypeStruct((TS, D), F32)] if split_out
                 else [jax.ShapeDtypeStruct((T, D), F32)])
    mod_lo = 6 * sub
    body = functools.partial(_ffn_body, n_x=n_x, n_o=n_o, tm=tm)
    in_specs = _group_specs(n_x == 2, tm) + _mod_specs(layer, (mod_lo, mod_lo + 1, mod_lo + 2), tm) + [
        _resident((D, D_FF)),
        _resident((D, D_FF)),
        _resident((D_FF, D)),
    ] + _ln_specs(layer, 2 * sub)
    out_specs = _group_specs(split_out, tm)
    args = [*xs, mod, mod, mod, w1, w3, w2, ln_g, ln_b]
    if casts is not None:
        ws, lead = casts
        c_in, c_out, c_shapes = _cast_plan(ws, lead, T // tm, lambda i: i)
        body = _hosting_casts(body, len(in_specs), len(out_specs), len(ws))
        in_specs, out_specs, out_shape = in_specs + c_in, out_specs + c_out, out_shape + c_shapes
        args = args + list(ws)
    return pl.pallas_call(
        body,
        grid=(T // tm,),
        in_specs=in_specs,
        out_specs=out_specs,
        out_shape=out_shape,
        compiler_params=_params(1),
        name="ffn_sublayer",
    )(*args)


PF_AQ, PF_FF, PF_FB, PF_AG = 0, 512, 1024, 1536
PF_BQ, PF_BK, PF_BG, PF_LAF, PF_LAB = 2048, 2304, 2560, 3072, 3328
PF_W = 3584
PB_AV, PB_BV = 0, 512
PB_W = 1024
WM_AQ, WM_FF, WM_FB, WM_AG, WM_BQ, WM_BK, WM_BG, WM_AI, WM_BV = 0, 512, 1024, 1536, 2048, 2304, 2560, 3072, 3584
WM_W = 4096


def _log_sigmoid(x):
    return jnp.minimum(x, 0.0) - jnp.log(1.0 + jnp.exp(-jnp.abs(x)))


def _inproj_body(x_ref, shift_ref, scale_ref, w_ref, wz_ref, gu_ref, gb_ref, lb_ref, pf_ref, pb_ref, *, layer_e):
    def lower_bound(d):
        l = lb_ref[d]
        e = jnp.exp(l - jnp.max(l, axis=0, keepdims=True))
        sm = e / jnp.sum(e, axis=0, keepdims=True)
        return jnp.sum(sm[:layer_e + 1], axis=0, keepdims=True)

    lbs = [lower_bound(0), lower_bound(1)]
    for r in range(0, x_ref.shape[0], PROJ_SUB):
        rows = slice(r, r + PROJ_SUB)
        h = (x_ref[rows, :] * (1.0 + scale_ref[...]) + shift_ref[...]).astype(BF16)

        def proj(off, width):
            return _dot(h, w_ref[:, off:off + width])

        pf_ref[rows, PF_AQ:PF_AQ + A_W] = proj(WM_AQ, A_W)
        for lb, wm, pf in ((lbs[0], WM_FF, PF_FF), (lbs[1], WM_FB, PF_FB)):
            pf_ref[rows, pf:pf + A_W] = lb + (1.0 - lb) * jax.nn.sigmoid(proj(wm, A_W))
        pf_ref[rows, PF_AG:PF_AG + A_W] = _silu(proj(WM_AG, A_W))
        pf_ref[rows, PF_BQ:PF_BQ + B_QK] = proj(WM_BQ, B_QK) * (B_DK ** -0.5)
        pf_ref[rows, PF_BK:PF_BK + B_QK] = proj(WM_BK, B_QK)
        pf_ref[rows, PF_BG:PF_BG + B_V] = _silu(proj(WM_BG, B_V))
        pb_ref[rows, PB_AV:PB_AV + A_W] = _silu(proj(WM_AI, A_W)).astype(BF16)
        pb_ref[rows, PB_BV:PB_BV + B_V] = proj(WM_BV, B_V).astype(BF16)
        z = _dot(h, wz_ref[...]).astype(BF16)
        pre = _dot(z, gu_ref[...]) + gb_ref[...]
        pf_ref[rows, PF_LAF:PF_LAF + 2 * B_QK] = _log_sigmoid(pre) * (1.0 / GLA_TAU)


def _inproj(x, mod, wmain, wz, gup, gb, hgrn_lb, layer, layer_e, tm=INPROJ_TM):
    n_l = hgrn_lb.shape[1]
    return pl.pallas_call(
        functools.partial(_inproj_body, layer_e=layer_e),
        grid=(T // tm,),
        in_specs=[pl.BlockSpec((tm, D), lambda i: (i, 0))] + _mod_specs(layer, (3, 4), tm) + [
            _resident((D, WM_W)),
            _resident((D, LANES)),
            _resident((LANES, 2 * B_QK)),
            _resident((1, 2 * B_QK)),
            _resident((2, n_l, A_W)),
        ],
        out_specs=[
            pl.BlockSpec((tm, PF_W), lambda i: (i, 0)),
            pl.BlockSpec((tm, PB_W), lambda i: (i, 0)),
        ],
        out_shape=[jax.ShapeDtypeStruct((T, PF_W), F32), jax.ShapeDtypeStruct((T, PB_W), BF16)],
        compiler_params=_params(1),
        name="ab_inproj",
    )(x, mod, mod, wmain, wz, gup, gb, hgrn_lb)


SCAN_PROMPT_SEQS = 4
SCAN_UNROLL = 8


def _prefix_rows(x):
    row = lax.broadcasted_iota(jnp.int32, x.shape, 0)
    s = 1
    while s < x.shape[0]:
        x = x + jnp.where(row >= s, pltpu.roll(x, s, axis=0), 0.0)
        s *= 2
    return x


def _scan_body(*refs, seq_len, seqs, pair, has_s0, emit_state):
    n = seq_len // CHUNK
    n_all = seqs * n
    nh = 2 if pair else 1
    it = iter(refs)
    q_ref = next(it)
    if pair:
        k_ref, laf_ref, lab_ref = next(it), next(it), next(it)
    else:
        ff_ref, fb_ref = next(it), next(it)
    g_ref, v_ref, nw_ref = next(it), next(it), next(it)
    s0_ref = next(it) if has_s0 else None
    o_ref = next(it)
    st_ref = next(it) if emit_state else None
    qd_s, oi_s, kv_s, dec_s, sb_s = it

    row = lax.broadcasted_iota(jnp.int32, (CHUNK, CHUNK), 0)
    col = lax.broadcasted_iota(jnp.int32, (CHUNK, CHUNK), 1)
    tril = row >= col
    triu = row <= col
    lane = lax.broadcasted_iota(jnp.int32, (1, LANES), 1)
    lane2 = lax.broadcasted_iota(jnp.int32, (1, 2 * LANES), 1)
    if pair:
        masks = [lane < HALF, lane >= HALF]
        masks2 = [(lane2 % LANES) < HALF, (lane2 % LANES) >= HALF]
    else:
        masks, masks2 = [None], [None]

    def pick(mask, x):
        return x if mask is None else jnp.where(mask, x, jnp.zeros_like(x))

    def rows_of(c):
        return pl.ds(pl.multiple_of(c * CHUNK, CHUNK), CHUNK)

    def loop(body):
        if n_all <= SCAN_UNROLL:
            for c in range(n_all):
                body(c)
        else:
            def fbody(i, carry):
                for u in range(SCAN_UNROLL):
                    body(i * SCAN_UNROLL + u)
                return carry
            lax.fori_loop(0, n_all // SCAN_UNROLL, fbody, 0)

    def phase1(c):
        rows = rows_of(c)
        q = q_ref[rows, :]
        if pair:
            k_f = k_b = k_ref[rows, :]
            la_f, la_b = laf_ref[rows, :], lab_ref[rows, :]
        else:
            f_f, f_b = ff_ref[rows, :], fb_ref[rows, :]
            k_f, k_b = 1.0 - f_f, 1.0 - f_b
            la_f, la_b = jnp.log(f_f), jnp.log(f_b)
        cs = _prefix_rows(jnp.concatenate([la_f, la_b], axis=1))
        cf, cbi = cs[:, :LANES], cs[:, LANES:]
        tot_f, tot_b = cf[CHUNK - 1:CHUNK, :], cbi[CHUNK - 1:CHUNK, :]
        rb = tot_b - cbi + la_b
        ref_f, ref_b = cf[CHUNK // 2 - 1:CHUNK // 2, :], rb[CHUNK // 2:CHUNK // 2 + 1, :]
        qtf = q * jnp.exp(cf - ref_f)
        qtb = q * jnp.exp(rb - ref_b)
        ktf = k_f * jnp.exp(ref_f - cf)
        ktb = k_b * jnp.exp(ref_b - rb)
        qd = jnp.concatenate([qtf * jnp.exp(ref_f), qtb * jnp.exp(ref_b)], axis=1).astype(BF16)
        ku = jnp.concatenate([ktf * jnp.exp(tot_f - ref_f), ktb * jnp.exp(tot_b - ref_b)], axis=1).astype(BF16)
        qd_s[rows, :] = qd
        qt = jnp.concatenate([qtf, qtb], axis=0).astype(BF16)
        kt = jnp.concatenate([ktf, ktb], axis=0).astype(BF16)
        kv = None
        for hh in range(nh):
            v = v_ref[rows, hh * LANES:(hh + 1) * LANES]
            sc = _dot_nt(pick(masks[hh], qt), kt)
            att = jnp.where(tril, sc[:CHUNK, :CHUNK], 0.0) + jnp.where(triu, sc[CHUNK:, CHUNK:], 0.0)
            oi_s[rows, hh * LANES:(hh + 1) * LANES] = _dot(att.astype(BF16), v)
            kv_h = _dot_tn(v, ku)
            kv = kv_h if kv is None else jnp.where(masks2[0], kv, kv_h)
        kv_s[c] = kv
        dec_s[c] = jnp.exp(jnp.concatenate([tot_f, tot_b], axis=1))

    loop(phase1)

    def recurrence(sq, d, reverse):
        cols = slice(d * LANES, (d + 1) * LANES)
        c0 = sq * n
        st0 = s0_ref[sq, d, 0].T if has_s0 else jnp.zeros((LANES, LANES), F32)

        def step(c, st):
            sb_s[c, :, cols] = st.astype(BF16)
            return st * dec_s[c, :, cols] + kv_s[c, :, cols]

        if n <= 8:
            st = st0
            for c in (range(n - 1, -1, -1) if reverse else range(n)):
                st = step(c0 + c, st)
        else:
            st = lax.fori_loop(0, n, lambda i, st: step(c0 + (n - 1 - i if reverse else i), st), st0)
        if emit_state:
            st_ref[sq, d, 0] = st.T

    for sq in range(seqs):
        recurrence(sq, 0, False)
        recurrence(sq, 1, True)

    nw = nw_ref[...]

    def phase2(c):
        rows = rows_of(c)
        qcat = qd_s[rows, :]
        scat = sb_s[c]
        for hh in range(nh):
            cols = slice(hh * LANES, (hh + 1) * LANES)
            o = oi_s[rows, cols] + _dot_nt(pick(masks2[hh], qcat), scat)
            o = o * lax.rsqrt(jnp.mean(o * o, axis=-1, keepdims=True) + RMS_EPS) * nw
            o_ref[rows, cols] = (o * g_ref[rows, cols]).astype(BF16)

    loop(phase2)


def _scan(pf, pb, norm_w, s0, *, prompt, pair):
    seq_len = SEQ if prompt else DEC_SEQ
    nseq = BATCH if prompt else DEC_BATCH
    seqs = SCAN_PROMPT_SEQS if prompt else 1
    rows = seqs * seq_len
    row_off = 0 if prompt else TP // rows
    units = B_HEADS // 2 if pair else A_HEADS
    nh = 2 if pair else 1
    n_all = rows // CHUNK
    has_s0 = s0 is not None
    emit_state = prompt

    def colspec(off, width=LANES):
        base = off // width
        return pl.BlockSpec((rows, width), lambda s, u: (s + row_off, base + u))

    if pair:
        in_specs = [colspec(PF_BQ), colspec(PF_BK), colspec(PF_LAF), colspec(PF_LAB),
                    colspec(PF_BG, 2 * LANES), colspec(PB_BV, 2 * LANES)]
        args = [pf, pf, pf, pf, pf, pb]
    else:
        in_specs = [colspec(PF_AQ), colspec(PF_FF), colspec(PF_FB), colspec(PF_AG), colspec(PB_AV)]
        args = [pf, pf, pf, pf, pb]
    in_specs.append(pl.BlockSpec((1, LANES), lambda s, u: (0, 0)))
    args.append(norm_w.reshape(1, LANES))
    state_spec = pl.BlockSpec((seqs, 2, 1, LANES, LANES), lambda s, u: (s, 0, u, 0, 0))
    if has_s0:
        in_specs.append(state_spec)
        args.append(s0)
    out_specs = [pl.BlockSpec((rows, nh * LANES), lambda s, u: (s, u))]
    out_shape = [jax.ShapeDtypeStruct((nseq * seq_len, units * nh * LANES), BF16)]
    if emit_state:
        out_specs.append(state_spec)
        out_shape.append(jax.ShapeDtypeStruct((nseq, 2, units, LANES, LANES), F32))
    scratch = [
        pltpu.VMEM((rows, 2 * LANES), BF16),
        pltpu.VMEM((rows, nh * LANES), F32),
        pltpu.VMEM((n_all, LANES, 2 * LANES), F32),
        pltpu.VMEM((n_all, 1, 2 * LANES), F32),
        pltpu.VMEM((n_all, LANES, 2 * LANES), BF16),
    ]
    return pl.pallas_call(
        functools.partial(_scan_body, seq_len=seq_len, seqs=seqs, pair=pair, has_s0=has_s0,
                          emit_state=emit_state),
        grid=(nseq // seqs, units),
        in_specs=in_specs,
        out_specs=out_specs,
        out_shape=out_shape,
        scratch_shapes=scratch,
        compiler_params=_params(2),
        name=f"scan_{'p' if prompt else 's'}_{'gla' if pair else 'hgrn'}",
    )(*args)


def _outproj_body(*refs, n_lhs, tm):
    lhs = refs[:2 * n_lhs]
    ws = refs[2 * n_lhs:3 * n_lhs]
    x_ref, m_ref, g_ref, b_ref, o_ref = refs[3 * n_lhs:]

    def compute(group):
        for r in range(0, tm, PROJ_SUB):
            rows = slice(r, r + PROJ_SUB)
            y = _dot(lhs[group][rows, :], ws[0][...])
            for j in range(1, n_lhs):
                y = y + _dot(lhs[2 * j + group][rows, :], ws[j][...])
            z = ALPHA * x_ref[rows, :] + m_ref[...] * y
            o_ref[rows, :] = _layer_norm(z, g_ref[...], b_ref[...])

    in_prompt = pl.program_id(0) < TP // tm
    pl.when(in_prompt)(lambda: compute(0))
    pl.when(jnp.logical_not(in_prompt))(lambda: compute(1))


def _outproj(lhs, ws, x, mod, ln_g, ln_b, layer, tm=OUT_TM):
    n_lhs = len(lhs)
    in_specs, args = [], []
    for a_p, a_s in lhs:
        in_specs += _group_specs(True, tm, a_p.shape[1])
        args += [a_p, a_s]
    in_specs += [_resident(w.shape) for w in ws]
    in_specs += [pl.BlockSpec((tm, D), lambda i: (i, 0))] + _mod_specs(layer, (5,), tm) + _ln_specs(layer, 1)
    return pl.pallas_call(
        functools.partial(_outproj_body, n_lhs=n_lhs, tm=tm),
        grid=(T // tm,),
        in_specs=in_specs,
        out_specs=pl.BlockSpec((tm, D), lambda i: (i, 0)),
        out_shape=jax.ShapeDtypeStruct((T, D), F32),
        compiler_params=_params(1),
        name="mixer_outproj",
    )(*args, *ws, x, mod, ln_g, ln_b)


def _rope_partner(x):
    lane = lax.broadcasted_iota(jnp.int32, x.shape, 1)
    first_half = (lane % (2 * ROPE_FREQS)) < ROPE_FREQS
    return jnp.where(first_half, pltpu.roll(x, LANES - ROPE_FREQS, axis=1), pltpu.roll(x, ROPE_FREQS, axis=1))


def _qkv_body(x_ref, shift_ref, scale_ref, w_ref, wkvt_ref, cos_ref, sin_ref,
              q_ref, k_ref, v_ref, kt_ref, vt_ref):
    qscale = C_HEAD_DIM ** -0.5
    in_prompt = pl.program_id(0) < TP // PROJ_TM

    def modulated(rows):
        return (x_ref[rows, :] * (1.0 + scale_ref[...]) + shift_ref[...]).astype(BF16)

    @pl.when(in_prompt)
    def _():
        for sq in range(PROJ_TM // SEQ):
            rows = slice(sq * SEQ, (sq + 1) * SEQ)
            h = modulated(rows)
            q_ref[rows, :] = (_dot(h, w_ref[:, :C_Q]) * qscale).astype(BF16)
            v_ref[rows, :] = _dot(h, w_ref[:, C_Q + C_KV:]).astype(BF16)
            kt_ref[sq] = _dot_nt(wkvt_ref[:C_KV, :], h)
            vt_ref[sq] = _dot_nt(wkvt_ref[C_KV:, :], h)

    @pl.when(jnp.logical_not(in_prompt))
    def _():
        for sq in range(PROJ_TM // SEQ):
            rows = slice(sq * SEQ, (sq + 1) * SEQ)
            h = modulated(rows)
            cos, sin = cos_ref[rows, :], sin_ref[rows, :]
            zq = _dot(h, w_ref[:, :C_Q])
            zk = _dot(h, w_ref[:, C_Q:C_Q + C_KV])
            v_ref[rows, :] = _dot(h, w_ref[:, C_Q + C_KV:]).astype(BF16)

            def rope(z, cos=cos, sin=sin):
                return z * cos + _rope_partner(z) * sin

            for j in range(C_Q // LANES):
                cols = slice(j * LANES, (j + 1) * LANES)
                q_ref[rows, cols] = (rope(zq[:, cols]) * qscale).astype(BF16)
            for j in range(C_KV // LANES):
                cols = slice(j * LANES, (j + 1) * LANES)
                k_ref[rows, cols] = rope(zk[:, cols]).astype(BF16)


def _qkv(x, mod, w, wkvt, cos, sin, layer):
    tm = PROJ_TM
    n_p = TP // tm
    lat = lambda i: (jnp.maximum(i - n_p, 0), 0)
    ctx = lambda i: (jnp.minimum(i, n_p - 1), 0, 0)
    return pl.pallas_call(
        _qkv_body,
        grid=(T // tm,),
        in_specs=[pl.BlockSpec((tm, D), lambda i: (i, 0))] + _mod_specs(layer, (3, 4), tm) + [
            _resident((D, C_Q + 2 * C_KV)),
            _resident((2 * C_KV, D)),
            pl.BlockSpec((tm, LANES), lat),
            pl.BlockSpec((tm, LANES), lat),
        ],
        out_specs=[
            pl.BlockSpec((tm, C_Q), lambda i: (i, 0)),
            pl.BlockSpec((tm, C_KV), lat),
            pl.BlockSpec((tm, C_KV), lambda i: (i, 0)),
            pl.BlockSpec((tm // SEQ, C_KV, SEQ), ctx),
            pl.BlockSpec((tm // SEQ, C_KV, SEQ), ctx),
        ],
        out_shape=[jax.ShapeDtypeStruct((T, C_Q), BF16), jax.ShapeDtypeStruct((TS, C_KV), BF16),
                   jax.ShapeDtypeStruct((T, C_KV), BF16),
                   jax.ShapeDtypeStruct((BATCH, C_KV, SEQ), F32), jax.ShapeDtypeStruct((BATCH, C_KV, SEQ), F32)],
        compiler_params=_params(1),
        name="c_qkv",
    )(x, mod, mod, w, wkvt, cos, sin)


def _rope_tables():
    t = np.arange(DEC_SEQ)
    pos = np.stack([t // GRID_W, t % GRID_W], axis=1).astype(np.float32)
    inv = (ROPE_BASE ** (-np.arange(ROPE_FREQS, dtype=np.float32) / ROPE_FREQS)).astype(np.float32)
    d = np.arange(C_HEAD_DIM)
    axis = d // (2 * ROPE_FREQS)
    ang = pos[:, axis] * inv[d % ROPE_FREQS][None, :]
    sign = np.where((d % (2 * ROPE_FREQS)) < ROPE_FREQS, -1.0, 1.0)[None, :]
    cos_h, sin_h = np.cos(ang), np.sin(ang) * sign
    reps = LANES // C_HEAD_DIM
    cos = np.tile(np.tile(cos_h, (1, reps)), (DEC_BATCH, 1)).astype(np.float32)
    sin = np.tile(np.tile(sin_h, (1, reps)), (DEC_BATCH, 1)).astype(np.float32)
    return jnp.asarray(cos), jnp.asarray(sin)


def _dup_head(blk, half):
    lane = lax.broadcasted_iota(jnp.int32, blk.shape, 1)
    keep = (lane >= HALF).astype(jnp.int32) == half
    return jnp.where(keep, blk, pltpu.roll(blk, HALF, axis=1))


def _pair_values(v, half):
    lo = lax.broadcasted_iota(jnp.int32, v.shape, 1) < HALF
    v2 = _dup_head(v.astype(F32), half)
    top = jnp.concatenate([jnp.where(lo, v2, 0.0), jnp.where(lo, 1.0, 0.0)], axis=1)
    bot = jnp.concatenate([jnp.where(lo, 0.0, v2), jnp.where(lo, 0.0, 1.0)], axis=1)
    return top.astype(BF16), bot.astype(BF16)


def _softmax_weights(scores, sink):
    mx = sink
    for s in scores:
        mx = jnp.maximum(mx, jnp.max(s, axis=-1, keepdims=True))
    return [jnp.exp(s - mx).astype(BF16) for s in scores], jnp.exp(sink - mx)


def _pair_sinks(sink_ref, jb):
    return (sink_ref[:, jb * LANES:jb * LANES + 1], sink_ref[:, jb * LANES + HALF:jb * LANES + HALF + 1])


KV_PER_BLOCK = LANES // C_HEAD_DIM
TILE_KV = C_KV_HEADS
TILE_KVW = TILE_KV * C_HEAD_DIM
CTX_TILES = BATCH * C_KV_HEADS // TILE_KV
CTX_BLOCKS = TILE_KV * C_GROUPS // 2
PIPE_LAG = 2


def _kv_block(ref, kv):
    lb = kv // KV_PER_BLOCK
    return ref[:, lb * LANES:(lb + 1) * LANES], kv % KV_PER_BLOCK


def _run_pipeline(g, n_tiles, stages):
    for cur in (0, 1):
        steady = jnp.logical_and(g % 2 == cur, jnp.logical_and(g >= PIPE_LAG, g < n_tiles))
        pl.when(steady)(functools.partial(stages, cur, 1 - cur, True, True, True))
    pl.when(g == 0)(functools.partial(stages, 0, 1, True, False, False))
    pl.when(g == 1)(functools.partial(stages, 1, 0, True, True, False))
    pl.when(g == n_tiles)(functools.partial(stages, n_tiles % 2, 1 - n_tiles % 2, False, True, True))
    pl.when(g == n_tiles + 1)(functools.partial(stages, 1 - n_tiles % 2, n_tiles % 2, False, False, True))


def _ctx_tile(g, lag):
    t = jnp.clip(g - lag, 0, CTX_TILES - 1)
    n = C_KV_HEADS // TILE_KV
    return t // n, t % n


def _ctx_attn_body(q_ref, kt_ref, v_ref, sink_ref, o_ref, s_s, e_s, st_s):
    g = pl.program_id(0)
    lo = lax.broadcasted_iota(jnp.int32, (1, LANES), 1) < HALF
    pairs_per_kv = C_GROUPS // 2

    def stages(cur, prev, scores, weights, values):
        for kv in range(TILE_KV if values else 0):
            v_top, v_bot = _pair_values(*_kv_block(v_ref, kv))
            for jb in range(kv * pairs_per_kv, (kv + 1) * pairs_per_kv):
                od = _dot(e_s[cur, 2 * jb], v_top) + _dot(e_s[cur, 2 * jb + 1], v_bot)
                o_ref[:, jb * LANES:(jb + 1) * LANES] = (od[:, :LANES] / (od[:, LANES:] + st_s[cur, jb])).astype(BF16)
        for jb in range(CTX_BLOCKS if weights else 0):
            terms = []
            for hh, sink in enumerate(_pair_sinks(sink_ref, jb)):
                (e,), term = _softmax_weights([s_s[prev, 2 * jb + hh]], sink)
                e_s[prev, 2 * jb + hh] = e
                terms.append(term)
            st_s[prev, jb] = jnp.where(lo, terms[0], terms[1])
        for kv in range(TILE_KV if scores else 0):
            kt = kt_ref[kv * C_HEAD_DIM:(kv + 1) * C_HEAD_DIM, :]
            zero = jnp.zeros_like(kt)
            kt_a = jnp.concatenate([kt, zero], axis=0).astype(BF16)
            kt_b = jnp.concatenate([zero, kt], axis=0).astype(BF16)
            for jb in range(kv * pairs_per_kv, (kv + 1) * pairs_per_kv):
                q = q_ref[:, jb * LANES:(jb + 1) * LANES]
                s_s[cur, 2 * jb] = _dot(q, kt_a)
                s_s[cur, 2 * jb + 1] = _dot(q, kt_b)

    _run_pipeline(g, CTX_TILES, stages)


def _ctx_attn(q, kt, v, sink_cols):
    gw = CTX_BLOCKS * LANES

    def at(lag, fn):
        return lambda g: fn(*_ctx_tile(g, lag))

    return pl.pallas_call(
        _ctx_attn_body,
        grid=(CTX_TILES + PIPE_LAG,),
        in_specs=[
            pl.BlockSpec((SEQ, gw), at(0, lambda b, p: (b, p))),
            pl.BlockSpec((None, TILE_KVW, SEQ), at(0, lambda b, p: (b, p, 0))),
            pl.BlockSpec((SEQ, TILE_KVW), at(2, lambda b, p: (b, p))),
            pl.BlockSpec((1, gw), at(1, lambda b, p: (0, p))),
        ],
        out_specs=pl.BlockSpec((SEQ, gw), at(2, lambda b, p: (b, p))),
        out_shape=jax.ShapeDtypeStruct((TP, C_Q), BF16),
        scratch_shapes=[
            pltpu.VMEM((2, 2 * CTX_BLOCKS, SEQ, SEQ), F32),
            pltpu.VMEM((2, 2 * CTX_BLOCKS, SEQ, SEQ), BF16),
            pltpu.VMEM((2, CTX_BLOCKS, SEQ, LANES), F32),
        ],
        compiler_params=_params(1),
        name="c_attn_ctx",
    )(q, kt, v, sink_cols)


LAT_TQ = 128
LAT_WIN = LAT_TQ + 2 * WINDOW
LAT_NQ = DEC_SEQ // LAT_TQ
LAT_KVB = C_KV_HEADS // TILE_KV
LAT_TILES = DEC_BATCH * LAT_KVB * LAT_NQ


def _lat_tile(g, lag):
    t = jnp.clip(g - lag, 0, LAT_TILES - 1)
    bp = t // LAT_NQ
    return bp // LAT_KVB, bp % LAT_KVB, t % LAT_NQ


def _lat_window(j):
    return pl.multiple_of(jnp.clip(j * LAT_TQ - WINDOW, 0, DEC_SEQ - LAT_WIN), LANES)


def _lat_attn_body(q_ref, k_ref, v_ref, ck_ref, cv_ref, sink_ref, o_ref,
                   ka_s, kb_s, cka_s, ckb_s, va_s, vb_s, cva_s, cvb_s, sl_s, sc_s, el_s, ec_s, st_s):
    g = pl.program_id(0)
    lo = lax.broadcasted_iota(jnp.int32, (1, LANES), 1) < HALF
    pairs_per_kv = C_GROUPS // 2
    b_a, p_a, j_a = _lat_tile(g, 0)
    b_c, p_c, j_c = _lat_tile(g, 2)
    v_slot_a = (b_a * LAT_KVB + p_a) % 2
    v_slot_c = (b_c * LAT_KVB + p_c) % 2

    @pl.when(jnp.logical_and(j_a == 0, g < LAT_TILES))
    def _():
        for kv in range(TILE_KV):
            for src, a_s, b_s in ((k_ref, ka_s, kb_s), (ck_ref, cka_s, ckb_s)):
                blk, kh = _kv_block(src, kv)
                lo2 = lax.broadcasted_iota(jnp.int32, blk.shape, 1) < HALF
                k2 = _dup_head(blk.astype(F32), kh)
                a_s[kv] = jnp.where(lo2, k2, 0.0).astype(BF16)
                b_s[kv] = jnp.where(lo2, 0.0, k2).astype(BF16)
            va_s[v_slot_a, kv], vb_s[v_slot_a, kv] = _pair_values(*_kv_block(v_ref, kv))
            cva_s[v_slot_a, kv], cvb_s[v_slot_a, kv] = _pair_values(*_kv_block(cv_ref, kv))

    def stages(cur, prev, scores, weights, values):
        win_c = pl.ds(_lat_window(j_c), LAT_WIN)
        for kv in range(TILE_KV if values else 0):
            v_loc = (va_s[v_slot_c, kv, win_c, :], vb_s[v_slot_c, kv, win_c, :])
            v_ctx = (cva_s[v_slot_c, kv], cvb_s[v_slot_c, kv])
            for jb in range(kv * pairs_per_kv, (kv + 1) * pairs_per_kv):
                od = None
                for hh in range(2):
                    part = _dot(el_s[cur, 2 * jb + hh], v_loc[hh]) + _dot(ec_s[cur, 2 * jb + hh], v_ctx[hh])
                    od = part if od is None else od + part
                o_ref[:, jb * LANES:(jb + 1) * LANES] = (od[:, :LANES] / (od[:, LANES:] + st_s[cur, jb])).astype(BF16)
        for jb in range(CTX_BLOCKS if weights else 0):
            terms = []
            for hh, sink in enumerate(_pair_sinks(sink_ref, jb)):
                (e_loc, e_ctx), term = _softmax_weights([sl_s[prev, 2 * jb + hh], sc_s[prev, 2 * jb + hh]], sink)
                el_s[prev, 2 * jb + hh] = e_loc
                ec_s[prev, 2 * jb + hh] = e_ctx
                terms.append(term)
            st_s[prev, jb] = jnp.where(lo, terms[0], terms[1])
        start = _lat_window(j_a)
        win = pl.ds(start, LAT_WIN)
        qpos = j_a * LAT_TQ + lax.broadcasted_iota(jnp.int32, (LAT_TQ, LAT_WIN), 0)
        kpos = start + lax.broadcasted_iota(jnp.int32, (LAT_TQ, LAT_WIN), 1)
        band = jnp.abs(qpos - kpos) <= WINDOW
        for kv in range(TILE_KV if scores else 0):
            for jb in range(kv * pairs_per_kv, (kv + 1) * pairs_per_kv):
                q = q_ref[:, jb * LANES:(jb + 1) * LANES]
                for hh, (kl, kc) in enumerate(((ka_s, cka_s), (kb_s, ckb_s))):
                    sl_s[cur, 2 * jb + hh] = jnp.where(band, _dot_nt(q, kl[kv, win, :]), -jnp.inf)
                    sc_s[cur, 2 * jb + hh] = _dot_nt(q, kc[kv])

    _run_pipeline(g, LAT_TILES, stages)


def _lat_attn(q, k, v, ck, cv, sink_cols):
    gw = CTX_BLOCKS * LANES
    q_off = TP // LAT_TQ
    kv_off = TP // DEC_SEQ
    kvb, n_heads = TILE_KV, 2 * CTX_BLOCKS

    def at(lag, fn):
        return lambda g: fn(*_lat_tile(g, lag))

    return pl.pallas_call(
        _lat_attn_body,
        grid=(LAT_TILES + PIPE_LAG,),
        in_specs=[
            pl.BlockSpec((LAT_TQ, gw), at(0, lambda b, p, j: (q_off + b * LAT_NQ + j, p))),
            pl.BlockSpec((DEC_SEQ, TILE_KVW), at(0, lambda b, p, j: (b, p))),
            pl.BlockSpec((DEC_SEQ, TILE_KVW), at(0, lambda b, p, j: (kv_off + b, p))),
            pl.BlockSpec((PAST_LEN, TILE_KVW), at(0, lambda b, p, j: (b, p))),
            pl.BlockSpec((PAST_LEN, TILE_KVW), at(0, lambda b, p, j: (b, p))),
            pl.BlockSpec((1, gw), at(1, lambda b, p, j: (0, p))),
        ],
        out_specs=pl.BlockSpec((LAT_TQ, gw), at(2, lambda b, p, j: (b * LAT_NQ + j, p))),
        out_shape=jax.ShapeDtypeStruct((TS, C_Q), BF16),
        scratch_shapes=[
            pltpu.VMEM((kvb, DEC_SEQ, LANES), BF16), pltpu.VMEM((kvb, DEC_SEQ, LANES), BF16),
            pltpu.VMEM((kvb, PAST_LEN, LANES), BF16), pltpu.VMEM((kvb, PAST_LEN, LANES), BF16),
            pltpu.VMEM((2, kvb, DEC_SEQ, 2 * LANES), BF16), pltpu.VMEM((2, kvb, DEC_SEQ, 2 * LANES), BF16),
            pltpu.VMEM((2, kvb, PAST_LEN, 2 * LANES), BF16), pltpu.VMEM((2, kvb, PAST_LEN, 2 * LANES), BF16),
            pltpu.VMEM((2, n_heads, LAT_TQ, LAT_WIN), F32), pltpu.VMEM((2, n_heads, LAT_TQ, PAST_LEN), F32),
            pltpu.VMEM((2, n_heads, LAT_TQ, LAT_WIN), BF16), pltpu.VMEM((2, n_heads, LAT_TQ, PAST_LEN), BF16),
            pltpu.VMEM((2, CTX_BLOCKS, LAT_TQ, LANES), F32),
        ],
        compiler_params=_params(1),
        name="c_attn_latent",
    )(q, k, v, ck, cv, sink_cols)


def kernel(x_prompt, x_sample, state_hgrn, state_gla, cache_k, cache_v, c, c_ctx, w_mod, b_mod, ln_g, ln_b,
           ffn_w1, ffn_w3, ffn_w2, w_in_ab, hgrn_lb, gla_gate_up, gla_gate_b, norm_a, norm_b, w_out_ab,
           w_qkv_c, sink_c, w_out_c):
    cs = jnp.zeros((8, D), F32).at[0].set(c_ctx).at[1:1 + DEC_BATCH].set(c)
    mod = _mod_vectors(cs, w_mod, b_mod).reshape(DEPTH, 8, 1, N_MOD * D)
    ffn_ws = (ffn_w1, ffn_w3, ffn_w2)
    ln_g, ln_b = ln_g.reshape(DEPTH, 3, 1, D), ln_b.reshape(DEPTH, 3, 1, D)

    def ffn(xs, ws, layer, sub, split_out=False, cast_next=None):
        casts = None if cast_next is None else (ffn_ws, cast_next)
        return _ffn_sublayer(xs, mod, *ws, ln_g, ln_b, layer, sub, split_out=split_out, casts=casts)

    x, *ws_01 = ffn([x_prompt.reshape(TP, D), x_sample.reshape(TS, D)], [w[0, 0].astype(BF16) for w in ffn_ws],
                    0, 0, cast_next=(0, 1))
    w_in = w_in_ab[0]
    o_aq, o_ai, o_ff, o_fb, o_ag = 0, A_W, 2 * A_W, 3 * A_W, 4 * A_W
    o_bq = 5 * A_W
    o_bk, o_bv = o_bq + B_QK, o_bq + 2 * B_QK
    o_bg = o_bv + B_V
    o_z = o_bg + B_V
    order = [(o_aq, A_W), (o_ff, A_W), (o_fb, A_W), (o_ag, A_W), (o_bq, B_QK), (o_bk, B_QK), (o_bg, B_V),
             (o_ai, A_W), (o_bv, B_V)]
    wmain = jnp.concatenate([w_in[:, o:o + w] for o, w in order], axis=1).astype(BF16)
    wz = jnp.pad(w_in[:, o_z:o_z + 2 * GATE_RANK], ((0, 0), (0, LANES - 2 * GATE_RANK))).astype(BF16)
    gup = jnp.zeros((LANES, 2 * B_QK), F32)
    gup = gup.at[:GATE_RANK, :B_QK].set(gla_gate_up[0, 0]).at[GATE_RANK:2 * GATE_RANK, B_QK:].set(gla_gate_up[0, 1])
    gb = gla_gate_b[0].reshape(1, 2 * B_QK)
    pf, pb = _inproj(x, mod, wmain, wz, gup.astype(BF16), gb, hgrn_lb, 0, 0)

    s0_a = state_hgrn[:, 0]
    s0_b = state_gla[:, 0].reshape(DEC_BATCH, 2, B_HEADS // 2, LANES, B_DV)
    oa_p, st_a = _scan(pf, pb, norm_a[0], None, prompt=True, pair=False)
    ob_p, st_b = _scan(pf, pb, norm_b[0], None, prompt=True, pair=True)
    (oa_s,) = _scan(pf, pb, norm_a[0], s0_a, prompt=False, pair=False)
    (ob_s,) = _scan(pf, pb, norm_b[0], s0_b, prompt=False, pair=True)
    w_out = w_out_ab[0].astype(BF16)
    x = _outproj([(oa_p, oa_s), (ob_p, ob_s)], [w_out[:A_W], w_out[A_W:]], x, mod, ln_g, ln_b, 0)
    x, *ws_10 = ffn([x], ws_01, 0, 1, cast_next=(1, 0))
    new_hgrn = st_a.reshape(BATCH, 1, 2, A_HEADS, A_DK, A_DV)
    new_gla = st_b.reshape(BATCH, 1, 2, B_HEADS, B_DK, B_DV)

    x, *ws_11 = ffn([x], ws_10, 1, 0, cast_next=(1, 1))
    cos, sin = _rope_tables()
    w_qkv = w_qkv_c[0].astype(BF16)
    q, k, v, kt, vt = _qkv(x, mod, w_qkv, w_qkv[:, C_Q:].T, cos, sin, 1)
    sink_cols = jnp.repeat(sink_c[0], C_HEAD_DIM).reshape(1, C_Q)
    o_p = _ctx_attn(q, kt, v, sink_cols)
    ck = cache_k[:, 0].reshape(DEC_BATCH * PAST_LEN, C_KV)
    cv = cache_v[:, 0].reshape(DEC_BATCH * PAST_LEN, C_KV)
    o_s = _lat_attn(q, k, v, ck, cv, sink_cols)
    x = _outproj([(o_p, o_s)], [w_out_c[0].astype(BF16)], x, mod, ln_g, ln_b, 1)
    y_p, y_s = ffn([x], ws_11, 1, 1, split_out=True)

    def cache_layout(zt):
        return zt.reshape(BATCH, 1, C_KV_HEADS, C_HEAD_DIM, SEQ).transpose(0, 1, 4, 2, 3)

    new_k, new_v = cache_layout(kt), cache_layout(vt)

    return (y_p.reshape(BATCH, SEQ, D), y_s.reshape(DEC_BATCH, DEC_SEQ, D), new_hgrn, new_gla, new_k, new_v)
```

```python
import functools
import math

import jax
import jax.numpy as jnp
import numpy as np
from jax import lax
from jax.experimental import pallas as pl
from jax.experimental.pallas import tpu as pltpu

D = 1024
BATCH, SEQ = 16, 256
DEC_BATCH, DEC_SEQ = 2, 2048
PAST_LEN = 512
GRID_W = 64
D_FF = 2816
N_MOD = 9
A_HEADS, A_DK, A_DV = 4, 128, 128
A_W = A_HEADS * A_DK
B_HEADS, B_DK, B_DV = 4, 64, 128
B_QK = B_HEADS * B_DK
B_V = B_HEADS * B_DV
GATE_RANK = 16
GLA_TAU = 16.0
CHUNK = 128
C_HEADS, C_KV_HEADS, C_HEAD_DIM = 16, 4, 64
C_GROUPS = C_HEADS // C_KV_HEADS
C_Q = C_HEADS * C_HEAD_DIM
C_KV = C_KV_HEADS * C_HEAD_DIM
WINDOW = 128
ROPE_FREQS = C_HEAD_DIM // 4
ROPE_BASE = 10000.0
DEPTH = 2
ALPHA = (2.0 * DEPTH) ** 0.25
LN_EPS = 1e-5
RMS_EPS = 1e-6

TP = BATCH * SEQ
TS = DEC_BATCH * DEC_SEQ
T = TP + TS
N_SEG = 1 + DEC_BATCH

LANES = 128
HALF = LANES // 2
FFN_TM = 1024
FFN_TM_SPLIT = 512
FFN_SUB = 256
PROJ_TM = 4 * SEQ
INPROJ_TM = 512
PROJ_SUB = 256
OUT_TM = 1024
VMEM_LIMIT = 60 * 1024 * 1024

F32 = jnp.float32
BF16 = jnp.bfloat16


def _dot(a, b):
    return jnp.dot(a, b, preferred_element_type=F32)


def _dot_nt(a, b):
    return lax.dot_general(a, b, (((1,), (1,)), ((), ())), preferred_element_type=F32)


def _dot_tn(a, b):
    return lax.dot_general(a, b, (((0,), (0,)), ((), ())), preferred_element_type=F32)


def _silu(x):
    return x * jax.nn.sigmoid(x)


def _layer_norm(z, g, b):
    mu = jnp.mean(z, axis=-1, keepdims=True)
    zc = z - mu
    var = jnp.mean(zc * zc, axis=-1, keepdims=True)
    return zc * lax.rsqrt(var + LN_EPS) * g + b


def _seg_of_tile(i, tm):
    n_p = TP // tm
    n_s = DEC_SEQ // tm
    return jnp.where(i < n_p, 0, 1 + lax.div(jnp.maximum(i - n_p, 0), n_s))


def _params(n_axes):
    return pltpu.CompilerParams(dimension_semantics=("arbitrary",) * n_axes, vmem_limit_bytes=VMEM_LIMIT)


def _resident(shape):
    nd = len(shape)
    return pl.BlockSpec(shape, lambda *_: (0,) * nd, pipeline_mode=pl.Buffered(1))


def _resident_slice(shape, lead):
    block = (None,) * len(lead) + tuple(shape)
    return pl.BlockSpec(block, lambda *_: tuple(lead) + (0,) * len(shape), pipeline_mode=pl.Buffered(1))


def _mod_specs(layer, cols, tm):
    return [pl.BlockSpec((None, None, 1, D), functools.partial(
        lambda i, c: (layer, _seg_of_tile(i, tm), 0, c), c=c)) for c in cols]


def _ln_specs(layer, idx):
    return [_resident_slice((1, D), (layer, idx))] * 2


BF16_SUBLANES = 16


def _cast_plan(ws, lead, n_steps, step_of):
    in_specs, out_specs, out_shapes = [], [], []
    for w in ws:
        rows, cols = w.shape[len(lead):]
        blk = next(b for b in range(BF16_SUBLANES, rows + 1, BF16_SUBLANES)
                   if rows % b == 0 and rows // b <= n_steps)
        last = rows // blk - 1
        in_specs.append(pl.BlockSpec((None,) * len(lead) + (blk, cols), functools.partial(
            lambda *g, last: tuple(lead) + (jnp.minimum(step_of(*g), last), 0), last=last)))
        out_specs.append(pl.BlockSpec((blk, cols), functools.partial(
            lambda *g, last: (jnp.minimum(step_of(*g), last), 0), last=last)))
        out_shapes.append(jax.ShapeDtypeStruct((rows, cols), BF16))
    return in_specs, out_specs, out_shapes


def _hosting_casts(body, n_in, n_out, n_cast):
    def hosted(*refs, **kw):
        ins, refs = refs[:n_in], refs[n_in:]
        cast_in, refs = refs[:n_cast], refs[n_cast:]
        outs, refs = refs[:n_out], refs[n_out:]
        cast_out, scratch = refs[:n_cast], refs[n_cast:]
        for src, dst in zip(cast_in, cast_out):
            dst[...] = src[...].astype(BF16)
        body(*ins, *outs, *scratch, **kw)
    return hosted


def _mod_body(c_ref, w_ref, b_ref, o_ref):
    c = c_ref[...]
    s = _silu(c).astype(BF16)
    o_ref[0] = _dot(s, w_ref[0].astype(BF16)) + b_ref[0]


def _mod_vectors(cs, w_mod, b_mod):
    tn = 1536
    n = N_MOD * D
    return pl.pallas_call(
        _mod_body,
        grid=(DEPTH, n // tn),
        in_specs=[
            pl.BlockSpec((8, D), lambda l, j: (0, 0)),
            pl.BlockSpec((1, D, tn), lambda l, j: (l, 0, j)),
            pl.BlockSpec((1, 1, tn), lambda l, j: (l, 0, j)),
        ],
        out_specs=pl.BlockSpec((1, 8, tn), lambda l, j: (l, 0, j)),
        out_shape=jax.ShapeDtypeStruct((DEPTH, 8, n), F32),
        compiler_params=_params(2),
        name="mod_vectors",
    )(cs, w_mod, b_mod.reshape(DEPTH, 1, n))


def _ffn_body(*refs, n_x, n_o, tm):
    x_refs = refs[:n_x]
    shift_ref, scale_ref, gate_ref, w1_ref, w3_ref, w2_ref, g_ref, b_ref = refs[n_x:n_x + 8]
    o_refs = refs[n_x + 8:]

    def compute(x_ref, o_ref):
        shift, scale, gate = shift_ref[...], scale_ref[...], gate_ref[...]
        for r in range(0, tm, FFN_SUB):
            rows = slice(r, r + FFN_SUB)
            x = x_ref[rows, :]
            h = (x * (1.0 + scale) + shift).astype(BF16)
            a = _dot(h, w1_ref[...])
            b = _dot(h, w3_ref[...])
            g = (_silu(a) * b).astype(BF16)
            y = _dot(g, w2_ref[...])
            z = ALPHA * x + (0.5 * gate) * y
            o_ref[rows, :] = _layer_norm(z, g_ref[...], b_ref[...])

    if n_x == 1 and n_o == 1:
        compute(x_refs[0], o_refs[0])
    else:
        in_prompt = pl.program_id(0) < TP // tm
        pl.when(in_prompt)(lambda: compute(x_refs[0], o_refs[0]))
        pl.when(jnp.logical_not(in_prompt))(lambda: compute(x_refs[-1], o_refs[-1]))


def _group_specs(split, tm, width=D):
    if not split:
        return [pl.BlockSpec((tm, width), lambda i: (i, 0))]
    n_p = TP // tm
    return [pl.BlockSpec((tm, width), lambda i: (jnp.minimum(i, n_p - 1), 0)),
            pl.BlockSpec((tm, width), lambda i: (jnp.maximum(i - n_p, 0), 0))]


def _ffn_sublayer(xs, mod, w1, w3, w2, ln_g, ln_b, layer, sub, split_out=False, casts=None):
    n_x, n_o = len(xs), 2 if split_out else 1
    tm = FFN_TM if n_x == n_o == 1 else FFN_TM_SPLIT
    out_shape = ([jax.ShapeDtypeStruct((TP, D), F32), jax.ShapeDtypeStruct((TS, D), F32)] if split_out
                 else [jax.ShapeDtypeStruct((T, D), F32)])
    mod_lo = 6 * sub
    body = functools.partial(_ffn_body, n_x=n_x, n_o=n_o, tm=tm)
    in_specs = _group_specs(n_x == 2, tm) + _mod_specs(layer, (mod_lo, mod_lo + 1, mod_lo + 2), tm) + [
        _resident((D, D_FF)),
        _resident((D, D_FF)),
        _resident((D_FF, D)),
    ] + _ln_specs(layer, 2 * sub)
    out_specs = _group_specs(split_out, tm)
    args = [*xs, mod, mod, mod, w1, w3, w2, ln_g, ln_b]
    if casts is not None:
        ws, lead = casts
        c_in, c_out, c_shapes = _cast_plan(ws, lead, T // tm, lambda i: i)
        body = _hosting_casts(body, len(in_specs), len(out_specs), len(ws))
        in_specs, out_specs, out_shape = in_specs + c_in, out_specs + c_out, out_shape + c_shapes
        args = args + list(ws)
    return pl.pallas_call(
        body,
        grid=(T // tm,),
        in_specs=in_specs,
        out_specs=out_specs,
        out_shape=out_shape,
        compiler_params=_params(1),
        name="ffn_sublayer",
    )(*args)


PF_AQ, PF_FF, PF_FB, PF_AG = 0, 512, 1024, 1536
PF_BQ, PF_BK, PF_BG, PF_LAF, PF_LAB = 2048, 2304, 2560, 3072, 3328
PF_W = 3584
PB_AV, PB_BV = 0, 512
PB_W = 1024
WM_AQ, WM_FF, WM_FB, WM_AG, WM_BQ, WM_BK, WM_BG, WM_AI, WM_BV = 0, 512, 1024, 1536, 2048, 2304, 2560, 3072, 3584
WM_W = 4096


def _log_sigmoid(x):
    return jnp.minimum(x, 0.0) - jnp.log(1.0 + jnp.exp(-jnp.abs(x)))


def _inproj_body(x_ref, shift_ref, scale_ref, w_ref, wz_ref, gu_ref, gb_ref, lb_ref, pf_ref, pb_ref, *, layer_e):
    def lower_bound(d):
        l = lb_ref[d]
        e = jnp.exp(l - jnp.max(l, axis=0, keepdims=True))
        sm = e / jnp.sum(e, axis=0, keepdims=True)
        return jnp.sum(sm[:layer_e + 1], axis=0, keepdims=True)

    lbs = [lower_bound(0), lower_bound(1)]
    for r in range(0, x_ref.shape[0], PROJ_SUB):
        rows = slice(r, r + PROJ_SUB)
        h = (x_ref[rows, :] * (1.0 + scale_ref[...]) + shift_ref[...]).astype(BF16)

        def proj(off, width):
            return _dot(h, w_ref[:, off:off + width])

        pf_ref[rows, PF_AQ:PF_AQ + A_W] = proj(WM_AQ, A_W)
        for lb, wm, pf in ((lbs[0], WM_FF, PF_FF), (lbs[1], WM_FB, PF_FB)):
            pf_ref[rows, pf:pf + A_W] = lb + (1.0 - lb) * jax.nn.sigmoid(proj(wm, A_W))
        pf_ref[rows, PF_AG:PF_AG + A_W] = _silu(proj(WM_AG, A_W))
        pf_ref[rows, PF_BQ:PF_BQ + B_QK] = proj(WM_BQ, B_QK) * (B_DK ** -0.5)
        pf_ref[rows, PF_BK:PF_BK + B_QK] = proj(WM_BK, B_QK)
        pf_ref[rows, PF_BG:PF_BG + B_V] = _silu(proj(WM_BG, B_V))
        pb_ref[rows, PB_AV:PB_AV + A_W] = _silu(proj(WM_AI, A_W)).astype(BF16)
        pb_ref[rows, PB_BV:PB_BV + B_V] = proj(WM_BV, B_V).astype(BF16)
        z = _dot(h, wz_ref[...]).astype(BF16)
        pre = _dot(z, gu_ref[...]) + gb_ref[...]
        pf_ref[rows, PF_LAF:PF_LAF + 2 * B_QK] = _log_sigmoid(pre) * (1.0 / GLA_TAU)


def _inproj(x, mod, wmain, wz, gup, gb, hgrn_lb, layer, layer_e, tm=INPROJ_TM):
    n_l = hgrn_lb.shape[1]
    return pl.pallas_call(
        functools.partial(_inproj_body, layer_e=layer_e),
        grid=(T // tm,),
        in_specs=[pl.BlockSpec((tm, D), lambda i: (i, 0))] + _mod_specs(layer, (3, 4), tm) + [
            _resident((D, WM_W)),
            _resident((D, LANES)),
            _resident((LANES, 2 * B_QK)),
            _resident((1, 2 * B_QK)),
            _resident((2, n_l, A_W)),
        ],
        out_specs=[
            pl.BlockSpec((tm, PF_W), lambda i: (i, 0)),
            pl.BlockSpec((tm, PB_W), lambda i: (i, 0)),
        ],
        out_shape=[jax.ShapeDtypeStruct((T, PF_W), F32), jax.ShapeDtypeStruct((T, PB_W), BF16)],
        compiler_params=_params(1),
        name="ab_inproj",
    )(x, mod, mod, wmain, wz, gup, gb, hgrn_lb)


SCAN_PROMPT_SEQS = 4
SCAN_UNROLL = 16


def _prefix_rows(x):
    row = lax.broadcasted_iota(jnp.int32, x.shape, 0)
    s = 1
    while s < x.shape[0]:
        x = x + jnp.where(row >= s, pltpu.roll(x, s, axis=0), 0.0)
        s *= 2
    return x


def _scan_body(*refs, seq_len, seqs, pair, has_s0, emit_state):
    n = seq_len // CHUNK
    n_all = seqs * n
    nh = 2 if pair else 1
    it = iter(refs)
    q_ref = next(it)
    if pair:
        k_ref, laf_ref, lab_ref = next(it), next(it), next(it)
    else:
        ff_ref, fb_ref = next(it), next(it)
    g_ref, v_ref, nw_ref = next(it), next(it), next(it)
    s0_ref = next(it) if has_s0 else None
    o_ref = next(it)
    st_ref = next(it) if emit_state else None
    qd_s, oi_s, kv_s, dec_s, sb_s = it

    row = lax.broadcasted_iota(jnp.int32, (CHUNK, CHUNK), 0)
    col = lax.broadcasted_iota(jnp.int32, (CHUNK, CHUNK), 1)
    tril = row >= col
    triu = row <= col
    lane = lax.broadcasted_iota(jnp.int32, (1, LANES), 1)
    lane2 = lax.broadcasted_iota(jnp.int32, (1, 2 * LANES), 1)
    if pair:
        masks = [lane < HALF, lane >= HALF]
        masks2 = [(lane2 % LANES) < HALF, (lane2 % LANES) >= HALF]
    else:
        masks, masks2 = [None], [None]

    def pick(mask, x):
        return x if mask is None else jnp.where(mask, x, jnp.zeros_like(x))

    def rows_of(c):
        return pl.ds(pl.multiple_of(c * CHUNK, CHUNK), CHUNK)

    def loop(body):
        if n_all <= SCAN_UNROLL:
            for c in range(n_all):
                body(c)
        else:
            def fbody(i, carry):
                for u in range(SCAN_UNROLL):
                    body(i * SCAN_UNROLL + u)
                return carry
            lax.fori_loop(0, n_all // SCAN_UNROLL, fbody, 0)

    def phase1(c):
        rows = rows_of(c)
        q = q_ref[rows, :]
        if pair:
            k_f = k_b = k_ref[rows, :]
            la_f, la_b = laf_ref[rows, :], lab_ref[rows, :]
        else:
            f_f, f_b = ff_ref[rows, :], fb_ref[rows, :]
            k_f, k_b = 1.0 - f_f, 1.0 - f_b
            la_f, la_b = jnp.log(f_f), jnp.log(f_b)
        cs = _prefix_rows(jnp.concatenate([la_f, la_b], axis=1))
        cf, cbi = cs[:, :LANES], cs[:, LANES:]
        tot_f, tot_b = cf[CHUNK - 1:CHUNK, :], cbi[CHUNK - 1:CHUNK, :]
        rb = tot_b - cbi + la_b
        ref_f, ref_b = cf[CHUNK // 2 - 1:CHUNK // 2, :], rb[CHUNK // 2:CHUNK // 2 + 1, :]
        qtf = q * jnp.exp(cf - ref_f)
        qtb = q * jnp.exp(rb - ref_b)
        ktf = k_f * jnp.exp(ref_f - cf)
        ktb = k_b * jnp.exp(ref_b - rb)
        qd = jnp.concatenate([qtf * jnp.exp(ref_f), qtb * jnp.exp(ref_b)], axis=1).astype(BF16)
        ku = jnp.concatenate([ktf * jnp.exp(tot_f - ref_f), ktb * jnp.exp(tot_b - ref_b)], axis=1).astype(BF16)
        qd_s[rows, :] = qd
        qt = jnp.concatenate([qtf, qtb], axis=0).astype(BF16)
        kt = jnp.concatenate([ktf, ktb], axis=0).astype(BF16)
        kv = None
        for hh in range(nh):
            v = v_ref[rows, hh * LANES:(hh + 1) * LANES]
            sc = _dot_nt(pick(masks[hh], qt), kt)
            att = jnp.where(tril, sc[:CHUNK, :CHUNK], 0.0) + jnp.where(triu, sc[CHUNK:, CHUNK:], 0.0)
            oi_s[rows, hh * LANES:(hh + 1) * LANES] = _dot(att.astype(BF16), v)
            kv_h = _dot_tn(v, ku)
            kv = kv_h if kv is None else jnp.where(masks2[0], kv, kv_h)
        kv_s[c] = kv
        dec_s[c] = jnp.exp(jnp.concatenate([tot_f, tot_b], axis=1))

    loop(phase1)

    def recurrence(sq, d, reverse):
        cols = slice(d * LANES, (d + 1) * LANES)
        c0 = sq * n
        st0 = s0_ref[sq, d, 0].T if has_s0 else jnp.zeros((LANES, LANES), F32)

        def step(c, st):
            sb_s[c, :, cols] = st.astype(BF16)
            return st * dec_s[c, :, cols] + kv_s[c, :, cols]

        if n <= 8:
            st = st0
            for c in (range(n - 1, -1, -1) if reverse else range(n)):
                st = step(c0 + c, st)
        else:
            st = lax.fori_loop(0, n, lambda i, st: step(c0 + (n - 1 - i if reverse else i), st), st0)
        if emit_state:
            st_ref[sq, d, 0] = st.T

    for sq in range(seqs):
        recurrence(sq, 0, False)
        recurrence(sq, 1, True)

    nw = nw_ref[...]

    def phase2(c):
        rows = rows_of(c)
        qcat = qd_s[rows, :]
        scat = sb_s[c]
        for hh in range(nh):
            cols = slice(hh * LANES, (hh + 1) * LANES)
            o = oi_s[rows, cols] + _dot_nt(pick(masks2[hh], qcat), scat)
            o = o * lax.rsqrt(jnp.mean(o * o, axis=-1, keepdims=True) + RMS_EPS) * nw
            o_ref[rows, cols] = (o * g_ref[rows, cols]).astype(BF16)

    loop(phase2)


def _scan(pf, pb, norm_w, s0, *, prompt, pair):
    seq_len = SEQ if prompt else DEC_SEQ
    nseq = BATCH if prompt else DEC_BATCH
    seqs = SCAN_PROMPT_SEQS if prompt else 1
    rows = seqs * seq_len
    row_off = 0 if prompt else TP // rows
    units = B_HEADS // 2 if pair else A_HEADS
    nh = 2 if pair else 1
    n_all = rows // CHUNK
    has_s0 = s0 is not None
    emit_state = prompt

    def colspec(off, width=LANES):
        base = off // width
        return pl.BlockSpec((rows, width), lambda s, u: (s + row_off, base + u))

    if pair:
        in_specs = [colspec(PF_BQ), colspec(PF_BK), colspec(PF_LAF), colspec(PF_LAB),
                    colspec(PF_BG, 2 * LANES), colspec(PB_BV, 2 * LANES)]
        args = [pf, pf, pf, pf, pf, pb]
    else:
        in_specs = [colspec(PF_AQ), colspec(PF_FF), colspec(PF_FB), colspec(PF_AG), colspec(PB_AV)]
        args = [pf, pf, pf, pf, pb]
    in_specs.append(pl.BlockSpec((1, LANES), lambda s, u: (0, 0)))
    args.append(norm_w.reshape(1, LANES))
    state_spec = pl.BlockSpec((seqs, 2, 1, LANES, LANES), lambda s, u: (s, 0, u, 0, 0))
    if has_s0:
        in_specs.append(state_spec)
        args.append(s0)
    out_specs = [pl.BlockSpec((rows, nh * LANES), lambda s, u: (s, u))]
    out_shape = [jax.ShapeDtypeStruct((nseq * seq_len, units * nh * LANES), BF16)]
    if emit_state:
        out_specs.append(state_spec)
        out_shape.append(jax.ShapeDtypeStruct((nseq, 2, units, LANES, LANES), F32))
    scratch = [
        pltpu.VMEM((rows, 2 * LANES), BF16),
        pltpu.VMEM((rows, nh * LANES), F32),
        pltpu.VMEM((n_all, LANES, 2 * LANES), F32),
        pltpu.VMEM((n_all, 1, 2 * LANES), F32),
        pltpu.VMEM((n_all, LANES, 2 * LANES), BF16),
    ]
    return pl.pallas_call(
        functools.partial(_scan_body, seq_len=seq_len, seqs=seqs, pair=pair, has_s0=has_s0,
                          emit_state=emit_state),
        grid=(nseq // seqs, units),
        in_specs=in_specs,
        out_specs=out_specs,
        out_shape=out_shape,
        scratch_shapes=scratch,
        compiler_params=_params(2),
        name=f"scan_{'p' if prompt else 's'}_{'gla' if pair else 'hgrn'}",
    )(*args)


def _outproj_body(*refs, n_lhs, tm):
    lhs = refs[:2 * n_lhs]
    ws = refs[2 * n_lhs:3 * n_lhs]
    x_ref, m_ref, g_ref, b_ref, o_ref = refs[3 * n_lhs:]

    def compute(group):
        for r in range(0, tm, PROJ_SUB):
            rows = slice(r, r + PROJ_SUB)
            y = _dot(lhs[group][rows, :], ws[0][...])
            for j in range(1, n_lhs):
                y = y + _dot(lhs[2 * j + group][rows, :], ws[j][...])
            z = ALPHA * x_ref[rows, :] + m_ref[...] * y
            o_ref[rows, :] = _layer_norm(z, g_ref[...], b_ref[...])

    in_prompt = pl.program_id(0) < TP // tm
    pl.when(in_prompt)(lambda: compute(0))
    pl.when(jnp.logical_not(in_prompt))(lambda: compute(1))


def _outproj(lhs, ws, x, mod, ln_g, ln_b, layer, tm=OUT_TM):
    n_lhs = len(lhs)
    in_specs, args = [], []
    for a_p, a_s in lhs:
        in_specs += _group_specs(True, tm, a_p.shape[1])
        args += [a_p, a_s]
    in_specs += [_resident(w.shape) for w in ws]
    in_specs += [pl.BlockSpec((tm, D), lambda i: (i, 0))] + _mod_specs(layer, (5,), tm) + _ln_specs(layer, 1)
    return pl.pallas_call(
        functools.partial(_outproj_body, n_lhs=n_lhs, tm=tm),
        grid=(T // tm,),
        in_specs=in_specs,
        out_specs=pl.BlockSpec((tm, D), lambda i: (i, 0)),
        out_shape=jax.ShapeDtypeStruct((T, D), F32),
        compiler_params=_params(1),
        name="mixer_outproj",
    )(*args, *ws, x, mod, ln_g, ln_b)


def _rope_partner(x):
    lane = lax.broadcasted_iota(jnp.int32, x.shape, 1)
    first_half = (lane % (2 * ROPE_FREQS)) < ROPE_FREQS
    return jnp.where(first_half, pltpu.roll(x, LANES - ROPE_FREQS, axis=1), pltpu.roll(x, ROPE_FREQS, axis=1))


def _qkv_body(x_ref, shift_ref, scale_ref, w_ref, wkvt_ref, cos_ref, sin_ref,
              q_ref, k_ref, v_ref, kt_ref, vt_ref):
    qscale = C_HEAD_DIM ** -0.5
    in_prompt = pl.program_id(0) < TP // PROJ_TM

    def modulated(rows):
        return (x_ref[rows, :] * (1.0 + scale_ref[...]) + shift_ref[...]).astype(BF16)

    @pl.when(in_prompt)
    def _():
        for sq in range(PROJ_TM // SEQ):
            rows = slice(sq * SEQ, (sq + 1) * SEQ)
            h = modulated(rows)
            q_ref[rows, :] = (_dot(h, w_ref[:, :C_Q]) * qscale).astype(BF16)
            v_ref[rows, :] = _dot(h, w_ref[:, C_Q + C_KV:]).astype(BF16)
            kt_ref[sq] = _dot_nt(wkvt_ref[:C_KV, :], h)
            vt_ref[sq] = _dot_nt(wkvt_ref[C_KV:, :], h)

    @pl.when(jnp.logical_not(in_prompt))
    def _():
        for sq in range(PROJ_TM // SEQ):
            rows = slice(sq * SEQ, (sq + 1) * SEQ)
            h = modulated(rows)
            cos, sin = cos_ref[rows, :], sin_ref[rows, :]
            zq = _dot(h, w_ref[:, :C_Q])
            zk = _dot(h, w_ref[:, C_Q:C_Q + C_KV])
            v_ref[rows, :] = _dot(h, w_ref[:, C_Q + C_KV:]).astype(BF16)

            def rope(z, cos=cos, sin=sin):
                return z * cos + _rope_partner(z) * sin

            for j in range(C_Q // LANES):
                cols = slice(j * LANES, (j + 1) * LANES)
                q_ref[rows, cols] = (rope(zq[:, cols]) * qscale).astype(BF16)
            for j in range(C_KV // LANES):
                cols = slice(j * LANES, (j + 1) * LANES)
                k_ref[rows, cols] = rope(zk[:, cols]).astype(BF16)


def _qkv(x, mod, w, wkvt, cos, sin, layer):
    tm = PROJ_TM
    n_p = TP // tm
    lat = lambda i: (jnp.maximum(i - n_p, 0), 0)
    ctx = lambda i: (jnp.minimum(i, n_p - 1), 0, 0)
    return pl.pallas_call(
        _qkv_body,
        grid=(T // tm,),
        in_specs=[pl.BlockSpec((tm, D), lambda i: (i, 0))] + _mod_specs(layer, (3, 4), tm) + [
            _resident((D, C_Q + 2 * C_KV)),
            _resident((2 * C_KV, D)),
            pl.BlockSpec((tm, LANES), lat),
            pl.BlockSpec((tm, LANES), lat),
        ],
        out_specs=[
            pl.BlockSpec((tm, C_Q), lambda i: (i, 0)),
            pl.BlockSpec((tm, C_KV), lat),
            pl.BlockSpec((tm, C_KV), lambda i: (i, 0)),
            pl.BlockSpec((tm // SEQ, C_KV, SEQ), ctx),
            pl.BlockSpec((tm // SEQ, C_KV, SEQ), ctx),
        ],
        out_shape=[jax.ShapeDtypeStruct((T, C_Q), BF16), jax.ShapeDtypeStruct((TS, C_KV), BF16),
                   jax.ShapeDtypeStruct((T, C_KV), BF16),
                   jax.ShapeDtypeStruct((BATCH, C_KV, SEQ), F32), jax.ShapeDtypeStruct((BATCH, C_KV, SEQ), F32)],
        compiler_params=_params(1),
        name="c_qkv",
    )(x, mod, mod, w, wkvt, cos, sin)


def _rope_tables():
    t = np.arange(DEC_SEQ)
    pos = np.stack([t // GRID_W, t % GRID_W], axis=1).astype(np.float32)
    inv = (ROPE_BASE ** (-np.arange(ROPE_FREQS, dtype=np.float32) / ROPE_FREQS)).astype(np.float32)
    d = np.arange(C_HEAD_DIM)
    axis = d // (2 * ROPE_FREQS)
    ang = pos[:, axis] * inv[d % ROPE_FREQS][None, :]
    sign = np.where((d % (2 * ROPE_FREQS)) < ROPE_FREQS, -1.0, 1.0)[None, :]
    cos_h, sin_h = np.cos(ang), np.sin(ang) * sign
    reps = LANES // C_HEAD_DIM
    cos = np.tile(np.tile(cos_h, (1, reps)), (DEC_BATCH, 1)).astype(np.float32)
    sin = np.tile(np.tile(sin_h, (1, reps)), (DEC_BATCH, 1)).astype(np.float32)
    return jnp.asarray(cos), jnp.asarray(sin)


def _dup_head(blk, half):
    lane = lax.broadcasted_iota(jnp.int32, blk.shape, 1)
    keep = (lane >= HALF).astype(jnp.int32) == half
    return jnp.where(keep, blk, pltpu.roll(blk, HALF, axis=1))


def _pair_values(v, half):
    lo = lax.broadcasted_iota(jnp.int32, v.shape, 1) < HALF
    v2 = _dup_head(v.astype(F32), half)
    top = jnp.concatenate([jnp.where(lo, v2, 0.0), jnp.where(lo, 1.0, 0.0)], axis=1)
    bot = jnp.concatenate([jnp.where(lo, 0.0, v2), jnp.where(lo, 0.0, 1.0)], axis=1)
    return top.astype(BF16), bot.astype(BF16)


def _softmax_weights(scores, sink):
    mx = sink
    for s in scores:
        mx = jnp.maximum(mx, jnp.max(s, axis=-1, keepdims=True))
    return [jnp.exp(s - mx).astype(BF16) for s in scores], jnp.exp(sink - mx)


def _pair_sinks(sink_ref, jb):
    return (sink_ref[:, jb * LANES:jb * LANES + 1], sink_ref[:, jb * LANES + HALF:jb * LANES + HALF + 1])


KV_PER_BLOCK = LANES // C_HEAD_DIM
TILE_KV = C_KV_HEADS
TILE_KVW = TILE_KV * C_HEAD_DIM
CTX_TILES = BATCH * C_KV_HEADS // TILE_KV
CTX_BLOCKS = TILE_KV * C_GROUPS // 2
PIPE_LAG = 2


def _kv_block(ref, kv):
    lb = kv // KV_PER_BLOCK
    return ref[:, lb * LANES:(lb + 1) * LANES], kv % KV_PER_BLOCK


def _run_pipeline(g, n_tiles, stages):
    for cur in (0, 1):
        steady = jnp.logical_and(g % 2 == cur, jnp.logical_and(g >= PIPE_LAG, g < n_tiles))
        pl.when(steady)(functools.partial(stages, cur, 1 - cur, True, True, True))
    pl.when(g == 0)(functools.partial(stages, 0, 1, True, False, False))
    pl.when(g == 1)(functools.partial(stages, 1, 0, True, True, False))
    pl.when(g == n_tiles)(functools.partial(stages, n_tiles % 2, 1 - n_tiles % 2, False, True, True))
    pl.when(g == n_tiles + 1)(functools.partial(stages, 1 - n_tiles % 2, n_tiles % 2, False, False, True))


def _ctx_tile(g, lag):
    t = jnp.clip(g - lag, 0, CTX_TILES - 1)
    n = C_KV_HEADS // TILE_KV
    return t // n, t % n


def _ctx_attn_body(q_ref, kt_ref, v_ref, sink_ref, o_ref, s_s, e_s, st_s):
    g = pl.program_id(0)
    lo = lax.broadcasted_iota(jnp.int32, (1, LANES), 1) < HALF
    pairs_per_kv = C_GROUPS // 2

    def stages(cur, prev, scores, weights, values):
        for kv in range(TILE_KV if values else 0):
            v_top, v_bot = _pair_values(*_kv_block(v_ref, kv))
            for jb in range(kv * pairs_per_kv, (kv + 1) * pairs_per_kv):
                od = _dot(e_s[cur, 2 * jb], v_top) + _dot(e_s[cur, 2 * jb + 1], v_bot)
                o_ref[:, jb * LANES:(jb + 1) * LANES] = (od[:, :LANES] / (od[:, LANES:] + st_s[cur, jb])).astype(BF16)
        for jb in range(CTX_BLOCKS if weights else 0):
            terms = []
            for hh, sink in enumerate(_pair_sinks(sink_ref, jb)):
                (e,), term = _softmax_weights([s_s[prev, 2 * jb + hh]], sink)
                e_s[prev, 2 * jb + hh] = e
                terms.append(term)
            st_s[prev, jb] = jnp.where(lo, terms[0], terms[1])
        for kv in range(TILE_KV if scores else 0):
            kt = kt_ref[kv * C_HEAD_DIM:(kv + 1) * C_HEAD_DIM, :]
            zero = jnp.zeros_like(kt)
            kt_a = jnp.concatenate([kt, zero], axis=0).astype(BF16)
            kt_b = jnp.concatenate([zero, kt], axis=0).astype(BF16)
            for jb in range(kv * pairs_per_kv, (kv + 1) * pairs_per_kv):
                q = q_ref[:, jb * LANES:(jb + 1) * LANES]
                s_s[cur, 2 * jb] = _dot(q, kt_a)
                s_s[cur, 2 * jb + 1] = _dot(q, kt_b)

    _run_pipeline(g, CTX_TILES, stages)


def _ctx_attn(q, kt, v, sink_cols):
    gw = CTX_BLOCKS * LANES

    def at(lag, fn):
        return lambda g: fn(*_ctx_tile(g, lag))

    return pl.pallas_call(
        _ctx_attn_body,
        grid=(CTX_TILES + PIPE_LAG,),
        in_specs=[
            pl.BlockSpec((SEQ, gw), at(0, lambda b, p: (b, p))),
            pl.BlockSpec((None, TILE_KVW, SEQ), at(0, lambda b, p: (b, p, 0))),
            pl.BlockSpec((SEQ, TILE_KVW), at(2, lambda b, p: (b, p))),
            pl.BlockSpec((1, gw), at(1, lambda b, p: (0, p))),
        ],
        out_specs=pl.BlockSpec((SEQ, gw), at(2, lambda b, p: (b, p))),
        out_shape=jax.ShapeDtypeStruct((TP, C_Q), BF16),
        scratch_shapes=[
            pltpu.VMEM((2, 2 * CTX_BLOCKS, SEQ, SEQ), F32),
            pltpu.VMEM((2, 2 * CTX_BLOCKS, SEQ, SEQ), BF16),
            pltpu.VMEM((2, CTX_BLOCKS, SEQ, LANES), F32),
        ],
        compiler_params=_params(1),
        name="c_attn_ctx",
    )(q, kt, v, sink_cols)


LAT_TQ = 128
LAT_WIN = LAT_TQ + 2 * WINDOW
LAT_NQ = DEC_SEQ // LAT_TQ
LAT_KVB = C_KV_HEADS // TILE_KV
LAT_TILES = DEC_BATCH * LAT_KVB * LAT_NQ


def _lat_tile(g, lag):
    t = jnp.clip(g - lag, 0, LAT_TILES - 1)
    bp = t // LAT_NQ
    return bp // LAT_KVB, bp % LAT_KVB, t % LAT_NQ


def _lat_window(j):
    return pl.multiple_of(jnp.clip(j * LAT_TQ - WINDOW, 0, DEC_SEQ - LAT_WIN), LANES)


def _lat_attn_body(q_ref, k_ref, v_ref, ck_ref, cv_ref, sink_ref, o_ref,
                   ka_s, kb_s, cka_s, ckb_s, va_s, vb_s, cva_s, cvb_s, sl_s, sc_s, el_s, ec_s, st_s):
    g = pl.program_id(0)
    lo = lax.broadcasted_iota(jnp.int32, (1, LANES), 1) < HALF
    pairs_per_kv = C_GROUPS // 2
    b_a, p_a, j_a = _lat_tile(g, 0)
    b_c, p_c, j_c = _lat_tile(g, 2)
    v_slot_a = (b_a * LAT_KVB + p_a) % 2
    v_slot_c = (b_c * LAT_KVB + p_c) % 2

    @pl.when(jnp.logical_and(j_a == 0, g < LAT_TILES))
    def _():
        for kv in range(TILE_KV):
            for src, a_s, b_s in ((k_ref, ka_s, kb_s), (ck_ref, cka_s, ckb_s)):
                blk, kh = _kv_block(src, kv)
                lo2 = lax.broadcasted_iota(jnp.int32, blk.shape, 1) < HALF
                k2 = _dup_head(blk.astype(F32), kh)
                a_s[kv] = jnp.where(lo2, k2, 0.0).astype(BF16)
                b_s[kv] = jnp.where(lo2, 0.0, k2).astype(BF16)
            va_s[v_slot_a, kv], vb_s[v_slot_a, kv] = _pair_values(*_kv_block(v_ref, kv))
            cva_s[v_slot_a, kv], cvb_s[v_slot_a, kv] = _pair_values(*_kv_block(cv_ref, kv))

    def stages(cur, prev, scores, weights, values):
        win_c = pl.ds(_lat_window(j_c), LAT_WIN)
        for kv in range(TILE_KV if values else 0):
            v_loc = (va_s[v_slot_c, kv, win_c, :], vb_s[v_slot_c, kv, win_c, :])
            v_ctx = (cva_s[v_slot_c, kv], cvb_s[v_slot_c, kv])
            for jb in range(kv * pairs_per_kv, (kv + 1) * pairs_per_kv):
                od = None
                for hh in range(2):
                    part = _dot(el_s[cur, 2 * jb + hh], v_loc[hh]) + _dot(ec_s[cur, 2 * jb + hh], v_ctx[hh])
                    od = part if od is None else od + part
                o_ref[:, jb * LANES:(jb + 1) * LANES] = (od[:, :LANES] / (od[:, LANES:] + st_s[cur, jb])).astype(BF16)
        for jb in range(CTX_BLOCKS if weights else 0):
            terms = []
            for hh, sink in enumerate(_pair_sinks(sink_ref, jb)):
                (e_loc, e_ctx), term = _softmax_weights([sl_s[prev, 2 * jb + hh], sc_s[prev, 2 * jb + hh]], sink)
                el_s[prev, 2 * jb + hh] = e_loc
                ec_s[prev, 2 * jb + hh] = e_ctx
                terms.append(term)
            st_s[prev, jb] = jnp.where(lo, terms[0], terms[1])
        start = _lat_window(j_a)
        win = pl.ds(start, LAT_WIN)
        qpos = j_a * LAT_TQ + lax.broadcasted_iota(jnp.int32, (LAT_TQ, LAT_WIN), 0)
        kpos = start + lax.broadcasted_iota(jnp.int32, (LAT_TQ, LAT_WIN), 1)
        band = jnp.abs(qpos - kpos) <= WINDOW
        for kv in range(TILE_KV if scores else 0):
            for jb in range(kv * pairs_per_kv, (kv + 1) * pairs_per_kv):
                q = q_ref[:, jb * LANES:(jb + 1) * LANES]
                for hh, (kl, kc) in enumerate(((ka_s, cka_s), (kb_s, ckb_s))):
                    sl_s[cur, 2 * jb + hh] = jnp.where(band, _dot_nt(q, kl[kv, win, :]), -jnp.inf)
                    sc_s[cur, 2 * jb + hh] = _dot_nt(q, kc[kv])

    _run_pipeline(g, LAT_TILES, stages)


def _lat_attn(q, k, v, ck, cv, sink_cols):
    gw = CTX_BLOCKS * LANES
    q_off = TP // LAT_TQ
    kv_off = TP // DEC_SEQ
    kvb, n_heads = TILE_KV, 2 * CTX_BLOCKS

    def at(lag, fn):
        return lambda g: fn(*_lat_tile(g, lag))

    return pl.pallas_call(
        _lat_attn_body,
        grid=(LAT_TILES + PIPE_LAG,),
        in_specs=[
            pl.BlockSpec((LAT_TQ, gw), at(0, lambda b, p, j: (q_off + b * LAT_NQ + j, p))),
            pl.BlockSpec((DEC_SEQ, TILE_KVW), at(0, lambda b, p, j: (b, p))),
            pl.BlockSpec((DEC_SEQ, TILE_KVW), at(0, lambda b, p, j: (kv_off + b, p))),
            pl.BlockSpec((PAST_LEN, TILE_KVW), at(0, lambda b, p, j: (b, p))),
            pl.BlockSpec((PAST_LEN, TILE_KVW), at(0, lambda b, p, j: (b, p))),
            pl.BlockSpec((1, gw), at(1, lambda b, p, j: (0, p))),
        ],
        out_specs=pl.BlockSpec((LAT_TQ, gw), at(2, lambda b, p, j: (b * LAT_NQ + j, p))),
        out_shape=jax.ShapeDtypeStruct((TS, C_Q), BF16),
        scratch_shapes=[
            pltpu.VMEM((kvb, DEC_SEQ, LANES), BF16), pltpu.VMEM((kvb, DEC_SEQ, LANES), BF16),
            pltpu.VMEM((kvb, PAST_LEN, LANES), BF16), pltpu.VMEM((kvb, PAST_LEN, LANES), BF16),
            pltpu.VMEM((2, kvb, DEC_SEQ, 2 * LANES), BF16), pltpu.VMEM((2, kvb, DEC_SEQ, 2 * LANES), BF16),
            pltpu.VMEM((2, kvb, PAST_LEN, 2 * LANES), BF16), pltpu.VMEM((2, kvb, PAST_LEN, 2 * LANES), BF16),
            pltpu.VMEM((2, n_heads, LAT_TQ, LAT_WIN), F32), pltpu.VMEM((2, n_heads, LAT_TQ, PAST_LEN), F32),
            pltpu.VMEM((2, n_heads, LAT_TQ, LAT_WIN), BF16), pltpu.VMEM((2, n_heads, LAT_TQ, PAST_LEN), BF16),
            pltpu.VMEM((2, CTX_BLOCKS, LAT_TQ, LANES), F32),
        ],
        compiler_params=_params(1),
        name="c_attn_latent",
    )(q, k, v, ck, cv, sink_cols)


def kernel(x_prompt, x_sample, state_hgrn, state_gla, cache_k, cache_v, c, c_ctx, w_mod, b_mod, ln_g, ln_b,
           ffn_w1, ffn_w3, ffn_w2, w_in_ab, hgrn_lb, gla_gate_up, gla_gate_b, norm_a, norm_b, w_out_ab,
           w_qkv_c, sink_c, w_out_c):
    cs = jnp.zeros((8, D), F32).at[0].set(c_ctx).at[1:1 + DEC_BATCH].set(c)
    mod = _mod_vectors(cs, w_mod, b_mod).reshape(DEPTH, 8, 1, N_MOD * D)
    ffn_ws = (ffn_w1, ffn_w3, ffn_w2)
    ln_g, ln_b = ln_g.reshape(DEPTH, 3, 1, D), ln_b.reshape(DEPTH, 3, 1, D)

    def ffn(xs, ws, layer, sub, split_out=False, cast_next=None):
        casts = None if cast_next is None else (ffn_ws, cast_next)
        return _ffn_sublayer(xs, mod, *ws, ln_g, ln_b, layer, sub, split_out=split_out, casts=casts)

    x, *ws_01 = ffn([x_prompt.reshape(TP, D), x_sample.reshape(TS, D)], [w[0, 0].astype(BF16) for w in ffn_ws],
                    0, 0, cast_next=(0, 1))
    w_in = w_in_ab[0]
    o_aq, o_ai, o_ff, o_fb, o_ag = 0, A_W, 2 * A_W, 3 * A_W, 4 * A_W
    o_bq = 5 * A_W
    o_bk, o_bv = o_bq + B_QK, o_bq + 2 * B_QK
    o_bg = o_bv + B_V
    o_z = o_bg + B_V
    order = [(o_aq, A_W), (o_ff, A_W), (o_fb, A_W), (o_ag, A_W), (o_bq, B_QK), (o_bk, B_QK), (o_bg, B_V),
             (o_ai, A_W), (o_bv, B_V)]
    wmain = jnp.concatenate([w_in[:, o:o + w] for o, w in order], axis=1).astype(BF16)
    wz = jnp.pad(w_in[:, o_z:o_z + 2 * GATE_RANK], ((0, 0), (0, LANES - 2 * GATE_RANK))).astype(BF16)
    gup = jnp.zeros((LANES, 2 * B_QK), F32)
    gup = gup.at[:GATE_RANK, :B_QK].set(gla_gate_up[0, 0]).at[GATE_RANK:2 * GATE_RANK, B_QK:].set(gla_gate_up[0, 1])
    gb = gla_gate_b[0].reshape(1, 2 * B_QK)
    pf, pb = _inproj(x, mod, wmain, wz, gup.astype(BF16), gb, hgrn_lb, 0, 0)

    s0_a = state_hgrn[:, 0]
    s0_b = state_gla[:, 0].reshape(DEC_BATCH, 2, B_HEADS // 2, LANES, B_DV)
    oa_p, st_a = _scan(pf, pb, norm_a[0], None, prompt=True, pair=False)
    ob_p, st_b = _scan(pf, pb, norm_b[0], None, prompt=True, pair=True)
    (oa_s,) = _scan(pf, pb, norm_a[0], s0_a, prompt=False, pair=False)
    (ob_s,) = _scan(pf, pb, norm_b[0], s0_b, prompt=False, pair=True)
    w_out = w_out_ab[0].astype(BF16)
    x = _outproj([(oa_p, oa_s), (ob_p, ob_s)], [w_out[:A_W], w_out[A_W:]], x, mod, ln_g, ln_b, 0)
    x, *ws_10 = ffn([x], ws_01, 0, 1, cast_next=(1, 0))
    new_hgrn = st_a.reshape(BATCH, 1, 2, A_HEADS, A_DK, A_DV)
    new_gla = st_b.reshape(BATCH, 1, 2, B_HEADS, B_DK, B_DV)

    x, *ws_11 = ffn([x], ws_10, 1, 0, cast_next=(1, 1))
    cos, sin = _rope_tables()
    w_qkv = w_qkv_c[0].astype(BF16)
    q, k, v, kt, vt = _qkv(x, mod, w_qkv, w_qkv[:, C_Q:].T, cos, sin, 1)
    sink_cols = jnp.repeat(sink_c[0], C_HEAD_DIM).reshape(1, C_Q)
    o_p = _ctx_attn(q, kt, v, sink_cols)
    ck = cache_k[:, 0].reshape(DEC_BATCH * PAST_LEN, C_KV)
    cv = cache_v[:, 0].reshape(DEC_BATCH * PAST_LEN, C_KV)
    o_s = _lat_attn(q, k, v, ck, cv, sink_cols)
    x = _outproj([(o_p, o_s)], [w_out_c[0].astype(BF16)], x, mod, ln_g, ln_b, 1)
    y_p, y_s = ffn([x], ws_11, 1, 1, split_out=True)

    def cache_layout(zt):
        return zt.reshape(BATCH, 1, C_KV_HEADS, C_HEAD_DIM, SEQ).transpose(0, 1, 4, 2, 3)

    new_k, new_v = cache_layout(kt), cache_layout(vt)

    return (y_p.reshape(BATCH, SEQ, D), y_s.reshape(DEC_BATCH, DEC_SEQ, D), new_hgrn, new_gla, new_k, new_v)
```
